```python
import jax, jax.numpy as jnp
from jax import lax
import numpy as np

D_MODEL = 1024
BATCH = 8
SEQ = 16384
DEPTH = 1

D_MIX = D_MODEL
DN_HEADS = 4
DN_HEAD_DIM = D_MODEL // 8
DN_WIDTH = DN_HEADS * DN_HEAD_DIM
CONV_K = 4
CHUNK = 64
AT_HEADS = 8
AT_HEAD_DIM = D_MODEL // 16
AT_WIDTH = AT_HEADS * AT_HEAD_DIM
PATTERNS = ((128, 1), (512, 4), (2048, 16))
Q_BLOCK = 128
ROPE_THETA = 10000.0
EPS = 1e-6
SPLIT_SIZES = (3 * DN_WIDTH, DN_WIDTH, DN_HEADS, DN_HEADS, AT_WIDTH, AT_WIDTH, AT_WIDTH, AT_WIDTH)
IN_COLS = 4 * DN_WIDTH + 2 * DN_HEADS + 4 * AT_WIDTH

kernel_name = "hybrid_deltanet_dilated_swa_adaln"


def rmsnorm(x, w):
    xf = x.astype(jnp.float32)
    xf = xf * lax.rsqrt(jnp.mean(xf * xf, axis=-1, keepdims=True) + EPS)
    return xf.astype(x.dtype) * w


def l2norm(x):
    xf = x.astype(jnp.float32)
    return xf * lax.rsqrt(jnp.sum(xf * xf, axis=-1, keepdims=True) + EPS)


def rope(x, positions):
    hd = x.shape[-1]
    half = hd // 2
    inv_freq = ROPE_THETA ** (-jnp.arange(half, dtype=jnp.float32) / half)
    ang = positions.astype(jnp.float32)[..., None] * inv_freq
    cos = jnp.cos(ang)[:, :, None, :]
    sin = jnp.sin(ang)[:, :, None, :]
    x1, x2 = x[..., :half], x[..., half:]
    out = jnp.concatenate([x1 * cos - x2 * sin, x2 * cos + x1 * sin], axis=-1)
    return out.astype(x.dtype)


def causal_short_conv(x, w):
    K = w.shape[0]
    S = x.shape[1]
    xp = jnp.pad(x, ((0, 0), (K - 1, 0), (0, 0)))
    out = xp[:, 0:S] * w[0]
    for j in range(1, K):
        out = out + xp[:, j:j + S] * w[j]
    return out


def gated_delta_rule(q, k, v, g, beta):
    B, S, H, dk = q.shape
    dv = v.shape[-1]
    nc = S // CHUNK

    def chunks4(t):
        return t.reshape(B, nc, CHUNK, H, t.shape[-1]).transpose(0, 3, 1, 2, 4)

    def chunks3(t):
        return t.reshape(B, nc, CHUNK, H).transpose(0, 3, 1, 2)

    q = chunks4(q) * (dk ** -0.5)
    k = chunks4(k)
    v = chunks4(v)
    beta = chunks3(beta)
    gc = jnp.cumsum(chunks3(g), axis=-1)

    tril = jnp.tril(jnp.ones((CHUNK, CHUNK), dtype=bool))
    strict = tril & ~jnp.eye(CHUNK, dtype=bool)
    decay_mat = jnp.exp(jnp.where(tril, gc[..., :, None] - gc[..., None, :], -jnp.inf))

    kb = k * beta[..., None]
    vb = v * beta[..., None]
    a_low = jnp.where(strict, jnp.einsum('bhncd,bhnsd->bhncs', kb, k) * decay_mat, 0.0)
    ia = a_low + jnp.eye(CHUNK, dtype=jnp.float32)
    u = lax.linalg.triangular_solve(ia, vb, left_side=True, lower=True, unit_diagonal=True)
    w = lax.linalg.triangular_solve(ia, kb * jnp.exp(gc)[..., None],
                                    left_side=True, lower=True, unit_diagonal=True)
    attn_intra = jnp.where(tril, jnp.einsum('bhncd,bhnsd->bhncs', q, k) * decay_mat, 0.0)
    q_dec = q * jnp.exp(gc)[..., None]
    k_dec = k * jnp.exp(gc[..., -1:] - gc)[..., None]
    g_last = jnp.exp(gc[..., -1])

    xs = (jnp.moveaxis(u, 2, 0), jnp.moveaxis(w, 2, 0), jnp.moveaxis(q_dec, 2, 0),
          jnp.moveaxis(k_dec, 2, 0), jnp.moveaxis(attn_intra, 2, 0), jnp.moveaxis(g_last, 2, 0))

    def step(state, inp):
        u_n, w_n, qd_n, kd_n, at_n, gl_n = inp
        v_new = u_n - jnp.einsum('bhcd,bhde->bhce', w_n, state)
        o = jnp.einsum('bhcd,bhde->bhce', qd_n, state) + jnp.einsum('bhcs,bhse->bhce', at_n, v_new)
        state = state * gl_n[..., None, None] + jnp.einsum('bhcd,bhce->bhde', kd_n, v_new)
        return state, o

    s0 = jnp.zeros((B, H, dk, dv), dtype=jnp.float32)
    _, o = lax.scan(step, s0, xs)
    return o.transpose(1, 0, 3, 2, 4).reshape(B, S, H, dv)


def strided_window_attention(q, k, v, dilation, w_sub):
    B, S, H, hd = q.shape
    L = S // dilation
    nb = -(-L // Q_BLOCK)
    Lp = nb * Q_BLOCK

    def split(t):
        t = t.reshape(B, L, dilation, H, hd).transpose(0, 2, 1, 3, 4)
        return jnp.pad(t, ((0, 0), (0, 0), (0, Lp - L), (0, 0), (0, 0)))

    def band(t):
        tp = jnp.pad(t, ((0, 0), (0, 0), (Q_BLOCK, 0), (0, 0), (0, 0)))
        tp = tp.reshape(B, dilation, nb + 1, Q_BLOCK, H, hd)
        return jnp.concatenate([tp[:, :, :-1], tp[:, :, 1:]], axis=3)

    qb = split(q).reshape(B, dilation, nb, Q_BLOCK, H, hd)
    kb = band(split(k))
    vb = band(split(v))

    s = jnp.einsum('bdnqhe,bdnkhe->bdnhqk', qb, kb).astype(jnp.float32) * (hd ** -0.5)
    qi = jnp.arange(Q_BLOCK)[:, None]
    kj = jnp.arange(2 * Q_BLOCK)[None, :]
    rel = Q_BLOCK + qi - kj
    kidx = jnp.arange(nb)[:, None] * Q_BLOCK - Q_BLOCK + jnp.arange(2 * Q_BLOCK)[None, :]
    mask = ((rel >= 0) & (rel <= w_sub))[None] & (kidx >= 0)[:, None, :]
    s = jnp.where(mask[None, None, :, None], s, -jnp.inf)
    m = jnp.max(s, axis=-1, keepdims=True)
    p = jnp.exp(s - m)
    l = jnp.sum(p, axis=-1, keepdims=True)
    o = jnp.einsum('bdnhqk,bdnkhe->bdnqhe', p / l, vb.astype(jnp.float32))
    lse = (m + jnp.log(l))[..., 0]

    o = o.reshape(B, dilation, Lp, H, hd)[:, :, :L].transpose(0, 2, 1, 3, 4).reshape(B, S, H, hd)
    lse = lse.transpose(0, 1, 2, 4, 3).reshape(B, dilation, Lp, H)[:, :, :L]
    lse = lse.transpose(0, 2, 1, 3).reshape(B, S, H)
    return o, lse


def dilated_attention(q, k, v):
    outs, lses = [], []
    for window, dilation in PATTERNS:
        o, lse = strided_window_attention(q, k, v, dilation, window // dilation)
        outs.append(o)
        lses.append(lse)
    wts = jax.nn.softmax(jnp.stack(lses, axis=0), axis=0)
    return jnp.sum(wts[..., None] * jnp.stack(outs, axis=0), axis=0)


def _fwd_setup_inputs(seed: int = 0) -> dict:
    key = jax.random.key(seed)
    ks = jax.random.split(key, 16)
    f32 = jnp.float32
    x = jax.random.normal(ks[0], (BATCH, SEQ, D_MODEL), f32)
    c = jax.random.normal(ks[1], (BATCH, D_MODEL), f32)
    positions = jnp.broadcast_to(jnp.arange(SEQ, dtype=jnp.int32)[None], (BATCH, SEQ))
    w_mod = jax.random.normal(ks[2], (DEPTH, D_MODEL, 3 * D_MODEL), f32) * (0.2 * D_MODEL ** -0.5)
    b_mod = jax.random.normal(ks[3], (DEPTH, 3 * D_MODEL), f32) * 0.01
    norm_w = 1.0 + 0.01 * jax.random.normal(ks[4], (DEPTH, D_MODEL), f32)
    w_in = jax.random.normal(ks[5], (DEPTH, D_MODEL, IN_COLS), f32) * (D_MODEL ** -0.5)
    conv_w = jax.random.normal(ks[6], (DEPTH, CONV_K, 3 * DN_WIDTH), f32) * (CONV_K ** -0.5)
    a_log = jnp.log(jax.random.uniform(ks[7], (DEPTH, DN_HEADS), f32, 1.0, 16.0))
    dt = jnp.exp(jax.random.uniform(ks[8], (DEPTH, DN_HEADS), f32, jnp.log(1e-3), jnp.log(1e-1)))
    dt_bias = dt + jnp.log(-jnp.expm1(-dt))
    dn_norm_w = 1.0 + 0.01 * jax.random.normal(ks[9], (DEPTH, DN_HEAD_DIM), f32)
    at_norm_w = 1.0 + 0.01 * jax.random.normal(ks[10], (DEPTH, AT_HEAD_DIM), f32)
    w_out = jax.random.normal(ks[11], (DEPTH, D_MIX, D_MODEL), f32) * (D_MIX ** -0.5)
    final_norm_w = 1.0 + 0.01 * jax.random.normal(ks[12], (D_MODEL,), f32)
    return {"x": x, "c": c, "positions": positions, "w_mod": w_mod, "b_mod": b_mod,
            "norm_w": norm_w, "w_in": w_in, "conv_w": conv_w, "a_log": a_log,
            "dt_bias": dt_bias, "dn_norm_w": dn_norm_w, "at_norm_w": at_norm_w,
            "w_out": w_out, "final_norm_w": final_norm_w}


def _fwd_reference(x, c, positions, w_mod, b_mod, norm_w, w_in, conv_w, a_log, dt_bias,
              dn_norm_w, at_norm_w, w_out, final_norm_w):
    B, S, _ = x.shape
    offsets = tuple(int(o) for o in np.cumsum(SPLIT_SIZES)[:-1])
    for layer in range(DEPTH):
        mod = jax.nn.silu(c) @ w_mod[layer] + b_mod[layer]
        shift, scale, gate = jnp.split(mod, 3, axis=-1)
        h = rmsnorm(x, norm_w[layer]) * (1.0 + scale[:, None]) + shift[:, None]

        proj = h @ w_in[layer]
        dn_qkv, dn_z, dn_b, dn_a, at_q, at_k, at_v, at_z = jnp.split(proj, offsets, axis=-1)

        dn_qkv = jax.nn.silu(causal_short_conv(dn_qkv, conv_w[layer]))
        dq, dk_, dvv = jnp.split(dn_qkv, 3, axis=-1)
        dq = l2norm(dq.reshape(B, S, DN_HEADS, DN_HEAD_DIM))
        dk_ = l2norm(dk_.reshape(B, S, DN_HEADS, DN_HEAD_DIM))
        dvv = dvv.reshape(B, S, DN_HEADS, DN_HEAD_DIM).astype(jnp.float32)
        beta = jax.nn.sigmoid(dn_b.astype(jnp.float32))
        g = -jnp.exp(a_log[layer].astype(jnp.float32)) * jax.nn.softplus(
            (dn_a + dt_bias[layer]).astype(jnp.float32))
        o_dn = gated_delta_rule(dq, dk_, dvv, g, beta).astype(x.dtype)
        o_dn = rmsnorm(o_dn, dn_norm_w[layer]) * jax.nn.silu(dn_z.reshape(B, S, DN_HEADS, DN_HEAD_DIM))
        o_dn = o_dn.reshape(B, S, DN_WIDTH)

        aq = rope(at_q.reshape(B, S, AT_HEADS, AT_HEAD_DIM), positions)
        ak = rope(at_k.reshape(B, S, AT_HEADS, AT_HEAD_DIM), positions)
        av = at_v.reshape(B, S, AT_HEADS, AT_HEAD_DIM)
        o_at = dilated_attention(aq, ak, av).astype(x.dtype)
        o_at = rmsnorm(o_at, at_norm_w[layer]) * jax.nn.silu(at_z.reshape(B, S, AT_HEADS, AT_HEAD_DIM))
        o_at = o_at.reshape(B, S, AT_WIDTH)

        mix = jnp.concatenate([o_dn, o_at], axis=-1) @ w_out[layer]
        x = x + gate[:, None] * mix
    return rmsnorm(x, final_norm_w)


import jax as _jax
import jax.numpy as _jnp

TWIN_FORMAT = 'train_step'
FWD_PARAMS = ['x', 'c', 'positions', 'w_mod', 'b_mod', 'norm_w', 'w_in', 'conv_w', 'a_log', 'dt_bias', 'dn_norm_w', 'at_norm_w', 'w_out', 'final_norm_w']
TWIN_WEIGHTS = ['w_mod', 'b_mod', 'norm_w', 'w_in', 'conv_w', 'a_log', 'dt_bias', 'dn_norm_w', 'at_norm_w', 'w_out', 'final_norm_w']
TWIN_DIFF_INPUT = 'x'
TWIN_INPUTS = ['x', 'c', 'positions', 'w_mod', 'b_mod', 'norm_w', 'w_in', 'conv_w', 'a_log', 'dt_bias', 'dn_norm_w', 'at_norm_w', 'w_out', 'final_norm_w', 'loss_target', 'm_w_mod', 'm_b_mod', 'm_norm_w', 'm_w_in', 'm_conv_w', 'm_a_log', 'm_dt_bias', 'm_dn_norm_w', 'm_at_norm_w', 'm_w_out', 'm_final_norm_w', 'v_w_mod', 'v_b_mod', 'v_norm_w', 'v_w_in', 'v_conv_w', 'v_a_log', 'v_dt_bias', 'v_dn_norm_w', 'v_at_norm_w', 'v_w_out', 'v_final_norm_w']
TWIN_OUTPUTS = ['loss', 'grad_x', 'grad_w_mod', 'grad_b_mod', 'grad_norm_w', 'grad_w_in', 'grad_conv_w', 'grad_a_log', 'grad_dt_bias', 'grad_dn_norm_w', 'grad_at_norm_w', 'grad_w_out', 'grad_final_norm_w', 'delta_w_mod', 'delta_b_mod', 'delta_norm_w', 'delta_w_in', 'delta_conv_w', 'delta_a_log', 'delta_dt_bias', 'delta_dn_norm_w', 'delta_at_norm_w', 'delta_w_out', 'delta_final_norm_w', 'new_m_w_mod', 'new_m_b_mod', 'new_m_norm_w', 'new_m_w_in', 'new_m_conv_w', 'new_m_a_log', 'new_m_dt_bias', 'new_m_dn_norm_w', 'new_m_at_norm_w', 'new_m_w_out', 'new_m_final_norm_w', 'new_v_w_mod', 'new_v_b_mod', 'new_v_norm_w', 'new_v_w_in', 'new_v_conv_w', 'new_v_a_log', 'new_v_dt_bias', 'new_v_dn_norm_w', 'new_v_at_norm_w', 'new_v_w_out', 'new_v_final_norm_w']
TWIN_LEAF_KINDS = {'loss': 'loss', 'grad_x': 'grad_x', 'grad_w_mod': 'grad_w', 'grad_b_mod': 'grad_w', 'grad_norm_w': 'grad_w', 'grad_w_in': 'grad_w', 'grad_conv_w': 'grad_w', 'grad_a_log': 'grad_w', 'grad_dt_bias': 'grad_w', 'grad_dn_norm_w': 'grad_w', 'grad_at_norm_w': 'grad_w', 'grad_w_out': 'grad_w', 'grad_final_norm_w': 'grad_w', 'delta_w_mod': 'delta_w', 'delta_b_mod': 'delta_w', 'delta_norm_w': 'delta_w', 'delta_w_in': 'delta_w', 'delta_conv_w': 'delta_w', 'delta_a_log': 'delta_w', 'delta_dt_bias': 'delta_w', 'delta_dn_norm_w': 'delta_w', 'delta_at_norm_w': 'delta_w', 'delta_w_out': 'delta_w', 'delta_final_norm_w': 'delta_w', 'new_m_w_mod': 'new_m', 'new_m_b_mod': 'new_m', 'new_m_norm_w': 'new_m', 'new_m_w_in': 'new_m', 'new_m_conv_w': 'new_m', 'new_m_a_log': 'new_m', 'new_m_dt_bias': 'new_m', 'new_m_dn_norm_w': 'new_m', 'new_m_at_norm_w': 'new_m', 'new_m_w_out': 'new_m', 'new_m_final_norm_w': 'new_m', 'new_v_w_mod': 'new_v', 'new_v_b_mod': 'new_v', 'new_v_norm_w': 'new_v', 'new_v_w_in': 'new_v', 'new_v_conv_w': 'new_v', 'new_v_a_log': 'new_v', 'new_v_dt_bias': 'new_v', 'new_v_dn_norm_w': 'new_v', 'new_v_at_norm_w': 'new_v', 'new_v_w_out': 'new_v', 'new_v_final_norm_w': 'new_v'}


def _forward(args):
    return _fwd_reference(*[args[k] for k in FWD_PARAMS])


def _output_shape():
    def fwd():
        inp = _fwd_setup_inputs(0)
        return _fwd_reference(*[inp[k] for k in FWD_PARAMS])
    out = _jax.eval_shape(fwd)
    return out.shape, out.dtype

N_MICROBATCH = 1
ADAM_LR = 0.001
ADAM_B1 = 0.9
ADAM_B2 = 0.999
ADAM_EPS = 1e-08
ADAM_WD = 0.01
ADAM_STEP = 10
PER_EXAMPLE_BATCH_AXIS = {'x': 0, 'c': 0, 'positions': 0, 'loss_target': 0}
SHARED_INPUTS = []
_WEIGHT_DTYPES = {'w_mod': _jnp.float32, 'b_mod': _jnp.float32, 'norm_w': _jnp.float32, 'w_in': _jnp.float32, 'conv_w': _jnp.float32, 'a_log': _jnp.float32, 'dt_bias': _jnp.float32, 'dn_norm_w': _jnp.float32, 'at_norm_w': _jnp.float32, 'w_out': _jnp.float32, 'final_norm_w': _jnp.float32}
MOMENT_SCALE = {'w_mod': 9.211327e-02, 'b_mod': 1.612253e-01, 'norm_w': 4.356304e-02, 'w_in': 2.259856e-02, 'conv_w': 2.118809e-02, 'a_log': 2.856348e-01, 'dt_bias': 2.630411e-01, 'dn_norm_w': 4.986447e-02, 'at_norm_w': 7.525375e-02, 'w_out': 2.590867e-02, 'final_norm_w': 1.280024e+02}


def _to_microbatches(a, axis):
    t = _jnp.moveaxis(a, axis, 0)
    t = t.reshape((N_MICROBATCH, t.shape[0] // N_MICROBATCH) + t.shape[1:])
    return _jnp.moveaxis(t, 1, axis + 1)


def setup_inputs(seed: int = 0) -> dict:
    inp = _fwd_setup_inputs(seed)
    key = _jax.random.fold_in(_jax.random.key(seed), 7919)
    shape, _ = _output_shape()
    out = dict(inp)
    out["loss_target"] = _jax.random.normal(_jax.random.fold_in(key, 0), shape, _jnp.float32)
    for i, name in enumerate(TWIN_WEIGHTS):
        w = inp[name].astype(_jnp.float32)
        if MOMENT_SCALE is None:
            s = _jnp.sqrt(_jnp.mean(_jnp.square(w)) + 1e-30)
        else:
            s = MOMENT_SCALE[name]
        km, kv = _jax.random.split(_jax.random.fold_in(key, i + 1))
        out[name] = w
        out["m_" + name] = s * _jax.random.normal(km, w.shape, _jnp.float32)
        out["v_" + name] = (s * s) * _jax.random.uniform(kv, w.shape, _jnp.float32, 0.5, 1.5)
    if N_MICROBATCH > 1:
        for name, axis in PER_EXAMPLE_BATCH_AXIS.items():
            out[name] = _to_microbatches(out[name], axis)
    return {'x': out['x'], 'c': out['c'], 'positions': out['positions'], 'w_mod': out['w_mod'], 'b_mod': out['b_mod'], 'norm_w': out['norm_w'], 'w_in': out['w_in'], 'conv_w': out['conv_w'], 'a_log': out['a_log'], 'dt_bias': out['dt_bias'], 'dn_norm_w': out['dn_norm_w'], 'at_norm_w': out['at_norm_w'], 'w_out': out['w_out'], 'final_norm_w': out['final_norm_w'], 'loss_target': out['loss_target'], 'm_w_mod': out['m_w_mod'], 'm_b_mod': out['m_b_mod'], 'm_norm_w': out['m_norm_w'], 'm_w_in': out['m_w_in'], 'm_conv_w': out['m_conv_w'], 'm_a_log': out['m_a_log'], 'm_dt_bias': out['m_dt_bias'], 'm_dn_norm_w': out['m_dn_norm_w'], 'm_at_norm_w': out['m_at_norm_w'], 'm_w_out': out['m_w_out'], 'm_final_norm_w': out['m_final_norm_w'], 'v_w_mod': out['v_w_mod'], 'v_b_mod': out['v_b_mod'], 'v_norm_w': out['v_norm_w'], 'v_w_in': out['v_w_in'], 'v_conv_w': out['v_conv_w'], 'v_a_log': out['v_a_log'], 'v_dt_bias': out['v_dt_bias'], 'v_dn_norm_w': out['v_dn_norm_w'], 'v_at_norm_w': out['v_at_norm_w'], 'v_w_out': out['v_w_out'], 'v_final_norm_w': out['v_final_norm_w']}


def _loss(weights, diff, rest, loss_target):
    with _jax.named_scope("forward"):
        args = {**rest, TWIN_DIFF_INPUT: diff, **{k: w.astype(_WEIGHT_DTYPES[k]) for k, w in weights.items()}}
        y = _forward(args)
    with _jax.named_scope("loss_head"):
        err = _jnp.square(y.astype(_jnp.float32) - loss_target)
        return 0.5 * _jnp.sum(_jnp.mean(err, axis=-1)) if err.ndim else 0.5 * err


def _adamw(w, g, m, v):
    m = ADAM_B1 * m + (1.0 - ADAM_B1) * g
    v = ADAM_B2 * v + (1.0 - ADAM_B2) * _jnp.square(g)
    m_hat = m / (1.0 - ADAM_B1 ** ADAM_STEP)
    v_hat = v / (1.0 - ADAM_B2 ** ADAM_STEP)
    delta = -ADAM_LR * (m_hat / (_jnp.sqrt(v_hat) + ADAM_EPS) + ADAM_WD * w)
    return delta, m, v


def reference(x, c, positions, w_mod, b_mod, norm_w, w_in, conv_w, a_log, dt_bias, dn_norm_w, at_norm_w, w_out, final_norm_w, loss_target, m_w_mod, m_b_mod, m_norm_w, m_w_in, m_conv_w, m_a_log, m_dt_bias, m_dn_norm_w, m_at_norm_w, m_w_out, m_final_norm_w, v_w_mod, v_b_mod, v_norm_w, v_w_in, v_conv_w, v_a_log, v_dt_bias, v_dn_norm_w, v_at_norm_w, v_w_out, v_final_norm_w):
    given = dict(x=x, c=c, positions=positions, w_mod=w_mod, b_mod=b_mod, norm_w=norm_w, w_in=w_in, conv_w=conv_w, a_log=a_log, dt_bias=dt_bias, dn_norm_w=dn_norm_w, at_norm_w=at_norm_w, w_out=w_out, final_norm_w=final_norm_w, loss_target=loss_target, m_w_mod=m_w_mod, m_b_mod=m_b_mod, m_norm_w=m_norm_w, m_w_in=m_w_in, m_conv_w=m_conv_w, m_a_log=m_a_log, m_dt_bias=m_dt_bias, m_dn_norm_w=m_dn_norm_w, m_at_norm_w=m_at_norm_w, m_w_out=m_w_out, m_final_norm_w=m_final_norm_w, v_w_mod=v_w_mod, v_b_mod=v_b_mod, v_norm_w=v_norm_w, v_w_in=v_w_in, v_conv_w=v_conv_w, v_a_log=v_a_log, v_dt_bias=v_dt_bias, v_dn_norm_w=v_dn_norm_w, v_at_norm_w=v_at_norm_w, v_w_out=v_w_out, v_final_norm_w=v_final_norm_w)
    weights = {n: given[n] for n in TWIN_WEIGHTS}
    shared = {n: given[n] for n in SHARED_INPUTS}
    per_example = {n: given[n] for n in ['x', 'c', 'positions']}
    grad_fn = _jax.value_and_grad(_loss, argnums=(0, 1))

    def one_microbatch(ex, loss_target):
        ex = dict(ex)
        diff = ex.pop(TWIN_DIFF_INPUT)
        return grad_fn(weights, diff, {**shared, **ex}, loss_target)

    if N_MICROBATCH == 1:
        loss, (grad_w, grad_x) = one_microbatch(per_example, given["loss_target"])
    else:
        def body(carry, xs):
            loss_sum, grad_sum = carry
            l_k, (gw_k, gx_k) = one_microbatch(xs[0], xs[1])
            with _jax.named_scope("update"):
                return (loss_sum + l_k, _jax.tree.map(_jnp.add, grad_sum, gw_k)), gx_k

        init = (_jnp.zeros((), _jnp.float32), _jax.tree.map(_jnp.zeros_like, weights))
        (loss, grad_w), grad_x = _jax.lax.scan(body, init, (per_example, given["loss_target"]))
    with _jax.named_scope("update"):
        delta_w, new_m, new_v = {}, {}, {}
        for n in TWIN_WEIGHTS:
            delta_w[n], new_m[n], new_v[n] = _adamw(weights[n], grad_w[n], given["m_" + n], given["v_" + n])
    return (loss, grad_x, *[grad_w[n] for n in TWIN_WEIGHTS], *[delta_w[n] for n in TWIN_WEIGHTS],
            *[new_m[n] for n in TWIN_WEIGHTS], *[new_v[n] for n in TWIN_WEIGHTS])
```

```python
import functools

import jax
import jax.numpy as jnp
from jax import lax
from jax.experimental import pallas as pl
from jax.experimental.pallas import tpu as pltpu

F32, BF16 = jnp.float32, jnp.bfloat16
HI = lax.Precision.HIGHEST
SDS = jax.ShapeDtypeStruct

D_MODEL = 1024
DN_HEADS, DN_DIM, DN_WIDTH = 4, 128, 512
AT_HEADS, AT_DIM, AT_WIDTH = 8, 64, 512
CONV_K = 4
CHUNK = 64
Q_BLOCK = 128
W_SUB = 128
DILATIONS = (1, 4, 16)
ROPE_THETA = 10000.0
EPS = 1e-6
N_DEV = 8
LANES = 128
BA_PAD = 128
IN_SPLITS = (1536, 512, 4, 4, 512, 512, 512, 512)
IN_COLS = sum(IN_SPLITS)
IN_SHARD = IN_COLS // N_DEV
VMEM_LIMIT = 56 * 2 ** 20

ADAM_LR, ADAM_B1, ADAM_B2, ADAM_EPS, ADAM_WD, ADAM_STEP = 0.001, 0.9, 0.999, 1e-08, 0.01, 10

PK_CONV, PK_DMOD, PK_SILUC, PK_DNW, PK_DFW, PK_ALOG, PK_DTB, PK_DNN, PK_ATN, PK_END = (
    0, 6144, 9216, 10240, 11264, 12288, 12416, 12544, 12672, 12800)
PK_ROWS = PK_END // LANES

_NT = (((1,), (1,)), ((), ()))
_TN = (((0,), (0,)), ((), ()))


def _params(*sem):
    return pltpu.CompilerParams(dimension_semantics=sem or None, vmem_limit_bytes=VMEM_LIMIT)


def _bf(x):
    return x.astype(BF16)


def _nn(a, b):
    return jnp.dot(_bf(a), _bf(b), preferred_element_type=F32)


def _nt(a, b):
    return lax.dot_general(_bf(a), _bf(b), _NT, preferred_element_type=F32)


def _tn(a, b):
    return lax.dot_general(_bf(a), _bf(b), _TN, preferred_element_type=F32)


def _hnn(a, b):
    return jnp.dot(a, b, precision=HI, preferred_element_type=F32)


def _hnt(a, b):
    return lax.dot_general(a, b, _NT, precision=HI, preferred_element_type=F32)


def _htn(a, b):
    return lax.dot_general(a, b, _TN, precision=HI, preferred_element_type=F32)


@jax.custom_vjp
def _d_nt(a, b):
    return _nt(a, b)


def _d_nt_fwd(a, b):
    return _nt(a, b), (a, b)


def _d_nt_bwd(res, g):
    a, b = res
    return _nn(g, b), _tn(g, a)


_d_nt.defvjp(_d_nt_fwd, _d_nt_bwd)


@jax.custom_vjp
def _d_hnn(a, b):
    return _hnn(a, b)


def _d_hnn_fwd(a, b):
    return _hnn(a, b), (a, b)


def _d_hnn_bwd(res, g):
    a, b = res
    return _hnt(g, b), _htn(a, g)


_d_hnn.defvjp(_d_hnn_fwd, _d_hnn_bwd)


def _silu(x):
    return x * jax.nn.sigmoid(x)


def _softplus(x):
    return jnp.maximum(x, 0.0) + jnp.log(1.0 + jnp.exp(-jnp.abs(x)))


def _l2n(x):
    return x * lax.rsqrt(jnp.sum(x * x, axis=-1, keepdims=True) + EPS)


def _post_q(x):
    return _l2n(_silu(x)) * (DN_DIM ** -0.5)


def _post_k(x):
    return _l2n(_silu(x))


def _post_v(x):
    return _silu(x)


def _beta_decay(ba, alog_row, dtb_row):
    lane = lax.broadcasted_iota(jnp.int32, ba.shape, 1)
    return jnp.where(lane < DN_HEADS, jax.nn.sigmoid(ba), -jnp.exp(alog_row) * _softplus(ba + dtb_row))


def _gate_dn(o, z, w):
    return (o * lax.rsqrt(jnp.mean(o * o, axis=-1, keepdims=True) + EPS)) * w * _silu(z)


def _group_ones(scale):
    r = lax.broadcasted_iota(jnp.int32, (LANES, LANES), 0)
    c = lax.broadcasted_iota(jnp.int32, (LANES, LANES), 1)
    return jnp.where((r // AT_DIM) == (c // AT_DIM), scale, 0.0).astype(F32)


def _gate_at(o, z, w2, hnn):
    ms = hnn(o * o, _group_ones(1.0 / AT_DIM))
    return (o * lax.rsqrt(ms + EPS)) * w2 * _silu(z)


def _swap_half64(x):
    lane = lax.broadcasted_iota(jnp.int32, x.shape, 1)
    return jnp.where((lane & (AT_DIM - 1)) < AT_DIM // 2, pltpu.roll(x, LANES - AT_DIM // 2, 1),
                     pltpu.roll(x, AT_DIM // 2, 1))


def _tri_inv(a, hnn):
    r = lax.broadcasted_iota(jnp.int32, (CHUNK, CHUNK), 0)
    c = lax.broadcasted_iota(jnp.int32, (CHUNK, CHUNK), 1)
    eye = (r == c).astype(F32)
    blk = (r // 16) == (c // 16)
    dg = jnp.where(blk, a, 0.0)
    lo = jnp.where(blk, 0.0, a)
    d2 = hnn(dg, dg)
    d4 = hnn(d2, d2)
    d8 = hnn(d4, d4)
    dinv = hnn(hnn(hnn(eye - dg, eye + d2), eye + d4), eye + d8)
    n1 = hnn(dinv, lo)
    n2 = hnn(n1, n1)
    return hnn(hnn(eye - n1, eye + n2), dinv)


def _chunk_fwd(q, k, v, beta, g, diff):
    hnn = _d_hnn if diff else _hnn
    nt = _d_nt if diff else _nt
    r = lax.broadcasted_iota(jnp.int32, (CHUNK, CHUNK), 0)
    c = lax.broadcasted_iota(jnp.int32, (CHUNK, CHUNK), 1)
    tril = r >= c
    strict = r > c
    lower = tril.astype(F32)
    upper = (c > r).astype(F32)
    g64 = jnp.broadcast_to(g, (CHUNK, CHUNK))
    g128 = jnp.broadcast_to(g, (CHUNK, DN_DIM))
    dm = hnn(lower, jnp.where(strict, g64, 0.0))
    gam = jnp.where(tril, jnp.exp(dm), 0.0)
    eg = jnp.exp(hnn(lower, g128))
    e2 = jnp.exp(hnn(upper, g128))
    gl = jnp.exp(jnp.sum(g, axis=0, keepdims=True))
    kb = k * beta
    vb = v * beta
    a = jnp.where(strict, nt(kb, k) * gam, 0.0)
    t = _tri_inv(a, hnn)
    u = hnn(t, vb)
    w = hnn(t, kb * eg)
    p = jnp.where(tril, nt(q, k) * gam, 0.0)
    return u, w, p, q * eg, k * e2, gl


def _exchange(arrays, scatter, name):
    n = len(arrays)
    out_shapes = []
    for a, sc in zip(arrays, scatter):
        out_shapes.append(SDS(a.shape if sc else (N_DEV,) + a.shape, a.dtype))

    def body(*refs):
        ins, outs = refs[:n], refs[n:2 * n]
        send_sems, recv_sems, loc_sems = refs[2 * n:]
        x, y, c = lax.axis_index("x"), lax.axis_index("y"), lax.axis_index("c")
        me = 4 * x + 2 * y + c
        local, remote = [], []
        for i in range(n):
            src = ins[i].at[me] if scatter[i] else ins[i]
            cp = pltpu.make_async_copy(src, outs[i].at[me], loc_sems.at[i])
            cp.start()
            local.append(cp)
        for dlt in range(1, N_DEV):
            px = 1 - x if dlt & 4 else x
            py = 1 - y if dlt & 2 else y
            pc = 1 - c if dlt & 1 else c
            peer = 4 * px + 2 * py + pc
            for i in range(n):
                src = ins[i].at[peer] if scatter[i] else ins[i]
                cp = pltpu.make_async_remote_copy(
                    src_ref=src, dst_ref=outs[i].at[me],
                    send_sem=send_sems.at[i, dlt - 1], recv_sem=recv_sems.at[i, dlt - 1],
                    device_id=(px, py, pc), device_id_type=pl.DeviceIdType.MESH)
                cp.start()
                arrive = pltpu.make_async_remote_copy(
                    src_ref=src, dst_ref=outs[i].at[peer],
                    send_sem=send_sems.at[i, dlt - 1], recv_sem=recv_sems.at[i, dlt - 1],
                    device_id=(px, py, pc), device_id_type=pl.DeviceIdType.MESH)
                remote.append((cp, arrive))
        for cp, arrive in remote:
            cp.wait_send()
            arrive.wait_recv()
        for cp in local:
            cp.wait()

    any_spec = pl.BlockSpec(memory_space=pl.ANY)
    return pl.pallas_call(
        body, name=name, out_shape=tuple(out_shapes),
        in_specs=[any_spec] * n, out_specs=tuple([any_spec] * n),
        scratch_shapes=[pltpu.SemaphoreType.DMA((n, N_DEV - 1)), pltpu.SemaphoreType.DMA((n, N_DEV - 1)),
                        pltpu.SemaphoreType.DMA((n,))],
    )(*arrays)


def _adaln_mod(c, w_mod, b_mod):
    def body(c_ref, w_ref, b_ref, mod_ref, sc_ref):
        sc = _silu(c_ref[...])
        sc8 = jnp.broadcast_to(sc, (8, D_MODEL))
        mod_ref[...] = _nn(sc8, w_ref[...])[0:1] + b_ref[...]
        sc_ref[...] = sc

    return pl.pallas_call(body, name="adaln_mod", compiler_params=_params(),
                          out_shape=(SDS((1, 3 * D_MODEL), F32), SDS((1, D_MODEL), F32)))(c, w_mod, b_mod)


def _ln_proj(x, mod, norm_w, ws, cos_t, sin_t, ts):
    s = x.shape[0]
    widths = [w.shape[1] for w in ws]

    def body(x_ref, mod_ref, nw_ref, cos_ref, sin_ref, wqkv, wz, wba, waq, wak, wav, waz,
             h_ref, oqkv, oz, oba, oq, ok, ov, oaz):
        xt = x_ref[...]
        r = lax.rsqrt(jnp.mean(xt * xt, axis=-1, keepdims=True) + EPS)
        shift, scale = mod_ref[:, 0:D_MODEL], mod_ref[:, D_MODEL:2 * D_MODEL]
        h = ((xt * r) * nw_ref[...]) * (1.0 + scale) + shift
        hb = _bf(h)
        h_ref[...] = hb
        oqkv[...] = jnp.dot(hb, wqkv[...], preferred_element_type=F32)
        oz[...] = jnp.dot(hb, wz[...], preferred_element_type=F32)
        oba[...] = jnp.dot(hb, wba[...], preferred_element_type=F32)
        oaz[...] = jnp.dot(hb, waz[...], preferred_element_type=F32)
        ov[...] = _bf(jnp.dot(hb, wav[...], preferred_element_type=F32))
        cs, sn = cos_ref[...], sin_ref[...]
        for w_ref, o_ref in ((waq, oq), (wak, ok)):
            t = jnp.dot(hb, w_ref[...], preferred_element_type=F32)
            for j in range(AT_WIDTH // LANES):
                cols = slice(j * LANES, (j + 1) * LANES)
                tj = t[:, cols]
                o_ref[:, cols] = _bf(tj * cs + _swap_half64(tj) * sn)

    tok = lambda w: pl.BlockSpec((ts, w), lambda i: (i, 0))
    full = lambda a: pl.BlockSpec(a.shape, lambda i: (0, 0))
    return pl.pallas_call(
        body, name="ln_proj", grid=(s // ts,), compiler_params=_params("arbitrary"),
        in_specs=[tok(D_MODEL), full(mod), full(norm_w), tok(LANES), tok(LANES)] + [full(w) for w in ws],
        out_specs=(tok(D_MODEL), tok(widths[0]), tok(widths[1]), tok(widths[2]), tok(widths[3]), tok(widths[4]),
                   tok(widths[5]), tok(widths[6])),
        out_shape=(SDS((s, D_MODEL), BF16), SDS((s, widths[0]), F32), SDS((s, widths[1]), F32),
                   SDS((s, widths[2]), F32), SDS((s, widths[3]), BF16), SDS((s, widths[4]), BF16),
                   SDS((s, widths[5]), BF16), SDS((s, widths[6]), F32)),
    )(x, mod, norm_w, cos_t, sin_t, *ws)


def _conv_taps(ext, rows):
    taps = []
    for j in range(CONV_K):
        sh = CONV_K - 1 - j
        rolled = pltpu.roll(ext, sh, 0) if sh else ext
        taps.append(rolled[8:8 + rows])
    return taps


def _dn_prep(qkv_pre, ba, conv_w8, alog_row, dtb_row, ts):
    s = qkv_pre.shape[0]
    cw = 3 * DN_WIDTH

    def body(pre_ref, halo_ref, ba_ref, cw_ref, al_ref, dtb_ref, q_ref, k_ref, v_ref, bg_ref):
        n = pl.program_id(0)
        prev = jnp.where(n == 0, 0.0, halo_ref[...])
        ext = jnp.concatenate([prev, pre_ref[...]], axis=0)
        taps = _conv_taps(ext, ts)
        conv = taps[0] * cw_ref[0:1, :]
        for j in range(1, CONV_K):
            conv = conv + taps[j] * cw_ref[j:j + 1, :]
        for h in range(DN_HEADS):
            cols = slice(h * DN_DIM, (h + 1) * DN_DIM)
            q_ref[:, cols] = _post_q(conv[:, h * DN_DIM:(h + 1) * DN_DIM])
            k_ref[:, cols] = _post_k(conv[:, DN_WIDTH + h * DN_DIM:DN_WIDTH + (h + 1) * DN_DIM])
            v_ref[:, cols] = _post_v(conv[:, 2 * DN_WIDTH + h * DN_DIM:2 * DN_WIDTH + (h + 1) * DN_DIM])
        bg_ref[...] = _beta_decay(ba_ref[...], al_ref[...], dtb_ref[...])

    tok = lambda w: pl.BlockSpec((ts, w), lambda i: (i, 0))
    full = lambda a: pl.BlockSpec(a.shape, lambda i: (0, 0))
    halo = pl.BlockSpec((8, cw), lambda i: (jnp.maximum(i * (ts // 8) - 1, 0), 0))
    return pl.pallas_call(
        body, name="dn_prep", grid=(s // ts,), compiler_params=_params("arbitrary"),
        in_specs=[tok(cw), halo, tok(BA_PAD), full(conv_w8), full(alog_row), full(dtb_row)],
        out_specs=(tok(DN_WIDTH), tok(DN_WIDTH), tok(DN_WIDTH), tok(BA_PAD)),
        out_shape=(SDS((s, DN_WIDTH), F32),) * 3 + (SDS((s, BA_PAD), F32),),
    )(qkv_pre, qkv_pre, ba, conv_w8, alog_row, dtb_row)


def _dn_chunk_prep(q, k, v, bg, ts):
    s = q.shape[0]
    ncs = ts // CHUNK

    def body(q_ref, k_ref, v_ref, bg_ref, u_ref, w_ref, qd_ref, kd_ref, p_ref, gl_ref):
        def chunk(ci, carry):
            rows = pl.ds(pl.multiple_of(ci * CHUNK, CHUNK), CHUNK)
            rows8 = pl.ds(pl.multiple_of(ci * 8, 8), 8)
            bgc = bg_ref[rows, :]
            for h in range(DN_HEADS):
                cols = slice(h * DN_DIM, (h + 1) * DN_DIM)
                u, w, p, qd, kd, gl = _chunk_fwd(q_ref[rows, cols], k_ref[rows, cols], v_ref[rows, cols],
                                                 bgc[:, h:h + 1], bgc[:, DN_HEADS + h:DN_HEADS + h + 1], False)
                u_ref[rows, cols] = u
                w_ref[rows, cols] = w
                qd_ref[rows, cols] = qd
                kd_ref[rows, cols] = kd
                p_ref[h, rows, :] = p
                gl_ref[rows8, cols] = jnp.broadcast_to(gl, (8, DN_DIM))
            return carry

        lax.fori_loop(0, ncs, chunk, 0)

    tok = lambda w: pl.BlockSpec((ts, w), lambda i: (i, 0))
    return pl.pallas_call(
        body, name="dn_chunk_prep", grid=(s // ts,), compiler_params=_params("arbitrary"),
        in_specs=[tok(DN_WIDTH)] * 3 + [tok(BA_PAD)],
        out_specs=(tok(DN_WIDTH),) * 4 + (pl.BlockSpec((DN_HEADS, ts, CHUNK), lambda i: (0, i, 0)),
                                           pl.BlockSpec((ncs * 8, DN_WIDTH), lambda i: (i, 0))),
        out_shape=(SDS((s, DN_WIDTH), F32),) * 4 + (SDS((DN_HEADS, s, CHUNK), F32),
                                                     SDS((s // CHUNK * 8, DN_WIDTH), F32)),
    )(q, k, v, bg)


def _dn_scan(u, w, qd, kd, p, gl, ts):
    s = u.shape[0]
    ncs = ts // CHUNK

    def body(u_ref, w_ref, qd_ref, kd_ref, p_ref, gl_ref, o_ref, vn_ref, st_ref, state):
        @pl.when(pl.program_id(0) == 0)
        def _():
            state[...] = jnp.zeros_like(state)

        def chunk(ci, carry):
            rows = pl.ds(pl.multiple_of(ci * CHUNK, CHUNK), CHUNK)
            rows8 = pl.ds(pl.multiple_of(ci * 8, 8), 8)
            srows = pl.ds(pl.multiple_of(ci * DN_DIM, DN_DIM), DN_DIM)
            for h in range(DN_HEADS):
                cols = slice(h * DN_DIM, (h + 1) * DN_DIM)
                sf = state[h]
                st_ref[srows, cols] = sf
                vn = u_ref[rows, cols] - _nn(w_ref[rows, cols], sf)
                o_ref[rows, cols] = _nn(qd_ref[rows, cols], sf) + _nn(p_ref[h, rows, :], vn)
                vn_ref[rows, cols] = vn
                state[h] = sf * gl_ref[rows8, cols][0:1] + _tn(kd_ref[rows, cols], vn)
            return carry

        lax.fori_loop(0, ncs, chunk, 0)

    tok = lambda wd: pl.BlockSpec((ts, wd), lambda i: (i, 0))
    return pl.pallas_call(
        body, name="dn_scan", grid=(s // ts,), compiler_params=_params("arbitrary"),
        in_specs=[tok(DN_WIDTH)] * 4 + [pl.BlockSpec((DN_HEADS, ts, CHUNK), lambda i: (0, i, 0)),
                                        pl.BlockSpec((ncs * 8, DN_WIDTH), lambda i: (i, 0))],
        out_specs=(tok(DN_WIDTH), tok(DN_WIDTH), pl.BlockSpec((ncs * DN_DIM, DN_WIDTH), lambda i: (i, 0))),
        out_shape=(SDS((s, DN_WIDTH), F32), SDS((s, DN_WIDTH), F32), SDS((s // CHUNK * DN_DIM, DN_WIDTH), F32)),
        scratch_shapes=[pltpu.VMEM((DN_HEADS, DN_DIM, DN_DIM), F32)],
    )(u, w, qd, kd, p, gl)


def _attn_mask(n, valid=None):
    qi = lax.broadcasted_iota(jnp.int32, (Q_BLOCK, 2 * Q_BLOCK), 0)
    kj = lax.broadcasted_iota(jnp.int32, (Q_BLOCK, 2 * Q_BLOCK), 1)
    rel = Q_BLOCK + qi - kj
    mask = (rel >= 0) & (rel <= W_SUB) & ((kj >= Q_BLOCK) | (n > 0))
    return mask if valid is None else mask & valid


def _attn_specs(d, nb):
    cur = pl.BlockSpec((Q_BLOCK, AT_WIDTH), lambda r, n: (jnp.minimum(n, nb - 1), r))
    prev = pl.BlockSpec((Q_BLOCK, AT_WIDTH), lambda r, n: (jnp.maximum(jnp.minimum(n, nb - 1) - 1, 0), r))
    return cur, prev


def _attn_fwd(qr, kr, vb, d):
    s = qr.shape[0]
    sub = s // d
    nb = sub // Q_BLOCK
    scale = AT_DIM ** -0.5

    def body(q_ref, kc_ref, kp_ref, vc_ref, vp_ref, o_ref, lse_ref):
        mask = _attn_mask(pl.program_id(1))
        lo = lax.broadcasted_iota(jnp.int32, (Q_BLOCK, LANES), 1) < AT_DIM
        for j in range(AT_WIDTH // LANES):
            cols = slice(j * LANES, (j + 1) * LANES)
            q = q_ref[:, cols]
            kk = jnp.concatenate([kp_ref[:, cols], kc_ref[:, cols]], axis=0)
            vv = jnp.concatenate([vp_ref[:, cols], vc_ref[:, cols]], axis=0)
            outs, lses = [], []
            for sel in (lo, ~lo):
                qm = jnp.where(sel, q, jnp.zeros_like(q))
                sc = lax.dot_general(qm, kk, _NT, preferred_element_type=F32) * scale
                sc = jnp.where(mask, sc, -1e30)
                m = jnp.max(sc, axis=-1, keepdims=True)
                pr = jnp.exp(sc - m)
                l = jnp.sum(pr, axis=-1, keepdims=True)
                outs.append(jnp.dot(_bf(pr), vv, preferred_element_type=F32) / l)
                lses.append(m + jnp.log(l))
            o_ref[:, cols] = jnp.where(lo, outs[0], outs[1])
            lse_ref[:, cols] = jnp.where(lo, lses[0], lses[1])

    cur, prev = _attn_specs(d, nb)
    view = lambda a: a.reshape(sub, d * AT_WIDTH)
    o, lse = pl.pallas_call(
        body, name=f"attn_fwd_d{d}", grid=(d, nb), compiler_params=_params("arbitrary", "arbitrary"),
        in_specs=[cur, cur, prev, cur, prev], out_specs=(cur, cur),
        out_shape=(SDS((sub, d * AT_WIDTH), F32),) * 2,
    )(view(qr), view(kr), view(kr), view(vb), view(vb))
    return o.reshape(s, AT_WIDTH), lse.reshape(s, AT_WIDTH)


def _mix_prep(o_dn, z_dn, o_ps, lse_ps, z_at, dnw, atw2, ts):
    s = o_dn.shape[0]

    def body(odn, zdn, o1, o2, o3, l1, l2, l3, zat, dnw_ref, atw_ref, cat_ref, oat_ref, lse_ref):
        for h in range(DN_HEADS):
            cols = slice(h * DN_DIM, (h + 1) * DN_DIM)
            cat_ref[:, cols] = _bf(_gate_dn(odn[:, cols], zdn[:, cols], dnw_ref[...]))
        for j in range(AT_WIDTH // LANES):
            cols = slice(j * LANES, (j + 1) * LANES)
            ls = [l1[:, cols], l2[:, cols], l3[:, cols]]
            m = jnp.maximum(jnp.maximum(ls[0], ls[1]), ls[2])
            es = [jnp.exp(l - m) for l in ls]
            den = es[0] + es[1] + es[2]
            oat = (es[0] * o1[:, cols] + es[1] * o2[:, cols] + es[2] * o3[:, cols]) / den
            oat_ref[:, cols] = oat
            lse_ref[:, cols] = m + jnp.log(den)
            cat_ref[:, DN_WIDTH + j * LANES:DN_WIDTH + (j + 1) * LANES] = _bf(
                _gate_at(oat, zat[:, cols], atw_ref[...], _hnn))

    tok = lambda w: pl.BlockSpec((ts, w), lambda i: (i, 0))
    full = lambda a: pl.BlockSpec(a.shape, lambda i: (0, 0))
    return pl.pallas_call(
        body, name="mix_prep", grid=(s // ts,), compiler_params=_params("arbitrary"),
        in_specs=[tok(DN_WIDTH)] * 9 + [full(dnw), full(atw2)],
        out_specs=(tok(D_MODEL), tok(AT_WIDTH), tok(AT_WIDTH)),
        out_shape=(SDS((s, D_MODEL), BF16), SDS((s, AT_WIDTH), F32), SDS((s, AT_WIDTH), F32)),
    )(o_dn, z_dn, *o_ps, *lse_ps, z_at, dnw, atw2)


def _out_loss(cat, x, tgt, w_out, gate, fw, ts):
    s = x.shape[0]

    def body(cat_ref, x_ref, t_ref, w_ref, g_ref, fw_ref, dx2_ref, dcat_ref, gw_ref, dfw_ref, dgate_ref, loss_ref):
        @pl.when(pl.program_id(0) == 0)
        def _():
            gw_ref[...] = jnp.zeros_like(gw_ref)
            dfw_ref[...] = jnp.zeros_like(dfw_ref)
            dgate_ref[...] = jnp.zeros_like(dgate_ref)
            loss_ref[...] = jnp.zeros_like(loss_ref)

        catb = cat_ref[...]
        wb = w_ref[...]
        gate, fwv = g_ref[...], fw_ref[...]
        mix = jnp.dot(catb, wb, preferred_element_type=F32)
        x2 = x_ref[...] + gate * mix
        r2 = lax.rsqrt(jnp.mean(x2 * x2, axis=-1, keepdims=True) + EPS)
        xn2 = x2 * r2
        err = xn2 * fwv - t_ref[...]
        row = jnp.sum(err * err, axis=-1, keepdims=True) * (1.0 / D_MODEL)
        loss_ref[...] += 0.5 * jnp.sum(row, axis=0, keepdims=True)
        dy = err * (1.0 / D_MODEL)
        dfw_ref[...] += jnp.sum(dy * xn2, axis=0, keepdims=True)
        dxn = dy * fwv
        dx2 = r2 * (dxn - xn2 * jnp.mean(dxn * xn2, axis=-1, keepdims=True))
        dx2_ref[...] = dx2
        dgate_ref[...] += jnp.sum(dx2 * mix, axis=0, keepdims=True)
        dmix = _bf(gate * dx2)
        dcat_ref[...] = lax.dot_general(dmix, wb, _NT, preferred_element_type=F32)
        gw_ref[...] += lax.dot_general(catb, dmix, _TN, preferred_element_type=F32)

    tok = lambda w: pl.BlockSpec((ts, w), lambda i: (i, 0))
    full = lambda a: pl.BlockSpec(a.shape, lambda i: (0, 0))
    row = pl.BlockSpec((1, D_MODEL), lambda i: (0, 0))
    return pl.pallas_call(
        body, name="out_loss", grid=(s // ts,), compiler_params=_params("arbitrary"),
        in_specs=[tok(D_MODEL), tok(D_MODEL), tok(D_MODEL), full(w_out), full(gate), full(fw)],
        out_specs=(tok(D_MODEL), tok(D_MODEL), pl.BlockSpec((D_MODEL, D_MODEL), lambda i: (0, 0)), row, row,
                   pl.BlockSpec((1, 1), lambda i: (0, 0))),
        out_shape=(SDS((s, D_MODEL), F32), SDS((s, D_MODEL), F32), SDS((D_MODEL, D_MODEL), F32),
                   SDS((1, D_MODEL), F32), SDS((1, D_MODEL), F32), SDS((1, 1), F32)),
    )(cat, x, tgt, w_out, gate, fw)


def _mix_bwd(dcat, o_dn, z_dn, o_at, z_at, dnw, atw2, ts):
    s = dcat.shape[0]

    def body(dcat_ref, odn, zdn, oat, zat, dnw_ref, atw_ref, dodn, dzdn, doat, dzat, delta, ddnw, datw):
        @pl.when(pl.program_id(0) == 0)
        def _():
            ddnw[...] = jnp.zeros_like(ddnw)
            datw[...] = jnp.zeros_like(datw)

        for h in range(DN_HEADS):
            cols = slice(h * DN_DIM, (h + 1) * DN_DIM)
            _, vjp = jax.vjp(_gate_dn, odn[:, cols], zdn[:, cols], dnw_ref[...])
            do, dz, dw = vjp(dcat_ref[:, cols])
            dodn[:, cols] = do
            dzdn[:, cols] = _bf(dz)
            ddnw[...] += dw
        for j in range(AT_WIDTH // LANES):
            cols = slice(j * LANES, (j + 1) * LANES)
            o = oat[:, cols]
            _, vjp = jax.vjp(functools.partial(_gate_at, hnn=_d_hnn), o, zat[:, cols], atw_ref[...])
            do, dz, dw = vjp(dcat_ref[:, DN_WIDTH + j * LANES:DN_WIDTH + (j + 1) * LANES])
            doat[:, cols] = do
            dzat[:, cols] = _bf(dz)
            datw[...] += dw
            delta[:, cols] = _hnn(do * o, _group_ones(1.0))

    tok = lambda w: pl.BlockSpec((ts, w), lambda i: (i, 0))
    full = lambda a: pl.BlockSpec(a.shape, lambda i: (0, 0))
    row = pl.BlockSpec((1, LANES), lambda i: (0, 0))
    return pl.pallas_call(
        body, name="mix_bwd", grid=(s // ts,), compiler_params=_params("arbitrary"),
        in_specs=[tok(D_MODEL)] + [tok(DN_WIDTH)] * 4 + [full(dnw), full(atw2)],
        out_specs=(tok(DN_WIDTH),) * 5 + (row, row),
        out_shape=(SDS((s, DN_WIDTH), F32), SDS((s, DN_WIDTH), BF16), SDS((s, AT_WIDTH), F32),
                   SDS((s, AT_WIDTH), BF16), SDS((s, AT_WIDTH), F32), SDS((1, LANES), F32), SDS((1, LANES), F32)),
    )(dcat, o_dn, z_dn, o_at, z_at, dnw, atw2)


def _attn_bwd(qr, kr, vb, do, lse, delta, d):
    s = qr.shape[0]
    sub = s // d
    nb = sub // Q_BLOCK
    scale = AT_DIM ** -0.5

    def body(q_ref, kc_ref, kp_ref, vc_ref, vp_ref, do_ref, lse_ref, dl_ref, dq_ref, dk_ref, dv_ref, dk_acc, dv_acc):
        n = pl.program_id(1)
        valid = n < nb

        @pl.when(n == 0)
        def _():
            dk_acc[...] = jnp.zeros_like(dk_acc)
            dv_acc[...] = jnp.zeros_like(dv_acc)

        mask = _attn_mask(n, valid)
        lo = lax.broadcasted_iota(jnp.int32, (Q_BLOCK, LANES), 1) < AT_DIM
        for j in range(AT_WIDTH // LANES):
            cols = slice(j * LANES, (j + 1) * LANES)
            q = q_ref[:, cols]
            kk = jnp.concatenate([kp_ref[:, cols], kc_ref[:, cols]], axis=0)
            vv = jnp.concatenate([vp_ref[:, cols], vc_ref[:, cols]], axis=0)
            dob = _bf(do_ref[:, cols])
            lse2, dl2 = lse_ref[:, cols], dl_ref[:, cols]
            dkk = jnp.zeros((2 * Q_BLOCK, LANES), F32)
            dvv = jnp.zeros((2 * Q_BLOCK, LANES), F32)
            dqs = []
            for sel in (lo, ~lo):
                qm = jnp.where(sel, q, jnp.zeros_like(q))
                dom = jnp.where(sel, dob, jnp.zeros_like(dob))
                lse_c = jnp.max(jnp.where(sel, lse2, -jnp.inf), axis=-1, keepdims=True)
                dl_c = jnp.max(jnp.where(sel, dl2, -jnp.inf), axis=-1, keepdims=True)
                sc = lax.dot_general(qm, kk, _NT, preferred_element_type=F32) * scale
                pr = jnp.where(mask, jnp.exp(jnp.where(mask, sc - lse_c, 0.0)), 0.0)
                dp = lax.dot_general(dom, vv, _NT, preferred_element_type=F32)
                ds = _bf(pr * (dp - dl_c) * scale)
                dqs.append(jnp.dot(ds, kk, preferred_element_type=F32))
                dkk = dkk + lax.dot_general(ds, qm, _TN, preferred_element_type=F32)
                dvv = dvv + lax.dot_general(_bf(pr), dom, _TN, preferred_element_type=F32)

            @pl.when(valid)
            def _():
                dq_ref[:, cols] = jnp.where(lo, dqs[0], dqs[1])

            dk_ref[:, cols] = dk_acc[:, cols] + dkk[:Q_BLOCK]
            dv_ref[:, cols] = dv_acc[:, cols] + dvv[:Q_BLOCK]
            dk_acc[:, cols] = dkk[Q_BLOCK:]
            dv_acc[:, cols] = dvv[Q_BLOCK:]

    cur, prev = _attn_specs(d, nb)
    kout = pl.BlockSpec((Q_BLOCK, AT_WIDTH), lambda r, n: (jnp.maximum(n - 1, 0), r))
    view = lambda a: a.reshape(sub, d * AT_WIDTH)
    outs = pl.pallas_call(
        body, name=f"attn_bwd_d{d}", grid=(d, nb + 1), compiler_params=_params("arbitrary", "arbitrary"),
        in_specs=[cur, cur, prev, cur, prev, cur, cur, cur], out_specs=(cur, kout, kout),
        out_shape=(SDS((sub, d * AT_WIDTH), F32),) * 3,
        scratch_shapes=[pltpu.VMEM((Q_BLOCK, AT_WIDTH), F32), pltpu.VMEM((Q_BLOCK, AT_WIDTH), F32)],
    )(view(qr), view(kr), view(kr), view(vb), view(vb), view(do), view(lse), view(delta))
    return tuple(o.reshape(s, AT_WIDTH) for o in outs)


def _rope_bwd(dqs, dks, dvs, cos_t, sin_t, ts):
    s = cos_t.shape[0]

    def body(q1, q2, q3, k1, k2, k3, v1, v2, v3, cos_ref, sin_ref, oq, ok, ov):
        cs, sn = cos_ref[...], sin_ref[...]
        for j in range(AT_WIDTH // LANES):
            cols = slice(j * LANES, (j + 1) * LANES)
            for (a, b, c), o_ref in (((q1, q2, q3), oq), ((k1, k2, k3), ok)):
                g = a[:, cols] + b[:, cols] + c[:, cols]
                o_ref[:, cols] = _bf(g * cs + _swap_half64(g * sn))
            ov[:, cols] = _bf(v1[:, cols] + v2[:, cols] + v3[:, cols])

    tok = lambda w: pl.BlockSpec((ts, w), lambda i: (i, 0))
    return pl.pallas_call(
        body, name="rope_bwd", grid=(s // ts,), compiler_params=_params("arbitrary"),
        in_specs=[tok(AT_WIDTH)] * 9 + [tok(LANES)] * 2, out_specs=(tok(AT_WIDTH),) * 3,
        out_shape=(SDS((s, AT_WIDTH), BF16),) * 3,
    )(*dqs, *dks, *dvs, cos_t, sin_t)


def _dn_scan_bwd(do, st, vn, w, qd, kd, p, gl, ts):
    s = do.shape[0]
    ncs = ts // CHUNK
    nt = s // ts

    def body(do_ref, st_ref, vn_ref, w_ref, qd_ref, kd_ref, p_ref, gl_ref,
             du_ref, dw_ref, dqd_ref, dkd_ref, dp_ref, dgl_ref, dstate):
        @pl.when(pl.program_id(0) == 0)
        def _():
            dstate[...] = jnp.zeros_like(dstate)

        def chunk(jr, carry):
            ci = ncs - 1 - jr
            rows = pl.ds(pl.multiple_of(ci * CHUNK, CHUNK), CHUNK)
            rows8 = pl.ds(pl.multiple_of(ci * 8, 8), 8)
            srows = pl.ds(pl.multiple_of(ci * DN_DIM, DN_DIM), DN_DIM)
            for h in range(DN_HEADS):
                cols = slice(h * DN_DIM, (h + 1) * DN_DIM)
                ds_, sf = dstate[h], st_ref[srows, cols]
                vnc, doc, wc, qdc, kdc, pc = (vn_ref[rows, cols], do_ref[rows, cols], w_ref[rows, cols],
                                              qd_ref[rows, cols], kd_ref[rows, cols], p_ref[h, rows, :])
                dvn = _nn(kdc, ds_) + _tn(pc, doc)
                du_ref[rows, cols] = dvn
                dw_ref[rows, cols] = -_nt(dvn, sf)
                dqd_ref[rows, cols] = _nt(doc, sf)
                dkd_ref[rows, cols] = _nt(vnc, ds_)
                dp_ref[h, rows, :] = _nt(doc, vnc)
                dgl = jnp.sum(jnp.sum(ds_ * sf, axis=1, keepdims=True), axis=0, keepdims=True)
                dgl_ref[rows8, cols] = jnp.broadcast_to(dgl, (8, DN_DIM))
                dstate[h] = ds_ * gl_ref[rows8, cols][0:1] + _tn(qdc, doc) - _tn(wc, dvn)
            return carry

        lax.fori_loop(0, ncs, chunk, 0)

    tok = lambda wd: pl.BlockSpec((ts, wd), lambda i: (nt - 1 - i, 0))
    pspec = pl.BlockSpec((DN_HEADS, ts, CHUNK), lambda i: (0, nt - 1 - i, 0))
    g8 = pl.BlockSpec((ncs * 8, DN_WIDTH), lambda i: (nt - 1 - i, 0))
    return pl.pallas_call(
        body, name="dn_scan_bwd", grid=(nt,), compiler_params=_params("arbitrary"),
        in_specs=[tok(DN_WIDTH), pl.BlockSpec((ncs * DN_DIM, DN_WIDTH), lambda i: (nt - 1 - i, 0))]
        + [tok(DN_WIDTH)] * 4 + [pspec, g8],
        out_specs=(tok(DN_WIDTH),) * 4 + (pspec, g8),
        out_shape=(SDS((s, DN_WIDTH), F32),) * 4 + (SDS((DN_HEADS, s, CHUNK), F32),
                                                     SDS((s // CHUNK * 8, DN_WIDTH), F32)),
        scratch_shapes=[pltpu.VMEM((DN_HEADS, DN_DIM, DN_DIM), F32)],
    )(do, st, vn, w, qd, kd, p, gl)


def _dn_chunk_bwd(q, k, v, bg, du, dw, dqd, dkd, dp, dgl, ts):
    s = q.shape[0]
    ncs = ts // CHUNK

    def body(q_ref, k_ref, v_ref, bg_ref, du_ref, dw_ref, dqd_ref, dkd_ref, dp_ref, dgl_ref,
             dq_ref, dk_ref, dv_ref, dbg_ref):
        def chunk(ci, carry):
            rows = pl.ds(pl.multiple_of(ci * CHUNK, CHUNK), CHUNK)
            rows8 = pl.ds(pl.multiple_of(ci * 8, 8), 8)
            bgc = bg_ref[rows, :]
            lane = lax.broadcasted_iota(jnp.int32, (CHUNK, BA_PAD), 1)
            dbg = jnp.zeros((CHUNK, BA_PAD), F32)
            for h in range(DN_HEADS):
                cols = slice(h * DN_DIM, (h + 1) * DN_DIM)
                _, vjp = jax.vjp(functools.partial(_chunk_fwd, diff=True),
                                 q_ref[rows, cols], k_ref[rows, cols], v_ref[rows, cols],
                                 bgc[:, h:h + 1], bgc[:, DN_HEADS + h:DN_HEADS + h + 1])
                dglc = dgl_ref[rows8, cols][0:1, 0:1]
                dq, dk, dv, dbeta, dg = vjp((du_ref[rows, cols], dw_ref[rows, cols], dp_ref[h, rows, :],
                                             dqd_ref[rows, cols], dkd_ref[rows, cols], dglc))
                dq_ref[rows, cols] = dq
                dk_ref[rows, cols] = dk
                dv_ref[rows, cols] = dv
                dbg = dbg + jnp.where(lane == h, dbeta, 0.0) + jnp.where(lane == DN_HEADS + h, dg, 0.0)
            dbg_ref[rows, :] = dbg
            return carry

        lax.fori_loop(0, ncs, chunk, 0)

    tok = lambda wd: pl.BlockSpec((ts, wd), lambda i: (i, 0))
    pspec = pl.BlockSpec((DN_HEADS, ts, CHUNK), lambda i: (0, i, 0))
    g8 = pl.BlockSpec((ncs * 8, DN_WIDTH), lambda i: (i, 0))
    return pl.pallas_call(
        body, name="dn_chunk_bwd", grid=(s // ts,), compiler_params=_params("arbitrary"),
        in_specs=[tok(DN_WIDTH)] * 3 + [tok(BA_PAD)] + [tok(DN_WIDTH)] * 4 + [pspec, g8],
        out_specs=(tok(DN_WIDTH),) * 3 + (tok(BA_PAD),),
        out_shape=(SDS((s, DN_WIDTH), F32),) * 3 + (SDS((s, BA_PAD), F32),),
    )(q, k, v, bg, du, dw, dqd, dkd, dp, dgl)


def _dn_prep_bwd(qkv_pre, ba, dq, dk, dv, dbg, conv_w8, alog_row, dtb_row, ts):
    s = qkv_pre.shape[0]
    cw = 3 * DN_WIDTH
    nt = s // ts

    def body(pre_ref, ph_ref, nh_ref, ba_ref, dq_ref, dqh_ref, dk_ref, dkh_ref, dv_ref, dvh_ref, dbg_ref,
             cw_ref, al_ref, dtb_ref, dpre_ref, dba_ref, dcw_ref, dal_ref, ddtb_ref):
        n = pl.program_id(0)

        @pl.when(n == 0)
        def _():
            dcw_ref[...] = jnp.zeros_like(dcw_ref)
            dal_ref[...] = jnp.zeros_like(dal_ref)
            ddtb_ref[...] = jnp.zeros_like(ddtb_ref)

        last = n == nt - 1
        prev = jnp.where(n == 0, 0.0, ph_ref[...])
        ext = jnp.concatenate([prev, pre_ref[...], nh_ref[...]], axis=0)
        taps = _conv_taps(ext, ts + 8)
        conv = taps[0] * cw_ref[0:1, :]
        for j in range(1, CONV_K):
            conv = conv + taps[j] * cw_ref[j:j + 1, :]

        def cot(main, halo, cols):
            return jnp.concatenate([main[:, cols], jnp.where(last, 0.0, halo[:, cols])], axis=0)

        pieces = []
        for grp, (fn, mref, href) in enumerate(((_post_q, dq_ref, dqh_ref), (_post_k, dk_ref, dkh_ref),
                                                (_post_v, dv_ref, dvh_ref))):
            for h in range(DN_HEADS):
                cols = slice(h * DN_DIM, (h + 1) * DN_DIM)
                c0 = grp * DN_WIDTH + h * DN_DIM
                _, vjp = jax.vjp(fn, conv[:, c0:c0 + DN_DIM])
                pieces.append(vjp(cot(mref, href, cols))[0])
        dconv = jnp.concatenate(pieces, axis=1)
        rows = ts + 8
        dpre = dconv[:ts] * cw_ref[CONV_K - 1:CONV_K, :]
        for j in range(CONV_K - 1):
            sh = CONV_K - 1 - j
            dpre = dpre + pltpu.roll(dconv, rows - sh, 0)[:ts] * cw_ref[j:j + 1, :]
        dpre_ref[...] = _bf(dpre)
        for j in range(CONV_K):
            dcw_ref[j:j + 1, :] += jnp.sum(dconv[:ts] * taps[j][:ts], axis=0, keepdims=True)

        _, vjp = jax.vjp(_beta_decay, ba_ref[...], al_ref[...], dtb_ref[...])
        dba, dal, ddtb = vjp(dbg_ref[...])
        dba_ref[...] = _bf(dba)
        dal_ref[...] += dal
        ddtb_ref[...] += ddtb

    tok = lambda w: pl.BlockSpec((ts, w), lambda i: (i, 0))
    full = lambda a: pl.BlockSpec(a.shape, lambda i: (0, 0))
    prevh = lambda w: pl.BlockSpec((8, w), lambda i: (jnp.maximum(i * (ts // 8) - 1, 0), 0))
    nexth = lambda w: pl.BlockSpec((8, w), lambda i: (jnp.minimum((i + 1) * (ts // 8), s // 8 - 1), 0))
    row = pl.BlockSpec((1, LANES), lambda i: (0, 0))
    return pl.pallas_call(
        body, name="dn_prep_bwd", grid=(nt,), compiler_params=_params("arbitrary"),
        in_specs=[tok(cw), prevh(cw), nexth(cw), tok(BA_PAD),
                  tok(DN_WIDTH), nexth(DN_WIDTH), tok(DN_WIDTH), nexth(DN_WIDTH), tok(DN_WIDTH), nexth(DN_WIDTH),
                  tok(BA_PAD), full(conv_w8), full(alog_row), full(dtb_row)],
        out_specs=(tok(cw), tok(BA_PAD), pl.BlockSpec((8, cw), lambda i: (0, 0)), row, row),
        out_shape=(SDS((s, cw), BF16), SDS((s, BA_PAD), BF16), SDS((8, cw), F32), SDS((1, LANES), F32),
                   SDS((1, LANES), F32)),
    )(qkv_pre, qkv_pre, qkv_pre, ba, dq, dq, dk, dk, dv, dv, dbg, conv_w8, alog_row, dtb_row)


def _dh_dx(dps, ws, x, mod, norm_w, dx2, ts):
    s = x.shape[0]
    widths = [w.shape[1] for w in ws]
    np_ = len(ws)

    def body(*refs):
        dp_refs, w_refs = refs[:np_], refs[np_:2 * np_]
        x_ref, mod_ref, nw_ref, dx2_ref, gx_ref, dshift, dscale, dnw = refs[2 * np_:]

        @pl.when(pl.program_id(0) == 0)
        def _():
            dshift[...] = jnp.zeros_like(dshift)
            dscale[...] = jnp.zeros_like(dscale)
            dnw[...] = jnp.zeros_like(dnw)

        dh = lax.dot_general(dp_refs[0][...], w_refs[0][...], _NT, preferred_element_type=F32)
        for a, b in zip(dp_refs[1:], w_refs[1:]):
            dh = dh + lax.dot_general(a[...], b[...], _NT, preferred_element_type=F32)
        xt = x_ref[...]
        r = lax.rsqrt(jnp.mean(xt * xt, axis=-1, keepdims=True) + EPS)
        xn = xt * r
        nw = nw_ref[...]
        sc1 = 1.0 + mod_ref[:, D_MODEL:2 * D_MODEL]
        dshift[...] += jnp.sum(dh, axis=0, keepdims=True)
        dscale[...] += jnp.sum(dh * (xn * nw), axis=0, keepdims=True)
        dnw[...] += jnp.sum(dh * sc1 * xn, axis=0, keepdims=True)
        dxn = dh * sc1 * nw
        gx_ref[...] = r * (dxn - xn * jnp.mean(dxn * xn, axis=-1, keepdims=True)) + dx2_ref[...]

    tok = lambda w: pl.BlockSpec((ts, w), lambda i: (i, 0))
    full = lambda a: pl.BlockSpec(a.shape, lambda i: (0, 0))
    row = pl.BlockSpec((1, D_MODEL), lambda i: (0, 0))
    return pl.pallas_call(
        body, name="dh_dx", grid=(s // ts,), compiler_params=_params("arbitrary"),
        in_specs=[tok(w) for w in widths] + [full(w) for w in ws] + [tok(D_MODEL), full(mod), full(norm_w),
                                                                    tok(D_MODEL)],
        out_specs=(tok(D_MODEL), row, row, row),
        out_shape=(SDS((s, D_MODEL), F32),) + (SDS((1, D_MODEL), F32),) * 3,
    )(*dps, *ws, x, mod, norm_w, dx2)


def _grad_w_in(h, dps, ts, name):
    s = h.shape[0]
    widths = [p.shape[1] for p in dps]
    np_ = len(dps)

    def body(*refs):
        h_ref, dp_refs, outs = refs[0], refs[1:1 + np_], refs[1 + np_:]

        @pl.when(pl.program_id(0) == 0)
        def _():
            for o in outs:
                o[...] = jnp.zeros_like(o)

        hb = h_ref[...]
        for p, o in zip(dp_refs, outs):
            o[...] += lax.dot_general(hb, p[...], _TN, preferred_element_type=F32)

    tok = lambda w: pl.BlockSpec((ts, w), lambda i: (i, 0))
    return pl.pallas_call(
        body, name=name, grid=(s // ts,), compiler_params=_params("arbitrary"),
        in_specs=[tok(D_MODEL)] + [tok(w) for w in widths],
        out_specs=tuple(pl.BlockSpec((D_MODEL, w), lambda i: (0, 0)) for w in widths),
        out_shape=tuple(SDS((D_MODEL, w), F32) for w in widths),
    )(h, *dps)


def _adamw_math(w, g, m, v):
    m = ADAM_B1 * m + (1.0 - ADAM_B1) * g
    v = ADAM_B2 * v + (1.0 - ADAM_B2) * (g * g)
    m_hat = m / (1.0 - ADAM_B1 ** ADAM_STEP)
    v_hat = v / (1.0 - ADAM_B2 ** ADAM_STEP)
    delta = -ADAM_LR * (m_hat / (jnp.sqrt(v_hat) + ADAM_EPS) + ADAM_WD * w)
    return delta, m, v


def _adamw(w, m, v, g, name, slots=False):
    def body(w_ref, m_ref, v_ref, g_ref, g_out, d_out, m_out, v_out):
        if slots:
            g = g_ref[0]
            for k in range(1, N_DEV):
                g = g + g_ref[k]
        else:
            g = g_ref[...]
        g_out[...] = g
        d_out[...], m_out[...], v_out[...] = _adamw_math(w_ref[...], g, m_ref[...], v_ref[...])

    return pl.pallas_call(body, name=name, compiler_params=_params(),
                          out_shape=(SDS(w.shape, F32),) * 4)(w, m, v, g)


def _adamw_w_mod(w, m, v, siluc_all, dmod_mine):
    def body(w_ref, m_ref, v_ref, sc_ref, dm_ref, g_out, d_out, m_out, v_out):
        g = _htn(sc_ref[...], dm_ref[...])
        g_out[...] = g
        d_out[...], m_out[...], v_out[...] = _adamw_math(w_ref[...], g, m_ref[...], v_ref[...])

    return pl.pallas_call(body, name="adamw_w_mod", compiler_params=_params(),
                          out_shape=(SDS(w.shape, F32),) * 4)(w, m, v, siluc_all, dmod_mine)


def _pack_sum(pack_all):
    def body(p_ref, o_ref):
        t = p_ref[0]
        for k in range(1, N_DEV):
            t = t + p_ref[k]
        o_ref[...] = t

    return pl.pallas_call(body, name="pack_sum", out_shape=SDS(pack_all.shape[1:], F32))(pack_all)


def _tile(s, want):
    t = min(want, s)
    assert s % t == 0
    return t


def _local_step(x, c, positions, w_mod_bf, b_mod, norm_w, w_in_bf, conv_w, a_log, dt_bias, dn_norm_w, at_norm_w,
                w_out_bf, final_norm_w, tgt):
    s = x.shape[0]
    o = [0]
    for wdt in IN_SPLITS:
        o.append(o[-1] + wdt)
    w_ba = jnp.pad(w_in_bf[:, o[2]:o[4]], ((0, 0), (0, BA_PAD - 2 * DN_HEADS)))
    ws = [w_in_bf[:, o[0]:o[1]], w_in_bf[:, o[1]:o[2]], w_ba, w_in_bf[:, o[4]:o[5]], w_in_bf[:, o[5]:o[6]],
          w_in_bf[:, o[6]:o[7]], w_in_bf[:, o[7]:o[8]]]
    conv_w8 = jnp.pad(conv_w, ((0, 8 - CONV_K), (0, 0)))
    alog_row = jnp.pad(a_log, ((0, 0), (DN_HEADS, BA_PAD - 2 * DN_HEADS)))
    dtb_row = jnp.pad(dt_bias, ((0, 0), (DN_HEADS, BA_PAD - 2 * DN_HEADS)))
    atw2 = jnp.concatenate([at_norm_w, at_norm_w], axis=1)

    half = AT_DIM // 2
    inv_freq = ROPE_THETA ** (-jnp.arange(half, dtype=F32) / half)
    ang = positions.astype(F32)[:, None] * inv_freq
    cos, sin = jnp.cos(ang), jnp.sin(ang)
    cos_t = jnp.concatenate([cos, cos, cos, cos], axis=1)
    sin_t = jnp.concatenate([-sin, sin, -sin, sin], axis=1)

    mod, siluc = _adaln_mod(c, w_mod_bf, b_mod)
    gate = mod[:, 2 * D_MODEL:]
    hbf, qkv_pre, z_dn, ba, qr, kr, vb, z_at = _ln_proj(x, mod, norm_w, ws, cos_t, sin_t, _tile(s, 256))
    q, k, v, bg = _dn_prep(qkv_pre, ba, conv_w8, alog_row, dtb_row, _tile(s, 256))
    u, w, qd, kd, p, gl = _dn_chunk_prep(q, k, v, bg, _tile(s, 512))
    o_dn, vn, st = _dn_scan(u, w, qd, kd, p, gl, _tile(s, 512))
    o_ps, lse_ps = [], []
    for d in DILATIONS:
        o_p, lse_p = _attn_fwd(qr, kr, vb, d)
        o_ps.append(o_p)
        lse_ps.append(lse_p)
    cat, o_at, lse = _mix_prep(o_dn, z_dn, o_ps, lse_ps, z_at, dn_norm_w, atw2, _tile(s, 512))
    dx2, dcat, gw_out, dfw, dgate, loss = _out_loss(cat, x, tgt, w_out_bf, gate, final_norm_w, _tile(s, 512))

    do_dn, dz_dn, do_at, dz_at, delta, ddnw, datw = _mix_bwd(dcat, o_dn, z_dn, o_at, z_at, dn_norm_w, atw2,
                                                             _tile(s, 512))
    dqs, dks, dvs = [], [], []
    for d in DILATIONS:
        dq_p, dk_p, dv_p = _attn_bwd(qr, kr, vb, do_at, lse, delta, d)
        dqs.append(dq_p)
        dks.append(dk_p)
        dvs.append(dv_p)
    daq, dak, dav = _rope_bwd(dqs, dks, dvs, cos_t, sin_t, _tile(s, 512))
    du, dw, dqd, dkd, dp, dgl = _dn_scan_bwd(do_dn, st, vn, w, qd, kd, p, gl, _tile(s, 512))
    dq, dk, dv, dbg = _dn_chunk_bwd(q, k, v, bg, du, dw, dqd, dkd, dp, dgl, _tile(s, 512))
    dqkv, dba, dcw, dal, ddtb = _dn_prep_bwd(qkv_pre, ba, dq, dk, dv, dbg, conv_w8, alog_row, dtb_row, _tile(s, 256))
    dps = [dqkv, dz_dn, dba, daq, dak, dav, dz_at]
    gx, dshift, dscale, dnw = _dh_dx(dps, ws, x, mod, norm_w, dx2, _tile(s, 256))
    g_qkv, g_z, g_ba = _grad_w_in(hbf, dps[:3], _tile(s, 512), "grad_w_in_dn")
    g_aq, g_ak, g_av, g_az = _grad_w_in(hbf, dps[3:], _tile(s, 512), "grad_w_in_at")
    gw_in = jnp.concatenate([g_qkv, g_z, g_ba[:, :2 * DN_HEADS], g_aq, g_ak, g_av, g_az], axis=1)
    dmod = jnp.concatenate([dshift, dscale, dgate], axis=1)
    small = dict(conv=dcw[:CONV_K], dmod=dmod, siluc=siluc, dnw=dnw, dfw=dfw, alog=dal, dtb=ddtb, dnn=ddnw, atn=datw)
    return loss, gx, gw_in, gw_out, small


def kernel(x, c, positions, w_mod, b_mod, norm_w, w_in, conv_w, a_log, dt_bias, dn_norm_w, at_norm_w, w_out, final_norm_w, loss_target, m_w_mod, m_b_mod, m_norm_w, m_w_in, m_conv_w, m_a_log, m_dt_bias, m_dn_norm_w, m_at_norm_w, m_w_out, m_final_norm_w, v_w_mod, v_b_mod, v_norm_w, v_w_in, v_conv_w, v_a_log, v_dt_bias, v_dn_norm_w, v_at_norm_w, v_w_out, v_final_norm_w):
    me = 4 * lax.axis_index("x") + 2 * lax.axis_index("y") + lax.axis_index("c")
    s = x.shape[1]

    g_mod, g_in, g_conv, g_out = _exchange(
        [_bf(w_mod[0]), _bf(w_in[0]), conv_w[0], _bf(w_out[0])], [False] * 4, "gather_weights")
    w_mod_bf = g_mod.transpose(1, 0, 2).reshape(D_MODEL, 3 * D_MODEL)
    w_in_bf = g_in.transpose(1, 0, 2).reshape(D_MODEL, IN_COLS)
    conv_full = g_conv.transpose(1, 0, 2).reshape(CONV_K, 3 * DN_WIDTH)
    w_out_bf = g_out.reshape(D_MODEL, D_MODEL)

    loss, gx, gw_in, gw_out, small = _local_step(
        x[0], c, positions[0], w_mod_bf, b_mod, norm_w, w_in_bf, conv_full, a_log, dt_bias, dn_norm_w, at_norm_w,
        w_out_bf, final_norm_w.reshape(1, D_MODEL), loss_target[0])

    pack = jnp.concatenate([small["conv"].reshape(1, -1), small["dmod"], small["siluc"], small["dnw"], small["dfw"],
                            small["alog"], small["dtb"], small["dnn"], small["atn"]], axis=1).reshape(PK_ROWS, LANES)
    gw_in_slabs = gw_in.reshape(D_MODEL, N_DEV, IN_SHARD).transpose(1, 0, 2)
    gw_out_slabs = gw_out.reshape(N_DEV, D_MODEL // N_DEV, D_MODEL)
    r_in, r_out, pack_all = _exchange([gw_in_slabs, gw_out_slabs, pack], [True, True, False], "exchange_grads")

    res = {}
    res["w_in"] = _adamw(w_in[0], m_w_in[0], v_w_in[0], r_in, "adamw_w_in", slots=True)
    res["w_out"] = _adamw(w_out[0], m_w_out[0], v_w_out[0], r_out, "adamw_w_out", slots=True)
    flat_all = pack_all.reshape(N_DEV, PK_END)
    dmod_mine = lax.dynamic_slice(flat_all, (0, PK_DMOD + me * (3 * D_MODEL // N_DEV)), (N_DEV, 3 * D_MODEL // N_DEV))
    res["w_mod"] = _adamw_w_mod(w_mod[0], m_w_mod[0], v_w_mod[0], flat_all[:, PK_SILUC:PK_DNW], dmod_mine)
    tot = _pack_sum(pack_all).reshape(1, PK_END)
    g_conv_full = tot[:, PK_CONV:PK_DMOD].reshape(CONV_K, 3 * DN_WIDTH)
    g_conv_mine = lax.dynamic_slice(g_conv_full, (0, me * (3 * DN_WIDTH // N_DEV)), (CONV_K, 3 * DN_WIDTH // N_DEV))
    res["conv_w"] = _adamw(conv_w[0], m_conv_w[0], v_conv_w[0], g_conv_mine, "adamw_conv_w")
    res["b_mod"] = _adamw(b_mod, m_b_mod, v_b_mod, tot[:, PK_DMOD:PK_SILUC], "adamw_b_mod")
    res["norm_w"] = _adamw(norm_w, m_norm_w, v_norm_w, tot[:, PK_DNW:PK_DFW], "adamw_norm_w")
    res["a_log"] = _adamw(a_log, m_a_log, v_a_log, tot[:, PK_ALOG + DN_HEADS:PK_ALOG + 2 * DN_HEADS], "adamw_a_log")
    res["dt_bias"] = _adamw(dt_bias, m_dt_bias, v_dt_bias, tot[:, PK_DTB + DN_HEADS:PK_DTB + 2 * DN_HEADS],
                            "adamw_dt_bias")
    res["dn_norm_w"] = _adamw(dn_norm_w, m_dn_norm_w, v_dn_norm_w, tot[:, PK_DNN:PK_ATN], "adamw_dn_norm_w")
    g_atn = tot[:, PK_ATN:PK_ATN + AT_DIM] + tot[:, PK_ATN + AT_DIM:PK_END]
    res["at_norm_w"] = _adamw(at_norm_w, m_at_norm_w, v_at_norm_w, g_atn, "adamw_at_norm_w")
    fin = _adamw(final_norm_w.reshape(1, D_MODEL), m_final_norm_w.reshape(1, D_MODEL),
                 v_final_norm_w.reshape(1, D_MODEL), tot[:, PK_DFW:PK_ALOG], "adamw_final_norm_w")
    res["final_norm_w"] = tuple(a.reshape(D_MODEL) for a in fin)

    lead = ("w_mod", "w_in", "conv_w", "w_out")
    names = ("w_mod", "b_mod", "norm_w", "w_in", "conv_w", "a_log", "dt_bias", "dn_norm_w", "at_norm_w", "w_out",
             "final_norm_w")
    out = [lax.psum(loss[0, 0], ("x", "y", "c")), gx.reshape(1, s, D_MODEL)]
    for kind in range(4):
        for nm in names:
            a = res[nm][kind]
            out.append(a[None] if nm in lead else a)
    return tuple(out)
```

```python
import functools

import jax
import jax.numpy as jnp
from jax import lax
from jax.experimental import pallas as pl
from jax.experimental.pallas import tpu as pltpu

F32, BF16 = jnp.float32, jnp.bfloat16
HI = lax.Precision.HIGHEST
SDS = jax.ShapeDtypeStruct

D_MODEL = 1024
DN_HEADS, DN_DIM, DN_WIDTH = 4, 128, 512
AT_HEADS, AT_DIM, AT_WIDTH = 8, 64, 512
CONV_K = 4
CHUNK = 64
Q_BLOCK = 128
W_SUB = 128
DILATIONS = (1, 4, 16)
ROPE_THETA = 10000.0
EPS = 1e-6
N_DEV = 8
LANES = 128
BA_PAD = 128
IN_SPLITS = (1536, 512, 4, 4, 512, 512, 512, 512)
IN_COLS = sum(IN_SPLITS)
IN_SHARD = IN_COLS // N_DEV
VMEM_LIMIT = 56 * 2 ** 20

ADAM_LR, ADAM_B1, ADAM_B2, ADAM_EPS, ADAM_WD, ADAM_STEP = 0.001, 0.9, 0.999, 1e-08, 0.01, 10

PK_CONV, PK_DMOD, PK_SILUC, PK_DNW, PK_DFW, PK_ALOG, PK_DTB, PK_DNN, PK_ATN, PK_END = (
    0, 6144, 9216, 10240, 11264, 12288, 12416, 12544, 12672, 12800)
PK_ROWS = PK_END // LANES

_NT = (((1,), (1,)), ((), ()))
_TN = (((0,), (0,)), ((), ()))


def _params(*sem):
    return pltpu.CompilerParams(dimension_semantics=sem or None, vmem_limit_bytes=VMEM_LIMIT)


def _bf(x):
    return x.astype(BF16)


def _nn(a, b):
    return jnp.dot(_bf(a), _bf(b), preferred_element_type=F32)


def _nt(a, b):
    return lax.dot_general(_bf(a), _bf(b), _NT, preferred_element_type=F32)


def _tn(a, b):
    return lax.dot_general(_bf(a), _bf(b), _TN, preferred_element_type=F32)


def _hnn(a, b):
    return jnp.dot(a, b, precision=HI, preferred_element_type=F32)


def _hnt(a, b):
    return lax.dot_general(a, b, _NT, precision=HI, preferred_element_type=F32)


def _htn(a, b):
    return lax.dot_general(a, b, _TN, precision=HI, preferred_element_type=F32)


@jax.custom_vjp
def _d_hnn(a, b):
    return _hnn(a, b)


def _d_hnn_fwd(a, b):
    return _hnn(a, b), (a, b)


def _d_hnn_bwd(res, g):
    a, b = res
    return _hnt(g, b), _htn(a, g)


_d_hnn.defvjp(_d_hnn_fwd, _d_hnn_bwd)


def _silu(x):
    return x * jax.nn.sigmoid(x)


def _softplus(x):
    return jnp.maximum(x, 0.0) + jnp.log(1.0 + jnp.exp(-jnp.abs(x)))


def _l2n(x):
    return x * lax.rsqrt(jnp.sum(x * x, axis=-1, keepdims=True) + EPS)


def _post_q(x):
    return _l2n(_silu(x)) * (DN_DIM ** -0.5)


def _post_k(x):
    return _l2n(_silu(x))


def _post_v(x):
    return _silu(x)


def _beta_decay(ba, alog_row, dtb_row):
    lane = lax.broadcasted_iota(jnp.int32, ba.shape, 1)
    return jnp.where(lane < DN_HEADS, jax.nn.sigmoid(ba), -jnp.exp(alog_row) * _softplus(ba + dtb_row))


def _gate_dn(o, z, w):
    return (o * lax.rsqrt(jnp.mean(o * o, axis=-1, keepdims=True) + EPS)) * w * _silu(z)


def _group_ones(scale):
    r = lax.broadcasted_iota(jnp.int32, (LANES, LANES), 0)
    c = lax.broadcasted_iota(jnp.int32, (LANES, LANES), 1)
    return jnp.where((r // AT_DIM) == (c // AT_DIM), scale, 0.0).astype(F32)


def _gate_at(o, z, w2, hnn):
    ms = hnn(o * o, _group_ones(1.0 / AT_DIM))
    return (o * lax.rsqrt(ms + EPS)) * w2 * _silu(z)


def _swap_half64(x):
    lane = lax.broadcasted_iota(jnp.int32, x.shape, 1)
    return jnp.where((lane & (AT_DIM - 1)) < AT_DIM // 2, pltpu.roll(x, LANES - AT_DIM // 2, 1),
                     pltpu.roll(x, AT_DIM // 2, 1))


_NN = (((1,), (0,)), ((), ()))


def _hl(a):
    hi = a.astype(BF16)
    return hi, (a - hi.astype(F32)).astype(BF16)


def _mm3(a, b, dims=_NN):
    (ah, al), (bh, bl) = a, b
    f = lambda x, y: lax.dot_general(x, y, dims, preferred_element_type=F32)
    return f(ah, bh) + (f(ah, bl) + f(al, bh))


def _chunk_masks():
    r = lax.broadcasted_iota(jnp.int32, (CHUNK, CHUNK), 0)
    c = lax.broadcasted_iota(jnp.int32, (CHUNK, CHUNK), 1)
    return r >= c, r > c, (r == c).astype(F32), (r // 16) == (c // 16)


def _tri_inv(mats):
    _, _, eye, blk = _chunk_masks()
    dg = [jnp.where(blk, a, 0.0) for a in mats]
    lo = [jnp.where(blk, 0.0, a) for a in mats]
    sdg = [_hl(x) for x in dg]
    d2 = [_mm3(s, s) for s in sdg]
    sd2 = [_hl(x) for x in d2]
    d4 = [_mm3(s, s) for s in sd2]
    sd4 = [_hl(x) for x in d4]
    d8 = [_mm3(s, s) for s in sd4]
    p1 = [_mm3(_hl(eye - a), _hl(eye + b)) for a, b in zip(dg, d2)]
    p2 = [_mm3(_hl(a), _hl(eye + b)) for a, b in zip(p1, d4)]
    dinv = [_mm3(_hl(a), _hl(eye + b)) for a, b in zip(p2, d8)]
    sdinv = [_hl(x) for x in dinv]
    n1 = [_mm3(s, _hl(b)) for s, b in zip(sdinv, lo)]
    sn1 = [_hl(x) for x in n1]
    n2 = [_mm3(s, s) for s in sn1]
    q1 = [_mm3(_hl(eye - a), _hl(eye + b)) for a, b in zip(n1, n2)]
    return [_mm3(_hl(a), s) for a, s in zip(q1, sdinv)]


def _chunk_common(qs, ks, vs, betas, gcs):
    tril, _, _, _ = _chunk_masks()
    out = []
    for q, k, v, beta, gc in zip(qs, ks, vs, betas, gcs):
        gb = jnp.broadcast_to(gc, (CHUNK, DN_DIM))
        gt = gb.T[:CHUNK, :]
        gam = jnp.where(tril, jnp.exp(jnp.where(tril, gb[:, :CHUNK] - gt, 0.0)), 0.0)
        last = gb[CHUNK - 1:CHUNK, :]
        eg, e2 = jnp.exp(gb), jnp.exp(last - gb)
        kb, vb = k * beta, v * beta
        out.append(dict(gam=gam, eg=eg, e2=e2, gl=jnp.exp(last[:, 0:1]), kb=kb, vb=vb, kbg=kb * eg,
                        m=_nt(kb, k), qk=_nt(q, k)))
    return out


def _chunk_fwd(qs, ks, vs, betas, gcs):
    tril, strict, _, _ = _chunk_masks()
    cm = _chunk_common(qs, ks, vs, betas, gcs)
    ts = _tri_inv([jnp.where(strict, c["m"] * c["gam"], 0.0) for c in cm])
    outs = []
    for q, k, c, t in zip(qs, ks, cm, ts):
        uw = _mm3(_hl(t), _hl(jnp.concatenate([c["vb"], c["kbg"]], axis=1)))
        p = jnp.where(tril, c["qk"] * c["gam"], 0.0)
        outs.append((uw[:, :DN_DIM], uw[:, DN_DIM:], p, q * c["eg"], k * c["e2"], c["gl"], t))
    return outs


def _chunk_bwd(qs, ks, vs, betas, gcs, ts, cots):
    tril, strict, _, _ = _chunk_masks()
    cm = _chunk_common(qs, ks, vs, betas, gcs)
    row = lax.broadcasted_iota(jnp.int32, (CHUNK, 1), 0)
    ones = jnp.ones((CHUNK, DN_DIM), BF16)
    rs = lambda x: jnp.sum(x, axis=-1, keepdims=True)
    sts = [_hl(t) for t in ts]
    duw = [_hl(jnp.concatenate([ct[0], ct[1]], axis=1)) for ct in cots]
    dts = [_mm3(a, _hl(jnp.concatenate([c["vb"], c["kbg"]], axis=1)), _NT) for a, c in zip(duw, cm)]
    xs = [_mm3(s, _hl(d), _TN) for s, d in zip(sts, dts)]
    das = [jnp.where(strict, -_mm3(_hl(x), s, _NT), 0.0) for x, s in zip(xs, sts)]
    dvks = [_mm3(s, a, _TN) for s, a in zip(sts, duw)]
    outs = []
    for q, k, v, beta, c, ct, da, dvk in zip(qs, ks, vs, betas, cm, cots, das, dvks):
        _, _, dp, dqd, dkd, dgl = ct
        dvb, dkbg = dvk[:, :DN_DIM], dvk[:, DN_DIM:]
        dm = da * c["gam"]
        dqk = jnp.where(tril, dp, 0.0) * c["gam"]
        e = dm * c["m"] + dqk * c["qk"]
        dmq = jnp.concatenate([dm, dqk], axis=0)
        r1 = _nn(dmq, k)
        dkb = r1[:CHUNK] + dkbg * c["eg"]
        dq = r1[CHUNK:] + dqd * c["eg"]
        dk = _tn(dmq, jnp.concatenate([c["kb"], q], axis=0)) + dkd * c["e2"] + dkb * beta
        dbeta = rs(dkb * k) + rs(dvb * v)
        eh, el = _hl(e)
        colsum = (lax.dot_general(eh, ones, _TN, preferred_element_type=F32)
                  + lax.dot_general(el, ones, _TN, preferred_element_type=F32))[:, 0:1]
        rs_kd = rs(dkd * (k * c["e2"]))
        dgc = rs(e) - colsum + rs(dqd * q * c["eg"]) + rs(dkbg * c["kbg"]) - rs_kd
        tail = jnp.sum(rs_kd, axis=0, keepdims=True) + dgl * c["gl"]
        dgc = dgc + jnp.where(row == CHUNK - 1, tail, 0.0)
        outs.append((dq, dk, dvb * beta, dbeta, dgc))
    return outs


def _chunk_cumsum(x, reverse=False):
    n = x.shape[0]
    pos = lax.broadcasted_iota(jnp.int32, x.shape, 0) & (CHUNK - 1)
    sh = 1
    while sh < CHUNK:
        if reverse:
            x = x + jnp.where(pos < CHUNK - sh, pltpu.roll(x, n - sh, 0), 0.0)
        else:
            x = x + jnp.where(pos >= sh, pltpu.roll(x, sh, 0), 0.0)
        sh *= 2
    return x


GC_LANE = 2 * DN_HEADS


def _exchange(arrays, scatter, name):
    n = len(arrays)
    out_shapes = []
    for a, sc in zip(arrays, scatter):
        out_shapes.append(SDS(a.shape if sc else (N_DEV,) + a.shape, a.dtype))

    def body(*refs):
        ins, outs = refs[:n], refs[n:2 * n]
        send_sems, recv_sems, loc_sems = refs[2 * n:]
        x, y, c = lax.axis_index("x"), lax.axis_index("y"), lax.axis_index("c")
        me = 4 * x + 2 * y + c
        local, remote = [], []
        for i in range(n):
            src = ins[i].at[me] if scatter[i] else ins[i]
            cp = pltpu.make_async_copy(src, outs[i].at[me], loc_sems.at[i])
            cp.start()
            local.append(cp)
        for dlt in range(1, N_DEV):
            px = 1 - x if dlt & 4 else x
            py = 1 - y if dlt & 2 else y
            pc = 1 - c if dlt & 1 else c
            peer = 4 * px + 2 * py + pc
            for i in range(n):
                src = ins[i].at[peer] if scatter[i] else ins[i]
                cp = pltpu.make_async_remote_copy(
                    src_ref=src, dst_ref=outs[i].at[me],
                    send_sem=send_sems.at[i, dlt - 1], recv_sem=recv_sems.at[i, dlt - 1],
                    device_id=(px, py, pc), device_id_type=pl.DeviceIdType.MESH)
                cp.start()
                arrive = pltpu.make_async_remote_copy(
                    src_ref=src, dst_ref=outs[i].at[peer],
                    send_sem=send_sems.at[i, dlt - 1], recv_sem=recv_sems.at[i, dlt - 1],
                    device_id=(px, py, pc), device_id_type=pl.DeviceIdType.MESH)
                remote.append((cp, arrive))
        for cp, arrive in remote:
            cp.wait_send()
            arrive.wait_recv()
        for cp in local:
            cp.wait()

    any_spec = pl.BlockSpec(memory_space=pl.ANY)
    return pl.pallas_call(
        body, name=name, out_shape=tuple(out_shapes),
        in_specs=[any_spec] * n, out_specs=tuple([any_spec] * n),
        scratch_shapes=[pltpu.SemaphoreType.DMA((n, N_DEV - 1)), pltpu.SemaphoreType.DMA((n, N_DEV - 1)),
                        pltpu.SemaphoreType.DMA((n,))],
    )(*arrays)


def _adaln_mod(c, w_mod, b_mod):
    def body(c_ref, w_ref, b_ref, mod_ref, sc_ref):
        sc = _silu(c_ref[...])
        sc8 = jnp.broadcast_to(sc, (8, D_MODEL))
        mod_ref[...] = _nn(sc8, w_ref[...])[0:1] + b_ref[...]
        sc_ref[...] = sc

    return pl.pallas_call(body, name="adaln_mod", compiler_params=_params(),
                          out_shape=(SDS((1, 3 * D_MODEL), F32), SDS((1, D_MODEL), F32)))(c, w_mod, b_mod)


def _ln_proj(x, mod, norm_w, ws, cos_t, sin_t, ts):
    s = x.shape[0]
    widths = [w.shape[1] for w in ws]

    def body(x_ref, mod_ref, nw_ref, cos_ref, sin_ref, wqkv, wz, wba, waq, wak, wav, waz,
             h_ref, oqkv, oz, oba, oq, ok, ov, oaz):
        xt = x_ref[...]
        r = lax.rsqrt(jnp.mean(xt * xt, axis=-1, keepdims=True) + EPS)
        shift, scale = mod_ref[:, 0:D_MODEL], mod_ref[:, D_MODEL:2 * D_MODEL]
        h = ((xt * r) * nw_ref[...]) * (1.0 + scale) + shift
        hb = _bf(h)
        h_ref[...] = hb
        oqkv[...] = jnp.dot(hb, wqkv[...], preferred_element_type=F32)
        oz[...] = jnp.dot(hb, wz[...], preferred_element_type=F32)
        oba[...] = jnp.dot(hb, wba[...], preferred_element_type=F32)
        oaz[...] = jnp.dot(hb, waz[...], preferred_element_type=F32)
        ov[...] = _bf(jnp.dot(hb, wav[...], preferred_element_type=F32))
        cs, sn = cos_ref[...], sin_ref[...]
        for w_ref, o_ref in ((waq, oq), (wak, ok)):
            t = jnp.dot(hb, w_ref[...], preferred_element_type=F32)
            for j in range(AT_WIDTH // LANES):
                cols = slice(j * LANES, (j + 1) * LANES)
                tj = t[:, cols]
                o_ref[:, cols] = _bf(tj * cs + _swap_half64(tj) * sn)

    tok = lambda w: pl.BlockSpec((ts, w), lambda i: (i, 0))
    full = lambda a: pl.BlockSpec(a.shape, lambda i: (0, 0))
    return pl.pallas_call(
        body, name="ln_proj", grid=(s // ts,), compiler_params=_params("arbitrary"),
        in_specs=[tok(D_MODEL), full(mod), full(norm_w), tok(LANES), tok(LANES)] + [full(w) for w in ws],
        out_specs=(tok(D_MODEL), tok(widths[0]), tok(widths[1]), tok(widths[2]), tok(widths[3]), tok(widths[4]),
                   tok(widths[5]), tok(widths[6])),
        out_shape=(SDS((s, D_MODEL), BF16), SDS((s, widths[0]), F32), SDS((s, widths[1]), F32),
                   SDS((s, widths[2]), F32), SDS((s, widths[3]), BF16), SDS((s, widths[4]), BF16),
                   SDS((s, widths[5]), BF16), SDS((s, widths[6]), F32)),
    )(x, mod, norm_w, cos_t, sin_t, *ws)


def _conv_taps(ext, rows):
    taps = []
    for j in range(CONV_K):
        sh = CONV_K - 1 - j
        rolled = pltpu.roll(ext, sh, 0) if sh else ext
        taps.append(rolled[8:8 + rows])
    return taps


def _dn_prep(qkv_pre, ba, conv_w8, alog_row, dtb_row, ts):
    s = qkv_pre.shape[0]
    cw = 3 * DN_WIDTH

    def body(pre_ref, halo_ref, ba_ref, cw_ref, al_ref, dtb_ref, q_ref, k_ref, v_ref, bg_ref):
        n = pl.program_id(0)
        prev = jnp.where(n == 0, 0.0, halo_ref[...])
        ext = jnp.concatenate([prev, pre_ref[...]], axis=0)
        taps = _conv_taps(ext, ts)
        conv = taps[0] * cw_ref[0:1, :]
        for j in range(1, CONV_K):
            conv = conv + taps[j] * cw_ref[j:j + 1, :]
        for h in range(DN_HEADS):
            cols = slice(h * DN_DIM, (h + 1) * DN_DIM)
            q_ref[:, cols] = _post_q(conv[:, h * DN_DIM:(h + 1) * DN_DIM])
            k_ref[:, cols] = _post_k(conv[:, DN_WIDTH + h * DN_DIM:DN_WIDTH + (h + 1) * DN_DIM])
            v_ref[:, cols] = _post_v(conv[:, 2 * DN_WIDTH + h * DN_DIM:2 * DN_WIDTH + (h + 1) * DN_DIM])
        bg = _beta_decay(ba_ref[...], al_ref[...], dtb_ref[...])
        lane = lax.broadcasted_iota(jnp.int32, bg.shape, 1)
        run = pltpu.roll(_chunk_cumsum(bg), DN_HEADS, 1)
        bg_ref[...] = jnp.where((lane >= GC_LANE) & (lane < GC_LANE + DN_HEADS), run, bg)

    tok = lambda w: pl.BlockSpec((ts, w), lambda i: (i, 0))
    full = lambda a: pl.BlockSpec(a.shape, lambda i: (0, 0))
    halo = pl.BlockSpec((8, cw), lambda i: (jnp.maximum(i * (ts // 8) - 1, 0), 0))
    return pl.pallas_call(
        body, name="dn_prep", grid=(s // ts,), compiler_params=_params("arbitrary"),
        in_specs=[tok(cw), halo, tok(BA_PAD), full(conv_w8), full(alog_row), full(dtb_row)],
        out_specs=(tok(DN_WIDTH), tok(DN_WIDTH), tok(DN_WIDTH), tok(BA_PAD)),
        out_shape=(SDS((s, DN_WIDTH), F32),) * 3 + (SDS((s, BA_PAD), F32),),
    )(qkv_pre, qkv_pre, ba, conv_w8, alog_row, dtb_row)


def _dn_chunk_prep(q, k, v, bg, ts):
    s = q.shape[0]
    ncs = ts // CHUNK

    def body(q_ref, k_ref, v_ref, bg_ref, u_ref, w_ref, qd_ref, kd_ref, p_ref, gl_ref, t_ref):
        def chunk(ci, carry):
            rows = pl.ds(pl.multiple_of(ci * CHUNK, CHUNK), CHUNK)
            rows8 = pl.ds(pl.multiple_of(ci * 8, 8), 8)
            bgc = bg_ref[rows, :]
            hs = range(DN_HEADS)
            sl = [slice(h * DN_DIM, (h + 1) * DN_DIM) for h in hs]
            outs = _chunk_fwd([q_ref[rows, c] for c in sl], [k_ref[rows, c] for c in sl],
                              [v_ref[rows, c] for c in sl], [bgc[:, h:h + 1] for h in hs],
                              [bgc[:, GC_LANE + h:GC_LANE + h + 1] for h in hs])
            for h, (u, w, p, qd, kd, gl, t) in enumerate(outs):
                u_ref[rows, sl[h]] = u
                w_ref[rows, sl[h]] = w
                qd_ref[rows, sl[h]] = qd
                kd_ref[rows, sl[h]] = kd
                p_ref[h, rows, :] = p
                t_ref[h, rows, :] = t
                gl_ref[rows8, sl[h]] = jnp.broadcast_to(gl, (8, DN_DIM))
            return carry

        lax.fori_loop(0, ncs, chunk, 0)

    tok = lambda w: pl.BlockSpec((ts, w), lambda i: (i, 0))
    sq = pl.BlockSpec((DN_HEADS, ts, CHUNK), lambda i: (0, i, 0))
    return pl.pallas_call(
        body, name="dn_chunk_prep", grid=(s // ts,), compiler_params=_params("arbitrary"),
        in_specs=[tok(DN_WIDTH)] * 3 + [tok(BA_PAD)],
        out_specs=(tok(DN_WIDTH),) * 4 + (sq, pl.BlockSpec((ncs * 8, DN_WIDTH), lambda i: (i, 0)), sq),
        out_shape=(SDS((s, DN_WIDTH), F32),) * 4 + (SDS((DN_HEADS, s, CHUNK), F32),
                                                     SDS((s // CHUNK * 8, DN_WIDTH), F32),
                                                     SDS((DN_HEADS, s, CHUNK), F32)),
    )(q, k, v, bg)


def _dn_scan(u, w, qd, kd, p, gl, ts):
    s = u.shape[0]
    ncs = ts // CHUNK

    def body(u_ref, w_ref, qd_ref, kd_ref, p_ref, gl_ref, o_ref, vn_ref, st_ref, state):
        @pl.when(pl.program_id(0) == 0)
        def _():
            state[...] = jnp.zeros_like(state)

        def chunk(ci, carry):
            rows = pl.ds(pl.multiple_of(ci * CHUNK, CHUNK), CHUNK)
            rows8 = pl.ds(pl.multiple_of(ci * 8, 8), 8)
            srows = pl.ds(pl.multiple_of(ci * DN_DIM, DN_DIM), DN_DIM)
            for h in range(DN_HEADS):
                cols = slice(h * DN_DIM, (h + 1) * DN_DIM)
                sf = state[h]
                st_ref[srows, cols] = sf
                vn = u_ref[rows, cols] - _nn(w_ref[rows, cols], sf)
                o_ref[rows, cols] = _nn(qd_ref[rows, cols], sf) + _nn(p_ref[h, rows, :], vn)
                vn_ref[rows, cols] = vn
                state[h] = sf * gl_ref[rows8, cols][0:1] + _tn(kd_ref[rows, cols], vn)
            return carry

        lax.fori_loop(0, ncs, chunk, 0)

    tok = lambda wd: pl.BlockSpec((ts, wd), lambda i: (i, 0))
    return pl.pallas_call(
        body, name="dn_scan", grid=(s // ts,), compiler_params=_params("arbitrary"),
        in_specs=[tok(DN_WIDTH)] * 4 + [pl.BlockSpec((DN_HEADS, ts, CHUNK), lambda i: (0, i, 0)),
                                        pl.BlockSpec((ncs * 8, DN_WIDTH), lambda i: (i, 0))],
        out_specs=(tok(DN_WIDTH), tok(DN_WIDTH), pl.BlockSpec((ncs * DN_DIM, DN_WIDTH), lambda i: (i, 0))),
        out_shape=(SDS((s, DN_WIDTH), F32), SDS((s, DN_WIDTH), F32), SDS((s // CHUNK * DN_DIM, DN_WIDTH), F32)),
        scratch_shapes=[pltpu.VMEM((DN_HEADS, DN_DIM, DN_DIM), F32)],
    )(u, w, qd, kd, p, gl)


def _attn_mask(n, valid=None):
    qi = lax.broadcasted_iota(jnp.int32, (Q_BLOCK, 2 * Q_BLOCK), 0)
    kj = lax.broadcasted_iota(jnp.int32, (Q_BLOCK, 2 * Q_BLOCK), 1)
    rel = Q_BLOCK + qi - kj
    mask = (rel >= 0) & (rel <= W_SUB) & ((kj >= Q_BLOCK) | (n > 0))
    return mask if valid is None else mask & valid


def _attn_specs(d, nb):
    cur = pl.BlockSpec((Q_BLOCK, AT_WIDTH), lambda r, n: (jnp.minimum(n, nb - 1), r))
    prev = pl.BlockSpec((Q_BLOCK, AT_WIDTH), lambda r, n: (jnp.maximum(jnp.minimum(n, nb - 1) - 1, 0), r))
    return cur, prev


def _attn_fwd(qr, kr, vb, d):
    s = qr.shape[0]
    sub = s // d
    nb = sub // Q_BLOCK
    scale = AT_DIM ** -0.5

    def body(q_ref, kc_ref, kp_ref, vc_ref, vp_ref, o_ref, lse_ref):
        mask = _attn_mask(pl.program_id(1))
        lo = lax.broadcasted_iota(jnp.int32, (Q_BLOCK, LANES), 1) < AT_DIM
        for j in range(AT_WIDTH // LANES):
            cols = slice(j * LANES, (j + 1) * LANES)
            q = q_ref[:, cols]
            kk = jnp.concatenate([kp_ref[:, cols], kc_ref[:, cols]], axis=0)
            vv = jnp.concatenate([vp_ref[:, cols], vc_ref[:, cols]], axis=0)
            outs, lses = [], []
            for sel in (lo, ~lo):
                qm = jnp.where(sel, q, jnp.zeros_like(q))
                sc = lax.dot_general(qm, kk, _NT, preferred_element_type=F32) * scale
                sc = jnp.where(mask, sc, -1e30)
                m = jnp.max(sc, axis=-1, keepdims=True)
                pr = jnp.exp(sc - m)
                l = jnp.sum(pr, axis=-1, keepdims=True)
                outs.append(jnp.dot(_bf(pr), vv, preferred_element_type=F32) / l)
                lses.append(m + jnp.log(l))
            o_ref[:, cols] = jnp.where(lo, outs[0], outs[1])
            lse_ref[:, cols] = jnp.where(lo, lses[0], lses[1])

    cur, prev = _attn_specs(d, nb)
    view = lambda a: a.reshape(sub, d * AT_WIDTH)
    o, lse = pl.pallas_call(
        body, name=f"attn_fwd_d{d}", grid=(d, nb), compiler_params=_params("arbitrary", "arbitrary"),
        in_specs=[cur, cur, prev, cur, prev], out_specs=(cur, cur),
        out_shape=(SDS((sub, d * AT_WIDTH), F32),) * 2,
    )(view(qr), view(kr), view(kr), view(vb), view(vb))
    return o.reshape(s, AT_WIDTH), lse.reshape(s, AT_WIDTH)


def _mix_prep(o_dn, z_dn, o_ps, lse_ps, z_at, dnw, atw2, ts):
    s = o_dn.shape[0]

    def body(odn, zdn, o1, o2, o3, l1, l2, l3, zat, dnw_ref, atw_ref, cat_ref, oat_ref, lse_ref):
        for h in range(DN_HEADS):
            cols = slice(h * DN_DIM, (h + 1) * DN_DIM)
            cat_ref[:, cols] = _bf(_gate_dn(odn[:, cols], zdn[:, cols], dnw_ref[...]))
        for j in range(AT_WIDTH // LANES):
            cols = slice(j * LANES, (j + 1) * LANES)
            ls = [l1[:, cols], l2[:, cols], l3[:, cols]]
            m = jnp.maximum(jnp.maximum(ls[0], ls[1]), ls[2])
            es = [jnp.exp(l - m) for l in ls]
            den = es[0] + es[1] + es[2]
            oat = (es[0] * o1[:, cols] + es[1] * o2[:, cols] + es[2] * o3[:, cols]) / den
            oat_ref[:, cols] = oat
            lse_ref[:, cols] = m + jnp.log(den)
            cat_ref[:, DN_WIDTH + j * LANES:DN_WIDTH + (j + 1) * LANES] = _bf(
                _gate_at(oat, zat[:, cols], atw_ref[...], _hnn))

    tok = lambda w: pl.BlockSpec((ts, w), lambda i: (i, 0))
    full = lambda a: pl.BlockSpec(a.shape, lambda i: (0, 0))
    return pl.pallas_call(
        body, name="mix_prep", grid=(s // ts,), compiler_params=_params("arbitrary"),
        in_specs=[tok(DN_WIDTH)] * 9 + [full(dnw), full(atw2)],
        out_specs=(tok(D_MODEL), tok(AT_WIDTH), tok(AT_WIDTH)),
        out_shape=(SDS((s, D_MODEL), BF16), SDS((s, AT_WIDTH), F32), SDS((s, AT_WIDTH), F32)),
    )(o_dn, z_dn, *o_ps, *lse_ps, z_at, dnw, atw2)


def _out_loss(cat, x, tgt, w_out, gate, fw, ts):
    s = x.shape[0]

    def body(cat_ref, x_ref, t_ref, w_ref, g_ref, fw_ref, dx2_ref, dcat_ref, gw_ref, dfw_ref, dgate_ref, loss_ref):
        @pl.when(pl.program_id(0) == 0)
        def _():
            gw_ref[...] = jnp.zeros_like(gw_ref)
            dfw_ref[...] = jnp.zeros_like(dfw_ref)
            dgate_ref[...] = jnp.zeros_like(dgate_ref)
            loss_ref[...] = jnp.zeros_like(loss_ref)

        catb = cat_ref[...]
        wb = w_ref[...]
        gate, fwv = g_ref[...], fw_ref[...]
        mix = jnp.dot(catb, wb, preferred_element_type=F32)
        x2 = x_ref[...] + gate * mix
        r2 = lax.rsqrt(jnp.mean(x2 * x2, axis=-1, keepdims=True) + EPS)
        xn2 = x2 * r2
        err = xn2 * fwv - t_ref[...]
        row = jnp.sum(err * err, axis=-1, keepdims=True) * (1.0 / D_MODEL)
        loss_ref[...] += 0.5 * jnp.sum(row, axis=0, keepdims=True)
        dy = err * (1.0 / D_MODEL)
        dfw_ref[...] += jnp.sum(dy * xn2, axis=0, keepdims=True)
        dxn = dy * fwv
        dx2 = r2 * (dxn - xn2 * jnp.mean(dxn * xn2, axis=-1, keepdims=True))
        dx2_ref[...] = dx2
        dgate_ref[...] += jnp.sum(dx2 * mix, axis=0, keepdims=True)
        dmix = _bf(gate * dx2)
        dcat_ref[...] = lax.dot_general(dmix, wb, _NT, preferred_element_type=F32)
        gw_ref[...] += lax.dot_general(catb, dmix, _TN, preferred_element_type=F32)

    tok = lambda w: pl.BlockSpec((ts, w), lambda i: (i, 0))
    full = lambda a: pl.BlockSpec(a.shape, lambda i: (0, 0))
    row = pl.BlockSpec((1, D_MODEL), lambda i: (0, 0))
    return pl.pallas_call(
        body, name="out_loss", grid=(s // ts,), compiler_params=_params("arbitrary"),
        in_specs=[tok(D_MODEL), tok(D_MODEL), tok(D_MODEL), full(w_out), full(gate), full(fw)],
        out_specs=(tok(D_MODEL), tok(D_MODEL), pl.BlockSpec((D_MODEL, D_MODEL), lambda i: (0, 0)), row, row,
                   pl.BlockSpec((1, 1), lambda i: (0, 0))),
        out_shape=(SDS((s, D_MODEL), F32), SDS((s, D_MODEL), F32), SDS((D_MODEL, D_MODEL), F32),
                   SDS((1, D_MODEL), F32), SDS((1, D_MODEL), F32), SDS((1, 1), F32)),
    )(cat, x, tgt, w_out, gate, fw)


def _mix_bwd(dcat, o_dn, z_dn, o_at, z_at, dnw, atw2, ts):
    s = dcat.shape[0]

    def body(dcat_ref, odn, zdn, oat, zat, dnw_ref, atw_ref, dodn, dzdn, doat, dzat, delta, ddnw, datw):
        @pl.when(pl.program_id(0) == 0)
        def _():
            ddnw[...] = jnp.zeros_like(ddnw)
            datw[...] = jnp.zeros_like(datw)

        for h in range(DN_HEADS):
            cols = slice(h * DN_DIM, (h + 1) * DN_DIM)
            _, vjp = jax.vjp(_gate_dn, odn[:, cols], zdn[:, cols], dnw_ref[...])
            do, dz, dw = vjp(dcat_ref[:, cols])
            dodn[:, cols] = do
            dzdn[:, cols] = _bf(dz)
            ddnw[...] += dw
        for j in range(AT_WIDTH // LANES):
            cols = slice(j * LANES, (j + 1) * LANES)
            o = oat[:, cols]
            _, vjp = jax.vjp(functools.partial(_gate_at, hnn=_d_hnn), o, zat[:, cols], atw_ref[...])
            do, dz, dw = vjp(dcat_ref[:, DN_WIDTH + j * LANES:DN_WIDTH + (j + 1) * LANES])
            doat[:, cols] = do
            dzat[:, cols] = _bf(dz)
            datw[...] += dw
            delta[:, cols] = _hnn(do * o, _group_ones(1.0))

    tok = lambda w: pl.BlockSpec((ts, w), lambda i: (i, 0))
    full = lambda a: pl.BlockSpec(a.shape, lambda i: (0, 0))
    row = pl.BlockSpec((1, LANES), lambda i: (0, 0))
    return pl.pallas_call(
        body, name="mix_bwd", grid=(s // ts,), compiler_params=_params("arbitrary"),
        in_specs=[tok(D_MODEL)] + [tok(DN_WIDTH)] * 4 + [full(dnw), full(atw2)],
        out_specs=(tok(DN_WIDTH),) * 5 + (row, row),
        out_shape=(SDS((s, DN_WIDTH), F32), SDS((s, DN_WIDTH), BF16), SDS((s, AT_WIDTH), F32),
                   SDS((s, AT_WIDTH), BF16), SDS((s, AT_WIDTH), F32), SDS((1, LANES), F32), SDS((1, LANES), F32)),
    )(dcat, o_dn, z_dn, o_at, z_at, dnw, atw2)


def _attn_bwd(qr, kr, vb, do, lse, delta, d):
    s = qr.shape[0]
    sub = s // d
    nb = sub // Q_BLOCK
    scale = AT_DIM ** -0.5

    def body(q_ref, kc_ref, kp_ref, vc_ref, vp_ref, do_ref, lse_ref, dl_ref, dq_ref, dk_ref, dv_ref, dk_acc, dv_acc):
        n = pl.program_id(1)
        valid = n < nb

        @pl.when(n == 0)
        def _():
            dk_acc[...] = jnp.zeros_like(dk_acc)
            dv_acc[...] = jnp.zeros_like(dv_acc)

        mask = _attn_mask(n, valid)
        lo = lax.broadcasted_iota(jnp.int32, (Q_BLOCK, LANES), 1) < AT_DIM
        for j in range(AT_WIDTH // LANES):
            cols = slice(j * LANES, (j + 1) * LANES)
            q = q_ref[:, cols]
            kk = jnp.concatenate([kp_ref[:, cols], kc_ref[:, cols]], axis=0)
            vv = jnp.concatenate([vp_ref[:, cols], vc_ref[:, cols]], axis=0)
            dob = _bf(do_ref[:, cols])
            lse2, dl2 = lse_ref[:, cols], dl_ref[:, cols]
            dkk = jnp.zeros((2 * Q_BLOCK, LANES), F32)
            dvv = jnp.zeros((2 * Q_BLOCK, LANES), F32)
            dqs = []
            for sel in (lo, ~lo):
                qm = jnp.where(sel, q, jnp.zeros_like(q))
                dom = jnp.where(sel, dob, jnp.zeros_like(dob))
                lse_c = jnp.max(jnp.where(sel, lse2, -jnp.inf), axis=-1, keepdims=True)
                dl_c = jnp.max(jnp.where(sel, dl2, -jnp.inf), axis=-1, keepdims=True)
                sc = lax.dot_general(qm, kk, _NT, preferred_element_type=F32) * scale
                pr = jnp.where(mask, jnp.exp(jnp.where(mask, sc - lse_c, 0.0)), 0.0)
                dp = lax.dot_general(dom, vv, _NT, preferred_element_type=F32)
                ds = _bf(pr * (dp - dl_c) * scale)
                dqs.append(jnp.dot(ds, kk, preferred_element_type=F32))
                dkk = dkk + lax.dot_general(ds, qm, _TN, preferred_element_type=F32)
                dvv = dvv + lax.dot_general(_bf(pr), dom, _TN, preferred_element_type=F32)

            @pl.when(valid)
            def _():
                dq_ref[:, cols] = jnp.where(lo, dqs[0], dqs[1])

            dk_ref[:, cols] = dk_acc[:, cols] + dkk[:Q_BLOCK]
            dv_ref[:, cols] = dv_acc[:, cols] + dvv[:Q_BLOCK]
            dk_acc[:, cols] = dkk[Q_BLOCK:]
            dv_acc[:, cols] = dvv[Q_BLOCK:]

    cur, prev = _attn_specs(d, nb)
    kout = pl.BlockSpec((Q_BLOCK, AT_WIDTH), lambda r, n: (jnp.maximum(n - 1, 0), r))
    view = lambda a: a.reshape(sub, d * AT_WIDTH)
    outs = pl.pallas_call(
        body, name=f"attn_bwd_d{d}", grid=(d, nb + 1), compiler_params=_params("arbitrary", "arbitrary"),
        in_specs=[cur, cur, prev, cur, prev, cur, cur, cur], out_specs=(cur, kout, kout),
        out_shape=(SDS((sub, d * AT_WIDTH), F32),) * 3,
        scratch_shapes=[pltpu.VMEM((Q_BLOCK, AT_WIDTH), F32), pltpu.VMEM((Q_BLOCK, AT_WIDTH), F32)],
    )(view(qr), view(kr), view(kr), view(vb), view(vb), view(do), view(lse), view(delta))
    return tuple(o.reshape(s, AT_WIDTH) for o in outs)


def _rope_bwd(dqs, dks, dvs, cos_t, sin_t, ts):
    s = cos_t.shape[0]

    def body(q1, q2, q3, k1, k2, k3, v1, v2, v3, cos_ref, sin_ref, oq, ok, ov):
        cs, sn = cos_ref[...], sin_ref[...]
        for j in range(AT_WIDTH // LANES):
            cols = slice(j * LANES, (j + 1) * LANES)
            for (a, b, c), o_ref in (((q1, q2, q3), oq), ((k1, k2, k3), ok)):
                g = a[:, cols] + b[:, cols] + c[:, cols]
                o_ref[:, cols] = _bf(g * cs + _swap_half64(g * sn))
            ov[:, cols] = _bf(v1[:, cols] + v2[:, cols] + v3[:, cols])

    tok = lambda w: pl.BlockSpec((ts, w), lambda i: (i, 0))
    return pl.pallas_call(
        body, name="rope_bwd", grid=(s // ts,), compiler_params=_params("arbitrary"),
        in_specs=[tok(AT_WIDTH)] * 9 + [tok(LANES)] * 2, out_specs=(tok(AT_WIDTH),) * 3,
        out_shape=(SDS((s, AT_WIDTH), BF16),) * 3,
    )(*dqs, *dks, *dvs, cos_t, sin_t)


def _dn_scan_bwd(do, st, vn, w, qd, kd, p, gl, ts):
    s = do.shape[0]
    ncs = ts // CHUNK
    nt = s // ts

    def body(do_ref, st_ref, vn_ref, w_ref, qd_ref, kd_ref, p_ref, gl_ref,
             du_ref, dw_ref, dqd_ref, dkd_ref, dp_ref, dgl_ref, dstate):
        @pl.when(pl.program_id(0) == 0)
        def _():
            dstate[...] = jnp.zeros_like(dstate)

        def chunk(jr, carry):
            ci = ncs - 1 - jr
            rows = pl.ds(pl.multiple_of(ci * CHUNK, CHUNK), CHUNK)
            rows8 = pl.ds(pl.multiple_of(ci * 8, 8), 8)
            srows = pl.ds(pl.multiple_of(ci * DN_DIM, DN_DIM), DN_DIM)
            for h in range(DN_HEADS):
                cols = slice(h * DN_DIM, (h + 1) * DN_DIM)
                ds_, sf = dstate[h], st_ref[srows, cols]
                vnc, doc, wc, qdc, kdc, pc = (vn_ref[rows, cols], do_ref[rows, cols], w_ref[rows, cols],
                                              qd_ref[rows, cols], kd_ref[rows, cols], p_ref[h, rows, :])
                dvn = _nn(kdc, ds_) + _tn(pc, doc)
                du_ref[rows, cols] = dvn
                dw_ref[rows, cols] = -_nt(dvn, sf)
                dqd_ref[rows, cols] = _nt(doc, sf)
                dkd_ref[rows, cols] = _nt(vnc, ds_)
                dp_ref[h, rows, :] = _nt(doc, vnc)
                dgl = jnp.sum(jnp.sum(ds_ * sf, axis=1, keepdims=True), axis=0, keepdims=True)
                dgl_ref[rows8, cols] = jnp.broadcast_to(dgl, (8, DN_DIM))
                dstate[h] = ds_ * gl_ref[rows8, cols][0:1] + _tn(qdc, doc) - _tn(wc, dvn)
            return carry

        lax.fori_loop(0, ncs, chunk, 0)

    tok = lambda wd: pl.BlockSpec((ts, wd), lambda i: (nt - 1 - i, 0))
    pspec = pl.BlockSpec((DN_HEADS, ts, CHUNK), lambda i: (0, nt - 1 - i, 0))
    g8 = pl.BlockSpec((ncs * 8, DN_WIDTH), lambda i: (nt - 1 - i, 0))
    return pl.pallas_call(
        body, name="dn_scan_bwd", grid=(nt,), compiler_params=_params("arbitrary"),
        in_specs=[tok(DN_WIDTH), pl.BlockSpec((ncs * DN_DIM, DN_WIDTH), lambda i: (nt - 1 - i, 0))]
        + [tok(DN_WIDTH)] * 4 + [pspec, g8],
        out_specs=(tok(DN_WIDTH),) * 4 + (pspec, g8),
        out_shape=(SDS((s, DN_WIDTH), F32),) * 4 + (SDS((DN_HEADS, s, CHUNK), F32),
                                                     SDS((s // CHUNK * 8, DN_WIDTH), F32)),
        scratch_shapes=[pltpu.VMEM((DN_HEADS, DN_DIM, DN_DIM), F32)],
    )(do, st, vn, w, qd, kd, p, gl)


def _dn_chunk_bwd(q, k, v, bg, t, du, dw, dqd, dkd, dp, dgl, ts):
    s = q.shape[0]
    ncs = ts // CHUNK

    def body(q_ref, k_ref, v_ref, bg_ref, t_ref, du_ref, dw_ref, dqd_ref, dkd_ref, dp_ref, dgl_ref,
             dq_ref, dk_ref, dv_ref, dbg_ref):
        def chunk(ci, carry):
            rows = pl.ds(pl.multiple_of(ci * CHUNK, CHUNK), CHUNK)
            rows8 = pl.ds(pl.multiple_of(ci * 8, 8), 8)
            bgc = bg_ref[rows, :]
            lane = lax.broadcasted_iota(jnp.int32, (CHUNK, BA_PAD), 1)
            hs = range(DN_HEADS)
            sl = [slice(h * DN_DIM, (h + 1) * DN_DIM) for h in hs]
            cots = [(du_ref[rows, c], dw_ref[rows, c], dp_ref[h, rows, :], dqd_ref[rows, c], dkd_ref[rows, c],
                     dgl_ref[rows8, c][0:1, 0:1]) for h, c in zip(hs, sl)]
            outs = _chunk_bwd([q_ref[rows, c] for c in sl], [k_ref[rows, c] for c in sl],
                              [v_ref[rows, c] for c in sl], [bgc[:, h:h + 1] for h in hs],
                              [bgc[:, GC_LANE + h:GC_LANE + h + 1] for h in hs],
                              [t_ref[h, rows, :] for h in hs], cots)
            dbg = jnp.zeros((CHUNK, BA_PAD), F32)
            for h, (dq, dk, dv, dbeta, dgc) in enumerate(outs):
                dq_ref[rows, sl[h]] = dq
                dk_ref[rows, sl[h]] = dk
                dv_ref[rows, sl[h]] = dv
                dbg = dbg + jnp.where(lane == h, dbeta, 0.0) + jnp.where(lane == GC_LANE + h, dgc, 0.0)
            dbg_ref[rows, :] = dbg
            return carry

        lax.fori_loop(0, ncs, chunk, 0)

    tok = lambda wd: pl.BlockSpec((ts, wd), lambda i: (i, 0))
    pspec = pl.BlockSpec((DN_HEADS, ts, CHUNK), lambda i: (0, i, 0))
    g8 = pl.BlockSpec((ncs * 8, DN_WIDTH), lambda i: (i, 0))
    return pl.pallas_call(
        body, name="dn_chunk_bwd", grid=(s // ts,), compiler_params=_params("arbitrary"),
        in_specs=[tok(DN_WIDTH)] * 3 + [tok(BA_PAD), pspec] + [tok(DN_WIDTH)] * 4 + [pspec, g8],
        out_specs=(tok(DN_WIDTH),) * 3 + (tok(BA_PAD),),
        out_shape=(SDS((s, DN_WIDTH), F32),) * 3 + (SDS((s, BA_PAD), F32),),
    )(q, k, v, bg, t, du, dw, dqd, dkd, dp, dgl)


def _dn_prep_bwd(qkv_pre, ba, dq, dk, dv, dbg, conv_w8, alog_row, dtb_row, ts):
    s = qkv_pre.shape[0]
    cw = 3 * DN_WIDTH
    nt = s // ts

    def body(pre_ref, ph_ref, nh_ref, ba_ref, dq_ref, dqh_ref, dk_ref, dkh_ref, dv_ref, dvh_ref, dbg_ref,
             cw_ref, al_ref, dtb_ref, dpre_ref, dba_ref, dcw_ref, dal_ref, ddtb_ref):
        n = pl.program_id(0)

        @pl.when(n == 0)
        def _():
            dcw_ref[...] = jnp.zeros_like(dcw_ref)
            dal_ref[...] = jnp.zeros_like(dal_ref)
            ddtb_ref[...] = jnp.zeros_like(ddtb_ref)

        last = n == nt - 1
        prev = jnp.where(n == 0, 0.0, ph_ref[...])
        ext = jnp.concatenate([prev, pre_ref[...], nh_ref[...]], axis=0)
        taps = _conv_taps(ext, ts + 8)
        conv = taps[0] * cw_ref[0:1, :]
        for j in range(1, CONV_K):
            conv = conv + taps[j] * cw_ref[j:j + 1, :]

        def cot(main, halo, cols):
            return jnp.concatenate([main[:, cols], jnp.where(last, 0.0, halo[:, cols])], axis=0)

        pieces = []
        for grp, (fn, mref, href) in enumerate(((_post_q, dq_ref, dqh_ref), (_post_k, dk_ref, dkh_ref),
                                                (_post_v, dv_ref, dvh_ref))):
            for h in range(DN_HEADS):
                cols = slice(h * DN_DIM, (h + 1) * DN_DIM)
                c0 = grp * DN_WIDTH + h * DN_DIM
                _, vjp = jax.vjp(fn, conv[:, c0:c0 + DN_DIM])
                pieces.append(vjp(cot(mref, href, cols))[0])
        dconv = jnp.concatenate(pieces, axis=1)
        rows = ts + 8
        dpre = dconv[:ts] * cw_ref[CONV_K - 1:CONV_K, :]
        for j in range(CONV_K - 1):
            sh = CONV_K - 1 - j
            dpre = dpre + pltpu.roll(dconv, rows - sh, 0)[:ts] * cw_ref[j:j + 1, :]
        dpre_ref[...] = _bf(dpre)
        for j in range(CONV_K):
            dcw_ref[j:j + 1, :] += jnp.sum(dconv[:ts] * taps[j][:ts], axis=0, keepdims=True)

        dbg = dbg_ref[...]
        lane = lax.broadcasted_iota(jnp.int32, dbg.shape, 1)
        dg = pltpu.roll(_chunk_cumsum(dbg, reverse=True), BA_PAD - DN_HEADS, 1)
        cot_bg = jnp.where(lane < DN_HEADS, dbg, jnp.where(lane < GC_LANE, dg, 0.0))
        _, vjp = jax.vjp(_beta_decay, ba_ref[...], al_ref[...], dtb_ref[...])
        dba, dal, ddtb = vjp(cot_bg)
        dba_ref[...] = _bf(dba)
        dal_ref[...] += dal
        ddtb_ref[...] += ddtb

    tok = lambda w: pl.BlockSpec((ts, w), lambda i: (i, 0))
    full = lambda a: pl.BlockSpec(a.shape, lambda i: (0, 0))
    prevh = lambda w: pl.BlockSpec((8, w), lambda i: (jnp.maximum(i * (ts // 8) - 1, 0), 0))
    nexth = lambda w: pl.BlockSpec((8, w), lambda i: (jnp.minimum((i + 1) * (ts // 8), s // 8 - 1), 0))
    row = pl.BlockSpec((1, LANES), lambda i: (0, 0))
    return pl.pallas_call(
        body, name="dn_prep_bwd", grid=(nt,), compiler_params=_params("arbitrary"),
        in_specs=[tok(cw), prevh(cw), nexth(cw), tok(BA_PAD),
                  tok(DN_WIDTH), nexth(DN_WIDTH), tok(DN_WIDTH), nexth(DN_WIDTH), tok(DN_WIDTH), nexth(DN_WIDTH),
                  tok(BA_PAD), full(conv_w8), full(alog_row), full(dtb_row)],
        out_specs=(tok(cw), tok(BA_PAD), pl.BlockSpec((8, cw), lambda i: (0, 0)), row, row),
        out_shape=(SDS((s, cw), BF16), SDS((s, BA_PAD), BF16), SDS((8, cw), F32), SDS((1, LANES), F32),
                   SDS((1, LANES), F32)),
    )(qkv_pre, qkv_pre, qkv_pre, ba, dq, dq, dk, dk, dv, dv, dbg, conv_w8, alog_row, dtb_row)


def _dh_dx(dps, ws, x, mod, norm_w, dx2, ts):
    s = x.shape[0]
    widths = [w.shape[1] for w in ws]
    np_ = len(ws)

    def body(*refs):
        dp_refs, w_refs = refs[:np_], refs[np_:2 * np_]
        x_ref, mod_ref, nw_ref, dx2_ref, gx_ref, dshift, dscale, dnw = refs[2 * np_:]

        @pl.when(pl.program_id(0) == 0)
        def _():
            dshift[...] = jnp.zeros_like(dshift)
            dscale[...] = jnp.zeros_like(dscale)
            dnw[...] = jnp.zeros_like(dnw)

        dh = lax.dot_general(dp_refs[0][...], w_refs[0][...], _NT, preferred_element_type=F32)
        for a, b in zip(dp_refs[1:], w_refs[1:]):
            dh = dh + lax.dot_general(a[...], b[...], _NT, preferred_element_type=F32)
        xt = x_ref[...]
        r = lax.rsqrt(jnp.mean(xt * xt, axis=-1, keepdims=True) + EPS)
        xn = xt * r
        nw = nw_ref[...]
        sc1 = 1.0 + mod_ref[:, D_MODEL:2 * D_MODEL]
        dshift[...] += jnp.sum(dh, axis=0, keepdims=True)
        dscale[...] += jnp.sum(dh * (xn * nw), axis=0, keepdims=True)
        dnw[...] += jnp.sum(dh * sc1 * xn, axis=0, keepdims=True)
        dxn = dh * sc1 * nw
        gx_ref[...] = r * (dxn - xn * jnp.mean(dxn * xn, axis=-1, keepdims=True)) + dx2_ref[...]

    tok = lambda w: pl.BlockSpec((ts, w), lambda i: (i, 0))
    full = lambda a: pl.BlockSpec(a.shape, lambda i: (0, 0))
    row = pl.BlockSpec((1, D_MODEL), lambda i: (0, 0))
    return pl.pallas_call(
        body, name="dh_dx", grid=(s // ts,), compiler_params=_params("arbitrary"),
        in_specs=[tok(w) for w in widths] + [full(w) for w in ws] + [tok(D_MODEL), full(mod), full(norm_w),
                                                                    tok(D_MODEL)],
        out_specs=(tok(D_MODEL), row, row, row),
        out_shape=(SDS((s, D_MODEL), F32),) + (SDS((1, D_MODEL), F32),) * 3,
    )(*dps, *ws, x, mod, norm_w, dx2)


def _grad_w_in(h, dps, ts, name):
    s = h.shape[0]
    widths = [p.shape[1] for p in dps]
    np_ = len(dps)

    def body(*refs):
        h_ref, dp_refs, outs = refs[0], refs[1:1 + np_], refs[1 + np_:]

        @pl.when(pl.program_id(0) == 0)
        def _():
            for o in outs:
                o[...] = jnp.zeros_like(o)

        hb = h_ref[...]
        for p, o in zip(dp_refs, outs):
            o[...] += lax.dot_general(hb, p[...], _TN, preferred_element_type=F32)

    tok = lambda w: pl.BlockSpec((ts, w), lambda i: (i, 0))
    return pl.pallas_call(
        body, name=name, grid=(s // ts,), compiler_params=_params("arbitrary"),
        in_specs=[tok(D_MODEL)] + [tok(w) for w in widths],
        out_specs=tuple(pl.BlockSpec((D_MODEL, w), lambda i: (0, 0)) for w in widths),
        out_shape=tuple(SDS((D_MODEL, w), F32) for w in widths),
    )(h, *dps)


def _adamw_math(w, g, m, v):
    m = ADAM_B1 * m + (1.0 - ADAM_B1) * g
    v = ADAM_B2 * v + (1.0 - ADAM_B2) * (g * g)
    m_hat = m / (1.0 - ADAM_B1 ** ADAM_STEP)
    v_hat = v / (1.0 - ADAM_B2 ** ADAM_STEP)
    delta = -ADAM_LR * (m_hat / (jnp.sqrt(v_hat) + ADAM_EPS) + ADAM_WD * w)
    return delta, m, v


def _adamw(w, m, v, g, name, slots=False):
    def body(w_ref, m_ref, v_ref, g_ref, g_out, d_out, m_out, v_out):
        if slots:
            g = g_ref[0].astype(F32)
            for k in range(1, N_DEV):
                g = g + g_ref[k].astype(F32)
        else:
            g = g_ref[...]
        g_out[...] = g
        d_out[...], m_out[...], v_out[...] = _adamw_math(w_ref[...], g, m_ref[...], v_ref[...])

    return pl.pallas_call(body, name=name, compiler_params=_params(),
                          out_shape=(SDS(w.shape, F32),) * 4)(w, m, v, g)


def _adamw_w_mod(w, m, v, siluc_all, dmod_mine):
    def body(w_ref, m_ref, v_ref, sc_ref, dm_ref, g_out, d_out, m_out, v_out):
        g = _htn(sc_ref[...], dm_ref[...])
        g_out[...] = g
        d_out[...], m_out[...], v_out[...] = _adamw_math(w_ref[...], g, m_ref[...], v_ref[...])

    return pl.pallas_call(body, name="adamw_w_mod", compiler_params=_params(),
                          out_shape=(SDS(w.shape, F32),) * 4)(w, m, v, siluc_all, dmod_mine)


def _pack_sum(pack_all):
    def body(p_ref, o_ref):
        t = p_ref[0]
        for k in range(1, N_DEV):
            t = t + p_ref[k]
        o_ref[...] = t

    return pl.pallas_call(body, name="pack_sum", out_shape=SDS(pack_all.shape[1:], F32))(pack_all)


def _tile(s, want):
    t = min(want, s)
    assert s % t == 0
    return t


def _local_step(x, c, positions, w_mod_bf, b_mod, norm_w, w_in_bf, conv_w, a_log, dt_bias, dn_norm_w, at_norm_w,
                w_out_bf, final_norm_w, tgt):
    s = x.shape[0]
    o = [0]
    for wdt in IN_SPLITS:
        o.append(o[-1] + wdt)
    w_ba = jnp.pad(w_in_bf[:, o[2]:o[4]], ((0, 0), (0, BA_PAD - 2 * DN_HEADS)))
    ws = [w_in_bf[:, o[0]:o[1]], w_in_bf[:, o[1]:o[2]], w_ba, w_in_bf[:, o[4]:o[5]], w_in_bf[:, o[5]:o[6]],
          w_in_bf[:, o[6]:o[7]], w_in_bf[:, o[7]:o[8]]]
    conv_w8 = jnp.pad(conv_w, ((0, 8 - CONV_K), (0, 0)))
    alog_row = jnp.pad(a_log, ((0, 0), (DN_HEADS, BA_PAD - 2 * DN_HEADS)))
    dtb_row = jnp.pad(dt_bias, ((0, 0), (DN_HEADS, BA_PAD - 2 * DN_HEADS)))
    atw2 = jnp.concatenate([at_norm_w, at_norm_w], axis=1)

    half = AT_DIM // 2
    inv_freq = ROPE_THETA ** (-jnp.arange(half, dtype=F32) / half)
    ang = positions.astype(F32)[:, None] * inv_freq
    cos, sin = jnp.cos(ang), jnp.sin(ang)
    cos_t = jnp.concatenate([cos, cos, cos, cos], axis=1)
    sin_t = jnp.concatenate([-sin, sin, -sin, sin], axis=1)

    mod, siluc = _adaln_mod(c, w_mod_bf, b_mod)
    gate = mod[:, 2 * D_MODEL:]
    hbf, qkv_pre, z_dn, ba, qr, kr, vb, z_at = _ln_proj(x, mod, norm_w, ws, cos_t, sin_t, _tile(s, 256))
    q, k, v, bg = _dn_prep(qkv_pre, ba, conv_w8, alog_row, dtb_row, _tile(s, 256))
    u, w, qd, kd, p, gl, tinv = _dn_chunk_prep(q, k, v, bg, _tile(s, 512))
    o_dn, vn, st = _dn_scan(u, w, qd, kd, p, gl, _tile(s, 512))
    o_ps, lse_ps = [], []
    for d in DILATIONS:
        o_p, lse_p = _attn_fwd(qr, kr, vb, d)
        o_ps.append(o_p)
        lse_ps.append(lse_p)
    cat, o_at, lse = _mix_prep(o_dn, z_dn, o_ps, lse_ps, z_at, dn_norm_w, atw2, _tile(s, 512))
    dx2, dcat, gw_out, dfw, dgate, loss = _out_loss(cat, x, tgt, w_out_bf, gate, final_norm_w, _tile(s, 512))

    do_dn, dz_dn, do_at, dz_at, delta, ddnw, datw = _mix_bwd(dcat, o_dn, z_dn, o_at, z_at, dn_norm_w, atw2,
                                                             _tile(s, 512))
    dqs, dks, dvs = [], [], []
    for d in DILATIONS:
        dq_p, dk_p, dv_p = _attn_bwd(qr, kr, vb, do_at, lse, delta, d)
        dqs.append(dq_p)
        dks.append(dk_p)
        dvs.append(dv_p)
    daq, dak, dav = _rope_bwd(dqs, dks, dvs, cos_t, sin_t, _tile(s, 512))
    du, dw, dqd, dkd, dp, dgl = _dn_scan_bwd(do_dn, st, vn, w, qd, kd, p, gl, _tile(s, 512))
    dq, dk, dv, dbg = _dn_chunk_bwd(q, k, v, bg, tinv, du, dw, dqd, dkd, dp, dgl, _tile(s, 512))
    dqkv, dba, dcw, dal, ddtb = _dn_prep_bwd(qkv_pre, ba, dq, dk, dv, dbg, conv_w8, alog_row, dtb_row, _tile(s, 256))
    dps = [dqkv, dz_dn, dba, daq, dak, dav, dz_at]
    gx, dshift, dscale, dnw = _dh_dx(dps, ws, x, mod, norm_w, dx2, _tile(s, 256))
    g_qkv, g_z, g_ba = _grad_w_in(hbf, dps[:3], _tile(s, 512), "grad_w_in_dn")
    g_aq, g_ak, g_av, g_az = _grad_w_in(hbf, dps[3:], _tile(s, 512), "grad_w_in_at")
    gw_in = jnp.concatenate([g_qkv, g_z, g_ba[:, :2 * DN_HEADS], g_aq, g_ak, g_av, g_az], axis=1)
    dmod = jnp.concatenate([dshift, dscale, dgate], axis=1)
    small = dict(conv=dcw[:CONV_K], dmod=dmod, siluc=siluc, dnw=dnw, dfw=dfw, alog=dal, dtb=ddtb, dnn=ddnw, atn=datw)
    return loss, gx, gw_in, gw_out, small


def kernel(x, c, positions, w_mod, b_mod, norm_w, w_in, conv_w, a_log, dt_bias, dn_norm_w, at_norm_w, w_out, final_norm_w, loss_target, m_w_mod, m_b_mod, m_norm_w, m_w_in, m_conv_w, m_a_log, m_dt_bias, m_dn_norm_w, m_at_norm_w, m_w_out, m_final_norm_w, v_w_mod, v_b_mod, v_norm_w, v_w_in, v_conv_w, v_a_log, v_dt_bias, v_dn_norm_w, v_at_norm_w, v_w_out, v_final_norm_w):
    me = 4 * lax.axis_index("x") + 2 * lax.axis_index("y") + lax.axis_index("c")
    s = x.shape[1]

    g_mod, g_in, g_conv, g_out = _exchange(
        [_bf(w_mod[0]), _bf(w_in[0]), conv_w[0], _bf(w_out[0])], [False] * 4, "gather_weights")
    w_mod_bf = g_mod.transpose(1, 0, 2).reshape(D_MODEL, 3 * D_MODEL)
    w_in_bf = g_in.transpose(1, 0, 2).reshape(D_MODEL, IN_COLS)
    conv_full = g_conv.transpose(1, 0, 2).reshape(CONV_K, 3 * DN_WIDTH)
    w_out_bf = g_out.reshape(D_MODEL, D_MODEL)

    loss, gx, gw_in, gw_out, small = _local_step(
        x[0], c, positions[0], w_mod_bf, b_mod, norm_w, w_in_bf, conv_full, a_log, dt_bias, dn_norm_w, at_norm_w,
        w_out_bf, final_norm_w.reshape(1, D_MODEL), loss_target[0])

    pack = jnp.concatenate([small["conv"].reshape(1, -1), small["dmod"], small["siluc"], small["dnw"], small["dfw"],
                            small["alog"], small["dtb"], small["dnn"], small["atn"]], axis=1).reshape(PK_ROWS, LANES)
    gw_in_slabs = _bf(gw_in).reshape(D_MODEL, N_DEV, IN_SHARD).transpose(1, 0, 2)
    gw_out_slabs = _bf(gw_out).reshape(N_DEV, D_MODEL // N_DEV, D_MODEL)
    r_in, r_out, pack_all = _exchange([gw_in_slabs, gw_out_slabs, pack], [True, True, False], "exchange_grads")

    res = {}
    res["w_in"] = _adamw(w_in[0], m_w_in[0], v_w_in[0], r_in, "adamw_w_in", slots=True)
    res["w_out"] = _adamw(w_out[0], m_w_out[0], v_w_out[0], r_out, "adamw_w_out", slots=True)
    flat_all = pack_all.reshape(N_DEV, PK_END)
    dmod_mine = lax.dynamic_slice(flat_all, (0, PK_DMOD + me * (3 * D_MODEL // N_DEV)), (N_DEV, 3 * D_MODEL // N_DEV))
    res["w_mod"] = _adamw_w_mod(w_mod[0], m_w_mod[0], v_w_mod[0], flat_all[:, PK_SILUC:PK_DNW], dmod_mine)
    tot = _pack_sum(pack_all).reshape(1, PK_END)
    g_conv_full = tot[:, PK_CONV:PK_DMOD].reshape(CONV_K, 3 * DN_WIDTH)
    g_conv_mine = lax.dynamic_slice(g_conv_full, (0, me * (3 * DN_WIDTH // N_DEV)), (CONV_K, 3 * DN_WIDTH // N_DEV))
    res["conv_w"] = _adamw(conv_w[0], m_conv_w[0], v_conv_w[0], g_conv_mine, "adamw_conv_w")
    res["b_mod"] = _adamw(b_mod, m_b_mod, v_b_mod, tot[:, PK_DMOD:PK_SILUC], "adamw_b_mod")
    res["norm_w"] = _adamw(norm_w, m_norm_w, v_norm_w, tot[:, PK_DNW:PK_DFW], "adamw_norm_w")
    res["a_log"] = _adamw(a_log, m_a_log, v_a_log, tot[:, PK_ALOG + DN_HEADS:PK_ALOG + 2 * DN_HEADS], "adamw_a_log")
    res["dt_bias"] = _adamw(dt_bias, m_dt_bias, v_dt_bias, tot[:, PK_DTB + DN_HEADS:PK_DTB + 2 * DN_HEADS],
                            "adamw_dt_bias")
    res["dn_norm_w"] = _adamw(dn_norm_w, m_dn_norm_w, v_dn_norm_w, tot[:, PK_DNN:PK_ATN], "adamw_dn_norm_w")
    g_atn = tot[:, PK_ATN:PK_ATN + AT_DIM] + tot[:, PK_ATN + AT_DIM:PK_END]
    res["at_norm_w"] = _adamw(at_norm_w, m_at_norm_w, v_at_norm_w, g_atn, "adamw_at_norm_w")
    fin = _adamw(final_norm_w.reshape(1, D_MODEL), m_final_norm_w.reshape(1, D_MODEL),
                 v_final_norm_w.reshape(1, D_MODEL), tot[:, PK_DFW:PK_ALOG], "adamw_final_norm_w")
    res["final_norm_w"] = tuple(a.reshape(D_MODEL) for a in fin)

    lead = ("w_mod", "w_in", "conv_w", "w_out")
    names = ("w_mod", "b_mod", "norm_w", "w_in", "conv_w", "a_log", "dt_bias", "dn_norm_w", "at_norm_w", "w_out",
             "final_norm_w")
    out = [lax.psum(loss[0, 0], ("x", "y", "c")), gx.reshape(1, s, D_MODEL)]
    for kind in range(4):
        for nm in names:
            a = res[nm][kind]
            out.append(a[None] if nm in lead else a)
    return tuple(out)
```

```python
import functools

import jax
import jax.numpy as jnp
from jax import lax
from jax.experimental import pallas as pl
from jax.experimental.pallas import tpu as pltpu

F32, BF16 = jnp.float32, jnp.bfloat16
HI = lax.Precision.HIGHEST
SDS = jax.ShapeDtypeStruct

D_MODEL = 1024
DN_HEADS, DN_DIM, DN_WIDTH = 4, 128, 512
AT_HEADS, AT_DIM, AT_WIDTH = 8, 64, 512
CONV_K = 4
CHUNK = 64
Q_BLOCK = 128
W_SUB = 128
DILATIONS = (1, 4, 16)
AT_PAIRS = 4
ATT_BLK = Q_BLOCK * max(DILATIONS)
ROPE_THETA = 10000.0
EPS = 1e-6
N_DEV = 8
LANES = 128
BA_PAD = 128
IN_SPLITS = (1536, 512, 4, 4, 512, 512, 512, 512)
IN_COLS = sum(IN_SPLITS)
IN_SHARD = IN_COLS // N_DEV
VMEM_LIMIT = 56 * 2 ** 20

ADAM_LR, ADAM_B1, ADAM_B2, ADAM_EPS, ADAM_WD, ADAM_STEP = 0.001, 0.9, 0.999, 1e-08, 0.01, 10

PK_CONV, PK_DMOD, PK_SILUC, PK_DNW, PK_DFW, PK_ALOG, PK_DTB, PK_DNN, PK_ATN, PK_END = (
    0, 6144, 9216, 10240, 11264, 12288, 12416, 12544, 12672, 12800)
PK_ROWS = PK_END // LANES

_NT = (((1,), (1,)), ((), ()))
_TN = (((0,), (0,)), ((), ()))


def _params(*sem):
    return pltpu.CompilerParams(dimension_semantics=sem or None, vmem_limit_bytes=VMEM_LIMIT)


def _bf(x):
    return x.astype(BF16)


def _nn(a, b):
    return jnp.dot(_bf(a), _bf(b), preferred_element_type=F32)


def _nt(a, b):
    return lax.dot_general(_bf(a), _bf(b), _NT, preferred_element_type=F32)


def _tn(a, b):
    return lax.dot_general(_bf(a), _bf(b), _TN, preferred_element_type=F32)


def _hnn(a, b):
    return jnp.dot(a, b, precision=HI, preferred_element_type=F32)


def _hnt(a, b):
    return lax.dot_general(a, b, _NT, precision=HI, preferred_element_type=F32)


def _htn(a, b):
    return lax.dot_general(a, b, _TN, precision=HI, preferred_element_type=F32)


@jax.custom_vjp
def _d_hnn(a, b):
    return _hnn(a, b)


def _d_hnn_fwd(a, b):
    return _hnn(a, b), (a, b)


def _d_hnn_bwd(res, g):
    a, b = res
    return _hnt(g, b), _htn(a, g)


_d_hnn.defvjp(_d_hnn_fwd, _d_hnn_bwd)


def _silu(x):
    return x * jax.nn.sigmoid(x)


def _softplus(x):
    return jnp.maximum(x, 0.0) + jnp.log(1.0 + jnp.exp(-jnp.abs(x)))


def _l2n(x):
    return x * lax.rsqrt(jnp.sum(x * x, axis=-1, keepdims=True) + EPS)


def _post_q(x):
    return _l2n(_silu(x)) * (DN_DIM ** -0.5)


def _post_k(x):
    return _l2n(_silu(x))


def _post_v(x):
    return _silu(x)


def _beta_decay(ba, alog_row, dtb_row):
    lane = lax.broadcasted_iota(jnp.int32, ba.shape, 1)
    return jnp.where(lane < DN_HEADS, jax.nn.sigmoid(ba), -jnp.exp(alog_row) * _softplus(ba + dtb_row))


def _gate_dn(o, z, w):
    return (o * lax.rsqrt(jnp.mean(o * o, axis=-1, keepdims=True) + EPS)) * w * _silu(z)


def _group_ones(scale):
    r = lax.broadcasted_iota(jnp.int32, (LANES, LANES), 0)
    c = lax.broadcasted_iota(jnp.int32, (LANES, LANES), 1)
    return jnp.where((r // AT_DIM) == (c // AT_DIM), scale, 0.0).astype(F32)


def _gate_at(o, z, w2, hnn):
    ms = hnn(o * o, _group_ones(1.0 / AT_DIM))
    return (o * lax.rsqrt(ms + EPS)) * w2 * _silu(z)


def _swap_half64(x):
    lane = lax.broadcasted_iota(jnp.int32, x.shape, 1)
    return jnp.where((lane & (AT_DIM - 1)) < AT_DIM // 2, pltpu.roll(x, LANES - AT_DIM // 2, 1),
                     pltpu.roll(x, AT_DIM // 2, 1))


_NN = (((1,), (0,)), ((), ()))


def _hl(a):
    hi = a.astype(BF16)
    return hi, (a - hi.astype(F32)).astype(BF16)


def _mm3(a, b, dims=_NN):
    (ah, al), (bh, bl) = a, b
    f = lambda x, y: lax.dot_general(x, y, dims, preferred_element_type=F32)
    return f(ah, bh) + (f(ah, bl) + f(al, bh))


def _chunk_masks():
    r = lax.broadcasted_iota(jnp.int32, (CHUNK, CHUNK), 0)
    c = lax.broadcasted_iota(jnp.int32, (CHUNK, CHUNK), 1)
    return r >= c, r > c, (r == c).astype(F32), (r // 16) == (c // 16)


def _tri_inv(mats):
    _, _, eye, blk = _chunk_masks()
    dg = [jnp.where(blk, a, 0.0) for a in mats]
    lo = [jnp.where(blk, 0.0, a) for a in mats]
    sdg = [_hl(x) for x in dg]
    d2 = [_mm3(s, s) for s in sdg]
    sd2 = [_hl(x) for x in d2]
    d4 = [_mm3(s, s) for s in sd2]
    sd4 = [_hl(x) for x in d4]
    d8 = [_mm3(s, s) for s in sd4]
    p1 = [_mm3(_hl(eye - a), _hl(eye + b)) for a, b in zip(dg, d2)]
    p2 = [_mm3(_hl(a), _hl(eye + b)) for a, b in zip(p1, d4)]
    dinv = [_mm3(_hl(a), _hl(eye + b)) for a, b in zip(p2, d8)]
    sdinv = [_hl(x) for x in dinv]
    n1 = [_mm3(s, _hl(b)) for s, b in zip(sdinv, lo)]
    sn1 = [_hl(x) for x in n1]
    n2 = [_mm3(s, s) for s in sn1]
    q1 = [_mm3(_hl(eye - a), _hl(eye + b)) for a, b in zip(n1, n2)]
    return [_mm3(_hl(a), s) for a, s in zip(q1, sdinv)]


def _chunk_common(qs, ks, vs, betas, gcs):
    tril, _, _, _ = _chunk_masks()
    out = []
    for q, k, v, beta, gc in zip(qs, ks, vs, betas, gcs):
        gb = jnp.broadcast_to(gc, (CHUNK, DN_DIM))
        gt = gb.T[:CHUNK, :]
        gam = jnp.where(tril, jnp.exp(jnp.where(tril, gb[:, :CHUNK] - gt, 0.0)), 0.0)
        last = gb[CHUNK - 1:CHUNK, :]
        eg, e2 = jnp.exp(gb), jnp.exp(last - gb)
        kb, vb = k * beta, v * beta
        out.append(dict(gam=gam, eg=eg, e2=e2, gl=jnp.exp(last[:, 0:1]), kb=kb, vb=vb, kbg=kb * eg,
                        m=_nt(kb, k), qk=_nt(q, k)))
    return out


def _chunk_fwd(qs, ks, vs, betas, gcs):
    tril, strict, _, _ = _chunk_masks()
    cm = _chunk_common(qs, ks, vs, betas, gcs)
    ts = _tri_inv([jnp.where(strict, c["m"] * c["gam"], 0.0) for c in cm])
    outs = []
    for q, k, c, t in zip(qs, ks, cm, ts):
        uw = _mm3(_hl(t), _hl(jnp.concatenate([c["vb"], c["kbg"]], axis=1)))
        p = jnp.where(tril, c["qk"] * c["gam"], 0.0)
        outs.append((uw[:, :DN_DIM], uw[:, DN_DIM:], p, q * c["eg"], k * c["e2"], c["gl"], t))
    return outs


def _chunk_bwd(qs, ks, vs, betas, gcs, ts, cots):
    tril, strict, _, _ = _chunk_masks()
    cm = _chunk_common(qs, ks, vs, betas, gcs)
    row = lax.broadcasted_iota(jnp.int32, (CHUNK, 1), 0)
    ones = jnp.ones((CHUNK, DN_DIM), BF16)
    rs = lambda x: jnp.sum(x, axis=-1, keepdims=True)
    sts = [_hl(t) for t in ts]
    duw = [_hl(jnp.concatenate([ct[0], ct[1]], axis=1)) for ct in cots]
    dts = [_mm3(a, _hl(jnp.concatenate([c["vb"], c["kbg"]], axis=1)), _NT) for a, c in zip(duw, cm)]
    xs = [_mm3(s, _hl(d), _TN) for s, d in zip(sts, dts)]
    das = [jnp.where(strict, -_mm3(_hl(x), s, _NT), 0.0) for x, s in zip(xs, sts)]
    dvks = [_mm3(s, a, _TN) for s, a in zip(sts, duw)]
    outs = []
    for q, k, v, beta, c, ct, da, dvk in zip(qs, ks, vs, betas, cm, cots, das, dvks):
        _, _, dp, dqd, dkd, dgl = ct
        dvb, dkbg = dvk[:, :DN_DIM], dvk[:, DN_DIM:]
        dm = da * c["gam"]
        dqk = jnp.where(tril, dp, 0.0) * c["gam"]
        e = dm * c["m"] + dqk * c["qk"]
        dmq = jnp.concatenate([dm, dqk], axis=0)
        r1 = _nn(dmq, k)
        dkb = r1[:CHUNK] + dkbg * c["eg"]
        dq = r1[CHUNK:] + dqd * c["eg"]
        dk = _tn(dmq, jnp.concatenate([c["kb"], q], axis=0)) + dkd * c["e2"] + dkb * beta
        dbeta = rs(dkb * k) + rs(dvb * v)
        eh, el = _hl(e)
        colsum = (lax.dot_general(eh, ones, _TN, preferred_element_type=F32)
                  + lax.dot_general(el, ones, _TN, preferred_element_type=F32))[:, 0:1]
        rs_kd = rs(dkd * (k * c["e2"]))
        dgc = rs(e) - colsum + rs(dqd * q * c["eg"]) + rs(dkbg * c["kbg"]) - rs_kd
        tail = jnp.sum(rs_kd, axis=0, keepdims=True) + dgl * c["gl"]
        dgc = dgc + jnp.where(row == CHUNK - 1, tail, 0.0)
        outs.append((dq, dk, dvb * beta, dbeta, dgc))
    return outs


def _chunk_cumsum(x, reverse=False):
    n = x.shape[0]
    pos = lax.broadcasted_iota(jnp.int32, x.shape, 0) & (CHUNK - 1)
    sh = 1
    while sh < CHUNK:
        if reverse:
            x = x + jnp.where(pos < CHUNK - sh, pltpu.roll(x, n - sh, 0), 0.0)
        else:
            x = x + jnp.where(pos >= sh, pltpu.roll(x, sh, 0), 0.0)
        sh *= 2
    return x


GC_LANE = 2 * DN_HEADS


def _exchange(arrays, scatter, name):
    n = len(arrays)
    out_shapes = []
    for a, sc in zip(arrays, scatter):
        out_shapes.append(SDS(a.shape if sc else (N_DEV,) + a.shape, a.dtype))

    def body(*refs):
        ins, outs = refs[:n], refs[n:2 * n]
        send_sems, recv_sems, loc_sems = refs[2 * n:]
        x, y, c = lax.axis_index("x"), lax.axis_index("y"), lax.axis_index("c")
        me = 4 * x + 2 * y + c
        local, remote = [], []
        for i in range(n):
            src = ins[i].at[me] if scatter[i] else ins[i]
            cp = pltpu.make_async_copy(src, outs[i].at[me], loc_sems.at[i])
            cp.start()
            local.append(cp)
        for dlt in range(1, N_DEV):
            px = 1 - x if dlt & 4 else x
            py = 1 - y if dlt & 2 else y
            pc = 1 - c if dlt & 1 else c
            peer = 4 * px + 2 * py + pc
            for i in range(n):
                src = ins[i].at[peer] if scatter[i] else ins[i]
                cp = pltpu.make_async_remote_copy(
                    src_ref=src, dst_ref=outs[i].at[me],
                    send_sem=send_sems.at[i, dlt - 1], recv_sem=recv_sems.at[i, dlt - 1],
                    device_id=(px, py, pc), device_id_type=pl.DeviceIdType.MESH)
                cp.start()
                arrive = pltpu.make_async_remote_copy(
                    src_ref=src, dst_ref=outs[i].at[peer],
                    send_sem=send_sems.at[i, dlt - 1], recv_sem=recv_sems.at[i, dlt - 1],
                    device_id=(px, py, pc), device_id_type=pl.DeviceIdType.MESH)
                remote.append((cp, arrive))
        for cp, arrive in remote:
            cp.wait_send()
            arrive.wait_recv()
        for cp in local:
            cp.wait()

    any_spec = pl.BlockSpec(memory_space=pl.ANY)
    return pl.pallas_call(
        body, name=name, out_shape=tuple(out_shapes),
        in_specs=[any_spec] * n, out_specs=tuple([any_spec] * n),
        scratch_shapes=[pltpu.SemaphoreType.DMA((n, N_DEV - 1)), pltpu.SemaphoreType.DMA((n, N_DEV - 1)),
                        pltpu.SemaphoreType.DMA((n,))],
    )(*arrays)


def _adaln_mod(c, w_mod, b_mod):
    def body(c_ref, w_ref, b_ref, mod_ref, sc_ref):
        sc = _silu(c_ref[...])
        sc8 = jnp.broadcast_to(sc, (8, D_MODEL))
        mod_ref[...] = _nn(sc8, w_ref[...])[0:1] + b_ref[...]
        sc_ref[...] = sc

    return pl.pallas_call(body, name="adaln_mod", compiler_params=_params(),
                          out_shape=(SDS((1, 3 * D_MODEL), F32), SDS((1, D_MODEL), F32)))(c, w_mod, b_mod)


def _ln_proj(x, mod, norm_w, ws, cos_t, sin_t, ts):
    s = x.shape[0]
    widths = [w.shape[1] for w in ws]

    def body(x_ref, mod_ref, nw_ref, cos_ref, sin_ref, wqkv, wz, wba, waq, wak, wav, waz,
             h_ref, oqkv, oz, oba, oq, ok, ov, oaz):
        xt = x_ref[...]
        r = lax.rsqrt(jnp.mean(xt * xt, axis=-1, keepdims=True) + EPS)
        shift, scale = mod_ref[:, 0:D_MODEL], mod_ref[:, D_MODEL:2 * D_MODEL]
        h = ((xt * r) * nw_ref[...]) * (1.0 + scale) + shift
        hb = _bf(h)
        h_ref[...] = hb
        oqkv[...] = jnp.dot(hb, wqkv[...], preferred_element_type=F32)
        oz[...] = jnp.dot(hb, wz[...], preferred_element_type=F32)
        oba[...] = jnp.dot(hb, wba[...], preferred_element_type=F32)
        oaz[...] = jnp.dot(hb, waz[...], preferred_element_type=F32)
        tv = jnp.dot(hb, wav[...], preferred_element_type=F32)
        for j in range(AT_PAIRS):
            ov[j] = tv[:, j * LANES:(j + 1) * LANES]
        cs, sn = cos_ref[...], sin_ref[...]
        for w_ref, o_ref in ((waq, oq), (wak, ok)):
            t = jnp.dot(hb, w_ref[...], preferred_element_type=F32)
            for j in range(AT_PAIRS):
                tj = t[:, j * LANES:(j + 1) * LANES]
                o_ref[j] = tj * cs + _swap_half64(tj) * sn

    tok = lambda w: pl.BlockSpec((ts, w), lambda i: (i, 0))
    full = lambda a: pl.BlockSpec(a.shape, lambda i: (0, 0))
    pairs = pl.BlockSpec((AT_PAIRS, ts, LANES), lambda i: (0, i, 0))
    return pl.pallas_call(
        body, name="ln_proj", grid=(s // ts,), compiler_params=_params("arbitrary"),
        in_specs=[tok(D_MODEL), full(mod), full(norm_w), tok(LANES), tok(LANES)] + [full(w) for w in ws],
        out_specs=(tok(D_MODEL), tok(widths[0]), tok(widths[1]), tok(widths[2]), pairs, pairs, pairs,
                   tok(widths[6])),
        out_shape=(SDS((s, D_MODEL), BF16), SDS((s, widths[0]), F32), SDS((s, widths[1]), F32),
                   SDS((s, widths[2]), F32)) + (SDS((AT_PAIRS, s, LANES), F32),) * 3 + (SDS((s, widths[6]), F32),),
    )(x, mod, norm_w, cos_t, sin_t, *ws)


def _conv_taps(ext, rows):
    taps = []
    for j in range(CONV_K):
        sh = CONV_K - 1 - j
        rolled = pltpu.roll(ext, sh, 0) if sh else ext
        taps.append(rolled[8:8 + rows])
    return taps


def _dn_prep(qkv_pre, ba, conv_w8, alog_row, dtb_row, ts):
    s = qkv_pre.shape[0]
    cw = 3 * DN_WIDTH

    def body(pre_ref, halo_ref, ba_ref, cw_ref, al_ref, dtb_ref, q_ref, k_ref, v_ref, bg_ref):
        n = pl.program_id(0)
        prev = jnp.where(n == 0, 0.0, halo_ref[...])
        ext = jnp.concatenate([prev, pre_ref[...]], axis=0)
        taps = _conv_taps(ext, ts)
        conv = taps[0] * cw_ref[0:1, :]
        for j in range(1, CONV_K):
            conv = conv + taps[j] * cw_ref[j:j + 1, :]
        for h in range(DN_HEADS):
            cols = slice(h * DN_DIM, (h + 1) * DN_DIM)
            q_ref[:, cols] = _post_q(conv[:, h * DN_DIM:(h + 1) * DN_DIM])
            k_ref[:, cols] = _post_k(conv[:, DN_WIDTH + h * DN_DIM:DN_WIDTH + (h + 1) * DN_DIM])
            v_ref[:, cols] = _post_v(conv[:, 2 * DN_WIDTH + h * DN_DIM:2 * DN_WIDTH + (h + 1) * DN_DIM])
        bg = _beta_decay(ba_ref[...], al_ref[...], dtb_ref[...])
        lane = lax.broadcasted_iota(jnp.int32, bg.shape, 1)
        run = pltpu.roll(_chunk_cumsum(bg), DN_HEADS, 1)
        bg_ref[...] = jnp.where((lane >= GC_LANE) & (lane < GC_LANE + DN_HEADS), run, bg)

    tok = lambda w: pl.BlockSpec((ts, w), lambda i: (i, 0))
    full = lambda a: pl.BlockSpec(a.shape, lambda i: (0, 0))
    halo = pl.BlockSpec((8, cw), lambda i: (jnp.maximum(i * (ts // 8) - 1, 0), 0))
    return pl.pallas_call(
        body, name="dn_prep", grid=(s // ts,), compiler_params=_params("arbitrary"),
        in_specs=[tok(cw), halo, tok(BA_PAD), full(conv_w8), full(alog_row), full(dtb_row)],
        out_specs=(tok(DN_WIDTH), tok(DN_WIDTH), tok(DN_WIDTH), tok(BA_PAD)),
        out_shape=(SDS((s, DN_WIDTH), F32),) * 3 + (SDS((s, BA_PAD), F32),),
    )(qkv_pre, qkv_pre, ba, conv_w8, alog_row, dtb_row)


def _dn_chunk_prep(q, k, v, bg, ts):
    s = q.shape[0]
    ncs = ts // CHUNK

    def body(q_ref, k_ref, v_ref, bg_ref, u_ref, w_ref, qd_ref, kd_ref, p_ref, gl_ref, t_ref):
        def chunk(ci, carry):
            rows = pl.ds(pl.multiple_of(ci * CHUNK, CHUNK), CHUNK)
            rows8 = pl.ds(pl.multiple_of(ci * 8, 8), 8)
            bgc = bg_ref[rows, :]
            hs = range(DN_HEADS)
            sl = [slice(h * DN_DIM, (h + 1) * DN_DIM) for h in hs]
            outs = _chunk_fwd([q_ref[rows, c] for c in sl], [k_ref[rows, c] for c in sl],
                              [v_ref[rows, c] for c in sl], [bgc[:, h:h + 1] for h in hs],
                              [bgc[:, GC_LANE + h:GC_LANE + h + 1] for h in hs])
            for h, (u, w, p, qd, kd, gl, t) in enumerate(outs):
                u_ref[rows, sl[h]] = u
                w_ref[rows, sl[h]] = w
                qd_ref[rows, sl[h]] = qd
                kd_ref[rows, sl[h]] = kd
                p_ref[h, rows, :] = p
                t_ref[h, rows, :] = t
                gl_ref[rows8, sl[h]] = jnp.broadcast_to(gl, (8, DN_DIM))
            return carry

        lax.fori_loop(0, ncs, chunk, 0)

    tok = lambda w: pl.BlockSpec((ts, w), lambda i: (i, 0))
    sq = pl.BlockSpec((DN_HEADS, ts, CHUNK), lambda i: (0, i, 0))
    return pl.pallas_call(
        body, name="dn_chunk_prep", grid=(s // ts,), compiler_params=_params("arbitrary"),
        in_specs=[tok(DN_WIDTH)] * 3 + [tok(BA_PAD)],
        out_specs=(tok(DN_WIDTH),) * 4 + (sq, pl.BlockSpec((ncs * 8, DN_WIDTH), lambda i: (i, 0)), sq),
        out_shape=(SDS((s, DN_WIDTH), F32),) * 4 + (SDS((DN_HEADS, s, CHUNK), F32),
                                                     SDS((s // CHUNK * 8, DN_WIDTH), F32),
                                                     SDS((DN_HEADS, s, CHUNK), F32)),
    )(q, k, v, bg)


def _dn_scan(u, w, qd, kd, p, gl, ts):
    s = u.shape[0]
    ncs = ts // CHUNK

    def body(u_ref, w_ref, qd_ref, kd_ref, p_ref, gl_ref, o_ref, vn_ref, st_ref, state):
        @pl.when(pl.program_id(0) == 0)
        def _():
            state[...] = jnp.zeros_like(state)

        def chunk(ci, carry):
            rows = pl.ds(pl.multiple_of(ci * CHUNK, CHUNK), CHUNK)
            rows8 = pl.ds(pl.multiple_of(ci * 8, 8), 8)
            srows = pl.ds(pl.multiple_of(ci * DN_DIM, DN_DIM), DN_DIM)
            for h in range(DN_HEADS):
                cols = slice(h * DN_DIM, (h + 1) * DN_DIM)
                sf = state[h]
                st_ref[srows, cols] = sf
                vn = u_ref[rows, cols] - _nn(w_ref[rows, cols], sf)
                o_ref[rows, cols] = _nn(qd_ref[rows, cols], sf) + _nn(p_ref[h, rows, :], vn)
                vn_ref[rows, cols] = vn
                state[h] = sf * gl_ref[rows8, cols][0:1] + _tn(kd_ref[rows, cols], vn)
            return carry

        lax.fori_loop(0, ncs, chunk, 0)

    tok = lambda wd: pl.BlockSpec((ts, wd), lambda i: (i, 0))
    return pl.pallas_call(
        body, name="dn_scan", grid=(s // ts,), compiler_params=_params("arbitrary"),
        in_specs=[tok(DN_WIDTH)] * 4 + [pl.BlockSpec((DN_HEADS, ts, CHUNK), lambda i: (0, i, 0)),
                                        pl.BlockSpec((ncs * 8, DN_WIDTH), lambda i: (i, 0))],
        out_specs=(tok(DN_WIDTH), tok(DN_WIDTH), pl.BlockSpec((ncs * DN_DIM, DN_WIDTH), lambda i: (i, 0))),
        out_shape=(SDS((s, DN_WIDTH), F32), SDS((s, DN_WIDTH), F32), SDS((s // CHUNK * DN_DIM, DN_WIDTH), F32)),
        scratch_shapes=[pltpu.VMEM((DN_HEADS, DN_DIM, DN_DIM), F32)],
    )(u, w, qd, kd, p, gl)


def _attn_mask(first):
    qi = lax.broadcasted_iota(jnp.int32, (Q_BLOCK, 2 * Q_BLOCK), 0)
    kj = lax.broadcasted_iota(jnp.int32, (Q_BLOCK, 2 * Q_BLOCK), 1)
    rel = Q_BLOCK + qi - kj
    return (rel >= 0) & (rel <= W_SUB) & ((kj >= Q_BLOCK) | jnp.logical_not(first))


def _attn_combo(c, d):
    if d == 1:
        qs = pl.multiple_of(c * Q_BLOCK, Q_BLOCK)
        return qs, pl.multiple_of(ATT_BLK - Q_BLOCK + c * Q_BLOCK, Q_BLOCK), c == 0
    r, m = c % d, c // d
    qs = r + (d * Q_BLOCK) * m
    return qs, ATT_BLK + qs - d * Q_BLOCK, m == 0


def _rows(start, size, d):
    return pl.ds(start, size) if d == 1 else pl.ds(start, size, stride=d)


def _shift_in(ext, cur, n):
    @pl.when(n == 0)
    def _():
        ext[0:ATT_BLK, :] = jnp.zeros((ATT_BLK, LANES), F32)

    @pl.when(n > 0)
    def _():
        ext[0:ATT_BLK, :] = ext[ATT_BLK:2 * ATT_BLK, :]

    ext[ATT_BLK:2 * ATT_BLK, :] = cur


def _attn_fwd(qr, kr, vv):
    s = qr.shape[1]
    nblk = s // ATT_BLK
    scale = AT_DIM ** -0.5
    npat = len(DILATIONS)

    def body(q_ref, k_ref, v_ref, o_ref, lse_ref, kext, vext, o_p, l_p):
        n = pl.program_id(1)
        _shift_in(kext, k_ref[0], n)
        _shift_in(vext, v_ref[0], n)
        lo = lax.broadcasted_iota(jnp.int32, (Q_BLOCK, LANES), 1) < AT_DIM
        for pi, d in enumerate(DILATIONS):
            def combo(c, carry, pi=pi, d=d):
                qs, ks, m0 = _attn_combo(c, d)
                mask = _attn_mask((n == 0) & m0)
                q = _bf(q_ref[0, _rows(qs, Q_BLOCK, d), :])
                kk = _bf(kext[_rows(ks, 2 * Q_BLOCK, d), :])
                vb = _bf(vext[_rows(ks, 2 * Q_BLOCK, d), :])
                outs, lses = [], []
                for sel in (lo, ~lo):
                    qm = jnp.where(sel, q, jnp.zeros_like(q))
                    sc = lax.dot_general(qm, kk, _NT, preferred_element_type=F32) * scale
                    sc = jnp.where(mask, sc, -1e30)
                    mx = jnp.max(sc, axis=-1, keepdims=True)
                    pr = jnp.exp(sc - mx)
                    l = jnp.sum(pr, axis=-1, keepdims=True)
                    outs.append(jnp.dot(_bf(pr), vb, preferred_element_type=F32) / l)
                    lses.append(mx + jnp.log(l))
                o_p[pi, _rows(qs, Q_BLOCK, d), :] = jnp.where(lo, outs[0], outs[1])
                l_p[pi, _rows(qs, Q_BLOCK, d), :] = jnp.where(lo, lses[0], lses[1])
                return carry

            lax.fori_loop(0, ATT_BLK // Q_BLOCK, combo, 0)

        def merge(i, carry):
            rows = pl.ds(pl.multiple_of(i * 256, 256), 256)
            ls = [l_p[pi, rows, :] for pi in range(npat)]
            mx = jnp.maximum(jnp.maximum(ls[0], ls[1]), ls[2])
            es = [jnp.exp(l - mx) for l in ls]
            den = es[0] + es[1] + es[2]
            o_ref[0, rows, :] = (es[0] * o_p[0, rows, :] + es[1] * o_p[1, rows, :] + es[2] * o_p[2, rows, :]) / den
            lse_ref[0, rows, :] = mx + jnp.log(den)
            return carry

        lax.fori_loop(0, ATT_BLK // 256, merge, 0)

    blk = pl.BlockSpec((1, ATT_BLK, LANES), lambda j, n: (j, n, 0))
    return pl.pallas_call(
        body, name="attn_fwd", grid=(AT_PAIRS, nblk), compiler_params=_params("arbitrary", "arbitrary"),
        in_specs=[blk] * 3, out_specs=(blk, blk),
        out_shape=(SDS((AT_PAIRS, s, LANES), F32),) * 2,
        scratch_shapes=[pltpu.VMEM((2 * ATT_BLK, LANES), F32), pltpu.VMEM((2 * ATT_BLK, LANES), F32),
                        pltpu.VMEM((npat, ATT_BLK, LANES), F32), pltpu.VMEM((npat, ATT_BLK, LANES), F32)],
    )(qr, kr, vv)


def _mix_prep(o_dn, z_dn, o_at, z_at, dnw, atw2, ts):
    s = o_dn.shape[0]

    def body(odn, zdn, oat, zat, dnw_ref, atw_ref, cat_ref):
        for h in range(DN_HEADS):
            cols = slice(h * DN_DIM, (h + 1) * DN_DIM)
            cat_ref[:, cols] = _bf(_gate_dn(odn[:, cols], zdn[:, cols], dnw_ref[...]))
        for j in range(AT_PAIRS):
            cat_ref[:, DN_WIDTH + j * LANES:DN_WIDTH + (j + 1) * LANES] = _bf(
                _gate_at(oat[j], zat[:, j * LANES:(j + 1) * LANES], atw_ref[...], _hnn))

    tok = lambda w: pl.BlockSpec((ts, w), lambda i: (i, 0))
    full = lambda a: pl.BlockSpec(a.shape, lambda i: (0, 0))
    pairs = pl.BlockSpec((AT_PAIRS, ts, LANES), lambda i: (0, i, 0))
    return pl.pallas_call(
        body, name="mix_prep", grid=(s // ts,), compiler_params=_params("arbitrary"),
        in_specs=[tok(DN_WIDTH), tok(DN_WIDTH), pairs, tok(AT_WIDTH), full(dnw), full(atw2)],
        out_specs=tok(D_MODEL), out_shape=SDS((s, D_MODEL), BF16),
    )(o_dn, z_dn, o_at, z_at, dnw, atw2)


def _out_loss(cat, x, tgt, w_out, gate, fw, ts):
    s = x.shape[0]

    def body(cat_ref, x_ref, t_ref, w_ref, g_ref, fw_ref, dx2_ref, dcat_ref, gw_ref, dfw_ref, dgate_ref, loss_ref):
        @pl.when(pl.program_id(0) == 0)
        def _():
            gw_ref[...] = jnp.zeros_like(gw_ref)
            dfw_ref[...] = jnp.zeros_like(dfw_ref)
            dgate_ref[...] = jnp.zeros_like(dgate_ref)
            loss_ref[...] = jnp.zeros_like(loss_ref)

        catb = cat_ref[...]
        wb = w_ref[...]
        gate, fwv = g_ref[...], fw_ref[...]
        mix = jnp.dot(catb, wb, preferred_element_type=F32)
        x2 = x_ref[...] + gate * mix
        r2 = lax.rsqrt(jnp.mean(x2 * x2, axis=-1, keepdims=True) + EPS)
        xn2 = x2 * r2
        err = xn2 * fwv - t_ref[...]
        row = jnp.sum(err * err, axis=-1, keepdims=True) * (1.0 / D_MODEL)
        loss_ref[...] += 0.5 * jnp.sum(row, axis=0, keepdims=True)
        dy = err * (1.0 / D_MODEL)
        dfw_ref[...] += jnp.sum(dy * xn2, axis=0, keepdims=True)
        dxn = dy * fwv
        dx2 = r2 * (dxn - xn2 * jnp.mean(dxn * xn2, axis=-1, keepdims=True))
        dx2_ref[...] = dx2
        dgate_ref[...] += jnp.sum(dx2 * mix, axis=0, keepdims=True)
        dmix = _bf(gate * dx2)
        dcat_ref[...] = lax.dot_general(dmix, wb, _NT, preferred_element_type=F32)
        gw_ref[...] += lax.dot_general(catb, dmix, _TN, preferred_element_type=F32)

    tok = lambda w: pl.BlockSpec((ts, w), lambda i: (i, 0))
    full = lambda a: pl.BlockSpec(a.shape, lambda i: (0, 0))
    row = pl.BlockSpec((1, D_MODEL), lambda i: (0, 0))
    return pl.pallas_call(
        body, name="out_loss", grid=(s // ts,), compiler_params=_params("arbitrary"),
        in_specs=[tok(D_MODEL), tok(D_MODEL), tok(D_MODEL), full(w_out), full(gate), full(fw)],
        out_specs=(tok(D_MODEL), tok(D_MODEL), pl.BlockSpec((D_MODEL, D_MODEL), lambda i: (0, 0)), row, row,
                   pl.BlockSpec((1, 1), lambda i: (0, 0))),
        out_shape=(SDS((s, D_MODEL), F32), SDS((s, D_MODEL), F32), SDS((D_MODEL, D_MODEL), F32),
                   SDS((1, D_MODEL), F32), SDS((1, D_MODEL), F32), SDS((1, 1), F32)),
    )(cat, x, tgt, w_out, gate, fw)


def _mix_bwd(dcat, o_dn, z_dn, o_at, z_at, dnw, atw2, ts):
    s = dcat.shape[0]

    def body(dcat_ref, odn, zdn, oat, zat, dnw_ref, atw_ref, dodn, dzdn, doat, dzat, delta, ddnw, datw):
        @pl.when(pl.program_id(0) == 0)
        def _():
            ddnw[...] = jnp.zeros_like(ddnw)
            datw[...] = jnp.zeros_like(datw)

        for h in range(DN_HEADS):
            cols = slice(h * DN_DIM, (h + 1) * DN_DIM)
            _, vjp = jax.vjp(_gate_dn, odn[:, cols], zdn[:, cols], dnw_ref[...])
            do, dz, dw = vjp(dcat_ref[:, cols])
            dodn[:, cols] = do
            dzdn[:, cols] = _bf(dz)
            ddnw[...] += dw
        for j in range(AT_PAIRS):
            cols = slice(j * LANES, (j + 1) * LANES)
            o = oat[j]
            _, vjp = jax.vjp(functools.partial(_gate_at, hnn=_d_hnn), o, zat[:, cols], atw_ref[...])
            do, dz, dw = vjp(dcat_ref[:, DN_WIDTH + j * LANES:DN_WIDTH + (j + 1) * LANES])
            doat[j] = do
            dzat[:, cols] = _bf(dz)
            datw[...] += dw
            delta[j] = _hnn(do * o, _group_ones(1.0))

    tok = lambda w: pl.BlockSpec((ts, w), lambda i: (i, 0))
    full = lambda a: pl.BlockSpec(a.shape, lambda i: (0, 0))
    row = pl.BlockSpec((1, LANES), lambda i: (0, 0))
    pairs = pl.BlockSpec((AT_PAIRS, ts, LANES), lambda i: (0, i, 0))
    return pl.pallas_call(
        body, name="mix_bwd", grid=(s // ts,), compiler_params=_params("arbitrary"),
        in_specs=[tok(D_MODEL), tok(DN_WIDTH), tok(DN_WIDTH), pairs, tok(AT_WIDTH), full(dnw), full(atw2)],
        out_specs=(tok(DN_WIDTH), tok(DN_WIDTH), pairs, tok(AT_WIDTH), pairs, row, row),
        out_shape=(SDS((s, DN_WIDTH), F32), SDS((s, DN_WIDTH), BF16), SDS((AT_PAIRS, s, LANES), F32),
                   SDS((s, AT_WIDTH), BF16), SDS((AT_PAIRS, s, LANES), F32), SDS((1, LANES), F32),
                   SDS((1, LANES), F32)),
    )(dcat, o_dn, z_dn, o_at, z_at, dnw, atw2)


def _shift_acc(ext, n):
    @pl.when(n == 0)
    def _():
        ext[0:ATT_BLK, :] = jnp.zeros((ATT_BLK, LANES), F32)

    @pl.when(n > 0)
    def _():
        ext[0:ATT_BLK, :] = ext[ATT_BLK:2 * ATT_BLK, :]

    ext[ATT_BLK:2 * ATT_BLK, :] = jnp.zeros((ATT_BLK, LANES), F32)


def _attn_bwd(qr, kr, vv, do, lse, delta):
    s = qr.shape[1]
    nblk = s // ATT_BLK
    scale = AT_DIM ** -0.5

    def body(q_ref, k_ref, v_ref, do_ref, lse_ref, dl_ref, dq_ref, dk_ref, dv_ref, kext, vext, dkext, dvext):
        n = pl.program_id(1)
        _shift_in(kext, k_ref[0], n)
        _shift_in(vext, v_ref[0], n)
        _shift_acc(dkext, n)
        _shift_acc(dvext, n)

        @pl.when(n < nblk)
        def _():
            dq_ref[0] = jnp.zeros((ATT_BLK, LANES), F32)
            lo = lax.broadcasted_iota(jnp.int32, (Q_BLOCK, LANES), 1) < AT_DIM
            for d in DILATIONS:
                def combo(c, carry, d=d):
                    qs, ks, m0 = _attn_combo(c, d)
                    mask = _attn_mask((n == 0) & m0)
                    qrows, krows = _rows(qs, Q_BLOCK, d), _rows(ks, 2 * Q_BLOCK, d)
                    q = _bf(q_ref[0, qrows, :])
                    kk = _bf(kext[krows, :])
                    vb = _bf(vext[krows, :])
                    dob = _bf(do_ref[0, qrows, :])
                    lse2, dl2 = lse_ref[0, qrows, :], dl_ref[0, qrows, :]
                    dkk = jnp.zeros((2 * Q_BLOCK, LANES), F32)
                    dvv = jnp.zeros((2 * Q_BLOCK, LANES), F32)
                    dqs = []
                    for sel in (lo, ~lo):
                        qm = jnp.where(sel, q, jnp.zeros_like(q))
                        dom = jnp.where(sel, dob, jnp.zeros_like(dob))
                        lse_c = jnp.max(jnp.where(sel, lse2, -jnp.inf), axis=-1, keepdims=True)
                        dl_c = jnp.max(jnp.where(sel, dl2, -jnp.inf), axis=-1, keepdims=True)
                        sc = lax.dot_general(qm, kk, _NT, preferred_element_type=F32) * scale
                        pr = jnp.where(mask, jnp.exp(jnp.where(mask, sc - lse_c, 0.0)), 0.0)
                        dp = lax.dot_general(dom, vb, _NT, preferred_element_type=F32)
                        ds = _bf(pr * (dp - dl_c) * scale)
                        dqs.append(jnp.dot(ds, kk, preferred_element_type=F32))
                        dkk = dkk + lax.dot_general(ds, qm, _TN, preferred_element_type=F32)
                        dvv = dvv + lax.dot_general(_bf(pr), dom, _TN, preferred_element_type=F32)
                    dq_ref[0, qrows, :] += jnp.where(lo, dqs[0], dqs[1])
                    dkext[krows, :] += dkk
                    dvext[krows, :] += dvv
                    return carry

                lax.fori_loop(0, ATT_BLK // Q_BLOCK, combo, 0)

        dk_ref[0] = dkext[0:ATT_BLK, :]
        dv_ref[0] = dvext[0:ATT_BLK, :]

    cur = pl.BlockSpec((1, ATT_BLK, LANES), lambda j, n: (j, jnp.minimum(n, nblk - 1), 0))
    done = pl.BlockSpec((1, ATT_BLK, LANES), lambda j, n: (j, jnp.maximum(n - 1, 0), 0))
    return pl.pallas_call(
        body, name="attn_bwd", grid=(AT_PAIRS, nblk + 1), compiler_params=_params("arbitrary", "arbitrary"),
        in_specs=[cur] * 6, out_specs=(cur, done, done),
        out_shape=(SDS((AT_PAIRS, s, LANES), F32),) * 3,
        scratch_shapes=[pltpu.VMEM((2 * ATT_BLK, LANES), F32)] * 4,
    )(qr, kr, vv, do, lse, delta)


def _rope_bwd(dq, dk, dv, cos_t, sin_t, ts):
    s = cos_t.shape[0]

    def body(q_ref, k_ref, v_ref, cos_ref, sin_ref, oq, ok, ov):
        cs, sn = cos_ref[...], sin_ref[...]
        for j in range(AT_PAIRS):
            cols = slice(j * LANES, (j + 1) * LANES)
            for g_ref, o_ref in ((q_ref, oq), (k_ref, ok)):
                g = g_ref[j]
                o_ref[:, cols] = _bf(g * cs + _swap_half64(g * sn))
            ov[:, cols] = _bf(v_ref[j])

    tok = lambda w: pl.BlockSpec((ts, w), lambda i: (i, 0))
    pairs = pl.BlockSpec((AT_PAIRS, ts, LANES), lambda i: (0, i, 0))
    return pl.pallas_call(
        body, name="rope_bwd", grid=(s // ts,), compiler_params=_params("arbitrary"),
        in_specs=[pairs] * 3 + [tok(LANES)] * 2, out_specs=(tok(AT_WIDTH),) * 3,
        out_shape=(SDS((s, AT_WIDTH), BF16),) * 3,
    )(dq, dk, dv, cos_t, sin_t)


def _dn_scan_bwd(do, st, vn, w, qd, kd, p, gl, ts):
    s = do.shape[0]
    ncs = ts // CHUNK
    nt = s // ts

    def body(do_ref, st_ref, vn_ref, w_ref, qd_ref, kd_ref, p_ref, gl_ref,
             du_ref, dw_ref, dqd_ref, dkd_ref, dp_ref, dgl_ref, dstate):
        @pl.when(pl.program_id(0) == 0)
        def _():
            dstate[...] = jnp.zeros_like(dstate)

        def chunk(jr, carry):
            ci = ncs - 1 - jr
            rows = pl.ds(pl.multiple_of(ci * CHUNK, CHUNK), CHUNK)
            rows8 = pl.ds(pl.multiple_of(ci * 8, 8), 8)
            srows = pl.ds(pl.multiple_of(ci * DN_DIM, DN_DIM), DN_DIM)
            for h in range(DN_HEADS):
                cols = slice(h * DN_DIM, (h + 1) * DN_DIM)
                ds_, sf = dstate[h], st_ref[srows, cols]
                vnc, doc, wc, qdc, kdc, pc = (vn_ref[rows, cols], do_ref[rows, cols], w_ref[rows, cols],
                                              qd_ref[rows, cols], kd_ref[rows, cols], p_ref[h, rows, :])
                dvn = _nn(kdc, ds_) + _tn(pc, doc)
                du_ref[rows, cols] = dvn
                dw_ref[rows, cols] = -_nt(dvn, sf)
                dqd_ref[rows, cols] = _nt(doc, sf)
                dkd_ref[rows, cols] = _nt(vnc, ds_)
                dp_ref[h, rows, :] = _nt(doc, vnc)
                dgl = jnp.sum(jnp.sum(ds_ * sf, axis=1, keepdims=True), axis=0, keepdims=True)
                dgl_ref[rows8, cols] = jnp.broadcast_to(dgl, (8, DN_DIM))
                dstate[h] = ds_ * gl_ref[rows8, cols][0:1] + _tn(qdc, doc) - _tn(wc, dvn)
            return carry

        lax.fori_loop(0, ncs, chunk, 0)

    tok = lambda wd: pl.BlockSpec((ts, wd), lambda i: (nt - 1 - i, 0))
    pspec = pl.BlockSpec((DN_HEADS, ts, CHUNK), lambda i: (0, nt - 1 - i, 0))
    g8 = pl.BlockSpec((ncs * 8, DN_WIDTH), lambda i: (nt - 1 - i, 0))
    return pl.pallas_call(
        body, name="dn_scan_bwd", grid=(nt,), compiler_params=_params("arbitrary"),
        in_specs=[tok(DN_WIDTH), pl.BlockSpec((ncs * DN_DIM, DN_WIDTH), lambda i: (nt - 1 - i, 0))]
        + [tok(DN_WIDTH)] * 4 + [pspec, g8],
        out_specs=(tok(DN_WIDTH),) * 4 + (pspec, g8),
        out_shape=(SDS((s, DN_WIDTH), F32),) * 4 + (SDS((DN_HEADS, s, CHUNK), F32),
                                                     SDS((s // CHUNK * 8, DN_WIDTH), F32)),
        scratch_shapes=[pltpu.VMEM((DN_HEADS, DN_DIM, DN_DIM), F32)],
    )(do, st, vn, w, qd, kd, p, gl)


def _dn_chunk_bwd(q, k, v, bg, t, du, dw, dqd, dkd, dp, dgl, ts):
    s = q.shape[0]
    ncs = ts // CHUNK

    def body(q_ref, k_ref, v_ref, bg_ref, t_ref, du_ref, dw_ref, dqd_ref, dkd_ref, dp_ref, dgl_ref,
             dq_ref, dk_ref, dv_ref, dbg_ref):
        def chunk(ci, carry):
            rows = pl.ds(pl.multiple_of(ci * CHUNK, CHUNK), CHUNK)
            rows8 = pl.ds(pl.multiple_of(ci * 8, 8), 8)
            bgc = bg_ref[rows, :]
            lane = lax.broadcasted_iota(jnp.int32, (CHUNK, BA_PAD), 1)
            hs = range(DN_HEADS)
            sl = [slice(h * DN_DIM, (h + 1) * DN_DIM) for h in hs]
            cots = [(du_ref[rows, c], dw_ref[rows, c], dp_ref[h, rows, :], dqd_ref[rows, c], dkd_ref[rows, c],
                     dgl_ref[rows8, c][0:1, 0:1]) for h, c in zip(hs, sl)]
            outs = _chunk_bwd([q_ref[rows, c] for c in sl], [k_ref[rows, c] for c in sl],
                              [v_ref[rows, c] for c in sl], [bgc[:, h:h + 1] for h in hs],
                              [bgc[:, GC_LANE + h:GC_LANE + h + 1] for h in hs],
                              [t_ref[h, rows, :] for h in hs], cots)
            dbg = jnp.zeros((CHUNK, BA_PAD), F32)
            for h, (dq, dk, dv, dbeta, dgc) in enumerate(outs):
                dq_ref[rows, sl[h]] = dq
                dk_ref[rows, sl[h]] = dk
                dv_ref[rows, sl[h]] = dv
                dbg = dbg + jnp.where(lane == h, dbeta, 0.0) + jnp.where(lane == GC_LANE + h, dgc, 0.0)
            dbg_ref[rows, :] = dbg
            return carry

        lax.fori_loop(0, ncs, chunk, 0)

    tok = lambda wd: pl.BlockSpec((ts, wd), lambda i: (i, 0))
    pspec = pl.BlockSpec((DN_HEADS, ts, CHUNK), lambda i: (0, i, 0))
    g8 = pl.BlockSpec((ncs * 8, DN_WIDTH), lambda i: (i, 0))
    return pl.pallas_call(
        body, name="dn_chunk_bwd", grid=(s // ts,), compiler_params=_params("arbitrary"),
        in_specs=[tok(DN_WIDTH)] * 3 + [tok(BA_PAD), pspec] + [tok(DN_WIDTH)] * 4 + [pspec, g8],
        out_specs=(tok(DN_WIDTH),) * 3 + (tok(BA_PAD),),
        out_shape=(SDS((s, DN_WIDTH), F32),) * 3 + (SDS((s, BA_PAD), F32),),
    )(q, k, v, bg, t, du, dw, dqd, dkd, dp, dgl)


def _dn_prep_bwd(qkv_pre, ba, dq, dk, dv, dbg, conv_w8, alog_row, dtb_row, ts):
    s = qkv_pre.shape[0]
    cw = 3 * DN_WIDTH
    nt = s // ts

    def body(pre_ref, ph_ref, nh_ref, ba_ref, dq_ref, dqh_ref, dk_ref, dkh_ref, dv_ref, dvh_ref, dbg_ref,
             cw_ref, al_ref, dtb_ref, dpre_ref, dba_ref, dcw_ref, dal_ref, ddtb_ref):
        n = pl.program_id(0)

        @pl.when(n == 0)
        def _():
            dcw_ref[...] = jnp.zeros_like(dcw_ref)
            dal_ref[...] = jnp.zeros_like(dal_ref)
            ddtb_ref[...] = jnp.zeros_like(ddtb_ref)

        last = n == nt - 1
        prev = jnp.where(n == 0, 0.0, ph_ref[...])
        ext = jnp.concatenate([prev, pre_ref[...], nh_ref[...]], axis=0)
        taps = _conv_taps(ext, ts + 8)
        conv = taps[0] * cw_ref[0:1, :]
        for j in range(1, CONV_K):
            conv = conv + taps[j] * cw_ref[j:j + 1, :]

        def cot(main, halo, cols):
            return jnp.concatenate([main[:, cols], jnp.where(last, 0.0, halo[:, cols])], axis=0)

        pieces = []
        for grp, (fn, mref, href) in enumerate(((_post_q, dq_ref, dqh_ref), (_post_k, dk_ref, dkh_ref),
                                                (_post_v, dv_ref, dvh_ref))):
            for h in range(DN_HEADS):
                cols = slice(h * DN_DIM, (h + 1) * DN_DIM)
                c0 = grp * DN_WIDTH + h * DN_DIM
                _, vjp = jax.vjp(fn, conv[:, c0:c0 + DN_DIM])
                pieces.append(vjp(cot(mref, href, cols))[0])
        dconv = jnp.concatenate(pieces, axis=1)
        rows = ts + 8
        dpre = dconv[:ts] * cw_ref[CONV_K - 1:CONV_K, :]
        for j in range(CONV_K - 1):
            sh = CONV_K - 1 - j
            dpre = dpre + pltpu.roll(dconv, rows - sh, 0)[:ts] * cw_ref[j:j + 1, :]
        dpre_ref[...] = _bf(dpre)
        for j in range(CONV_K):
            dcw_ref[j:j + 1, :] += jnp.sum(dconv[:ts] * taps[j][:ts], axis=0, keepdims=True)

        dbg = dbg_ref[...]
        lane = lax.broadcasted_iota(jnp.int32, dbg.shape, 1)
        dg = pltpu.roll(_chunk_cumsum(dbg, reverse=True), BA_PAD - DN_HEADS, 1)
        cot_bg = jnp.where(lane < DN_HEADS, dbg, jnp.where(lane < GC_LANE, dg, 0.0))
        _, vjp = jax.vjp(_beta_decay, ba_ref[...], al_ref[...], dtb_ref[...])
        dba, dal, ddtb = vjp(cot_bg)
        dba_ref[...] = _bf(dba)
        dal_ref[...] += dal
        ddtb_ref[...] += ddtb

    tok = lambda w: pl.BlockSpec((ts, w), lambda i: (i, 0))
    full = lambda a: pl.BlockSpec(a.shape, lambda i: (0, 0))
    prevh = lambda w: pl.BlockSpec((8, w), lambda i: (jnp.maximum(i * (ts // 8) - 1, 0), 0))
    nexth = lambda w: pl.BlockSpec((8, w), lambda i: (jnp.minimum((i + 1) * (ts // 8), s // 8 - 1), 0))
    row = pl.BlockSpec((1, LANES), lambda i: (0, 0))
    return pl.pallas_call(
        body, name="dn_prep_bwd", grid=(nt,), compiler_params=_params("arbitrary"),
        in_specs=[tok(cw), prevh(cw), nexth(cw), tok(BA_PAD),
                  tok(DN_WIDTH), nexth(DN_WIDTH), tok(DN_WIDTH), nexth(DN_WIDTH), tok(DN_WIDTH), nexth(DN_WIDTH),
                  tok(BA_PAD), full(conv_w8), full(alog_row), full(dtb_row)],
        out_specs=(tok(cw), tok(BA_PAD), pl.BlockSpec((8, cw), lambda i: (0, 0)), row, row),
        out_shape=(SDS((s, cw), BF16), SDS((s, BA_PAD), BF16), SDS((8, cw), F32), SDS((1, LANES), F32),
                   SDS((1, LANES), F32)),
    )(qkv_pre, qkv_pre, qkv_pre, ba, dq, dq, dk, dk, dv, dv, dbg, conv_w8, alog_row, dtb_row)


def _dh_dx(dps, ws, x, mod, norm_w, dx2, ts):
    s = x.shape[0]
    widths = [w.shape[1] for w in ws]
    np_ = len(ws)

    def body(*refs):
        dp_refs, w_refs = refs[:np_], refs[np_:2 * np_]
        x_ref, mod_ref, nw_ref, dx2_ref, gx_ref, dshift, dscale, dnw = refs[2 * np_:]

        @pl.when(pl.program_id(0) == 0)
        def _():
            dshift[...] = jnp.zeros_like(dshift)
            dscale[...] = jnp.zeros_like(dscale)
            dnw[...] = jnp.zeros_like(dnw)

        dh = lax.dot_general(dp_refs[0][...], w_refs[0][...], _NT, preferred_element_type=F32)
        for a, b in zip(dp_refs[1:], w_refs[1:]):
            dh = dh + lax.dot_general(a[...], b[...], _NT, preferred_element_type=F32)
        xt = x_ref[...]
        r = lax.rsqrt(jnp.mean(xt * xt, axis=-1, keepdims=True) + EPS)
        xn = xt * r
        nw = nw_ref[...]
        sc1 = 1.0 + mod_ref[:, D_MODEL:2 * D_MODEL]
        dshift[...] += jnp.sum(dh, axis=0, keepdims=True)
        dscale[...] += jnp.sum(dh * (xn * nw), axis=0, keepdims=True)
        dnw[...] += jnp.sum(dh * sc1 * xn, axis=0, keepdims=True)
        dxn = dh * sc1 * nw
        gx_ref[...] = r * (dxn - xn * jnp.mean(dxn * xn, axis=-1, keepdims=True)) + dx2_ref[...]

    tok = lambda w: pl.BlockSpec((ts, w), lambda i: (i, 0))
    full = lambda a: pl.BlockSpec(a.shape, lambda i: (0, 0))
    row = pl.BlockSpec((1, D_MODEL), lambda i: (0, 0))
    return pl.pallas_call(
        body, name="dh_dx", grid=(s // ts,), compiler_params=_params("arbitrary"),
        in_specs=[tok(w) for w in widths] + [full(w) for w in ws] + [tok(D_MODEL), full(mod), full(norm_w),
                                                                    tok(D_MODEL)],
        out_specs=(tok(D_MODEL), row, row, row),
        out_shape=(SDS((s, D_MODEL), F32),) + (SDS((1, D_MODEL), F32),) * 3,
    )(*dps, *ws, x, mod, norm_w, dx2)


def _grad_w_in(h, dps, ts, name):
    s = h.shape[0]
    widths = [p.shape[1] for p in dps]
    np_ = len(dps)

    def body(*refs):
        h_ref, dp_refs, outs = refs[0], refs[1:1 + np_], refs[1 + np_:]

        @pl.when(pl.program_id(0) == 0)
        def _():
            for o in outs:
                o[...] = jnp.zeros_like(o)

        hb = h_ref[...]
        for p, o in zip(dp_refs, outs):
            o[...] += lax.dot_general(hb, p[...], _TN, preferred_element_type=F32)

    tok = lambda w: pl.BlockSpec((ts, w), lambda i: (i, 0))
    return pl.pallas_call(
        body, name=name, grid=(s // ts,), compiler_params=_params("arbitrary"),
        in_specs=[tok(D_MODEL)] + [tok(w) for w in widths],
        out_specs=tuple(pl.BlockSpec((D_MODEL, w), lambda i: (0, 0)) for w in widths),
        out_shape=tuple(SDS((D_MODEL, w), F32) for w in widths),
    )(h, *dps)


def _adamw_math(w, g, m, v):
    m = ADAM_B1 * m + (1.0 - ADAM_B1) * g
    v = ADAM_B2 * v + (1.0 - ADAM_B2) * (g * g)
    m_hat = m / (1.0 - ADAM_B1 ** ADAM_STEP)
    v_hat = v / (1.0 - ADAM_B2 ** ADAM_STEP)
    delta = -ADAM_LR * (m_hat / (jnp.sqrt(v_hat) + ADAM_EPS) + ADAM_WD * w)
    return delta, m, v


def _adamw(w, m, v, g, name, slots=False):
    def body(w_ref, m_ref, v_ref, g_ref, g_out, d_out, m_out, v_out):
        if slots:
            g = g_ref[0].astype(F32)
            for k in range(1, N_DEV):
                g = g + g_ref[k].astype(F32)
        else:
            g = g_ref[...]
        g_out[...] = g
        d_out[...], m_out[...], v_out[...] = _adamw_math(w_ref[...], g, m_ref[...], v_ref[...])

    return pl.pallas_call(body, name=name, compiler_params=_params(),
                          out_shape=(SDS(w.shape, F32),) * 4)(w, m, v, g)


def _adamw_w_mod(w, m, v, siluc_all, dmod_mine):
    def body(w_ref, m_ref, v_ref, sc_ref, dm_ref, g_out, d_out, m_out, v_out):
        g = _htn(sc_ref[...], dm_ref[...])
        g_out[...] = g
        d_out[...], m_out[...], v_out[...] = _adamw_math(w_ref[...], g, m_ref[...], v_ref[...])

    return pl.pallas_call(body, name="adamw_w_mod", compiler_params=_params(),
                          out_shape=(SDS(w.shape, F32),) * 4)(w, m, v, siluc_all, dmod_mine)


def _pack_sum(pack_all):
    def body(p_ref, o_ref):
        t = p_ref[0]
        for k in range(1, N_DEV):
            t = t + p_ref[k]
        o_ref[...] = t

    return pl.pallas_call(body, name="pack_sum", out_shape=SDS(pack_all.shape[1:], F32))(pack_all)


def _tile(s, want):
    t = min(want, s)
    assert s % t == 0
    return t


def _local_step(x, c, positions, w_mod_bf, b_mod, norm_w, w_in_bf, conv_w, a_log, dt_bias, dn_norm_w, at_norm_w,
                w_out_bf, final_norm_w, tgt):
    s = x.shape[0]
    o = [0]
    for wdt in IN_SPLITS:
        o.append(o[-1] + wdt)
    w_ba = jnp.pad(w_in_bf[:, o[2]:o[4]], ((0, 0), (0, BA_PAD - 2 * DN_HEADS)))
    ws = [w_in_bf[:, o[0]:o[1]], w_in_bf[:, o[1]:o[2]], w_ba, w_in_bf[:, o[4]:o[5]], w_in_bf[:, o[5]:o[6]],
          w_in_bf[:, o[6]:o[7]], w_in_bf[:, o[7]:o[8]]]
    conv_w8 = jnp.pad(conv_w, ((0, 8 - CONV_K), (0, 0)))
    alog_row = jnp.pad(a_log, ((0, 0), (DN_HEADS, BA_PAD - 2 * DN_HEADS)))
    dtb_row = jnp.pad(dt_bias, ((0, 0), (DN_HEADS, BA_PAD - 2 * DN_HEADS)))
    atw2 = jnp.concatenate([at_norm_w, at_norm_w], axis=1)

    half = AT_DIM // 2
    inv_freq = ROPE_THETA ** (-jnp.arange(half, dtype=F32) / half)
    ang = positions.astype(F32)[:, None] * inv_freq
    cos, sin = jnp.cos(ang), jnp.sin(ang)
    cos_t = jnp.concatenate([cos, cos, cos, cos], axis=1)
    sin_t = jnp.concatenate([-sin, sin, -sin, sin], axis=1)

    mod, siluc = _adaln_mod(c, w_mod_bf, b_mod)
    gate = mod[:, 2 * D_MODEL:]
    hbf, qkv_pre, z_dn, ba, qr, kr, vb, z_at = _ln_proj(x, mod, norm_w, ws, cos_t, sin_t, _tile(s, 256))
    q, k, v, bg = _dn_prep(qkv_pre, ba, conv_w8, alog_row, dtb_row, _tile(s, 256))
    u, w, qd, kd, p, gl, tinv = _dn_chunk_prep(q, k, v, bg, _tile(s, 512))
    o_dn, vn, st = _dn_scan(u, w, qd, kd, p, gl, _tile(s, 512))
    o_at, lse = _attn_fwd(qr, kr, vb)
    cat = _mix_prep(o_dn, z_dn, o_at, z_at, dn_norm_w, atw2, _tile(s, 512))
    dx2, dcat, gw_out, dfw, dgate, loss = _out_loss(cat, x, tgt, w_out_bf, gate, final_norm_w, _tile(s, 512))

    do_dn, dz_dn, do_at, dz_at, delta, ddnw, datw = _mix_bwd(dcat, o_dn, z_dn, o_at, z_at, dn_norm_w, atw2,
                                                             _tile(s, 512))
    daq, dak, dav = _rope_bwd(*_attn_bwd(qr, kr, vb, do_at, lse, delta), cos_t, sin_t, _tile(s, 512))
    du, dw, dqd, dkd, dp, dgl = _dn_scan_bwd(do_dn, st, vn, w, qd, kd, p, gl, _tile(s, 512))
    dq, dk, dv, dbg = _dn_chunk_bwd(q, k, v, bg, tinv, du, dw, dqd, dkd, dp, dgl, _tile(s, 512))
    dqkv, dba, dcw, dal, ddtb = _dn_prep_bwd(qkv_pre, ba, dq, dk, dv, dbg, conv_w8, alog_row, dtb_row, _tile(s, 256))
    dps = [dqkv, dz_dn, dba, daq, dak, dav, dz_at]
    gx, dshift, dscale, dnw = _dh_dx(dps, ws, x, mod, norm_w, dx2, _tile(s, 256))
    g_qkv, g_z, g_ba = _grad_w_in(hbf, dps[:3], _tile(s, 512), "grad_w_in_dn")
    g_aq, g_ak, g_av, g_az = _grad_w_in(hbf, dps[3:], _tile(s, 512), "grad_w_in_at")
    gw_in = jnp.concatenate([g_qkv, g_z, g_ba[:, :2 * DN_HEADS], g_aq, g_ak, g_av, g_az], axis=1)
    dmod = jnp.concatenate([dshift, dscale, dgate], axis=1)
    small = dict(conv=dcw[:CONV_K], dmod=dmod, siluc=siluc, dnw=dnw, dfw=dfw, alog=dal, dtb=ddtb, dnn=ddnw, atn=datw)
    return loss, gx, gw_in, gw_out, small


def kernel(x, c, positions, w_mod, b_mod, norm_w, w_in, conv_w, a_log, dt_bias, dn_norm_w, at_norm_w, w_out, final_norm_w, loss_target, m_w_mod, m_b_mod, m_norm_w, m_w_in, m_conv_w, m_a_log, m_dt_bias, m_dn_norm_w, m_at_norm_w, m_w_out, m_final_norm_w, v_w_mod, v_b_mod, v_norm_w, v_w_in, v_conv_w, v_a_log, v_dt_bias, v_dn_norm_w, v_at_norm_w, v_w_out, v_final_norm_w):
    me = 4 * lax.axis_index("x") + 2 * lax.axis_index("y") + lax.axis_index("c")
    s = x.shape[1]

    g_mod, g_in, g_conv, g_out = _exchange(
        [_bf(w_mod[0]), _bf(w_in[0]), conv_w[0], _bf(w_out[0])], [False] * 4, "gather_weights")
    w_mod_bf = g_mod.transpose(1, 0, 2).reshape(D_MODEL, 3 * D_MODEL)
    w_in_bf = g_in.transpose(1, 0, 2).reshape(D_MODEL, IN_COLS)
    conv_full = g_conv.transpose(1, 0, 2).reshape(CONV_K, 3 * DN_WIDTH)
    w_out_bf = g_out.reshape(D_MODEL, D_MODEL)

    loss, gx, gw_in, gw_out, small = _local_step(
        x[0], c, positions[0], w_mod_bf, b_mod, norm_w, w_in_bf, conv_full, a_log, dt_bias, dn_norm_w, at_norm_w,
        w_out_bf, final_norm_w.reshape(1, D_MODEL), loss_target[0])

    pack = jnp.concatenate([small["conv"].reshape(1, -1), small["dmod"], small["siluc"], small["dnw"], small["dfw"],
                            small["alog"], small["dtb"], small["dnn"], small["atn"]], axis=1).reshape(PK_ROWS, LANES)
    gw_in_slabs = _bf(gw_in).reshape(D_MODEL, N_DEV, IN_SHARD).transpose(1, 0, 2)
    gw_out_slabs = _bf(gw_out).reshape(N_DEV, D_MODEL // N_DEV, D_MODEL)
    r_in, r_out, pack_all = _exchange([gw_in_slabs, gw_out_slabs, pack], [True, True, False], "exchange_grads")

    res = {}
    res["w_in"] = _adamw(w_in[0], m_w_in[0], v_w_in[0], r_in, "adamw_w_in", slots=True)
    res["w_out"] = _adamw(w_out[0], m_w_out[0], v_w_out[0], r_out, "adamw_w_out", slots=True)
    flat_all = pack_all.reshape(N_DEV, PK_END)
    dmod_mine = lax.dynamic_slice(flat_all, (0, PK_DMOD + me * (3 * D_MODEL // N_DEV)), (N_DEV, 3 * D_MODEL // N_DEV))
    res["w_mod"] = _adamw_w_mod(w_mod[0], m_w_mod[0], v_w_mod[0], flat_all[:, PK_SILUC:PK_DNW], dmod_mine)
    tot = _pack_sum(pack_all).reshape(1, PK_END)
    g_conv_full = tot[:, PK_CONV:PK_DMOD].reshape(CONV_K, 3 * DN_WIDTH)
    g_conv_mine = lax.dynamic_slice(g_conv_full, (0, me * (3 * DN_WIDTH // N_DEV)), (CONV_K, 3 * DN_WIDTH // N_DEV))
    res["conv_w"] = _adamw(conv_w[0], m_conv_w[0], v_conv_w[0], g_conv_mine, "adamw_conv_w")
    res["b_mod"] = _adamw(b_mod, m_b_mod, v_b_mod, tot[:, PK_DMOD:PK_SILUC], "adamw_b_mod")
    res["norm_w"] = _adamw(norm_w, m_norm_w, v_norm_w, tot[:, PK_DNW:PK_DFW], "adamw_norm_w")
    res["a_log"] = _adamw(a_log, m_a_log, v_a_log, tot[:, PK_ALOG + DN_HEADS:PK_ALOG + 2 * DN_HEADS], "adamw_a_log")
    res["dt_bias"] = _adamw(dt_bias, m_dt_bias, v_dt_bias, tot[:, PK_DTB + DN_HEADS:PK_DTB + 2 * DN_HEADS],
                            "adamw_dt_bias")
    res["dn_norm_w"] = _adamw(dn_norm_w, m_dn_norm_w, v_dn_norm_w, tot[:, PK_DNN:PK_ATN], "adamw_dn_norm_w")
    g_atn = tot[:, PK_ATN:PK_ATN + AT_DIM] + tot[:, PK_ATN + AT_DIM:PK_END]
    res["at_norm_w"] = _adamw(at_norm_w, m_at_norm_w, v_at_norm_w, g_atn, "adamw_at_norm_w")
    fin = _adamw(final_norm_w.reshape(1, D_MODEL), m_final_norm_w.reshape(1, D_MODEL),
                 v_final_norm_w.reshape(1, D_MODEL), tot[:, PK_DFW:PK_ALOG], "adamw_final_norm_w")
    res["final_norm_w"] = tuple(a.reshape(D_MODEL) for a in fin)

    lead = ("w_mod", "w_in", "conv_w", "w_out")
    names = ("w_mod", "b_mod", "norm_w", "w_in", "conv_w", "a_log", "dt_bias", "dn_norm_w", "at_norm_w", "w_out",
             "final_norm_w")
    out = [lax.psum(loss[0, 0], ("x", "y", "c")), gx.reshape(1, s, D_MODEL)]
    for kind in range(4):
        for nm in names:
            a = res[nm][kind]
            out.append(a[None] if nm in lead else a)
    return tuple(out)
```

```python
import functools

import jax
import jax.numpy as jnp
from jax import lax
from jax.experimental import pallas as pl
from jax.experimental.pallas import tpu as pltpu

F32, BF16 = jnp.float32, jnp.bfloat16
HI = lax.Precision.HIGHEST
SDS = jax.ShapeDtypeStruct

D_MODEL = 1024
DN_HEADS, DN_DIM, DN_WIDTH = 4, 128, 512
AT_HEADS, AT_DIM, AT_WIDTH = 8, 64, 512
CONV_K = 4
CHUNK = 64
Q_BLOCK = 128
W_SUB = 128
DILATIONS = (1, 4, 16)
AT_PAIRS = 4
ATT_BLK = Q_BLOCK * max(DILATIONS)
ATT_UNROLL, ATT_UNROLL_BWD = 4, 2
ROPE_THETA = 10000.0
EPS = 1e-6
N_DEV = 8
LANES = 128
BA_PAD = 128
IN_SPLITS = (1536, 512, 4, 4, 512, 512, 512, 512)
IN_COLS = sum(IN_SPLITS)
IN_SHARD = IN_COLS // N_DEV
VMEM_LIMIT = 56 * 2 ** 20

ADAM_LR, ADAM_B1, ADAM_B2, ADAM_EPS, ADAM_WD, ADAM_STEP = 0.001, 0.9, 0.999, 1e-08, 0.01, 10

PK_CONV, PK_DMOD, PK_SILUC, PK_DNW, PK_DFW, PK_ALOG, PK_DTB, PK_DNN, PK_ATN, PK_END = (
    0, 6144, 9216, 10240, 11264, 12288, 12416, 12544, 12672, 12800)
PK_ROWS = PK_END // LANES

_NT = (((1,), (1,)), ((), ()))
_TN = (((0,), (0,)), ((), ()))


def _params(*sem):
    return pltpu.CompilerParams(dimension_semantics=sem or None, vmem_limit_bytes=VMEM_LIMIT)


def _bf(x):
    return x.astype(BF16)


def _nn(a, b):
    return jnp.dot(_bf(a), _bf(b), preferred_element_type=F32)


def _nt(a, b):
    return lax.dot_general(_bf(a), _bf(b), _NT, preferred_element_type=F32)


def _tn(a, b):
    return lax.dot_general(_bf(a), _bf(b), _TN, preferred_element_type=F32)


def _hnn(a, b):
    return jnp.dot(a, b, precision=HI, preferred_element_type=F32)


def _hnt(a, b):
    return lax.dot_general(a, b, _NT, precision=HI, preferred_element_type=F32)


def _htn(a, b):
    return lax.dot_general(a, b, _TN, precision=HI, preferred_element_type=F32)


@jax.custom_vjp
def _d_hnn(a, b):
    return _hnn(a, b)


def _d_hnn_fwd(a, b):
    return _hnn(a, b), (a, b)


def _d_hnn_bwd(res, g):
    a, b = res
    return _hnt(g, b), _htn(a, g)


_d_hnn.defvjp(_d_hnn_fwd, _d_hnn_bwd)


def _silu(x):
    return x * jax.nn.sigmoid(x)


def _softplus(x):
    return jnp.maximum(x, 0.0) + jnp.log(1.0 + jnp.exp(-jnp.abs(x)))


def _l2n(x):
    return x * lax.rsqrt(jnp.sum(x * x, axis=-1, keepdims=True) + EPS)


def _post_q(x):
    return _l2n(_silu(x)) * (DN_DIM ** -0.5)


def _post_k(x):
    return _l2n(_silu(x))


def _post_v(x):
    return _silu(x)


def _beta_decay(ba, alog_row, dtb_row):
    lane = lax.broadcasted_iota(jnp.int32, ba.shape, 1)
    return jnp.where(lane < DN_HEADS, jax.nn.sigmoid(ba), -jnp.exp(alog_row) * _softplus(ba + dtb_row))


def _gate_dn(o, z, w):
    return (o * lax.rsqrt(jnp.mean(o * o, axis=-1, keepdims=True) + EPS)) * w * _silu(z)


def _group_ones(scale):
    r = lax.broadcasted_iota(jnp.int32, (LANES, LANES), 0)
    c = lax.broadcasted_iota(jnp.int32, (LANES, LANES), 1)
    return jnp.where((r // AT_DIM) == (c // AT_DIM), scale, 0.0).astype(F32)


def _gate_at(o, z, w2, hnn):
    ms = hnn(o * o, _group_ones(1.0 / AT_DIM))
    return (o * lax.rsqrt(ms + EPS)) * w2 * _silu(z)


def _swap_half64(x):
    lane = lax.broadcasted_iota(jnp.int32, x.shape, 1)
    return jnp.where((lane & (AT_DIM - 1)) < AT_DIM // 2, pltpu.roll(x, LANES - AT_DIM // 2, 1),
                     pltpu.roll(x, AT_DIM // 2, 1))


_NN = (((1,), (0,)), ((), ()))


def _hl(a):
    hi = a.astype(BF16)
    return hi, (a - hi.astype(F32)).astype(BF16)


def _mm3(a, b, dims=_NN):
    (ah, al), (bh, bl) = a, b
    f = lambda x, y: lax.dot_general(x, y, dims, preferred_element_type=F32)
    return f(ah, bh) + (f(ah, bl) + f(al, bh))


def _chunk_masks():
    r = lax.broadcasted_iota(jnp.int32, (CHUNK, CHUNK), 0)
    c = lax.broadcasted_iota(jnp.int32, (CHUNK, CHUNK), 1)
    return r >= c, r > c, (r == c).astype(F32), (r // 16) == (c // 16)


def _tri_inv(mats):
    _, _, eye, blk = _chunk_masks()
    dg = [jnp.where(blk, a, 0.0) for a in mats]
    lo = [jnp.where(blk, 0.0, a) for a in mats]
    sdg = [_hl(x) for x in dg]
    d2 = [_mm3(s, s) for s in sdg]
    sd2 = [_hl(x) for x in d2]
    d4 = [_mm3(s, s) for s in sd2]
    sd4 = [_hl(x) for x in d4]
    d8 = [_mm3(s, s) for s in sd4]
    p1 = [_mm3(_hl(eye - a), _hl(eye + b)) for a, b in zip(dg, d2)]
    p2 = [_mm3(_hl(a), _hl(eye + b)) for a, b in zip(p1, d4)]
    dinv = [_mm3(_hl(a), _hl(eye + b)) for a, b in zip(p2, d8)]
    sdinv = [_hl(x) for x in dinv]
    n1 = [_mm3(s, _hl(b)) for s, b in zip(sdinv, lo)]
    sn1 = [_hl(x) for x in n1]
    n2 = [_mm3(s, s) for s in sn1]
    q1 = [_mm3(_hl(eye - a), _hl(eye + b)) for a, b in zip(n1, n2)]
    return [_mm3(_hl(a), s) for a, s in zip(q1, sdinv)]


def _chunk_common(qs, ks, vs, betas, gcs):
    tril, _, _, _ = _chunk_masks()
    out = []
    for q, k, v, beta, gc in zip(qs, ks, vs, betas, gcs):
        gb = jnp.broadcast_to(gc, (CHUNK, DN_DIM))
        gt = gb.T[:CHUNK, :]
        gam = jnp.where(tril, jnp.exp(jnp.where(tril, gb[:, :CHUNK] - gt, 0.0)), 0.0)
        last = gb[CHUNK - 1:CHUNK, :]
        eg, e2 = jnp.exp(gb), jnp.exp(last - gb)
        kb, vb = k * beta, v * beta
        out.append(dict(gam=gam, eg=eg, e2=e2, gl=jnp.exp(last[:, 0:1]), kb=kb, vb=vb, kbg=kb * eg,
                        m=_nt(kb, k), qk=_nt(q, k)))
    return out


def _chunk_fwd(qs, ks, vs, betas, gcs):
    tril, strict, _, _ = _chunk_masks()
    cm = _chunk_common(qs, ks, vs, betas, gcs)
    ts = _tri_inv([jnp.where(strict, c["m"] * c["gam"], 0.0) for c in cm])
    outs = []
    for q, k, c, t in zip(qs, ks, cm, ts):
        uw = _mm3(_hl(t), _hl(jnp.concatenate([c["vb"], c["kbg"]], axis=1)))
        p = jnp.where(tril, c["qk"] * c["gam"], 0.0)
        outs.append((uw[:, :DN_DIM], uw[:, DN_DIM:], p, q * c["eg"], k * c["e2"], c["gl"], t))
    return outs


def _chunk_bwd(qs, ks, vs, betas, gcs, ts, cots):
    tril, strict, _, _ = _chunk_masks()
    cm = _chunk_common(qs, ks, vs, betas, gcs)
    row = lax.broadcasted_iota(jnp.int32, (CHUNK, 1), 0)
    ones = jnp.ones((CHUNK, DN_DIM), BF16)
    rs = lambda x: jnp.sum(x, axis=-1, keepdims=True)
    sts = [_hl(t) for t in ts]
    duw = [_hl(jnp.concatenate([ct[0], ct[1]], axis=1)) for ct in cots]
    dts = [_mm3(a, _hl(jnp.concatenate([c["vb"], c["kbg"]], axis=1)), _NT) for a, c in zip(duw, cm)]
    xs = [_mm3(s, _hl(d), _TN) for s, d in zip(sts, dts)]
    das = [jnp.where(strict, -_mm3(_hl(x), s, _NT), 0.0) for x, s in zip(xs, sts)]
    dvks = [_mm3(s, a, _TN) for s, a in zip(sts, duw)]
    outs = []
    for q, k, v, beta, c, ct, da, dvk in zip(qs, ks, vs, betas, cm, cots, das, dvks):
        _, _, dp, dqd, dkd, dgl = ct
        dvb, dkbg = dvk[:, :DN_DIM], dvk[:, DN_DIM:]
        dm = da * c["gam"]
        dqk = jnp.where(tril, dp, 0.0) * c["gam"]
        e = dm * c["m"] + dqk * c["qk"]
        dmq = jnp.concatenate([dm, dqk], axis=0)
        r1 = _nn(dmq, k)
        dkb = r1[:CHUNK] + dkbg * c["eg"]
        dq = r1[CHUNK:] + dqd * c["eg"]
        dk = _tn(dmq, jnp.concatenate([c["kb"], q], axis=0)) + dkd * c["e2"] + dkb * beta
        dbeta = rs(dkb * k) + rs(dvb * v)
        eh, el = _hl(e)
        colsum = (lax.dot_general(eh, ones, _TN, preferred_element_type=F32)
                  + lax.dot_general(el, ones, _TN, preferred_element_type=F32))[:, 0:1]
        rs_kd = rs(dkd * (k * c["e2"]))
        dgc = rs(e) - colsum + rs(dqd * q * c["eg"]) + rs(dkbg * c["kbg"]) - rs_kd
        tail = jnp.sum(rs_kd, axis=0, keepdims=True) + dgl * c["gl"]
        dgc = dgc + jnp.where(row == CHUNK - 1, tail, 0.0)
        outs.append((dq, dk, dvb * beta, dbeta, dgc))
    return outs


def _chunk_cumsum(x, reverse=False):
    n = x.shape[0]
    pos = lax.broadcasted_iota(jnp.int32, x.shape, 0) & (CHUNK - 1)
    sh = 1
    while sh < CHUNK:
        if reverse:
            x = x + jnp.where(pos < CHUNK - sh, pltpu.roll(x, n - sh, 0), 0.0)
        else:
            x = x + jnp.where(pos >= sh, pltpu.roll(x, sh, 0), 0.0)
        sh *= 2
    return x


GC_LANE = 2 * DN_HEADS


def _exchange(arrays, scatter, name):
    n = len(arrays)
    out_shapes = []
    for a, sc in zip(arrays, scatter):
        out_shapes.append(SDS(a.shape if sc else (N_DEV,) + a.shape, a.dtype))

    def body(*refs):
        ins, outs = refs[:n], refs[n:2 * n]
        send_sems, recv_sems, loc_sems = refs[2 * n:]
        x, y, c = lax.axis_index("x"), lax.axis_index("y"), lax.axis_index("c")
        me = 4 * x + 2 * y + c
        local, remote = [], []
        for i in range(n):
            src = ins[i].at[me] if scatter[i] else ins[i]
            cp = pltpu.make_async_copy(src, outs[i].at[me], loc_sems.at[i])
            cp.start()
            local.append(cp)
        for dlt in range(1, N_DEV):
            px = 1 - x if dlt & 4 else x
            py = 1 - y if dlt & 2 else y
            pc = 1 - c if dlt & 1 else c
            peer = 4 * px + 2 * py + pc
            for i in range(n):
                src = ins[i].at[peer] if scatter[i] else ins[i]
                cp = pltpu.make_async_remote_copy(
                    src_ref=src, dst_ref=outs[i].at[me],
                    send_sem=send_sems.at[i, dlt - 1], recv_sem=recv_sems.at[i, dlt - 1],
                    device_id=(px, py, pc), device_id_type=pl.DeviceIdType.MESH)
                cp.start()
                arrive = pltpu.make_async_remote_copy(
                    src_ref=src, dst_ref=outs[i].at[peer],
                    send_sem=send_sems.at[i, dlt - 1], recv_sem=recv_sems.at[i, dlt - 1],
                    device_id=(px, py, pc), device_id_type=pl.DeviceIdType.MESH)
                remote.append((cp, arrive))
        for cp, arrive in remote:
            cp.wait_send()
            arrive.wait_recv()
        for cp in local:
            cp.wait()

    any_spec = pl.BlockSpec(memory_space=pl.ANY)
    return pl.pallas_call(
        body, name=name, out_shape=tuple(out_shapes),
        in_specs=[any_spec] * n, out_specs=tuple([any_spec] * n),
        scratch_shapes=[pltpu.SemaphoreType.DMA((n, N_DEV - 1)), pltpu.SemaphoreType.DMA((n, N_DEV - 1)),
                        pltpu.SemaphoreType.DMA((n,))],
    )(*arrays)


def _adaln_mod(c, w_mod, b_mod):
    def body(c_ref, w_ref, b_ref, mod_ref, sc_ref):
        sc = _silu(c_ref[...])
        sc8 = jnp.broadcast_to(sc, (8, D_MODEL))
        mod_ref[...] = _nn(sc8, w_ref[...])[0:1] + b_ref[...]
        sc_ref[...] = sc

    return pl.pallas_call(body, name="adaln_mod", compiler_params=_params(),
                          out_shape=(SDS((1, 3 * D_MODEL), F32), SDS((1, D_MODEL), F32)))(c, w_mod, b_mod)


def _ln_proj(x, mod, norm_w, ws, cos_t, sin_t, ts):
    s = x.shape[0]
    widths = [w.shape[1] for w in ws]

    def body(x_ref, mod_ref, nw_ref, cos_ref, sin_ref, wqkv, wz, wba, waq, wak, wav, waz,
             h_ref, oqkv, oz, oba, oq, ok, ov, oaz):
        xt = x_ref[...]
        r = lax.rsqrt(jnp.mean(xt * xt, axis=-1, keepdims=True) + EPS)
        shift, scale = mod_ref[:, 0:D_MODEL], mod_ref[:, D_MODEL:2 * D_MODEL]
        h = ((xt * r) * nw_ref[...]) * (1.0 + scale) + shift
        hb = _bf(h)
        h_ref[...] = hb
        oqkv[...] = jnp.dot(hb, wqkv[...], preferred_element_type=F32)
        oz[...] = jnp.dot(hb, wz[...], preferred_element_type=F32)
        oba[...] = jnp.dot(hb, wba[...], preferred_element_type=F32)
        oaz[...] = jnp.dot(hb, waz[...], preferred_element_type=F32)
        tv = jnp.dot(hb, wav[...], preferred_element_type=F32)
        for j in range(AT_PAIRS):
            ov[j] = tv[:, j * LANES:(j + 1) * LANES]
        cs, sn = cos_ref[...], sin_ref[...]
        for w_ref, o_ref in ((waq, oq), (wak, ok)):
            t = jnp.dot(hb, w_ref[...], preferred_element_type=F32)
            for j in range(AT_PAIRS):
                tj = t[:, j * LANES:(j + 1) * LANES]
                o_ref[j] = tj * cs + _swap_half64(tj) * sn

    tok = lambda w: pl.BlockSpec((ts, w), lambda i: (i, 0))
    full = lambda a: pl.BlockSpec(a.shape, lambda i: (0, 0))
    pairs = pl.BlockSpec((AT_PAIRS, ts, LANES), lambda i: (0, i, 0))
    return pl.pallas_call(
        body, name="ln_proj", grid=(s // ts,), compiler_params=_params("arbitrary"),
        in_specs=[tok(D_MODEL), full(mod), full(norm_w), tok(LANES), tok(LANES)] + [full(w) for w in ws],
        out_specs=(tok(D_MODEL), tok(widths[0]), tok(widths[1]), tok(widths[2]), pairs, pairs, pairs,
                   tok(widths[6])),
        out_shape=(SDS((s, D_MODEL), BF16), SDS((s, widths[0]), F32), SDS((s, widths[1]), F32),
                   SDS((s, widths[2]), F32)) + (SDS((AT_PAIRS, s, LANES), F32),) * 3 + (SDS((s, widths[6]), F32),),
    )(x, mod, norm_w, cos_t, sin_t, *ws)


def _conv_taps(ext, rows):
    taps = []
    for j in range(CONV_K):
        sh = CONV_K - 1 - j
        rolled = pltpu.roll(ext, sh, 0) if sh else ext
        taps.append(rolled[8:8 + rows])
    return taps


def _dn_prep(qkv_pre, ba, conv_w8, alog_row, dtb_row, ts):
    s = qkv_pre.shape[0]
    cw = 3 * DN_WIDTH

    def body(pre_ref, halo_ref, ba_ref, cw_ref, al_ref, dtb_ref, q_ref, k_ref, v_ref, bg_ref):
        n = pl.program_id(0)
        prev = jnp.where(n == 0, 0.0, halo_ref[...])
        ext = jnp.concatenate([prev, pre_ref[...]], axis=0)
        taps = _conv_taps(ext, ts)
        conv = taps[0] * cw_ref[0:1, :]
        for j in range(1, CONV_K):
            conv = conv + taps[j] * cw_ref[j:j + 1, :]
        for h in range(DN_HEADS):
            cols = slice(h * DN_DIM, (h + 1) * DN_DIM)
            q_ref[:, cols] = _post_q(conv[:, h * DN_DIM:(h + 1) * DN_DIM])
            k_ref[:, cols] = _post_k(conv[:, DN_WIDTH + h * DN_DIM:DN_WIDTH + (h + 1) * DN_DIM])
            v_ref[:, cols] = _post_v(conv[:, 2 * DN_WIDTH + h * DN_DIM:2 * DN_WIDTH + (h + 1) * DN_DIM])
        bg = _beta_decay(ba_ref[...], al_ref[...], dtb_ref[...])
        lane = lax.broadcasted_iota(jnp.int32, bg.shape, 1)
        run = pltpu.roll(_chunk_cumsum(bg), DN_HEADS, 1)
        bg_ref[...] = jnp.where((lane >= GC_LANE) & (lane < GC_LANE + DN_HEADS), run, bg)

    tok = lambda w: pl.BlockSpec((ts, w), lambda i: (i, 0))
    full = lambda a: pl.BlockSpec(a.shape, lambda i: (0, 0))
    halo = pl.BlockSpec((8, cw), lambda i: (jnp.maximum(i * (ts // 8) - 1, 0), 0))
    return pl.pallas_call(
        body, name="dn_prep", grid=(s // ts,), compiler_params=_params("arbitrary"),
        in_specs=[tok(cw), halo, tok(BA_PAD), full(conv_w8), full(alog_row), full(dtb_row)],
        out_specs=(tok(DN_WIDTH), tok(DN_WIDTH), tok(DN_WIDTH), tok(BA_PAD)),
        out_shape=(SDS((s, DN_WIDTH), F32),) * 3 + (SDS((s, BA_PAD), F32),),
    )(qkv_pre, qkv_pre, ba, conv_w8, alog_row, dtb_row)


def _dn_chunk_prep(q, k, v, bg, ts):
    s = q.shape[0]
    ncs = ts // CHUNK

    def body(q_ref, k_ref, v_ref, bg_ref, u_ref, w_ref, qd_ref, kd_ref, p_ref, gl_ref, t_ref):
        def chunk(ci, carry):
            rows = pl.ds(pl.multiple_of(ci * CHUNK, CHUNK), CHUNK)
            rows8 = pl.ds(pl.multiple_of(ci * 8, 8), 8)
            bgc = bg_ref[rows, :]
            hs = range(DN_HEADS)
            sl = [slice(h * DN_DIM, (h + 1) * DN_DIM) for h in hs]
            outs = _chunk_fwd([q_ref[rows, c] for c in sl], [k_ref[rows, c] for c in sl],
                              [v_ref[rows, c] for c in sl], [bgc[:, h:h + 1] for h in hs],
                              [bgc[:, GC_LANE + h:GC_LANE + h + 1] for h in hs])
            for h, (u, w, p, qd, kd, gl, t) in enumerate(outs):
                u_ref[rows, sl[h]] = u
                w_ref[rows, sl[h]] = w
                qd_ref[rows, sl[h]] = qd
                kd_ref[rows, sl[h]] = kd
                p_ref[h, rows, :] = p
                t_ref[h, rows, :] = t
                gl_ref[rows8, sl[h]] = jnp.broadcast_to(gl, (8, DN_DIM))
            return carry

        lax.fori_loop(0, ncs, chunk, 0)

    tok = lambda w: pl.BlockSpec((ts, w), lambda i: (i, 0))
    sq = pl.BlockSpec((DN_HEADS, ts, CHUNK), lambda i: (0, i, 0))
    return pl.pallas_call(
        body, name="dn_chunk_prep", grid=(s // ts,), compiler_params=_params("arbitrary"),
        in_specs=[tok(DN_WIDTH)] * 3 + [tok(BA_PAD)],
        out_specs=(tok(DN_WIDTH),) * 4 + (sq, pl.BlockSpec((ncs * 8, DN_WIDTH), lambda i: (i, 0)), sq),
        out_shape=(SDS((s, DN_WIDTH), F32),) * 4 + (SDS((DN_HEADS, s, CHUNK), F32),
                                                     SDS((s // CHUNK * 8, DN_WIDTH), F32),
                                                     SDS((DN_HEADS, s, CHUNK), F32)),
    )(q, k, v, bg)


def _dn_scan(u, w, qd, kd, p, gl, ts):
    s = u.shape[0]
    ncs = ts // CHUNK

    def body(u_ref, w_ref, qd_ref, kd_ref, p_ref, gl_ref, o_ref, vn_ref, st_ref, state):
        @pl.when(pl.program_id(0) == 0)
        def _():
            state[...] = jnp.zeros_like(state)

        def chunk(ci, carry):
            rows = pl.ds(pl.multiple_of(ci * CHUNK, CHUNK), CHUNK)
            rows8 = pl.ds(pl.multiple_of(ci * 8, 8), 8)
            srows = pl.ds(pl.multiple_of(ci * DN_DIM, DN_DIM), DN_DIM)
            for h in range(DN_HEADS):
                cols = slice(h * DN_DIM, (h + 1) * DN_DIM)
                sf = state[h]
                st_ref[srows, cols] = sf
                vn = u_ref[rows, cols] - _nn(w_ref[rows, cols], sf)
                o_ref[rows, cols] = _nn(qd_ref[rows, cols], sf) + _nn(p_ref[h, rows, :], vn)
                vn_ref[rows, cols] = vn
                state[h] = sf * gl_ref[rows8, cols][0:1] + _tn(kd_ref[rows, cols], vn)
            return carry

        lax.fori_loop(0, ncs, chunk, 0)

    tok = lambda wd: pl.BlockSpec((ts, wd), lambda i: (i, 0))
    return pl.pallas_call(
        body, name="dn_scan", grid=(s // ts,), compiler_params=_params("arbitrary"),
        in_specs=[tok(DN_WIDTH)] * 4 + [pl.BlockSpec((DN_HEADS, ts, CHUNK), lambda i: (0, i, 0)),
                                        pl.BlockSpec((ncs * 8, DN_WIDTH), lambda i: (i, 0))],
        out_specs=(tok(DN_WIDTH), tok(DN_WIDTH), pl.BlockSpec((ncs * DN_DIM, DN_WIDTH), lambda i: (i, 0))),
        out_shape=(SDS((s, DN_WIDTH), F32), SDS((s, DN_WIDTH), F32), SDS((s // CHUNK * DN_DIM, DN_WIDTH), F32)),
        scratch_shapes=[pltpu.VMEM((DN_HEADS, DN_DIM, DN_DIM), F32)],
    )(u, w, qd, kd, p, gl)


def _attn_mask(first):
    qi = lax.broadcasted_iota(jnp.int32, (Q_BLOCK, 2 * Q_BLOCK), 0)
    kj = lax.broadcasted_iota(jnp.int32, (Q_BLOCK, 2 * Q_BLOCK), 1)
    rel = Q_BLOCK + qi - kj
    return (rel >= 0) & (rel <= W_SUB) & ((kj >= Q_BLOCK) | jnp.logical_not(first))


def _attn_combo(c, d):
    if d == 1:
        qs = pl.multiple_of(c * Q_BLOCK, Q_BLOCK)
        return qs, pl.multiple_of(ATT_BLK - Q_BLOCK + c * Q_BLOCK, Q_BLOCK), c == 0
    r, m = c % d, c // d
    qs = r + (d * Q_BLOCK) * m
    return qs, ATT_BLK + qs - d * Q_BLOCK, m == 0


def _rows(start, size, d):
    return pl.ds(start, size) if d == 1 else pl.ds(start, size, stride=d)


def _shift_in(ext, cur, n):
    @pl.when(n == 0)
    def _():
        ext[0:ATT_BLK, :] = jnp.zeros((ATT_BLK, LANES), F32)

    @pl.when(n > 0)
    def _():
        ext[0:ATT_BLK, :] = ext[ATT_BLK:2 * ATT_BLK, :]

    ext[ATT_BLK:2 * ATT_BLK, :] = cur


def _attn_fwd(qr, kr, vv):
    s = qr.shape[1]
    nblk = s // ATT_BLK
    scale = AT_DIM ** -0.5
    npat = len(DILATIONS)

    def body(q_ref, k_ref, v_ref, o_ref, lse_ref, kext, vext, o_p, l_p):
        n = pl.program_id(1)
        _shift_in(kext, k_ref[0], n)
        _shift_in(vext, v_ref[0], n)
        lo = lax.broadcasted_iota(jnp.int32, (Q_BLOCK, LANES), 1) < AT_DIM
        for pi, d in enumerate(DILATIONS):
            def combo(c, pi=pi, d=d):
                qs, ks, m0 = _attn_combo(c, d)
                mask = _attn_mask((n == 0) & m0)
                q = _bf(q_ref[0, _rows(qs, Q_BLOCK, d), :])
                kk = _bf(kext[_rows(ks, 2 * Q_BLOCK, d), :])
                vb = _bf(vext[_rows(ks, 2 * Q_BLOCK, d), :])
                outs, lses = [], []
                for sel in (lo, ~lo):
                    qm = jnp.where(sel, q, jnp.zeros_like(q))
                    sc = lax.dot_general(qm, kk, _NT, preferred_element_type=F32) * scale
                    sc = jnp.where(mask, sc, -1e30)
                    mx = jnp.max(sc, axis=-1, keepdims=True)
                    pr = jnp.exp(sc - mx)
                    l = jnp.sum(pr, axis=-1, keepdims=True)
                    outs.append(jnp.dot(_bf(pr), vb, preferred_element_type=F32) / l)
                    lses.append(mx + jnp.log(l))
                return qs, jnp.where(lo, outs[0], outs[1]), jnp.where(lo, lses[0], lses[1])

            def group(g, carry, pi=pi, d=d, combo=combo):
                res = [combo(g * ATT_UNROLL + u) for u in range(ATT_UNROLL)]
                for qs, o, l in res:
                    o_p[pi, _rows(qs, Q_BLOCK, d), :] = o
                    l_p[pi, _rows(qs, Q_BLOCK, d), :] = l
                return carry

            lax.fori_loop(0, ATT_BLK // Q_BLOCK // ATT_UNROLL, group, 0)

        def merge(i, carry):
            rows = pl.ds(pl.multiple_of(i * 256, 256), 256)
            ls = [l_p[pi, rows, :] for pi in range(npat)]
            mx = jnp.maximum(jnp.maximum(ls[0], ls[1]), ls[2])
            es = [jnp.exp(l - mx) for l in ls]
            den = es[0] + es[1] + es[2]
            o_ref[0, rows, :] = (es[0] * o_p[0, rows, :] + es[1] * o_p[1, rows, :] + es[2] * o_p[2, rows, :]) / den
            lse_ref[0, rows, :] = mx + jnp.log(den)
            return carry

        lax.fori_loop(0, ATT_BLK // 256, merge, 0)

    blk = pl.BlockSpec((1, ATT_BLK, LANES), lambda j, n: (j, n, 0))
    return pl.pallas_call(
        body, name="attn_fwd", grid=(AT_PAIRS, nblk), compiler_params=_params("arbitrary", "arbitrary"),
        in_specs=[blk] * 3, out_specs=(blk, blk),
        out_shape=(SDS((AT_PAIRS, s, LANES), F32),) * 2,
        scratch_shapes=[pltpu.VMEM((2 * ATT_BLK, LANES), F32), pltpu.VMEM((2 * ATT_BLK, LANES), F32),
                        pltpu.VMEM((npat, ATT_BLK, LANES), F32), pltpu.VMEM((npat, ATT_BLK, LANES), F32)],
    )(qr, kr, vv)


def _mix_prep(o_dn, z_dn, o_at, z_at, dnw, atw2, ts):
    s = o_dn.shape[0]

    def body(odn, zdn, oat, zat, dnw_ref, atw_ref, cat_ref):
        for h in range(DN_HEADS):
            cols = slice(h * DN_DIM, (h + 1) * DN_DIM)
            cat_ref[:, cols] = _bf(_gate_dn(odn[:, cols], zdn[:, cols], dnw_ref[...]))
        for j in range(AT_PAIRS):
            cat_ref[:, DN_WIDTH + j * LANES:DN_WIDTH + (j + 1) * LANES] = _bf(
                _gate_at(oat[j], zat[:, j * LANES:(j + 1) * LANES], atw_ref[...], _hnn))

    tok = lambda w: pl.BlockSpec((ts, w), lambda i: (i, 0))
    full = lambda a: pl.BlockSpec(a.shape, lambda i: (0, 0))
    pairs = pl.BlockSpec((AT_PAIRS, ts, LANES), lambda i: (0, i, 0))
    return pl.pallas_call(
        body, name="mix_prep", grid=(s // ts,), compiler_params=_params("arbitrary"),
        in_specs=[tok(DN_WIDTH), tok(DN_WIDTH), pairs, tok(AT_WIDTH), full(dnw), full(atw2)],
        out_specs=tok(D_MODEL), out_shape=SDS((s, D_MODEL), BF16),
    )(o_dn, z_dn, o_at, z_at, dnw, atw2)


def _out_loss(cat, x, tgt, w_out, gate, fw, ts):
    s = x.shape[0]

    def body(cat_ref, x_ref, t_ref, w_ref, g_ref, fw_ref, dx2_ref, dcat_ref, gw_ref, dfw_ref, dgate_ref, loss_ref):
        @pl.when(pl.program_id(0) == 0)
        def _():
            gw_ref[...] = jnp.zeros_like(gw_ref)
            dfw_ref[...] = jnp.zeros_like(dfw_ref)
            dgate_ref[...] = jnp.zeros_like(dgate_ref)
            loss_ref[...] = jnp.zeros_like(loss_ref)

        catb = cat_ref[...]
        wb = w_ref[...]
        gate, fwv = g_ref[...], fw_ref[...]
        mix = jnp.dot(catb, wb, preferred_element_type=F32)
        x2 = x_ref[...] + gate * mix
        r2 = lax.rsqrt(jnp.mean(x2 * x2, axis=-1, keepdims=True) + EPS)
        xn2 = x2 * r2
        err = xn2 * fwv - t_ref[...]
        row = jnp.sum(err * err, axis=-1, keepdims=True) * (1.0 / D_MODEL)
        loss_ref[...] += 0.5 * jnp.sum(row, axis=0, keepdims=True)
        dy = err * (1.0 / D_MODEL)
        dfw_ref[...] += jnp.sum(dy * xn2, axis=0, keepdims=True)
        dxn = dy * fwv
        dx2 = r2 * (dxn - xn2 * jnp.mean(dxn * xn2, axis=-1, keepdims=True))
        dx2_ref[...] = dx2
        dgate_ref[...] += jnp.sum(dx2 * mix, axis=0, keepdims=True)
        dmix = _bf(gate * dx2)
        dcat_ref[...] = lax.dot_general(dmix, wb, _NT, preferred_element_type=F32)
        gw_ref[...] += lax.dot_general(catb, dmix, _TN, preferred_element_type=F32)

    tok = lambda w: pl.BlockSpec((ts, w), lambda i: (i, 0))
    full = lambda a: pl.BlockSpec(a.shape, lambda i: (0, 0))
    row = pl.BlockSpec((1, D_MODEL), lambda i: (0, 0))
    return pl.pallas_call(
        body, name="out_loss", grid=(s // ts,), compiler_params=_params("arbitrary"),
        in_specs=[tok(D_MODEL), tok(D_MODEL), tok(D_MODEL), full(w_out), full(gate), full(fw)],
        out_specs=(tok(D_MODEL), tok(D_MODEL), pl.BlockSpec((D_MODEL, D_MODEL), lambda i: (0, 0)), row, row,
                   pl.BlockSpec((1, 1), lambda i: (0, 0))),
        out_shape=(SDS((s, D_MODEL), F32), SDS((s, D_MODEL), F32), SDS((D_MODEL, D_MODEL), F32),
                   SDS((1, D_MODEL), F32), SDS((1, D_MODEL), F32), SDS((1, 1), F32)),
    )(cat, x, tgt, w_out, gate, fw)


def _mix_bwd(dcat, o_dn, z_dn, o_at, z_at, dnw, atw2, ts):
    s = dcat.shape[0]

    def body(dcat_ref, odn, zdn, oat, zat, dnw_ref, atw_ref, dodn, dzdn, doat, dzat, delta, ddnw, datw):
        @pl.when(pl.program_id(0) == 0)
        def _():
            ddnw[...] = jnp.zeros_like(ddnw)
            datw[...] = jnp.zeros_like(datw)

        for h in range(DN_HEADS):
            cols = slice(h * DN_DIM, (h + 1) * DN_DIM)
            _, vjp = jax.vjp(_gate_dn, odn[:, cols], zdn[:, cols], dnw_ref[...])
            do, dz, dw = vjp(dcat_ref[:, cols])
            dodn[:, cols] = do
            dzdn[:, cols] = _bf(dz)
            ddnw[...] += dw
        for j in range(AT_PAIRS):
            cols = slice(j * LANES, (j + 1) * LANES)
            o = oat[j]
            _, vjp = jax.vjp(functools.partial(_gate_at, hnn=_d_hnn), o, zat[:, cols], atw_ref[...])
            do, dz, dw = vjp(dcat_ref[:, DN_WIDTH + j * LANES:DN_WIDTH + (j + 1) * LANES])
            doat[j] = do
            dzat[:, cols] = _bf(dz)
            datw[...] += dw
            delta[j] = _hnn(do * o, _group_ones(1.0))

    tok = lambda w: pl.BlockSpec((ts, w), lambda i: (i, 0))
    full = lambda a: pl.BlockSpec(a.shape, lambda i: (0, 0))
    row = pl.BlockSpec((1, LANES), lambda i: (0, 0))
    pairs = pl.BlockSpec((AT_PAIRS, ts, LANES), lambda i: (0, i, 0))
    return pl.pallas_call(
        body, name="mix_bwd", grid=(s // ts,), compiler_params=_params("arbitrary"),
        in_specs=[tok(D_MODEL), tok(DN_WIDTH), tok(DN_WIDTH), pairs, tok(AT_WIDTH), full(dnw), full(atw2)],
        out_specs=(tok(DN_WIDTH), tok(DN_WIDTH), pairs, tok(AT_WIDTH), pairs, row, row),
        out_shape=(SDS((s, DN_WIDTH), F32), SDS((s, DN_WIDTH), BF16), SDS((AT_PAIRS, s, LANES), F32),
                   SDS((s, AT_WIDTH), BF16), SDS((AT_PAIRS, s, LANES), F32), SDS((1, LANES), F32),
                   SDS((1, LANES), F32)),
    )(dcat, o_dn, z_dn, o_at, z_at, dnw, atw2)


def _shift_acc(ext, n):
    @pl.when(n == 0)
    def _():
        ext[0:ATT_BLK, :] = jnp.zeros((ATT_BLK, LANES), F32)

    @pl.when(n > 0)
    def _():
        ext[0:ATT_BLK, :] = ext[ATT_BLK:2 * ATT_BLK, :]

    ext[ATT_BLK:2 * ATT_BLK, :] = jnp.zeros((ATT_BLK, LANES), F32)


def _attn_bwd(qr, kr, vv, do, lse, delta):
    s = qr.shape[1]
    nblk = s // ATT_BLK
    scale = AT_DIM ** -0.5

    def body(q_ref, k_ref, v_ref, do_ref, lse_ref, dl_ref, dq_ref, dk_ref, dv_ref, kext, vext, dkext, dvext):
        n = pl.program_id(1)
        _shift_in(kext, k_ref[0], n)
        _shift_in(vext, v_ref[0], n)
        _shift_acc(dkext, n)
        _shift_acc(dvext, n)

        @pl.when(n < nblk)
        def _():
            dq_ref[0] = jnp.zeros((ATT_BLK, LANES), F32)
            lo = lax.broadcasted_iota(jnp.int32, (Q_BLOCK, LANES), 1) < AT_DIM
            for d in DILATIONS:
                def combo(c, d=d):
                    qs, ks, m0 = _attn_combo(c, d)
                    mask = _attn_mask((n == 0) & m0)
                    qrows, krows = _rows(qs, Q_BLOCK, d), _rows(ks, 2 * Q_BLOCK, d)
                    q = _bf(q_ref[0, qrows, :])
                    kk = _bf(kext[krows, :])
                    vb = _bf(vext[krows, :])
                    dob = _bf(do_ref[0, qrows, :])
                    lse2, dl2 = lse_ref[0, qrows, :], dl_ref[0, qrows, :]
                    dkk = jnp.zeros((2 * Q_BLOCK, LANES), F32)
                    dvv = jnp.zeros((2 * Q_BLOCK, LANES), F32)
                    dqs = []
                    for sel in (lo, ~lo):
                        qm = jnp.where(sel, q, jnp.zeros_like(q))
                        dom = jnp.where(sel, dob, jnp.zeros_like(dob))
                        lse_c = jnp.max(jnp.where(sel, lse2, -jnp.inf), axis=-1, keepdims=True)
                        dl_c = jnp.max(jnp.where(sel, dl2, -jnp.inf), axis=-1, keepdims=True)
                        sc = lax.dot_general(qm, kk, _NT, preferred_element_type=F32) * scale
                        pr = jnp.where(mask, jnp.exp(jnp.where(mask, sc - lse_c, 0.0)), 0.0)
                        dp = lax.dot_general(dom, vb, _NT, preferred_element_type=F32)
                        ds = _bf(pr * (dp - dl_c) * scale)
                        dqs.append(jnp.dot(ds, kk, preferred_element_type=F32))
                        dkk = dkk + lax.dot_general(ds, qm, _TN, preferred_element_type=F32)
                        dvv = dvv + lax.dot_general(_bf(pr), dom, _TN, preferred_element_type=F32)
                    return qrows, krows, jnp.where(lo, dqs[0], dqs[1]), dkk, dvv

                def group(g, carry, combo=combo):
                    res = [combo(g * ATT_UNROLL_BWD + u) for u in range(ATT_UNROLL_BWD)]
                    for qrows, krows, dq, dkk, dvv in res:
                        dq_ref[0, qrows, :] += dq
                        dkext[krows, :] += dkk
                        dvext[krows, :] += dvv
                    return carry

                lax.fori_loop(0, ATT_BLK // Q_BLOCK // ATT_UNROLL_BWD, group, 0)

        dk_ref[0] = dkext[0:ATT_BLK, :]
        dv_ref[0] = dvext[0:ATT_BLK, :]

    cur = pl.BlockSpec((1, ATT_BLK, LANES), lambda j, n: (j, jnp.minimum(n, nblk - 1), 0))
    done = pl.BlockSpec((1, ATT_BLK, LANES), lambda j, n: (j, jnp.maximum(n - 1, 0), 0))
    return pl.pallas_call(
        body, name="attn_bwd", grid=(AT_PAIRS, nblk + 1), compiler_params=_params("arbitrary", "arbitrary"),
        in_specs=[cur] * 6, out_specs=(cur, done, done),
        out_shape=(SDS((AT_PAIRS, s, LANES), F32),) * 3,
        scratch_shapes=[pltpu.VMEM((2 * ATT_BLK, LANES), F32)] * 4,
    )(qr, kr, vv, do, lse, delta)


def _rope_bwd(dq, dk, dv, cos_t, sin_t, ts):
    s = cos_t.shape[0]

    def body(q_ref, k_ref, v_ref, cos_ref, sin_ref, oq, ok, ov):
        cs, sn = cos_ref[...], sin_ref[...]
        for j in range(AT_PAIRS):
            cols = slice(j * LANES, (j + 1) * LANES)
            for g_ref, o_ref in ((q_ref, oq), (k_ref, ok)):
                g = g_ref[j]
                o_ref[:, cols] = _bf(g * cs + _swap_half64(g * sn))
            ov[:, cols] = _bf(v_ref[j])

    tok = lambda w: pl.BlockSpec((ts, w), lambda i: (i, 0))
    pairs = pl.BlockSpec((AT_PAIRS, ts, LANES), lambda i: (0, i, 0))
    return pl.pallas_call(
        body, name="rope_bwd", grid=(s // ts,), compiler_params=_params("arbitrary"),
        in_specs=[pairs] * 3 + [tok(LANES)] * 2, out_specs=(tok(AT_WIDTH),) * 3,
        out_shape=(SDS((s, AT_WIDTH), BF16),) * 3,
    )(dq, dk, dv, cos_t, sin_t)


def _dn_scan_bwd(do, st, vn, w, qd, kd, p, gl, ts):
    s = do.shape[0]
    ncs = ts // CHUNK
    nt = s // ts

    def body(do_ref, st_ref, vn_ref, w_ref, qd_ref, kd_ref, p_ref, gl_ref,
             du_ref, dw_ref, dqd_ref, dkd_ref, dp_ref, dgl_ref, dstate):
        @pl.when(pl.program_id(0) == 0)
        def _():
            dstate[...] = jnp.zeros_like(dstate)

        def chunk(jr, carry):
            ci = ncs - 1 - jr
            rows = pl.ds(pl.multiple_of(ci * CHUNK, CHUNK), CHUNK)
            rows8 = pl.ds(pl.multiple_of(ci * 8, 8), 8)
            srows = pl.ds(pl.multiple_of(ci * DN_DIM, DN_DIM), DN_DIM)
            for h in range(DN_HEADS):
                cols = slice(h * DN_DIM, (h + 1) * DN_DIM)
                ds_, sf = dstate[h], st_ref[srows, cols]
                vnc, doc, wc, qdc, kdc, pc = (vn_ref[rows, cols], do_ref[rows, cols], w_ref[rows, cols],
                                              qd_ref[rows, cols], kd_ref[rows, cols], p_ref[h, rows, :])
                dvn = _nn(kdc, ds_) + _tn(pc, doc)
                du_ref[rows, cols] = dvn
                dw_ref[rows, cols] = -_nt(dvn, sf)
                dqd_ref[rows, cols] = _nt(doc, sf)
                dkd_ref[rows, cols] = _nt(vnc, ds_)
                dp_ref[h, rows, :] = _nt(doc, vnc)
                dgl = jnp.sum(jnp.sum(ds_ * sf, axis=1, keepdims=True), axis=0, keepdims=True)
                dgl_ref[rows8, cols] = jnp.broadcast_to(dgl, (8, DN_DIM))
                dstate[h] = ds_ * gl_ref[rows8, cols][0:1] + _tn(qdc, doc) - _tn(wc, dvn)
            return carry

        lax.fori_loop(0, ncs, chunk, 0)

    tok = lambda wd: pl.BlockSpec((ts, wd), lambda i: (nt - 1 - i, 0))
    pspec = pl.BlockSpec((DN_HEADS, ts, CHUNK), lambda i: (0, nt - 1 - i, 0))
    g8 = pl.BlockSpec((ncs * 8, DN_WIDTH), lambda i: (nt - 1 - i, 0))
    return pl.pallas_call(
        body, name="dn_scan_bwd", grid=(nt,), compiler_params=_params("arbitrary"),
        in_specs=[tok(DN_WIDTH), pl.BlockSpec((ncs * DN_DIM, DN_WIDTH), lambda i: (nt - 1 - i, 0))]
        + [tok(DN_WIDTH)] * 4 + [pspec, g8],
        out_specs=(tok(DN_WIDTH),) * 4 + (pspec, g8),
        out_shape=(SDS((s, DN_WIDTH), F32),) * 4 + (SDS((DN_HEADS, s, CHUNK), F32),
                                                     SDS((s // CHUNK * 8, DN_WIDTH), F32)),
        scratch_shapes=[pltpu.VMEM((DN_HEADS, DN_DIM, DN_DIM), F32)],
    )(do, st, vn, w, qd, kd, p, gl)


def _dn_chunk_bwd(q, k, v, bg, t, du, dw, dqd, dkd, dp, dgl, ts):
    s = q.shape[0]
    ncs = ts // CHUNK

    def body(q_ref, k_ref, v_ref, bg_ref, t_ref, du_ref, dw_ref, dqd_ref, dkd_ref, dp_ref, dgl_ref,
             dq_ref, dk_ref, dv_ref, dbg_ref):
        def chunk(ci, carry):
            rows = pl.ds(pl.multiple_of(ci * CHUNK, CHUNK), CHUNK)
            rows8 = pl.ds(pl.multiple_of(ci * 8, 8), 8)
            bgc = bg_ref[rows, :]
            lane = lax.broadcasted_iota(jnp.int32, (CHUNK, BA_PAD), 1)
            hs = range(DN_HEADS)
            sl = [slice(h * DN_DIM, (h + 1) * DN_DIM) for h in hs]
            cots = [(du_ref[rows, c], dw_ref[rows, c], dp_ref[h, rows, :], dqd_ref[rows, c], dkd_ref[rows, c],
                     dgl_ref[rows8, c][0:1, 0:1]) for h, c in zip(hs, sl)]
            outs = _chunk_bwd([q_ref[rows, c] for c in sl], [k_ref[rows, c] for c in sl],
                              [v_ref[rows, c] for c in sl], [bgc[:, h:h + 1] for h in hs],
                              [bgc[:, GC_LANE + h:GC_LANE + h + 1] for h in hs],
                              [t_ref[h, rows, :] for h in hs], cots)
            dbg = jnp.zeros((CHUNK, BA_PAD), F32)
            for h, (dq, dk, dv, dbeta, dgc) in enumerate(outs):
                dq_ref[rows, sl[h]] = dq
                dk_ref[rows, sl[h]] = dk
                dv_ref[rows, sl[h]] = dv
                dbg = dbg + jnp.where(lane == h, dbeta, 0.0) + jnp.where(lane == GC_LANE + h, dgc, 0.0)
            dbg_ref[rows, :] = dbg
            return carry

        lax.fori_loop(0, ncs, chunk, 0)

    tok = lambda wd: pl.BlockSpec((ts, wd), lambda i: (i, 0))
    pspec = pl.BlockSpec((DN_HEADS, ts, CHUNK), lambda i: (0, i, 0))
    g8 = pl.BlockSpec((ncs * 8, DN_WIDTH), lambda i: (i, 0))
    return pl.pallas_call(
        body, name="dn_chunk_bwd", grid=(s // ts,), compiler_params=_params("arbitrary"),
        in_specs=[tok(DN_WIDTH)] * 3 + [tok(BA_PAD), pspec] + [tok(DN_WIDTH)] * 4 + [pspec, g8],
        out_specs=(tok(DN_WIDTH),) * 3 + (tok(BA_PAD),),
        out_shape=(SDS((s, DN_WIDTH), F32),) * 3 + (SDS((s, BA_PAD), F32),),
    )(q, k, v, bg, t, du, dw, dqd, dkd, dp, dgl)


def _dn_prep_bwd(qkv_pre, ba, dq, dk, dv, dbg, conv_w8, alog_row, dtb_row, ts):
    s = qkv_pre.shape[0]
    cw = 3 * DN_WIDTH
    nt = s // ts

    def body(pre_ref, ph_ref, nh_ref, ba_ref, dq_ref, dqh_ref, dk_ref, dkh_ref, dv_ref, dvh_ref, dbg_ref,
             cw_ref, al_ref, dtb_ref, dpre_ref, dba_ref, dcw_ref, dal_ref, ddtb_ref):
        n = pl.program_id(0)

        @pl.when(n == 0)
        def _():
            dcw_ref[...] = jnp.zeros_like(dcw_ref)
            dal_ref[...] = jnp.zeros_like(dal_ref)
            ddtb_ref[...] = jnp.zeros_like(ddtb_ref)

        last = n == nt - 1
        prev = jnp.where(n == 0, 0.0, ph_ref[...])
        ext = jnp.concatenate([prev, pre_ref[...], nh_ref[...]], axis=0)
        taps = _conv_taps(ext, ts + 8)
        conv = taps[0] * cw_ref[0:1, :]
        for j in range(1, CONV_K):
            conv = conv + taps[j] * cw_ref[j:j + 1, :]

        def cot(main, halo, cols):
            return jnp.concatenate([main[:, cols], jnp.where(last, 0.0, halo[:, cols])], axis=0)

        pieces = []
        for grp, (fn, mref, href) in enumerate(((_post_q, dq_ref, dqh_ref), (_post_k, dk_ref, dkh_ref),
                                                (_post_v, dv_ref, dvh_ref))):
            for h in range(DN_HEADS):
                cols = slice(h * DN_DIM, (h + 1) * DN_DIM)
                c0 = grp * DN_WIDTH + h * DN_DIM
                _, vjp = jax.vjp(fn, conv[:, c0:c0 + DN_DIM])
                pieces.append(vjp(cot(mref, href, cols))[0])
        dconv = jnp.concatenate(pieces, axis=1)
        rows = ts + 8
        dpre = dconv[:ts] * cw_ref[CONV_K - 1:CONV_K, :]
        for j in range(CONV_K - 1):
            sh = CONV_K - 1 - j
            dpre = dpre + pltpu.roll(dconv, rows - sh, 0)[:ts] * cw_ref[j:j + 1, :]
        dpre_ref[...] = _bf(dpre)
        for j in range(CONV_K):
            dcw_ref[j:j + 1, :] += jnp.sum(dconv[:ts] * taps[j][:ts], axis=0, keepdims=True)

        dbg = dbg_ref[...]
        lane = lax.broadcasted_iota(jnp.int32, dbg.shape, 1)
        dg = pltpu.roll(_chunk_cumsum(dbg, reverse=True), BA_PAD - DN_HEADS, 1)
        cot_bg = jnp.where(lane < DN_HEADS, dbg, jnp.where(lane < GC_LANE, dg, 0.0))
        _, vjp = jax.vjp(_beta_decay, ba_ref[...], al_ref[...], dtb_ref[...])
        dba, dal, ddtb = vjp(cot_bg)
        dba_ref[...] = _bf(dba)
        dal_ref[...] += dal
        ddtb_ref[...] += ddtb

    tok = lambda w: pl.BlockSpec((ts, w), lambda i: (i, 0))
    full = lambda a: pl.BlockSpec(a.shape, lambda i: (0, 0))
    prevh = lambda w: pl.BlockSpec((8, w), lambda i: (jnp.maximum(i * (ts // 8) - 1, 0), 0))
    nexth = lambda w: pl.BlockSpec((8, w), lambda i: (jnp.minimum((i + 1) * (ts // 8), s // 8 - 1), 0))
    row = pl.BlockSpec((1, LANES), lambda i: (0, 0))
    return pl.pallas_call(
        body, name="dn_prep_bwd", grid=(nt,), compiler_params=_params("arbitrary"),
        in_specs=[tok(cw), prevh(cw), nexth(cw), tok(BA_PAD),
                  tok(DN_WIDTH), nexth(DN_WIDTH), tok(DN_WIDTH), nexth(DN_WIDTH), tok(DN_WIDTH), nexth(DN_WIDTH),
                  tok(BA_PAD), full(conv_w8), full(alog_row), full(dtb_row)],
        out_specs=(tok(cw), tok(BA_PAD), pl.BlockSpec((8, cw), lambda i: (0, 0)), row, row),
        out_shape=(SDS((s, cw), BF16), SDS((s, BA_PAD), BF16), SDS((8, cw), F32), SDS((1, LANES), F32),
                   SDS((1, LANES), F32)),
    )(qkv_pre, qkv_pre, qkv_pre, ba, dq, dq, dk, dk, dv, dv, dbg, conv_w8, alog_row, dtb_row)


def _dh_dx(dps, ws, x, mod, norm_w, dx2, ts):
    s = x.shape[0]
    widths = [w.shape[1] for w in ws]
    np_ = len(ws)

    def body(*refs):
        dp_refs, w_refs = refs[:np_], refs[np_:2 * np_]
        x_ref, mod_ref, nw_ref, dx2_ref, gx_ref, dshift, dscale, dnw = refs[2 * np_:]

        @pl.when(pl.program_id(0) == 0)
        def _():
            dshift[...] = jnp.zeros_like(dshift)
            dscale[...] = jnp.zeros_like(dscale)
            dnw[...] = jnp.zeros_like(dnw)

        dh = lax.dot_general(dp_refs[0][...], w_refs[0][...], _NT, preferred_element_type=F32)
        for a, b in zip(dp_refs[1:], w_refs[1:]):
            dh = dh + lax.dot_general(a[...], b[...], _NT, preferred_element_type=F32)
        xt = x_ref[...]
        r = lax.rsqrt(jnp.mean(xt * xt, axis=-1, keepdims=True) + EPS)
        xn = xt * r
        nw = nw_ref[...]
        sc1 = 1.0 + mod_ref[:, D_MODEL:2 * D_MODEL]
        dshift[...] += jnp.sum(dh, axis=0, keepdims=True)
        dscale[...] += jnp.sum(dh * (xn * nw), axis=0, keepdims=True)
        dnw[...] += jnp.sum(dh * sc1 * xn, axis=0, keepdims=True)
        dxn = dh * sc1 * nw
        gx_ref[...] = r * (dxn - xn * jnp.mean(dxn * xn, axis=-1, keepdims=True)) + dx2_ref[...]

    tok = lambda w: pl.BlockSpec((ts, w), lambda i: (i, 0))
    full = lambda a: pl.BlockSpec(a.shape, lambda i: (0, 0))
    row = pl.BlockSpec((1, D_MODEL), lambda i: (0, 0))
    return pl.pallas_call(
        body, name="dh_dx", grid=(s // ts,), compiler_params=_params("arbitrary"),
        in_specs=[tok(w) for w in widths] + [full(w) for w in ws] + [tok(D_MODEL), full(mod), full(norm_w),
                                                                    tok(D_MODEL)],
        out_specs=(tok(D_MODEL), row, row, row),
        out_shape=(SDS((s, D_MODEL), F32),) + (SDS((1, D_MODEL), F32),) * 3,
    )(*dps, *ws, x, mod, norm_w, dx2)


def _grad_w_in(h, dps, ts, name):
    s = h.shape[0]
    widths = [p.shape[1] for p in dps]
    np_ = len(dps)

    def body(*refs):
        h_ref, dp_refs, outs = refs[0], refs[1:1 + np_], refs[1 + np_:]

        @pl.when(pl.program_id(0) == 0)
        def _():
            for o in outs:
                o[...] = jnp.zeros_like(o)

        hb = h_ref[...]
        for p, o in zip(dp_refs, outs):
            o[...] += lax.dot_general(hb, p[...], _TN, preferred_element_type=F32)

    tok = lambda w: pl.BlockSpec((ts, w), lambda i: (i, 0))
    return pl.pallas_call(
        body, name=name, grid=(s // ts,), compiler_params=_params("arbitrary"),
        in_specs=[tok(D_MODEL)] + [tok(w) for w in widths],
        out_specs=tuple(pl.BlockSpec((D_MODEL, w), lambda i: (0, 0)) for w in widths),
        out_shape=tuple(SDS((D_MODEL, w), F32) for w in widths),
    )(h, *dps)


def _adamw_math(w, g, m, v):
    m = ADAM_B1 * m + (1.0 - ADAM_B1) * g
    v = ADAM_B2 * v + (1.0 - ADAM_B2) * (g * g)
    m_hat = m / (1.0 - ADAM_B1 ** ADAM_STEP)
    v_hat = v / (1.0 - ADAM_B2 ** ADAM_STEP)
    delta = -ADAM_LR * (m_hat / (jnp.sqrt(v_hat) + ADAM_EPS) + ADAM_WD * w)
    return delta, m, v


def _adamw(w, m, v, g, name, slots=False):
    def body(w_ref, m_ref, v_ref, g_ref, g_out, d_out, m_out, v_out):
        if slots:
            g = g_ref[0].astype(F32)
            for k in range(1, N_DEV):
                g = g + g_ref[k].astype(F32)
        else:
            g = g_ref[...]
        g_out[...] = g
        d_out[...], m_out[...], v_out[...] = _adamw_math(w_ref[...], g, m_ref[...], v_ref[...])

    return pl.pallas_call(body, name=name, compiler_params=_params(),
                          out_shape=(SDS(w.shape, F32),) * 4)(w, m, v, g)


def _adamw_w_mod(w, m, v, siluc_all, dmod_mine):
    def body(w_ref, m_ref, v_ref, sc_ref, dm_ref, g_out, d_out, m_out, v_out):
        g = _htn(sc_ref[...], dm_ref[...])
        g_out[...] = g
        d_out[...], m_out[...], v_out[...] = _adamw_math(w_ref[...], g, m_ref[...], v_ref[...])

    return pl.pallas_call(body, name="adamw_w_mod", compiler_params=_params(),
                          out_shape=(SDS(w.shape, F32),) * 4)(w, m, v, siluc_all, dmod_mine)


def _pack_sum(pack_all):
    def body(p_ref, o_ref):
        t = p_ref[0]
        for k in range(1, N_DEV):
            t = t + p_ref[k]
        o_ref[...] = t

    return pl.pallas_call(body, name="pack_sum", out_shape=SDS(pack_all.shape[1:], F32))(pack_all)


def _tile(s, want):
    t = min(want, s)
    assert s % t == 0
    return t


def _local_step(x, c, positions, w_mod_bf, b_mod, norm_w, w_in_bf, conv_w, a_log, dt_bias, dn_norm_w, at_norm_w,
                w_out_bf, final_norm_w, tgt):
    s = x.shape[0]
    o = [0]
    for wdt in IN_SPLITS:
        o.append(o[-1] + wdt)
    w_ba = jnp.pad(w_in_bf[:, o[2]:o[4]], ((0, 0), (0, BA_PAD - 2 * DN_HEADS)))
    ws = [w_in_bf[:, o[0]:o[1]], w_in_bf[:, o[1]:o[2]], w_ba, w_in_bf[:, o[4]:o[5]], w_in_bf[:, o[5]:o[6]],
          w_in_bf[:, o[6]:o[7]], w_in_bf[:, o[7]:o[8]]]
    conv_w8 = jnp.pad(conv_w, ((0, 8 - CONV_K), (0, 0)))
    alog_row = jnp.pad(a_log, ((0, 0), (DN_HEADS, BA_PAD - 2 * DN_HEADS)))
    dtb_row = jnp.pad(dt_bias, ((0, 0), (DN_HEADS, BA_PAD - 2 * DN_HEADS)))
    atw2 = jnp.concatenate([at_norm_w, at_norm_w], axis=1)

    half = AT_DIM // 2
    inv_freq = ROPE_THETA ** (-jnp.arange(half, dtype=F32) / half)
    ang = positions.astype(F32)[:, None] * inv_freq
    cos, sin = jnp.cos(ang), jnp.sin(ang)
    cos_t = jnp.concatenate([cos, cos, cos, cos], axis=1)
    sin_t = jnp.concatenate([-sin, sin, -sin, sin], axis=1)

    mod, siluc = _adaln_mod(c, w_mod_bf, b_mod)
    gate = mod[:, 2 * D_MODEL:]
    hbf, qkv_pre, z_dn, ba, qr, kr, vb, z_at = _ln_proj(x, mod, norm_w, ws, cos_t, sin_t, _tile(s, 256))
    q, k, v, bg = _dn_prep(qkv_pre, ba, conv_w8, alog_row, dtb_row, _tile(s, 256))
    u, w, qd, kd, p, gl, tinv = _dn_chunk_prep(q, k, v, bg, _tile(s, 512))
    o_dn, vn, st = _dn_scan(u, w, qd, kd, p, gl, _tile(s, 512))
    o_at, lse = _attn_fwd(qr, kr, vb)
    cat = _mix_prep(o_dn, z_dn, o_at, z_at, dn_norm_w, atw2, _tile(s, 512))
    dx2, dcat, gw_out, dfw, dgate, loss = _out_loss(cat, x, tgt, w_out_bf, gate, final_norm_w, _tile(s, 512))

    do_dn, dz_dn, do_at, dz_at, delta, ddnw, datw = _mix_bwd(dcat, o_dn, z_dn, o_at, z_at, dn_norm_w, atw2,
                                                             _tile(s, 512))
    daq, dak, dav = _rope_bwd(*_attn_bwd(qr, kr, vb, do_at, lse, delta), cos_t, sin_t, _tile(s, 512))
    du, dw, dqd, dkd, dp, dgl = _dn_scan_bwd(do_dn, st, vn, w, qd, kd, p, gl, _tile(s, 512))
    dq, dk, dv, dbg = _dn_chunk_bwd(q, k, v, bg, tinv, du, dw, dqd, dkd, dp, dgl, _tile(s, 512))
    dqkv, dba, dcw, dal, ddtb = _dn_prep_bwd(qkv_pre, ba, dq, dk, dv, dbg, conv_w8, alog_row, dtb_row, _tile(s, 256))
    dps = [dqkv, dz_dn, dba, daq, dak, dav, dz_at]
    gx, dshift, dscale, dnw = _dh_dx(dps, ws, x, mod, norm_w, dx2, _tile(s, 256))
    g_qkv, g_z, g_ba = _grad_w_in(hbf, dps[:3], _tile(s, 512), "grad_w_in_dn")
    g_aq, g_ak, g_av, g_az = _grad_w_in(hbf, dps[3:], _tile(s, 512), "grad_w_in_at")
    gw_in = jnp.concatenate([g_qkv, g_z, g_ba[:, :2 * DN_HEADS], g_aq, g_ak, g_av, g_az], axis=1)
    dmod = jnp.concatenate([dshift, dscale, dgate], axis=1)
    small = dict(conv=dcw[:CONV_K], dmod=dmod, siluc=siluc, dnw=dnw, dfw=dfw, alog=dal, dtb=ddtb, dnn=ddnw, atn=datw)
    return loss, gx, gw_in, gw_out, small


def kernel(x, c, positions, w_mod, b_mod, norm_w, w_in, conv_w, a_log, dt_bias, dn_norm_w, at_norm_w, w_out, final_norm_w, loss_target, m_w_mod, m_b_mod, m_norm_w, m_w_in, m_conv_w, m_a_log, m_dt_bias, m_dn_norm_w, m_at_norm_w, m_w_out, m_final_norm_w, v_w_mod, v_b_mod, v_norm_w, v_w_in, v_conv_w, v_a_log, v_dt_bias, v_dn_norm_w, v_at_norm_w, v_w_out, v_final_norm_w):
    me = 4 * lax.axis_index("x") + 2 * lax.axis_index("y") + lax.axis_index("c")
    s = x.shape[1]

    g_mod, g_in, g_conv, g_out = _exchange(
        [_bf(w_mod[0]), _bf(w_in[0]), conv_w[0], _bf(w_out[0])], [False] * 4, "gather_weights")
    w_mod_bf = g_mod.transpose(1, 0, 2).reshape(D_MODEL, 3 * D_MODEL)
    w_in_bf = g_in.transpose(1, 0, 2).reshape(D_MODEL, IN_COLS)
    conv_full = g_conv.transpose(1, 0, 2).reshape(CONV_K, 3 * DN_WIDTH)
    w_out_bf = g_out.reshape(D_MODEL, D_MODEL)

    loss, gx, gw_in, gw_out, small = _local_step(
        x[0], c, positions[0], w_mod_bf, b_mod, norm_w, w_in_bf, conv_full, a_log, dt_bias, dn_norm_w, at_norm_w,
        w_out_bf, final_norm_w.reshape(1, D_MODEL), loss_target[0])

    pack = jnp.concatenate([small["conv"].reshape(1, -1), small["dmod"], small["siluc"], small["dnw"], small["dfw"],
                            small["alog"], small["dtb"], small["dnn"], small["atn"]], axis=1).reshape(PK_ROWS, LANES)
    gw_in_slabs = _bf(gw_in).reshape(D_MODEL, N_DEV, IN_SHARD).transpose(1, 0, 2)
    gw_out_slabs = _bf(gw_out).reshape(N_DEV, D_MODEL // N_DEV, D_MODEL)
    r_in, r_out, pack_all = _exchange([gw_in_slabs, gw_out_slabs, pack], [True, True, False], "exchange_grads")

    res = {}
    res["w_in"] = _adamw(w_in[0], m_w_in[0], v_w_in[0], r_in, "adamw_w_in", slots=True)
    res["w_out"] = _adamw(w_out[0], m_w_out[0], v_w_out[0], r_out, "adamw_w_out", slots=True)
    flat_all = pack_all.reshape(N_DEV, PK_END)
    dmod_mine = lax.dynamic_slice(flat_all, (0, PK_DMOD + me * (3 * D_MODEL // N_DEV)), (N_DEV, 3 * D_MODEL // N_DEV))
    res["w_mod"] = _adamw_w_mod(w_mod[0], m_w_mod[0], v_w_mod[0], flat_all[:, PK_SILUC:PK_DNW], dmod_mine)
    tot = _pack_sum(pack_all).reshape(1, PK_END)
    g_conv_full = tot[:, PK_CONV:PK_DMOD].reshape(CONV_K, 3 * DN_WIDTH)
    g_conv_mine = lax.dynamic_slice(g_conv_full, (0, me * (3 * DN_WIDTH // N_DEV)), (CONV_K, 3 * DN_WIDTH // N_DEV))
    res["conv_w"] = _adamw(conv_w[0], m_conv_w[0], v_conv_w[0], g_conv_mine, "adamw_conv_w")
    res["b_mod"] = _adamw(b_mod, m_b_mod, v_b_mod, tot[:, PK_DMOD:PK_SILUC], "adamw_b_mod")
    res["norm_w"] = _adamw(norm_w, m_norm_w, v_norm_w, tot[:, PK_DNW:PK_DFW], "adamw_norm_w")
    res["a_log"] = _adamw(a_log, m_a_log, v_a_log, tot[:, PK_ALOG + DN_HEADS:PK_ALOG + 2 * DN_HEADS], "adamw_a_log")
    res["dt_bias"] = _adamw(dt_bias, m_dt_bias, v_dt_bias, tot[:, PK_DTB + DN_HEADS:PK_DTB + 2 * DN_HEADS],
                            "adamw_dt_bias")
    res["dn_norm_w"] = _adamw(dn_norm_w, m_dn_norm_w, v_dn_norm_w, tot[:, PK_DNN:PK_ATN], "adamw_dn_norm_w")
    g_atn = tot[:, PK_ATN:PK_ATN + AT_DIM] + tot[:, PK_ATN + AT_DIM:PK_END]
    res["at_norm_w"] = _adamw(at_norm_w, m_at_norm_w, v_at_norm_w, g_atn, "adamw_at_norm_w")
    fin = _adamw(final_norm_w.reshape(1, D_MODEL), m_final_norm_w.reshape(1, D_MODEL),
                 v_final_norm_w.reshape(1, D_MODEL), tot[:, PK_DFW:PK_ALOG], "adamw_final_norm_w")
    res["final_norm_w"] = tuple(a.reshape(D_MODEL) for a in fin)

    lead = ("w_mod", "w_in", "conv_w", "w_out")
    names = ("w_mod", "b_mod", "norm_w", "w_in", "conv_w", "a_log", "dt_bias", "dn_norm_w", "at_norm_w", "w_out",
             "final_norm_w")
    out = [lax.psum(loss[0, 0], ("x", "y", "c")), gx.reshape(1, s, D_MODEL)]
    for kind in range(4):
        for nm in names:
            a = res[nm][kind]
            out.append(a[None] if nm in lead else a)
    return tuple(out)
```

```python
import functools

import jax
import jax.numpy as jnp
from jax import lax
from jax.experimental import pallas as pl
from jax.experimental.pallas import tpu as pltpu

F32, BF16 = jnp.float32, jnp.bfloat16
HI = lax.Precision.HIGHEST
SDS = jax.ShapeDtypeStruct

D_MODEL = 1024
DN_HEADS, DN_DIM, DN_WIDTH = 4, 128, 512
AT_HEADS, AT_DIM, AT_WIDTH = 8, 64, 512
CONV_K = 4
CHUNK = 64
Q_BLOCK = 128
W_SUB = 128
DILATIONS = (1, 4, 16)
AT_PAIRS = 4
ATT_BLK = Q_BLOCK * max(DILATIONS)
ATT_UNROLL, ATT_UNROLL_BWD = 4, 4
CH_UNROLL = 4
ROPE_THETA = 10000.0
EPS = 1e-6
N_DEV = 8
LANES = 128
BA_PAD = 128
IN_SPLITS = (1536, 512, 4, 4, 512, 512, 512, 512)
IN_COLS = sum(IN_SPLITS)
IN_SHARD = IN_COLS // N_DEV
VMEM_LIMIT = 56 * 2 ** 20

ADAM_LR, ADAM_B1, ADAM_B2, ADAM_EPS, ADAM_WD, ADAM_STEP = 0.001, 0.9, 0.999, 1e-08, 0.01, 10

PK_CONV, PK_DMOD, PK_SILUC, PK_DNW, PK_DFW, PK_ALOG, PK_DTB, PK_DNN, PK_ATN, PK_END = (
    0, 6144, 9216, 10240, 11264, 12288, 12416, 12544, 12672, 12800)
PK_ROWS = PK_END // LANES

_NT = (((1,), (1,)), ((), ()))
_TN = (((0,), (0,)), ((), ()))


def _params(*sem):
    return pltpu.CompilerParams(dimension_semantics=sem or None, vmem_limit_bytes=VMEM_LIMIT)


def _bf(x):
    return x.astype(BF16)


def _nn(a, b):
    return jnp.dot(_bf(a), _bf(b), preferred_element_type=F32)


def _nt(a, b):
    return lax.dot_general(_bf(a), _bf(b), _NT, preferred_element_type=F32)


def _tn(a, b):
    return lax.dot_general(_bf(a), _bf(b), _TN, preferred_element_type=F32)


def _hnn(a, b):
    return jnp.dot(a, b, precision=HI, preferred_element_type=F32)


def _hnt(a, b):
    return lax.dot_general(a, b, _NT, precision=HI, preferred_element_type=F32)


def _htn(a, b):
    return lax.dot_general(a, b, _TN, precision=HI, preferred_element_type=F32)


@jax.custom_vjp
def _d_hnn(a, b):
    return _hnn(a, b)


def _d_hnn_fwd(a, b):
    return _hnn(a, b), (a, b)


def _d_hnn_bwd(res, g):
    a, b = res
    return _hnt(g, b), _htn(a, g)


_d_hnn.defvjp(_d_hnn_fwd, _d_hnn_bwd)


def _silu(x):
    return x * jax.nn.sigmoid(x)


def _softplus(x):
    return jnp.maximum(x, 0.0) + jnp.log(1.0 + jnp.exp(-jnp.abs(x)))


def _l2n(x):
    return x * lax.rsqrt(jnp.sum(x * x, axis=-1, keepdims=True) + EPS)


def _post_q(x):
    return _l2n(_silu(x)) * (DN_DIM ** -0.5)


def _post_k(x):
    return _l2n(_silu(x))


def _post_v(x):
    return _silu(x)


def _beta_decay(ba, alog_row, dtb_row):
    lane = lax.broadcasted_iota(jnp.int32, ba.shape, 1)
    return jnp.where(lane < DN_HEADS, jax.nn.sigmoid(ba), -jnp.exp(alog_row) * _softplus(ba + dtb_row))


def _gate_dn(o, z, w):
    return (o * lax.rsqrt(jnp.mean(o * o, axis=-1, keepdims=True) + EPS)) * w * _silu(z)


def _group_ones(scale):
    r = lax.broadcasted_iota(jnp.int32, (LANES, LANES), 0)
    c = lax.broadcasted_iota(jnp.int32, (LANES, LANES), 1)
    return jnp.where((r // AT_DIM) == (c // AT_DIM), scale, 0.0).astype(F32)


def _gate_at(o, z, w2, hnn):
    ms = hnn(o * o, _group_ones(1.0 / AT_DIM))
    return (o * lax.rsqrt(ms + EPS)) * w2 * _silu(z)


def _swap_half64(x):
    lane = lax.broadcasted_iota(jnp.int32, x.shape, 1)
    return jnp.where((lane & (AT_DIM - 1)) < AT_DIM // 2, pltpu.roll(x, LANES - AT_DIM // 2, 1),
                     pltpu.roll(x, AT_DIM // 2, 1))


_NN = (((1,), (0,)), ((), ()))


def _hl(a):
    hi = a.astype(BF16)
    return hi, (a - hi.astype(F32)).astype(BF16)


def _mm3(a, b, dims=_NN):
    (ah, al), (bh, bl) = a, b
    f = lambda x, y: lax.dot_general(x, y, dims, preferred_element_type=F32)
    return f(ah, bh) + (f(ah, bl) + f(al, bh))


def _chunk_masks():
    r = lax.broadcasted_iota(jnp.int32, (CHUNK, CHUNK), 0)
    c = lax.broadcasted_iota(jnp.int32, (CHUNK, CHUNK), 1)
    return r >= c, r > c, (r == c).astype(F32), (r // 16) == (c // 16)


def _tri_inv(mats):
    _, _, eye, blk = _chunk_masks()
    dg = [jnp.where(blk, a, 0.0) for a in mats]
    lo = [jnp.where(blk, 0.0, a) for a in mats]
    sdg = [_hl(x) for x in dg]
    d2 = [_mm3(s, s) for s in sdg]
    sd2 = [_hl(x) for x in d2]
    d4 = [_mm3(s, s) for s in sd2]
    sd4 = [_hl(x) for x in d4]
    d8 = [_mm3(s, s) for s in sd4]
    p1 = [_mm3(_hl(eye - a), _hl(eye + b)) for a, b in zip(dg, d2)]
    p2 = [_mm3(_hl(a), _hl(eye + b)) for a, b in zip(p1, d4)]
    dinv = [_mm3(_hl(a), _hl(eye + b)) for a, b in zip(p2, d8)]
    sdinv = [_hl(x) for x in dinv]
    n1 = [_mm3(s, _hl(b)) for s, b in zip(sdinv, lo)]
    sn1 = [_hl(x) for x in n1]
    n2 = [_mm3(s, s) for s in sn1]
    q1 = [_mm3(_hl(eye - a), _hl(eye + b)) for a, b in zip(n1, n2)]
    return [_mm3(_hl(a), s) for a, s in zip(q1, sdinv)]


def _chunk_common(qs, ks, vs, betas, gcs):
    tril, _, _, _ = _chunk_masks()
    out = []
    for q, k, v, beta, gc in zip(qs, ks, vs, betas, gcs):
        gb = jnp.broadcast_to(gc, (CHUNK, DN_DIM))
        gt = gb.T[:CHUNK, :]
        gam = jnp.where(tril, jnp.exp(jnp.where(tril, gb[:, :CHUNK] - gt, 0.0)), 0.0)
        last = gb[CHUNK - 1:CHUNK, :]
        eg, e2 = jnp.exp(gb), jnp.exp(last - gb)
        kb, vb = k * beta, v * beta
        out.append(dict(gam=gam, eg=eg, e2=e2, gl=jnp.exp(last[:, 0:1]), kb=kb, vb=vb, kbg=kb * eg,
                        m=_nt(kb, k), qk=_nt(q, k)))
    return out


def _chunk_fwd(qs, ks, vs, betas, gcs):
    tril, strict, _, _ = _chunk_masks()
    cm = _chunk_common(qs, ks, vs, betas, gcs)
    ts = _tri_inv([jnp.where(strict, c["m"] * c["gam"], 0.0) for c in cm])
    outs = []
    for q, k, c, t in zip(qs, ks, cm, ts):
        uw = _mm3(_hl(t), _hl(jnp.concatenate([c["vb"], c["kbg"]], axis=1)))
        p = jnp.where(tril, c["qk"] * c["gam"], 0.0)
        outs.append((uw[:, :DN_DIM], uw[:, DN_DIM:], p, q * c["eg"], k * c["e2"], c["gl"], t))
    return outs


def _chunk_bwd(qs, ks, vs, betas, gcs, ts, cots):
    tril, strict, _, _ = _chunk_masks()
    cm = _chunk_common(qs, ks, vs, betas, gcs)
    row = lax.broadcasted_iota(jnp.int32, (CHUNK, 1), 0)
    ones = jnp.ones((CHUNK, DN_DIM), BF16)
    rs = lambda x: jnp.sum(x, axis=-1, keepdims=True)
    sts = [_hl(t) for t in ts]
    duw = [_hl(jnp.concatenate([ct[0], ct[1]], axis=1)) for ct in cots]
    dts = [_mm3(a, _hl(jnp.concatenate([c["vb"], c["kbg"]], axis=1)), _NT) for a, c in zip(duw, cm)]
    xs = [_mm3(s, _hl(d), _TN) for s, d in zip(sts, dts)]
    das = [jnp.where(strict, -_mm3(_hl(x), s, _NT), 0.0) for x, s in zip(xs, sts)]
    dvks = [_mm3(s, a, _TN) for s, a in zip(sts, duw)]
    outs = []
    for q, k, v, beta, c, ct, da, dvk in zip(qs, ks, vs, betas, cm, cots, das, dvks):
        _, _, dp, dqd, dkd, dgl = ct
        dvb, dkbg = dvk[:, :DN_DIM], dvk[:, DN_DIM:]
        dm = da * c["gam"]
        dqk = jnp.where(tril, dp, 0.0) * c["gam"]
        e = dm * c["m"] + dqk * c["qk"]
        dmq = jnp.concatenate([dm, dqk], axis=0)
        r1 = _nn(dmq, k)
        dkb = r1[:CHUNK] + dkbg * c["eg"]
        dq = r1[CHUNK:] + dqd * c["eg"]
        dk = _tn(dmq, jnp.concatenate([c["kb"], q], axis=0)) + dkd * c["e2"] + dkb * beta
        dbeta = rs(dkb * k) + rs(dvb * v)
        eh, el = _hl(e)
        colsum = (lax.dot_general(eh, ones, _TN, preferred_element_type=F32)
                  + lax.dot_general(el, ones, _TN, preferred_element_type=F32))[:, 0:1]
        rs_kd = rs(dkd * (k * c["e2"]))
        dgc = rs(e) - colsum + rs(dqd * q * c["eg"]) + rs(dkbg * c["kbg"]) - rs_kd
        tail = jnp.sum(rs_kd, axis=0, keepdims=True) + dgl * c["gl"]
        dgc = dgc + jnp.where(row == CHUNK - 1, tail, 0.0)
        outs.append((dq, dk, dvb * beta, dbeta, dgc))
    return outs


def _chunk_cumsum(x, reverse=False):
    n = x.shape[0]
    pos = lax.broadcasted_iota(jnp.int32, x.shape, 0) & (CHUNK - 1)
    sh = 1
    while sh < CHUNK:
        if reverse:
            x = x + jnp.where(pos < CHUNK - sh, pltpu.roll(x, n - sh, 0), 0.0)
        else:
            x = x + jnp.where(pos >= sh, pltpu.roll(x, sh, 0), 0.0)
        sh *= 2
    return x


GC_LANE = 2 * DN_HEADS


def _exchange(arrays, scatter, name):
    n = len(arrays)
    out_shapes = []
    for a, sc in zip(arrays, scatter):
        out_shapes.append(SDS(a.shape if sc else (N_DEV,) + a.shape, a.dtype))

    def body(*refs):
        ins, outs = refs[:n], refs[n:2 * n]
        send_sems, recv_sems, loc_sems = refs[2 * n:]
        x, y, c = lax.axis_index("x"), lax.axis_index("y"), lax.axis_index("c")
        me = 4 * x + 2 * y + c
        local, remote = [], []
        for i in range(n):
            src = ins[i].at[me] if scatter[i] else ins[i]
            cp = pltpu.make_async_copy(src, outs[i].at[me], loc_sems.at[i])
            cp.start()
            local.append(cp)
        for dlt in range(1, N_DEV):
            px = 1 - x if dlt & 4 else x
            py = 1 - y if dlt & 2 else y
            pc = 1 - c if dlt & 1 else c
            peer = 4 * px + 2 * py + pc
            for i in range(n):
                src = ins[i].at[peer] if scatter[i] else ins[i]
                cp = pltpu.make_async_remote_copy(
                    src_ref=src, dst_ref=outs[i].at[me],
                    send_sem=send_sems.at[i, dlt - 1], recv_sem=recv_sems.at[i, dlt - 1],
                    device_id=(px, py, pc), device_id_type=pl.DeviceIdType.MESH)
                cp.start()
                arrive = pltpu.make_async_remote_copy(
                    src_ref=src, dst_ref=outs[i].at[peer],
                    send_sem=send_sems.at[i, dlt - 1], recv_sem=recv_sems.at[i, dlt - 1],
                    device_id=(px, py, pc), device_id_type=pl.DeviceIdType.MESH)
                remote.append((cp, arrive))
        for cp, arrive in remote:
            cp.wait_send()
            arrive.wait_recv()
        for cp in local:
            cp.wait()

    any_spec = pl.BlockSpec(memory_space=pl.ANY)
    return pl.pallas_call(
        body, name=name, out_shape=tuple(out_shapes),
        in_specs=[any_spec] * n, out_specs=tuple([any_spec] * n),
        scratch_shapes=[pltpu.SemaphoreType.DMA((n, N_DEV - 1)), pltpu.SemaphoreType.DMA((n, N_DEV - 1)),
                        pltpu.SemaphoreType.DMA((n,))],
    )(*arrays)


def _adaln_mod(c, w_mod, b_mod):
    def body(c_ref, w_ref, b_ref, mod_ref, sc_ref):
        sc = _silu(c_ref[...])
        sc8 = jnp.broadcast_to(sc, (8, D_MODEL))
        mod_ref[...] = _nn(sc8, w_ref[...])[0:1] + b_ref[...]
        sc_ref[...] = sc

    return pl.pallas_call(body, name="adaln_mod", compiler_params=_params(),
                          out_shape=(SDS((1, 3 * D_MODEL), F32), SDS((1, D_MODEL), F32)))(c, w_mod, b_mod)


def _ln_proj(x, mod, norm_w, ws, cos_t, sin_t, ts):
    s = x.shape[0]
    widths = [w.shape[1] for w in ws]

    def body(x_ref, mod_ref, nw_ref, cos_ref, sin_ref, wqkv, wz, wba, waq, wak, wav, waz,
             h_ref, oqkv, oz, oba, oq, ok, ov, oaz):
        xt = x_ref[...]
        r = lax.rsqrt(jnp.mean(xt * xt, axis=-1, keepdims=True) + EPS)
        shift, scale = mod_ref[:, 0:D_MODEL], mod_ref[:, D_MODEL:2 * D_MODEL]
        h = ((xt * r) * nw_ref[...]) * (1.0 + scale) + shift
        hb = _bf(h)
        h_ref[...] = hb
        oqkv[...] = jnp.dot(hb, wqkv[...], preferred_element_type=F32)
        oz[...] = jnp.dot(hb, wz[...], preferred_element_type=F32)
        oba[...] = jnp.dot(hb, wba[...], preferred_element_type=F32)
        oaz[...] = jnp.dot(hb, waz[...], preferred_element_type=F32)
        tv = jnp.dot(hb, wav[...], preferred_element_type=F32)
        for j in range(AT_PAIRS):
            ov[j] = tv[:, j * LANES:(j + 1) * LANES]
        cs, sn = cos_ref[...], sin_ref[...]
        for w_ref, o_ref in ((waq, oq), (wak, ok)):
            t = jnp.dot(hb, w_ref[...], preferred_element_type=F32)
            for j in range(AT_PAIRS):
                tj = t[:, j * LANES:(j + 1) * LANES]
                o_ref[j] = tj * cs + _swap_half64(tj) * sn

    tok = lambda w: pl.BlockSpec((ts, w), lambda i: (i, 0))
    full = lambda a: pl.BlockSpec(a.shape, lambda i: (0, 0))
    pairs = pl.BlockSpec((AT_PAIRS, ts, LANES), lambda i: (0, i, 0))
    return pl.pallas_call(
        body, name="ln_proj", grid=(s // ts,), compiler_params=_params("arbitrary"),
        in_specs=[tok(D_MODEL), full(mod), full(norm_w), tok(LANES), tok(LANES)] + [full(w) for w in ws],
        out_specs=(tok(D_MODEL), tok(widths[0]), tok(widths[1]), tok(widths[2]), pairs, pairs, pairs,
                   tok(widths[6])),
        out_shape=(SDS((s, D_MODEL), BF16), SDS((s, widths[0]), F32), SDS((s, widths[1]), F32),
                   SDS((s, widths[2]), F32)) + (SDS((AT_PAIRS, s, LANES), F32),) * 3 + (SDS((s, widths[6]), F32),),
    )(x, mod, norm_w, cos_t, sin_t, *ws)


def _conv_taps(ext, rows):
    taps = []
    for j in range(CONV_K):
        sh = CONV_K - 1 - j
        rolled = pltpu.roll(ext, sh, 0) if sh else ext
        taps.append(rolled[8:8 + rows])
    return taps


def _dn_prep(qkv_pre, ba, conv_w8, alog_row, dtb_row, ts):
    s = qkv_pre.shape[0]
    cw = 3 * DN_WIDTH

    def body(pre_ref, halo_ref, ba_ref, cw_ref, al_ref, dtb_ref, q_ref, k_ref, v_ref, bg_ref):
        n = pl.program_id(0)
        prev = jnp.where(n == 0, 0.0, halo_ref[...])
        ext = jnp.concatenate([prev, pre_ref[...]], axis=0)
        taps = _conv_taps(ext, ts)
        conv = taps[0] * cw_ref[0:1, :]
        for j in range(1, CONV_K):
            conv = conv + taps[j] * cw_ref[j:j + 1, :]
        for h in range(DN_HEADS):
            cols = slice(h * DN_DIM, (h + 1) * DN_DIM)
            q_ref[:, cols] = _post_q(conv[:, h * DN_DIM:(h + 1) * DN_DIM])
            k_ref[:, cols] = _post_k(conv[:, DN_WIDTH + h * DN_DIM:DN_WIDTH + (h + 1) * DN_DIM])
            v_ref[:, cols] = _post_v(conv[:, 2 * DN_WIDTH + h * DN_DIM:2 * DN_WIDTH + (h + 1) * DN_DIM])
        bg = _beta_decay(ba_ref[...], al_ref[...], dtb_ref[...])
        lane = lax.broadcasted_iota(jnp.int32, bg.shape, 1)
        run = pltpu.roll(_chunk_cumsum(bg), DN_HEADS, 1)
        bg_ref[...] = jnp.where((lane >= GC_LANE) & (lane < GC_LANE + DN_HEADS), run, bg)

    tok = lambda w: pl.BlockSpec((ts, w), lambda i: (i, 0))
    full = lambda a: pl.BlockSpec(a.shape, lambda i: (0, 0))
    halo = pl.BlockSpec((8, cw), lambda i: (jnp.maximum(i * (ts // 8) - 1, 0), 0))
    return pl.pallas_call(
        body, name="dn_prep", grid=(s // ts,), compiler_params=_params("arbitrary"),
        in_specs=[tok(cw), halo, tok(BA_PAD), full(conv_w8), full(alog_row), full(dtb_row)],
        out_specs=(tok(DN_WIDTH), tok(DN_WIDTH), tok(DN_WIDTH), tok(BA_PAD)),
        out_shape=(SDS((s, DN_WIDTH), F32),) * 3 + (SDS((s, BA_PAD), F32),),
    )(qkv_pre, qkv_pre, ba, conv_w8, alog_row, dtb_row)


def _dn_chunk_prep(q, k, v, bg, ts):
    s = q.shape[0]
    ncs = ts // CHUNK

    def body(q_ref, k_ref, v_ref, bg_ref, u_ref, w_ref, qd_ref, kd_ref, p_ref, gl_ref, t_ref):
        def chunks(cg, carry):
            where = []
            for ci in (cg * CH_UNROLL + i for i in range(CH_UNROLL)):
                rows = pl.ds(pl.multiple_of(ci * CHUNK, CHUNK), CHUNK)
                rows8 = pl.ds(pl.multiple_of(ci * 8, 8), 8)
                where += [(rows, rows8, h, slice(h * DN_DIM, (h + 1) * DN_DIM)) for h in range(DN_HEADS)]
            bgs = [bg_ref[rows, :] for rows, _, _, _ in where]
            outs = _chunk_fwd([q_ref[rows, c] for rows, _, _, c in where], [k_ref[rows, c] for rows, _, _, c in where],
                              [v_ref[rows, c] for rows, _, _, c in where],
                              [b[:, h:h + 1] for b, (_, _, h, _) in zip(bgs, where)],
                              [b[:, GC_LANE + h:GC_LANE + h + 1] for b, (_, _, h, _) in zip(bgs, where)])
            for (rows, rows8, h, c), (u, w, p, qd, kd, gl, t) in zip(where, outs):
                u_ref[rows, c] = u
                w_ref[rows, c] = w
                qd_ref[rows, c] = qd
                kd_ref[rows, c] = kd
                p_ref[h, rows, :] = p
                t_ref[h, rows, :] = t
                gl_ref[rows8, c] = jnp.broadcast_to(gl, (8, DN_DIM))
            return carry

        lax.fori_loop(0, ncs // CH_UNROLL, chunks, 0)

    tok = lambda w: pl.BlockSpec((ts, w), lambda i: (i, 0))
    sq = pl.BlockSpec((DN_HEADS, ts, CHUNK), lambda i: (0, i, 0))
    return pl.pallas_call(
        body, name="dn_chunk_prep", grid=(s // ts,), compiler_params=_params("arbitrary"),
        in_specs=[tok(DN_WIDTH)] * 3 + [tok(BA_PAD)],
        out_specs=(tok(DN_WIDTH),) * 4 + (sq, pl.BlockSpec((ncs * 8, DN_WIDTH), lambda i: (i, 0)), sq),
        out_shape=(SDS((s, DN_WIDTH), F32),) * 4 + (SDS((DN_HEADS, s, CHUNK), F32),
                                                     SDS((s // CHUNK * 8, DN_WIDTH), F32),
                                                     SDS((DN_HEADS, s, CHUNK), F32)),
    )(q, k, v, bg)


def _dn_scan(u, w, qd, kd, p, gl, ts):
    s = u.shape[0]
    ncs = ts // CHUNK

    def body(u_ref, w_ref, qd_ref, kd_ref, p_ref, gl_ref, o_ref, vn_ref, st_ref, state):
        @pl.when(pl.program_id(0) == 0)
        def _():
            state[...] = jnp.zeros_like(state)

        def chunk(ci, carry):
            rows = pl.ds(pl.multiple_of(ci * CHUNK, CHUNK), CHUNK)
            rows8 = pl.ds(pl.multiple_of(ci * 8, 8), 8)
            srows = pl.ds(pl.multiple_of(ci * DN_DIM, DN_DIM), DN_DIM)
            for h in range(DN_HEADS):
                cols = slice(h * DN_DIM, (h + 1) * DN_DIM)
                sf = state[h]
                st_ref[srows, cols] = sf
                vn = u_ref[rows, cols] - _nn(w_ref[rows, cols], sf)
                o_ref[rows, cols] = _nn(qd_ref[rows, cols], sf) + _nn(p_ref[h, rows, :], vn)
                vn_ref[rows, cols] = vn
                state[h] = sf * gl_ref[rows8, cols][0:1] + _tn(kd_ref[rows, cols], vn)
            return carry

        lax.fori_loop(0, ncs, chunk, 0)

    tok = lambda wd: pl.BlockSpec((ts, wd), lambda i: (i, 0))
    return pl.pallas_call(
        body, name="dn_scan", grid=(s // ts,), compiler_params=_params("arbitrary"),
        in_specs=[tok(DN_WIDTH)] * 4 + [pl.BlockSpec((DN_HEADS, ts, CHUNK), lambda i: (0, i, 0)),
                                        pl.BlockSpec((ncs * 8, DN_WIDTH), lambda i: (i, 0))],
        out_specs=(tok(DN_WIDTH), tok(DN_WIDTH), pl.BlockSpec((ncs * DN_DIM, DN_WIDTH), lambda i: (i, 0))),
        out_shape=(SDS((s, DN_WIDTH), F32), SDS((s, DN_WIDTH), F32), SDS((s // CHUNK * DN_DIM, DN_WIDTH), F32)),
        scratch_shapes=[pltpu.VMEM((DN_HEADS, DN_DIM, DN_DIM), F32)],
    )(u, w, qd, kd, p, gl)


def _attn_mask(first):
    qi = lax.broadcasted_iota(jnp.int32, (Q_BLOCK, 2 * Q_BLOCK), 0)
    kj = lax.broadcasted_iota(jnp.int32, (Q_BLOCK, 2 * Q_BLOCK), 1)
    rel = Q_BLOCK + qi - kj
    return (rel >= 0) & (rel <= W_SUB) & ((kj >= Q_BLOCK) | jnp.logical_not(first))


def _attn_combo(c, d):
    if d == 1:
        qs = pl.multiple_of(c * Q_BLOCK, Q_BLOCK)
        return qs, pl.multiple_of(ATT_BLK - Q_BLOCK + c * Q_BLOCK, Q_BLOCK), c == 0
    r, m = c % d, c // d
    qs = r + (d * Q_BLOCK) * m
    return qs, ATT_BLK + qs - d * Q_BLOCK, m == 0


def _rows(start, size, d):
    return pl.ds(start, size) if d == 1 else pl.ds(start, size, stride=d)


def _shift_in(ext, cur, n):
    @pl.when(n == 0)
    def _():
        ext[0:ATT_BLK, :] = jnp.zeros((ATT_BLK, LANES), F32)

    @pl.when(n > 0)
    def _():
        ext[0:ATT_BLK, :] = ext[ATT_BLK:2 * ATT_BLK, :]

    ext[ATT_BLK:2 * ATT_BLK, :] = cur


def _attn_fwd(qr, kr, vv):
    s = qr.shape[1]
    nblk = s // ATT_BLK
    scale = AT_DIM ** -0.5
    npat = len(DILATIONS)

    def body(q_ref, k_ref, v_ref, o_ref, lse_ref, kext, vext, o_p, l_p):
        n = pl.program_id(1)
        _shift_in(kext, k_ref[0], n)
        _shift_in(vext, v_ref[0], n)
        lo = lax.broadcasted_iota(jnp.int32, (Q_BLOCK, LANES), 1) < AT_DIM
        for pi, d in enumerate(DILATIONS):
            def combo(c, pi=pi, d=d):
                qs, ks, m0 = _attn_combo(c, d)
                mask = _attn_mask((n == 0) & m0)
                q = _bf(q_ref[0, _rows(qs, Q_BLOCK, d), :])
                kk = _bf(kext[_rows(ks, 2 * Q_BLOCK, d), :])
                vb = _bf(vext[_rows(ks, 2 * Q_BLOCK, d), :])
                outs, lses = [], []
                for sel in (lo, ~lo):
                    qm = jnp.where(sel, q, jnp.zeros_like(q))
                    sc = lax.dot_general(qm, kk, _NT, preferred_element_type=F32) * scale
                    sc = jnp.where(mask, sc, -1e30)
                    mx = jnp.max(sc, axis=-1, keepdims=True)
                    pr = jnp.exp(sc - mx)
                    l = jnp.sum(pr, axis=-1, keepdims=True)
                    outs.append(jnp.dot(_bf(pr), vb, preferred_element_type=F32) / l)
                    lses.append(mx + jnp.log(l))
                return qs, jnp.where(lo, outs[0], outs[1]), jnp.where(lo, lses[0], lses[1])

            def group(g, carry, pi=pi, d=d, combo=combo):
                res = [combo(g * ATT_UNROLL + u) for u in range(ATT_UNROLL)]
                for qs, o, l in res:
                    o_p[pi, _rows(qs, Q_BLOCK, d), :] = o
                    l_p[pi, _rows(qs, Q_BLOCK, d), :] = l
                return carry

            lax.fori_loop(0, ATT_BLK // Q_BLOCK // ATT_UNROLL, group, 0)

        def merge(i, carry):
            rows = pl.ds(pl.multiple_of(i * 256, 256), 256)
            ls = [l_p[pi, rows, :] for pi in range(npat)]
            mx = jnp.maximum(jnp.maximum(ls[0], ls[1]), ls[2])
            es = [jnp.exp(l - mx) for l in ls]
            den = es[0] + es[1] + es[2]
            o_ref[0, rows, :] = (es[0] * o_p[0, rows, :] + es[1] * o_p[1, rows, :] + es[2] * o_p[2, rows, :]) / den
            lse_ref[0, rows, :] = mx + jnp.log(den)
            return carry

        lax.fori_loop(0, ATT_BLK // 256, merge, 0)

    blk = pl.BlockSpec((1, ATT_BLK, LANES), lambda j, n: (j, n, 0))
    return pl.pallas_call(
        body, name="attn_fwd", grid=(AT_PAIRS, nblk), compiler_params=_params("arbitrary", "arbitrary"),
        in_specs=[blk] * 3, out_specs=(blk, blk),
        out_shape=(SDS((AT_PAIRS, s, LANES), F32),) * 2,
        scratch_shapes=[pltpu.VMEM((2 * ATT_BLK, LANES), F32), pltpu.VMEM((2 * ATT_BLK, LANES), F32),
                        pltpu.VMEM((npat, ATT_BLK, LANES), F32), pltpu.VMEM((npat, ATT_BLK, LANES), F32)],
    )(qr, kr, vv)


def _mix_prep(o_dn, z_dn, o_at, z_at, dnw, atw2, ts):
    s = o_dn.shape[0]

    def body(odn, zdn, oat, zat, dnw_ref, atw_ref, cat_ref):
        for h in range(DN_HEADS):
            cols = slice(h * DN_DIM, (h + 1) * DN_DIM)
            cat_ref[:, cols] = _bf(_gate_dn(odn[:, cols], zdn[:, cols], dnw_ref[...]))
        for j in range(AT_PAIRS):
            cat_ref[:, DN_WIDTH + j * LANES:DN_WIDTH + (j + 1) * LANES] = _bf(
                _gate_at(oat[j], zat[:, j * LANES:(j + 1) * LANES], atw_ref[...], _hnn))

    tok = lambda w: pl.BlockSpec((ts, w), lambda i: (i, 0))
    full = lambda a: pl.BlockSpec(a.shape, lambda i: (0, 0))
    pairs = pl.BlockSpec((AT_PAIRS, ts, LANES), lambda i: (0, i, 0))
    return pl.pallas_call(
        body, name="mix_prep", grid=(s // ts,), compiler_params=_params("arbitrary"),
        in_specs=[tok(DN_WIDTH), tok(DN_WIDTH), pairs, tok(AT_WIDTH), full(dnw), full(atw2)],
        out_specs=tok(D_MODEL), out_shape=SDS((s, D_MODEL), BF16),
    )(o_dn, z_dn, o_at, z_at, dnw, atw2)


def _out_loss(cat, x, tgt, w_out, gate, fw, ts):
    s = x.shape[0]

    def body(cat_ref, x_ref, t_ref, w_ref, g_ref, fw_ref, dx2_ref, dcat_ref, gw_ref, dfw_ref, dgate_ref, loss_ref):
        @pl.when(pl.program_id(0) == 0)
        def _():
            gw_ref[...] = jnp.zeros_like(gw_ref)
            dfw_ref[...] = jnp.zeros_like(dfw_ref)
            dgate_ref[...] = jnp.zeros_like(dgate_ref)
            loss_ref[...] = jnp.zeros_like(loss_ref)

        catb = cat_ref[...]
        wb = w_ref[...]
        gate, fwv = g_ref[...], fw_ref[...]
        mix = jnp.dot(catb, wb, preferred_element_type=F32)
        x2 = x_ref[...] + gate * mix
        r2 = lax.rsqrt(jnp.mean(x2 * x2, axis=-1, keepdims=True) + EPS)
        xn2 = x2 * r2
        err = xn2 * fwv - t_ref[...]
        row = jnp.sum(err * err, axis=-1, keepdims=True) * (1.0 / D_MODEL)
        loss_ref[...] += 0.5 * jnp.sum(row, axis=0, keepdims=True)
        dy = err * (1.0 / D_MODEL)
        dfw_ref[...] += jnp.sum(dy * xn2, axis=0, keepdims=True)
        dxn = dy * fwv
        dx2 = r2 * (dxn - xn2 * jnp.mean(dxn * xn2, axis=-1, keepdims=True))
        dx2_ref[...] = dx2
        dgate_ref[...] += jnp.sum(dx2 * mix, axis=0, keepdims=True)
        dmix = _bf(gate * dx2)
        dcat_ref[...] = lax.dot_general(dmix, wb, _NT, preferred_element_type=F32)
        gw_ref[...] += lax.dot_general(catb, dmix, _TN, preferred_element_type=F32)

    tok = lambda w: pl.BlockSpec((ts, w), lambda i: (i, 0))
    full = lambda a: pl.BlockSpec(a.shape, lambda i: (0, 0))
    row = pl.BlockSpec((1, D_MODEL), lambda i: (0, 0))
    return pl.pallas_call(
        body, name="out_loss", grid=(s // ts,), compiler_params=_params("arbitrary"),
        in_specs=[tok(D_MODEL), tok(D_MODEL), tok(D_MODEL), full(w_out), full(gate), full(fw)],
        out_specs=(tok(D_MODEL), tok(D_MODEL), pl.BlockSpec((D_MODEL, D_MODEL), lambda i: (0, 0)), row, row,
                   pl.BlockSpec((1, 1), lambda i: (0, 0))),
        out_shape=(SDS((s, D_MODEL), F32), SDS((s, D_MODEL), F32), SDS((D_MODEL, D_MODEL), F32),
                   SDS((1, D_MODEL), F32), SDS((1, D_MODEL), F32), SDS((1, 1), F32)),
    )(cat, x, tgt, w_out, gate, fw)


def _mix_bwd(dcat, o_dn, z_dn, o_at, z_at, dnw, atw2, ts):
    s = dcat.shape[0]

    def body(dcat_ref, odn, zdn, oat, zat, dnw_ref, atw_ref, dodn, dzdn, doat, dzat, delta, ddnw, datw):
        @pl.when(pl.program_id(0) == 0)
        def _():
            ddnw[...] = jnp.zeros_like(ddnw)
            datw[...] = jnp.zeros_like(datw)

        for h in range(DN_HEADS):
            cols = slice(h * DN_DIM, (h + 1) * DN_DIM)
            _, vjp = jax.vjp(_gate_dn, odn[:, cols], zdn[:, cols], dnw_ref[...])
            do, dz, dw = vjp(dcat_ref[:, cols])
            dodn[:, cols] = do
            dzdn[:, cols] = _bf(dz)
            ddnw[...] += dw
        for j in range(AT_PAIRS):
            cols = slice(j * LANES, (j + 1) * LANES)
            o = oat[j]
            _, vjp = jax.vjp(functools.partial(_gate_at, hnn=_d_hnn), o, zat[:, cols], atw_ref[...])
            do, dz, dw = vjp(dcat_ref[:, DN_WIDTH + j * LANES:DN_WIDTH + (j + 1) * LANES])
            doat[j] = do
            dzat[:, cols] = _bf(dz)
            datw[...] += dw
            delta[j] = _hnn(do * o, _group_ones(1.0))

    tok = lambda w: pl.BlockSpec((ts, w), lambda i: (i, 0))
    full = lambda a: pl.BlockSpec(a.shape, lambda i: (0, 0))
    row = pl.BlockSpec((1, LANES), lambda i: (0, 0))
    pairs = pl.BlockSpec((AT_PAIRS, ts, LANES), lambda i: (0, i, 0))
    return pl.pallas_call(
        body, name="mix_bwd", grid=(s // ts,), compiler_params=_params("arbitrary"),
        in_specs=[tok(D_MODEL), tok(DN_WIDTH), tok(DN_WIDTH), pairs, tok(AT_WIDTH), full(dnw), full(atw2)],
        out_specs=(tok(DN_WIDTH), tok(DN_WIDTH), pairs, tok(AT_WIDTH), pairs, row, row),
        out_shape=(SDS((s, DN_WIDTH), F32), SDS((s, DN_WIDTH), BF16), SDS((AT_PAIRS, s, LANES), F32),
                   SDS((s, AT_WIDTH), BF16), SDS((AT_PAIRS, s, LANES), F32), SDS((1, LANES), F32),
                   SDS((1, LANES), F32)),
    )(dcat, o_dn, z_dn, o_at, z_at, dnw, atw2)


def _shift_acc(ext, n):
    @pl.when(n == 0)
    def _():
        ext[0:ATT_BLK, :] = jnp.zeros((ATT_BLK, LANES), F32)

    @pl.when(n > 0)
    def _():
        ext[0:ATT_BLK, :] = ext[ATT_BLK:2 * ATT_BLK, :]

    ext[ATT_BLK:2 * ATT_BLK, :] = jnp.zeros((ATT_BLK, LANES), F32)


def _attn_bwd(qr, kr, vv, do, lse, delta):
    s = qr.shape[1]
    nblk = s // ATT_BLK
    scale = AT_DIM ** -0.5

    def body(q_ref, k_ref, v_ref, do_ref, lse_ref, dl_ref, dq_ref, dk_ref, dv_ref, kext, vext, dkext, dvext):
        n = pl.program_id(1)
        _shift_in(kext, k_ref[0], n)
        _shift_in(vext, v_ref[0], n)
        _shift_acc(dkext, n)
        _shift_acc(dvext, n)

        @pl.when(n < nblk)
        def _():
            dq_ref[0] = jnp.zeros((ATT_BLK, LANES), F32)
            lo = lax.broadcasted_iota(jnp.int32, (Q_BLOCK, LANES), 1) < AT_DIM
            for d in DILATIONS:
                def combo(c, d=d):
                    qs, ks, m0 = _attn_combo(c, d)
                    mask = _attn_mask((n == 0) & m0)
                    qrows, krows = _rows(qs, Q_BLOCK, d), _rows(ks, 2 * Q_BLOCK, d)
                    q = _bf(q_ref[0, qrows, :])
                    kk = _bf(kext[krows, :])
                    vb = _bf(vext[krows, :])
                    dob = _bf(do_ref[0, qrows, :])
                    lse2, dl2 = lse_ref[0, qrows, :], dl_ref[0, qrows, :]
                    dkk = jnp.zeros((2 * Q_BLOCK, LANES), F32)
                    dvv = jnp.zeros((2 * Q_BLOCK, LANES), F32)
                    dqs = []
                    for sel in (lo, ~lo):
                        qm = jnp.where(sel, q, jnp.zeros_like(q))
                        dom = jnp.where(sel, dob, jnp.zeros_like(dob))
                        lse_c = jnp.max(jnp.where(sel, lse2, -jnp.inf), axis=-1, keepdims=True)
                        dl_c = jnp.max(jnp.where(sel, dl2, -jnp.inf), axis=-1, keepdims=True)
                        sc = lax.dot_general(qm, kk, _NT, preferred_element_type=F32) * scale
                        pr = jnp.where(mask, jnp.exp(jnp.where(mask, sc - lse_c, 0.0)), 0.0)
                        dp = lax.dot_general(dom, vb, _NT, preferred_element_type=F32)
                        ds = _bf(pr * (dp - dl_c) * scale)
                        dqs.append(jnp.dot(ds, kk, preferred_element_type=F32))
                        dkk = dkk + lax.dot_general(ds, qm, _TN, preferred_element_type=F32)
                        dvv = dvv + lax.dot_general(_bf(pr), dom, _TN, preferred_element_type=F32)
                    return qrows, krows, jnp.where(lo, dqs[0], dqs[1]), dkk, dvv

                def group(g, carry, combo=combo):
                    res = [combo(g * ATT_UNROLL_BWD + u) for u in range(ATT_UNROLL_BWD)]
                    for qrows, krows, dq, dkk, dvv in res:
                        dq_ref[0, qrows, :] += dq
                        dkext[krows, :] += dkk
                        dvext[krows, :] += dvv
                    return carry

                lax.fori_loop(0, ATT_BLK // Q_BLOCK // ATT_UNROLL_BWD, group, 0)

        dk_ref[0] = dkext[0:ATT_BLK, :]
        dv_ref[0] = dvext[0:ATT_BLK, :]

    cur = pl.BlockSpec((1, ATT_BLK, LANES), lambda j, n: (j, jnp.minimum(n, nblk - 1), 0))
    done = pl.BlockSpec((1, ATT_BLK, LANES), lambda j, n: (j, jnp.maximum(n - 1, 0), 0))
    return pl.pallas_call(
        body, name="attn_bwd", grid=(AT_PAIRS, nblk + 1), compiler_params=_params("arbitrary", "arbitrary"),
        in_specs=[cur] * 6, out_specs=(cur, done, done),
        out_shape=(SDS((AT_PAIRS, s, LANES), F32),) * 3,
        scratch_shapes=[pltpu.VMEM((2 * ATT_BLK, LANES), F32)] * 4,
    )(qr, kr, vv, do, lse, delta)


def _rope_bwd(dq, dk, dv, cos_t, sin_t, ts):
    s = cos_t.shape[0]

    def body(q_ref, k_ref, v_ref, cos_ref, sin_ref, oq, ok, ov):
        cs, sn = cos_ref[...], sin_ref[...]
        for j in range(AT_PAIRS):
            cols = slice(j * LANES, (j + 1) * LANES)
            for g_ref, o_ref in ((q_ref, oq), (k_ref, ok)):
                g = g_ref[j]
                o_ref[:, cols] = _bf(g * cs + _swap_half64(g * sn))
            ov[:, cols] = _bf(v_ref[j])

    tok = lambda w: pl.BlockSpec((ts, w), lambda i: (i, 0))
    pairs = pl.BlockSpec((AT_PAIRS, ts, LANES), lambda i: (0, i, 0))
    return pl.pallas_call(
        body, name="rope_bwd", grid=(s // ts,), compiler_params=_params("arbitrary"),
        in_specs=[pairs] * 3 + [tok(LANES)] * 2, out_specs=(tok(AT_WIDTH),) * 3,
        out_shape=(SDS((s, AT_WIDTH), BF16),) * 3,
    )(dq, dk, dv, cos_t, sin_t)


def _dn_scan_bwd(do, st, vn, w, qd, kd, p, gl, ts):
    s = do.shape[0]
    ncs = ts // CHUNK
    nt = s // ts

    def body(do_ref, st_ref, vn_ref, w_ref, qd_ref, kd_ref, p_ref, gl_ref,
             du_ref, dw_ref, dqd_ref, dkd_ref, dp_ref, dgl_ref, dstate):
        @pl.when(pl.program_id(0) == 0)
        def _():
            dstate[...] = jnp.zeros_like(dstate)

        def chunk(jr, carry):
            ci = ncs - 1 - jr
            rows = pl.ds(pl.multiple_of(ci * CHUNK, CHUNK), CHUNK)
            rows8 = pl.ds(pl.multiple_of(ci * 8, 8), 8)
            srows = pl.ds(pl.multiple_of(ci * DN_DIM, DN_DIM), DN_DIM)
            for h in range(DN_HEADS):
                cols = slice(h * DN_DIM, (h + 1) * DN_DIM)
                ds_, sf = dstate[h], st_ref[srows, cols]
                vnc, doc, wc, qdc, kdc, pc = (vn_ref[rows, cols], do_ref[rows, cols], w_ref[rows, cols],
                                              qd_ref[rows, cols], kd_ref[rows, cols], p_ref[h, rows, :])
                dvn = _nn(kdc, ds_) + _tn(pc, doc)
                du_ref[rows, cols] = dvn
                dw_ref[rows, cols] = -_nt(dvn, sf)
                dqd_ref[rows, cols] = _nt(doc, sf)
                dkd_ref[rows, cols] = _nt(vnc, ds_)
                dp_ref[h, rows, :] = _nt(doc, vnc)
                dgl = jnp.sum(jnp.sum(ds_ * sf, axis=1, keepdims=True), axis=0, keepdims=True)
                dgl_ref[rows8, cols] = jnp.broadcast_to(dgl, (8, DN_DIM))
                dstate[h] = ds_ * gl_ref[rows8, cols][0:1] + _tn(qdc, doc) - _tn(wc, dvn)
            return carry

        lax.fori_loop(0, ncs, chunk, 0)

    tok = lambda wd: pl.BlockSpec((ts, wd), lambda i: (nt - 1 - i, 0))
    pspec = pl.BlockSpec((DN_HEADS, ts, CHUNK), lambda i: (0, nt - 1 - i, 0))
    g8 = pl.BlockSpec((ncs * 8, DN_WIDTH), lambda i: (nt - 1 - i, 0))
    return pl.pallas_call(
        body, name="dn_scan_bwd", grid=(nt,), compiler_params=_params("arbitrary"),
        in_specs=[tok(DN_WIDTH), pl.BlockSpec((ncs * DN_DIM, DN_WIDTH), lambda i: (nt - 1 - i, 0))]
        + [tok(DN_WIDTH)] * 4 + [pspec, g8],
        out_specs=(tok(DN_WIDTH),) * 4 + (pspec, g8),
        out_shape=(SDS((s, DN_WIDTH), F32),) * 4 + (SDS((DN_HEADS, s, CHUNK), F32),
                                                     SDS((s // CHUNK * 8, DN_WIDTH), F32)),
        scratch_shapes=[pltpu.VMEM((DN_HEADS, DN_DIM, DN_DIM), F32)],
    )(do, st, vn, w, qd, kd, p, gl)


def _dn_chunk_bwd(q, k, v, bg, t, du, dw, dqd, dkd, dp, dgl, ts):
    s = q.shape[0]
    ncs = ts // CHUNK

    def body(q_ref, k_ref, v_ref, bg_ref, t_ref, du_ref, dw_ref, dqd_ref, dkd_ref, dp_ref, dgl_ref,
             dq_ref, dk_ref, dv_ref, dbg_ref):
        def chunks(cg, carry):
            lane = lax.broadcasted_iota(jnp.int32, (CHUNK, BA_PAD), 1)
            where = []
            for ci in (cg * CH_UNROLL + i for i in range(CH_UNROLL)):
                rows = pl.ds(pl.multiple_of(ci * CHUNK, CHUNK), CHUNK)
                rows8 = pl.ds(pl.multiple_of(ci * 8, 8), 8)
                where += [(rows, rows8, h, slice(h * DN_DIM, (h + 1) * DN_DIM)) for h in range(DN_HEADS)]
            bgs = [bg_ref[rows, :] for rows, _, _, _ in where]
            cots = [(du_ref[rows, c], dw_ref[rows, c], dp_ref[h, rows, :], dqd_ref[rows, c], dkd_ref[rows, c],
                     dgl_ref[rows8, c][0:1, 0:1]) for rows, rows8, h, c in where]
            outs = _chunk_bwd([q_ref[rows, c] for rows, _, _, c in where], [k_ref[rows, c] for rows, _, _, c in where],
                              [v_ref[rows, c] for rows, _, _, c in where],
                              [b[:, h:h + 1] for b, (_, _, h, _) in zip(bgs, where)],
                              [b[:, GC_LANE + h:GC_LANE + h + 1] for b, (_, _, h, _) in zip(bgs, where)],
                              [t_ref[h, rows, :] for rows, _, h, _ in where], cots)
            for i in range(CH_UNROLL):
                dbg = jnp.zeros((CHUNK, BA_PAD), F32)
                for (rows, _, h, c), (dq, dk, dv, dbeta, dgc) in list(zip(where, outs))[i * DN_HEADS:(i + 1) * DN_HEADS]:
                    dq_ref[rows, c] = dq
                    dk_ref[rows, c] = dk
                    dv_ref[rows, c] = dv
                    dbg = dbg + jnp.where(lane == h, dbeta, 0.0) + jnp.where(lane == GC_LANE + h, dgc, 0.0)
                dbg_ref[where[i * DN_HEADS][0], :] = dbg
            return carry

        lax.fori_loop(0, ncs // CH_UNROLL, chunks, 0)

    tok = lambda wd: pl.BlockSpec((ts, wd), lambda i: (i, 0))
    pspec = pl.BlockSpec((DN_HEADS, ts, CHUNK), lambda i: (0, i, 0))
    g8 = pl.BlockSpec((ncs * 8, DN_WIDTH), lambda i: (i, 0))
    return pl.pallas_call(
        body, name="dn_chunk_bwd", grid=(s // ts,), compiler_params=_params("arbitrary"),
        in_specs=[tok(DN_WIDTH)] * 3 + [tok(BA_PAD), pspec] + [tok(DN_WIDTH)] * 4 + [pspec, g8],
        out_specs=(tok(DN_WIDTH),) * 3 + (tok(BA_PAD),),
        out_shape=(SDS((s, DN_WIDTH), F32),) * 3 + (SDS((s, BA_PAD), F32),),
    )(q, k, v, bg, t, du, dw, dqd, dkd, dp, dgl)


def _dn_prep_bwd(qkv_pre, ba, dq, dk, dv, dbg, conv_w8, alog_row, dtb_row, ts):
    s = qkv_pre.shape[0]
    cw = 3 * DN_WIDTH
    nt = s // ts

    def body(pre_ref, ph_ref, nh_ref, ba_ref, dq_ref, dqh_ref, dk_ref, dkh_ref, dv_ref, dvh_ref, dbg_ref,
             cw_ref, al_ref, dtb_ref, dpre_ref, dba_ref, dcw_ref, dal_ref, ddtb_ref):
        n = pl.program_id(0)

        @pl.when(n == 0)
        def _():
            dcw_ref[...] = jnp.zeros_like(dcw_ref)
            dal_ref[...] = jnp.zeros_like(dal_ref)
            ddtb_ref[...] = jnp.zeros_like(ddtb_ref)

        last = n == nt - 1
        prev = jnp.where(n == 0, 0.0, ph_ref[...])
        ext = jnp.concatenate([prev, pre_ref[...], nh_ref[...]], axis=0)
        taps = _conv_taps(ext, ts + 8)
        conv = taps[0] * cw_ref[0:1, :]
        for j in range(1, CONV_K):
            conv = conv + taps[j] * cw_ref[j:j + 1, :]

        def cot(main, halo, cols):
            return jnp.concatenate([main[:, cols], jnp.where(last, 0.0, halo[:, cols])], axis=0)

        pieces = []
        for grp, (fn, mref, href) in enumerate(((_post_q, dq_ref, dqh_ref), (_post_k, dk_ref, dkh_ref),
                                                (_post_v, dv_ref, dvh_ref))):
            for h in range(DN_HEADS):
                cols = slice(h * DN_DIM, (h + 1) * DN_DIM)
                c0 = grp * DN_WIDTH + h * DN_DIM
                _, vjp = jax.vjp(fn, conv[:, c0:c0 + DN_DIM])
                pieces.append(vjp(cot(mref, href, cols))[0])
        dconv = jnp.concatenate(pieces, axis=1)
        rows = ts + 8
        dpre = dconv[:ts] * cw_ref[CONV_K - 1:CONV_K, :]
        for j in range(CONV_K - 1):
            sh = CONV_K - 1 - j
            dpre = dpre + pltpu.roll(dconv, rows - sh, 0)[:ts] * cw_ref[j:j + 1, :]
        dpre_ref[...] = _bf(dpre)
        for j in range(CONV_K):
            dcw_ref[j:j + 1, :] += jnp.sum(dconv[:ts] * taps[j][:ts], axis=0, keepdims=True)

        dbg = dbg_ref[...]
        lane = lax.broadcasted_iota(jnp.int32, dbg.shape, 1)
        dg = pltpu.roll(_chunk_cumsum(dbg, reverse=True), BA_PAD - DN_HEADS, 1)
        cot_bg = jnp.where(lane < DN_HEADS, dbg, jnp.where(lane < GC_LANE, dg, 0.0))
        _, vjp = jax.vjp(_beta_decay, ba_ref[...], al_ref[...], dtb_ref[...])
        dba, dal, ddtb = vjp(cot_bg)
        dba_ref[...] = _bf(dba)
        dal_ref[...] += dal
        ddtb_ref[...] += ddtb

    tok = lambda w: pl.BlockSpec((ts, w), lambda i: (i, 0))
    full = lambda a: pl.BlockSpec(a.shape, lambda i: (0, 0))
    prevh = lambda w: pl.BlockSpec((8, w), lambda i: (jnp.maximum(i * (ts // 8) - 1, 0), 0))
    nexth = lambda w: pl.BlockSpec((8, w), lambda i: (jnp.minimum((i + 1) * (ts // 8), s // 8 - 1), 0))
    row = pl.BlockSpec((1, LANES), lambda i: (0, 0))
    return pl.pallas_call(
        body, name="dn_prep_bwd", grid=(nt,), compiler_params=_params("arbitrary"),
        in_specs=[tok(cw), prevh(cw), nexth(cw), tok(BA_PAD),
                  tok(DN_WIDTH), nexth(DN_WIDTH), tok(DN_WIDTH), nexth(DN_WIDTH), tok(DN_WIDTH), nexth(DN_WIDTH),
                  tok(BA_PAD), full(conv_w8), full(alog_row), full(dtb_row)],
        out_specs=(tok(cw), tok(BA_PAD), pl.BlockSpec((8, cw), lambda i: (0, 0)), row, row),
        out_shape=(SDS((s, cw), BF16), SDS((s, BA_PAD), BF16), SDS((8, cw), F32), SDS((1, LANES), F32),
                   SDS((1, LANES), F32)),
    )(qkv_pre, qkv_pre, qkv_pre, ba, dq, dq, dk, dk, dv, dv, dbg, conv_w8, alog_row, dtb_row)


def _dh_dx(dps, ws, x, mod, norm_w, dx2, ts):
    s = x.shape[0]
    widths = [w.shape[1] for w in ws]
    np_ = len(ws)

    def body(*refs):
        dp_refs, w_refs = refs[:np_], refs[np_:2 * np_]
        x_ref, mod_ref, nw_ref, dx2_ref, gx_ref, dshift, dscale, dnw = refs[2 * np_:]

        @pl.when(pl.program_id(0) == 0)
        def _():
            dshift[...] = jnp.zeros_like(dshift)
            dscale[...] = jnp.zeros_like(dscale)
            dnw[...] = jnp.zeros_like(dnw)

        dh = lax.dot_general(dp_refs[0][...], w_refs[0][...], _NT, preferred_element_type=F32)
        for a, b in zip(dp_refs[1:], w_refs[1:]):
            dh = dh + lax.dot_general(a[...], b[...], _NT, preferred_element_type=F32)
        xt = x_ref[...]
        r = lax.rsqrt(jnp.mean(xt * xt, axis=-1, keepdims=True) + EPS)
        xn = xt * r
        nw = nw_ref[...]
        sc1 = 1.0 + mod_ref[:, D_MODEL:2 * D_MODEL]
        dshift[...] += jnp.sum(dh, axis=0, keepdims=True)
        dscale[...] += jnp.sum(dh * (xn * nw), axis=0, keepdims=True)
        dnw[...] += jnp.sum(dh * sc1 * xn, axis=0, keepdims=True)
        dxn = dh * sc1 * nw
        gx_ref[...] = r * (dxn - xn * jnp.mean(dxn * xn, axis=-1, keepdims=True)) + dx2_ref[...]

    tok = lambda w: pl.BlockSpec((ts, w), lambda i: (i, 0))
    full = lambda a: pl.BlockSpec(a.shape, lambda i: (0, 0))
    row = pl.BlockSpec((1, D_MODEL), lambda i: (0, 0))
    return pl.pallas_call(
        body, name="dh_dx", grid=(s // ts,), compiler_params=_params("arbitrary"),
        in_specs=[tok(w) for w in widths] + [full(w) for w in ws] + [tok(D_MODEL), full(mod), full(norm_w),
                                                                    tok(D_MODEL)],
        out_specs=(tok(D_MODEL), row, row, row),
        out_shape=(SDS((s, D_MODEL), F32),) + (SDS((1, D_MODEL), F32),) * 3,
    )(*dps, *ws, x, mod, norm_w, dx2)


def _grad_w_in(h, dps, ts, name):
    s = h.shape[0]
    widths = [p.shape[1] for p in dps]
    np_ = len(dps)

    def body(*refs):
        h_ref, dp_refs, outs = refs[0], refs[1:1 + np_], refs[1 + np_:]

        @pl.when(pl.program_id(0) == 0)
        def _():
            for o in outs:
                o[...] = jnp.zeros_like(o)

        hb = h_ref[...]
        for p, o in zip(dp_refs, outs):
            o[...] += lax.dot_general(hb, p[...], _TN, preferred_element_type=F32)

    tok = lambda w: pl.BlockSpec((ts, w), lambda i: (i, 0))
    return pl.pallas_call(
        body, name=name, grid=(s // ts,), compiler_params=_params("arbitrary"),
        in_specs=[tok(D_MODEL)] + [tok(w) for w in widths],
        out_specs=tuple(pl.BlockSpec((D_MODEL, w), lambda i: (0, 0)) for w in widths),
        out_shape=tuple(SDS((D_MODEL, w), F32) for w in widths),
    )(h, *dps)


def _adamw_math(w, g, m, v):
    m = ADAM_B1 * m + (1.0 - ADAM_B1) * g
    v = ADAM_B2 * v + (1.0 - ADAM_B2) * (g * g)
    m_hat = m / (1.0 - ADAM_B1 ** ADAM_STEP)
    v_hat = v / (1.0 - ADAM_B2 ** ADAM_STEP)
    delta = -ADAM_LR * (m_hat / (jnp.sqrt(v_hat) + ADAM_EPS) + ADAM_WD * w)
    return delta, m, v


def _adamw(w, m, v, g, name, slots=False):
    def body(w_ref, m_ref, v_ref, g_ref, g_out, d_out, m_out, v_out):
        if slots:
            g = g_ref[0].astype(F32)
            for k in range(1, N_DEV):
                g = g + g_ref[k].astype(F32)
        else:
            g = g_ref[...]
        g_out[...] = g
        d_out[...], m_out[...], v_out[...] = _adamw_math(w_ref[...], g, m_ref[...], v_ref[...])

    return pl.pallas_call(body, name=name, compiler_params=_params(),
                          out_shape=(SDS(w.shape, F32),) * 4)(w, m, v, g)


def _adamw_w_mod(w, m, v, siluc_all, dmod_mine):
    def body(w_ref, m_ref, v_ref, sc_ref, dm_ref, g_out, d_out, m_out, v_out):
        g = _htn(sc_ref[...], dm_ref[...])
        g_out[...] = g
        d_out[...], m_out[...], v_out[...] = _adamw_math(w_ref[...], g, m_ref[...], v_ref[...])

    return pl.pallas_call(body, name="adamw_w_mod", compiler_params=_params(),
                          out_shape=(SDS(w.shape, F32),) * 4)(w, m, v, siluc_all, dmod_mine)


def _pack_sum(pack_all):
    def body(p_ref, o_ref):
        t = p_ref[0]
        for k in range(1, N_DEV):
            t = t + p_ref[k]
        o_ref[...] = t

    return pl.pallas_call(body, name="pack_sum", out_shape=SDS(pack_all.shape[1:], F32))(pack_all)


def _tile(s, want):
    t = min(want, s)
    assert s % t == 0
    return t


def _local_step(x, c, positions, w_mod_bf, b_mod, norm_w, w_in_bf, conv_w, a_log, dt_bias, dn_norm_w, at_norm_w,
                w_out_bf, final_norm_w, tgt):
    s = x.shape[0]
    o = [0]
    for wdt in IN_SPLITS:
        o.append(o[-1] + wdt)
    w_ba = jnp.pad(w_in_bf[:, o[2]:o[4]], ((0, 0), (0, BA_PAD - 2 * DN_HEADS)))
    ws = [w_in_bf[:, o[0]:o[1]], w_in_bf[:, o[1]:o[2]], w_ba, w_in_bf[:, o[4]:o[5]], w_in_bf[:, o[5]:o[6]],
          w_in_bf[:, o[6]:o[7]], w_in_bf[:, o[7]:o[8]]]
    conv_w8 = jnp.pad(conv_w, ((0, 8 - CONV_K), (0, 0)))
    alog_row = jnp.pad(a_log, ((0, 0), (DN_HEADS, BA_PAD - 2 * DN_HEADS)))
    dtb_row = jnp.pad(dt_bias, ((0, 0), (DN_HEADS, BA_PAD - 2 * DN_HEADS)))
    atw2 = jnp.concatenate([at_norm_w, at_norm_w], axis=1)

    half = AT_DIM // 2
    inv_freq = ROPE_THETA ** (-jnp.arange(half, dtype=F32) / half)
    ang = positions.astype(F32)[:, None] * inv_freq
    cos, sin = jnp.cos(ang), jnp.sin(ang)
    cos_t = jnp.concatenate([cos, cos, cos, cos], axis=1)
    sin_t = jnp.concatenate([-sin, sin, -sin, sin], axis=1)

    mod, siluc = _adaln_mod(c, w_mod_bf, b_mod)
    gate = mod[:, 2 * D_MODEL:]
    hbf, qkv_pre, z_dn, ba, qr, kr, vb, z_at = _ln_proj(x, mod, norm_w, ws, cos_t, sin_t, _tile(s, 256))
    q, k, v, bg = _dn_prep(qkv_pre, ba, conv_w8, alog_row, dtb_row, _tile(s, 256))
    u, w, qd, kd, p, gl, tinv = _dn_chunk_prep(q, k, v, bg, _tile(s, 512))
    o_dn, vn, st = _dn_scan(u, w, qd, kd, p, gl, _tile(s, 512))
    o_at, lse = _attn_fwd(qr, kr, vb)
    cat = _mix_prep(o_dn, z_dn, o_at, z_at, dn_norm_w, atw2, _tile(s, 512))
    dx2, dcat, gw_out, dfw, dgate, loss = _out_loss(cat, x, tgt, w_out_bf, gate, final_norm_w, _tile(s, 512))

    do_dn, dz_dn, do_at, dz_at, delta, ddnw, datw = _mix_bwd(dcat, o_dn, z_dn, o_at, z_at, dn_norm_w, atw2,
                                                             _tile(s, 512))
    daq, dak, dav = _rope_bwd(*_attn_bwd(qr, kr, vb, do_at, lse, delta), cos_t, sin_t, _tile(s, 512))
    du, dw, dqd, dkd, dp, dgl = _dn_scan_bwd(do_dn, st, vn, w, qd, kd, p, gl, _tile(s, 512))
    dq, dk, dv, dbg = _dn_chunk_bwd(q, k, v, bg, tinv, du, dw, dqd, dkd, dp, dgl, _tile(s, 512))
    dqkv, dba, dcw, dal, ddtb = _dn_prep_bwd(qkv_pre, ba, dq, dk, dv, dbg, conv_w8, alog_row, dtb_row, _tile(s, 256))
    dps = [dqkv, dz_dn, dba, daq, dak, dav, dz_at]
    gx, dshift, dscale, dnw = _dh_dx(dps, ws, x, mod, norm_w, dx2, _tile(s, 256))
    g_qkv, g_z, g_ba = _grad_w_in(hbf, dps[:3], _tile(s, 512), "grad_w_in_dn")
    g_aq, g_ak, g_av, g_az = _grad_w_in(hbf, dps[3:], _tile(s, 512), "grad_w_in_at")
    gw_in = jnp.concatenate([g_qkv, g_z, g_ba[:, :2 * DN_HEADS], g_aq, g_ak, g_av, g_az], axis=1)
    dmod = jnp.concatenate([dshift, dscale, dgate], axis=1)
    small = dict(conv=dcw[:CONV_K], dmod=dmod, siluc=siluc, dnw=dnw, dfw=dfw, alog=dal, dtb=ddtb, dnn=ddnw, atn=datw)
    return loss, gx, gw_in, gw_out, small


def kernel(x, c, positions, w_mod, b_mod, norm_w, w_in, conv_w, a_log, dt_bias, dn_norm_w, at_norm_w, w_out, final_norm_w, loss_target, m_w_mod, m_b_mod, m_norm_w, m_w_in, m_conv_w, m_a_log, m_dt_bias, m_dn_norm_w, m_at_norm_w, m_w_out, m_final_norm_w, v_w_mod, v_b_mod, v_norm_w, v_w_in, v_conv_w, v_a_log, v_dt_bias, v_dn_norm_w, v_at_norm_w, v_w_out, v_final_norm_w):
    me = 4 * lax.axis_index("x") + 2 * lax.axis_index("y") + lax.axis_index("c")
    s = x.shape[1]

    g_mod, g_in, g_conv, g_out = _exchange(
        [_bf(w_mod[0]), _bf(w_in[0]), conv_w[0], _bf(w_out[0])], [False] * 4, "gather_weights")
    w_mod_bf = g_mod.transpose(1, 0, 2).reshape(D_MODEL, 3 * D_MODEL)
    w_in_bf = g_in.transpose(1, 0, 2).reshape(D_MODEL, IN_COLS)
    conv_full = g_conv.transpose(1, 0, 2).reshape(CONV_K, 3 * DN_WIDTH)
    w_out_bf = g_out.reshape(D_MODEL, D_MODEL)

    loss, gx, gw_in, gw_out, small = _local_step(
        x[0], c, positions[0], w_mod_bf, b_mod, norm_w, w_in_bf, conv_full, a_log, dt_bias, dn_norm_w, at_norm_w,
        w_out_bf, final_norm_w.reshape(1, D_MODEL), loss_target[0])

    pack = jnp.concatenate([small["conv"].reshape(1, -1), small["dmod"], small["siluc"], small["dnw"], small["dfw"],
                            small["alog"], small["dtb"], small["dnn"], small["atn"]], axis=1).reshape(PK_ROWS, LANES)
    gw_in_slabs = _bf(gw_in).reshape(D_MODEL, N_DEV, IN_SHARD).transpose(1, 0, 2)
    gw_out_slabs = _bf(gw_out).reshape(N_DEV, D_MODEL // N_DEV, D_MODEL)
    r_in, r_out, pack_all = _exchange([gw_in_slabs, gw_out_slabs, pack], [True, True, False], "exchange_grads")

    res = {}
    res["w_in"] = _adamw(w_in[0], m_w_in[0], v_w_in[0], r_in, "adamw_w_in", slots=True)
    res["w_out"] = _adamw(w_out[0], m_w_out[0], v_w_out[0], r_out, "adamw_w_out", slots=True)
    flat_all = pack_all.reshape(N_DEV, PK_END)
    dmod_mine = lax.dynamic_slice(flat_all, (0, PK_DMOD + me * (3 * D_MODEL // N_DEV)), (N_DEV, 3 * D_MODEL // N_DEV))
    res["w_mod"] = _adamw_w_mod(w_mod[0], m_w_mod[0], v_w_mod[0], flat_all[:, PK_SILUC:PK_DNW], dmod_mine)
    tot = _pack_sum(pack_all).reshape(1, PK_END)
    g_conv_full = tot[:, PK_CONV:PK_DMOD].reshape(CONV_K, 3 * DN_WIDTH)
    g_conv_mine = lax.dynamic_slice(g_conv_full, (0, me * (3 * DN_WIDTH // N_DEV)), (CONV_K, 3 * DN_WIDTH // N_DEV))
    res["conv_w"] = _adamw(conv_w[0], m_conv_w[0], v_conv_w[0], g_conv_mine, "adamw_conv_w")
    res["b_mod"] = _adamw(b_mod, m_b_mod, v_b_mod, tot[:, PK_DMOD:PK_SILUC], "adamw_b_mod")
    res["norm_w"] = _adamw(norm_w, m_norm_w, v_norm_w, tot[:, PK_DNW:PK_DFW], "adamw_norm_w")
    res["a_log"] = _adamw(a_log, m_a_log, v_a_log, tot[:, PK_ALOG + DN_HEADS:PK_ALOG + 2 * DN_HEADS], "adamw_a_log")
    res["dt_bias"] = _adamw(dt_bias, m_dt_bias, v_dt_bias, tot[:, PK_DTB + DN_HEADS:PK_DTB + 2 * DN_HEADS],
                            "adamw_dt_bias")
    res["dn_norm_w"] = _adamw(dn_norm_w, m_dn_norm_w, v_dn_norm_w, tot[:, PK_DNN:PK_ATN], "adamw_dn_norm_w")
    g_atn = tot[:, PK_ATN:PK_ATN + AT_DIM] + tot[:, PK_ATN + AT_DIM:PK_END]
    res["at_norm_w"] = _adamw(at_norm_w, m_at_norm_w, v_at_norm_w, g_atn, "adamw_at_norm_w")
    fin = _adamw(final_norm_w.reshape(1, D_MODEL), m_final_norm_w.reshape(1, D_MODEL),
                 v_final_norm_w.reshape(1, D_MODEL), tot[:, PK_DFW:PK_ALOG], "adamw_final_norm_w")
    res["final_norm_w"] = tuple(a.reshape(D_MODEL) for a in fin)

    lead = ("w_mod", "w_in", "conv_w", "w_out")
    names = ("w_mod", "b_mod", "norm_w", "w_in", "conv_w", "a_log", "dt_bias", "dn_norm_w", "at_norm_w", "w_out",
             "final_norm_w")
    out = [lax.psum(loss[0, 0], ("x", "y", "c")), gx.reshape(1, s, D_MODEL)]
    for kind in range(4):
        for nm in names:
            a = res[nm][kind]
            out.append(a[None] if nm in lead else a)
    return tuple(out)
```

```python
import functools

import jax
import jax.numpy as jnp
from jax import lax
from jax.experimental import pallas as pl
from jax.experimental.pallas import tpu as pltpu

F32, BF16 = jnp.float32, jnp.bfloat16
HI = lax.Precision.HIGHEST
SDS = jax.ShapeDtypeStruct

D_MODEL = 1024
DN_HEADS, DN_DIM, DN_WIDTH = 4, 128, 512
AT_HEADS, AT_DIM, AT_WIDTH = 8, 64, 512
CONV_K = 4
CHUNK = 64
Q_BLOCK = 128
W_SUB = 128
DILATIONS = (1, 4, 16)
AT_PAIRS = 4
ATT_BLK = Q_BLOCK * max(DILATIONS)
ATT_UNROLL, ATT_UNROLL_BWD = 4, 4
CH_UNROLL = 4
ROPE_THETA = 10000.0
EPS = 1e-6
N_DEV = 8
LANES = 128
BA_PAD = 128
IN_SPLITS = (1536, 512, 4, 4, 512, 512, 512, 512)
IN_COLS = sum(IN_SPLITS)
IN_SHARD = IN_COLS // N_DEV
VMEM_LIMIT = 56 * 2 ** 20

ADAM_LR, ADAM_B1, ADAM_B2, ADAM_EPS, ADAM_WD, ADAM_STEP = 0.001, 0.9, 0.999, 1e-08, 0.01, 10

PK_CONV, PK_DMOD, PK_SILUC, PK_DNW, PK_DFW, PK_ALOG, PK_DTB, PK_DNN, PK_ATN, PK_LOSS, PK_END = (
    0, 6144, 9216, 10240, 11264, 12288, 12416, 12544, 12672, 12800, 12928)
PK_ROWS = PK_END // LANES

_NT = (((1,), (1,)), ((), ()))
_TN = (((0,), (0,)), ((), ()))


def _params(*sem):
    return pltpu.CompilerParams(dimension_semantics=sem or None, vmem_limit_bytes=VMEM_LIMIT)


def _bf(x):
    return x.astype(BF16)


def _nn(a, b):
    return jnp.dot(_bf(a), _bf(b), preferred_element_type=F32)


def _nt(a, b):
    return lax.dot_general(_bf(a), _bf(b), _NT, preferred_element_type=F32)


def _tn(a, b):
    return lax.dot_general(_bf(a), _bf(b), _TN, preferred_element_type=F32)


def _hnn(a, b):
    return jnp.dot(a, b, precision=HI, preferred_element_type=F32)


def _hnt(a, b):
    return lax.dot_general(a, b, _NT, precision=HI, preferred_element_type=F32)


def _htn(a, b):
    return lax.dot_general(a, b, _TN, precision=HI, preferred_element_type=F32)


@jax.custom_vjp
def _d_hnn(a, b):
    return _hnn(a, b)


def _d_hnn_fwd(a, b):
    return _hnn(a, b), (a, b)


def _d_hnn_bwd(res, g):
    a, b = res
    return _hnt(g, b), _htn(a, g)


_d_hnn.defvjp(_d_hnn_fwd, _d_hnn_bwd)


def _silu(x):
    return x * jax.nn.sigmoid(x)


def _softplus(x):
    return jnp.maximum(x, 0.0) + jnp.log(1.0 + jnp.exp(-jnp.abs(x)))


def _l2n(x):
    return x * lax.rsqrt(jnp.sum(x * x, axis=-1, keepdims=True) + EPS)


def _post_q(x):
    return _l2n(_silu(x)) * (DN_DIM ** -0.5)


def _post_k(x):
    return _l2n(_silu(x))


def _post_v(x):
    return _silu(x)


def _beta_decay(ba, alog_row, dtb_row):
    lane = lax.broadcasted_iota(jnp.int32, ba.shape, 1)
    return jnp.where(lane < DN_HEADS, jax.nn.sigmoid(ba), -jnp.exp(alog_row) * _softplus(ba + dtb_row))


def _gate_dn(o, z, w):
    return (o * lax.rsqrt(jnp.mean(o * o, axis=-1, keepdims=True) + EPS)) * w * _silu(z)


def _group_ones(scale):
    r = lax.broadcasted_iota(jnp.int32, (LANES, LANES), 0)
    c = lax.broadcasted_iota(jnp.int32, (LANES, LANES), 1)
    return jnp.where((r // AT_DIM) == (c // AT_DIM), scale, 0.0).astype(F32)


def _gate_at(o, z, w2, hnn):
    ms = hnn(o * o, _group_ones(1.0 / AT_DIM))
    return (o * lax.rsqrt(ms + EPS)) * w2 * _silu(z)


def _swap_half64(x):
    lane = lax.broadcasted_iota(jnp.int32, x.shape, 1)
    return jnp.where((lane & (AT_DIM - 1)) < AT_DIM // 2, pltpu.roll(x, LANES - AT_DIM // 2, 1),
                     pltpu.roll(x, AT_DIM // 2, 1))


_NN = (((1,), (0,)), ((), ()))


def _hl(a):
    hi = a.astype(BF16)
    return hi, (a - hi.astype(F32)).astype(BF16)


def _mm3(a, b, dims=_NN):
    (ah, al), (bh, bl) = a, b
    f = lambda x, y: lax.dot_general(x, y, dims, preferred_element_type=F32)
    return f(ah, bh) + (f(ah, bl) + f(al, bh))


def _chunk_masks():
    r = lax.broadcasted_iota(jnp.int32, (CHUNK, CHUNK), 0)
    c = lax.broadcasted_iota(jnp.int32, (CHUNK, CHUNK), 1)
    return r >= c, r > c, (r == c).astype(F32), (r // 16) == (c // 16)


def _tri_inv(mats):
    _, _, eye, blk = _chunk_masks()
    dg = [jnp.where(blk, a, 0.0) for a in mats]
    lo = [jnp.where(blk, 0.0, a) for a in mats]
    sdg = [_hl(x) for x in dg]
    d2 = [_mm3(s, s) for s in sdg]
    sd2 = [_hl(x) for x in d2]
    d4 = [_mm3(s, s) for s in sd2]
    sd4 = [_hl(x) for x in d4]
    d8 = [_mm3(s, s) for s in sd4]
    p1 = [_mm3(_hl(eye - a), _hl(eye + b)) for a, b in zip(dg, d2)]
    p2 = [_mm3(_hl(a), _hl(eye + b)) for a, b in zip(p1, d4)]
    dinv = [_mm3(_hl(a), _hl(eye + b)) for a, b in zip(p2, d8)]
    sdinv = [_hl(x) for x in dinv]
    n1 = [_mm3(s, _hl(b)) for s, b in zip(sdinv, lo)]
    sn1 = [_hl(x) for x in n1]
    n2 = [_mm3(s, s) for s in sn1]
    q1 = [_mm3(_hl(eye - a), _hl(eye + b)) for a, b in zip(n1, n2)]
    return [_mm3(_hl(a), s) for a, s in zip(q1, sdinv)]


def _chunk_common(qs, ks, vs, betas, gcs):
    tril, _, _, _ = _chunk_masks()
    out = []
    for q, k, v, beta, gc in zip(qs, ks, vs, betas, gcs):
        gb = jnp.broadcast_to(gc, (CHUNK, DN_DIM))
        gt = gb.T[:CHUNK, :]
        gam = jnp.where(tril, jnp.exp(jnp.where(tril, gb[:, :CHUNK] - gt, 0.0)), 0.0)
        last = gb[CHUNK - 1:CHUNK, :]
        eg, e2 = jnp.exp(gb), jnp.exp(last - gb)
        kb, vb = k * beta, v * beta
        out.append(dict(gam=gam, eg=eg, e2=e2, gl=jnp.exp(last[:, 0:1]), kb=kb, vb=vb, kbg=kb * eg,
                        m=_nt(kb, k), qk=_nt(q, k)))
    return out


def _chunk_fwd(qs, ks, vs, betas, gcs):
    tril, strict, _, _ = _chunk_masks()
    cm = _chunk_common(qs, ks, vs, betas, gcs)
    ts = _tri_inv([jnp.where(strict, c["m"] * c["gam"], 0.0) for c in cm])
    outs = []
    for q, k, c, t in zip(qs, ks, cm, ts):
        uw = _mm3(_hl(t), _hl(jnp.concatenate([c["vb"], c["kbg"]], axis=1)))
        p = jnp.where(tril, c["qk"] * c["gam"], 0.0)
        outs.append((uw[:, :DN_DIM], uw[:, DN_DIM:], p, q * c["eg"], k * c["e2"], c["gl"], t))
    return outs


def _chunk_bwd(qs, ks, vs, betas, gcs, ts, cots):
    tril, strict, _, _ = _chunk_masks()
    cm = _chunk_common(qs, ks, vs, betas, gcs)
    row = lax.broadcasted_iota(jnp.int32, (CHUNK, 1), 0)
    ones = jnp.ones((CHUNK, DN_DIM), BF16)
    rs = lambda x: jnp.sum(x, axis=-1, keepdims=True)
    sts = [_hl(t) for t in ts]
    duw = [_hl(jnp.concatenate([ct[0], ct[1]], axis=1)) for ct in cots]
    dts = [_mm3(a, _hl(jnp.concatenate([c["vb"], c["kbg"]], axis=1)), _NT) for a, c in zip(duw, cm)]
    xs = [_mm3(s, _hl(d), _TN) for s, d in zip(sts, dts)]
    das = [jnp.where(strict, -_mm3(_hl(x), s, _NT), 0.0) for x, s in zip(xs, sts)]
    dvks = [_mm3(s, a, _TN) for s, a in zip(sts, duw)]
    outs = []
    for q, k, v, beta, c, ct, da, dvk in zip(qs, ks, vs, betas, cm, cots, das, dvks):
        _, _, dp, dqd, dkd, dgl = ct
        dvb, dkbg = dvk[:, :DN_DIM], dvk[:, DN_DIM:]
        dm = da * c["gam"]
        dqk = jnp.where(tril, dp, 0.0) * c["gam"]
        e = dm * c["m"] + dqk * c["qk"]
        dmq = jnp.concatenate([dm, dqk], axis=0)
        r1 = _nn(dmq, k)
        dkb = r1[:CHUNK] + dkbg * c["eg"]
        dq = r1[CHUNK:] + dqd * c["eg"]
        dk = _tn(dmq, jnp.concatenate([c["kb"], q], axis=0)) + dkd * c["e2"] + dkb * beta
        dbeta = rs(dkb * k) + rs(dvb * v)
        eh, el = _hl(e)
        colsum = (lax.dot_general(eh, ones, _TN, preferred_element_type=F32)
                  + lax.dot_general(el, ones, _TN, preferred_element_type=F32))[:, 0:1]
        rs_kd = rs(dkd * (k * c["e2"]))
        dgc = rs(e) - colsum + rs(dqd * q * c["eg"]) + rs(dkbg * c["kbg"]) - rs_kd
        tail = jnp.sum(rs_kd, axis=0, keepdims=True) + dgl * c["gl"]
        dgc = dgc + jnp.where(row == CHUNK - 1, tail, 0.0)
        outs.append((dq, dk, dvb * beta, dbeta, dgc))
    return outs


def _chunk_cumsum(x, reverse=False):
    n = x.shape[0]
    pos = lax.broadcasted_iota(jnp.int32, x.shape, 0) & (CHUNK - 1)
    sh = 1
    while sh < CHUNK:
        if reverse:
            x = x + jnp.where(pos < CHUNK - sh, pltpu.roll(x, n - sh, 0), 0.0)
        else:
            x = x + jnp.where(pos >= sh, pltpu.roll(x, sh, 0), 0.0)
        sh *= 2
    return x


GC_LANE = 2 * DN_HEADS


def _exchange(arrays, scatter, name):
    n = len(arrays)
    out_shapes = []
    for a, sc in zip(arrays, scatter):
        out_shapes.append(SDS(a.shape if sc else (N_DEV,) + a.shape, a.dtype))

    def body(*refs):
        ins, outs = refs[:n], refs[n:2 * n]
        send_sems, recv_sems, loc_sems = refs[2 * n:]
        x, y, c = lax.axis_index("x"), lax.axis_index("y"), lax.axis_index("c")
        me = 4 * x + 2 * y + c
        local, remote = [], []
        for i in range(n):
            src = ins[i].at[me] if scatter[i] else ins[i]
            cp = pltpu.make_async_copy(src, outs[i].at[me], loc_sems.at[i])
            cp.start()
            local.append(cp)
        for dlt in range(1, N_DEV):
            px = 1 - x if dlt & 4 else x
            py = 1 - y if dlt & 2 else y
            pc = 1 - c if dlt & 1 else c
            peer = 4 * px + 2 * py + pc
            for i in range(n):
                src = ins[i].at[peer] if scatter[i] else ins[i]
                cp = pltpu.make_async_remote_copy(
                    src_ref=src, dst_ref=outs[i].at[me],
                    send_sem=send_sems.at[i, dlt - 1], recv_sem=recv_sems.at[i, dlt - 1],
                    device_id=(px, py, pc), device_id_type=pl.DeviceIdType.MESH)
                cp.start()
                arrive = pltpu.make_async_remote_copy(
                    src_ref=src, dst_ref=outs[i].at[peer],
                    send_sem=send_sems.at[i, dlt - 1], recv_sem=recv_sems.at[i, dlt - 1],
                    device_id=(px, py, pc), device_id_type=pl.DeviceIdType.MESH)
                remote.append((cp, arrive))
        for cp, arrive in remote:
            cp.wait_send()
            arrive.wait_recv()
        for cp in local:
            cp.wait()

    any_spec = pl.BlockSpec(memory_space=pl.ANY)
    return pl.pallas_call(
        body, name=name, out_shape=tuple(out_shapes),
        in_specs=[any_spec] * n, out_specs=tuple([any_spec] * n),
        scratch_shapes=[pltpu.SemaphoreType.DMA((n, N_DEV - 1)), pltpu.SemaphoreType.DMA((n, N_DEV - 1)),
                        pltpu.SemaphoreType.DMA((n,))],
    )(*arrays)


def _adaln_mod(c, w_mod, b_mod):
    def body(c_ref, w_ref, b_ref, mod_ref, sc_ref):
        sc = _silu(c_ref[...])
        sc8 = jnp.broadcast_to(sc, (8, D_MODEL))
        mod_ref[...] = _nn(sc8, w_ref[...])[0:1] + b_ref[...]
        sc_ref[...] = sc

    return pl.pallas_call(body, name="adaln_mod", compiler_params=_params(),
                          out_shape=(SDS((1, 3 * D_MODEL), F32), SDS((1, D_MODEL), F32)))(c, w_mod, b_mod)


def _ln_proj(x, mod, norm_w, ws, cos_t, sin_t, ts):
    s = x.shape[0]
    widths = [w.shape[1] for w in ws]

    def body(x_ref, mod_ref, nw_ref, cos_ref, sin_ref, wqkv, wz, wba, waq, wak, wav, waz,
             h_ref, oqkv, oz, oba, oq, ok, ov, oaz):
        xt = x_ref[...]
        r = lax.rsqrt(jnp.mean(xt * xt, axis=-1, keepdims=True) + EPS)
        shift, scale = mod_ref[:, 0:D_MODEL], mod_ref[:, D_MODEL:2 * D_MODEL]
        h = ((xt * r) * nw_ref[...]) * (1.0 + scale) + shift
        hb = _bf(h)
        h_ref[...] = hb
        oqkv[...] = jnp.dot(hb, wqkv[...], preferred_element_type=F32)
        oz[...] = jnp.dot(hb, wz[...], preferred_element_type=F32)
        oba[...] = jnp.dot(hb, wba[...], preferred_element_type=F32)
        oaz[...] = jnp.dot(hb, waz[...], preferred_element_type=F32)
        tv = jnp.dot(hb, wav[...], preferred_element_type=F32)
        for j in range(AT_PAIRS):
            ov[j] = tv[:, j * LANES:(j + 1) * LANES]
        cs, sn = cos_ref[...], sin_ref[...]
        for w_ref, o_ref in ((waq, oq), (wak, ok)):
            t = jnp.dot(hb, w_ref[...], preferred_element_type=F32)
            for j in range(AT_PAIRS):
                tj = t[:, j * LANES:(j + 1) * LANES]
                o_ref[j] = tj * cs + _swap_half64(tj) * sn

    tok = lambda w: pl.BlockSpec((ts, w), lambda i: (i, 0))
    full = lambda a: pl.BlockSpec(a.shape, lambda i: (0, 0))
    pairs = pl.BlockSpec((AT_PAIRS, ts, LANES), lambda i: (0, i, 0))
    return pl.pallas_call(
        body, name="ln_proj", grid=(s // ts,), compiler_params=_params("arbitrary"),
        in_specs=[tok(D_MODEL), full(mod), full(norm_w), tok(LANES), tok(LANES)] + [full(w) for w in ws],
        out_specs=(tok(D_MODEL), tok(widths[0]), tok(widths[1]), tok(widths[2]), pairs, pairs, pairs,
                   tok(widths[6])),
        out_shape=(SDS((s, D_MODEL), BF16), SDS((s, widths[0]), F32), SDS((s, widths[1]), F32),
                   SDS((s, widths[2]), F32)) + (SDS((AT_PAIRS, s, LANES), F32),) * 3 + (SDS((s, widths[6]), F32),),
    )(x, mod, norm_w, cos_t, sin_t, *ws)


def _conv_taps(ext, rows):
    taps = []
    for j in range(CONV_K):
        sh = CONV_K - 1 - j
        rolled = pltpu.roll(ext, sh, 0) if sh else ext
        taps.append(rolled[8:8 + rows])
    return taps


def _dn_prep(qkv_pre, ba, conv_w8, alog_row, dtb_row, ts):
    s = qkv_pre.shape[0]
    cw = 3 * DN_WIDTH

    def body(pre_ref, halo_ref, ba_ref, cw_ref, al_ref, dtb_ref, q_ref, k_ref, v_ref, bg_ref):
        n = pl.program_id(0)
        prev = jnp.where(n == 0, 0.0, halo_ref[...])
        ext = jnp.concatenate([prev, pre_ref[...]], axis=0)
        taps = _conv_taps(ext, ts)
        conv = taps[0] * cw_ref[0:1, :]
        for j in range(1, CONV_K):
            conv = conv + taps[j] * cw_ref[j:j + 1, :]
        for h in range(DN_HEADS):
            cols = slice(h * DN_DIM, (h + 1) * DN_DIM)
            q_ref[:, cols] = _post_q(conv[:, h * DN_DIM:(h + 1) * DN_DIM])
            k_ref[:, cols] = _post_k(conv[:, DN_WIDTH + h * DN_DIM:DN_WIDTH + (h + 1) * DN_DIM])
            v_ref[:, cols] = _post_v(conv[:, 2 * DN_WIDTH + h * DN_DIM:2 * DN_WIDTH + (h + 1) * DN_DIM])
        bg = _beta_decay(ba_ref[...], al_ref[...], dtb_ref[...])
        lane = lax.broadcasted_iota(jnp.int32, bg.shape, 1)
        run = pltpu.roll(_chunk_cumsum(bg), DN_HEADS, 1)
        bg_ref[...] = jnp.where((lane >= GC_LANE) & (lane < GC_LANE + DN_HEADS), run, bg)

    tok = lambda w: pl.BlockSpec((ts, w), lambda i: (i, 0))
    full = lambda a: pl.BlockSpec(a.shape, lambda i: (0, 0))
    halo = pl.BlockSpec((8, cw), lambda i: (jnp.maximum(i * (ts // 8) - 1, 0), 0))
    return pl.pallas_call(
        body, name="dn_prep", grid=(s // ts,), compiler_params=_params("arbitrary"),
        in_specs=[tok(cw), halo, tok(BA_PAD), full(conv_w8), full(alog_row), full(dtb_row)],
        out_specs=(tok(DN_WIDTH), tok(DN_WIDTH), tok(DN_WIDTH), tok(BA_PAD)),
        out_shape=(SDS((s, DN_WIDTH), F32),) * 3 + (SDS((s, BA_PAD), F32),),
    )(qkv_pre, qkv_pre, ba, conv_w8, alog_row, dtb_row)


def _dn_chunk_prep(q, k, v, bg, ts):
    s = q.shape[0]
    ncs = ts // CHUNK

    def body(q_ref, k_ref, v_ref, bg_ref, u_ref, w_ref, qd_ref, kd_ref, p_ref, gl_ref, t_ref):
        def chunks(cg, carry):
            where = []
            for ci in (cg * CH_UNROLL + i for i in range(CH_UNROLL)):
                rows = pl.ds(pl.multiple_of(ci * CHUNK, CHUNK), CHUNK)
                rows8 = pl.ds(pl.multiple_of(ci * 8, 8), 8)
                where += [(rows, rows8, h, slice(h * DN_DIM, (h + 1) * DN_DIM)) for h in range(DN_HEADS)]
            bgs = [bg_ref[rows, :] for rows, _, _, _ in where]
            outs = _chunk_fwd([q_ref[rows, c] for rows, _, _, c in where], [k_ref[rows, c] for rows, _, _, c in where],
                              [v_ref[rows, c] for rows, _, _, c in where],
                              [b[:, h:h + 1] for b, (_, _, h, _) in zip(bgs, where)],
                              [b[:, GC_LANE + h:GC_LANE + h + 1] for b, (_, _, h, _) in zip(bgs, where)])
            for (rows, rows8, h, c), (u, w, p, qd, kd, gl, t) in zip(where, outs):
                u_ref[rows, c] = u
                w_ref[rows, c] = w
                qd_ref[rows, c] = qd
                kd_ref[rows, c] = kd
                p_ref[h, rows, :] = p
                t_ref[h, rows, :] = t
                gl_ref[rows8, c] = jnp.broadcast_to(gl, (8, DN_DIM))
            return carry

        lax.fori_loop(0, ncs // CH_UNROLL, chunks, 0)

    tok = lambda w: pl.BlockSpec((ts, w), lambda i: (i, 0))
    sq = pl.BlockSpec((DN_HEADS, ts, CHUNK), lambda i: (0, i, 0))
    return pl.pallas_call(
        body, name="dn_chunk_prep", grid=(s // ts,), compiler_params=_params("arbitrary"),
        in_specs=[tok(DN_WIDTH)] * 3 + [tok(BA_PAD)],
        out_specs=(tok(DN_WIDTH),) * 4 + (sq, pl.BlockSpec((ncs * 8, DN_WIDTH), lambda i: (i, 0)), sq),
        out_shape=(SDS((s, DN_WIDTH), F32),) * 4 + (SDS((DN_HEADS, s, CHUNK), F32),
                                                     SDS((s // CHUNK * 8, DN_WIDTH), F32),
                                                     SDS((DN_HEADS, s, CHUNK), F32)),
    )(q, k, v, bg)


def _dn_scan(u, w, qd, kd, p, gl, ts):
    s = u.shape[0]
    ncs = ts // CHUNK

    def body(u_ref, w_ref, qd_ref, kd_ref, p_ref, gl_ref, o_ref, vn_ref, st_ref, state):
        @pl.when(pl.program_id(0) == 0)
        def _():
            state[...] = jnp.zeros_like(state)

        def chunk(ci, carry):
            rows = pl.ds(pl.multiple_of(ci * CHUNK, CHUNK), CHUNK)
            rows8 = pl.ds(pl.multiple_of(ci * 8, 8), 8)
            srows = pl.ds(pl.multiple_of(ci * DN_DIM, DN_DIM), DN_DIM)
            hs = range(DN_HEADS)
            sl = [slice(h * DN_DIM, (h + 1) * DN_DIM) for h in hs]
            sf = [state[h] for h in hs]
            sb = [_bf(x) for x in sf]
            ws = [_nn(w_ref[rows, c], b) for c, b in zip(sl, sb)]
            qs = [_nn(qd_ref[rows, c], b) for c, b in zip(sl, sb)]
            vn = [u_ref[rows, c] - x for c, x in zip(sl, ws)]
            vb = [_bf(x) for x in vn]
            kv = [_tn(kd_ref[rows, c], b) for c, b in zip(sl, vb)]
            pv = [_nn(p_ref[h, rows, :], b) for h, b in zip(hs, vb)]
            for h in hs:
                state[h] = sf[h] * gl_ref[rows8, sl[h]][0:1] + kv[h]
            for h in hs:
                st_ref[srows, sl[h]] = sf[h]
                vn_ref[rows, sl[h]] = vn[h]
                o_ref[rows, sl[h]] = qs[h] + pv[h]
            return carry

        lax.fori_loop(0, ncs, chunk, 0)

    tok = lambda wd: pl.BlockSpec((ts, wd), lambda i: (i, 0))
    return pl.pallas_call(
        body, name="dn_scan", grid=(s // ts,), compiler_params=_params("arbitrary"),
        in_specs=[tok(DN_WIDTH)] * 4 + [pl.BlockSpec((DN_HEADS, ts, CHUNK), lambda i: (0, i, 0)),
                                        pl.BlockSpec((ncs * 8, DN_WIDTH), lambda i: (i, 0))],
        out_specs=(tok(DN_WIDTH), tok(DN_WIDTH), pl.BlockSpec((ncs * DN_DIM, DN_WIDTH), lambda i: (i, 0))),
        out_shape=(SDS((s, DN_WIDTH), F32), SDS((s, DN_WIDTH), F32), SDS((s // CHUNK * DN_DIM, DN_WIDTH), F32)),
        scratch_shapes=[pltpu.VMEM((DN_HEADS, DN_DIM, DN_DIM), F32)],
    )(u, w, qd, kd, p, gl)


LOG2E, LN2 = 1.4426950408889634, 0.6931471805599453
MASKED = -1e30


def _band_bias():
    qi = lax.broadcasted_iota(jnp.int32, (Q_BLOCK, 2 * Q_BLOCK), 0)
    kj = lax.broadcasted_iota(jnp.int32, (Q_BLOCK, 2 * Q_BLOCK), 1)
    rel = Q_BLOCK + qi - kj
    return jnp.where((rel >= 0) & (rel <= W_SUB), 0.0, MASKED)


def _first_bias(first):
    kj = lax.broadcasted_iota(jnp.int32, (1, 2 * Q_BLOCK), 1)
    return jnp.where((kj < Q_BLOCK) & first, MASKED, 0.0)


def _attn_combo(c, d):
    if d == 1:
        qs = pl.multiple_of(c * Q_BLOCK, Q_BLOCK)
        return qs, pl.multiple_of(ATT_BLK - Q_BLOCK + c * Q_BLOCK, Q_BLOCK), c == 0
    r, m = c % d, c // d
    qs = r + (d * Q_BLOCK) * m
    return qs, ATT_BLK + qs - d * Q_BLOCK, m == 0


def _rows(start, size, d):
    return pl.ds(pl.multiple_of(start, Q_BLOCK), size) if d == 1 else pl.ds(start, size, stride=d)


def _shift_in(ext, cur, n):
    @pl.when(n == 0)
    def _():
        ext[0:ATT_BLK, :] = jnp.zeros((ATT_BLK, LANES), F32)

    @pl.when(n > 0)
    def _():
        ext[0:ATT_BLK, :] = ext[ATT_BLK:2 * ATT_BLK, :]

    ext[ATT_BLK:2 * ATT_BLK, :] = cur


def _attn_fwd(qr, kr, vv):
    s = qr.shape[1]
    nblk = s // ATT_BLK
    scale = AT_DIM ** -0.5
    npat = len(DILATIONS)

    def body(q_ref, k_ref, v_ref, o_ref, lse_ref, kext, vext, o_p, l_p, bias_ref):
        n = pl.program_id(1)
        _shift_in(kext, k_ref[0], n)
        _shift_in(vext, v_ref[0], n)
        bias_ref[...] = _band_bias()
        lo = lax.broadcasted_iota(jnp.int32, (Q_BLOCK, LANES), 1) < AT_DIM
        for pi, d in enumerate(DILATIONS):
            def combo(c, pi=pi, d=d):
                qs, ks, m0 = _attn_combo(c, d)
                bias = bias_ref[...] + _first_bias((n == 0) & m0)
                q = _bf(q_ref[0, _rows(qs, Q_BLOCK, d), :])
                kk = _bf(kext[_rows(ks, 2 * Q_BLOCK, d), :])
                vb = _bf(vext[_rows(ks, 2 * Q_BLOCK, d), :])
                outs, lses = [], []
                for sel in (lo, ~lo):
                    qm = jnp.where(sel, q, jnp.zeros_like(q))
                    sc = lax.dot_general(qm, kk, _NT, preferred_element_type=F32) * (scale * LOG2E) + bias
                    mx = jnp.max(sc, axis=-1, keepdims=True)
                    pr = jnp.exp2(sc - mx)
                    l = jnp.sum(pr, axis=-1, keepdims=True)
                    outs.append(jnp.dot(_bf(pr), vb, preferred_element_type=F32) / l)
                    lses.append(mx * LN2 + jnp.log(l))
                return qs, jnp.where(lo, outs[0], outs[1]), jnp.where(lo, lses[0], lses[1])

            def group(g, carry, pi=pi, d=d, combo=combo):
                res = [combo(g * ATT_UNROLL + u) for u in range(ATT_UNROLL)]
                for qs, o, l in res:
                    o_p[pi, _rows(qs, Q_BLOCK, d), :] = o
                    l_p[pi, _rows(qs, Q_BLOCK, d), :] = l
                return carry

            lax.fori_loop(0, ATT_BLK // Q_BLOCK // ATT_UNROLL, group, 0)

        def merge(i, carry):
            rows = pl.ds(pl.multiple_of(i * 256, 256), 256)
            ls = [l_p[pi, rows, :] for pi in range(npat)]
            mx = jnp.maximum(jnp.maximum(ls[0], ls[1]), ls[2])
            es = [jnp.exp(l - mx) for l in ls]
            den = es[0] + es[1] + es[2]
            o_ref[0, rows, :] = (es[0] * o_p[0, rows, :] + es[1] * o_p[1, rows, :] + es[2] * o_p[2, rows, :]) / den
            lse_ref[0, rows, :] = mx + jnp.log(den)
            return carry

        lax.fori_loop(0, ATT_BLK // 256, merge, 0)

    blk = pl.BlockSpec((1, ATT_BLK, LANES), lambda j, n: (j, n, 0))
    return pl.pallas_call(
        body, name="attn_fwd", grid=(AT_PAIRS, nblk), compiler_params=_params("arbitrary", "arbitrary"),
        in_specs=[blk] * 3, out_specs=(blk, blk),
        out_shape=(SDS((AT_PAIRS, s, LANES), F32),) * 2,
        scratch_shapes=[pltpu.VMEM((2 * ATT_BLK, LANES), F32), pltpu.VMEM((2 * ATT_BLK, LANES), F32),
                        pltpu.VMEM((npat, ATT_BLK, LANES), F32), pltpu.VMEM((npat, ATT_BLK, LANES), F32),
                        pltpu.VMEM((Q_BLOCK, 2 * Q_BLOCK), F32)],
    )(qr, kr, vv)


def _mix_prep(o_dn, z_dn, o_at, z_at, dnw, atw2, ts):
    s = o_dn.shape[0]

    def body(odn, zdn, oat, zat, dnw_ref, atw_ref, cat_ref):
        for h in range(DN_HEADS):
            cols = slice(h * DN_DIM, (h + 1) * DN_DIM)
            cat_ref[:, cols] = _bf(_gate_dn(odn[:, cols], zdn[:, cols], dnw_ref[...]))
        for j in range(AT_PAIRS):
            cat_ref[:, DN_WIDTH + j * LANES:DN_WIDTH + (j + 1) * LANES] = _bf(
                _gate_at(oat[j], zat[:, j * LANES:(j + 1) * LANES], atw_ref[...], _hnn))

    tok = lambda w: pl.BlockSpec((ts, w), lambda i: (i, 0))
    full = lambda a: pl.BlockSpec(a.shape, lambda i: (0, 0))
    pairs = pl.BlockSpec((AT_PAIRS, ts, LANES), lambda i: (0, i, 0))
    return pl.pallas_call(
        body, name="mix_prep", grid=(s // ts,), compiler_params=_params("arbitrary"),
        in_specs=[tok(DN_WIDTH), tok(DN_WIDTH), pairs, tok(AT_WIDTH), full(dnw), full(atw2)],
        out_specs=tok(D_MODEL), out_shape=SDS((s, D_MODEL), BF16),
    )(o_dn, z_dn, o_at, z_at, dnw, atw2)


def _out_loss(cat, x, tgt, w_out, gate, fw, ts):
    s = x.shape[0]

    def body(cat_ref, x_ref, t_ref, w_ref, g_ref, fw_ref, dx2_ref, dcat_ref, gw_ref, dfw_ref, dgate_ref, loss_ref):
        @pl.when(pl.program_id(0) == 0)
        def _():
            gw_ref[...] = jnp.zeros_like(gw_ref)
            dfw_ref[...] = jnp.zeros_like(dfw_ref)
            dgate_ref[...] = jnp.zeros_like(dgate_ref)
            loss_ref[...] = jnp.zeros_like(loss_ref)

        catb = cat_ref[...]
        wb = w_ref[...]
        gate, fwv = g_ref[...], fw_ref[...]
        mix = jnp.dot(catb, wb, preferred_element_type=F32)
        x2 = x_ref[...] + gate * mix
        r2 = lax.rsqrt(jnp.mean(x2 * x2, axis=-1, keepdims=True) + EPS)
        xn2 = x2 * r2
        err = xn2 * fwv - t_ref[...]
        row = jnp.sum(err * err, axis=-1, keepdims=True) * (1.0 / D_MODEL)
        loss_ref[...] += 0.5 * jnp.sum(row, axis=0, keepdims=True)
        dy = err * (1.0 / D_MODEL)
        dfw_ref[...] += jnp.sum(dy * xn2, axis=0, keepdims=True)
        dxn = dy * fwv
        dx2 = r2 * (dxn - xn2 * jnp.mean(dxn * xn2, axis=-1, keepdims=True))
        dx2_ref[...] = dx2
        dgate_ref[...] += jnp.sum(dx2 * mix, axis=0, keepdims=True)
        dmix = _bf(gate * dx2)
        dcat_ref[...] = lax.dot_general(dmix, wb, _NT, preferred_element_type=F32)
        gw_ref[...] += lax.dot_general(catb, dmix, _TN, preferred_element_type=F32)

    tok = lambda w: pl.BlockSpec((ts, w), lambda i: (i, 0))
    full = lambda a: pl.BlockSpec(a.shape, lambda i: (0, 0))
    row = pl.BlockSpec((1, D_MODEL), lambda i: (0, 0))
    return pl.pallas_call(
        body, name="out_loss", grid=(s // ts,), compiler_params=_params("arbitrary"),
        in_specs=[tok(D_MODEL), tok(D_MODEL), tok(D_MODEL), full(w_out), full(gate), full(fw)],
        out_specs=(tok(D_MODEL), tok(D_MODEL), pl.BlockSpec((D_MODEL, D_MODEL), lambda i: (0, 0)), row, row,
                   pl.BlockSpec((1, 1), lambda i: (0, 0))),
        out_shape=(SDS((s, D_MODEL), F32), SDS((s, D_MODEL), F32), SDS((D_MODEL, D_MODEL), F32),
                   SDS((1, D_MODEL), F32), SDS((1, D_MODEL), F32), SDS((1, 1), F32)),
    )(cat, x, tgt, w_out, gate, fw)


def _mix_bwd(dcat, o_dn, z_dn, o_at, z_at, dnw, atw2, ts):
    s = dcat.shape[0]

    def body(dcat_ref, odn, zdn, oat, zat, dnw_ref, atw_ref, dodn, dzdn, doat, dzat, delta, ddnw, datw):
        @pl.when(pl.program_id(0) == 0)
        def _():
            ddnw[...] = jnp.zeros_like(ddnw)
            datw[...] = jnp.zeros_like(datw)

        for h in range(DN_HEADS):
            cols = slice(h * DN_DIM, (h + 1) * DN_DIM)
            _, vjp = jax.vjp(_gate_dn, odn[:, cols], zdn[:, cols], dnw_ref[...])
            do, dz, dw = vjp(dcat_ref[:, cols])
            dodn[:, cols] = do
            dzdn[:, cols] = _bf(dz)
            ddnw[...] += dw
        for j in range(AT_PAIRS):
            cols = slice(j * LANES, (j + 1) * LANES)
            o = oat[j]
            _, vjp = jax.vjp(functools.partial(_gate_at, hnn=_d_hnn), o, zat[:, cols], atw_ref[...])
            do, dz, dw = vjp(dcat_ref[:, DN_WIDTH + j * LANES:DN_WIDTH + (j + 1) * LANES])
            doat[j] = do
            dzat[:, cols] = _bf(dz)
            datw[...] += dw
            delta[j] = _hnn(do * o, _group_ones(1.0))

    tok = lambda w: pl.BlockSpec((ts, w), lambda i: (i, 0))
    full = lambda a: pl.BlockSpec(a.shape, lambda i: (0, 0))
    row = pl.BlockSpec((1, LANES), lambda i: (0, 0))
    pairs = pl.BlockSpec((AT_PAIRS, ts, LANES), lambda i: (0, i, 0))
    return pl.pallas_call(
        body, name="mix_bwd", grid=(s // ts,), compiler_params=_params("arbitrary"),
        in_specs=[tok(D_MODEL), tok(DN_WIDTH), tok(DN_WIDTH), pairs, tok(AT_WIDTH), full(dnw), full(atw2)],
        out_specs=(tok(DN_WIDTH), tok(DN_WIDTH), pairs, tok(AT_WIDTH), pairs, row, row),
        out_shape=(SDS((s, DN_WIDTH), F32), SDS((s, DN_WIDTH), BF16), SDS((AT_PAIRS, s, LANES), F32),
                   SDS((s, AT_WIDTH), BF16), SDS((AT_PAIRS, s, LANES), F32), SDS((1, LANES), F32),
                   SDS((1, LANES), F32)),
    )(dcat, o_dn, z_dn, o_at, z_at, dnw, atw2)


def _shift_acc(ext, n):
    @pl.when(n == 0)
    def _():
        ext[0:ATT_BLK, :] = jnp.zeros((ATT_BLK, LANES), F32)

    @pl.when(n > 0)
    def _():
        ext[0:ATT_BLK, :] = ext[ATT_BLK:2 * ATT_BLK, :]

    ext[ATT_BLK:2 * ATT_BLK, :] = jnp.zeros((ATT_BLK, LANES), F32)


def _attn_bwd(qr, kr, vv, do, lse, delta):
    s = qr.shape[1]
    nblk = s // ATT_BLK
    scale = AT_DIM ** -0.5

    def body(q_ref, k_ref, v_ref, do_ref, lse_ref, dl_ref, dq_ref, dk_ref, dv_ref, kext, vext, dkext, dvext,
             bias_ref):
        n = pl.program_id(1)
        _shift_in(kext, k_ref[0], n)
        _shift_in(vext, v_ref[0], n)
        _shift_acc(dkext, n)
        _shift_acc(dvext, n)
        bias_ref[...] = _band_bias()

        @pl.when(n < nblk)
        def _():
            dq_ref[0] = jnp.zeros((ATT_BLK, LANES), F32)
            lo = lax.broadcasted_iota(jnp.int32, (Q_BLOCK, LANES), 1) < AT_DIM
            for d in DILATIONS:
                def combo(c, d=d):
                    qs, ks, m0 = _attn_combo(c, d)
                    bias = bias_ref[...] + _first_bias((n == 0) & m0)
                    qrows, krows = _rows(qs, Q_BLOCK, d), _rows(ks, 2 * Q_BLOCK, d)
                    q = _bf(q_ref[0, qrows, :])
                    kk = _bf(kext[krows, :])
                    vb = _bf(vext[krows, :])
                    dob = _bf(do_ref[0, qrows, :])
                    lse2, dl2 = lse_ref[0, qrows, :], dl_ref[0, qrows, :]
                    dkk = jnp.zeros((2 * Q_BLOCK, LANES), F32)
                    dvv = jnp.zeros((2 * Q_BLOCK, LANES), F32)
                    dqs = []
                    for sel in (lo, ~lo):
                        qm = jnp.where(sel, q, jnp.zeros_like(q))
                        dom = jnp.where(sel, dob, jnp.zeros_like(dob))
                        lse_c = jnp.max(jnp.where(sel, lse2, -jnp.inf), axis=-1, keepdims=True)
                        dl_c = jnp.max(jnp.where(sel, dl2, -jnp.inf), axis=-1, keepdims=True)
                        sc = lax.dot_general(qm, kk, _NT, preferred_element_type=F32) * (scale * LOG2E) + bias
                        pr = jnp.exp2(sc - lse_c * LOG2E)
                        dp = lax.dot_general(dom, vb, _NT, preferred_element_type=F32)
                        ds = _bf(pr * (dp - dl_c) * scale)
                        dqs.append(jnp.dot(ds, kk, preferred_element_type=F32))
                        dkk = dkk + lax.dot_general(ds, qm, _TN, preferred_element_type=F32)
                        dvv = dvv + lax.dot_general(_bf(pr), dom, _TN, preferred_element_type=F32)
                    return qrows, krows, jnp.where(lo, dqs[0], dqs[1]), dkk, dvv

                def group(g, carry, combo=combo):
                    res = [combo(g * ATT_UNROLL_BWD + u) for u in range(ATT_UNROLL_BWD)]
                    for qrows, krows, dq, dkk, dvv in res:
                        dq_ref[0, qrows, :] += dq
                        dkext[krows, :] += dkk
                        dvext[krows, :] += dvv
                    return carry

                lax.fori_loop(0, ATT_BLK // Q_BLOCK // ATT_UNROLL_BWD, group, 0)

        dk_ref[0] = dkext[0:ATT_BLK, :]
        dv_ref[0] = dvext[0:ATT_BLK, :]

    cur = pl.BlockSpec((1, ATT_BLK, LANES), lambda j, n: (j, jnp.minimum(n, nblk - 1), 0))
    done = pl.BlockSpec((1, ATT_BLK, LANES), lambda j, n: (j, jnp.maximum(n - 1, 0), 0))
    return pl.pallas_call(
        body, name="attn_bwd", grid=(AT_PAIRS, nblk + 1), compiler_params=_params("arbitrary", "arbitrary"),
        in_specs=[cur] * 6, out_specs=(cur, done, done),
        out_shape=(SDS((AT_PAIRS, s, LANES), F32),) * 3,
        scratch_shapes=[pltpu.VMEM((2 * ATT_BLK, LANES), F32)] * 4 + [pltpu.VMEM((Q_BLOCK, 2 * Q_BLOCK), F32)],
    )(qr, kr, vv, do, lse, delta)


def _rope_bwd(dq, dk, dv, cos_t, sin_t, ts):
    s = cos_t.shape[0]

    def body(q_ref, k_ref, v_ref, cos_ref, sin_ref, oq, ok, ov):
        cs, sn = cos_ref[...], sin_ref[...]
        for j in range(AT_PAIRS):
            cols = slice(j * LANES, (j + 1) * LANES)
            for g_ref, o_ref in ((q_ref, oq), (k_ref, ok)):
                g = g_ref[j]
                o_ref[:, cols] = _bf(g * cs + _swap_half64(g * sn))
            ov[:, cols] = _bf(v_ref[j])

    tok = lambda w: pl.BlockSpec((ts, w), lambda i: (i, 0))
    pairs = pl.BlockSpec((AT_PAIRS, ts, LANES), lambda i: (0, i, 0))
    return pl.pallas_call(
        body, name="rope_bwd", grid=(s // ts,), compiler_params=_params("arbitrary"),
        in_specs=[pairs] * 3 + [tok(LANES)] * 2, out_specs=(tok(AT_WIDTH),) * 3,
        out_shape=(SDS((s, AT_WIDTH), BF16),) * 3,
    )(dq, dk, dv, cos_t, sin_t)


def _dn_scan_bwd(do, st, vn, w, qd, kd, p, gl, ts):
    s = do.shape[0]
    ncs = ts // CHUNK
    nt = s // ts

    def body(do_ref, st_ref, vn_ref, w_ref, qd_ref, kd_ref, p_ref, gl_ref,
             du_ref, dw_ref, dqd_ref, dkd_ref, dp_ref, dgl_ref, dstate):
        @pl.when(pl.program_id(0) == 0)
        def _():
            dstate[...] = jnp.zeros_like(dstate)

        def chunk(jr, carry):
            ci = ncs - 1 - jr
            rows = pl.ds(pl.multiple_of(ci * CHUNK, CHUNK), CHUNK)
            rows8 = pl.ds(pl.multiple_of(ci * 8, 8), 8)
            srows = pl.ds(pl.multiple_of(ci * DN_DIM, DN_DIM), DN_DIM)
            hs = range(DN_HEADS)
            sl = [slice(h * DN_DIM, (h + 1) * DN_DIM) for h in hs]
            ds_ = [dstate[h] for h in hs]
            dsb = [_bf(x) for x in ds_]
            dob = [_bf(do_ref[rows, c]) for c in sl]
            pdo = [_tn(p_ref[h, rows, :], b) for h, b in zip(hs, dob)]
            qdo = [_tn(qd_ref[rows, c], b) for c, b in zip(sl, dob)]
            dvn = [_nn(kd_ref[rows, c], b) + x for c, b, x in zip(sl, dsb, pdo)]
            dvb = [_bf(x) for x in dvn]
            wdv = [_tn(w_ref[rows, c], b) for c, b in zip(sl, dvb)]
            for h in hs:
                dstate[h] = ds_[h] * gl_ref[rows8, sl[h]][0:1] + qdo[h] - wdv[h]
            sfs = [st_ref[srows, c] for c in sl]
            sbs = [_bf(x) for x in sfs]
            vnb = [_bf(vn_ref[rows, c]) for c in sl]
            for h in hs:
                du_ref[rows, sl[h]] = dvn[h]
                dw_ref[rows, sl[h]] = -_nt(dvb[h], sbs[h])
                dqd_ref[rows, sl[h]] = _nt(dob[h], sbs[h])
                dkd_ref[rows, sl[h]] = _nt(vnb[h], dsb[h])
                dp_ref[h, rows, :] = _nt(dob[h], vnb[h])
                dgl = jnp.sum(jnp.sum(ds_[h] * sfs[h], axis=1, keepdims=True), axis=0, keepdims=True)
                dgl_ref[rows8, sl[h]] = jnp.broadcast_to(dgl, (8, DN_DIM))
            return carry

        lax.fori_loop(0, ncs, chunk, 0)

    tok = lambda wd: pl.BlockSpec((ts, wd), lambda i: (nt - 1 - i, 0))
    pspec = pl.BlockSpec((DN_HEADS, ts, CHUNK), lambda i: (0, nt - 1 - i, 0))
    g8 = pl.BlockSpec((ncs * 8, DN_WIDTH), lambda i: (nt - 1 - i, 0))
    return pl.pallas_call(
        body, name="dn_scan_bwd", grid=(nt,), compiler_params=_params("arbitrary"),
        in_specs=[tok(DN_WIDTH), pl.BlockSpec((ncs * DN_DIM, DN_WIDTH), lambda i: (nt - 1 - i, 0))]
        + [tok(DN_WIDTH)] * 4 + [pspec, g8],
        out_specs=(tok(DN_WIDTH),) * 4 + (pspec, g8),
        out_shape=(SDS((s, DN_WIDTH), F32),) * 4 + (SDS((DN_HEADS, s, CHUNK), F32),
                                                     SDS((s // CHUNK * 8, DN_WIDTH), F32)),
        scratch_shapes=[pltpu.VMEM((DN_HEADS, DN_DIM, DN_DIM), F32)],
    )(do, st, vn, w, qd, kd, p, gl)


def _dn_chunk_bwd(q, k, v, bg, t, du, dw, dqd, dkd, dp, dgl, ts):
    s = q.shape[0]
    ncs = ts // CHUNK

    def body(q_ref, k_ref, v_ref, bg_ref, t_ref, du_ref, dw_ref, dqd_ref, dkd_ref, dp_ref, dgl_ref,
             dq_ref, dk_ref, dv_ref, dbg_ref):
        def chunks(cg, carry):
            lane = lax.broadcasted_iota(jnp.int32, (CHUNK, BA_PAD), 1)
            where = []
            for ci in (cg * CH_UNROLL + i for i in range(CH_UNROLL)):
                rows = pl.ds(pl.multiple_of(ci * CHUNK, CHUNK), CHUNK)
                rows8 = pl.ds(pl.multiple_of(ci * 8, 8), 8)
                where += [(rows, rows8, h, slice(h * DN_DIM, (h + 1) * DN_DIM)) for h in range(DN_HEADS)]
            bgs = [bg_ref[rows, :] for rows, _, _, _ in where]
            cots = [(du_ref[rows, c], dw_ref[rows, c], dp_ref[h, rows, :], dqd_ref[rows, c], dkd_ref[rows, c],
                     dgl_ref[rows8, c][0:1, 0:1]) for rows, rows8, h, c in where]
            outs = _chunk_bwd([q_ref[rows, c] for rows, _, _, c in where], [k_ref[rows, c] for rows, _, _, c in where],
                              [v_ref[rows, c] for rows, _, _, c in where],
                              [b[:, h:h + 1] for b, (_, _, h, _) in zip(bgs, where)],
                              [b[:, GC_LANE + h:GC_LANE + h + 1] for b, (_, _, h, _) in zip(bgs, where)],
                              [t_ref[h, rows, :] for rows, _, h, _ in where], cots)
            for i in range(CH_UNROLL):
                dbg = jnp.zeros((CHUNK, BA_PAD), F32)
                for (rows, _, h, c), (dq, dk, dv, dbeta, dgc) in list(zip(where, outs))[i * DN_HEADS:(i + 1) * DN_HEADS]:
                    dq_ref[rows, c] = dq
                    dk_ref[rows, c] = dk
                    dv_ref[rows, c] = dv
                    dbg = dbg + jnp.where(lane == h, dbeta, 0.0) + jnp.where(lane == GC_LANE + h, dgc, 0.0)
                dbg_ref[where[i * DN_HEADS][0], :] = dbg
            return carry

        lax.fori_loop(0, ncs // CH_UNROLL, chunks, 0)

    tok = lambda wd: pl.BlockSpec((ts, wd), lambda i: (i, 0))
    pspec = pl.BlockSpec((DN_HEADS, ts, CHUNK), lambda i: (0, i, 0))
    g8 = pl.BlockSpec((ncs * 8, DN_WIDTH), lambda i: (i, 0))
    return pl.pallas_call(
        body, name="dn_chunk_bwd", grid=(s // ts,), compiler_params=_params("arbitrary"),
        in_specs=[tok(DN_WIDTH)] * 3 + [tok(BA_PAD), pspec] + [tok(DN_WIDTH)] * 4 + [pspec, g8],
        out_specs=(tok(DN_WIDTH),) * 3 + (tok(BA_PAD),),
        out_shape=(SDS((s, DN_WIDTH), F32),) * 3 + (SDS((s, BA_PAD), F32),),
    )(q, k, v, bg, t, du, dw, dqd, dkd, dp, dgl)


def _dn_prep_bwd(qkv_pre, ba, dq, dk, dv, dbg, conv_w8, alog_row, dtb_row, ts):
    s = qkv_pre.shape[0]
    cw = 3 * DN_WIDTH
    nt = s // ts

    def body(pre_ref, ph_ref, nh_ref, ba_ref, dq_ref, dqh_ref, dk_ref, dkh_ref, dv_ref, dvh_ref, dbg_ref,
             cw_ref, al_ref, dtb_ref, dpre_ref, dba_ref, dcw_ref, dal_ref, ddtb_ref):
        n = pl.program_id(0)

        @pl.when(n == 0)
        def _():
            dcw_ref[...] = jnp.zeros_like(dcw_ref)
            dal_ref[...] = jnp.zeros_like(dal_ref)
            ddtb_ref[...] = jnp.zeros_like(ddtb_ref)

        last = n == nt - 1
        prev = jnp.where(n == 0, 0.0, ph_ref[...])
        ext = jnp.concatenate([prev, pre_ref[...], nh_ref[...]], axis=0)
        taps = _conv_taps(ext, ts + 8)
        conv = taps[0] * cw_ref[0:1, :]
        for j in range(1, CONV_K):
            conv = conv + taps[j] * cw_ref[j:j + 1, :]

        def cot(main, halo, cols):
            return jnp.concatenate([main[:, cols], jnp.where(last, 0.0, halo[:, cols])], axis=0)

        pieces = []
        for grp, (fn, mref, href) in enumerate(((_post_q, dq_ref, dqh_ref), (_post_k, dk_ref, dkh_ref),
                                                (_post_v, dv_ref, dvh_ref))):
            for h in range(DN_HEADS):
                cols = slice(h * DN_DIM, (h + 1) * DN_DIM)
                c0 = grp * DN_WIDTH + h * DN_DIM
                _, vjp = jax.vjp(fn, conv[:, c0:c0 + DN_DIM])
                pieces.append(vjp(cot(mref, href, cols))[0])
        dconv = jnp.concatenate(pieces, axis=1)
        rows = ts + 8
        dpre = dconv[:ts] * cw_ref[CONV_K - 1:CONV_K, :]
        for j in range(CONV_K - 1):
            sh = CONV_K - 1 - j
            dpre = dpre + pltpu.roll(dconv, rows - sh, 0)[:ts] * cw_ref[j:j + 1, :]
        dpre_ref[...] = _bf(dpre)
        for j in range(CONV_K):
            dcw_ref[j:j + 1, :] += jnp.sum(dconv[:ts] * taps[j][:ts], axis=0, keepdims=True)

        dbg = dbg_ref[...]
        lane = lax.broadcasted_iota(jnp.int32, dbg.shape, 1)
        dg = pltpu.roll(_chunk_cumsum(dbg, reverse=True), BA_PAD - DN_HEADS, 1)
        cot_bg = jnp.where(lane < DN_HEADS, dbg, jnp.where(lane < GC_LANE, dg, 0.0))
        _, vjp = jax.vjp(_beta_decay, ba_ref[...], al_ref[...], dtb_ref[...])
        dba, dal, ddtb = vjp(cot_bg)
        dba_ref[...] = _bf(dba)
        dal_ref[...] += dal
        ddtb_ref[...] += ddtb

    tok = lambda w: pl.BlockSpec((ts, w), lambda i: (i, 0))
    full = lambda a: pl.BlockSpec(a.shape, lambda i: (0, 0))
    prevh = lambda w: pl.BlockSpec((8, w), lambda i: (jnp.maximum(i * (ts // 8) - 1, 0), 0))
    nexth = lambda w: pl.BlockSpec((8, w), lambda i: (jnp.minimum((i + 1) * (ts // 8), s // 8 - 1), 0))
    row = pl.BlockSpec((1, LANES), lambda i: (0, 0))
    return pl.pallas_call(
        body, name="dn_prep_bwd", grid=(nt,), compiler_params=_params("arbitrary"),
        in_specs=[tok(cw), prevh(cw), nexth(cw), tok(BA_PAD),
                  tok(DN_WIDTH), nexth(DN_WIDTH), tok(DN_WIDTH), nexth(DN_WIDTH), tok(DN_WIDTH), nexth(DN_WIDTH),
                  tok(BA_PAD), full(conv_w8), full(alog_row), full(dtb_row)],
        out_specs=(tok(cw), tok(BA_PAD), pl.BlockSpec((8, cw), lambda i: (0, 0)), row, row),
        out_shape=(SDS((s, cw), BF16), SDS((s, BA_PAD), BF16), SDS((8, cw), F32), SDS((1, LANES), F32),
                   SDS((1, LANES), F32)),
    )(qkv_pre, qkv_pre, qkv_pre, ba, dq, dq, dk, dk, dv, dv, dbg, conv_w8, alog_row, dtb_row)


def _dh_dx(dps, ws, x, mod, norm_w, dx2, ts):
    s = x.shape[0]
    widths = [w.shape[1] for w in ws]
    np_ = len(ws)

    def body(*refs):
        dp_refs, w_refs = refs[:np_], refs[np_:2 * np_]
        x_ref, mod_ref, nw_ref, dx2_ref, gx_ref, dshift, dscale, dnw = refs[2 * np_:]

        @pl.when(pl.program_id(0) == 0)
        def _():
            dshift[...] = jnp.zeros_like(dshift)
            dscale[...] = jnp.zeros_like(dscale)
            dnw[...] = jnp.zeros_like(dnw)

        dh = lax.dot_general(dp_refs[0][...], w_refs[0][...], _NT, preferred_element_type=F32)
        for a, b in zip(dp_refs[1:], w_refs[1:]):
            dh = dh + lax.dot_general(a[...], b[...], _NT, preferred_element_type=F32)
        xt = x_ref[...]
        r = lax.rsqrt(jnp.mean(xt * xt, axis=-1, keepdims=True) + EPS)
        xn = xt * r
        nw = nw_ref[...]
        sc1 = 1.0 + mod_ref[:, D_MODEL:2 * D_MODEL]
        dshift[...] += jnp.sum(dh, axis=0, keepdims=True)
        dscale[...] += jnp.sum(dh * (xn * nw), axis=0, keepdims=True)
        dnw[...] += jnp.sum(dh * sc1 * xn, axis=0, keepdims=True)
        dxn = dh * sc1 * nw
        gx_ref[...] = r * (dxn - xn * jnp.mean(dxn * xn, axis=-1, keepdims=True)) + dx2_ref[...]

    tok = lambda w: pl.BlockSpec((ts, w), lambda i: (i, 0))
    full = lambda a: pl.BlockSpec(a.shape, lambda i: (0, 0))
    row = pl.BlockSpec((1, D_MODEL), lambda i: (0, 0))
    return pl.pallas_call(
        body, name="dh_dx", grid=(s // ts,), compiler_params=_params("arbitrary"),
        in_specs=[tok(w) for w in widths] + [full(w) for w in ws] + [tok(D_MODEL), full(mod), full(norm_w),
                                                                    tok(D_MODEL)],
        out_specs=(tok(D_MODEL), row, row, row),
        out_shape=(SDS((s, D_MODEL), F32),) + (SDS((1, D_MODEL), F32),) * 3,
    )(*dps, *ws, x, mod, norm_w, dx2)


def _grad_w_in(h, dps, ts, name):
    s = h.shape[0]
    widths = [p.shape[1] for p in dps]
    np_ = len(dps)

    def body(*refs):
        h_ref, dp_refs, outs = refs[0], refs[1:1 + np_], refs[1 + np_:]

        @pl.when(pl.program_id(0) == 0)
        def _():
            for o in outs:
                o[...] = jnp.zeros_like(o)

        hb = h_ref[...]
        for p, o in zip(dp_refs, outs):
            o[...] += lax.dot_general(hb, p[...], _TN, preferred_element_type=F32)

    tok = lambda w: pl.BlockSpec((ts, w), lambda i: (i, 0))
    return pl.pallas_call(
        body, name=name, grid=(s // ts,), compiler_params=_params("arbitrary"),
        in_specs=[tok(D_MODEL)] + [tok(w) for w in widths],
        out_specs=tuple(pl.BlockSpec((D_MODEL, w), lambda i: (0, 0)) for w in widths),
        out_shape=tuple(SDS((D_MODEL, w), F32) for w in widths),
    )(h, *dps)


def _adamw_math(w, g, m, v):
    m = ADAM_B1 * m + (1.0 - ADAM_B1) * g
    v = ADAM_B2 * v + (1.0 - ADAM_B2) * (g * g)
    m_hat = m / (1.0 - ADAM_B1 ** ADAM_STEP)
    v_hat = v / (1.0 - ADAM_B2 ** ADAM_STEP)
    delta = -ADAM_LR * (m_hat / (jnp.sqrt(v_hat) + ADAM_EPS) + ADAM_WD * w)
    return delta, m, v


def _adamw(w, m, v, g, name, slots=False):
    def body(w_ref, m_ref, v_ref, g_ref, g_out, d_out, m_out, v_out):
        if slots:
            g = g_ref[0].astype(F32)
            for k in range(1, N_DEV):
                g = g + g_ref[k].astype(F32)
        else:
            g = g_ref[...]
        g_out[...] = g
        d_out[...], m_out[...], v_out[...] = _adamw_math(w_ref[...], g, m_ref[...], v_ref[...])

    return pl.pallas_call(body, name=name, compiler_params=_params(),
                          out_shape=(SDS(w.shape, F32),) * 4)(w, m, v, g)


def _adamw_w_mod(w, m, v, siluc_all, dmod_mine):
    def body(w_ref, m_ref, v_ref, sc_ref, dm_ref, g_out, d_out, m_out, v_out):
        g = _htn(sc_ref[...], dm_ref[...])
        g_out[...] = g
        d_out[...], m_out[...], v_out[...] = _adamw_math(w_ref[...], g, m_ref[...], v_ref[...])

    return pl.pallas_call(body, name="adamw_w_mod", compiler_params=_params(),
                          out_shape=(SDS(w.shape, F32),) * 4)(w, m, v, siluc_all, dmod_mine)


def _pack_sum(pack_all):
    def body(p_ref, o_ref):
        t = p_ref[0]
        for k in range(1, N_DEV):
            t = t + p_ref[k]
        o_ref[...] = t

    return pl.pallas_call(body, name="pack_sum", out_shape=SDS(pack_all.shape[1:], F32))(pack_all)


def _tile(s, want):
    t = min(want, s)
    assert s % t == 0
    return t


def _local_step(x, c, positions, w_mod_bf, b_mod, norm_w, w_in_bf, conv_w, a_log, dt_bias, dn_norm_w, at_norm_w,
                w_out_bf, final_norm_w, tgt):
    s = x.shape[0]
    o = [0]
    for wdt in IN_SPLITS:
        o.append(o[-1] + wdt)
    w_ba = jnp.pad(w_in_bf[:, o[2]:o[4]], ((0, 0), (0, BA_PAD - 2 * DN_HEADS)))
    ws = [w_in_bf[:, o[0]:o[1]], w_in_bf[:, o[1]:o[2]], w_ba, w_in_bf[:, o[4]:o[5]], w_in_bf[:, o[5]:o[6]],
          w_in_bf[:, o[6]:o[7]], w_in_bf[:, o[7]:o[8]]]
    conv_w8 = jnp.pad(conv_w, ((0, 8 - CONV_K), (0, 0)))
    alog_row = jnp.pad(a_log, ((0, 0), (DN_HEADS, BA_PAD - 2 * DN_HEADS)))
    dtb_row = jnp.pad(dt_bias, ((0, 0), (DN_HEADS, BA_PAD - 2 * DN_HEADS)))
    atw2 = jnp.concatenate([at_norm_w, at_norm_w], axis=1)

    half = AT_DIM // 2
    lane = jnp.arange(LANES)
    inv_freq = ROPE_THETA ** (-(lane % half).astype(F32) / half)
    ang = positions.astype(F32)[:, None] * inv_freq
    cos_t = jnp.cos(ang)
    sin_t = jnp.sin(ang) * jnp.where((lane // half) % 2 == 0, -1.0, 1.0)

    mod, siluc = _adaln_mod(c, w_mod_bf, b_mod)
    gate = mod[:, 2 * D_MODEL:]
    hbf, qkv_pre, z_dn, ba, qr, kr, vb, z_at = _ln_proj(x, mod, norm_w, ws, cos_t, sin_t, _tile(s, 256))
    q, k, v, bg = _dn_prep(qkv_pre, ba, conv_w8, alog_row, dtb_row, _tile(s, 256))
    u, w, qd, kd, p, gl, tinv = _dn_chunk_prep(q, k, v, bg, _tile(s, 512))
    o_dn, vn, st = _dn_scan(u, w, qd, kd, p, gl, _tile(s, 512))
    o_at, lse = _attn_fwd(qr, kr, vb)
    cat = _mix_prep(o_dn, z_dn, o_at, z_at, dn_norm_w, atw2, _tile(s, 512))
    dx2, dcat, gw_out, dfw, dgate, loss = _out_loss(cat, x, tgt, w_out_bf, gate, final_norm_w, _tile(s, 512))

    do_dn, dz_dn, do_at, dz_at, delta, ddnw, datw = _mix_bwd(dcat, o_dn, z_dn, o_at, z_at, dn_norm_w, atw2,
                                                             _tile(s, 512))
    daq, dak, dav = _rope_bwd(*_attn_bwd(qr, kr, vb, do_at, lse, delta), cos_t, sin_t, _tile(s, 512))
    du, dw, dqd, dkd, dp, dgl = _dn_scan_bwd(do_dn, st, vn, w, qd, kd, p, gl, _tile(s, 512))
    dq, dk, dv, dbg = _dn_chunk_bwd(q, k, v, bg, tinv, du, dw, dqd, dkd, dp, dgl, _tile(s, 512))
    dqkv, dba, dcw, dal, ddtb = _dn_prep_bwd(qkv_pre, ba, dq, dk, dv, dbg, conv_w8, alog_row, dtb_row, _tile(s, 256))
    dps = [dqkv, dz_dn, dba, daq, dak, dav, dz_at]
    gx, dshift, dscale, dnw = _dh_dx(dps, ws, x, mod, norm_w, dx2, _tile(s, 256))
    g_qkv, g_z, g_ba = _grad_w_in(hbf, dps[:3], _tile(s, 512), "grad_w_in_dn")
    g_aq, g_ak, g_av, g_az = _grad_w_in(hbf, dps[3:], _tile(s, 512), "grad_w_in_at")
    gw_in = jnp.concatenate([g_qkv, g_z, g_ba[:, :2 * DN_HEADS], g_aq, g_ak, g_av, g_az], axis=1)
    dmod = jnp.concatenate([dshift, dscale, dgate], axis=1)
    small = dict(conv=dcw[:CONV_K], dmod=dmod, siluc=siluc, dnw=dnw, dfw=dfw, alog=dal, dtb=ddtb, dnn=ddnw, atn=datw)
    return loss, gx, gw_in, gw_out, small


def kernel(x, c, positions, w_mod, b_mod, norm_w, w_in, conv_w, a_log, dt_bias, dn_norm_w, at_norm_w, w_out, final_norm_w, loss_target, m_w_mod, m_b_mod, m_norm_w, m_w_in, m_conv_w, m_a_log, m_dt_bias, m_dn_norm_w, m_at_norm_w, m_w_out, m_final_norm_w, v_w_mod, v_b_mod, v_norm_w, v_w_in, v_conv_w, v_a_log, v_dt_bias, v_dn_norm_w, v_at_norm_w, v_w_out, v_final_norm_w):
    me = 4 * lax.axis_index("x") + 2 * lax.axis_index("y") + lax.axis_index("c")
    s = x.shape[1]

    g_mod, g_in, g_conv, g_out = _exchange(
        [_bf(w_mod[0]), _bf(w_in[0]), conv_w[0], _bf(w_out[0])], [False] * 4, "gather_weights")
    w_mod_bf = g_mod.transpose(1, 0, 2).reshape(D_MODEL, 3 * D_MODEL)
    w_in_bf = g_in.transpose(1, 0, 2).reshape(D_MODEL, IN_COLS)
    conv_full = g_conv.transpose(1, 0, 2).reshape(CONV_K, 3 * DN_WIDTH)
    w_out_bf = g_out.reshape(D_MODEL, D_MODEL)

    loss, gx, gw_in, gw_out, small = _local_step(
        x[0], c, positions[0], w_mod_bf, b_mod, norm_w, w_in_bf, conv_full, a_log, dt_bias, dn_norm_w, at_norm_w,
        w_out_bf, final_norm_w.reshape(1, D_MODEL), loss_target[0])

    pack = jnp.concatenate([small["conv"].reshape(1, -1), small["dmod"], small["siluc"], small["dnw"], small["dfw"],
                            small["alog"], small["dtb"], small["dnn"], small["atn"],
                            jnp.pad(loss, ((0, 0), (0, LANES - 1)))], axis=1).reshape(PK_ROWS, LANES)
    gw_in_slabs = _bf(gw_in).reshape(D_MODEL, N_DEV, IN_SHARD).transpose(1, 0, 2)
    gw_out_slabs = _bf(gw_out).reshape(N_DEV, D_MODEL // N_DEV, D_MODEL)
    r_in, r_out, pack_all = _exchange([gw_in_slabs, gw_out_slabs, pack], [True, True, False], "exchange_grads")

    res = {}
    res["w_in"] = _adamw(w_in[0], m_w_in[0], v_w_in[0], r_in, "adamw_w_in", slots=True)
    res["w_out"] = _adamw(w_out[0], m_w_out[0], v_w_out[0], r_out, "adamw_w_out", slots=True)
    flat_all = pack_all.reshape(N_DEV, PK_END)
    dmod_mine = lax.dynamic_slice(flat_all, (0, PK_DMOD + me * (3 * D_MODEL // N_DEV)), (N_DEV, 3 * D_MODEL // N_DEV))
    res["w_mod"] = _adamw_w_mod(w_mod[0], m_w_mod[0], v_w_mod[0], flat_all[:, PK_SILUC:PK_DNW], dmod_mine)
    tot = _pack_sum(pack_all).reshape(1, PK_END)
    g_conv_full = tot[:, PK_CONV:PK_DMOD].reshape(CONV_K, 3 * DN_WIDTH)
    g_conv_mine = lax.dynamic_slice(g_conv_full, (0, me * (3 * DN_WIDTH // N_DEV)), (CONV_K, 3 * DN_WIDTH // N_DEV))
    res["conv_w"] = _adamw(conv_w[0], m_conv_w[0], v_conv_w[0], g_conv_mine, "adamw_conv_w")
    res["b_mod"] = _adamw(b_mod, m_b_mod, v_b_mod, tot[:, PK_DMOD:PK_SILUC], "adamw_b_mod")
    res["norm_w"] = _adamw(norm_w, m_norm_w, v_norm_w, tot[:, PK_DNW:PK_DFW], "adamw_norm_w")
    res["a_log"] = _adamw(a_log, m_a_log, v_a_log, tot[:, PK_ALOG + DN_HEADS:PK_ALOG + 2 * DN_HEADS], "adamw_a_log")
    res["dt_bias"] = _adamw(dt_bias, m_dt_bias, v_dt_bias, tot[:, PK_DTB + DN_HEADS:PK_DTB + 2 * DN_HEADS],
                            "adamw_dt_bias")
    res["dn_norm_w"] = _adamw(dn_norm_w, m_dn_norm_w, v_dn_norm_w, tot[:, PK_DNN:PK_ATN], "adamw_dn_norm_w")
    g_atn = tot[:, PK_ATN:PK_ATN + AT_DIM] + tot[:, PK_ATN + AT_DIM:PK_LOSS]
    res["at_norm_w"] = _adamw(at_norm_w, m_at_norm_w, v_at_norm_w, g_atn, "adamw_at_norm_w")
    fin = _adamw(final_norm_w.reshape(1, D_MODEL), m_final_norm_w.reshape(1, D_MODEL),
                 v_final_norm_w.reshape(1, D_MODEL), tot[:, PK_DFW:PK_ALOG], "adamw_final_norm_w")
    res["final_norm_w"] = tuple(a.reshape(D_MODEL) for a in fin)

    lead = ("w_mod", "w_in", "conv_w", "w_out")
    names = ("w_mod", "b_mod", "norm_w", "w_in", "conv_w", "a_log", "dt_bias", "dn_norm_w", "at_norm_w", "w_out",
             "final_norm_w")
    out = [tot[0, PK_LOSS], gx.reshape(1, s, D_MODEL)]
    for kind in range(4):
        for nm in names:
            a = res[nm][kind]
            out.append(a[None] if nm in lead else a)
    return tuple(out)
```

```python
import functools

import jax
import jax.numpy as jnp
from jax import lax
from jax.experimental import pallas as pl
from jax.experimental.pallas import tpu as pltpu

F32, BF16 = jnp.float32, jnp.bfloat16
HI = lax.Precision.HIGHEST
SDS = jax.ShapeDtypeStruct

D_MODEL = 1024
DN_HEADS, DN_DIM, DN_WIDTH = 4, 128, 512
AT_HEADS, AT_DIM, AT_WIDTH = 8, 64, 512
CONV_K = 4
CHUNK = 64
Q_BLOCK = 128
W_SUB = 128
DILATIONS = (1, 4, 16)
AT_PAIRS = 4
ATT_BLK = Q_BLOCK * max(DILATIONS)
ATT_UNROLL, ATT_UNROLL_BWD = 4, 4
CH_UNROLL = 4
ROPE_THETA = 10000.0
EPS = 1e-6
N_DEV = 8
LANES = 128
BA_PAD = 128
IN_SPLITS = (1536, 512, 4, 4, 512, 512, 512, 512)
IN_COLS = sum(IN_SPLITS)
IN_SHARD = IN_COLS // N_DEV
VMEM_LIMIT = 56 * 2 ** 20

ADAM_LR, ADAM_B1, ADAM_B2, ADAM_EPS, ADAM_WD, ADAM_STEP = 0.001, 0.9, 0.999, 1e-08, 0.01, 10

PK_CONV, PK_DMOD, PK_SILUC, PK_DNW, PK_DFW, PK_ALOG, PK_DTB, PK_DNN, PK_ATN, PK_LOSS, PK_END = (
    0, 6144, 9216, 10240, 11264, 12288, 12416, 12544, 12672, 12800, 12928)
PK_ROWS = PK_END // LANES

_NT = (((1,), (1,)), ((), ()))
_TN = (((0,), (0,)), ((), ()))


def _params(*sem):
    return pltpu.CompilerParams(dimension_semantics=sem or None, vmem_limit_bytes=VMEM_LIMIT)


def _bf(x):
    return x.astype(BF16)


def _nn(a, b):
    return jnp.dot(_bf(a), _bf(b), preferred_element_type=F32)


def _nt(a, b):
    return lax.dot_general(_bf(a), _bf(b), _NT, preferred_element_type=F32)


def _tn(a, b):
    return lax.dot_general(_bf(a), _bf(b), _TN, preferred_element_type=F32)


def _hnn(a, b):
    return jnp.dot(a, b, precision=HI, preferred_element_type=F32)


def _hnt(a, b):
    return lax.dot_general(a, b, _NT, precision=HI, preferred_element_type=F32)


def _htn(a, b):
    return lax.dot_general(a, b, _TN, precision=HI, preferred_element_type=F32)


@jax.custom_vjp
def _d_hnn(a, b):
    return _hnn(a, b)


def _d_hnn_fwd(a, b):
    return _hnn(a, b), (a, b)


def _d_hnn_bwd(res, g):
    a, b = res
    return _hnt(g, b), _htn(a, g)


_d_hnn.defvjp(_d_hnn_fwd, _d_hnn_bwd)


def _silu(x):
    return x * jax.nn.sigmoid(x)


def _softplus(x):
    return jnp.maximum(x, 0.0) + jnp.log(1.0 + jnp.exp(-jnp.abs(x)))


def _l2n(x):
    return x * lax.rsqrt(jnp.sum(x * x, axis=-1, keepdims=True) + EPS)


def _post_q(x):
    return _l2n(_silu(x)) * (DN_DIM ** -0.5)


def _post_k(x):
    return _l2n(_silu(x))


def _post_v(x):
    return _silu(x)


def _beta_decay(ba, alog_row, dtb_row):
    lane = lax.broadcasted_iota(jnp.int32, ba.shape, 1)
    return jnp.where(lane < DN_HEADS, jax.nn.sigmoid(ba), -jnp.exp(alog_row) * _softplus(ba + dtb_row))


def _gate_dn(o, z, w):
    return (o * lax.rsqrt(jnp.mean(o * o, axis=-1, keepdims=True) + EPS)) * w * _silu(z)


def _group_ones(scale):
    r = lax.broadcasted_iota(jnp.int32, (LANES, LANES), 0)
    c = lax.broadcasted_iota(jnp.int32, (LANES, LANES), 1)
    return jnp.where((r // AT_DIM) == (c // AT_DIM), scale, 0.0).astype(F32)


def _gate_at(o, z, w2, hnn):
    ms = hnn(o * o, _group_ones(1.0 / AT_DIM))
    return (o * lax.rsqrt(ms + EPS)) * w2 * _silu(z)


def _swap_half64(x):
    lane = lax.broadcasted_iota(jnp.int32, x.shape, 1)
    return jnp.where((lane & (AT_DIM - 1)) < AT_DIM // 2, pltpu.roll(x, LANES - AT_DIM // 2, 1),
                     pltpu.roll(x, AT_DIM // 2, 1))


_NN = (((1,), (0,)), ((), ()))


def _hl(a):
    hi = a.astype(BF16)
    return hi, (a - hi.astype(F32)).astype(BF16)


def _mm3(a, b, dims=_NN):
    (ah, al), (bh, bl) = a, b
    f = lambda x, y: lax.dot_general(x, y, dims, preferred_element_type=F32)
    return f(ah, bh) + (f(ah, bl) + f(al, bh))


def _chunk_masks():
    r = lax.broadcasted_iota(jnp.int32, (CHUNK, CHUNK), 0)
    c = lax.broadcasted_iota(jnp.int32, (CHUNK, CHUNK), 1)
    return r >= c, r > c, (r == c).astype(F32), (r // 16) == (c // 16)


def _tri_inv(mats):
    _, _, eye, blk = _chunk_masks()
    dg = [jnp.where(blk, a, 0.0) for a in mats]
    lo = [jnp.where(blk, 0.0, a) for a in mats]
    sdg = [_hl(x) for x in dg]
    d2 = [_mm3(s, s) for s in sdg]
    sd2 = [_hl(x) for x in d2]
    d4 = [_mm3(s, s) for s in sd2]
    sd4 = [_hl(x) for x in d4]
    d8 = [_mm3(s, s) for s in sd4]
    p1 = [_mm3(_hl(eye - a), _hl(eye + b)) for a, b in zip(dg, d2)]
    p2 = [_mm3(_hl(a), _hl(eye + b)) for a, b in zip(p1, d4)]
    dinv = [_mm3(_hl(a), _hl(eye + b)) for a, b in zip(p2, d8)]
    sdinv = [_hl(x) for x in dinv]
    n1 = [_mm3(s, _hl(b)) for s, b in zip(sdinv, lo)]
    sn1 = [_hl(x) for x in n1]
    n2 = [_mm3(s, s) for s in sn1]
    q1 = [_mm3(_hl(eye - a), _hl(eye + b)) for a, b in zip(n1, n2)]
    return [_mm3(_hl(a), s) for a, s in zip(q1, sdinv)]


def _chunk_common(qs, ks, vs, betas, gcs):
    tril, _, _, _ = _chunk_masks()
    out = []
    for q, k, v, beta, gc in zip(qs, ks, vs, betas, gcs):
        gb = jnp.broadcast_to(gc, (CHUNK, DN_DIM))
        gt = gb.T[:CHUNK, :]
        gam = jnp.where(tril, jnp.exp(jnp.where(tril, gb[:, :CHUNK] - gt, 0.0)), 0.0)
        last = gb[CHUNK - 1:CHUNK, :]
        eg, e2 = jnp.exp(gb), jnp.exp(last - gb)
        kb, vb = k * beta, v * beta
        out.append(dict(gam=gam, eg=eg, e2=e2, gl=jnp.exp(last[:, 0:1]), kb=kb, vb=vb, kbg=kb * eg,
                        m=_nt(kb, k), qk=_nt(q, k)))
    return out


def _chunk_fwd(qs, ks, vs, betas, gcs):
    tril, strict, _, _ = _chunk_masks()
    cm = _chunk_common(qs, ks, vs, betas, gcs)
    ts = _tri_inv([jnp.where(strict, c["m"] * c["gam"], 0.0) for c in cm])
    outs = []
    for q, k, c, t in zip(qs, ks, cm, ts):
        uw = _mm3(_hl(t), _hl(jnp.concatenate([c["vb"], c["kbg"]], axis=1)))
        p = jnp.where(tril, c["qk"] * c["gam"], 0.0)
        outs.append((uw[:, :DN_DIM], uw[:, DN_DIM:], p, q * c["eg"], k * c["e2"], c["gl"], t))
    return outs


def _chunk_bwd(qs, ks, vs, betas, gcs, ts, cots):
    tril, strict, _, _ = _chunk_masks()
    cm = _chunk_common(qs, ks, vs, betas, gcs)
    row = lax.broadcasted_iota(jnp.int32, (CHUNK, 1), 0)
    ones = jnp.ones((CHUNK, DN_DIM), BF16)
    rs = lambda x: jnp.sum(x, axis=-1, keepdims=True)
    sts = [_hl(t) for t in ts]
    duw = [_hl(jnp.concatenate([ct[0], ct[1]], axis=1)) for ct in cots]
    dts = [_mm3(a, _hl(jnp.concatenate([c["vb"], c["kbg"]], axis=1)), _NT) for a, c in zip(duw, cm)]
    xs = [_mm3(s, _hl(d), _TN) for s, d in zip(sts, dts)]
    das = [jnp.where(strict, -_mm3(_hl(x), s, _NT), 0.0) for x, s in zip(xs, sts)]
    dvks = [_mm3(s, a, _TN) for s, a in zip(sts, duw)]
    outs = []
    for q, k, v, beta, c, ct, da, dvk in zip(qs, ks, vs, betas, cm, cots, das, dvks):
        _, _, dp, dqd, dkd, dgl = ct
        dvb, dkbg = dvk[:, :DN_DIM], dvk[:, DN_DIM:]
        dm = da * c["gam"]
        dqk = jnp.where(tril, dp, 0.0) * c["gam"]
        e = dm * c["m"] + dqk * c["qk"]
        dmq = jnp.concatenate([dm, dqk], axis=0)
        r1 = _nn(dmq, k)
        dkb = r1[:CHUNK] + dkbg * c["eg"]
        dq = r1[CHUNK:] + dqd * c["eg"]
        dk = _tn(dmq, jnp.concatenate([c["kb"], q], axis=0)) + dkd * c["e2"] + dkb * beta
        dbeta = rs(dkb * k) + rs(dvb * v)
        eh, el = _hl(e)
        colsum = (lax.dot_general(eh, ones, _TN, preferred_element_type=F32)
                  + lax.dot_general(el, ones, _TN, preferred_element_type=F32))[:, 0:1]
        rs_kd = rs(dkd * (k * c["e2"]))
        dgc = rs(e) - colsum + rs(dqd * q * c["eg"]) + rs(dkbg * c["kbg"]) - rs_kd
        tail = jnp.sum(rs_kd, axis=0, keepdims=True) + dgl * c["gl"]
        dgc = dgc + jnp.where(row == CHUNK - 1, tail, 0.0)
        outs.append((dq, dk, dvb * beta, dbeta, dgc))
    return outs


def _chunk_cumsum(x, reverse=False):
    n = x.shape[0]
    pos = lax.broadcasted_iota(jnp.int32, x.shape, 0) & (CHUNK - 1)
    sh = 1
    while sh < CHUNK:
        if reverse:
            x = x + jnp.where(pos < CHUNK - sh, pltpu.roll(x, n - sh, 0), 0.0)
        else:
            x = x + jnp.where(pos >= sh, pltpu.roll(x, sh, 0), 0.0)
        sh *= 2
    return x


GC_LANE = 2 * DN_HEADS


def _exchange(arrays, scatter, name):
    n = len(arrays)
    out_shapes = []
    for a, sc in zip(arrays, scatter):
        out_shapes.append(SDS(a.shape if sc else (N_DEV,) + a.shape, a.dtype))

    def body(*refs):
        ins, outs = refs[:n], refs[n:2 * n]
        send_sems, recv_sems, loc_sems = refs[2 * n:]
        x, y, c = lax.axis_index("x"), lax.axis_index("y"), lax.axis_index("c")
        me = 4 * x + 2 * y + c
        local, remote = [], []
        for i in range(n):
            src = ins[i].at[me] if scatter[i] else ins[i]
            cp = pltpu.make_async_copy(src, outs[i].at[me], loc_sems.at[i])
            cp.start()
            local.append(cp)
        for dlt in range(1, N_DEV):
            px = 1 - x if dlt & 4 else x
            py = 1 - y if dlt & 2 else y
            pc = 1 - c if dlt & 1 else c
            peer = 4 * px + 2 * py + pc
            for i in range(n):
                src = ins[i].at[peer] if scatter[i] else ins[i]
                cp = pltpu.make_async_remote_copy(
                    src_ref=src, dst_ref=outs[i].at[me],
                    send_sem=send_sems.at[i, dlt - 1], recv_sem=recv_sems.at[i, dlt - 1],
                    device_id=(px, py, pc), device_id_type=pl.DeviceIdType.MESH)
                cp.start()
                arrive = pltpu.make_async_remote_copy(
                    src_ref=src, dst_ref=outs[i].at[peer],
                    send_sem=send_sems.at[i, dlt - 1], recv_sem=recv_sems.at[i, dlt - 1],
                    device_id=(px, py, pc), device_id_type=pl.DeviceIdType.MESH)
                remote.append((cp, arrive))
        for cp, arrive in remote:
            cp.wait_send()
            arrive.wait_recv()
        for cp in local:
            cp.wait()

    any_spec = pl.BlockSpec(memory_space=pl.ANY)
    return pl.pallas_call(
        body, name=name, out_shape=tuple(out_shapes),
        in_specs=[any_spec] * n, out_specs=tuple([any_spec] * n),
        scratch_shapes=[pltpu.SemaphoreType.DMA((n, N_DEV - 1)), pltpu.SemaphoreType.DMA((n, N_DEV - 1)),
                        pltpu.SemaphoreType.DMA((n,))],
    )(*arrays)


def _all_gather(arrays, name):
    n = len(arrays)

    def body(*refs):
        ins, outs = refs[:n], refs[n:2 * n]
        send_sems, recv_sems, loc_sems = refs[2 * n:]
        x, y, c = lax.axis_index("x"), lax.axis_index("y"), lax.axis_index("c")
        me, sibling = (x, y, c), (x, y, 1 - c)
        chips = [(1 - x, y), (x, 1 - y), (1 - x, 1 - y)]

        def copy(i, k, block, to, src=None):
            slot = outs[i].at[4 * block[0] + 2 * block[1] + block[2]]
            return pltpu.make_async_remote_copy(
                src_ref=slot if src is None else src, dst_ref=slot,
                send_sem=send_sems.at[i, k], recv_sem=recv_sems.at[i, k],
                device_id=to, device_id_type=pl.DeviceIdType.MESH)

        mine = [pltpu.make_async_copy(ins[i], outs[i].at[4 * x + 2 * y + c], loc_sems.at[i]) for i in range(n)]
        for cp in mine:
            cp.start()
        first = []
        for i in range(n):
            first.append(copy(i, 0, me, sibling, src=ins[i]))
            first += [copy(i, 1 + j, me, (*chip, c), src=ins[i]) for j, chip in enumerate(chips)]
        for cp in first:
            cp.start()
        passed = []
        for j, chip in enumerate(chips):
            for i in range(n):
                copy(i, 1 + j, (*chip, c), me).wait_recv()
                fwd = copy(i, 4 + j, (*chip, c), sibling)
                fwd.start()
                passed.append(fwd)
        for i in range(n):
            copy(i, 0, sibling, me).wait_recv()
        for j, chip in enumerate(chips):
            for i in range(n):
                copy(i, 4 + j, (*chip, 1 - c), me).wait_recv()
        for cp in first + passed:
            cp.wait_send()
        for cp in mine:
            cp.wait()

    any_spec = pl.BlockSpec(memory_space=pl.ANY)
    return pl.pallas_call(
        body, name=name, out_shape=tuple(SDS((N_DEV,) + a.shape, a.dtype) for a in arrays),
        in_specs=[any_spec] * n, out_specs=tuple([any_spec] * n),
        scratch_shapes=[pltpu.SemaphoreType.DMA((n, N_DEV - 1)), pltpu.SemaphoreType.DMA((n, N_DEV - 1)),
                        pltpu.SemaphoreType.DMA((n,))],
    )(*arrays)


def _adaln_mod(c, w_mod, b_mod):
    def body(c_ref, w_ref, b_ref, mod_ref, sc_ref):
        sc = _silu(c_ref[...])
        sc8 = jnp.broadcast_to(sc, (8, D_MODEL))
        mod_ref[...] = _nn(sc8, w_ref[...])[0:1] + b_ref[...]
        sc_ref[...] = sc

    return pl.pallas_call(body, name="adaln_mod", compiler_params=_params(),
                          out_shape=(SDS((1, 3 * D_MODEL), F32), SDS((1, D_MODEL), F32)))(c, w_mod, b_mod)


def _ln_proj(x, mod, norm_w, ws, cos_t, sin_t, ts):
    s = x.shape[0]
    widths = [w.shape[1] for w in ws]

    def body(x_ref, mod_ref, nw_ref, cos_ref, sin_ref, wqkv, wz, wba, waq, wak, wav, waz,
             h_ref, oqkv, oz, oba, oq, ok, ov, oaz):
        xt = x_ref[...]
        r = lax.rsqrt(jnp.mean(xt * xt, axis=-1, keepdims=True) + EPS)
        shift, scale = mod_ref[:, 0:D_MODEL], mod_ref[:, D_MODEL:2 * D_MODEL]
        h = ((xt * r) * nw_ref[...]) * (1.0 + scale) + shift
        hb = _bf(h)
        h_ref[...] = hb
        oqkv[...] = jnp.dot(hb, wqkv[...], preferred_element_type=F32)
        oz[...] = jnp.dot(hb, wz[...], preferred_element_type=F32)
        oba[...] = jnp.dot(hb, wba[...], preferred_element_type=F32)
        oaz[...] = jnp.dot(hb, waz[...], preferred_element_type=F32)
        tv = jnp.dot(hb, wav[...], preferred_element_type=F32)
        for j in range(AT_PAIRS):
            ov[j] = tv[:, j * LANES:(j + 1) * LANES]
        cs, sn = cos_ref[...], sin_ref[...]
        for w_ref, o_ref in ((waq, oq), (wak, ok)):
            t = jnp.dot(hb, w_ref[...], preferred_element_type=F32)
            for j in range(AT_PAIRS):
                tj = t[:, j * LANES:(j + 1) * LANES]
                o_ref[j] = tj * cs + _swap_half64(tj) * sn

    tok = lambda w: pl.BlockSpec((ts, w), lambda i: (i, 0))
    full = lambda a: pl.BlockSpec(a.shape, lambda i: (0, 0))
    pairs = pl.BlockSpec((AT_PAIRS, ts, LANES), lambda i: (0, i, 0))
    return pl.pallas_call(
        body, name="ln_proj", grid=(s // ts,), compiler_params=_params("arbitrary"),
        in_specs=[tok(D_MODEL), full(mod), full(norm_w), tok(LANES), tok(LANES)] + [full(w) for w in ws],
        out_specs=(tok(D_MODEL), tok(widths[0]), tok(widths[1]), tok(widths[2]), pairs, pairs, pairs,
                   tok(widths[6])),
        out_shape=(SDS((s, D_MODEL), BF16), SDS((s, widths[0]), F32), SDS((s, widths[1]), F32),
                   SDS((s, widths[2]), F32)) + (SDS((AT_PAIRS, s, LANES), F32),) * 3 + (SDS((s, widths[6]), F32),),
    )(x, mod, norm_w, cos_t, sin_t, *ws)


def _conv_taps(ext, rows):
    taps = []
    for j in range(CONV_K):
        sh = CONV_K - 1 - j
        rolled = pltpu.roll(ext, sh, 0) if sh else ext
        taps.append(rolled[8:8 + rows])
    return taps


def _dn_prep(qkv_pre, ba, conv_w8, alog_row, dtb_row, ts):
    s = qkv_pre.shape[0]
    cw = 3 * DN_WIDTH

    def body(pre_ref, halo_ref, ba_ref, cw_ref, al_ref, dtb_ref, q_ref, k_ref, v_ref, bg_ref):
        n = pl.program_id(0)
        prev = jnp.where(n == 0, 0.0, halo_ref[...])
        ext = jnp.concatenate([prev, pre_ref[...]], axis=0)
        taps = _conv_taps(ext, ts)
        conv = taps[0] * cw_ref[0:1, :]
        for j in range(1, CONV_K):
            conv = conv + taps[j] * cw_ref[j:j + 1, :]
        for h in range(DN_HEADS):
            cols = slice(h * DN_DIM, (h + 1) * DN_DIM)
            q_ref[:, cols] = _post_q(conv[:, h * DN_DIM:(h + 1) * DN_DIM])
            k_ref[:, cols] = _post_k(conv[:, DN_WIDTH + h * DN_DIM:DN_WIDTH + (h + 1) * DN_DIM])
            v_ref[:, cols] = _post_v(conv[:, 2 * DN_WIDTH + h * DN_DIM:2 * DN_WIDTH + (h + 1) * DN_DIM])
        bg = _beta_decay(ba_ref[...], al_ref[...], dtb_ref[...])
        lane = lax.broadcasted_iota(jnp.int32, bg.shape, 1)
        run = pltpu.roll(_chunk_cumsum(bg), DN_HEADS, 1)
        bg_ref[...] = jnp.where((lane >= GC_LANE) & (lane < GC_LANE + DN_HEADS), run, bg)

    tok = lambda w: pl.BlockSpec((ts, w), lambda i: (i, 0))
    full = lambda a: pl.BlockSpec(a.shape, lambda i: (0, 0))
    halo = pl.BlockSpec((8, cw), lambda i: (jnp.maximum(i * (ts // 8) - 1, 0), 0))
    return pl.pallas_call(
        body, name="dn_prep", grid=(s // ts,), compiler_params=_params("arbitrary"),
        in_specs=[tok(cw), halo, tok(BA_PAD), full(conv_w8), full(alog_row), full(dtb_row)],
        out_specs=(tok(DN_WIDTH), tok(DN_WIDTH), tok(DN_WIDTH), tok(BA_PAD)),
        out_shape=(SDS((s, DN_WIDTH), F32),) * 3 + (SDS((s, BA_PAD), F32),),
    )(qkv_pre, qkv_pre, ba, conv_w8, alog_row, dtb_row)


def _dn_chunk_prep(q, k, v, bg, ts):
    s = q.shape[0]
    ncs = ts // CHUNK

    def body(q_ref, k_ref, v_ref, bg_ref, u_ref, w_ref, qd_ref, kd_ref, p_ref, gl_ref, t_ref):
        def chunks(cg, carry):
            where = []
            for ci in (cg * CH_UNROLL + i for i in range(CH_UNROLL)):
                rows = pl.ds(pl.multiple_of(ci * CHUNK, CHUNK), CHUNK)
                rows8 = pl.ds(pl.multiple_of(ci * 8, 8), 8)
                where += [(rows, rows8, h, slice(h * DN_DIM, (h + 1) * DN_DIM)) for h in range(DN_HEADS)]
            bgs = [bg_ref[rows, :] for rows, _, _, _ in where]
            outs = _chunk_fwd([q_ref[rows, c] for rows, _, _, c in where], [k_ref[rows, c] for rows, _, _, c in where],
                              [v_ref[rows, c] for rows, _, _, c in where],
                              [b[:, h:h + 1] for b, (_, _, h, _) in zip(bgs, where)],
                              [b[:, GC_LANE + h:GC_LANE + h + 1] for b, (_, _, h, _) in zip(bgs, where)])
            for (rows, rows8, h, c), (u, w, p, qd, kd, gl, t) in zip(where, outs):
                u_ref[rows, c] = u
                w_ref[rows, c] = w
                qd_ref[rows, c] = qd
                kd_ref[rows, c] = kd
                p_ref[h, rows, :] = p
                t_ref[h, rows, :] = t
                gl_ref[rows8, c] = jnp.broadcast_to(gl, (8, DN_DIM))
            return carry

        lax.fori_loop(0, ncs // CH_UNROLL, chunks, 0)

    tok = lambda w: pl.BlockSpec((ts, w), lambda i: (i, 0))
    sq = pl.BlockSpec((DN_HEADS, ts, CHUNK), lambda i: (0, i, 0))
    return pl.pallas_call(
        body, name="dn_chunk_prep", grid=(s // ts,), compiler_params=_params("arbitrary"),
        in_specs=[tok(DN_WIDTH)] * 3 + [tok(BA_PAD)],
        out_specs=(tok(DN_WIDTH),) * 4 + (sq, pl.BlockSpec((ncs * 8, DN_WIDTH), lambda i: (i, 0)), sq),
        out_shape=(SDS((s, DN_WIDTH), F32),) * 4 + (SDS((DN_HEADS, s, CHUNK), F32),
                                                     SDS((s // CHUNK * 8, DN_WIDTH), F32),
                                                     SDS((DN_HEADS, s, CHUNK), F32)),
    )(q, k, v, bg)


def _dn_scan(u, w, qd, kd, p, gl, ts):
    s = u.shape[0]
    ncs = ts // CHUNK

    def body(u_ref, w_ref, qd_ref, kd_ref, p_ref, gl_ref, o_ref, vn_ref, st_ref, state):
        @pl.when(pl.program_id(0) == 0)
        def _():
            state[...] = jnp.zeros_like(state)

        def chunk(ci, carry):
            rows = pl.ds(pl.multiple_of(ci * CHUNK, CHUNK), CHUNK)
            rows8 = pl.ds(pl.multiple_of(ci * 8, 8), 8)
            srows = pl.ds(pl.multiple_of(ci * DN_DIM, DN_DIM), DN_DIM)
            hs = range(DN_HEADS)
            sl = [slice(h * DN_DIM, (h + 1) * DN_DIM) for h in hs]
            sf = [state[h] for h in hs]
            sb = [_bf(x) for x in sf]
            ws = [_nn(w_ref[rows, c], b) for c, b in zip(sl, sb)]
            qs = [_nn(qd_ref[rows, c], b) for c, b in zip(sl, sb)]
            vn = [u_ref[rows, c] - x for c, x in zip(sl, ws)]
            vb = [_bf(x) for x in vn]
            kv = [_tn(kd_ref[rows, c], b) for c, b in zip(sl, vb)]
            pv = [_nn(p_ref[h, rows, :], b) for h, b in zip(hs, vb)]
            for h in hs:
                state[h] = sf[h] * gl_ref[rows8, sl[h]][0:1] + kv[h]
            for h in hs:
                st_ref[srows, sl[h]] = sf[h]
                vn_ref[rows, sl[h]] = vn[h]
                o_ref[rows, sl[h]] = qs[h] + pv[h]
            return carry

        lax.fori_loop(0, ncs, chunk, 0)

    tok = lambda wd: pl.BlockSpec((ts, wd), lambda i: (i, 0))
    return pl.pallas_call(
        body, name="dn_scan", grid=(s // ts,), compiler_params=_params("arbitrary"),
        in_specs=[tok(DN_WIDTH)] * 4 + [pl.BlockSpec((DN_HEADS, ts, CHUNK), lambda i: (0, i, 0)),
                                        pl.BlockSpec((ncs * 8, DN_WIDTH), lambda i: (i, 0))],
        out_specs=(tok(DN_WIDTH), tok(DN_WIDTH), pl.BlockSpec((ncs * DN_DIM, DN_WIDTH), lambda i: (i, 0))),
        out_shape=(SDS((s, DN_WIDTH), F32), SDS((s, DN_WIDTH), F32), SDS((s // CHUNK * DN_DIM, DN_WIDTH), F32)),
        scratch_shapes=[pltpu.VMEM((DN_HEADS, DN_DIM, DN_DIM), F32)],
    )(u, w, qd, kd, p, gl)


LOG2E, LN2 = 1.4426950408889634, 0.6931471805599453
MASKED = -1e30


def _band_bias():
    qi = lax.broadcasted_iota(jnp.int32, (Q_BLOCK, 2 * Q_BLOCK), 0)
    kj = lax.broadcasted_iota(jnp.int32, (Q_BLOCK, 2 * Q_BLOCK), 1)
    rel = Q_BLOCK + qi - kj
    return jnp.where((rel >= 0) & (rel <= W_SUB), 0.0, MASKED)


def _first_bias(first):
    kj = lax.broadcasted_iota(jnp.int32, (1, 2 * Q_BLOCK), 1)
    return jnp.where((kj < Q_BLOCK) & first, MASKED, 0.0)


def _attn_combo(c, d):
    if d == 1:
        qs = pl.multiple_of(c * Q_BLOCK, Q_BLOCK)
        return qs, pl.multiple_of(ATT_BLK - Q_BLOCK + c * Q_BLOCK, Q_BLOCK), c == 0
    r, m = c % d, c // d
    qs = r + (d * Q_BLOCK) * m
    return qs, ATT_BLK + qs - d * Q_BLOCK, m == 0


def _rows(start, size, d):
    return pl.ds(pl.multiple_of(start, Q_BLOCK), size) if d == 1 else pl.ds(start, size, stride=d)


def _shift_in(ext, cur, n):
    @pl.when(n == 0)
    def _():
        ext[0:ATT_BLK, :] = jnp.zeros((ATT_BLK, LANES), F32)

    @pl.when(n > 0)
    def _():
        ext[0:ATT_BLK, :] = ext[ATT_BLK:2 * ATT_BLK, :]

    ext[ATT_BLK:2 * ATT_BLK, :] = cur


def _attn_fwd(qr, kr, vv):
    s = qr.shape[1]
    nblk = s // ATT_BLK
    scale = AT_DIM ** -0.5
    npat = len(DILATIONS)

    def body(q_ref, k_ref, v_ref, o_ref, lse_ref, kext, vext, o_p, l_p, bias_ref):
        n = pl.program_id(1)
        _shift_in(kext, k_ref[0], n)
        _shift_in(vext, v_ref[0], n)
        bias_ref[...] = _band_bias()
        lo = lax.broadcasted_iota(jnp.int32, (Q_BLOCK, LANES), 1) < AT_DIM
        for pi, d in enumerate(DILATIONS):
            def group(g, carry, pi=pi, d=d):
                cs = [_attn_combo(g * ATT_UNROLL + u, d) for u in range(ATT_UNROLL)]
                heads = [(i, sel) for i in range(ATT_UNROLL) for sel in (lo, ~lo)]
                band = bias_ref[...]
                bias = [band + _first_bias((n == 0) & m0) for _, _, m0 in cs]
                qb = [_bf(q_ref[0, _rows(qs, Q_BLOCK, d), :]) for qs, _, _ in cs]
                kk = [_bf(kext[_rows(ks, 2 * Q_BLOCK, d), :]) for _, ks, _ in cs]
                vb = [_bf(vext[_rows(ks, 2 * Q_BLOCK, d), :]) for _, ks, _ in cs]
                sc = [lax.dot_general(jnp.where(sel, qb[i], jnp.zeros_like(qb[i])), kk[i], _NT,
                                      preferred_element_type=F32) for i, sel in heads]
                sc = [x * (scale * LOG2E) + bias[i] for x, (i, _) in zip(sc, heads)]
                mx = [jnp.max(x, axis=-1, keepdims=True) for x in sc]
                pr = [jnp.exp2(x - m) for x, m in zip(sc, mx)]
                ls = [jnp.sum(x, axis=-1, keepdims=True) for x in pr]
                pv = [jnp.dot(_bf(x), vb[i], preferred_element_type=F32) for x, (i, _) in zip(pr, heads)]
                outs = [x / l for x, l in zip(pv, ls)]
                lses = [m * LN2 + jnp.log(l) for m, l in zip(mx, ls)]
                for i, (qs, _, _) in enumerate(cs):
                    o_p[pi, _rows(qs, Q_BLOCK, d), :] = jnp.where(lo, outs[2 * i], outs[2 * i + 1])
                    l_p[pi, _rows(qs, Q_BLOCK, d), :] = jnp.where(lo, lses[2 * i], lses[2 * i + 1])
                return carry

            lax.fori_loop(0, ATT_BLK // Q_BLOCK // ATT_UNROLL, group, 0)

        def merge(i, carry):
            rows = pl.ds(pl.multiple_of(i * 256, 256), 256)
            ls = [l_p[pi, rows, :] for pi in range(npat)]
            mx = jnp.maximum(jnp.maximum(ls[0], ls[1]), ls[2])
            es = [jnp.exp(l - mx) for l in ls]
            den = es[0] + es[1] + es[2]
            o_ref[0, rows, :] = (es[0] * o_p[0, rows, :] + es[1] * o_p[1, rows, :] + es[2] * o_p[2, rows, :]) / den
            lse_ref[0, rows, :] = mx + jnp.log(den)
            return carry

        lax.fori_loop(0, ATT_BLK // 256, merge, 0)

    blk = pl.BlockSpec((1, ATT_BLK, LANES), lambda j, n: (j, n, 0))
    return pl.pallas_call(
        body, name="attn_fwd", grid=(AT_PAIRS, nblk), compiler_params=_params("arbitrary", "arbitrary"),
        in_specs=[blk] * 3, out_specs=(blk, blk),
        out_shape=(SDS((AT_PAIRS, s, LANES), F32),) * 2,
        scratch_shapes=[pltpu.VMEM((2 * ATT_BLK, LANES), F32), pltpu.VMEM((2 * ATT_BLK, LANES), F32),
                        pltpu.VMEM((npat, ATT_BLK, LANES), F32), pltpu.VMEM((npat, ATT_BLK, LANES), F32),
                        pltpu.VMEM((Q_BLOCK, 2 * Q_BLOCK), F32)],
    )(qr, kr, vv)


def _mix_prep(o_dn, z_dn, o_at, z_at, dnw, atw2, ts):
    s = o_dn.shape[0]

    def body(odn, zdn, oat, zat, dnw_ref, atw_ref, cat_ref):
        for h in range(DN_HEADS):
            cols = slice(h * DN_DIM, (h + 1) * DN_DIM)
            cat_ref[:, cols] = _bf(_gate_dn(odn[:, cols], zdn[:, cols], dnw_ref[...]))
        for j in range(AT_PAIRS):
            cat_ref[:, DN_WIDTH + j * LANES:DN_WIDTH + (j + 1) * LANES] = _bf(
                _gate_at(oat[j], zat[:, j * LANES:(j + 1) * LANES], atw_ref[...], _hnn))

    tok = lambda w: pl.BlockSpec((ts, w), lambda i: (i, 0))
    full = lambda a: pl.BlockSpec(a.shape, lambda i: (0, 0))
    pairs = pl.BlockSpec((AT_PAIRS, ts, LANES), lambda i: (0, i, 0))
    return pl.pallas_call(
        body, name="mix_prep", grid=(s // ts,), compiler_params=_params("arbitrary"),
        in_specs=[tok(DN_WIDTH), tok(DN_WIDTH), pairs, tok(AT_WIDTH), full(dnw), full(atw2)],
        out_specs=tok(D_MODEL), out_shape=SDS((s, D_MODEL), BF16),
    )(o_dn, z_dn, o_at, z_at, dnw, atw2)


def _out_loss(cat, x, tgt, w_out, gate, fw, ts):
    s = x.shape[0]

    def body(cat_ref, x_ref, t_ref, w_ref, g_ref, fw_ref, dx2_ref, dcat_ref, gw_ref, dfw_ref, dgate_ref, loss_ref):
        @pl.when(pl.program_id(0) == 0)
        def _():
            gw_ref[...] = jnp.zeros_like(gw_ref)
            dfw_ref[...] = jnp.zeros_like(dfw_ref)
            dgate_ref[...] = jnp.zeros_like(dgate_ref)
            loss_ref[...] = jnp.zeros_like(loss_ref)

        catb = cat_ref[...]
        wb = w_ref[...]
        gate, fwv = g_ref[...], fw_ref[...]
        mix = jnp.dot(catb, wb, preferred_element_type=F32)
        x2 = x_ref[...] + gate * mix
        r2 = lax.rsqrt(jnp.mean(x2 * x2, axis=-1, keepdims=True) + EPS)
        xn2 = x2 * r2
        err = xn2 * fwv - t_ref[...]
        row = jnp.sum(err * err, axis=-1, keepdims=True) * (1.0 / D_MODEL)
        loss_ref[...] += 0.5 * jnp.sum(row, axis=0, keepdims=True)
        dy = err * (1.0 / D_MODEL)
        dfw_ref[...] += jnp.sum(dy * xn2, axis=0, keepdims=True)
        dxn = dy * fwv
        dx2 = r2 * (dxn - xn2 * jnp.mean(dxn * xn2, axis=-1, keepdims=True))
        dx2_ref[...] = dx2
        dgate_ref[...] += jnp.sum(dx2 * mix, axis=0, keepdims=True)
        dmix = _bf(gate * dx2)
        dcat_ref[...] = lax.dot_general(dmix, wb, _NT, preferred_element_type=F32)
        gw_ref[...] += lax.dot_general(catb, dmix, _TN, preferred_element_type=F32)

    tok = lambda w: pl.BlockSpec((ts, w), lambda i: (i, 0))
    full = lambda a: pl.BlockSpec(a.shape, lambda i: (0, 0))
    row = pl.BlockSpec((1, D_MODEL), lambda i: (0, 0))
    return pl.pallas_call(
        body, name="out_loss", grid=(s // ts,), compiler_params=_params("arbitrary"),
        in_specs=[tok(D_MODEL), tok(D_MODEL), tok(D_MODEL), full(w_out), full(gate), full(fw)],
        out_specs=(tok(D_MODEL), tok(D_MODEL), pl.BlockSpec((D_MODEL, D_MODEL), lambda i: (0, 0)), row, row,
                   pl.BlockSpec((1, 1), lambda i: (0, 0))),
        out_shape=(SDS((s, D_MODEL), F32), SDS((s, D_MODEL), F32), SDS((D_MODEL, D_MODEL), F32),
                   SDS((1, D_MODEL), F32), SDS((1, D_MODEL), F32), SDS((1, 1), F32)),
    )(cat, x, tgt, w_out, gate, fw)


def _mix_bwd(dcat, o_dn, z_dn, o_at, z_at, dnw, atw2, ts):
    s = dcat.shape[0]

    def body(dcat_ref, odn, zdn, oat, zat, dnw_ref, atw_ref, dodn, dzdn, doat, dzat, delta, ddnw, datw):
        @pl.when(pl.program_id(0) == 0)
        def _():
            ddnw[...] = jnp.zeros_like(ddnw)
            datw[...] = jnp.zeros_like(datw)

        for h in range(DN_HEADS):
            cols = slice(h * DN_DIM, (h + 1) * DN_DIM)
            _, vjp = jax.vjp(_gate_dn, odn[:, cols], zdn[:, cols], dnw_ref[...])
            do, dz, dw = vjp(dcat_ref[:, cols])
            dodn[:, cols] = do
            dzdn[:, cols] = _bf(dz)
            ddnw[...] += dw
        for j in range(AT_PAIRS):
            cols = slice(j * LANES, (j + 1) * LANES)
            o = oat[j]
            _, vjp = jax.vjp(functools.partial(_gate_at, hnn=_d_hnn), o, zat[:, cols], atw_ref[...])
            do, dz, dw = vjp(dcat_ref[:, DN_WIDTH + j * LANES:DN_WIDTH + (j + 1) * LANES])
            doat[j] = do
            dzat[:, cols] = _bf(dz)
            datw[...] += dw
            delta[j] = _hnn(do * o, _group_ones(1.0))

    tok = lambda w: pl.BlockSpec((ts, w), lambda i: (i, 0))
    full = lambda a: pl.BlockSpec(a.shape, lambda i: (0, 0))
    row = pl.BlockSpec((1, LANES), lambda i: (0, 0))
    pairs = pl.BlockSpec((AT_PAIRS, ts, LANES), lambda i: (0, i, 0))
    return pl.pallas_call(
        body, name="mix_bwd", grid=(s // ts,), compiler_params=_params("arbitrary"),
        in_specs=[tok(D_MODEL), tok(DN_WIDTH), tok(DN_WIDTH), pairs, tok(AT_WIDTH), full(dnw), full(atw2)],
        out_specs=(tok(DN_WIDTH), tok(DN_WIDTH), pairs, tok(AT_WIDTH), pairs, row, row),
        out_shape=(SDS((s, DN_WIDTH), F32), SDS((s, DN_WIDTH), BF16), SDS((AT_PAIRS, s, LANES), F32),
                   SDS((s, AT_WIDTH), BF16), SDS((AT_PAIRS, s, LANES), F32), SDS((1, LANES), F32),
                   SDS((1, LANES), F32)),
    )(dcat, o_dn, z_dn, o_at, z_at, dnw, atw2)


def _shift_acc(ext, n):
    @pl.when(n == 0)
    def _():
        ext[0:ATT_BLK, :] = jnp.zeros((ATT_BLK, LANES), F32)

    @pl.when(n > 0)
    def _():
        ext[0:ATT_BLK, :] = ext[ATT_BLK:2 * ATT_BLK, :]

    ext[ATT_BLK:2 * ATT_BLK, :] = jnp.zeros((ATT_BLK, LANES), F32)


def _attn_bwd(qr, kr, vv, do, lse, delta):
    s = qr.shape[1]
    nblk = s // ATT_BLK
    scale = AT_DIM ** -0.5

    def body(q_ref, k_ref, v_ref, do_ref, lse_ref, dl_ref, dq_ref, dk_ref, dv_ref, kext, vext, dkext, dvext,
             bias_ref):
        n = pl.program_id(1)
        _shift_in(kext, k_ref[0], n)
        _shift_in(vext, v_ref[0], n)
        _shift_acc(dkext, n)
        _shift_acc(dvext, n)
        bias_ref[...] = _band_bias()

        @pl.when(n < nblk)
        def _():
            dq_ref[0] = jnp.zeros((ATT_BLK, LANES), F32)
            lo = lax.broadcasted_iota(jnp.int32, (Q_BLOCK, LANES), 1) < AT_DIM
            for d in DILATIONS:
                def group(g, carry, d=d):
                    nu = ATT_UNROLL_BWD
                    cs = [_attn_combo(g * nu + u, d) for u in range(nu)]
                    heads = [(i, sel) for i in range(nu) for sel in (lo, ~lo)]
                    qrows = [_rows(qs, Q_BLOCK, d) for qs, _, _ in cs]
                    krows = [_rows(ks, 2 * Q_BLOCK, d) for _, ks, _ in cs]
                    band = bias_ref[...]
                    bias = [band + _first_bias((n == 0) & m0) for _, _, m0 in cs]
                    qb = [_bf(q_ref[0, r, :]) for r in qrows]
                    dob = [_bf(do_ref[0, r, :]) for r in qrows]
                    kk = [_bf(kext[r, :]) for r in krows]
                    vb = [_bf(vext[r, :]) for r in krows]
                    lse2 = [lse_ref[0, r, :] * LOG2E for r in qrows]
                    dl2 = [dl_ref[0, r, :] for r in qrows]
                    qm = [jnp.where(sel, qb[i], jnp.zeros_like(qb[i])) for i, sel in heads]
                    dom = [jnp.where(sel, dob[i], jnp.zeros_like(dob[i])) for i, sel in heads]
                    lse_c = [jnp.max(jnp.where(sel, lse2[i], -jnp.inf), axis=-1, keepdims=True) for i, sel in heads]
                    dl_c = [jnp.max(jnp.where(sel, dl2[i], -jnp.inf), axis=-1, keepdims=True) for i, sel in heads]
                    sc = [lax.dot_general(a, kk[i], _NT, preferred_element_type=F32) for a, (i, _) in zip(qm, heads)]
                    dp = [lax.dot_general(a, vb[i], _NT, preferred_element_type=F32) for a, (i, _) in zip(dom, heads)]
                    pr = [jnp.exp2(x * (scale * LOG2E) + bias[i] - l) for x, l, (i, _) in zip(sc, lse_c, heads)]
                    ds = [_bf(p * (x - dl) * scale) for p, x, dl in zip(pr, dp, dl_c)]
                    prb = [_bf(p) for p in pr]
                    dq = [jnp.dot(x, kk[i], preferred_element_type=F32) for x, (i, _) in zip(ds, heads)]
                    dk = [lax.dot_general(x, a, _TN, preferred_element_type=F32) for x, a in zip(ds, qm)]
                    dv = [lax.dot_general(x, a, _TN, preferred_element_type=F32) for x, a in zip(prb, dom)]
                    for i in range(nu):
                        dq_ref[0, qrows[i], :] += jnp.where(lo, dq[2 * i], dq[2 * i + 1])
                        dkext[krows[i], :] += dk[2 * i] + dk[2 * i + 1]
                        dvext[krows[i], :] += dv[2 * i] + dv[2 * i + 1]
                    return carry

                lax.fori_loop(0, ATT_BLK // Q_BLOCK // ATT_UNROLL_BWD, group, 0)

        dk_ref[0] = dkext[0:ATT_BLK, :]
        dv_ref[0] = dvext[0:ATT_BLK, :]

    cur = pl.BlockSpec((1, ATT_BLK, LANES), lambda j, n: (j, jnp.minimum(n, nblk - 1), 0))
    done = pl.BlockSpec((1, ATT_BLK, LANES), lambda j, n: (j, jnp.maximum(n - 1, 0), 0))
    return pl.pallas_call(
        body, name="attn_bwd", grid=(AT_PAIRS, nblk + 1), compiler_params=_params("arbitrary", "arbitrary"),
        in_specs=[cur] * 6, out_specs=(cur, done, done),
        out_shape=(SDS((AT_PAIRS, s, LANES), F32),) * 3,
        scratch_shapes=[pltpu.VMEM((2 * ATT_BLK, LANES), F32)] * 4 + [pltpu.VMEM((Q_BLOCK, 2 * Q_BLOCK), F32)],
    )(qr, kr, vv, do, lse, delta)


def _rope_bwd(dq, dk, dv, cos_t, sin_t, ts):
    s = cos_t.shape[0]

    def body(q_ref, k_ref, v_ref, cos_ref, sin_ref, oq, ok, ov):
        cs, sn = cos_ref[...], sin_ref[...]
        for j in range(AT_PAIRS):
            cols = slice(j * LANES, (j + 1) * LANES)
            for g_ref, o_ref in ((q_ref, oq), (k_ref, ok)):
                g = g_ref[j]
                o_ref[:, cols] = _bf(g * cs + _swap_half64(g * sn))
            ov[:, cols] = _bf(v_ref[j])

    tok = lambda w: pl.BlockSpec((ts, w), lambda i: (i, 0))
    pairs = pl.BlockSpec((AT_PAIRS, ts, LANES), lambda i: (0, i, 0))
    return pl.pallas_call(
        body, name="rope_bwd", grid=(s // ts,), compiler_params=_params("arbitrary"),
        in_specs=[pairs] * 3 + [tok(LANES)] * 2, out_specs=(tok(AT_WIDTH),) * 3,
        out_shape=(SDS((s, AT_WIDTH), BF16),) * 3,
    )(dq, dk, dv, cos_t, sin_t)


def _dn_scan_bwd(do, st, vn, w, qd, kd, p, gl, ts):
    s = do.shape[0]
    ncs = ts // CHUNK
    nt = s // ts

    def body(do_ref, st_ref, vn_ref, w_ref, qd_ref, kd_ref, p_ref, gl_ref,
             du_ref, dw_ref, dqd_ref, dkd_ref, dp_ref, dgl_ref, dstate):
        @pl.when(pl.program_id(0) == 0)
        def _():
            dstate[...] = jnp.zeros_like(dstate)

        def chunk(jr, carry):
            ci = ncs - 1 - jr
            rows = pl.ds(pl.multiple_of(ci * CHUNK, CHUNK), CHUNK)
            rows8 = pl.ds(pl.multiple_of(ci * 8, 8), 8)
            srows = pl.ds(pl.multiple_of(ci * DN_DIM, DN_DIM), DN_DIM)
            hs = range(DN_HEADS)
            sl = [slice(h * DN_DIM, (h + 1) * DN_DIM) for h in hs]
            ds_ = [dstate[h] for h in hs]
            dsb = [_bf(x) for x in ds_]
            dob = [_bf(do_ref[rows, c]) for c in sl]
            pdo = [_tn(p_ref[h, rows, :], b) for h, b in zip(hs, dob)]
            qdo = [_tn(qd_ref[rows, c], b) for c, b in zip(sl, dob)]
            dvn = [_nn(kd_ref[rows, c], b) + x for c, b, x in zip(sl, dsb, pdo)]
            dvb = [_bf(x) for x in dvn]
            wdv = [_tn(w_ref[rows, c], b) for c, b in zip(sl, dvb)]
            for h in hs:
                dstate[h] = ds_[h] * gl_ref[rows8, sl[h]][0:1] + qdo[h] - wdv[h]
            sfs = [st_ref[srows, c] for c in sl]
            sbs = [_bf(x) for x in sfs]
            vnb = [_bf(vn_ref[rows, c]) for c in sl]
            for h in hs:
                du_ref[rows, sl[h]] = dvn[h]
                dw_ref[rows, sl[h]] = -_nt(dvb[h], sbs[h])
                dqd_ref[rows, sl[h]] = _nt(dob[h], sbs[h])
                dkd_ref[rows, sl[h]] = _nt(vnb[h], dsb[h])
                dp_ref[h, rows, :] = _nt(dob[h], vnb[h])
                dgl = jnp.sum(jnp.sum(ds_[h] * sfs[h], axis=1, keepdims=True), axis=0, keepdims=True)
                dgl_ref[rows8, sl[h]] = jnp.broadcast_to(dgl, (8, DN_DIM))
            return carry

        lax.fori_loop(0, ncs, chunk, 0)

    tok = lambda wd: pl.BlockSpec((ts, wd), lambda i: (nt - 1 - i, 0))
    pspec = pl.BlockSpec((DN_HEADS, ts, CHUNK), lambda i: (0, nt - 1 - i, 0))
    g8 = pl.BlockSpec((ncs * 8, DN_WIDTH), lambda i: (nt - 1 - i, 0))
    return pl.pallas_call(
        body, name="dn_scan_bwd", grid=(nt,), compiler_params=_params("arbitrary"),
        in_specs=[tok(DN_WIDTH), pl.BlockSpec((ncs * DN_DIM, DN_WIDTH), lambda i: (nt - 1 - i, 0))]
        + [tok(DN_WIDTH)] * 4 + [pspec, g8],
        out_specs=(tok(DN_WIDTH),) * 4 + (pspec, g8),
        out_shape=(SDS((s, DN_WIDTH), F32),) * 4 + (SDS((DN_HEADS, s, CHUNK), F32),
                                                     SDS((s // CHUNK * 8, DN_WIDTH), F32)),
        scratch_shapes=[pltpu.VMEM((DN_HEADS, DN_DIM, DN_DIM), F32)],
    )(do, st, vn, w, qd, kd, p, gl)


def _dn_chunk_bwd(q, k, v, bg, t, du, dw, dqd, dkd, dp, dgl, ts):
    s = q.shape[0]
    ncs = ts // CHUNK

    def body(q_ref, k_ref, v_ref, bg_ref, t_ref, du_ref, dw_ref, dqd_ref, dkd_ref, dp_ref, dgl_ref,
             dq_ref, dk_ref, dv_ref, dbg_ref):
        def chunks(cg, carry):
            lane = lax.broadcasted_iota(jnp.int32, (CHUNK, BA_PAD), 1)
            where = []
            for ci in (cg * CH_UNROLL + i for i in range(CH_UNROLL)):
                rows = pl.ds(pl.multiple_of(ci * CHUNK, CHUNK), CHUNK)
                rows8 = pl.ds(pl.multiple_of(ci * 8, 8), 8)
                where += [(rows, rows8, h, slice(h * DN_DIM, (h + 1) * DN_DIM)) for h in range(DN_HEADS)]
            bgs = [bg_ref[rows, :] for rows, _, _, _ in where]
            cots = [(du_ref[rows, c], dw_ref[rows, c], dp_ref[h, rows, :], dqd_ref[rows, c], dkd_ref[rows, c],
                     dgl_ref[rows8, c][0:1, 0:1]) for rows, rows8, h, c in where]
            outs = _chunk_bwd([q_ref[rows, c] for rows, _, _, c in where], [k_ref[rows, c] for rows, _, _, c in where],
                              [v_ref[rows, c] for rows, _, _, c in where],
                              [b[:, h:h + 1] for b, (_, _, h, _) in zip(bgs, where)],
                              [b[:, GC_LANE + h:GC_LANE + h + 1] for b, (_, _, h, _) in zip(bgs, where)],
                              [t_ref[h, rows, :] for rows, _, h, _ in where], cots)
            for i in range(CH_UNROLL):
                dbg = jnp.zeros((CHUNK, BA_PAD), F32)
                for (rows, _, h, c), (dq, dk, dv, dbeta, dgc) in list(zip(where, outs))[i * DN_HEADS:(i + 1) * DN_HEADS]:
                    dq_ref[rows, c] = dq
                    dk_ref[rows, c] = dk
                    dv_ref[rows, c] = dv
                    dbg = dbg + jnp.where(lane == h, dbeta, 0.0) + jnp.where(lane == GC_LANE + h, dgc, 0.0)
                dbg_ref[where[i * DN_HEADS][0], :] = dbg
            return carry

        lax.fori_loop(0, ncs // CH_UNROLL, chunks, 0)

    tok = lambda wd: pl.BlockSpec((ts, wd), lambda i: (i, 0))
    pspec = pl.BlockSpec((DN_HEADS, ts, CHUNK), lambda i: (0, i, 0))
    g8 = pl.BlockSpec((ncs * 8, DN_WIDTH), lambda i: (i, 0))
    return pl.pallas_call(
        body, name="dn_chunk_bwd", grid=(s // ts,), compiler_params=_params("arbitrary"),
        in_specs=[tok(DN_WIDTH)] * 3 + [tok(BA_PAD), pspec] + [tok(DN_WIDTH)] * 4 + [pspec, g8],
        out_specs=(tok(DN_WIDTH),) * 3 + (tok(BA_PAD),),
        out_shape=(SDS((s, DN_WIDTH), F32),) * 3 + (SDS((s, BA_PAD), F32),),
    )(q, k, v, bg, t, du, dw, dqd, dkd, dp, dgl)


def _dn_prep_bwd(qkv_pre, ba, dq, dk, dv, dbg, conv_w8, alog_row, dtb_row, ts):
    s = qkv_pre.shape[0]
    cw = 3 * DN_WIDTH
    nt = s // ts

    def body(pre_ref, ph_ref, nh_ref, ba_ref, dq_ref, dqh_ref, dk_ref, dkh_ref, dv_ref, dvh_ref, dbg_ref,
             cw_ref, al_ref, dtb_ref, dpre_ref, dba_ref, dcw_ref, dal_ref, ddtb_ref):
        n = pl.program_id(0)

        @pl.when(n == 0)
        def _():
            dcw_ref[...] = jnp.zeros_like(dcw_ref)
            dal_ref[...] = jnp.zeros_like(dal_ref)
            ddtb_ref[...] = jnp.zeros_like(ddtb_ref)

        last = n == nt - 1
        prev = jnp.where(n == 0, 0.0, ph_ref[...])
        ext = jnp.concatenate([prev, pre_ref[...], nh_ref[...]], axis=0)
        taps = _conv_taps(ext, ts + 8)
        conv = taps[0] * cw_ref[0:1, :]
        for j in range(1, CONV_K):
            conv = conv + taps[j] * cw_ref[j:j + 1, :]

        def cot(main, halo, cols):
            return jnp.concatenate([main[:, cols], jnp.where(last, 0.0, halo[:, cols])], axis=0)

        pieces = []
        for grp, (fn, mref, href) in enumerate(((_post_q, dq_ref, dqh_ref), (_post_k, dk_ref, dkh_ref),
                                                (_post_v, dv_ref, dvh_ref))):
            for h in range(DN_HEADS):
                cols = slice(h * DN_DIM, (h + 1) * DN_DIM)
                c0 = grp * DN_WIDTH + h * DN_DIM
                _, vjp = jax.vjp(fn, conv[:, c0:c0 + DN_DIM])
                pieces.append(vjp(cot(mref, href, cols))[0])
        dconv = jnp.concatenate(pieces, axis=1)
        rows = ts + 8
        dpre = dconv[:ts] * cw_ref[CONV_K - 1:CONV_K, :]
        for j in range(CONV_K - 1):
            sh = CONV_K - 1 - j
            dpre = dpre + pltpu.roll(dconv, rows - sh, 0)[:ts] * cw_ref[j:j + 1, :]
        dpre_ref[...] = _bf(dpre)
        for j in range(CONV_K):
            dcw_ref[j:j + 1, :] += jnp.sum(dconv[:ts] * taps[j][:ts], axis=0, keepdims=True)

        dbg = dbg_ref[...]
        lane = lax.broadcasted_iota(jnp.int32, dbg.shape, 1)
        dg = pltpu.roll(_chunk_cumsum(dbg, reverse=True), BA_PAD - DN_HEADS, 1)
        cot_bg = jnp.where(lane < DN_HEADS, dbg, jnp.where(lane < GC_LANE, dg, 0.0))
        _, vjp = jax.vjp(_beta_decay, ba_ref[...], al_ref[...], dtb_ref[...])
        dba, dal, ddtb = vjp(cot_bg)
        dba_ref[...] = _bf(dba)
        dal_ref[...] += dal
        ddtb_ref[...] += ddtb

    tok = lambda w: pl.BlockSpec((ts, w), lambda i: (i, 0))
    full = lambda a: pl.BlockSpec(a.shape, lambda i: (0, 0))
    prevh = lambda w: pl.BlockSpec((8, w), lambda i: (jnp.maximum(i * (ts // 8) - 1, 0), 0))
    nexth = lambda w: pl.BlockSpec((8, w), lambda i: (jnp.minimum((i + 1) * (ts // 8), s // 8 - 1), 0))
    row = pl.BlockSpec((1, LANES), lambda i: (0, 0))
    return pl.pallas_call(
        body, name="dn_prep_bwd", grid=(nt,), compiler_params=_params("arbitrary"),
        in_specs=[tok(cw), prevh(cw), nexth(cw), tok(BA_PAD),
                  tok(DN_WIDTH), nexth(DN_WIDTH), tok(DN_WIDTH), nexth(DN_WIDTH), tok(DN_WIDTH), nexth(DN_WIDTH),
                  tok(BA_PAD), full(conv_w8), full(alog_row), full(dtb_row)],
        out_specs=(tok(cw), tok(BA_PAD), pl.BlockSpec((8, cw), lambda i: (0, 0)), row, row),
        out_shape=(SDS((s, cw), BF16), SDS((s, BA_PAD), BF16), SDS((8, cw), F32), SDS((1, LANES), F32),
                   SDS((1, LANES), F32)),
    )(qkv_pre, qkv_pre, qkv_pre, ba, dq, dq, dk, dk, dv, dv, dbg, conv_w8, alog_row, dtb_row)


def _dh_dx(dps, ws, x, mod, norm_w, dx2, ts):
    s = x.shape[0]
    widths = [w.shape[1] for w in ws]
    np_ = len(ws)

    def body(*refs):
        dp_refs, w_refs = refs[:np_], refs[np_:2 * np_]
        x_ref, mod_ref, nw_ref, dx2_ref, gx_ref, dshift, dscale, dnw = refs[2 * np_:]

        @pl.when(pl.program_id(0) == 0)
        def _():
            dshift[...] = jnp.zeros_like(dshift)
            dscale[...] = jnp.zeros_like(dscale)
            dnw[...] = jnp.zeros_like(dnw)

        dh = lax.dot_general(dp_refs[0][...], w_refs[0][...], _NT, preferred_element_type=F32)
        for a, b in zip(dp_refs[1:], w_refs[1:]):
            dh = dh + lax.dot_general(a[...], b[...], _NT, preferred_element_type=F32)
        xt = x_ref[...]
        r = lax.rsqrt(jnp.mean(xt * xt, axis=-1, keepdims=True) + EPS)
        xn = xt * r
        nw = nw_ref[...]
        sc1 = 1.0 + mod_ref[:, D_MODEL:2 * D_MODEL]
        dshift[...] += jnp.sum(dh, axis=0, keepdims=True)
        dscale[...] += jnp.sum(dh * (xn * nw), axis=0, keepdims=True)
        dnw[...] += jnp.sum(dh * sc1 * xn, axis=0, keepdims=True)
        dxn = dh * sc1 * nw
        gx_ref[...] = r * (dxn - xn * jnp.mean(dxn * xn, axis=-1, keepdims=True)) + dx2_ref[...]

    tok = lambda w: pl.BlockSpec((ts, w), lambda i: (i, 0))
    full = lambda a: pl.BlockSpec(a.shape, lambda i: (0, 0))
    row = pl.BlockSpec((1, D_MODEL), lambda i: (0, 0))
    return pl.pallas_call(
        body, name="dh_dx", grid=(s // ts,), compiler_params=_params("arbitrary"),
        in_specs=[tok(w) for w in widths] + [full(w) for w in ws] + [tok(D_MODEL), full(mod), full(norm_w),
                                                                    tok(D_MODEL)],
        out_specs=(tok(D_MODEL), row, row, row),
        out_shape=(SDS((s, D_MODEL), F32),) + (SDS((1, D_MODEL), F32),) * 3,
    )(*dps, *ws, x, mod, norm_w, dx2)


def _grad_w_in(h, dps, ts, name):
    s = h.shape[0]
    widths = [p.shape[1] for p in dps]
    np_ = len(dps)

    def body(*refs):
        h_ref, dp_refs, outs = refs[0], refs[1:1 + np_], refs[1 + np_:]

        @pl.when(pl.program_id(0) == 0)
        def _():
            for o in outs:
                o[...] = jnp.zeros_like(o)

        hb = h_ref[...]
        for p, o in zip(dp_refs, outs):
            o[...] += lax.dot_general(hb, p[...], _TN, preferred_element_type=F32)

    tok = lambda w: pl.BlockSpec((ts, w), lambda i: (i, 0))
    return pl.pallas_call(
        body, name=name, grid=(s // ts,), compiler_params=_params("arbitrary"),
        in_specs=[tok(D_MODEL)] + [tok(w) for w in widths],
        out_specs=tuple(pl.BlockSpec((D_MODEL, w), lambda i: (0, 0)) for w in widths),
        out_shape=tuple(SDS((D_MODEL, w), F32) for w in widths),
    )(h, *dps)


def _adamw_math(w, g, m, v):
    m = ADAM_B1 * m + (1.0 - ADAM_B1) * g
    v = ADAM_B2 * v + (1.0 - ADAM_B2) * (g * g)
    m_hat = m / (1.0 - ADAM_B1 ** ADAM_STEP)
    v_hat = v / (1.0 - ADAM_B2 ** ADAM_STEP)
    delta = -ADAM_LR * (m_hat / (jnp.sqrt(v_hat) + ADAM_EPS) + ADAM_WD * w)
    return delta, m, v


def _adamw(w, m, v, g, name, slots=False):
    def body(w_ref, m_ref, v_ref, g_ref, g_out, d_out, m_out, v_out):
        if slots:
            g = g_ref[0].astype(F32)
            for k in range(1, N_DEV):
                g = g + g_ref[k].astype(F32)
        else:
            g = g_ref[...]
        g_out[...] = g
        d_out[...], m_out[...], v_out[...] = _adamw_math(w_ref[...], g, m_ref[...], v_ref[...])

    return pl.pallas_call(body, name=name, compiler_params=_params(),
                          out_shape=(SDS(w.shape, F32),) * 4)(w, m, v, g)


def _adamw_w_mod(w, m, v, siluc_all, dmod_mine):
    def body(w_ref, m_ref, v_ref, sc_ref, dm_ref, g_out, d_out, m_out, v_out):
        g = _htn(sc_ref[...], dm_ref[...])
        g_out[...] = g
        d_out[...], m_out[...], v_out[...] = _adamw_math(w_ref[...], g, m_ref[...], v_ref[...])

    return pl.pallas_call(body, name="adamw_w_mod", compiler_params=_params(),
                          out_shape=(SDS(w.shape, F32),) * 4)(w, m, v, siluc_all, dmod_mine)


def _pack_sum(pack_all):
    def body(p_ref, o_ref):
        t = p_ref[0]
        for k in range(1, N_DEV):
            t = t + p_ref[k]
        o_ref[...] = t

    return pl.pallas_call(body, name="pack_sum", out_shape=SDS(pack_all.shape[1:], F32))(pack_all)


def _tile(s, want):
    t = min(want, s)
    assert s % t == 0
    return t


def _local_step(x, c, positions, w_mod_bf, b_mod, norm_w, w_in_bf, conv_w, a_log, dt_bias, dn_norm_w, at_norm_w,
                w_out_bf, final_norm_w, tgt):
    s = x.shape[0]
    o = [0]
    for wdt in IN_SPLITS:
        o.append(o[-1] + wdt)
    w_ba = jnp.pad(w_in_bf[:, o[2]:o[4]], ((0, 0), (0, BA_PAD - 2 * DN_HEADS)))
    ws = [w_in_bf[:, o[0]:o[1]], w_in_bf[:, o[1]:o[2]], w_ba, w_in_bf[:, o[4]:o[5]], w_in_bf[:, o[5]:o[6]],
          w_in_bf[:, o[6]:o[7]], w_in_bf[:, o[7]:o[8]]]
    conv_w8 = jnp.pad(conv_w, ((0, 8 - CONV_K), (0, 0)))
    alog_row = jnp.pad(a_log, ((0, 0), (DN_HEADS, BA_PAD - 2 * DN_HEADS)))
    dtb_row = jnp.pad(dt_bias, ((0, 0), (DN_HEADS, BA_PAD - 2 * DN_HEADS)))
    atw2 = jnp.concatenate([at_norm_w, at_norm_w], axis=1)

    half = AT_DIM // 2
    lane = jnp.arange(LANES)
    inv_freq = ROPE_THETA ** (-(lane % half).astype(F32) / half)
    ang = positions.astype(F32)[:, None] * inv_freq
    cos_t = jnp.cos(ang)
    sin_t = jnp.sin(ang) * jnp.where((lane // half) % 2 == 0, -1.0, 1.0)

    mod, siluc = _adaln_mod(c, w_mod_bf, b_mod)
    gate = mod[:, 2 * D_MODEL:]
    hbf, qkv_pre, z_dn, ba, qr, kr, vb, z_at = _ln_proj(x, mod, norm_w, ws, cos_t, sin_t, _tile(s, 256))
    q, k, v, bg = _dn_prep(qkv_pre, ba, conv_w8, alog_row, dtb_row, _tile(s, 256))
    u, w, qd, kd, p, gl, tinv = _dn_chunk_prep(q, k, v, bg, _tile(s, 512))
    o_dn, vn, st = _dn_scan(u, w, qd, kd, p, gl, _tile(s, 512))
    o_at, lse = _attn_fwd(qr, kr, vb)
    cat = _mix_prep(o_dn, z_dn, o_at, z_at, dn_norm_w, atw2, _tile(s, 512))
    dx2, dcat, gw_out, dfw, dgate, loss = _out_loss(cat, x, tgt, w_out_bf, gate, final_norm_w, _tile(s, 512))

    do_dn, dz_dn, do_at, dz_at, delta, ddnw, datw = _mix_bwd(dcat, o_dn, z_dn, o_at, z_at, dn_norm_w, atw2,
                                                             _tile(s, 512))
    daq, dak, dav = _rope_bwd(*_attn_bwd(qr, kr, vb, do_at, lse, delta), cos_t, sin_t, _tile(s, 512))
    du, dw, dqd, dkd, dp, dgl = _dn_scan_bwd(do_dn, st, vn, w, qd, kd, p, gl, _tile(s, 512))
    dq, dk, dv, dbg = _dn_chunk_bwd(q, k, v, bg, tinv, du, dw, dqd, dkd, dp, dgl, _tile(s, 512))
    dqkv, dba, dcw, dal, ddtb = _dn_prep_bwd(qkv_pre, ba, dq, dk, dv, dbg, conv_w8, alog_row, dtb_row, _tile(s, 256))
    dps = [dqkv, dz_dn, dba, daq, dak, dav, dz_at]
    gx, dshift, dscale, dnw = _dh_dx(dps, ws, x, mod, norm_w, dx2, _tile(s, 256))
    g_qkv, g_z, g_ba = _grad_w_in(hbf, dps[:3], _tile(s, 512), "grad_w_in_dn")
    g_aq, g_ak, g_av, g_az = _grad_w_in(hbf, dps[3:], _tile(s, 512), "grad_w_in_at")
    gw_in = jnp.concatenate([g_qkv, g_z, g_ba[:, :2 * DN_HEADS], g_aq, g_ak, g_av, g_az], axis=1)
    dmod = jnp.concatenate([dshift, dscale, dgate], axis=1)
    small = dict(conv=dcw[:CONV_K], dmod=dmod, siluc=siluc, dnw=dnw, dfw=dfw, alog=dal, dtb=ddtb, dnn=ddnw, atn=datw)
    return loss, gx, gw_in, gw_out, small


def kernel(x, c, positions, w_mod, b_mod, norm_w, w_in, conv_w, a_log, dt_bias, dn_norm_w, at_norm_w, w_out, final_norm_w, loss_target, m_w_mod, m_b_mod, m_norm_w, m_w_in, m_conv_w, m_a_log, m_dt_bias, m_dn_norm_w, m_at_norm_w, m_w_out, m_final_norm_w, v_w_mod, v_b_mod, v_norm_w, v_w_in, v_conv_w, v_a_log, v_dt_bias, v_dn_norm_w, v_at_norm_w, v_w_out, v_final_norm_w):
    me = 4 * lax.axis_index("x") + 2 * lax.axis_index("y") + lax.axis_index("c")
    s = x.shape[1]

    g_mod, g_in, g_conv, g_out = _all_gather(
        [_bf(w_mod[0]), _bf(w_in[0]), conv_w[0], _bf(w_out[0])], "gather_weights")
    w_mod_bf = g_mod.transpose(1, 0, 2).reshape(D_MODEL, 3 * D_MODEL)
    w_in_bf = g_in.transpose(1, 0, 2).reshape(D_MODEL, IN_COLS)
    conv_full = g_conv.transpose(1, 0, 2).reshape(CONV_K, 3 * DN_WIDTH)
    w_out_bf = g_out.reshape(D_MODEL, D_MODEL)

    loss, gx, gw_in, gw_out, small = _local_step(
        x[0], c, positions[0], w_mod_bf, b_mod, norm_w, w_in_bf, conv_full, a_log, dt_bias, dn_norm_w, at_norm_w,
        w_out_bf, final_norm_w.reshape(1, D_MODEL), loss_target[0])

    pack = jnp.concatenate([small["conv"].reshape(1, -1), small["dmod"], small["siluc"], small["dnw"], small["dfw"],
                            small["alog"], small["dtb"], small["dnn"], small["atn"],
                            jnp.pad(loss, ((0, 0), (0, LANES - 1)))], axis=1).reshape(PK_ROWS, LANES)
    gw_in_slabs = _bf(gw_in).reshape(D_MODEL, N_DEV, IN_SHARD).transpose(1, 0, 2)
    gw_out_slabs = _bf(gw_out).reshape(N_DEV, D_MODEL // N_DEV, D_MODEL)
    r_in, r_out, pack_all = _exchange([gw_in_slabs, gw_out_slabs, pack], [True, True, False], "exchange_grads")

    res = {}
    res["w_in"] = _adamw(w_in[0], m_w_in[0], v_w_in[0], r_in, "adamw_w_in", slots=True)
    res["w_out"] = _adamw(w_out[0], m_w_out[0], v_w_out[0], r_out, "adamw_w_out", slots=True)
    flat_all = pack_all.reshape(N_DEV, PK_END)
    dmod_mine = lax.dynamic_slice(flat_all, (0, PK_DMOD + me * (3 * D_MODEL // N_DEV)), (N_DEV, 3 * D_MODEL // N_DEV))
    res["w_mod"] = _adamw_w_mod(w_mod[0], m_w_mod[0], v_w_mod[0], flat_all[:, PK_SILUC:PK_DNW], dmod_mine)
    tot = _pack_sum(pack_all).reshape(1, PK_END)
    g_conv_full = tot[:, PK_CONV:PK_DMOD].reshape(CONV_K, 3 * DN_WIDTH)
    g_conv_mine = lax.dynamic_slice(g_conv_full, (0, me * (3 * DN_WIDTH // N_DEV)), (CONV_K, 3 * DN_WIDTH // N_DEV))
    res["conv_w"] = _adamw(conv_w[0], m_conv_w[0], v_conv_w[0], g_conv_mine, "adamw_conv_w")
    res["b_mod"] = _adamw(b_mod, m_b_mod, v_b_mod, tot[:, PK_DMOD:PK_SILUC], "adamw_b_mod")
    res["norm_w"] = _adamw(norm_w, m_norm_w, v_norm_w, tot[:, PK_DNW:PK_DFW], "adamw_norm_w")
    res["a_log"] = _adamw(a_log, m_a_log, v_a_log, tot[:, PK_ALOG + DN_HEADS:PK_ALOG + 2 * DN_HEADS], "adamw_a_log")
    res["dt_bias"] = _adamw(dt_bias, m_dt_bias, v_dt_bias, tot[:, PK_DTB + DN_HEADS:PK_DTB + 2 * DN_HEADS],
                            "adamw_dt_bias")
    res["dn_norm_w"] = _adamw(dn_norm_w, m_dn_norm_w, v_dn_norm_w, tot[:, PK_DNN:PK_ATN], "adamw_dn_norm_w")
    g_atn = tot[:, PK_ATN:PK_ATN + AT_DIM] + tot[:, PK_ATN + AT_DIM:PK_LOSS]
    res["at_norm_w"] = _adamw(at_norm_w, m_at_norm_w, v_at_norm_w, g_atn, "adamw_at_norm_w")
    fin = _adamw(final_norm_w.reshape(1, D_MODEL), m_final_norm_w.reshape(1, D_MODEL),
                 v_final_norm_w.reshape(1, D_MODEL), tot[:, PK_DFW:PK_ALOG], "adamw_final_norm_w")
    res["final_norm_w"] = tuple(a.reshape(D_MODEL) for a in fin)

    lead = ("w_mod", "w_in", "conv_w", "w_out")
    names = ("w_mod", "b_mod", "norm_w", "w_in", "conv_w", "a_log", "dt_bias", "dn_norm_w", "at_norm_w", "w_out",
             "final_norm_w")
    out = [tot[0, PK_LOSS], gx.reshape(1, s, D_MODEL)]
    for kind in range(4):
        for nm in names:
            a = res[nm][kind]
            out.append(a[None] if nm in lead else a)
    return tuple(out)
```

```python
import functools

import jax
import jax.numpy as jnp
from jax import lax
from jax.experimental import pallas as pl
from jax.experimental.pallas import tpu as pltpu

F32, BF16 = jnp.float32, jnp.bfloat16
HI = lax.Precision.HIGHEST
SDS = jax.ShapeDtypeStruct

D_MODEL = 1024
DN_HEADS, DN_DIM, DN_WIDTH = 4, 128, 512
AT_HEADS, AT_DIM, AT_WIDTH = 8, 64, 512
CONV_K = 4
CHUNK = 64
Q_BLOCK = 128
W_SUB = 128
DILATIONS = (1, 4, 16)
AT_PAIRS = 4
ATT_BLK = Q_BLOCK * max(DILATIONS)
ATT_UNROLL, ATT_UNROLL_BWD = 4, 4
CH_UNROLL = 4
ROPE_THETA = 10000.0
EPS = 1e-6
N_DEV = 8
LANES = 128
BA_PAD = 128
IN_SPLITS = (1536, 512, 4, 4, 512, 512, 512, 512)
IN_COLS = sum(IN_SPLITS)
IN_SHARD = IN_COLS // N_DEV
VMEM_LIMIT = 56 * 2 ** 20

ADAM_LR, ADAM_B1, ADAM_B2, ADAM_EPS, ADAM_WD, ADAM_STEP = 0.001, 0.9, 0.999, 1e-08, 0.01, 10

PK_CONV, PK_DMOD, PK_SILUC, PK_DNW, PK_DFW, PK_ALOG, PK_DTB, PK_DNN, PK_ATN, PK_LOSS, PK_END = (
    0, 6144, 9216, 10240, 11264, 12288, 12416, 12544, 12672, 12800, 12928)
PK_ROWS = PK_END // LANES

_NT = (((1,), (1,)), ((), ()))
_TN = (((0,), (0,)), ((), ()))


def _params(*sem):
    return pltpu.CompilerParams(dimension_semantics=sem or None, vmem_limit_bytes=VMEM_LIMIT)


def _bf(x):
    return x.astype(BF16)


def _nn(a, b):
    return jnp.dot(_bf(a), _bf(b), preferred_element_type=F32)


def _nt(a, b):
    return lax.dot_general(_bf(a), _bf(b), _NT, preferred_element_type=F32)


def _tn(a, b):
    return lax.dot_general(_bf(a), _bf(b), _TN, preferred_element_type=F32)


def _hnn(a, b):
    return jnp.dot(a, b, precision=HI, preferred_element_type=F32)


def _hnt(a, b):
    return lax.dot_general(a, b, _NT, precision=HI, preferred_element_type=F32)


def _htn(a, b):
    return lax.dot_general(a, b, _TN, precision=HI, preferred_element_type=F32)


@jax.custom_vjp
def _d_hnn(a, b):
    return _hnn(a, b)


def _d_hnn_fwd(a, b):
    return _hnn(a, b), (a, b)


def _d_hnn_bwd(res, g):
    a, b = res
    return _hnt(g, b), _htn(a, g)


_d_hnn.defvjp(_d_hnn_fwd, _d_hnn_bwd)


def _silu(x):
    return x * jax.nn.sigmoid(x)


def _softplus(x):
    return jnp.maximum(x, 0.0) + jnp.log(1.0 + jnp.exp(-jnp.abs(x)))


def _l2n(x):
    return x * lax.rsqrt(jnp.sum(x * x, axis=-1, keepdims=True) + EPS)


def _post_q(x):
    return _l2n(_silu(x)) * (DN_DIM ** -0.5)


def _post_k(x):
    return _l2n(_silu(x))


def _post_v(x):
    return _silu(x)


def _beta_decay(ba, alog_row, dtb_row):
    lane = lax.broadcasted_iota(jnp.int32, ba.shape, 1)
    return jnp.where(lane < DN_HEADS, jax.nn.sigmoid(ba), -jnp.exp(alog_row) * _softplus(ba + dtb_row))


def _gate_dn(o, z, w):
    return (o * lax.rsqrt(jnp.mean(o * o, axis=-1, keepdims=True) + EPS)) * w * _silu(z)


def _group_ones(scale):
    r = lax.broadcasted_iota(jnp.int32, (LANES, LANES), 0)
    c = lax.broadcasted_iota(jnp.int32, (LANES, LANES), 1)
    return jnp.where((r // AT_DIM) == (c // AT_DIM), scale, 0.0).astype(F32)


def _gate_at(o, z, w2, hnn):
    ms = hnn(o * o, _group_ones(1.0 / AT_DIM))
    return (o * lax.rsqrt(ms + EPS)) * w2 * _silu(z)


def _swap_half64(x):
    lane = lax.broadcasted_iota(jnp.int32, x.shape, 1)
    return jnp.where((lane & (AT_DIM - 1)) < AT_DIM // 2, pltpu.roll(x, LANES - AT_DIM // 2, 1),
                     pltpu.roll(x, AT_DIM // 2, 1))


_NN = (((1,), (0,)), ((), ()))


def _hl(a):
    hi = a.astype(BF16)
    return hi, (a - hi.astype(F32)).astype(BF16)


def _mm3(a, b, dims=_NN):
    (ah, al), (bh, bl) = a, b
    f = lambda x, y: lax.dot_general(x, y, dims, preferred_element_type=F32)
    return f(ah, bh) + (f(ah, bl) + f(al, bh))


def _chunk_masks():
    r = lax.broadcasted_iota(jnp.int32, (CHUNK, CHUNK), 0)
    c = lax.broadcasted_iota(jnp.int32, (CHUNK, CHUNK), 1)
    return r >= c, r > c, (r == c).astype(F32), (r // 16) == (c // 16)


def _tri_inv(mats):
    _, _, eye, blk = _chunk_masks()
    dg = [jnp.where(blk, a, 0.0) for a in mats]
    lo = [jnp.where(blk, 0.0, a) for a in mats]
    sdg = [_hl(x) for x in dg]
    d2 = [_mm3(s, s) for s in sdg]
    sd2 = [_hl(x) for x in d2]
    d4 = [_mm3(s, s) for s in sd2]
    sd4 = [_hl(x) for x in d4]
    d8 = [_mm3(s, s) for s in sd4]
    p1 = [_mm3(_hl(eye - a), _hl(eye + b)) for a, b in zip(dg, d2)]
    p2 = [_mm3(_hl(a), _hl(eye + b)) for a, b in zip(p1, d4)]
    dinv = [_mm3(_hl(a), _hl(eye + b)) for a, b in zip(p2, d8)]
    sdinv = [_hl(x) for x in dinv]
    n1 = [_mm3(s, _hl(b)) for s, b in zip(sdinv, lo)]
    sn1 = [_hl(x) for x in n1]
    n2 = [_mm3(s, s) for s in sn1]
    q1 = [_mm3(_hl(eye - a), _hl(eye + b)) for a, b in zip(n1, n2)]
    return [_mm3(_hl(a), s) for a, s in zip(q1, sdinv)]


def _chunk_common(qs, ks, vs, betas, gcs):
    tril, _, _, _ = _chunk_masks()
    out = []
    for q, k, v, beta, gc in zip(qs, ks, vs, betas, gcs):
        gb = jnp.broadcast_to(gc, (CHUNK, DN_DIM))
        gt = gb.T[:CHUNK, :]
        gam = jnp.where(tril, jnp.exp(jnp.where(tril, gb[:, :CHUNK] - gt, 0.0)), 0.0)
        last = gb[CHUNK - 1:CHUNK, :]
        eg, e2 = jnp.exp(gb), jnp.exp(last - gb)
        kb, vb = k * beta, v * beta
        out.append(dict(gam=gam, eg=eg, e2=e2, gl=jnp.exp(last[:, 0:1]), kb=kb, vb=vb, kbg=kb * eg,
                        m=_nt(kb, k), qk=_nt(q, k)))
    return out


def _chunk_fwd(qs, ks, vs, betas, gcs):
    tril, strict, _, _ = _chunk_masks()
    cm = _chunk_common(qs, ks, vs, betas, gcs)
    ts = _tri_inv([jnp.where(strict, c["m"] * c["gam"], 0.0) for c in cm])
    outs = []
    for q, k, c, t in zip(qs, ks, cm, ts):
        uw = _nn(t, jnp.concatenate([c["vb"], c["kbg"]], axis=1))
        p = jnp.where(tril, c["qk"] * c["gam"], 0.0)
        outs.append((uw[:, :DN_DIM], uw[:, DN_DIM:], p, q * c["eg"], k * c["e2"], c["gl"], t.T))
    return outs


def _chunk_bwd(qs, ks, vs, betas, gcs, ts, cots):
    tril, strict, _, _ = _chunk_masks()
    cm = _chunk_common(qs, ks, vs, betas, gcs)
    row = lax.broadcasted_iota(jnp.int32, (CHUNK, 1), 0)
    ones = jnp.ones((CHUNK, DN_DIM), BF16)
    rs = lambda x: jnp.sum(x, axis=-1, keepdims=True)
    tts = [_bf(t) for t in ts]
    duw = [_bf(jnp.concatenate([ct[0], ct[1]], axis=1)) for ct in cots]
    dts = [_nt(a, jnp.concatenate([c["vb"], c["kbg"]], axis=1)) for a, c in zip(duw, cm)]
    xs = [_nn(t, d) for t, d in zip(tts, dts)]
    das = [jnp.where(strict, -_nn(x, t), 0.0) for x, t in zip(xs, tts)]
    dvks = [_nn(t, a) for t, a in zip(tts, duw)]
    outs = []
    for q, k, v, beta, c, ct, da, dvk in zip(qs, ks, vs, betas, cm, cots, das, dvks):
        _, _, dp, dqd, dkd, dgl = ct
        dvb, dkbg = dvk[:, :DN_DIM], dvk[:, DN_DIM:]
        dm = da * c["gam"]
        dqk = jnp.where(tril, dp, 0.0) * c["gam"]
        e = dm * c["m"] + dqk * c["qk"]
        dmq = jnp.concatenate([dm, dqk], axis=0)
        r1 = _nn(dmq, k)
        dkb = r1[:CHUNK] + dkbg * c["eg"]
        dq = r1[CHUNK:] + dqd * c["eg"]
        dk = _tn(dmq, jnp.concatenate([c["kb"], q], axis=0)) + dkd * c["e2"] + dkb * beta
        dbeta = rs(dkb * k + dvb * v)
        eh, el = _hl(e)
        colsum = (lax.dot_general(eh, ones, _TN, preferred_element_type=F32)
                  + lax.dot_general(el, ones, _TN, preferred_element_type=F32))[:, 0:1]
        pkd = dkd * (k * c["e2"])
        dgc = rs(e) - colsum + rs(dqd * q * c["eg"] + dkbg * c["kbg"] - pkd)
        tail = rs(jnp.sum(pkd, axis=0, keepdims=True)) + dgl * c["gl"]
        dgc = dgc + jnp.where(row == CHUNK - 1, tail, 0.0)
        outs.append((dq, dk, dvb * beta, dbeta, dgc))
    return outs


def _chunk_cumsum(x, reverse=False):
    n = x.shape[0]
    pos = lax.broadcasted_iota(jnp.int32, x.shape, 0) & (CHUNK - 1)
    sh = 1
    while sh < CHUNK:
        if reverse:
            x = x + jnp.where(pos < CHUNK - sh, pltpu.roll(x, n - sh, 0), 0.0)
        else:
            x = x + jnp.where(pos >= sh, pltpu.roll(x, sh, 0), 0.0)
        sh *= 2
    return x


GC_LANE = 2 * DN_HEADS


def _exchange(arrays, scatter, name):
    n = len(arrays)
    out_shapes = []
    for a, sc in zip(arrays, scatter):
        out_shapes.append(SDS(a.shape if sc else (N_DEV,) + a.shape, a.dtype))

    def body(*refs):
        ins, outs = refs[:n], refs[n:2 * n]
        send_sems, recv_sems, loc_sems = refs[2 * n:]
        x, y, c = lax.axis_index("x"), lax.axis_index("y"), lax.axis_index("c")
        me = 4 * x + 2 * y + c
        local, remote = [], []
        for i in range(n):
            src = ins[i].at[me] if scatter[i] else ins[i]
            cp = pltpu.make_async_copy(src, outs[i].at[me], loc_sems.at[i])
            cp.start()
            local.append(cp)
        for dlt in range(1, N_DEV):
            px = 1 - x if dlt & 4 else x
            py = 1 - y if dlt & 2 else y
            pc = 1 - c if dlt & 1 else c
            peer = 4 * px + 2 * py + pc
            for i in range(n):
                src = ins[i].at[peer] if scatter[i] else ins[i]
                cp = pltpu.make_async_remote_copy(
                    src_ref=src, dst_ref=outs[i].at[me],
                    send_sem=send_sems.at[i, dlt - 1], recv_sem=recv_sems.at[i, dlt - 1],
                    device_id=(px, py, pc), device_id_type=pl.DeviceIdType.MESH)
                cp.start()
                arrive = pltpu.make_async_remote_copy(
                    src_ref=src, dst_ref=outs[i].at[peer],
                    send_sem=send_sems.at[i, dlt - 1], recv_sem=recv_sems.at[i, dlt - 1],
                    device_id=(px, py, pc), device_id_type=pl.DeviceIdType.MESH)
                remote.append((cp, arrive))
        for cp, arrive in remote:
            cp.wait_send()
            arrive.wait_recv()
        for cp in local:
            cp.wait()

    any_spec = pl.BlockSpec(memory_space=pl.ANY)
    return pl.pallas_call(
        body, name=name, out_shape=tuple(out_shapes),
        in_specs=[any_spec] * n, out_specs=tuple([any_spec] * n),
        scratch_shapes=[pltpu.SemaphoreType.DMA((n, N_DEV - 1)), pltpu.SemaphoreType.DMA((n, N_DEV - 1)),
                        pltpu.SemaphoreType.DMA((n,))],
    )(*arrays)


def _all_gather(arrays, name):
    n = len(arrays)

    def body(*refs):
        ins, outs = refs[:n], refs[n:2 * n]
        send_sems, recv_sems, loc_sems = refs[2 * n:]
        x, y, c = lax.axis_index("x"), lax.axis_index("y"), lax.axis_index("c")
        me, sibling = (x, y, c), (x, y, 1 - c)
        chips = [(1 - x, y), (x, 1 - y), (1 - x, 1 - y)]

        def copy(i, k, block, to, src=None):
            slot = outs[i].at[4 * block[0] + 2 * block[1] + block[2]]
            return pltpu.make_async_remote_copy(
                src_ref=slot if src is None else src, dst_ref=slot,
                send_sem=send_sems.at[i, k], recv_sem=recv_sems.at[i, k],
                device_id=to, device_id_type=pl.DeviceIdType.MESH)

        mine = [pltpu.make_async_copy(ins[i], outs[i].at[4 * x + 2 * y + c], loc_sems.at[i]) for i in range(n)]
        for cp in mine:
            cp.start()
        first = []
        for i in range(n):
            first.append(copy(i, 0, me, sibling, src=ins[i]))
            first += [copy(i, 1 + j, me, (*chip, c), src=ins[i]) for j, chip in enumerate(chips)]
        for cp in first:
            cp.start()
        passed = []
        for j, chip in enumerate(chips):
            for i in range(n):
                copy(i, 1 + j, (*chip, c), me).wait_recv()
                fwd = copy(i, 4 + j, (*chip, c), sibling)
                fwd.start()
                passed.append(fwd)
        for i in range(n):
            copy(i, 0, sibling, me).wait_recv()
        for j, chip in enumerate(chips):
            for i in range(n):
                copy(i, 4 + j, (*chip, 1 - c), me).wait_recv()
        for cp in first + passed:
            cp.wait_send()
        for cp in mine:
            cp.wait()

    any_spec = pl.BlockSpec(memory_space=pl.ANY)
    return pl.pallas_call(
        body, name=name, out_shape=tuple(SDS((N_DEV,) + a.shape, a.dtype) for a in arrays),
        in_specs=[any_spec] * n, out_specs=tuple([any_spec] * n),
        scratch_shapes=[pltpu.SemaphoreType.DMA((n, N_DEV - 1)), pltpu.SemaphoreType.DMA((n, N_DEV - 1)),
                        pltpu.SemaphoreType.DMA((n,))],
    )(*arrays)


def _adaln_mod(c, w_mod, b_mod):
    def body(c_ref, w_ref, b_ref, mod_ref, sc_ref):
        sc = _silu(c_ref[...])
        sc8 = jnp.broadcast_to(sc, (8, D_MODEL))
        mod_ref[...] = _nn(sc8, w_ref[...])[0:1] + b_ref[...]
        sc_ref[...] = sc

    return pl.pallas_call(body, name="adaln_mod", compiler_params=_params(),
                          out_shape=(SDS((1, 3 * D_MODEL), F32), SDS((1, D_MODEL), F32)))(c, w_mod, b_mod)


def _ln_proj(x, mod, norm_w, ws, cos_t, sin_t, ts):
    s = x.shape[0]
    widths = [w.shape[1] for w in ws]

    def body(x_ref, mod_ref, nw_ref, cos_ref, sin_ref, wqkv, wz, wba, waq, wak, wav, waz,
             h_ref, oqkv, oz, oba, oq, ok, ov, oaz):
        xt = x_ref[...]
        r = lax.rsqrt(jnp.mean(xt * xt, axis=-1, keepdims=True) + EPS)
        shift, scale = mod_ref[:, 0:D_MODEL], mod_ref[:, D_MODEL:2 * D_MODEL]
        h = ((xt * r) * nw_ref[...]) * (1.0 + scale) + shift
        hb = _bf(h)
        h_ref[...] = hb
        oqkv[...] = jnp.dot(hb, wqkv[...], preferred_element_type=F32)
        oz[...] = jnp.dot(hb, wz[...], preferred_element_type=F32)
        oba[...] = jnp.dot(hb, wba[...], preferred_element_type=F32)
        oaz[...] = jnp.dot(hb, waz[...], preferred_element_type=F32)
        tv = jnp.dot(hb, wav[...], preferred_element_type=F32)
        for j in range(AT_PAIRS):
            ov[j] = tv[:, j * LANES:(j + 1) * LANES]
        cs, sn = cos_ref[...], sin_ref[...]
        for w_ref, o_ref in ((waq, oq), (wak, ok)):
            t = jnp.dot(hb, w_ref[...], preferred_element_type=F32)
            for j in range(AT_PAIRS):
                tj = t[:, j * LANES:(j + 1) * LANES]
                o_ref[j] = tj * cs + _swap_half64(tj) * sn

    tok = lambda w: pl.BlockSpec((ts, w), lambda i: (i, 0))
    full = lambda a: pl.BlockSpec(a.shape, lambda i: (0, 0))
    pairs = pl.BlockSpec((AT_PAIRS, ts, LANES), lambda i: (0, i, 0))
    return pl.pallas_call(
        body, name="ln_proj", grid=(s // ts,), compiler_params=_params("arbitrary"),
        in_specs=[tok(D_MODEL), full(mod), full(norm_w), tok(LANES), tok(LANES)] + [full(w) for w in ws],
        out_specs=(tok(D_MODEL), tok(widths[0]), tok(widths[1]), tok(widths[2]), pairs, pairs, pairs,
                   tok(widths[6])),
        out_shape=(SDS((s, D_MODEL), BF16), SDS((s, widths[0]), F32), SDS((s, widths[1]), F32),
                   SDS((s, widths[2]), F32)) + (SDS((AT_PAIRS, s, LANES), F32),) * 3 + (SDS((s, widths[6]), F32),),
    )(x, mod, norm_w, cos_t, sin_t, *ws)


def _conv_taps(ext, rows):
    taps = []
    for j in range(CONV_K):
        sh = CONV_K - 1 - j
        rolled = pltpu.roll(ext, sh, 0) if sh else ext
        taps.append(rolled[8:8 + rows])
    return taps


def _dn_prep(qkv_pre, ba, conv_w8, alog_row, dtb_row, ts):
    s = qkv_pre.shape[0]
    cw = 3 * DN_WIDTH

    def body(pre_ref, halo_ref, ba_ref, cw_ref, al_ref, dtb_ref, q_ref, k_ref, v_ref, bg_ref):
        n = pl.program_id(0)
        prev = jnp.where(n == 0, 0.0, halo_ref[...])
        ext = jnp.concatenate([prev, pre_ref[...]], axis=0)
        taps = _conv_taps(ext, ts)
        conv = taps[0] * cw_ref[0:1, :]
        for j in range(1, CONV_K):
            conv = conv + taps[j] * cw_ref[j:j + 1, :]
        for h in range(DN_HEADS):
            cols = slice(h * DN_DIM, (h + 1) * DN_DIM)
            q_ref[:, cols] = _post_q(conv[:, h * DN_DIM:(h + 1) * DN_DIM])
            k_ref[:, cols] = _post_k(conv[:, DN_WIDTH + h * DN_DIM:DN_WIDTH + (h + 1) * DN_DIM])
            v_ref[:, cols] = _post_v(conv[:, 2 * DN_WIDTH + h * DN_DIM:2 * DN_WIDTH + (h + 1) * DN_DIM])
        bg = _beta_decay(ba_ref[...], al_ref[...], dtb_ref[...])
        lane = lax.broadcasted_iota(jnp.int32, bg.shape, 1)
        run = pltpu.roll(_chunk_cumsum(bg), DN_HEADS, 1)
        bg_ref[...] = jnp.where((lane >= GC_LANE) & (lane < GC_LANE + DN_HEADS), run, bg)

    tok = lambda w: pl.BlockSpec((ts, w), lambda i: (i, 0))
    full = lambda a: pl.BlockSpec(a.shape, lambda i: (0, 0))
    halo = pl.BlockSpec((8, cw), lambda i: (jnp.maximum(i * (ts // 8) - 1, 0), 0))
    return pl.pallas_call(
        body, name="dn_prep", grid=(s // ts,), compiler_params=_params("arbitrary"),
        in_specs=[tok(cw), halo, tok(BA_PAD), full(conv_w8), full(alog_row), full(dtb_row)],
        out_specs=(tok(DN_WIDTH), tok(DN_WIDTH), tok(DN_WIDTH), tok(BA_PAD)),
        out_shape=(SDS((s, DN_WIDTH), F32),) * 3 + (SDS((s, BA_PAD), F32),),
    )(qkv_pre, qkv_pre, ba, conv_w8, alog_row, dtb_row)


def _dn_chunk_prep(q, k, v, bg, ts):
    s = q.shape[0]
    ncs = ts // CHUNK

    def body(q_ref, k_ref, v_ref, bg_ref, u_ref, w_ref, qd_ref, kd_ref, p_ref, gl_ref, t_ref):
        def chunks(cg, carry):
            where = []
            for ci in (cg * CH_UNROLL + i for i in range(CH_UNROLL)):
                rows = pl.ds(pl.multiple_of(ci * CHUNK, CHUNK), CHUNK)
                rows8 = pl.ds(pl.multiple_of(ci * 8, 8), 8)
                where += [(rows, rows8, h, slice(h * DN_DIM, (h + 1) * DN_DIM)) for h in range(DN_HEADS)]
            bgs = [bg_ref[rows, :] for rows, _, _, _ in where]
            outs = _chunk_fwd([q_ref[rows, c] for rows, _, _, c in where], [k_ref[rows, c] for rows, _, _, c in where],
                              [v_ref[rows, c] for rows, _, _, c in where],
                              [b[:, h:h + 1] for b, (_, _, h, _) in zip(bgs, where)],
                              [b[:, GC_LANE + h:GC_LANE + h + 1] for b, (_, _, h, _) in zip(bgs, where)])
            for (rows, rows8, h, c), (u, w, p, qd, kd, gl, t) in zip(where, outs):
                u_ref[rows, c] = u
                w_ref[rows, c] = w
                qd_ref[rows, c] = qd
                kd_ref[rows, c] = kd
                p_ref[h, rows, :] = p
                t_ref[h, rows, :] = t
                gl_ref[rows8, c] = jnp.broadcast_to(gl, (8, DN_DIM))
            return carry

        lax.fori_loop(0, ncs // CH_UNROLL, chunks, 0)

    tok = lambda w: pl.BlockSpec((ts, w), lambda i: (i, 0))
    sq = pl.BlockSpec((DN_HEADS, ts, CHUNK), lambda i: (0, i, 0))
    return pl.pallas_call(
        body, name="dn_chunk_prep", grid=(s // ts,), compiler_params=_params("arbitrary"),
        in_specs=[tok(DN_WIDTH)] * 3 + [tok(BA_PAD)],
        out_specs=(tok(DN_WIDTH),) * 4 + (sq, pl.BlockSpec((ncs * 8, DN_WIDTH), lambda i: (i, 0)), sq),
        out_shape=(SDS((s, DN_WIDTH), F32),) * 4 + (SDS((DN_HEADS, s, CHUNK), F32),
                                                     SDS((s // CHUNK * 8, DN_WIDTH), F32),
                                                     SDS((DN_HEADS, s, CHUNK), F32)),
    )(q, k, v, bg)


def _dn_scan(u, w, qd, kd, p, gl, ts):
    s = u.shape[0]
    ncs = ts // CHUNK

    def body(u_ref, w_ref, qd_ref, kd_ref, p_ref, gl_ref, o_ref, vn_ref, st_ref, state):
        @pl.when(pl.program_id(0) == 0)
        def _():
            state[...] = jnp.zeros_like(state)

        def chunk(ci, carry):
            rows = pl.ds(pl.multiple_of(ci * CHUNK, CHUNK), CHUNK)
            rows8 = pl.ds(pl.multiple_of(ci * 8, 8), 8)
            srows = pl.ds(pl.multiple_of(ci * DN_DIM, DN_DIM), DN_DIM)
            hs = range(DN_HEADS)
            sl = [slice(h * DN_DIM, (h + 1) * DN_DIM) for h in hs]
            sf = [state[h] for h in hs]
            sb = [_bf(x) for x in sf]
            ws = [_nn(w_ref[rows, c], b) for c, b in zip(sl, sb)]
            qs = [_nn(qd_ref[rows, c], b) for c, b in zip(sl, sb)]
            vn = [u_ref[rows, c] - x for c, x in zip(sl, ws)]
            vb = [_bf(x) for x in vn]
            kv = [_tn(kd_ref[rows, c], b) for c, b in zip(sl, vb)]
            pv = [_nn(p_ref[h, rows, :], b) for h, b in zip(hs, vb)]
            for h in hs:
                state[h] = sf[h] * gl_ref[rows8, sl[h]][0:1] + kv[h]
            for h in hs:
                st_ref[srows, sl[h]] = sf[h]
                vn_ref[rows, sl[h]] = vn[h]
                o_ref[rows, sl[h]] = qs[h] + pv[h]
            return carry

        lax.fori_loop(0, ncs, chunk, 0)

    tok = lambda wd: pl.BlockSpec((ts, wd), lambda i: (i, 0))
    return pl.pallas_call(
        body, name="dn_scan", grid=(s // ts,), compiler_params=_params("arbitrary"),
        in_specs=[tok(DN_WIDTH)] * 4 + [pl.BlockSpec((DN_HEADS, ts, CHUNK), lambda i: (0, i, 0)),
                                        pl.BlockSpec((ncs * 8, DN_WIDTH), lambda i: (i, 0))],
        out_specs=(tok(DN_WIDTH), tok(DN_WIDTH), pl.BlockSpec((ncs * DN_DIM, DN_WIDTH), lambda i: (i, 0))),
        out_shape=(SDS((s, DN_WIDTH), F32), SDS((s, DN_WIDTH), F32), SDS((s // CHUNK * DN_DIM, DN_WIDTH), F32)),
        scratch_shapes=[pltpu.VMEM((DN_HEADS, DN_DIM, DN_DIM), F32)],
    )(u, w, qd, kd, p, gl)


LOG2E, LN2 = 1.4426950408889634, 0.6931471805599453
MASKED = -1e30


def _band_bias():
    qi = lax.broadcasted_iota(jnp.int32, (Q_BLOCK, 2 * Q_BLOCK), 0)
    kj = lax.broadcasted_iota(jnp.int32, (Q_BLOCK, 2 * Q_BLOCK), 1)
    rel = Q_BLOCK + qi - kj
    return jnp.where((rel >= 0) & (rel <= W_SUB), 0.0, MASKED)


def _first_bias(first):
    kj = lax.broadcasted_iota(jnp.int32, (1, 2 * Q_BLOCK), 1)
    return jnp.where((kj < Q_BLOCK) & first, MASKED, 0.0)


def _attn_combo(c, d):
    if d == 1:
        qs = pl.multiple_of(c * Q_BLOCK, Q_BLOCK)
        return qs, pl.multiple_of(ATT_BLK - Q_BLOCK + c * Q_BLOCK, Q_BLOCK), c == 0
    r, m = c % d, c // d
    qs = r + (d * Q_BLOCK) * m
    return qs, ATT_BLK + qs - d * Q_BLOCK, m == 0


def _rows(start, size, d):
    return pl.ds(pl.multiple_of(start, Q_BLOCK), size) if d == 1 else pl.ds(start, size, stride=d)


def _shift_in(ext, cur, n):
    @pl.when(n == 0)
    def _():
        ext[0:ATT_BLK, :] = jnp.zeros((ATT_BLK, LANES), F32)

    @pl.when(n > 0)
    def _():
        ext[0:ATT_BLK, :] = ext[ATT_BLK:2 * ATT_BLK, :]

    ext[ATT_BLK:2 * ATT_BLK, :] = cur


def _attn_fwd(qr, kr, vv):
    s = qr.shape[1]
    nblk = s // ATT_BLK
    scale = AT_DIM ** -0.5
    npat = len(DILATIONS)

    def body(q_ref, k_ref, v_ref, o_ref, lse_ref, kext, vext, o_p, l_p, bias_ref):
        n = pl.program_id(1)
        _shift_in(kext, k_ref[0], n)
        _shift_in(vext, v_ref[0], n)
        bias_ref[...] = _band_bias()
        lo = lax.broadcasted_iota(jnp.int32, (Q_BLOCK, LANES), 1) < AT_DIM
        for pi, d in enumerate(DILATIONS):
            def group(g, carry, pi=pi, d=d):
                cs = [_attn_combo(g * ATT_UNROLL + u, d) for u in range(ATT_UNROLL)]
                heads = [(i, sel) for i in range(ATT_UNROLL) for sel in (lo, ~lo)]
                band = bias_ref[...]
                bias = [band + _first_bias((n == 0) & m0) for _, _, m0 in cs]
                qb = [_bf(q_ref[0, _rows(qs, Q_BLOCK, d), :]) for qs, _, _ in cs]
                kk = [_bf(kext[_rows(ks, 2 * Q_BLOCK, d), :]) for _, ks, _ in cs]
                vb = [_bf(vext[_rows(ks, 2 * Q_BLOCK, d), :]) for _, ks, _ in cs]
                sc = [lax.dot_general(jnp.where(sel, qb[i], jnp.zeros_like(qb[i])), kk[i], _NT,
                                      preferred_element_type=F32) for i, sel in heads]
                sc = [x * (scale * LOG2E) + bias[i] for x, (i, _) in zip(sc, heads)]
                mx = [jnp.max(x, axis=-1, keepdims=True) for x in sc]
                pr = [jnp.exp2(x - m) for x, m in zip(sc, mx)]
                ls = [jnp.sum(x, axis=-1, keepdims=True) for x in pr]
                pv = [jnp.dot(_bf(x), vb[i], preferred_element_type=F32) for x, (i, _) in zip(pr, heads)]
                outs = [x / l for x, l in zip(pv, ls)]
                lses = [m * LN2 + jnp.log(l) for m, l in zip(mx, ls)]
                for i, (qs, _, _) in enumerate(cs):
                    o_p[pi, _rows(qs, Q_BLOCK, d), :] = jnp.where(lo, outs[2 * i], outs[2 * i + 1])
                    l_p[pi, _rows(qs, Q_BLOCK, d), :] = jnp.where(lo, lses[2 * i], lses[2 * i + 1])
                return carry

            lax.fori_loop(0, ATT_BLK // Q_BLOCK // ATT_UNROLL, group, 0)

        def merge(i, carry):
            rows = pl.ds(pl.multiple_of(i * 256, 256), 256)
            ls = [l_p[pi, rows, :] for pi in range(npat)]
            mx = jnp.maximum(jnp.maximum(ls[0], ls[1]), ls[2])
            es = [jnp.exp(l - mx) for l in ls]
            den = es[0] + es[1] + es[2]
            o_ref[0, rows, :] = (es[0] * o_p[0, rows, :] + es[1] * o_p[1, rows, :] + es[2] * o_p[2, rows, :]) / den
            lse_ref[0, rows, :] = mx + jnp.log(den)
            return carry

        lax.fori_loop(0, ATT_BLK // 256, merge, 0)

    blk = pl.BlockSpec((1, ATT_BLK, LANES), lambda j, n: (j, n, 0))
    return pl.pallas_call(
        body, name="attn_fwd", grid=(AT_PAIRS, nblk), compiler_params=_params("arbitrary", "arbitrary"),
        in_specs=[blk] * 3, out_specs=(blk, blk),
        out_shape=(SDS((AT_PAIRS, s, LANES), F32),) * 2,
        scratch_shapes=[pltpu.VMEM((2 * ATT_BLK, LANES), F32), pltpu.VMEM((2 * ATT_BLK, LANES), F32),
                        pltpu.VMEM((npat, ATT_BLK, LANES), F32), pltpu.VMEM((npat, ATT_BLK, LANES), F32),
                        pltpu.VMEM((Q_BLOCK, 2 * Q_BLOCK), F32)],
    )(qr, kr, vv)


def _out_loss(o_dn, z_dn, o_at, z_at, dnw, atw2, x, tgt, w_out, gate, fw, ts):
    s = x.shape[0]

    def body(odn, zdn, oat, zat, dnw_ref, atw_ref, x_ref, t_ref, w_ref, g_ref, fw_ref,
             dx2_ref, dcat_ref, gw_ref, dfw_ref, dgate_ref, loss_ref):
        @pl.when(pl.program_id(0) == 0)
        def _():
            gw_ref[...] = jnp.zeros_like(gw_ref)
            dfw_ref[...] = jnp.zeros_like(dfw_ref)
            dgate_ref[...] = jnp.zeros_like(dgate_ref)
            loss_ref[...] = jnp.zeros_like(loss_ref)

        parts = [_bf(_gate_dn(odn[:, h * DN_DIM:(h + 1) * DN_DIM], zdn[:, h * DN_DIM:(h + 1) * DN_DIM], dnw_ref[...]))
                 for h in range(DN_HEADS)]
        parts += [_bf(_gate_at(oat[j], zat[:, j * LANES:(j + 1) * LANES], atw_ref[...], _hnn))
                  for j in range(AT_PAIRS)]
        catb = jnp.concatenate(parts, axis=1)
        wb = w_ref[...]
        gate, fwv = g_ref[...], fw_ref[...]
        mix = jnp.dot(catb, wb, preferred_element_type=F32)
        x2 = x_ref[...] + gate * mix
        r2 = lax.rsqrt(jnp.mean(x2 * x2, axis=-1, keepdims=True) + EPS)
        xn2 = x2 * r2
        err = xn2 * fwv - t_ref[...]
        row = jnp.sum(err * err, axis=-1, keepdims=True) * (1.0 / D_MODEL)
        loss_ref[...] += 0.5 * jnp.sum(row, axis=0, keepdims=True)
        dy = err * (1.0 / D_MODEL)
        dfw_ref[...] += jnp.sum(dy * xn2, axis=0, keepdims=True)
        dxn = dy * fwv
        dx2 = r2 * (dxn - xn2 * jnp.mean(dxn * xn2, axis=-1, keepdims=True))
        dx2_ref[...] = dx2
        dgate_ref[...] += jnp.sum(dx2 * mix, axis=0, keepdims=True)
        dmix = _bf(gate * dx2)
        dcat_ref[...] = lax.dot_general(dmix, wb, _NT, preferred_element_type=F32)
        gw_ref[...] += lax.dot_general(catb, dmix, _TN, preferred_element_type=F32)

    tok = lambda w: pl.BlockSpec((ts, w), lambda i: (i, 0))
    full = lambda a: pl.BlockSpec(a.shape, lambda i: (0, 0))
    row = pl.BlockSpec((1, D_MODEL), lambda i: (0, 0))
    pairs = pl.BlockSpec((AT_PAIRS, ts, LANES), lambda i: (0, i, 0))
    return pl.pallas_call(
        body, name="out_loss", grid=(s // ts,), compiler_params=_params("arbitrary"),
        in_specs=[tok(DN_WIDTH), tok(DN_WIDTH), pairs, tok(AT_WIDTH), full(dnw), full(atw2),
                  tok(D_MODEL), tok(D_MODEL), full(w_out), full(gate), full(fw)],
        out_specs=(tok(D_MODEL), tok(D_MODEL), pl.BlockSpec((D_MODEL, D_MODEL), lambda i: (0, 0)), row, row,
                   pl.BlockSpec((1, 1), lambda i: (0, 0))),
        out_shape=(SDS((s, D_MODEL), F32), SDS((s, D_MODEL), F32), SDS((D_MODEL, D_MODEL), F32),
                   SDS((1, D_MODEL), F32), SDS((1, D_MODEL), F32), SDS((1, 1), F32)),
    )(o_dn, z_dn, o_at, z_at, dnw, atw2, x, tgt, w_out, gate, fw)


def _mix_bwd(dcat, o_dn, z_dn, o_at, z_at, dnw, atw2, ts):
    s = dcat.shape[0]

    def body(dcat_ref, odn, zdn, oat, zat, dnw_ref, atw_ref, dodn, dzdn, doat, dzat, delta, ddnw, datw):
        @pl.when(pl.program_id(0) == 0)
        def _():
            ddnw[...] = jnp.zeros_like(ddnw)
            datw[...] = jnp.zeros_like(datw)

        for h in range(DN_HEADS):
            cols = slice(h * DN_DIM, (h + 1) * DN_DIM)
            _, vjp = jax.vjp(_gate_dn, odn[:, cols], zdn[:, cols], dnw_ref[...])
            do, dz, dw = vjp(dcat_ref[:, cols])
            dodn[:, cols] = do
            dzdn[:, cols] = _bf(dz)
            ddnw[...] += dw
        for j in range(AT_PAIRS):
            cols = slice(j * LANES, (j + 1) * LANES)
            o = oat[j]
            _, vjp = jax.vjp(functools.partial(_gate_at, hnn=_d_hnn), o, zat[:, cols], atw_ref[...])
            do, dz, dw = vjp(dcat_ref[:, DN_WIDTH + j * LANES:DN_WIDTH + (j + 1) * LANES])
            doat[j] = do
            dzat[:, cols] = _bf(dz)
            datw[...] += dw
            delta[j] = _hnn(do * o, _group_ones(1.0))

    tok = lambda w: pl.BlockSpec((ts, w), lambda i: (i, 0))
    full = lambda a: pl.BlockSpec(a.shape, lambda i: (0, 0))
    row = pl.BlockSpec((1, LANES), lambda i: (0, 0))
    pairs = pl.BlockSpec((AT_PAIRS, ts, LANES), lambda i: (0, i, 0))
    return pl.pallas_call(
        body, name="mix_bwd", grid=(s // ts,), compiler_params=_params("arbitrary"),
        in_specs=[tok(D_MODEL), tok(DN_WIDTH), tok(DN_WIDTH), pairs, tok(AT_WIDTH), full(dnw), full(atw2)],
        out_specs=(tok(DN_WIDTH), tok(DN_WIDTH), pairs, tok(AT_WIDTH), pairs, row, row),
        out_shape=(SDS((s, DN_WIDTH), F32), SDS((s, DN_WIDTH), BF16), SDS((AT_PAIRS, s, LANES), F32),
                   SDS((s, AT_WIDTH), BF16), SDS((AT_PAIRS, s, LANES), F32), SDS((1, LANES), F32),
                   SDS((1, LANES), F32)),
    )(dcat, o_dn, z_dn, o_at, z_at, dnw, atw2)


def _shift_acc(ext, n):
    @pl.when(n == 0)
    def _():
        ext[0:ATT_BLK, :] = jnp.zeros((ATT_BLK, LANES), F32)

    @pl.when(n > 0)
    def _():
        ext[0:ATT_BLK, :] = ext[ATT_BLK:2 * ATT_BLK, :]

    ext[ATT_BLK:2 * ATT_BLK, :] = jnp.zeros((ATT_BLK, LANES), F32)


def _attn_bwd(qr, kr, vv, do, lse, delta):
    s = qr.shape[1]
    nblk = s // ATT_BLK
    scale = AT_DIM ** -0.5

    def body(q_ref, k_ref, v_ref, do_ref, lse_ref, dl_ref, dq_ref, dk_ref, dv_ref, kext, vext, dkext, dvext,
             bias_ref):
        n = pl.program_id(1)
        _shift_in(kext, k_ref[0], n)
        _shift_in(vext, v_ref[0], n)
        _shift_acc(dkext, n)
        _shift_acc(dvext, n)
        bias_ref[...] = _band_bias()

        @pl.when(n < nblk)
        def _():
            dq_ref[0] = jnp.zeros((ATT_BLK, LANES), F32)
            lo = lax.broadcasted_iota(jnp.int32, (Q_BLOCK, LANES), 1) < AT_DIM
            for d in DILATIONS:
                def group(g, carry, d=d):
                    nu = ATT_UNROLL_BWD
                    cs = [_attn_combo(g * nu + u, d) for u in range(nu)]
                    heads = [(i, sel) for i in range(nu) for sel in (lo, ~lo)]
                    qrows = [_rows(qs, Q_BLOCK, d) for qs, _, _ in cs]
                    krows = [_rows(ks, 2 * Q_BLOCK, d) for _, ks, _ in cs]
                    band = bias_ref[...]
                    bias = [band + _first_bias((n == 0) & m0) for _, _, m0 in cs]
                    qb = [_bf(q_ref[0, r, :]) for r in qrows]
                    dob = [_bf(do_ref[0, r, :]) for r in qrows]
                    kk = [_bf(kext[r, :]) for r in krows]
                    vb = [_bf(vext[r, :]) for r in krows]
                    lse2 = [lse_ref[0, r, :] * LOG2E for r in qrows]
                    dl2 = [dl_ref[0, r, :] for r in qrows]
                    qm = [jnp.where(sel, qb[i], jnp.zeros_like(qb[i])) for i, sel in heads]
                    dom = [jnp.where(sel, dob[i], jnp.zeros_like(dob[i])) for i, sel in heads]
                    lse_c = [jnp.max(jnp.where(sel, lse2[i], -jnp.inf), axis=-1, keepdims=True) for i, sel in heads]
                    dl_c = [jnp.max(jnp.where(sel, dl2[i], -jnp.inf), axis=-1, keepdims=True) for i, sel in heads]
                    sc = [lax.dot_general(a, kk[i], _NT, preferred_element_type=F32) for a, (i, _) in zip(qm, heads)]
                    dp = [lax.dot_general(a, vb[i], _NT, preferred_element_type=F32) for a, (i, _) in zip(dom, heads)]
                    pr = [jnp.exp2(x * (scale * LOG2E) + bias[i] - l) for x, l, (i, _) in zip(sc, lse_c, heads)]
                    ds = [_bf(p * (x - dl) * scale) for p, x, dl in zip(pr, dp, dl_c)]
                    prb = [_bf(p) for p in pr]
                    dq = [jnp.dot(x, kk[i], preferred_element_type=F32) for x, (i, _) in zip(ds, heads)]
                    dk = [lax.dot_general(x, a, _TN, preferred_element_type=F32) for x, a in zip(ds, qm)]
                    dv = [lax.dot_general(x, a, _TN, preferred_element_type=F32) for x, a in zip(prb, dom)]
                    for i in range(nu):
                        dq_ref[0, qrows[i], :] += jnp.where(lo, dq[2 * i], dq[2 * i + 1])
                        dkext[krows[i], :] += dk[2 * i] + dk[2 * i + 1]
                        dvext[krows[i], :] += dv[2 * i] + dv[2 * i + 1]
                    return carry

                lax.fori_loop(0, ATT_BLK // Q_BLOCK // ATT_UNROLL_BWD, group, 0)

        dk_ref[0] = dkext[0:ATT_BLK, :]
        dv_ref[0] = dvext[0:ATT_BLK, :]

    cur = pl.BlockSpec((1, ATT_BLK, LANES), lambda j, n: (j, jnp.minimum(n, nblk - 1), 0))
    done = pl.BlockSpec((1, ATT_BLK, LANES), lambda j, n: (j, jnp.maximum(n - 1, 0), 0))
    return pl.pallas_call(
        body, name="attn_bwd", grid=(AT_PAIRS, nblk + 1), compiler_params=_params("arbitrary", "arbitrary"),
        in_specs=[cur] * 6, out_specs=(cur, done, done),
        out_shape=(SDS((AT_PAIRS, s, LANES), F32),) * 3,
        scratch_shapes=[pltpu.VMEM((2 * ATT_BLK, LANES), F32)] * 4 + [pltpu.VMEM((Q_BLOCK, 2 * Q_BLOCK), F32)],
    )(qr, kr, vv, do, lse, delta)


def _rope_bwd(dq, dk, dv, cos_t, sin_t, ts):
    s = cos_t.shape[0]

    def body(q_ref, k_ref, v_ref, cos_ref, sin_ref, oq, ok, ov):
        cs, sn = cos_ref[...], sin_ref[...]
        for j in range(AT_PAIRS):
            cols = slice(j * LANES, (j + 1) * LANES)
            for g_ref, o_ref in ((q_ref, oq), (k_ref, ok)):
                g = g_ref[j]
                o_ref[:, cols] = _bf(g * cs + _swap_half64(g * sn))
            ov[:, cols] = _bf(v_ref[j])

    tok = lambda w: pl.BlockSpec((ts, w), lambda i: (i, 0))
    pairs = pl.BlockSpec((AT_PAIRS, ts, LANES), lambda i: (0, i, 0))
    return pl.pallas_call(
        body, name="rope_bwd", grid=(s // ts,), compiler_params=_params("arbitrary"),
        in_specs=[pairs] * 3 + [tok(LANES)] * 2, out_specs=(tok(AT_WIDTH),) * 3,
        out_shape=(SDS((s, AT_WIDTH), BF16),) * 3,
    )(dq, dk, dv, cos_t, sin_t)


def _dn_scan_bwd(do, st, vn, w, qd, kd, p, gl, ts):
    s = do.shape[0]
    ncs = ts // CHUNK
    nt = s // ts

    def body(do_ref, st_ref, vn_ref, w_ref, qd_ref, kd_ref, p_ref, gl_ref,
             du_ref, dw_ref, dqd_ref, dkd_ref, dp_ref, dgl_ref, dstate):
        @pl.when(pl.program_id(0) == 0)
        def _():
            dstate[...] = jnp.zeros_like(dstate)

        def chunk(jr, carry):
            ci = ncs - 1 - jr
            rows = pl.ds(pl.multiple_of(ci * CHUNK, CHUNK), CHUNK)
            rows8 = pl.ds(pl.multiple_of(ci * 8, 8), 8)
            srows = pl.ds(pl.multiple_of(ci * DN_DIM, DN_DIM), DN_DIM)
            hs = range(DN_HEADS)
            sl = [slice(h * DN_DIM, (h + 1) * DN_DIM) for h in hs]
            ds_ = [dstate[h] for h in hs]
            dsb = [_bf(x) for x in ds_]
            dob = [_bf(do_ref[rows, c]) for c in sl]
            pdo = [_tn(p_ref[h, rows, :], b) for h, b in zip(hs, dob)]
            qdo = [_tn(qd_ref[rows, c], b) for c, b in zip(sl, dob)]
            dvn = [_nn(kd_ref[rows, c], b) + x for c, b, x in zip(sl, dsb, pdo)]
            dvb = [_bf(x) for x in dvn]
            wdv = [_tn(w_ref[rows, c], b) for c, b in zip(sl, dvb)]
            for h in hs:
                dstate[h] = ds_[h] * gl_ref[rows8, sl[h]][0:1] + qdo[h] - wdv[h]
            sfs = [st_ref[srows, c] for c in sl]
            sbs = [_bf(x) for x in sfs]
            vnb = [_bf(vn_ref[rows, c]) for c in sl]
            for h in hs:
                du_ref[rows, sl[h]] = dvn[h]
                dw_ref[rows, sl[h]] = -_nt(dvb[h], sbs[h])
                dqd_ref[rows, sl[h]] = _nt(dob[h], sbs[h])
                dkd_ref[rows, sl[h]] = _nt(vnb[h], dsb[h])
                dp_ref[h, rows, :] = _nt(dob[h], vnb[h])
                dgl = jnp.sum(jnp.sum(ds_[h] * sfs[h], axis=1, keepdims=True), axis=0, keepdims=True)
                dgl_ref[rows8, sl[h]] = jnp.broadcast_to(dgl, (8, DN_DIM))
            return carry

        lax.fori_loop(0, ncs, chunk, 0)

    tok = lambda wd: pl.BlockSpec((ts, wd), lambda i: (nt - 1 - i, 0))
    pspec = pl.BlockSpec((DN_HEADS, ts, CHUNK), lambda i: (0, nt - 1 - i, 0))
    g8 = pl.BlockSpec((ncs * 8, DN_WIDTH), lambda i: (nt - 1 - i, 0))
    return pl.pallas_call(
        body, name="dn_scan_bwd", grid=(nt,), compiler_params=_params("arbitrary"),
        in_specs=[tok(DN_WIDTH), pl.BlockSpec((ncs * DN_DIM, DN_WIDTH), lambda i: (nt - 1 - i, 0))]
        + [tok(DN_WIDTH)] * 4 + [pspec, g8],
        out_specs=(tok(DN_WIDTH),) * 4 + (pspec, g8),
        out_shape=(SDS((s, DN_WIDTH), F32),) * 4 + (SDS((DN_HEADS, s, CHUNK), F32),
                                                     SDS((s // CHUNK * 8, DN_WIDTH), F32)),
        scratch_shapes=[pltpu.VMEM((DN_HEADS, DN_DIM, DN_DIM), F32)],
    )(do, st, vn, w, qd, kd, p, gl)


def _dn_chunk_bwd(q, k, v, bg, t, du, dw, dqd, dkd, dp, dgl, ts):
    s = q.shape[0]
    ncs = ts // CHUNK

    def body(q_ref, k_ref, v_ref, bg_ref, t_ref, du_ref, dw_ref, dqd_ref, dkd_ref, dp_ref, dgl_ref,
             dq_ref, dk_ref, dv_ref, dbg_ref):
        def chunks(cg, carry):
            lane = lax.broadcasted_iota(jnp.int32, (CHUNK, BA_PAD), 1)
            where = []
            for ci in (cg * CH_UNROLL + i for i in range(CH_UNROLL)):
                rows = pl.ds(pl.multiple_of(ci * CHUNK, CHUNK), CHUNK)
                rows8 = pl.ds(pl.multiple_of(ci * 8, 8), 8)
                where += [(rows, rows8, h, slice(h * DN_DIM, (h + 1) * DN_DIM)) for h in range(DN_HEADS)]
            bgs = [bg_ref[rows, :] for rows, _, _, _ in where]
            cots = [(du_ref[rows, c], dw_ref[rows, c], dp_ref[h, rows, :], dqd_ref[rows, c], dkd_ref[rows, c],
                     dgl_ref[rows8, c][0:1, 0:1]) for rows, rows8, h, c in where]
            outs = _chunk_bwd([q_ref[rows, c] for rows, _, _, c in where], [k_ref[rows, c] for rows, _, _, c in where],
                              [v_ref[rows, c] for rows, _, _, c in where],
                              [b[:, h:h + 1] for b, (_, _, h, _) in zip(bgs, where)],
                              [b[:, GC_LANE + h:GC_LANE + h + 1] for b, (_, _, h, _) in zip(bgs, where)],
                              [t_ref[h, rows, :] for rows, _, h, _ in where], cots)
            for i in range(CH_UNROLL):
                dbg = jnp.zeros((CHUNK, BA_PAD), F32)
                for (rows, _, h, c), (dq, dk, dv, dbeta, dgc) in list(zip(where, outs))[i * DN_HEADS:(i + 1) * DN_HEADS]:
                    dq_ref[rows, c] = dq
                    dk_ref[rows, c] = dk
                    dv_ref[rows, c] = dv
                    dbg = dbg + jnp.where(lane == h, dbeta, 0.0) + jnp.where(lane == GC_LANE + h, dgc, 0.0)
                dbg_ref[where[i * DN_HEADS][0], :] = dbg
            return carry

        lax.fori_loop(0, ncs // CH_UNROLL, chunks, 0)

    tok = lambda wd: pl.BlockSpec((ts, wd), lambda i: (i, 0))
    pspec = pl.BlockSpec((DN_HEADS, ts, CHUNK), lambda i: (0, i, 0))
    g8 = pl.BlockSpec((ncs * 8, DN_WIDTH), lambda i: (i, 0))
    return pl.pallas_call(
        body, name="dn_chunk_bwd", grid=(s // ts,), compiler_params=_params("arbitrary"),
        in_specs=[tok(DN_WIDTH)] * 3 + [tok(BA_PAD), pspec] + [tok(DN_WIDTH)] * 4 + [pspec, g8],
        out_specs=(tok(DN_WIDTH),) * 3 + (tok(BA_PAD),),
        out_shape=(SDS((s, DN_WIDTH), F32),) * 3 + (SDS((s, BA_PAD), F32),),
    )(q, k, v, bg, t, du, dw, dqd, dkd, dp, dgl)


def _dn_prep_bwd(qkv_pre, ba, dq, dk, dv, dbg, conv_w8, alog_row, dtb_row, ts):
    s = qkv_pre.shape[0]
    cw = 3 * DN_WIDTH
    nt = s // ts

    def body(pre_ref, ph_ref, nh_ref, ba_ref, dq_ref, dqh_ref, dk_ref, dkh_ref, dv_ref, dvh_ref, dbg_ref,
             cw_ref, al_ref, dtb_ref, dpre_ref, dba_ref, dcw_ref, dal_ref, ddtb_ref):
        n = pl.program_id(0)

        @pl.when(n == 0)
        def _():
            dcw_ref[...] = jnp.zeros_like(dcw_ref)
            dal_ref[...] = jnp.zeros_like(dal_ref)
            ddtb_ref[...] = jnp.zeros_like(ddtb_ref)

        last = n == nt - 1
        prev = jnp.where(n == 0, 0.0, ph_ref[...])
        ext = jnp.concatenate([prev, pre_ref[...], nh_ref[...]], axis=0)
        taps = _conv_taps(ext, ts + 8)
        conv = taps[0] * cw_ref[0:1, :]
        for j in range(1, CONV_K):
            conv = conv + taps[j] * cw_ref[j:j + 1, :]

        def cot(main, halo, cols):
            return jnp.concatenate([main[:, cols], jnp.where(last, 0.0, halo[:, cols])], axis=0)

        pieces = []
        for grp, (fn, mref, href) in enumerate(((_post_q, dq_ref, dqh_ref), (_post_k, dk_ref, dkh_ref),
                                                (_post_v, dv_ref, dvh_ref))):
            for h in range(DN_HEADS):
                cols = slice(h * DN_DIM, (h + 1) * DN_DIM)
                c0 = grp * DN_WIDTH + h * DN_DIM
                _, vjp = jax.vjp(fn, conv[:, c0:c0 + DN_DIM])
                pieces.append(vjp(cot(mref, href, cols))[0])
        dconv = jnp.concatenate(pieces, axis=1)
        rows = ts + 8
        dpre = dconv[:ts] * cw_ref[CONV_K - 1:CONV_K, :]
        for j in range(CONV_K - 1):
            sh = CONV_K - 1 - j
            dpre = dpre + pltpu.roll(dconv, rows - sh, 0)[:ts] * cw_ref[j:j + 1, :]
        dpre_ref[...] = _bf(dpre)
        for j in range(CONV_K):
            dcw_ref[j:j + 1, :] += jnp.sum(dconv[:ts] * taps[j][:ts], axis=0, keepdims=True)

        dbg = dbg_ref[...]
        lane = lax.broadcasted_iota(jnp.int32, dbg.shape, 1)
        dg = pltpu.roll(_chunk_cumsum(dbg, reverse=True), BA_PAD - DN_HEADS, 1)
        cot_bg = jnp.where(lane < DN_HEADS, dbg, jnp.where(lane < GC_LANE, dg, 0.0))
        _, vjp = jax.vjp(_beta_decay, ba_ref[...], al_ref[...], dtb_ref[...])
        dba, dal, ddtb = vjp(cot_bg)
        dba_ref[...] = _bf(dba)
        dal_ref[...] += dal
        ddtb_ref[...] += ddtb

    tok = lambda w: pl.BlockSpec((ts, w), lambda i: (i, 0))
    full = lambda a: pl.BlockSpec(a.shape, lambda i: (0, 0))
    prevh = lambda w: pl.BlockSpec((8, w), lambda i: (jnp.maximum(i * (ts // 8) - 1, 0), 0))
    nexth = lambda w: pl.BlockSpec((8, w), lambda i: (jnp.minimum((i + 1) * (ts // 8), s // 8 - 1), 0))
    row = pl.BlockSpec((1, LANES), lambda i: (0, 0))
    return pl.pallas_call(
        body, name="dn_prep_bwd", grid=(nt,), compiler_params=_params("arbitrary"),
        in_specs=[tok(cw), prevh(cw), nexth(cw), tok(BA_PAD),
                  tok(DN_WIDTH), nexth(DN_WIDTH), tok(DN_WIDTH), nexth(DN_WIDTH), tok(DN_WIDTH), nexth(DN_WIDTH),
                  tok(BA_PAD), full(conv_w8), full(alog_row), full(dtb_row)],
        out_specs=(tok(cw), tok(BA_PAD), pl.BlockSpec((8, cw), lambda i: (0, 0)), row, row),
        out_shape=(SDS((s, cw), BF16), SDS((s, BA_PAD), BF16), SDS((8, cw), F32), SDS((1, LANES), F32),
                   SDS((1, LANES), F32)),
    )(qkv_pre, qkv_pre, qkv_pre, ba, dq, dq, dk, dk, dv, dv, dbg, conv_w8, alog_row, dtb_row)


def _dh_dx(dps, ws, x, mod, norm_w, dx2, ts):
    s = x.shape[0]
    widths = [w.shape[1] for w in ws]
    np_ = len(ws)

    def body(*refs):
        dp_refs, w_refs = refs[:np_], refs[np_:2 * np_]
        x_ref, mod_ref, nw_ref, dx2_ref, gx_ref, dshift, dscale, dnw = refs[2 * np_:]

        @pl.when(pl.program_id(0) == 0)
        def _():
            dshift[...] = jnp.zeros_like(dshift)
            dscale[...] = jnp.zeros_like(dscale)
            dnw[...] = jnp.zeros_like(dnw)

        dh = lax.dot_general(dp_refs[0][...], w_refs[0][...], _NT, preferred_element_type=F32)
        for a, b in zip(dp_refs[1:], w_refs[1:]):
            dh = dh + lax.dot_general(a[...], b[...], _NT, preferred_element_type=F32)
        xt = x_ref[...]
        r = lax.rsqrt(jnp.mean(xt * xt, axis=-1, keepdims=True) + EPS)
        xn = xt * r
        nw = nw_ref[...]
        sc1 = 1.0 + mod_ref[:, D_MODEL:2 * D_MODEL]
        dshift[...] += jnp.sum(dh, axis=0, keepdims=True)
        dscale[...] += jnp.sum(dh * (xn * nw), axis=0, keepdims=True)
        dnw[...] += jnp.sum(dh * sc1 * xn, axis=0, keepdims=True)
        dxn = dh * sc1 * nw
        gx_ref[...] = r * (dxn - xn * jnp.mean(dxn * xn, axis=-1, keepdims=True)) + dx2_ref[...]

    tok = lambda w: pl.BlockSpec((ts, w), lambda i: (i, 0))
    full = lambda a: pl.BlockSpec(a.shape, lambda i: (0, 0))
    row = pl.BlockSpec((1, D_MODEL), lambda i: (0, 0))
    return pl.pallas_call(
        body, name="dh_dx", grid=(s // ts,), compiler_params=_params("arbitrary"),
        in_specs=[tok(w) for w in widths] + [full(w) for w in ws] + [tok(D_MODEL), full(mod), full(norm_w),
                                                                    tok(D_MODEL)],
        out_specs=(tok(D_MODEL), row, row, row),
        out_shape=(SDS((s, D_MODEL), F32),) + (SDS((1, D_MODEL), F32),) * 3,
    )(*dps, *ws, x, mod, norm_w, dx2)


def _grad_w_in(h, dps, ts, name):
    s = h.shape[0]
    widths = [p.shape[1] for p in dps]
    np_ = len(dps)

    def body(*refs):
        h_ref, dp_refs, outs = refs[0], refs[1:1 + np_], refs[1 + np_:]

        @pl.when(pl.program_id(0) == 0)
        def _():
            for o in outs:
                o[...] = jnp.zeros_like(o)

        hb = h_ref[...]
        for p, o in zip(dp_refs, outs):
            o[...] += lax.dot_general(hb, p[...], _TN, preferred_element_type=F32)

    tok = lambda w: pl.BlockSpec((ts, w), lambda i: (i, 0))
    return pl.pallas_call(
        body, name=name, grid=(s // ts,), compiler_params=_params("arbitrary"),
        in_specs=[tok(D_MODEL)] + [tok(w) for w in widths],
        out_specs=tuple(pl.BlockSpec((D_MODEL, w), lambda i: (0, 0)) for w in widths),
        out_shape=tuple(SDS((D_MODEL, w), F32) for w in widths),
    )(h, *dps)


def _adamw_math(w, g, m, v):
    m = ADAM_B1 * m + (1.0 - ADAM_B1) * g
    v = ADAM_B2 * v + (1.0 - ADAM_B2) * (g * g)
    m_hat = m / (1.0 - ADAM_B1 ** ADAM_STEP)
    v_hat = v / (1.0 - ADAM_B2 ** ADAM_STEP)
    delta = -ADAM_LR * (m_hat / (jnp.sqrt(v_hat) + ADAM_EPS) + ADAM_WD * w)
    return delta, m, v


def _adamw(w, m, v, g, name, slots=False):
    def body(w_ref, m_ref, v_ref, g_ref, g_out, d_out, m_out, v_out):
        if slots:
            g = g_ref[0].astype(F32)
            for k in range(1, N_DEV):
                g = g + g_ref[k].astype(F32)
        else:
            g = g_ref[...]
        g_out[...] = g
        d_out[...], m_out[...], v_out[...] = _adamw_math(w_ref[...], g, m_ref[...], v_ref[...])

    return pl.pallas_call(body, name=name, compiler_params=_params(),
                          out_shape=(SDS(w.shape, F32),) * 4)(w, m, v, g)


def _adamw_w_mod(w, m, v, siluc_all, dmod_mine):
    def body(w_ref, m_ref, v_ref, sc_ref, dm_ref, g_out, d_out, m_out, v_out):
        g = _htn(sc_ref[...], dm_ref[...])
        g_out[...] = g
        d_out[...], m_out[...], v_out[...] = _adamw_math(w_ref[...], g, m_ref[...], v_ref[...])

    return pl.pallas_call(body, name="adamw_w_mod", compiler_params=_params(),
                          out_shape=(SDS(w.shape, F32),) * 4)(w, m, v, siluc_all, dmod_mine)


def _pack_sum(pack_all):
    def body(p_ref, o_ref):
        t = p_ref[0]
        for k in range(1, N_DEV):
            t = t + p_ref[k]
        o_ref[...] = t

    return pl.pallas_call(body, name="pack_sum", out_shape=SDS(pack_all.shape[1:], F32))(pack_all)


def _tile(s, want):
    t = min(want, s)
    assert s % t == 0
    return t


def _local_step(x, c, positions, w_mod_bf, b_mod, norm_w, w_in_bf, conv_w, a_log, dt_bias, dn_norm_w, at_norm_w,
                w_out_bf, final_norm_w, tgt):
    s = x.shape[0]
    o = [0]
    for wdt in IN_SPLITS:
        o.append(o[-1] + wdt)
    w_ba = jnp.pad(w_in_bf[:, o[2]:o[4]], ((0, 0), (0, BA_PAD - 2 * DN_HEADS)))
    ws = [w_in_bf[:, o[0]:o[1]], w_in_bf[:, o[1]:o[2]], w_ba, w_in_bf[:, o[4]:o[5]], w_in_bf[:, o[5]:o[6]],
          w_in_bf[:, o[6]:o[7]], w_in_bf[:, o[7]:o[8]]]
    conv_w8 = jnp.pad(conv_w, ((0, 8 - CONV_K), (0, 0)))
    alog_row = jnp.pad(a_log, ((0, 0), (DN_HEADS, BA_PAD - 2 * DN_HEADS)))
    dtb_row = jnp.pad(dt_bias, ((0, 0), (DN_HEADS, BA_PAD - 2 * DN_HEADS)))
    atw2 = jnp.concatenate([at_norm_w, at_norm_w], axis=1)

    half = AT_DIM // 2
    lane = jnp.arange(LANES)
    inv_freq = ROPE_THETA ** (-(lane % half).astype(F32) / half)
    ang = positions.astype(F32)[:, None] * inv_freq
    cos_t = jnp.cos(ang)
    sin_t = jnp.sin(ang) * jnp.where((lane // half) % 2 == 0, -1.0, 1.0)

    mod, siluc = _adaln_mod(c, w_mod_bf, b_mod)
    gate = mod[:, 2 * D_MODEL:]
    hbf, qkv_pre, z_dn, ba, qr, kr, vb, z_at = _ln_proj(x, mod, norm_w, ws, cos_t, sin_t, _tile(s, 256))
    q, k, v, bg = _dn_prep(qkv_pre, ba, conv_w8, alog_row, dtb_row, _tile(s, 256))
    u, w, qd, kd, p, gl, tinv = _dn_chunk_prep(q, k, v, bg, _tile(s, 512))
    o_dn, vn, st = _dn_scan(u, w, qd, kd, p, gl, _tile(s, 512))
    o_at, lse = _attn_fwd(qr, kr, vb)
    dx2, dcat, gw_out, dfw, dgate, loss = _out_loss(o_dn, z_dn, o_at, z_at, dn_norm_w, atw2, x, tgt, w_out_bf, gate,
                                                    final_norm_w, _tile(s, 512))

    do_dn, dz_dn, do_at, dz_at, delta, ddnw, datw = _mix_bwd(dcat, o_dn, z_dn, o_at, z_at, dn_norm_w, atw2,
                                                             _tile(s, 512))
    daq, dak, dav = _rope_bwd(*_attn_bwd(qr, kr, vb, do_at, lse, delta), cos_t, sin_t, _tile(s, 512))
    du, dw, dqd, dkd, dp, dgl = _dn_scan_bwd(do_dn, st, vn, w, qd, kd, p, gl, _tile(s, 512))
    dq, dk, dv, dbg = _dn_chunk_bwd(q, k, v, bg, tinv, du, dw, dqd, dkd, dp, dgl, _tile(s, 512))
    dqkv, dba, dcw, dal, ddtb = _dn_prep_bwd(qkv_pre, ba, dq, dk, dv, dbg, conv_w8, alog_row, dtb_row, _tile(s, 256))
    dps = [dqkv, dz_dn, dba, daq, dak, dav, dz_at]
    gx, dshift, dscale, dnw = _dh_dx(dps, ws, x, mod, norm_w, dx2, _tile(s, 256))
    g_qkv, g_z, g_ba = _grad_w_in(hbf, dps[:3], _tile(s, 512), "grad_w_in_dn")
    g_aq, g_ak, g_av, g_az = _grad_w_in(hbf, dps[3:], _tile(s, 512), "grad_w_in_at")
    gw_in = jnp.concatenate([g_qkv, g_z, g_ba[:, :2 * DN_HEADS], g_aq, g_ak, g_av, g_az], axis=1)
    dmod = jnp.concatenate([dshift, dscale, dgate], axis=1)
    small = dict(conv=dcw[:CONV_K], dmod=dmod, siluc=siluc, dnw=dnw, dfw=dfw, alog=dal, dtb=ddtb, dnn=ddnw, atn=datw)
    return loss, gx, gw_in, gw_out, small


def kernel(x, c, positions, w_mod, b_mod, norm_w, w_in, conv_w, a_log, dt_bias, dn_norm_w, at_norm_w, w_out, final_norm_w, loss_target, m_w_mod, m_b_mod, m_norm_w, m_w_in, m_conv_w, m_a_log, m_dt_bias, m_dn_norm_w, m_at_norm_w, m_w_out, m_final_norm_w, v_w_mod, v_b_mod, v_norm_w, v_w_in, v_conv_w, v_a_log, v_dt_bias, v_dn_norm_w, v_at_norm_w, v_w_out, v_final_norm_w):
    me = 4 * lax.axis_index("x") + 2 * lax.axis_index("y") + lax.axis_index("c")
    s = x.shape[1]

    g_mod, g_in, g_conv, g_out = _all_gather(
        [_bf(w_mod[0]), _bf(w_in[0]), conv_w[0], _bf(w_out[0])], "gather_weights")
    w_mod_bf = g_mod.transpose(1, 0, 2).reshape(D_MODEL, 3 * D_MODEL)
    w_in_bf = g_in.transpose(1, 0, 2).reshape(D_MODEL, IN_COLS)
    conv_full = g_conv.transpose(1, 0, 2).reshape(CONV_K, 3 * DN_WIDTH)
    w_out_bf = g_out.reshape(D_MODEL, D_MODEL)

    loss, gx, gw_in, gw_out, small = _local_step(
        x[0], c, positions[0], w_mod_bf, b_mod, norm_w, w_in_bf, conv_full, a_log, dt_bias, dn_norm_w, at_norm_w,
        w_out_bf, final_norm_w.reshape(1, D_MODEL), loss_target[0])

    pack = jnp.concatenate([small["conv"].reshape(1, -1), small["dmod"], small["siluc"], small["dnw"], small["dfw"],
                            small["alog"], small["dtb"], small["dnn"], small["atn"],
                            jnp.pad(loss, ((0, 0), (0, LANES - 1)))], axis=1).reshape(PK_ROWS, LANES)
    gw_in_slabs = _bf(gw_in).reshape(D_MODEL, N_DEV, IN_SHARD).transpose(1, 0, 2)
    gw_out_slabs = _bf(gw_out).reshape(N_DEV, D_MODEL // N_DEV, D_MODEL)
    r_in, r_out, pack_all = _exchange([gw_in_slabs, gw_out_slabs, pack], [True, True, False], "exchange_grads")

    res = {}
    res["w_in"] = _adamw(w_in[0], m_w_in[0], v_w_in[0], r_in, "adamw_w_in", slots=True)
    res["w_out"] = _adamw(w_out[0], m_w_out[0], v_w_out[0], r_out, "adamw_w_out", slots=True)
    flat_all = pack_all.reshape(N_DEV, PK_END)
    dmod_mine = lax.dynamic_slice(flat_all, (0, PK_DMOD + me * (3 * D_MODEL // N_DEV)), (N_DEV, 3 * D_MODEL // N_DEV))
    res["w_mod"] = _adamw_w_mod(w_mod[0], m_w_mod[0], v_w_mod[0], flat_all[:, PK_SILUC:PK_DNW], dmod_mine)
    tot = _pack_sum(pack_all).reshape(1, PK_END)
    g_conv_full = tot[:, PK_CONV:PK_DMOD].reshape(CONV_K, 3 * DN_WIDTH)
    g_conv_mine = lax.dynamic_slice(g_conv_full, (0, me * (3 * DN_WIDTH // N_DEV)), (CONV_K, 3 * DN_WIDTH // N_DEV))
    res["conv_w"] = _adamw(conv_w[0], m_conv_w[0], v_conv_w[0], g_conv_mine, "adamw_conv_w")
    res["b_mod"] = _adamw(b_mod, m_b_mod, v_b_mod, tot[:, PK_DMOD:PK_SILUC], "adamw_b_mod")
    res["norm_w"] = _adamw(norm_w, m_norm_w, v_norm_w, tot[:, PK_DNW:PK_DFW], "adamw_norm_w")
    res["a_log"] = _adamw(a_log, m_a_log, v_a_log, tot[:, PK_ALOG + DN_HEADS:PK_ALOG + 2 * DN_HEADS], "adamw_a_log")
    res["dt_bias"] = _adamw(dt_bias, m_dt_bias, v_dt_bias, tot[:, PK_DTB + DN_HEADS:PK_DTB + 2 * DN_HEADS],
                            "adamw_dt_bias")
    res["dn_norm_w"] = _adamw(dn_norm_w, m_dn_norm_w, v_dn_norm_w, tot[:, PK_DNN:PK_ATN], "adamw_dn_norm_w")
    g_atn = tot[:, PK_ATN:PK_ATN + AT_DIM] + tot[:, PK_ATN + AT_DIM:PK_LOSS]
    res["at_norm_w"] = _adamw(at_norm_w, m_at_norm_w, v_at_norm_w, g_atn, "adamw_at_norm_w")
    fin = _adamw(final_norm_w.reshape(1, D_MODEL), m_final_norm_w.reshape(1, D_MODEL),
                 v_final_norm_w.reshape(1, D_MODEL), tot[:, PK_DFW:PK_ALOG], "adamw_final_norm_w")
    res["final_norm_w"] = tuple(a.reshape(D_MODEL) for a in fin)

    lead = ("w_mod", "w_in", "conv_w", "w_out")
    names = ("w_mod", "b_mod", "norm_w", "w_in", "conv_w", "a_log", "dt_bias", "dn_norm_w", "at_norm_w", "w_out",
             "final_norm_w")
    out = [tot[0, PK_LOSS], gx.reshape(1, s, D_MODEL)]
    for kind in range(4):
        for nm in names:
            a = res[nm][kind]
            out.append(a[None] if nm in lead else a)
    return tuple(out)
```

```python
import functools

import jax
import jax.numpy as jnp
from jax import lax
from jax.experimental import pallas as pl
from jax.experimental.pallas import tpu as pltpu

F32, BF16 = jnp.float32, jnp.bfloat16
HI = lax.Precision.HIGHEST
SDS = jax.ShapeDtypeStruct

D_MODEL = 1024
DN_HEADS, DN_DIM, DN_WIDTH = 4, 128, 512
AT_HEADS, AT_DIM, AT_WIDTH = 8, 64, 512
CONV_K = 4
CHUNK = 64
Q_BLOCK = 128
W_SUB = 128
DILATIONS = (1, 4, 16)
AT_PAIRS = 4
ATT_BLK = Q_BLOCK * max(DILATIONS)
ATT_UNROLL, ATT_UNROLL_BWD = 8, 4
CH_UNROLL = 4
ROPE_THETA = 10000.0
EPS = 1e-6
N_DEV = 8
LANES = 128
BA_PAD = 128
IN_SPLITS = (1536, 512, 4, 4, 512, 512, 512, 512)
IN_COLS = sum(IN_SPLITS)
IN_SHARD = IN_COLS // N_DEV
VMEM_LIMIT = 56 * 2 ** 20

ADAM_LR, ADAM_B1, ADAM_B2, ADAM_EPS, ADAM_WD, ADAM_STEP = 0.001, 0.9, 0.999, 1e-08, 0.01, 10

PK_CONV, PK_DMOD, PK_SILUC, PK_DNW, PK_DFW, PK_ALOG, PK_DTB, PK_DNN, PK_ATN, PK_LOSS, PK_END = (
    0, 6144, 9216, 10240, 11264, 12288, 12416, 12544, 12672, 12800, 12928)
PK_ROWS = PK_END // LANES

_NT = (((1,), (1,)), ((), ()))
_TN = (((0,), (0,)), ((), ()))


def _params(*sem):
    return pltpu.CompilerParams(dimension_semantics=sem or None, vmem_limit_bytes=VMEM_LIMIT)


def _bf(x):
    return x.astype(BF16)


def _nn(a, b):
    return jnp.dot(_bf(a), _bf(b), preferred_element_type=F32)


def _nt(a, b):
    return lax.dot_general(_bf(a), _bf(b), _NT, preferred_element_type=F32)


def _tn(a, b):
    return lax.dot_general(_bf(a), _bf(b), _TN, preferred_element_type=F32)


def _htn(a, b):
    return lax.dot_general(a, b, _TN, precision=HI, preferred_element_type=F32)


def _head_sum(x):
    r = lax.broadcasted_iota(jnp.int32, (LANES, LANES), 0)
    c = lax.broadcasted_iota(jnp.int32, (LANES, LANES), 1)
    same = jnp.where((r // AT_DIM) == (c // AT_DIM), 1.0, 0.0).astype(BF16)
    hi, lo = _hl(x)
    return jnp.dot(hi, same, preferred_element_type=F32) + jnp.dot(lo, same, preferred_element_type=F32)


@jax.custom_vjp
def _d_head_sum(x):
    return _head_sum(x)


_d_head_sum.defvjp(lambda x: (_head_sum(x), None), lambda _, g: (_head_sum(g),))


def _silu(x):
    return x * jax.nn.sigmoid(x)


def _softplus(x):
    return jnp.maximum(x, 0.0) + jnp.log(1.0 + jnp.exp(-jnp.abs(x)))


def _l2n(x):
    return x * lax.rsqrt(jnp.sum(x * x, axis=-1, keepdims=True) + EPS)


def _post_q(x):
    return _l2n(_silu(x)) * (DN_DIM ** -0.5)


def _post_k(x):
    return _l2n(_silu(x))


def _post_v(x):
    return _silu(x)


def _beta_decay(ba, alog_row, dtb_row):
    lane = lax.broadcasted_iota(jnp.int32, ba.shape, 1)
    return jnp.where(lane < DN_HEADS, jax.nn.sigmoid(ba), -jnp.exp(alog_row) * _softplus(ba + dtb_row))


def _gate_dn(o, z, w):
    return (o * lax.rsqrt(jnp.mean(o * o, axis=-1, keepdims=True) + EPS)) * w * _silu(z)


def _gate_at(o, z, w2, head_sum):
    ms = head_sum(o * o) * (1.0 / AT_DIM)
    return (o * lax.rsqrt(ms + EPS)) * w2 * _silu(z)


def _swap_half64(x):
    lane = lax.broadcasted_iota(jnp.int32, x.shape, 1)
    return jnp.where((lane & (AT_DIM - 1)) < AT_DIM // 2, pltpu.roll(x, LANES - AT_DIM // 2, 1),
                     pltpu.roll(x, AT_DIM // 2, 1))


_NN = (((1,), (0,)), ((), ()))


def _hl(a):
    hi = a.astype(BF16)
    return hi, (a - hi.astype(F32)).astype(BF16)


def _mm3(a, b, dims=_NN):
    (ah, al), (bh, bl) = a, b
    f = lambda x, y: lax.dot_general(x, y, dims, preferred_element_type=F32)
    return f(ah, bh) + (f(ah, bl) + f(al, bh))


def _chunk_masks():
    r = lax.broadcasted_iota(jnp.int32, (CHUNK, CHUNK), 0)
    c = lax.broadcasted_iota(jnp.int32, (CHUNK, CHUNK), 1)
    return r >= c, r > c, (r == c).astype(F32), (r // 16) == (c // 16)


def _tri_inv(mats):
    _, _, eye, blk = _chunk_masks()
    dg = [jnp.where(blk, a, 0.0) for a in mats]
    lo = [jnp.where(blk, 0.0, a) for a in mats]
    sdg = [_hl(x) for x in dg]
    d2 = [_mm3(s, s) for s in sdg]
    sd2 = [_hl(x) for x in d2]
    d4 = [_mm3(s, s) for s in sd2]
    sd4 = [_hl(x) for x in d4]
    d8 = [_mm3(s, s) for s in sd4]
    p1 = [_mm3(_hl(eye - a), _hl(eye + b)) for a, b in zip(dg, d2)]
    p2 = [_mm3(_hl(a), _hl(eye + b)) for a, b in zip(p1, d4)]
    dinv = [_mm3(_hl(a), _hl(eye + b)) for a, b in zip(p2, d8)]
    sdinv = [_hl(x) for x in dinv]
    n1 = [_mm3(s, _hl(b)) for s, b in zip(sdinv, lo)]
    sn1 = [_hl(x) for x in n1]
    n2 = [_mm3(s, s) for s in sn1]
    q1 = [_mm3(_hl(eye - a), _hl(eye + b)) for a, b in zip(n1, n2)]
    return [_mm3(_hl(a), s) for a, s in zip(q1, sdinv)]


def _chunk_common(qs, ks, vs, betas, gcs):
    tril, _, _, _ = _chunk_masks()
    out = []
    for q, k, v, beta, gc in zip(qs, ks, vs, betas, gcs):
        gb = jnp.broadcast_to(gc, (CHUNK, DN_DIM))
        gt = gb.T[:CHUNK, :]
        gam = jnp.where(tril, jnp.exp(jnp.where(tril, gb[:, :CHUNK] - gt, 0.0)), 0.0)
        last = gb[CHUNK - 1:CHUNK, :]
        eg, e2 = jnp.exp(gb), jnp.exp(last - gb)
        kb, vb = k * beta, v * beta
        out.append(dict(gam=gam, eg=eg, e2=e2, gl=jnp.exp(last[:, 0:1]), kb=kb, vb=vb, kbg=kb * eg,
                        m=_nt(kb, k), qk=_nt(q, k)))
    return out


def _chunk_fwd(qs, ks, vs, betas, gcs):
    tril, strict, _, _ = _chunk_masks()
    cm = _chunk_common(qs, ks, vs, betas, gcs)
    ts = _tri_inv([jnp.where(strict, c["m"] * c["gam"], 0.0) for c in cm])
    outs = []
    for q, k, c, t in zip(qs, ks, cm, ts):
        uw = _nn(t, jnp.concatenate([c["vb"], c["kbg"]], axis=1))
        p = jnp.where(tril, c["qk"] * c["gam"], 0.0)
        outs.append((uw[:, :DN_DIM], uw[:, DN_DIM:], p, q * c["eg"], k * c["e2"], c["gl"], t.T))
    return outs


def _chunk_bwd(qs, ks, vs, betas, gcs, ts, cots):
    tril, strict, _, _ = _chunk_masks()
    cm = _chunk_common(qs, ks, vs, betas, gcs)
    row = lax.broadcasted_iota(jnp.int32, (CHUNK, 1), 0)
    ones = jnp.ones((CHUNK, DN_DIM), BF16)
    rs = lambda x: jnp.sum(x, axis=-1, keepdims=True)
    tts = [_bf(t) for t in ts]
    duw = [_bf(jnp.concatenate([ct[0], ct[1]], axis=1)) for ct in cots]
    dts = [_nt(a, jnp.concatenate([c["vb"], c["kbg"]], axis=1)) for a, c in zip(duw, cm)]
    xs = [_nn(t, d) for t, d in zip(tts, dts)]
    das = [jnp.where(strict, -_nn(x, t), 0.0) for x, t in zip(xs, tts)]
    dvks = [_nn(t, a) for t, a in zip(tts, duw)]
    outs = []
    for q, k, v, beta, c, ct, da, dvk in zip(qs, ks, vs, betas, cm, cots, das, dvks):
        _, _, dp, dqd, dkd, dgl = ct
        dvb, dkbg = dvk[:, :DN_DIM], dvk[:, DN_DIM:]
        dm = da * c["gam"]
        dqk = jnp.where(tril, dp, 0.0) * c["gam"]
        e = dm * c["m"] + dqk * c["qk"]
        dmq = jnp.concatenate([dm, dqk], axis=0)
        r1 = _nn(dmq, k)
        dkb = r1[:CHUNK] + dkbg * c["eg"]
        dq = r1[CHUNK:] + dqd * c["eg"]
        dk = _tn(dmq, jnp.concatenate([c["kb"], q], axis=0)) + dkd * c["e2"] + dkb * beta
        dbeta = rs(dkb * k + dvb * v)
        eh, el = _hl(e)
        colsum = (lax.dot_general(eh, ones, _TN, preferred_element_type=F32)
                  + lax.dot_general(el, ones, _TN, preferred_element_type=F32))[:, 0:1]
        pkd = dkd * (k * c["e2"])
        dgc = rs(e) - colsum + rs(dqd * q * c["eg"] + dkbg * c["kbg"] - pkd)
        tail = rs(jnp.sum(pkd, axis=0, keepdims=True)) + dgl * c["gl"]
        dgc = dgc + jnp.where(row == CHUNK - 1, tail, 0.0)
        outs.append((dq, dk, dvb * beta, dbeta, dgc))
    return outs


def _chunk_cumsum(x, reverse=False):
    n = x.shape[0]
    pos = lax.broadcasted_iota(jnp.int32, x.shape, 0) & (CHUNK - 1)
    sh = 1
    while sh < CHUNK:
        if reverse:
            x = x + jnp.where(pos < CHUNK - sh, pltpu.roll(x, n - sh, 0), 0.0)
        else:
            x = x + jnp.where(pos >= sh, pltpu.roll(x, sh, 0), 0.0)
        sh *= 2
    return x


GC_LANE = 2 * DN_HEADS


def _exchange(arrays, scatter, name):
    n = len(arrays)
    out_shapes = []
    for a, sc in zip(arrays, scatter):
        out_shapes.append(SDS(a.shape if sc else (N_DEV,) + a.shape, a.dtype))

    def body(*refs):
        ins, outs = refs[:n], refs[n:2 * n]
        send_sems, recv_sems, loc_sems = refs[2 * n:]
        x, y, c = lax.axis_index("x"), lax.axis_index("y"), lax.axis_index("c")
        me = 4 * x + 2 * y + c
        local, remote = [], []
        for i in range(n):
            src = ins[i].at[me] if scatter[i] else ins[i]
            cp = pltpu.make_async_copy(src, outs[i].at[me], loc_sems.at[i])
            cp.start()
            local.append(cp)
        for dlt in range(1, N_DEV):
            px = 1 - x if dlt & 4 else x
            py = 1 - y if dlt & 2 else y
            pc = 1 - c if dlt & 1 else c
            peer = 4 * px + 2 * py + pc
            for i in range(n):
                src = ins[i].at[peer] if scatter[i] else ins[i]
                cp = pltpu.make_async_remote_copy(
                    src_ref=src, dst_ref=outs[i].at[me],
                    send_sem=send_sems.at[i, dlt - 1], recv_sem=recv_sems.at[i, dlt - 1],
                    device_id=(px, py, pc), device_id_type=pl.DeviceIdType.MESH)
                cp.start()
                arrive = pltpu.make_async_remote_copy(
                    src_ref=src, dst_ref=outs[i].at[peer],
                    send_sem=send_sems.at[i, dlt - 1], recv_sem=recv_sems.at[i, dlt - 1],
                    device_id=(px, py, pc), device_id_type=pl.DeviceIdType.MESH)
                remote.append((cp, arrive))
        for cp, arrive in remote:
            cp.wait_send()
            arrive.wait_recv()
        for cp in local:
            cp.wait()

    any_spec = pl.BlockSpec(memory_space=pl.ANY)
    return pl.pallas_call(
        body, name=name, out_shape=tuple(out_shapes),
        in_specs=[any_spec] * n, out_specs=tuple([any_spec] * n),
        scratch_shapes=[pltpu.SemaphoreType.DMA((n, N_DEV - 1)), pltpu.SemaphoreType.DMA((n, N_DEV - 1)),
                        pltpu.SemaphoreType.DMA((n,))],
    )(*arrays)


def _all_gather(arrays, name):
    n = len(arrays)

    def body(*refs):
        ins, outs = refs[:n], refs[n:2 * n]
        send_sems, recv_sems, loc_sems = refs[2 * n:]
        x, y, c = lax.axis_index("x"), lax.axis_index("y"), lax.axis_index("c")
        me, sibling = (x, y, c), (x, y, 1 - c)
        chips = [(1 - x, y), (x, 1 - y), (1 - x, 1 - y)]

        def copy(i, k, block, to, src=None):
            slot = outs[i].at[4 * block[0] + 2 * block[1] + block[2]]
            return pltpu.make_async_remote_copy(
                src_ref=slot if src is None else src, dst_ref=slot,
                send_sem=send_sems.at[i, k], recv_sem=recv_sems.at[i, k],
                device_id=to, device_id_type=pl.DeviceIdType.MESH)

        mine = [pltpu.make_async_copy(ins[i], outs[i].at[4 * x + 2 * y + c], loc_sems.at[i]) for i in range(n)]
        for cp in mine:
            cp.start()
        first = []
        for i in range(n):
            first.append(copy(i, 0, me, sibling, src=ins[i]))
            first += [copy(i, 1 + j, me, (*chip, c), src=ins[i]) for j, chip in enumerate(chips)]
        for cp in first:
            cp.start()
        passed = []
        for j, chip in enumerate(chips):
            for i in range(n):
                copy(i, 1 + j, (*chip, c), me).wait_recv()
                fwd = copy(i, 4 + j, (*chip, c), sibling)
                fwd.start()
                passed.append(fwd)
        for i in range(n):
            copy(i, 0, sibling, me).wait_recv()
        for j, chip in enumerate(chips):
            for i in range(n):
                copy(i, 4 + j, (*chip, 1 - c), me).wait_recv()
        for cp in first + passed:
            cp.wait_send()
        for cp in mine:
            cp.wait()

    any_spec = pl.BlockSpec(memory_space=pl.ANY)
    return pl.pallas_call(
        body, name=name, out_shape=tuple(SDS((N_DEV,) + a.shape, a.dtype) for a in arrays),
        in_specs=[any_spec] * n, out_specs=tuple([any_spec] * n),
        scratch_shapes=[pltpu.SemaphoreType.DMA((n, N_DEV - 1)), pltpu.SemaphoreType.DMA((n, N_DEV - 1)),
                        pltpu.SemaphoreType.DMA((n,))],
    )(*arrays)


def _adaln_mod(c, w_mod, b_mod):
    def body(c_ref, w_ref, b_ref, mod_ref, sc_ref):
        sc = _silu(c_ref[...])
        sc8 = jnp.broadcast_to(sc, (8, D_MODEL))
        mod_ref[...] = _nn(sc8, w_ref[...])[0:1] + b_ref[...]
        sc_ref[...] = sc

    return pl.pallas_call(body, name="adaln_mod", compiler_params=_params(),
                          out_shape=(SDS((1, 3 * D_MODEL), F32), SDS((1, D_MODEL), F32)))(c, w_mod, b_mod)


def _ln_proj(x, mod, norm_w, ws, cos_t, sin_t, ts):
    s = x.shape[0]
    widths = [w.shape[1] for w in ws]

    def body(x_ref, mod_ref, nw_ref, cos_ref, sin_ref, wqkv, wz, wba, waq, wak, wav, waz,
             h_ref, oqkv, oz, oba, oq, ok, ov, oaz):
        xt = x_ref[...]
        r = lax.rsqrt(jnp.mean(xt * xt, axis=-1, keepdims=True) + EPS)
        shift, scale = mod_ref[:, 0:D_MODEL], mod_ref[:, D_MODEL:2 * D_MODEL]
        h = ((xt * r) * nw_ref[...]) * (1.0 + scale) + shift
        hb = _bf(h)
        h_ref[...] = hb
        oqkv[...] = jnp.dot(hb, wqkv[...], preferred_element_type=F32)
        oz[...] = jnp.dot(hb, wz[...], preferred_element_type=F32)
        oba[...] = jnp.dot(hb, wba[...], preferred_element_type=F32)
        oaz[...] = jnp.dot(hb, waz[...], preferred_element_type=F32)
        tv = jnp.dot(hb, wav[...], preferred_element_type=F32)
        for j in range(AT_PAIRS):
            ov[j] = tv[:, j * LANES:(j + 1) * LANES]
        cs, sn = cos_ref[...], sin_ref[...]
        for w_ref, o_ref in ((waq, oq), (wak, ok)):
            t = jnp.dot(hb, w_ref[...], preferred_element_type=F32)
            for j in range(AT_PAIRS):
                tj = t[:, j * LANES:(j + 1) * LANES]
                o_ref[j] = tj * cs + _swap_half64(tj) * sn

    tok = lambda w: pl.BlockSpec((ts, w), lambda i: (i, 0))
    full = lambda a: pl.BlockSpec(a.shape, lambda i: (0, 0))
    pairs = pl.BlockSpec((AT_PAIRS, ts, LANES), lambda i: (0, i, 0))
    return pl.pallas_call(
        body, name="ln_proj", grid=(s // ts,), compiler_params=_params("arbitrary"),
        in_specs=[tok(D_MODEL), full(mod), full(norm_w), tok(LANES), tok(LANES)] + [full(w) for w in ws],
        out_specs=(tok(D_MODEL), tok(widths[0]), tok(widths[1]), tok(widths[2]), pairs, pairs, pairs,
                   tok(widths[6])),
        out_shape=(SDS((s, D_MODEL), BF16), SDS((s, widths[0]), F32), SDS((s, widths[1]), F32),
                   SDS((s, widths[2]), F32)) + (SDS((AT_PAIRS, s, LANES), F32),) * 3 + (SDS((s, widths[6]), F32),),
    )(x, mod, norm_w, cos_t, sin_t, *ws)


def _conv_taps(ext, rows):
    taps = []
    for j in range(CONV_K):
        sh = CONV_K - 1 - j
        rolled = pltpu.roll(ext, sh, 0) if sh else ext
        taps.append(rolled[8:8 + rows])
    return taps


def _dn_prep(qkv_pre, ba, conv_w8, alog_row, dtb_row, ts):
    s = qkv_pre.shape[0]
    cw = 3 * DN_WIDTH

    def body(pre_ref, halo_ref, ba_ref, cw_ref, al_ref, dtb_ref, q_ref, k_ref, v_ref, bg_ref):
        n = pl.program_id(0)
        prev = jnp.where(n == 0, 0.0, halo_ref[...])
        ext = jnp.concatenate([prev, pre_ref[...]], axis=0)
        taps = _conv_taps(ext, ts)
        conv = taps[0] * cw_ref[0:1, :]
        for j in range(1, CONV_K):
            conv = conv + taps[j] * cw_ref[j:j + 1, :]
        for h in range(DN_HEADS):
            cols = slice(h * DN_DIM, (h + 1) * DN_DIM)
            q_ref[:, cols] = _post_q(conv[:, h * DN_DIM:(h + 1) * DN_DIM])
            k_ref[:, cols] = _post_k(conv[:, DN_WIDTH + h * DN_DIM:DN_WIDTH + (h + 1) * DN_DIM])
            v_ref[:, cols] = _post_v(conv[:, 2 * DN_WIDTH + h * DN_DIM:2 * DN_WIDTH + (h + 1) * DN_DIM])
        bg = _beta_decay(ba_ref[...], al_ref[...], dtb_ref[...])
        lane = lax.broadcasted_iota(jnp.int32, bg.shape, 1)
        run = pltpu.roll(_chunk_cumsum(bg), DN_HEADS, 1)
        bg_ref[...] = jnp.where((lane >= GC_LANE) & (lane < GC_LANE + DN_HEADS), run, bg)

    tok = lambda w: pl.BlockSpec((ts, w), lambda i: (i, 0))
    full = lambda a: pl.BlockSpec(a.shape, lambda i: (0, 0))
    halo = pl.BlockSpec((8, cw), lambda i: (jnp.maximum(i * (ts // 8) - 1, 0), 0))
    return pl.pallas_call(
        body, name="dn_prep", grid=(s // ts,), compiler_params=_params("arbitrary"),
        in_specs=[tok(cw), halo, tok(BA_PAD), full(conv_w8), full(alog_row), full(dtb_row)],
        out_specs=(tok(DN_WIDTH), tok(DN_WIDTH), tok(DN_WIDTH), tok(BA_PAD)),
        out_shape=(SDS((s, DN_WIDTH), F32),) * 3 + (SDS((s, BA_PAD), F32),),
    )(qkv_pre, qkv_pre, ba, conv_w8, alog_row, dtb_row)


def _dn_chunk_prep(q, k, v, bg, ts):
    s = q.shape[0]
    ncs = ts // CHUNK

    def body(q_ref, k_ref, v_ref, bg_ref, u_ref, w_ref, qd_ref, kd_ref, p_ref, gl_ref, t_ref):
        def chunks(cg, carry):
            where = []
            for ci in (cg * CH_UNROLL + i for i in range(CH_UNROLL)):
                rows = pl.ds(pl.multiple_of(ci * CHUNK, CHUNK), CHUNK)
                rows8 = pl.ds(pl.multiple_of(ci * 8, 8), 8)
                where += [(rows, rows8, h, slice(h * DN_DIM, (h + 1) * DN_DIM)) for h in range(DN_HEADS)]
            bgs = [bg_ref[rows, :] for rows, _, _, _ in where]
            outs = _chunk_fwd([q_ref[rows, c] for rows, _, _, c in where], [k_ref[rows, c] for rows, _, _, c in where],
                              [v_ref[rows, c] for rows, _, _, c in where],
                              [b[:, h:h + 1] for b, (_, _, h, _) in zip(bgs, where)],
                              [b[:, GC_LANE + h:GC_LANE + h + 1] for b, (_, _, h, _) in zip(bgs, where)])
            for (rows, rows8, h, c), (u, w, p, qd, kd, gl, t) in zip(where, outs):
                u_ref[rows, c] = u
                w_ref[rows, c] = w
                qd_ref[rows, c] = qd
                kd_ref[rows, c] = kd
                p_ref[h, rows, :] = p
                t_ref[h, rows, :] = t
                gl_ref[rows8, c] = jnp.broadcast_to(gl, (8, DN_DIM))
            return carry

        lax.fori_loop(0, ncs // CH_UNROLL, chunks, 0)

    tok = lambda w: pl.BlockSpec((ts, w), lambda i: (i, 0))
    sq = pl.BlockSpec((DN_HEADS, ts, CHUNK), lambda i: (0, i, 0))
    return pl.pallas_call(
        body, name="dn_chunk_prep", grid=(s // ts,), compiler_params=_params("arbitrary"),
        in_specs=[tok(DN_WIDTH)] * 3 + [tok(BA_PAD)],
        out_specs=(tok(DN_WIDTH),) * 4 + (sq, pl.BlockSpec((ncs * 8, DN_WIDTH), lambda i: (i, 0)), sq),
        out_shape=(SDS((s, DN_WIDTH), F32),) * 4 + (SDS((DN_HEADS, s, CHUNK), F32),
                                                     SDS((s // CHUNK * 8, DN_WIDTH), F32),
                                                     SDS((DN_HEADS, s, CHUNK), F32)),
    )(q, k, v, bg)


def _dn_scan(u, w, qd, kd, p, gl, ts):
    s = u.shape[0]
    ncs = ts // CHUNK

    def body(u_ref, w_ref, qd_ref, kd_ref, p_ref, gl_ref, o_ref, vn_ref, st_ref, state):
        @pl.when(pl.program_id(0) == 0)
        def _():
            state[...] = jnp.zeros_like(state)

        def chunk(ci, carry):
            rows = pl.ds(pl.multiple_of(ci * CHUNK, CHUNK), CHUNK)
            rows8 = pl.ds(pl.multiple_of(ci * 8, 8), 8)
            srows = pl.ds(pl.multiple_of(ci * DN_DIM, DN_DIM), DN_DIM)
            hs = range(DN_HEADS)
            sl = [slice(h * DN_DIM, (h + 1) * DN_DIM) for h in hs]
            sf = [state[h] for h in hs]
            sb = [_bf(x) for x in sf]
            ws = [_nn(w_ref[rows, c], b) for c, b in zip(sl, sb)]
            qs = [_nn(qd_ref[rows, c], b) for c, b in zip(sl, sb)]
            vn = [u_ref[rows, c] - x for c, x in zip(sl, ws)]
            vb = [_bf(x) for x in vn]
            kv = [_tn(kd_ref[rows, c], b) for c, b in zip(sl, vb)]
            pv = [_nn(p_ref[h, rows, :], b) for h, b in zip(hs, vb)]
            for h in hs:
                state[h] = sf[h] * gl_ref[rows8, sl[h]][0:1] + kv[h]
            for h in hs:
                st_ref[srows, sl[h]] = sf[h]
                vn_ref[rows, sl[h]] = vn[h]
                o_ref[rows, sl[h]] = qs[h] + pv[h]
            return carry

        lax.fori_loop(0, ncs, chunk, 0)

    tok = lambda wd: pl.BlockSpec((ts, wd), lambda i: (i, 0))
    return pl.pallas_call(
        body, name="dn_scan", grid=(s // ts,), compiler_params=_params("arbitrary"),
        in_specs=[tok(DN_WIDTH)] * 4 + [pl.BlockSpec((DN_HEADS, ts, CHUNK), lambda i: (0, i, 0)),
                                        pl.BlockSpec((ncs * 8, DN_WIDTH), lambda i: (i, 0))],
        out_specs=(tok(DN_WIDTH), tok(DN_WIDTH), pl.BlockSpec((ncs * DN_DIM, DN_WIDTH), lambda i: (i, 0))),
        out_shape=(SDS((s, DN_WIDTH), F32), SDS((s, DN_WIDTH), F32), SDS((s // CHUNK * DN_DIM, DN_WIDTH), F32)),
        scratch_shapes=[pltpu.VMEM((DN_HEADS, DN_DIM, DN_DIM), F32)],
    )(u, w, qd, kd, p, gl)


LOG2E, LN2 = 1.4426950408889634, 0.6931471805599453
MASKED = -1e30


def _band_bias():
    qi = lax.broadcasted_iota(jnp.int32, (Q_BLOCK, 2 * Q_BLOCK), 0)
    kj = lax.broadcasted_iota(jnp.int32, (Q_BLOCK, 2 * Q_BLOCK), 1)
    rel = Q_BLOCK + qi - kj
    return jnp.where((rel >= 0) & (rel <= W_SUB), 0.0, MASKED)


def _first_bias(first):
    kj = lax.broadcasted_iota(jnp.int32, (1, 2 * Q_BLOCK), 1)
    return jnp.where((kj < Q_BLOCK) & first, MASKED, 0.0)


def _attn_combo(c, d):
    if d == 1:
        qs = pl.multiple_of(c * Q_BLOCK, Q_BLOCK)
        return qs, pl.multiple_of(ATT_BLK - Q_BLOCK + c * Q_BLOCK, Q_BLOCK), c == 0
    r, m = c % d, c // d
    qs = r + (d * Q_BLOCK) * m
    return qs, ATT_BLK + qs - d * Q_BLOCK, m == 0


def _rows(start, size, d):
    return pl.ds(pl.multiple_of(start, Q_BLOCK), size) if d == 1 else pl.ds(start, size, stride=d)


def _shift_in(ext, cur, n):
    @pl.when(n == 0)
    def _():
        ext[0:ATT_BLK, :] = jnp.zeros((ATT_BLK, LANES), F32)

    @pl.when(n > 0)
    def _():
        ext[0:ATT_BLK, :] = ext[ATT_BLK:2 * ATT_BLK, :]

    ext[ATT_BLK:2 * ATT_BLK, :] = cur


def _attn_fwd(qr, kr, vv):
    s = qr.shape[1]
    nblk = s // ATT_BLK
    scale = AT_DIM ** -0.5
    npat = len(DILATIONS)

    def body(q_ref, k_ref, v_ref, o_ref, lse_ref, kext, vext, o_p, l_p, bias_ref):
        n = pl.program_id(1)
        _shift_in(kext, k_ref[0], n)
        _shift_in(vext, v_ref[0], n)
        bias_ref[...] = _band_bias()
        lo = lax.broadcasted_iota(jnp.int32, (Q_BLOCK, LANES), 1) < AT_DIM
        for pi, d in enumerate(DILATIONS):
            def group(g, carry, pi=pi, d=d):
                cs = [_attn_combo(g * ATT_UNROLL + u, d) for u in range(ATT_UNROLL)]
                heads = [(i, sel) for i in range(ATT_UNROLL) for sel in (lo, ~lo)]
                band = bias_ref[...]
                bias = [band + _first_bias((n == 0) & m0) for _, _, m0 in cs]
                qb = [_bf(q_ref[0, _rows(qs, Q_BLOCK, d), :]) for qs, _, _ in cs]
                kk = [_bf(kext[_rows(ks, 2 * Q_BLOCK, d), :]) for _, ks, _ in cs]
                vb = [_bf(vext[_rows(ks, 2 * Q_BLOCK, d), :]) for _, ks, _ in cs]
                sc = [lax.dot_general(jnp.where(sel, qb[i], jnp.zeros_like(qb[i])), kk[i], _NT,
                                      preferred_element_type=F32) for i, sel in heads]
                sc = [x * (scale * LOG2E) + bias[i] for x, (i, _) in zip(sc, heads)]
                mx = [jnp.max(x, axis=-1, keepdims=True) for x in sc]
                pr = [jnp.exp2(x - m) for x, m in zip(sc, mx)]
                ls = [jnp.sum(x, axis=-1, keepdims=True) for x in pr]
                pv = [jnp.dot(_bf(x), vb[i], preferred_element_type=F32) for x, (i, _) in zip(pr, heads)]
                outs = [x / l for x, l in zip(pv, ls)]
                lses = [m * LN2 + jnp.log(l) for m, l in zip(mx, ls)]
                for i, (qs, _, _) in enumerate(cs):
                    o_p[pi, _rows(qs, Q_BLOCK, d), :] = jnp.where(lo, outs[2 * i], outs[2 * i + 1])
                    l_p[pi, _rows(qs, Q_BLOCK, d), :] = jnp.where(lo, lses[2 * i], lses[2 * i + 1])
                return carry

            lax.fori_loop(0, ATT_BLK // Q_BLOCK // ATT_UNROLL, group, 0)

        def merge(i, carry):
            rows = pl.ds(pl.multiple_of(i * 256, 256), 256)
            ls = [l_p[pi, rows, :] for pi in range(npat)]
            mx = jnp.maximum(jnp.maximum(ls[0], ls[1]), ls[2])
            es = [jnp.exp(l - mx) for l in ls]
            den = es[0] + es[1] + es[2]
            o_ref[0, rows, :] = (es[0] * o_p[0, rows, :] + es[1] * o_p[1, rows, :] + es[2] * o_p[2, rows, :]) / den
            lse_ref[0, rows, :] = mx + jnp.log(den)
            return carry

        lax.fori_loop(0, ATT_BLK // 256, merge, 0)

    blk = pl.BlockSpec((1, ATT_BLK, LANES), lambda j, n: (j, n, 0))
    return pl.pallas_call(
        body, name="attn_fwd", grid=(AT_PAIRS, nblk), compiler_params=_params("arbitrary", "arbitrary"),
        in_specs=[blk] * 3, out_specs=(blk, blk),
        out_shape=(SDS((AT_PAIRS, s, LANES), F32),) * 2,
        scratch_shapes=[pltpu.VMEM((2 * ATT_BLK, LANES), F32), pltpu.VMEM((2 * ATT_BLK, LANES), F32),
                        pltpu.VMEM((npat, ATT_BLK, LANES), F32), pltpu.VMEM((npat, ATT_BLK, LANES), F32),
                        pltpu.VMEM((Q_BLOCK, 2 * Q_BLOCK), F32)],
    )(qr, kr, vv)


def _out_loss(o_dn, z_dn, o_at, z_at, dnw, atw2, x, tgt, w_out, gate, fw, ts):
    s = x.shape[0]

    def body(odn, zdn, oat, zat, dnw_ref, atw_ref, x_ref, t_ref, w_ref, g_ref, fw_ref,
             dx2_ref, dcat_ref, gw_ref, dfw_ref, dgate_ref, loss_ref):
        @pl.when(pl.program_id(0) == 0)
        def _():
            gw_ref[...] = jnp.zeros_like(gw_ref)
            dfw_ref[...] = jnp.zeros_like(dfw_ref)
            dgate_ref[...] = jnp.zeros_like(dgate_ref)
            loss_ref[...] = jnp.zeros_like(loss_ref)

        parts = [_bf(_gate_dn(odn[:, h * DN_DIM:(h + 1) * DN_DIM], zdn[:, h * DN_DIM:(h + 1) * DN_DIM], dnw_ref[...]))
                 for h in range(DN_HEADS)]
        parts += [_bf(_gate_at(oat[j], zat[:, j * LANES:(j + 1) * LANES], atw_ref[...], _head_sum))
                  for j in range(AT_PAIRS)]
        catb = jnp.concatenate(parts, axis=1)
        wb = w_ref[...]
        gate, fwv = g_ref[...], fw_ref[...]
        mix = jnp.dot(catb, wb, preferred_element_type=F32)
        x2 = x_ref[...] + gate * mix
        r2 = lax.rsqrt(jnp.mean(x2 * x2, axis=-1, keepdims=True) + EPS)
        xn2 = x2 * r2
        err = xn2 * fwv - t_ref[...]
        row = jnp.sum(err * err, axis=-1, keepdims=True) * (1.0 / D_MODEL)
        loss_ref[...] += 0.5 * jnp.sum(row, axis=0, keepdims=True)
        dy = err * (1.0 / D_MODEL)
        dfw_ref[...] += jnp.sum(dy * xn2, axis=0, keepdims=True)
        dxn = dy * fwv
        dx2 = r2 * (dxn - xn2 * jnp.mean(dxn * xn2, axis=-1, keepdims=True))
        dx2_ref[...] = dx2
        dgate_ref[...] += jnp.sum(dx2 * mix, axis=0, keepdims=True)
        dmix = _bf(gate * dx2)
        dcat_ref[...] = lax.dot_general(dmix, wb, _NT, preferred_element_type=F32)
        gw_ref[...] += lax.dot_general(catb, dmix, _TN, preferred_element_type=F32)

    tok = lambda w: pl.BlockSpec((ts, w), lambda i: (i, 0))
    full = lambda a: pl.BlockSpec(a.shape, lambda i: (0, 0))
    row = pl.BlockSpec((1, D_MODEL), lambda i: (0, 0))
    pairs = pl.BlockSpec((AT_PAIRS, ts, LANES), lambda i: (0, i, 0))
    return pl.pallas_call(
        body, name="out_loss", grid=(s // ts,), compiler_params=_params("arbitrary"),
        in_specs=[tok(DN_WIDTH), tok(DN_WIDTH), pairs, tok(AT_WIDTH), full(dnw), full(atw2),
                  tok(D_MODEL), tok(D_MODEL), full(w_out), full(gate), full(fw)],
        out_specs=(tok(D_MODEL), tok(D_MODEL), pl.BlockSpec((D_MODEL, D_MODEL), lambda i: (0, 0)), row, row,
                   pl.BlockSpec((1, 1), lambda i: (0, 0))),
        out_shape=(SDS((s, D_MODEL), F32), SDS((s, D_MODEL), F32), SDS((D_MODEL, D_MODEL), F32),
                   SDS((1, D_MODEL), F32), SDS((1, D_MODEL), F32), SDS((1, 1), F32)),
    )(o_dn, z_dn, o_at, z_at, dnw, atw2, x, tgt, w_out, gate, fw)


def _mix_bwd(dcat, o_dn, z_dn, o_at, z_at, dnw, atw2, ts):
    s = dcat.shape[0]

    def body(dcat_ref, odn, zdn, oat, zat, dnw_ref, atw_ref, dodn, dzdn, doat, dzat, delta, ddnw, datw):
        @pl.when(pl.program_id(0) == 0)
        def _():
            ddnw[...] = jnp.zeros_like(ddnw)
            datw[...] = jnp.zeros_like(datw)

        for h in range(DN_HEADS):
            cols = slice(h * DN_DIM, (h + 1) * DN_DIM)
            _, vjp = jax.vjp(_gate_dn, odn[:, cols], zdn[:, cols], dnw_ref[...])
            do, dz, dw = vjp(dcat_ref[:, cols])
            dodn[:, cols] = do
            dzdn[:, cols] = _bf(dz)
            ddnw[...] += dw
        for j in range(AT_PAIRS):
            cols = slice(j * LANES, (j + 1) * LANES)
            o = oat[j]
            _, vjp = jax.vjp(functools.partial(_gate_at, head_sum=_d_head_sum), o, zat[:, cols], atw_ref[...])
            do, dz, dw = vjp(dcat_ref[:, DN_WIDTH + j * LANES:DN_WIDTH + (j + 1) * LANES])
            doat[j] = do
            dzat[:, cols] = _bf(dz)
            datw[...] += dw
            delta[j] = _head_sum(do * o)

    tok = lambda w: pl.BlockSpec((ts, w), lambda i: (i, 0))
    full = lambda a: pl.BlockSpec(a.shape, lambda i: (0, 0))
    row = pl.BlockSpec((1, LANES), lambda i: (0, 0))
    pairs = pl.BlockSpec((AT_PAIRS, ts, LANES), lambda i: (0, i, 0))
    return pl.pallas_call(
        body, name="mix_bwd", grid=(s // ts,), compiler_params=_params("arbitrary"),
        in_specs=[tok(D_MODEL), tok(DN_WIDTH), tok(DN_WIDTH), pairs, tok(AT_WIDTH), full(dnw), full(atw2)],
        out_specs=(tok(DN_WIDTH), tok(DN_WIDTH), pairs, tok(AT_WIDTH), pairs, row, row),
        out_shape=(SDS((s, DN_WIDTH), F32), SDS((s, DN_WIDTH), BF16), SDS((AT_PAIRS, s, LANES), F32),
                   SDS((s, AT_WIDTH), BF16), SDS((AT_PAIRS, s, LANES), F32), SDS((1, LANES), F32),
                   SDS((1, LANES), F32)),
    )(dcat, o_dn, z_dn, o_at, z_at, dnw, atw2)


def _shift_acc(ext, n):
    @pl.when(n == 0)
    def _():
        ext[0:ATT_BLK, :] = jnp.zeros((ATT_BLK, LANES), F32)

    @pl.when(n > 0)
    def _():
        ext[0:ATT_BLK, :] = ext[ATT_BLK:2 * ATT_BLK, :]

    ext[ATT_BLK:2 * ATT_BLK, :] = jnp.zeros((ATT_BLK, LANES), F32)


def _attn_bwd(qr, kr, vv, do, lse, delta):
    s = qr.shape[1]
    nblk = s // ATT_BLK
    scale = AT_DIM ** -0.5

    def body(q_ref, k_ref, v_ref, do_ref, lse_ref, dl_ref, dq_ref, dk_ref, dv_ref, kext, vext, dkext, dvext,
             bias_ref):
        n = pl.program_id(1)
        _shift_in(kext, k_ref[0], n)
        _shift_in(vext, v_ref[0], n)
        _shift_acc(dkext, n)
        _shift_acc(dvext, n)
        bias_ref[...] = _band_bias()

        @pl.when(n < nblk)
        def _():
            dq_ref[0] = jnp.zeros((ATT_BLK, LANES), F32)
            lo = lax.broadcasted_iota(jnp.int32, (Q_BLOCK, LANES), 1) < AT_DIM
            for d in DILATIONS:
                def group(g, carry, d=d):
                    nu = ATT_UNROLL_BWD
                    cs = [_attn_combo(g * nu + u, d) for u in range(nu)]
                    heads = [(i, sel) for i in range(nu) for sel in (lo, ~lo)]
                    qrows = [_rows(qs, Q_BLOCK, d) for qs, _, _ in cs]
                    krows = [_rows(ks, 2 * Q_BLOCK, d) for _, ks, _ in cs]
                    band = bias_ref[...]
                    bias = [band + _first_bias((n == 0) & m0) for _, _, m0 in cs]
                    qb = [_bf(q_ref[0, r, :]) for r in qrows]
                    dob = [_bf(do_ref[0, r, :]) for r in qrows]
                    kk = [_bf(kext[r, :]) for r in krows]
                    vb = [_bf(vext[r, :]) for r in krows]
                    lse2 = [lse_ref[0, r, :] * LOG2E for r in qrows]
                    dl2 = [dl_ref[0, r, :] for r in qrows]
                    qm = [jnp.where(sel, qb[i], jnp.zeros_like(qb[i])) for i, sel in heads]
                    dom = [jnp.where(sel, dob[i], jnp.zeros_like(dob[i])) for i, sel in heads]
                    lse_c = [jnp.max(jnp.where(sel, lse2[i], -jnp.inf), axis=-1, keepdims=True) for i, sel in heads]
                    dl_c = [jnp.max(jnp.where(sel, dl2[i], -jnp.inf), axis=-1, keepdims=True) for i, sel in heads]
                    sc = [lax.dot_general(a, kk[i], _NT, preferred_element_type=F32) for a, (i, _) in zip(qm, heads)]
                    dp = [lax.dot_general(a, vb[i], _NT, preferred_element_type=F32) for a, (i, _) in zip(dom, heads)]
                    pr = [jnp.exp2(x * (scale * LOG2E) + bias[i] - l) for x, l, (i, _) in zip(sc, lse_c, heads)]
                    ds = [_bf(p * (x - dl) * scale) for p, x, dl in zip(pr, dp, dl_c)]
                    prb = [_bf(p) for p in pr]
                    dq = [jnp.dot(x, kk[i], preferred_element_type=F32) for x, (i, _) in zip(ds, heads)]
                    dk = [lax.dot_general(x, a, _TN, preferred_element_type=F32) for x, a in zip(ds, qm)]
                    dv = [lax.dot_general(x, a, _TN, preferred_element_type=F32) for x, a in zip(prb, dom)]
                    for i in range(nu):
                        dq_ref[0, qrows[i], :] += jnp.where(lo, dq[2 * i], dq[2 * i + 1])
                        dkext[krows[i], :] += dk[2 * i] + dk[2 * i + 1]
                        dvext[krows[i], :] += dv[2 * i] + dv[2 * i + 1]
                    return carry

                lax.fori_loop(0, ATT_BLK // Q_BLOCK // ATT_UNROLL_BWD, group, 0)

        dk_ref[0] = dkext[0:ATT_BLK, :]
        dv_ref[0] = dvext[0:ATT_BLK, :]

    cur = pl.BlockSpec((1, ATT_BLK, LANES), lambda j, n: (j, jnp.minimum(n, nblk - 1), 0))
    done = pl.BlockSpec((1, ATT_BLK, LANES), lambda j, n: (j, jnp.maximum(n - 1, 0), 0))
    return pl.pallas_call(
        body, name="attn_bwd", grid=(AT_PAIRS, nblk + 1), compiler_params=_params("arbitrary", "arbitrary"),
        in_specs=[cur] * 6, out_specs=(cur, done, done),
        out_shape=(SDS((AT_PAIRS, s, LANES), F32),) * 3,
        scratch_shapes=[pltpu.VMEM((2 * ATT_BLK, LANES), F32)] * 4 + [pltpu.VMEM((Q_BLOCK, 2 * Q_BLOCK), F32)],
    )(qr, kr, vv, do, lse, delta)


def _rope_bwd(dq, dk, dv, cos_t, sin_t, ts):
    s = cos_t.shape[0]

    def body(q_ref, k_ref, v_ref, cos_ref, sin_ref, oq, ok, ov):
        cs, sn = cos_ref[...], sin_ref[...]
        for j in range(AT_PAIRS):
            cols = slice(j * LANES, (j + 1) * LANES)
            for g_ref, o_ref in ((q_ref, oq), (k_ref, ok)):
                g = g_ref[j]
                o_ref[:, cols] = _bf(g * cs + _swap_half64(g * sn))
            ov[:, cols] = _bf(v_ref[j])

    tok = lambda w: pl.BlockSpec((ts, w), lambda i: (i, 0))
    pairs = pl.BlockSpec((AT_PAIRS, ts, LANES), lambda i: (0, i, 0))
    return pl.pallas_call(
        body, name="rope_bwd", grid=(s // ts,), compiler_params=_params("arbitrary"),
        in_specs=[pairs] * 3 + [tok(LANES)] * 2, out_specs=(tok(AT_WIDTH),) * 3,
        out_shape=(SDS((s, AT_WIDTH), BF16),) * 3,
    )(dq, dk, dv, cos_t, sin_t)


def _dn_scan_bwd(do, st, vn, w, qd, kd, p, gl, ts):
    s = do.shape[0]
    ncs = ts // CHUNK
    nt = s // ts

    def body(do_ref, st_ref, vn_ref, w_ref, qd_ref, kd_ref, p_ref, gl_ref,
             du_ref, dw_ref, dqd_ref, dkd_ref, dp_ref, dgl_ref, dstate):
        @pl.when(pl.program_id(0) == 0)
        def _():
            dstate[...] = jnp.zeros_like(dstate)

        def chunk(jr, carry):
            ci = ncs - 1 - jr
            rows = pl.ds(pl.multiple_of(ci * CHUNK, CHUNK), CHUNK)
            rows8 = pl.ds(pl.multiple_of(ci * 8, 8), 8)
            srows = pl.ds(pl.multiple_of(ci * DN_DIM, DN_DIM), DN_DIM)
            hs = range(DN_HEADS)
            sl = [slice(h * DN_DIM, (h + 1) * DN_DIM) for h in hs]
            ds_ = [dstate[h] for h in hs]
            dsb = [_bf(x) for x in ds_]
            dob = [_bf(do_ref[rows, c]) for c in sl]
            pdo = [_tn(p_ref[h, rows, :], b) for h, b in zip(hs, dob)]
            qdo = [_tn(qd_ref[rows, c], b) for c, b in zip(sl, dob)]
            dvn = [_nn(kd_ref[rows, c], b) + x for c, b, x in zip(sl, dsb, pdo)]
            dvb = [_bf(x) for x in dvn]
            wdv = [_tn(w_ref[rows, c], b) for c, b in zip(sl, dvb)]
            for h in hs:
                dstate[h] = ds_[h] * gl_ref[rows8, sl[h]][0:1] + qdo[h] - wdv[h]
            sfs = [st_ref[srows, c] for c in sl]
            sbs = [_bf(x) for x in sfs]
            vnb = [_bf(vn_ref[rows, c]) for c in sl]
            for h in hs:
                du_ref[rows, sl[h]] = dvn[h]
                dw_ref[rows, sl[h]] = -_nt(dvb[h], sbs[h])
                dqd_ref[rows, sl[h]] = _nt(dob[h], sbs[h])
                dkd_ref[rows, sl[h]] = _nt(vnb[h], dsb[h])
                dp_ref[h, rows, :] = _nt(dob[h], vnb[h])
                dgl = jnp.sum(jnp.sum(ds_[h] * sfs[h], axis=1, keepdims=True), axis=0, keepdims=True)
                dgl_ref[rows8, sl[h]] = jnp.broadcast_to(dgl, (8, DN_DIM))
            return carry

        lax.fori_loop(0, ncs, chunk, 0)

    tok = lambda wd: pl.BlockSpec((ts, wd), lambda i: (nt - 1 - i, 0))
    pspec = pl.BlockSpec((DN_HEADS, ts, CHUNK), lambda i: (0, nt - 1 - i, 0))
    g8 = pl.BlockSpec((ncs * 8, DN_WIDTH), lambda i: (nt - 1 - i, 0))
    return pl.pallas_call(
        body, name="dn_scan_bwd", grid=(nt,), compiler_params=_params("arbitrary"),
        in_specs=[tok(DN_WIDTH), pl.BlockSpec((ncs * DN_DIM, DN_WIDTH), lambda i: (nt - 1 - i, 0))]
        + [tok(DN_WIDTH)] * 4 + [pspec, g8],
        out_specs=(tok(DN_WIDTH),) * 4 + (pspec, g8),
        out_shape=(SDS((s, DN_WIDTH), F32),) * 4 + (SDS((DN_HEADS, s, CHUNK), F32),
                                                     SDS((s // CHUNK * 8, DN_WIDTH), F32)),
        scratch_shapes=[pltpu.VMEM((DN_HEADS, DN_DIM, DN_DIM), F32)],
    )(do, st, vn, w, qd, kd, p, gl)


def _dn_chunk_bwd(q, k, v, bg, t, du, dw, dqd, dkd, dp, dgl, ts):
    s = q.shape[0]
    ncs = ts // CHUNK

    def body(q_ref, k_ref, v_ref, bg_ref, t_ref, du_ref, dw_ref, dqd_ref, dkd_ref, dp_ref, dgl_ref,
             dq_ref, dk_ref, dv_ref, dbg_ref):
        def chunks(cg, carry):
            lane = lax.broadcasted_iota(jnp.int32, (CHUNK, BA_PAD), 1)
            where = []
            for ci in (cg * CH_UNROLL + i for i in range(CH_UNROLL)):
                rows = pl.ds(pl.multiple_of(ci * CHUNK, CHUNK), CHUNK)
                rows8 = pl.ds(pl.multiple_of(ci * 8, 8), 8)
                where += [(rows, rows8, h, slice(h * DN_DIM, (h + 1) * DN_DIM)) for h in range(DN_HEADS)]
            bgs = [bg_ref[rows, :] for rows, _, _, _ in where]
            cots = [(du_ref[rows, c], dw_ref[rows, c], dp_ref[h, rows, :], dqd_ref[rows, c], dkd_ref[rows, c],
                     dgl_ref[rows8, c][0:1, 0:1]) for rows, rows8, h, c in where]
            outs = _chunk_bwd([q_ref[rows, c] for rows, _, _, c in where], [k_ref[rows, c] for rows, _, _, c in where],
                              [v_ref[rows, c] for rows, _, _, c in where],
                              [b[:, h:h + 1] for b, (_, _, h, _) in zip(bgs, where)],
                              [b[:, GC_LANE + h:GC_LANE + h + 1] for b, (_, _, h, _) in zip(bgs, where)],
                              [t_ref[h, rows, :] for rows, _, h, _ in where], cots)
            for i in range(CH_UNROLL):
                dbg = jnp.zeros((CHUNK, BA_PAD), F32)
                for (rows, _, h, c), (dq, dk, dv, dbeta, dgc) in list(zip(where, outs))[i * DN_HEADS:(i + 1) * DN_HEADS]:
                    dq_ref[rows, c] = dq
                    dk_ref[rows, c] = dk
                    dv_ref[rows, c] = dv
                    dbg = dbg + jnp.where(lane == h, dbeta, 0.0) + jnp.where(lane == GC_LANE + h, dgc, 0.0)
                dbg_ref[where[i * DN_HEADS][0], :] = dbg
            return carry

        lax.fori_loop(0, ncs // CH_UNROLL, chunks, 0)

    tok = lambda wd: pl.BlockSpec((ts, wd), lambda i: (i, 0))
    pspec = pl.BlockSpec((DN_HEADS, ts, CHUNK), lambda i: (0, i, 0))
    g8 = pl.BlockSpec((ncs * 8, DN_WIDTH), lambda i: (i, 0))
    return pl.pallas_call(
        body, name="dn_chunk_bwd", grid=(s // ts,), compiler_params=_params("arbitrary"),
        in_specs=[tok(DN_WIDTH)] * 3 + [tok(BA_PAD), pspec] + [tok(DN_WIDTH)] * 4 + [pspec, g8],
        out_specs=(tok(DN_WIDTH),) * 3 + (tok(BA_PAD),),
        out_shape=(SDS((s, DN_WIDTH), F32),) * 3 + (SDS((s, BA_PAD), F32),),
    )(q, k, v, bg, t, du, dw, dqd, dkd, dp, dgl)


def _dn_prep_bwd(qkv_pre, ba, dq, dk, dv, dbg, conv_w8, alog_row, dtb_row, ts):
    s = qkv_pre.shape[0]
    cw = 3 * DN_WIDTH
    nt = s // ts

    def body(pre_ref, ph_ref, nh_ref, ba_ref, dq_ref, dqh_ref, dk_ref, dkh_ref, dv_ref, dvh_ref, dbg_ref,
             cw_ref, al_ref, dtb_ref, dpre_ref, dba_ref, dcw_ref, dal_ref, ddtb_ref):
        n = pl.program_id(0)

        @pl.when(n == 0)
        def _():
            dcw_ref[...] = jnp.zeros_like(dcw_ref)
            dal_ref[...] = jnp.zeros_like(dal_ref)
            ddtb_ref[...] = jnp.zeros_like(ddtb_ref)

        last = n == nt - 1
        prev = jnp.where(n == 0, 0.0, ph_ref[...])
        ext = jnp.concatenate([prev, pre_ref[...], nh_ref[...]], axis=0)
        taps = _conv_taps(ext, ts + 8)
        conv = taps[0] * cw_ref[0:1, :]
        for j in range(1, CONV_K):
            conv = conv + taps[j] * cw_ref[j:j + 1, :]

        def cot(main, halo, cols):
            return jnp.concatenate([main[:, cols], jnp.where(last, 0.0, halo[:, cols])], axis=0)

        pieces = []
        for grp, (fn, mref, href) in enumerate(((_post_q, dq_ref, dqh_ref), (_post_k, dk_ref, dkh_ref),
                                                (_post_v, dv_ref, dvh_ref))):
            for h in range(DN_HEADS):
                cols = slice(h * DN_DIM, (h + 1) * DN_DIM)
                c0 = grp * DN_WIDTH + h * DN_DIM
                _, vjp = jax.vjp(fn, conv[:, c0:c0 + DN_DIM])
                pieces.append(vjp(cot(mref, href, cols))[0])
        dconv = jnp.concatenate(pieces, axis=1)
        rows = ts + 8
        dpre = dconv[:ts] * cw_ref[CONV_K - 1:CONV_K, :]
        for j in range(CONV_K - 1):
            sh = CONV_K - 1 - j
            dpre = dpre + pltpu.roll(dconv, rows - sh, 0)[:ts] * cw_ref[j:j + 1, :]
        dpre_ref[...] = _bf(dpre)
        for j in range(CONV_K):
            dcw_ref[j:j + 1, :] += jnp.sum(dconv[:ts] * taps[j][:ts], axis=0, keepdims=True)

        dbg = dbg_ref[...]
        lane = lax.broadcasted_iota(jnp.int32, dbg.shape, 1)
        dg = pltpu.roll(_chunk_cumsum(dbg, reverse=True), BA_PAD - DN_HEADS, 1)
        cot_bg = jnp.where(lane < DN_HEADS, dbg, jnp.where(lane < GC_LANE, dg, 0.0))
        _, vjp = jax.vjp(_beta_decay, ba_ref[...], al_ref[...], dtb_ref[...])
        dba, dal, ddtb = vjp(cot_bg)
        dba_ref[...] = _bf(dba)
        dal_ref[...] += dal
        ddtb_ref[...] += ddtb

    tok = lambda w: pl.BlockSpec((ts, w), lambda i: (i, 0))
    full = lambda a: pl.BlockSpec(a.shape, lambda i: (0, 0))
    prevh = lambda w: pl.BlockSpec((8, w), lambda i: (jnp.maximum(i * (ts // 8) - 1, 0), 0))
    nexth = lambda w: pl.BlockSpec((8, w), lambda i: (jnp.minimum((i + 1) * (ts // 8), s // 8 - 1), 0))
    row = pl.BlockSpec((1, LANES), lambda i: (0, 0))
    return pl.pallas_call(
        body, name="dn_prep_bwd", grid=(nt,), compiler_params=_params("arbitrary"),
        in_specs=[tok(cw), prevh(cw), nexth(cw), tok(BA_PAD),
                  tok(DN_WIDTH), nexth(DN_WIDTH), tok(DN_WIDTH), nexth(DN_WIDTH), tok(DN_WIDTH), nexth(DN_WIDTH),
                  tok(BA_PAD), full(conv_w8), full(alog_row), full(dtb_row)],
        out_specs=(tok(cw), tok(BA_PAD), pl.BlockSpec((8, cw), lambda i: (0, 0)), row, row),
        out_shape=(SDS((s, cw), BF16), SDS((s, BA_PAD), BF16), SDS((8, cw), F32), SDS((1, LANES), F32),
                   SDS((1, LANES), F32)),
    )(qkv_pre, qkv_pre, qkv_pre, ba, dq, dq, dk, dk, dv, dv, dbg, conv_w8, alog_row, dtb_row)


def _dh_dx(dps, ws, x, mod, norm_w, dx2, ts):
    s = x.shape[0]
    widths = [w.shape[1] for w in ws]
    np_ = len(ws)

    def body(*refs):
        dp_refs, w_refs = refs[:np_], refs[np_:2 * np_]
        x_ref, mod_ref, nw_ref, dx2_ref, gx_ref, dshift, dscale, dnw = refs[2 * np_:]

        @pl.when(pl.program_id(0) == 0)
        def _():
            dshift[...] = jnp.zeros_like(dshift)
            dscale[...] = jnp.zeros_like(dscale)
            dnw[...] = jnp.zeros_like(dnw)

        dh = lax.dot_general(dp_refs[0][...], w_refs[0][...], _NT, preferred_element_type=F32)
        for a, b in zip(dp_refs[1:], w_refs[1:]):
            dh = dh + lax.dot_general(a[...], b[...], _NT, preferred_element_type=F32)
        xt = x_ref[...]
        r = lax.rsqrt(jnp.mean(xt * xt, axis=-1, keepdims=True) + EPS)
        xn = xt * r
        nw = nw_ref[...]
        sc1 = 1.0 + mod_ref[:, D_MODEL:2 * D_MODEL]
        dshift[...] += jnp.sum(dh, axis=0, keepdims=True)
        dscale[...] += jnp.sum(dh * (xn * nw), axis=0, keepdims=True)
        dnw[...] += jnp.sum(dh * sc1 * xn, axis=0, keepdims=True)
        dxn = dh * sc1 * nw
        gx_ref[...] = r * (dxn - xn * jnp.mean(dxn * xn, axis=-1, keepdims=True)) + dx2_ref[...]

    tok = lambda w: pl.BlockSpec((ts, w), lambda i: (i, 0))
    full = lambda a: pl.BlockSpec(a.shape, lambda i: (0, 0))
    row = pl.BlockSpec((1, D_MODEL), lambda i: (0, 0))
    return pl.pallas_call(
        body, name="dh_dx", grid=(s // ts,), compiler_params=_params("arbitrary"),
        in_specs=[tok(w) for w in widths] + [full(w) for w in ws] + [tok(D_MODEL), full(mod), full(norm_w),
                                                                    tok(D_MODEL)],
        out_specs=(tok(D_MODEL), row, row, row),
        out_shape=(SDS((s, D_MODEL), F32),) + (SDS((1, D_MODEL), F32),) * 3,
    )(*dps, *ws, x, mod, norm_w, dx2)


def _grad_w_in(h, dps, ts, name):
    s = h.shape[0]
    widths = [p.shape[1] for p in dps]
    np_ = len(dps)

    def body(*refs):
        h_ref, dp_refs, outs = refs[0], refs[1:1 + np_], refs[1 + np_:]

        @pl.when(pl.program_id(0) == 0)
        def _():
            for o in outs:
                o[...] = jnp.zeros_like(o)

        hb = h_ref[...]
        for p, o in zip(dp_refs, outs):
            o[...] += lax.dot_general(hb, p[...], _TN, preferred_element_type=F32)

    tok = lambda w: pl.BlockSpec((ts, w), lambda i: (i, 0))
    return pl.pallas_call(
        body, name=name, grid=(s // ts,), compiler_params=_params("arbitrary"),
        in_specs=[tok(D_MODEL)] + [tok(w) for w in widths],
        out_specs=tuple(pl.BlockSpec((D_MODEL, w), lambda i: (0, 0)) for w in widths),
        out_shape=tuple(SDS((D_MODEL, w), F32) for w in widths),
    )(h, *dps)


def _adamw_math(w, g, m, v):
    m = ADAM_B1 * m + (1.0 - ADAM_B1) * g
    v = ADAM_B2 * v + (1.0 - ADAM_B2) * (g * g)
    m_hat = m / (1.0 - ADAM_B1 ** ADAM_STEP)
    v_hat = v / (1.0 - ADAM_B2 ** ADAM_STEP)
    delta = -ADAM_LR * (m_hat / (jnp.sqrt(v_hat) + ADAM_EPS) + ADAM_WD * w)
    return delta, m, v


def _adamw(w, m, v, g, name, slots=False):
    def body(w_ref, m_ref, v_ref, g_ref, g_out, d_out, m_out, v_out):
        if slots:
            g = g_ref[0].astype(F32)
            for k in range(1, N_DEV):
                g = g + g_ref[k].astype(F32)
        else:
            g = g_ref[...]
        g_out[...] = g
        d_out[...], m_out[...], v_out[...] = _adamw_math(w_ref[...], g, m_ref[...], v_ref[...])

    return pl.pallas_call(body, name=name, compiler_params=_params(),
                          out_shape=(SDS(w.shape, F32),) * 4)(w, m, v, g)


def _adamw_w_mod(w, m, v, siluc_all, dmod_mine):
    def body(w_ref, m_ref, v_ref, sc_ref, dm_ref, g_out, d_out, m_out, v_out):
        g = _htn(sc_ref[...], dm_ref[...])
        g_out[...] = g
        d_out[...], m_out[...], v_out[...] = _adamw_math(w_ref[...], g, m_ref[...], v_ref[...])

    return pl.pallas_call(body, name="adamw_w_mod", compiler_params=_params(),
                          out_shape=(SDS(w.shape, F32),) * 4)(w, m, v, siluc_all, dmod_mine)


def _pack_sum(pack_all):
    def body(p_ref, o_ref):
        t = p_ref[0]
        for k in range(1, N_DEV):
            t = t + p_ref[k]
        o_ref[...] = t

    return pl.pallas_call(body, name="pack_sum", out_shape=SDS(pack_all.shape[1:], F32))(pack_all)


def _tile(s, want):
    t = min(want, s)
    assert s % t == 0
    return t


def _local_step(x, c, positions, w_mod_bf, b_mod, norm_w, w_in_bf, conv_w, a_log, dt_bias, dn_norm_w, at_norm_w,
                w_out_bf, final_norm_w, tgt):
    s = x.shape[0]
    o = [0]
    for wdt in IN_SPLITS:
        o.append(o[-1] + wdt)
    w_ba = jnp.pad(w_in_bf[:, o[2]:o[4]], ((0, 0), (0, BA_PAD - 2 * DN_HEADS)))
    ws = [w_in_bf[:, o[0]:o[1]], w_in_bf[:, o[1]:o[2]], w_ba, w_in_bf[:, o[4]:o[5]], w_in_bf[:, o[5]:o[6]],
          w_in_bf[:, o[6]:o[7]], w_in_bf[:, o[7]:o[8]]]
    conv_w8 = jnp.pad(conv_w, ((0, 8 - CONV_K), (0, 0)))
    alog_row = jnp.pad(a_log, ((0, 0), (DN_HEADS, BA_PAD - 2 * DN_HEADS)))
    dtb_row = jnp.pad(dt_bias, ((0, 0), (DN_HEADS, BA_PAD - 2 * DN_HEADS)))
    atw2 = jnp.concatenate([at_norm_w, at_norm_w], axis=1)

    half = AT_DIM // 2
    lane = jnp.arange(LANES)
    inv_freq = ROPE_THETA ** (-(lane % half).astype(F32) / half)
    ang = positions.astype(F32)[:, None] * inv_freq
    cos_t = jnp.cos(ang)
    sin_t = jnp.sin(ang) * jnp.where((lane // half) % 2 == 0, -1.0, 1.0)

    mod, siluc = _adaln_mod(c, w_mod_bf, b_mod)
    gate = mod[:, 2 * D_MODEL:]
    hbf, qkv_pre, z_dn, ba, qr, kr, vb, z_at = _ln_proj(x, mod, norm_w, ws, cos_t, sin_t, _tile(s, 512))
    q, k, v, bg = _dn_prep(qkv_pre, ba, conv_w8, alog_row, dtb_row, _tile(s, 256))
    u, w, qd, kd, p, gl, tinv = _dn_chunk_prep(q, k, v, bg, _tile(s, 512))
    o_dn, vn, st = _dn_scan(u, w, qd, kd, p, gl, _tile(s, 512))
    o_at, lse = _attn_fwd(qr, kr, vb)
    dx2, dcat, gw_out, dfw, dgate, loss = _out_loss(o_dn, z_dn, o_at, z_at, dn_norm_w, atw2, x, tgt, w_out_bf, gate,
                                                    final_norm_w, _tile(s, 512))

    do_dn, dz_dn, do_at, dz_at, delta, ddnw, datw = _mix_bwd(dcat, o_dn, z_dn, o_at, z_at, dn_norm_w, atw2,
                                                             _tile(s, 512))
    daq, dak, dav = _rope_bwd(*_attn_bwd(qr, kr, vb, do_at, lse, delta), cos_t, sin_t, _tile(s, 512))
    du, dw, dqd, dkd, dp, dgl = _dn_scan_bwd(do_dn, st, vn, w, qd, kd, p, gl, _tile(s, 512))
    dq, dk, dv, dbg = _dn_chunk_bwd(q, k, v, bg, tinv, du, dw, dqd, dkd, dp, dgl, _tile(s, 512))
    dqkv, dba, dcw, dal, ddtb = _dn_prep_bwd(qkv_pre, ba, dq, dk, dv, dbg, conv_w8, alog_row, dtb_row, _tile(s, 256))
    dps = [dqkv, dz_dn, dba, daq, dak, dav, dz_at]
    gx, dshift, dscale, dnw = _dh_dx(dps, ws, x, mod, norm_w, dx2, _tile(s, 512))
    g_qkv, g_z, g_ba = _grad_w_in(hbf, dps[:3], _tile(s, 512), "grad_w_in_dn")
    g_aq, g_ak, g_av, g_az = _grad_w_in(hbf, dps[3:], _tile(s, 512), "grad_w_in_at")
    gw_in = jnp.concatenate([g_qkv, g_z, g_ba[:, :2 * DN_HEADS], g_aq, g_ak, g_av, g_az], axis=1)
    dmod = jnp.concatenate([dshift, dscale, dgate], axis=1)
    small = dict(conv=dcw[:CONV_K], dmod=dmod, siluc=siluc, dnw=dnw, dfw=dfw, alog=dal, dtb=ddtb, dnn=ddnw, atn=datw)
    return loss, gx, gw_in, gw_out, small


def kernel(x, c, positions, w_mod, b_mod, norm_w, w_in, conv_w, a_log, dt_bias, dn_norm_w, at_norm_w, w_out, final_norm_w, loss_target, m_w_mod, m_b_mod, m_norm_w, m_w_in, m_conv_w, m_a_log, m_dt_bias, m_dn_norm_w, m_at_norm_w, m_w_out, m_final_norm_w, v_w_mod, v_b_mod, v_norm_w, v_w_in, v_conv_w, v_a_log, v_dt_bias, v_dn_norm_w, v_at_norm_w, v_w_out, v_final_norm_w):
    me = 4 * lax.axis_index("x") + 2 * lax.axis_index("y") + lax.axis_index("c")
    s = x.shape[1]

    g_mod, g_in, g_conv, g_out = _all_gather(
        [_bf(w_mod[0]), _bf(w_in[0]), conv_w[0], _bf(w_out[0])], "gather_weights")
    w_mod_bf = g_mod.transpose(1, 0, 2).reshape(D_MODEL, 3 * D_MODEL)
    w_in_bf = g_in.transpose(1, 0, 2).reshape(D_MODEL, IN_COLS)
    conv_full = g_conv.transpose(1, 0, 2).reshape(CONV_K, 3 * DN_WIDTH)
    w_out_bf = g_out.reshape(D_MODEL, D_MODEL)

    loss, gx, gw_in, gw_out, small = _local_step(
        x[0], c, positions[0], w_mod_bf, b_mod, norm_w, w_in_bf, conv_full, a_log, dt_bias, dn_norm_w, at_norm_w,
        w_out_bf, final_norm_w.reshape(1, D_MODEL), loss_target[0])

    pack = jnp.concatenate([small["conv"].reshape(1, -1), small["dmod"], small["siluc"], small["dnw"], small["dfw"],
                            small["alog"], small["dtb"], small["dnn"], small["atn"],
                            jnp.pad(loss, ((0, 0), (0, LANES - 1)))], axis=1).reshape(PK_ROWS, LANES)
    gw_in_slabs = _bf(gw_in).reshape(D_MODEL, N_DEV, IN_SHARD).transpose(1, 0, 2)
    gw_out_slabs = _bf(gw_out).reshape(N_DEV, D_MODEL // N_DEV, D_MODEL)
    r_in, r_out, pack_all = _exchange([gw_in_slabs, gw_out_slabs, pack], [True, True, False], "exchange_grads")

    res = {}
    res["w_in"] = _adamw(w_in[0], m_w_in[0], v_w_in[0], r_in, "adamw_w_in", slots=True)
    res["w_out"] = _adamw(w_out[0], m_w_out[0], v_w_out[0], r_out, "adamw_w_out", slots=True)
    flat_all = pack_all.reshape(N_DEV, PK_END)
    dmod_mine = lax.dynamic_slice(flat_all, (0, PK_DMOD + me * (3 * D_MODEL // N_DEV)), (N_DEV, 3 * D_MODEL // N_DEV))
    res["w_mod"] = _adamw_w_mod(w_mod[0], m_w_mod[0], v_w_mod[0], flat_all[:, PK_SILUC:PK_DNW], dmod_mine)
    tot = _pack_sum(pack_all).reshape(1, PK_END)
    g_conv_full = tot[:, PK_CONV:PK_DMOD].reshape(CONV_K, 3 * DN_WIDTH)
    g_conv_mine = lax.dynamic_slice(g_conv_full, (0, me * (3 * DN_WIDTH // N_DEV)), (CONV_K, 3 * DN_WIDTH // N_DEV))
    res["conv_w"] = _adamw(conv_w[0], m_conv_w[0], v_conv_w[0], g_conv_mine, "adamw_conv_w")
    res["b_mod"] = _adamw(b_mod, m_b_mod, v_b_mod, tot[:, PK_DMOD:PK_SILUC], "adamw_b_mod")
    res["norm_w"] = _adamw(norm_w, m_norm_w, v_norm_w, tot[:, PK_DNW:PK_DFW], "adamw_norm_w")
    res["a_log"] = _adamw(a_log, m_a_log, v_a_log, tot[:, PK_ALOG + DN_HEADS:PK_ALOG + 2 * DN_HEADS], "adamw_a_log")
    res["dt_bias"] = _adamw(dt_bias, m_dt_bias, v_dt_bias, tot[:, PK_DTB + DN_HEADS:PK_DTB + 2 * DN_HEADS],
                            "adamw_dt_bias")
    res["dn_norm_w"] = _adamw(dn_norm_w, m_dn_norm_w, v_dn_norm_w, tot[:, PK_DNN:PK_ATN], "adamw_dn_norm_w")
    g_atn = tot[:, PK_ATN:PK_ATN + AT_DIM] + tot[:, PK_ATN + AT_DIM:PK_LOSS]
    res["at_norm_w"] = _adamw(at_norm_w, m_at_norm_w, v_at_norm_w, g_atn, "adamw_at_norm_w")
    fin = _adamw(final_norm_w.reshape(1, D_MODEL), m_final_norm_w.reshape(1, D_MODEL),
                 v_final_norm_w.reshape(1, D_MODEL), tot[:, PK_DFW:PK_ALOG], "adamw_final_norm_w")
    res["final_norm_w"] = tuple(a.reshape(D_MODEL) for a in fin)

    lead = ("w_mod", "w_in", "conv_w", "w_out")
    names = ("w_mod", "b_mod", "norm_w", "w_in", "conv_w", "a_log", "dt_bias", "dn_norm_w", "at_norm_w", "w_out",
             "final_norm_w")
    out = [tot[0, PK_LOSS], gx.reshape(1, s, D_MODEL)]
    for kind in range(4):
        for nm in names:
            a = res[nm][kind]
            out.append(a[None] if nm in lead else a)
    return tuple(out)
```

```python
import functools

import jax
import jax.numpy as jnp
from jax import lax
from jax.experimental import pallas as pl
from jax.experimental.pallas import tpu as pltpu

F32, BF16 = jnp.float32, jnp.bfloat16
HI = lax.Precision.HIGHEST
SDS = jax.ShapeDtypeStruct

D_MODEL = 1024
DN_HEADS, DN_DIM, DN_WIDTH = 4, 128, 512
AT_HEADS, AT_DIM, AT_WIDTH = 8, 64, 512
CONV_K = 4
CHUNK = 64
Q_BLOCK = 128
W_SUB = 128
DILATIONS = (1, 4, 16)
AT_PAIRS = 4
ATT_BLK = Q_BLOCK * max(DILATIONS)
ATT_UNROLL, ATT_UNROLL_BWD = 8, 4
CH_UNROLL = 4
ROPE_THETA = 10000.0
EPS = 1e-6
N_DEV = 8
LANES = 128
BA_PAD = 128
IN_SPLITS = (1536, 512, 4, 4, 512, 512, 512, 512)
IN_COLS = sum(IN_SPLITS)
IN_SHARD = IN_COLS // N_DEV
VMEM_LIMIT = 56 * 2 ** 20

ADAM_LR, ADAM_B1, ADAM_B2, ADAM_EPS, ADAM_WD, ADAM_STEP = 0.001, 0.9, 0.999, 1e-08, 0.01, 10

PK_CONV, PK_DMOD, PK_SILUC, PK_DNW, PK_DFW, PK_ALOG, PK_DTB, PK_DNN, PK_ATN, PK_LOSS, PK_END = (
    0, 6144, 9216, 10240, 11264, 12288, 12416, 12544, 12672, 12800, 12928)
PK_ROWS = PK_END // LANES

_NT = (((1,), (1,)), ((), ()))
_TN = (((0,), (0,)), ((), ()))


def _params(*sem):
    return pltpu.CompilerParams(dimension_semantics=sem or None, vmem_limit_bytes=VMEM_LIMIT)


def _bf(x):
    return x.astype(BF16)


def _nn(a, b):
    return jnp.dot(_bf(a), _bf(b), preferred_element_type=F32)


def _nt(a, b):
    return lax.dot_general(_bf(a), _bf(b), _NT, preferred_element_type=F32)


def _tn(a, b):
    return lax.dot_general(_bf(a), _bf(b), _TN, preferred_element_type=F32)


def _htn(a, b):
    return lax.dot_general(a, b, _TN, precision=HI, preferred_element_type=F32)


def _head_sum(x):
    r = lax.broadcasted_iota(jnp.int32, (LANES, LANES), 0)
    c = lax.broadcasted_iota(jnp.int32, (LANES, LANES), 1)
    same = jnp.where((r // AT_DIM) == (c // AT_DIM), 1.0, 0.0).astype(BF16)
    hi, lo = _hl(x)
    return jnp.dot(hi, same, preferred_element_type=F32) + jnp.dot(lo, same, preferred_element_type=F32)


@jax.custom_vjp
def _d_head_sum(x):
    return _head_sum(x)


_d_head_sum.defvjp(lambda x: (_head_sum(x), None), lambda _, g: (_head_sum(g),))


def _silu(x):
    return x * jax.nn.sigmoid(x)


def _softplus(x):
    return jnp.maximum(x, 0.0) + jnp.log(1.0 + jnp.exp(-jnp.abs(x)))


def _l2n(x):
    return x * lax.rsqrt(jnp.sum(x * x, axis=-1, keepdims=True) + EPS)


def _post_q(x):
    return _l2n(_silu(x)) * (DN_DIM ** -0.5)


def _post_k(x):
    return _l2n(_silu(x))


def _post_v(x):
    return _silu(x)


def _beta_decay(ba, alog_row, dtb_row):
    lane = lax.broadcasted_iota(jnp.int32, ba.shape, 1)
    return jnp.where(lane < DN_HEADS, jax.nn.sigmoid(ba), -jnp.exp(alog_row) * _softplus(ba + dtb_row))


def _gate_dn(o, z, w):
    return (o * lax.rsqrt(jnp.mean(o * o, axis=-1, keepdims=True) + EPS)) * w * _silu(z)


def _gate_at(o, z, w2, head_sum):
    ms = head_sum(o * o) * (1.0 / AT_DIM)
    return (o * lax.rsqrt(ms + EPS)) * w2 * _silu(z)


def _swap_half64(x):
    lane = lax.broadcasted_iota(jnp.int32, x.shape, 1)
    return jnp.where((lane & (AT_DIM - 1)) < AT_DIM // 2, pltpu.roll(x, LANES - AT_DIM // 2, 1),
                     pltpu.roll(x, AT_DIM // 2, 1))


_NN = (((1,), (0,)), ((), ()))


def _hl(a):
    hi = a.astype(BF16)
    return hi, (a - hi.astype(F32)).astype(BF16)


def _mm3(a, b, dims=_NN):
    (ah, al), (bh, bl) = a, b
    f = lambda x, y: lax.dot_general(x, y, dims, preferred_element_type=F32)
    return f(ah, bh) + (f(ah, bl) + f(al, bh))


def _chunk_masks():
    r = lax.broadcasted_iota(jnp.int32, (CHUNK, CHUNK), 0)
    c = lax.broadcasted_iota(jnp.int32, (CHUNK, CHUNK), 1)
    return r >= c, r > c, (r == c).astype(F32), (r // 16) == (c // 16)


def _tri_inv(mats):
    _, _, eye, blk = _chunk_masks()
    dg = [jnp.where(blk, a, 0.0) for a in mats]
    lo = [jnp.where(blk, 0.0, a) for a in mats]
    sdg = [_hl(x) for x in dg]
    d2 = [_mm3(s, s) for s in sdg]
    sd2 = [_hl(x) for x in d2]
    d4 = [_mm3(s, s) for s in sd2]
    sd4 = [_hl(x) for x in d4]
    d8 = [_mm3(s, s) for s in sd4]
    p1 = [_mm3(_hl(eye - a), _hl(eye + b)) for a, b in zip(dg, d2)]
    p2 = [_mm3(_hl(a), _hl(eye + b)) for a, b in zip(p1, d4)]
    dinv = [_mm3(_hl(a), _hl(eye + b)) for a, b in zip(p2, d8)]
    sdinv = [_hl(x) for x in dinv]
    n1 = [_mm3(s, _hl(b)) for s, b in zip(sdinv, lo)]
    sn1 = [_hl(x) for x in n1]
    n2 = [_mm3(s, s) for s in sn1]
    q1 = [_mm3(_hl(eye - a), _hl(eye + b)) for a, b in zip(n1, n2)]
    return [_mm3(_hl(a), s) for a, s in zip(q1, sdinv)]


def _chunk_common(qs, ks, vs, betas, gcs):
    tril, _, _, _ = _chunk_masks()
    out = []
    for q, k, v, beta, gc in zip(qs, ks, vs, betas, gcs):
        gb = jnp.broadcast_to(gc, (CHUNK, DN_DIM))
        gt = gb.T[:CHUNK, :]
        gam = jnp.where(tril, jnp.exp(jnp.where(tril, gb[:, :CHUNK] - gt, 0.0)), 0.0)
        last = gb[CHUNK - 1:CHUNK, :]
        eg, e2 = jnp.exp(gb), jnp.exp(last - gb)
        kb, vb = k * beta, v * beta
        out.append(dict(gam=gam, eg=eg, e2=e2, gl=jnp.exp(last[:, 0:1]), kb=kb, vb=vb, kbg=kb * eg,
                        m=_nt(kb, k), qk=_nt(q, k)))
    return out


def _chunk_fwd(qs, ks, vs, betas, gcs):
    tril, strict, _, _ = _chunk_masks()
    cm = _chunk_common(qs, ks, vs, betas, gcs)
    ts = _tri_inv([jnp.where(strict, c["m"] * c["gam"], 0.0) for c in cm])
    outs = []
    for q, k, c, t in zip(qs, ks, cm, ts):
        uw = _nn(t, jnp.concatenate([c["vb"], c["kbg"]], axis=1))
        p = jnp.where(tril, c["qk"] * c["gam"], 0.0)
        outs.append((uw[:, :DN_DIM], uw[:, DN_DIM:], p, q * c["eg"], k * c["e2"], c["gl"], t.T))
    return outs


def _chunk_bwd(qs, ks, vs, betas, gcs, ts, cots):
    tril, strict, _, _ = _chunk_masks()
    cm = _chunk_common(qs, ks, vs, betas, gcs)
    row = lax.broadcasted_iota(jnp.int32, (CHUNK, 1), 0)
    ones = jnp.ones((CHUNK, DN_DIM), BF16)
    rs = lambda x: jnp.sum(x, axis=-1, keepdims=True)
    tts = [_bf(t) for t in ts]
    duw = [_bf(jnp.concatenate([ct[0], ct[1]], axis=1)) for ct in cots]
    dts = [_nt(a, jnp.concatenate([c["vb"], c["kbg"]], axis=1)) for a, c in zip(duw, cm)]
    xs = [_nn(t, d) for t, d in zip(tts, dts)]
    das = [jnp.where(strict, -_nn(x, t), 0.0) for x, t in zip(xs, tts)]
    dvks = [_nn(t, a) for t, a in zip(tts, duw)]
    outs = []
    for q, k, v, beta, c, ct, da, dvk in zip(qs, ks, vs, betas, cm, cots, das, dvks):
        _, _, dp, dqd, dkd, dgl = ct
        dvb, dkbg = dvk[:, :DN_DIM], dvk[:, DN_DIM:]
        dm = da * c["gam"]
        dqk = jnp.where(tril, dp, 0.0) * c["gam"]
        e = dm * c["m"] + dqk * c["qk"]
        dmq = jnp.concatenate([dm, dqk], axis=0)
        r1 = _nn(dmq, k)
        dkb = r1[:CHUNK] + dkbg * c["eg"]
        dq = r1[CHUNK:] + dqd * c["eg"]
        dk = _tn(dmq, jnp.concatenate([c["kb"], q], axis=0)) + dkd * c["e2"] + dkb * beta
        dbeta = rs(dkb * k + dvb * v)
        eh, el = _hl(e)
        colsum = (lax.dot_general(eh, ones, _TN, preferred_element_type=F32)
                  + lax.dot_general(el, ones, _TN, preferred_element_type=F32))[:, 0:1]
        pkd = dkd * (k * c["e2"])
        dgc = rs(e) - colsum + rs(dqd * q * c["eg"] + dkbg * c["kbg"] - pkd)
        tail = rs(jnp.sum(pkd, axis=0, keepdims=True)) + dgl * c["gl"]
        dgc = dgc + jnp.where(row == CHUNK - 1, tail, 0.0)
        outs.append((dq, dk, dvb * beta, dbeta, dgc))
    return outs


def _chunk_cumsum(x, reverse=False):
    n = x.shape[0]
    pos = lax.broadcasted_iota(jnp.int32, x.shape, 0) & (CHUNK - 1)
    sh = 1
    while sh < CHUNK:
        if reverse:
            x = x + jnp.where(pos < CHUNK - sh, pltpu.roll(x, n - sh, 0), 0.0)
        else:
            x = x + jnp.where(pos >= sh, pltpu.roll(x, sh, 0), 0.0)
        sh *= 2
    return x


GC_LANE = 2 * DN_HEADS


def _exchange(arrays, scatter, name):
    n = len(arrays)
    out_shapes = []
    for a, sc in zip(arrays, scatter):
        out_shapes.append(SDS(a.shape if sc else (N_DEV,) + a.shape, a.dtype))

    def body(*refs):
        ins, outs = refs[:n], refs[n:2 * n]
        send_sems, recv_sems, loc_sems = refs[2 * n:]
        x, y, c = lax.axis_index("x"), lax.axis_index("y"), lax.axis_index("c")
        me = 4 * x + 2 * y + c
        local, remote = [], []
        for i in range(n):
            src = ins[i].at[me] if scatter[i] else ins[i]
            cp = pltpu.make_async_copy(src, outs[i].at[me], loc_sems.at[i])
            cp.start()
            local.append(cp)
        for dlt in range(1, N_DEV):
            px = 1 - x if dlt & 4 else x
            py = 1 - y if dlt & 2 else y
            pc = 1 - c if dlt & 1 else c
            peer = 4 * px + 2 * py + pc
            for i in range(n):
                src = ins[i].at[peer] if scatter[i] else ins[i]
                cp = pltpu.make_async_remote_copy(
                    src_ref=src, dst_ref=outs[i].at[me],
                    send_sem=send_sems.at[i, dlt - 1], recv_sem=recv_sems.at[i, dlt - 1],
                    device_id=(px, py, pc), device_id_type=pl.DeviceIdType.MESH)
                cp.start()
                arrive = pltpu.make_async_remote_copy(
                    src_ref=src, dst_ref=outs[i].at[peer],
                    send_sem=send_sems.at[i, dlt - 1], recv_sem=recv_sems.at[i, dlt - 1],
                    device_id=(px, py, pc), device_id_type=pl.DeviceIdType.MESH)
                remote.append((cp, arrive))
        for cp, arrive in remote:
            cp.wait_send()
            arrive.wait_recv()
        for cp in local:
            cp.wait()

    any_spec = pl.BlockSpec(memory_space=pl.ANY)
    return pl.pallas_call(
        body, name=name, out_shape=tuple(out_shapes),
        in_specs=[any_spec] * n, out_specs=tuple([any_spec] * n),
        scratch_shapes=[pltpu.SemaphoreType.DMA((n, N_DEV - 1)), pltpu.SemaphoreType.DMA((n, N_DEV - 1)),
                        pltpu.SemaphoreType.DMA((n,))],
    )(*arrays)


def _all_gather(arrays, name):
    n = len(arrays)

    def body(*refs):
        ins, outs = refs[:n], refs[n:2 * n]
        send_sems, recv_sems, loc_sems = refs[2 * n:]
        x, y, c = lax.axis_index("x"), lax.axis_index("y"), lax.axis_index("c")
        me, sibling = (x, y, c), (x, y, 1 - c)
        chips = [(1 - x, y), (x, 1 - y), (1 - x, 1 - y)]

        def copy(i, k, block, to, src=None):
            slot = outs[i].at[4 * block[0] + 2 * block[1] + block[2]]
            return pltpu.make_async_remote_copy(
                src_ref=slot if src is None else src, dst_ref=slot,
                send_sem=send_sems.at[i, k], recv_sem=recv_sems.at[i, k],
                device_id=to, device_id_type=pl.DeviceIdType.MESH)

        mine = [pltpu.make_async_copy(ins[i], outs[i].at[4 * x + 2 * y + c], loc_sems.at[i]) for i in range(n)]
        for cp in mine:
            cp.start()
        first = []
        for i in range(n):
            first.append(copy(i, 0, me, sibling, src=ins[i]))
            first += [copy(i, 1 + j, me, (*chip, c), src=ins[i]) for j, chip in enumerate(chips)]
        for cp in first:
            cp.start()
        passed = []
        for j, chip in enumerate(chips):
            for i in range(n):
                copy(i, 1 + j, (*chip, c), me).wait_recv()
                fwd = copy(i, 4 + j, (*chip, c), sibling)
                fwd.start()
                passed.append(fwd)
        for i in range(n):
            copy(i, 0, sibling, me).wait_recv()
        for j, chip in enumerate(chips):
            for i in range(n):
                copy(i, 4 + j, (*chip, 1 - c), me).wait_recv()
        for cp in first + passed:
            cp.wait_send()
        for cp in mine:
            cp.wait()

    any_spec = pl.BlockSpec(memory_space=pl.ANY)
    return pl.pallas_call(
        body, name=name, out_shape=tuple(SDS((N_DEV,) + a.shape, a.dtype) for a in arrays),
        in_specs=[any_spec] * n, out_specs=tuple([any_spec] * n),
        scratch_shapes=[pltpu.SemaphoreType.DMA((n, N_DEV - 1)), pltpu.SemaphoreType.DMA((n, N_DEV - 1)),
                        pltpu.SemaphoreType.DMA((n,))],
    )(*arrays)


_HBM = pl.BlockSpec(memory_space=pltpu.HBM)
_SEM = pl.BlockSpec(memory_space=pltpu.SEMAPHORE)


def _peers(x, y, c):
    out = []
    for dlt in range(1, N_DEV):
        px = 1 - x if dlt & 4 else x
        py = 1 - y if dlt & 2 else y
        pc = 1 - c if dlt & 1 else c
        out.append((dlt, (px, py, pc), 4 * px + 2 * py + pc))
    return out


def _scatter_start(arrays):
    n = len(arrays)
    ns = n * (N_DEV - 1)

    def body(*refs):
        ins, lands = refs[:n], refs[n:2 * n]
        send_sems, recv_sems = refs[2 * n:2 * n + ns], refs[2 * n + ns:2 * n + 2 * ns]
        token = refs[-1]
        x, y, c = lax.axis_index("x"), lax.axis_index("y"), lax.axis_index("c")
        me = 4 * x + 2 * y + c
        for dlt, peer, pi in _peers(x, y, c):
            for i in range(n):
                k = i * (N_DEV - 1) + dlt - 1
                pltpu.make_async_remote_copy(
                    src_ref=ins[i].at[pi], dst_ref=lands[i].at[me], send_sem=send_sems[k], recv_sem=recv_sems[k],
                    device_id=peer, device_id_type=pl.DeviceIdType.MESH).start()
        token[...] = jnp.zeros_like(token)

    sem = pltpu.SemaphoreType.DMA(())
    thru = tuple(pltpu.HBM(a.shape, a.dtype) for a in arrays)
    hbm = lambda a: pltpu.with_memory_space_constraint(a, pltpu.HBM)
    outs = pl.pallas_call(
        body, name="scatter_start", out_shape=(sem,) * (2 * ns) + thru + thru + (SDS((8, LANES), F32),),
        in_specs=[_HBM] * (2 * n),
        out_specs=(_SEM,) * (2 * ns) + (_HBM,) * (2 * n) + (pl.BlockSpec(memory_space=pltpu.VMEM),),
        input_output_aliases={i: 2 * ns + i for i in range(2 * n)},
        compiler_params=pltpu.CompilerParams(has_side_effects=pltpu.SideEffectType.DATAFLOW_SIDE_EFFECTING),
    )(*[hbm(a) for a in arrays], *[hbm(jnp.zeros(a.shape, a.dtype)) for a in arrays])
    return outs[:ns], outs[ns:2 * ns], outs[2 * ns:2 * ns + n], outs[2 * ns + n:2 * ns + 2 * n], outs[-1]


def _scatter_wait(send_sems, recv_sems, srcs, lands, after):
    n = len(srcs)
    ns = n * (N_DEV - 1)

    def body(*refs):
        ins, lands_ = refs[:n], refs[n:2 * n]
        send, recv = refs[2 * n:2 * n + ns], refs[2 * n + ns:2 * n + 2 * ns]
        x, y, c = lax.axis_index("x"), lax.axis_index("y"), lax.axis_index("c")
        for dlt, peer, pi in _peers(x, y, c):
            for i in range(n):
                k = i * (N_DEV - 1) + dlt - 1
                cp = pltpu.make_async_remote_copy(
                    src_ref=ins[i].at[pi], dst_ref=lands_[i].at[pi], send_sem=send[k], recv_sem=recv[k],
                    device_id=peer, device_id_type=pl.DeviceIdType.MESH)
                cp.wait_send()
                cp.wait_recv()

    thru = tuple(pltpu.HBM(a.shape, a.dtype) for a in srcs)
    outs = pl.pallas_call(
        body, name="scatter_wait", out_shape=thru + thru,
        in_specs=[_HBM] * (2 * n) + [_SEM] * (2 * ns) + [pl.BlockSpec(memory_space=pl.ANY)],
        out_specs=(_HBM,) * (2 * n), input_output_aliases={i: i for i in range(2 * n)},
        compiler_params=pltpu.CompilerParams(has_side_effects=pltpu.SideEffectType.DATAFLOW_SIDE_EFFECTING),
    )(*srcs, *lands, *send_sems, *recv_sems, after)
    return outs[n:]


def _adaln_mod(c, w_mod, b_mod):
    def body(c_ref, w_ref, b_ref, mod_ref, sc_ref):
        sc = _silu(c_ref[...])
        sc8 = jnp.broadcast_to(sc, (8, D_MODEL))
        mod_ref[...] = _nn(sc8, w_ref[...])[0:1] + b_ref[...]
        sc_ref[...] = sc

    return pl.pallas_call(body, name="adaln_mod", compiler_params=_params(),
                          out_shape=(SDS((1, 3 * D_MODEL), F32), SDS((1, D_MODEL), F32)))(c, w_mod, b_mod)


def _ln_proj(x, mod, norm_w, ws, cos_t, sin_t, ts):
    s = x.shape[0]
    widths = [w.shape[1] for w in ws]

    def body(x_ref, mod_ref, nw_ref, cos_ref, sin_ref, wqkv, wz, wba, waq, wak, wav, waz,
             h_ref, oqkv, oz, oba, oq, ok, ov, oaz):
        xt = x_ref[...]
        r = lax.rsqrt(jnp.mean(xt * xt, axis=-1, keepdims=True) + EPS)
        shift, scale = mod_ref[:, 0:D_MODEL], mod_ref[:, D_MODEL:2 * D_MODEL]
        h = ((xt * r) * nw_ref[...]) * (1.0 + scale) + shift
        hb = _bf(h)
        h_ref[...] = hb
        oqkv[...] = jnp.dot(hb, wqkv[...], preferred_element_type=F32)
        oz[...] = jnp.dot(hb, wz[...], preferred_element_type=F32)
        oba[...] = jnp.dot(hb, wba[...], preferred_element_type=F32)
        oaz[...] = jnp.dot(hb, waz[...], preferred_element_type=F32)
        tv = jnp.dot(hb, wav[...], preferred_element_type=F32)
        for j in range(AT_PAIRS):
            ov[j] = tv[:, j * LANES:(j + 1) * LANES]
        cs, sn = cos_ref[...], sin_ref[...]
        for w_ref, o_ref in ((waq, oq), (wak, ok)):
            t = jnp.dot(hb, w_ref[...], preferred_element_type=F32)
            for j in range(AT_PAIRS):
                tj = t[:, j * LANES:(j + 1) * LANES]
                o_ref[j] = tj * cs + _swap_half64(tj) * sn

    tok = lambda w: pl.BlockSpec((ts, w), lambda i: (i, 0))
    full = lambda a: pl.BlockSpec(a.shape, lambda i: (0, 0))
    pairs = pl.BlockSpec((AT_PAIRS, ts, LANES), lambda i: (0, i, 0))
    return pl.pallas_call(
        body, name="ln_proj", grid=(s // ts,), compiler_params=_params("arbitrary"),
        in_specs=[tok(D_MODEL), full(mod), full(norm_w), tok(LANES), tok(LANES)] + [full(w) for w in ws],
        out_specs=(tok(D_MODEL), tok(widths[0]), tok(widths[1]), tok(widths[2]), pairs, pairs, pairs,
                   tok(widths[6])),
        out_shape=(SDS((s, D_MODEL), BF16), SDS((s, widths[0]), F32), SDS((s, widths[1]), F32),
                   SDS((s, widths[2]), F32)) + (SDS((AT_PAIRS, s, LANES), F32),) * 3 + (SDS((s, widths[6]), F32),),
    )(x, mod, norm_w, cos_t, sin_t, *ws)


def _conv_taps(ext, rows):
    taps = []
    for j in range(CONV_K):
        sh = CONV_K - 1 - j
        rolled = pltpu.roll(ext, sh, 0) if sh else ext
        taps.append(rolled[8:8 + rows])
    return taps


def _dn_prep(qkv_pre, ba, conv_w8, alog_row, dtb_row, ts):
    s = qkv_pre.shape[0]
    cw = 3 * DN_WIDTH

    def body(pre_ref, halo_ref, ba_ref, cw_ref, al_ref, dtb_ref, q_ref, k_ref, v_ref, bg_ref):
        n = pl.program_id(0)
        prev = jnp.where(n == 0, 0.0, halo_ref[...])
        ext = jnp.concatenate([prev, pre_ref[...]], axis=0)
        taps = _conv_taps(ext, ts)
        conv = taps[0] * cw_ref[0:1, :]
        for j in range(1, CONV_K):
            conv = conv + taps[j] * cw_ref[j:j + 1, :]
        for h in range(DN_HEADS):
            cols = slice(h * DN_DIM, (h + 1) * DN_DIM)
            q_ref[:, cols] = _post_q(conv[:, h * DN_DIM:(h + 1) * DN_DIM])
            k_ref[:, cols] = _post_k(conv[:, DN_WIDTH + h * DN_DIM:DN_WIDTH + (h + 1) * DN_DIM])
            v_ref[:, cols] = _post_v(conv[:, 2 * DN_WIDTH + h * DN_DIM:2 * DN_WIDTH + (h + 1) * DN_DIM])
        bg = _beta_decay(ba_ref[...], al_ref[...], dtb_ref[...])
        lane = lax.broadcasted_iota(jnp.int32, bg.shape, 1)
        run = pltpu.roll(_chunk_cumsum(bg), DN_HEADS, 1)
        bg_ref[...] = jnp.where((lane >= GC_LANE) & (lane < GC_LANE + DN_HEADS), run, bg)

    tok = lambda w: pl.BlockSpec((ts, w), lambda i: (i, 0))
    full = lambda a: pl.BlockSpec(a.shape, lambda i: (0, 0))
    halo = pl.BlockSpec((8, cw), lambda i: (jnp.maximum(i * (ts // 8) - 1, 0), 0))
    return pl.pallas_call(
        body, name="dn_prep", grid=(s // ts,), compiler_params=_params("arbitrary"),
        in_specs=[tok(cw), halo, tok(BA_PAD), full(conv_w8), full(alog_row), full(dtb_row)],
        out_specs=(tok(DN_WIDTH), tok(DN_WIDTH), tok(DN_WIDTH), tok(BA_PAD)),
        out_shape=(SDS((s, DN_WIDTH), F32),) * 3 + (SDS((s, BA_PAD), F32),),
    )(qkv_pre, qkv_pre, ba, conv_w8, alog_row, dtb_row)


def _dn_chunk_prep(q, k, v, bg, ts):
    s = q.shape[0]
    ncs = ts // CHUNK

    def body(q_ref, k_ref, v_ref, bg_ref, u_ref, w_ref, qd_ref, kd_ref, p_ref, gl_ref, t_ref):
        def chunks(cg, carry):
            where = []
            for ci in (cg * CH_UNROLL + i for i in range(CH_UNROLL)):
                rows = pl.ds(pl.multiple_of(ci * CHUNK, CHUNK), CHUNK)
                rows8 = pl.ds(pl.multiple_of(ci * 8, 8), 8)
                where += [(rows, rows8, h, slice(h * DN_DIM, (h + 1) * DN_DIM)) for h in range(DN_HEADS)]
            bgs = [bg_ref[rows, :] for rows, _, _, _ in where]
            outs = _chunk_fwd([q_ref[rows, c] for rows, _, _, c in where], [k_ref[rows, c] for rows, _, _, c in where],
                              [v_ref[rows, c] for rows, _, _, c in where],
                              [b[:, h:h + 1] for b, (_, _, h, _) in zip(bgs, where)],
                              [b[:, GC_LANE + h:GC_LANE + h + 1] for b, (_, _, h, _) in zip(bgs, where)])
            for (rows, rows8, h, c), (u, w, p, qd, kd, gl, t) in zip(where, outs):
                u_ref[rows, c] = u
                w_ref[rows, c] = w
                qd_ref[rows, c] = qd
                kd_ref[rows, c] = kd
                p_ref[h, rows, :] = p
                t_ref[h, rows, :] = t
                gl_ref[rows8, c] = jnp.broadcast_to(gl, (8, DN_DIM))
            return carry

        lax.fori_loop(0, ncs // CH_UNROLL, chunks, 0)

    tok = lambda w: pl.BlockSpec((ts, w), lambda i: (i, 0))
    sq = pl.BlockSpec((DN_HEADS, ts, CHUNK), lambda i: (0, i, 0))
    return pl.pallas_call(
        body, name="dn_chunk_prep", grid=(s // ts,), compiler_params=_params("arbitrary"),
        in_specs=[tok(DN_WIDTH)] * 3 + [tok(BA_PAD)],
        out_specs=(tok(DN_WIDTH),) * 4 + (sq, pl.BlockSpec((ncs * 8, DN_WIDTH), lambda i: (i, 0)), sq),
        out_shape=(SDS((s, DN_WIDTH), F32),) * 4 + (SDS((DN_HEADS, s, CHUNK), F32),
                                                     SDS((s // CHUNK * 8, DN_WIDTH), F32),
                                                     SDS((DN_HEADS, s, CHUNK), F32)),
    )(q, k, v, bg)


def _dn_scan(u, w, qd, kd, p, gl, ts):
    s = u.shape[0]
    ncs = ts // CHUNK

    def body(u_ref, w_ref, qd_ref, kd_ref, p_ref, gl_ref, o_ref, vn_ref, st_ref, state):
        @pl.when(pl.program_id(0) == 0)
        def _():
            state[...] = jnp.zeros_like(state)

        def chunk(ci, carry):
            rows = pl.ds(pl.multiple_of(ci * CHUNK, CHUNK), CHUNK)
            rows8 = pl.ds(pl.multiple_of(ci * 8, 8), 8)
            srows = pl.ds(pl.multiple_of(ci * DN_DIM, DN_DIM), DN_DIM)
            hs = range(DN_HEADS)
            sl = [slice(h * DN_DIM, (h + 1) * DN_DIM) for h in hs]
            sf = [state[h] for h in hs]
            sb = [_bf(x) for x in sf]
            ws = [_nn(w_ref[rows, c], b) for c, b in zip(sl, sb)]
            qs = [_nn(qd_ref[rows, c], b) for c, b in zip(sl, sb)]
            vn = [u_ref[rows, c] - x for c, x in zip(sl, ws)]
            vb = [_bf(x) for x in vn]
            kv = [_tn(kd_ref[rows, c], b) for c, b in zip(sl, vb)]
            pv = [_nn(p_ref[h, rows, :], b) for h, b in zip(hs, vb)]
            for h in hs:
                state[h] = sf[h] * gl_ref[rows8, sl[h]][0:1] + kv[h]
            for h in hs:
                st_ref[srows, sl[h]] = sf[h]
                vn_ref[rows, sl[h]] = vn[h]
                o_ref[rows, sl[h]] = qs[h] + pv[h]
            return carry

        lax.fori_loop(0, ncs, chunk, 0)

    tok = lambda wd: pl.BlockSpec((ts, wd), lambda i: (i, 0))
    return pl.pallas_call(
        body, name="dn_scan", grid=(s // ts,), compiler_params=_params("arbitrary"),
        in_specs=[tok(DN_WIDTH)] * 4 + [pl.BlockSpec((DN_HEADS, ts, CHUNK), lambda i: (0, i, 0)),
                                        pl.BlockSpec((ncs * 8, DN_WIDTH), lambda i: (i, 0))],
        out_specs=(tok(DN_WIDTH), tok(DN_WIDTH), pl.BlockSpec((ncs * DN_DIM, DN_WIDTH), lambda i: (i, 0))),
        out_shape=(SDS((s, DN_WIDTH), F32), SDS((s, DN_WIDTH), F32), SDS((s // CHUNK * DN_DIM, DN_WIDTH), F32)),
        scratch_shapes=[pltpu.VMEM((DN_HEADS, DN_DIM, DN_DIM), F32)],
    )(u, w, qd, kd, p, gl)


LOG2E, LN2 = 1.4426950408889634, 0.6931471805599453
MASKED = -1e30


def _band_bias():
    qi = lax.broadcasted_iota(jnp.int32, (Q_BLOCK, 2 * Q_BLOCK), 0)
    kj = lax.broadcasted_iota(jnp.int32, (Q_BLOCK, 2 * Q_BLOCK), 1)
    rel = Q_BLOCK + qi - kj
    return jnp.where((rel >= 0) & (rel <= W_SUB), 0.0, MASKED)


def _first_bias(first):
    kj = lax.broadcasted_iota(jnp.int32, (1, 2 * Q_BLOCK), 1)
    return jnp.where((kj < Q_BLOCK) & first, MASKED, 0.0)


def _attn_combo(c, d):
    if d == 1:
        qs = pl.multiple_of(c * Q_BLOCK, Q_BLOCK)
        return qs, pl.multiple_of(ATT_BLK - Q_BLOCK + c * Q_BLOCK, Q_BLOCK), c == 0
    r, m = c % d, c // d
    qs = r + (d * Q_BLOCK) * m
    return qs, ATT_BLK + qs - d * Q_BLOCK, m == 0


def _rows(start, size, d):
    return pl.ds(pl.multiple_of(start, Q_BLOCK), size) if d == 1 else pl.ds(start, size, stride=d)


def _shift_in(ext, cur, n):
    @pl.when(n == 0)
    def _():
        ext[0:ATT_BLK, :] = jnp.zeros((ATT_BLK, LANES), F32)

    @pl.when(n > 0)
    def _():
        ext[0:ATT_BLK, :] = ext[ATT_BLK:2 * ATT_BLK, :]

    ext[ATT_BLK:2 * ATT_BLK, :] = cur


def _attn_fwd(qr, kr, vv):
    s = qr.shape[1]
    nblk = s // ATT_BLK
    scale = AT_DIM ** -0.5
    npat = len(DILATIONS)

    def body(q_ref, k_ref, v_ref, o_ref, lse_ref, kext, vext, o_p, l_p, bias_ref):
        n = pl.program_id(1)
        _shift_in(kext, k_ref[0], n)
        _shift_in(vext, v_ref[0], n)
        bias_ref[...] = _band_bias()
        lo = lax.broadcasted_iota(jnp.int32, (Q_BLOCK, LANES), 1) < AT_DIM
        for pi, d in enumerate(DILATIONS):
            def group(g, carry, pi=pi, d=d):
                cs = [_attn_combo(g * ATT_UNROLL + u, d) for u in range(ATT_UNROLL)]
                heads = [(i, sel) for i in range(ATT_UNROLL) for sel in (lo, ~lo)]
                band = bias_ref[...]
                bias = [band + _first_bias((n == 0) & m0) for _, _, m0 in cs]
                qb = [_bf(q_ref[0, _rows(qs, Q_BLOCK, d), :]) for qs, _, _ in cs]
                kk = [_bf(kext[_rows(ks, 2 * Q_BLOCK, d), :]) for _, ks, _ in cs]
                vb = [_bf(vext[_rows(ks, 2 * Q_BLOCK, d), :]) for _, ks, _ in cs]
                sc = [lax.dot_general(jnp.where(sel, qb[i], jnp.zeros_like(qb[i])), kk[i], _NT,
                                      preferred_element_type=F32) for i, sel in heads]
                sc = [x * (scale * LOG2E) + bias[i] for x, (i, _) in zip(sc, heads)]
                mx = [jnp.max(x, axis=-1, keepdims=True) for x in sc]
                pr = [jnp.exp2(x - m) for x, m in zip(sc, mx)]
                ls = [jnp.sum(x, axis=-1, keepdims=True) for x in pr]
                pv = [jnp.dot(_bf(x), vb[i], preferred_element_type=F32) for x, (i, _) in zip(pr, heads)]
                outs = [x / l for x, l in zip(pv, ls)]
                lses = [m * LN2 + jnp.log(l) for m, l in zip(mx, ls)]
                for i, (qs, _, _) in enumerate(cs):
                    o_p[pi, _rows(qs, Q_BLOCK, d), :] = jnp.where(lo, outs[2 * i], outs[2 * i + 1])
                    l_p[pi, _rows(qs, Q_BLOCK, d), :] = jnp.where(lo, lses[2 * i], lses[2 * i + 1])
                return carry

            lax.fori_loop(0, ATT_BLK // Q_BLOCK // ATT_UNROLL, group, 0)

        def merge(i, carry):
            rows = pl.ds(pl.multiple_of(i * 256, 256), 256)
            ls = [l_p[pi, rows, :] for pi in range(npat)]
            mx = jnp.maximum(jnp.maximum(ls[0], ls[1]), ls[2])
            es = [jnp.exp(l - mx) for l in ls]
            den = es[0] + es[1] + es[2]
            o_ref[0, rows, :] = (es[0] * o_p[0, rows, :] + es[1] * o_p[1, rows, :] + es[2] * o_p[2, rows, :]) / den
            lse_ref[0, rows, :] = mx + jnp.log(den)
            return carry

        lax.fori_loop(0, ATT_BLK // 256, merge, 0)

    blk = pl.BlockSpec((1, ATT_BLK, LANES), lambda j, n: (j, n, 0))
    return pl.pallas_call(
        body, name="attn_fwd", grid=(AT_PAIRS, nblk), compiler_params=_params("arbitrary", "arbitrary"),
        in_specs=[blk] * 3, out_specs=(blk, blk),
        out_shape=(SDS((AT_PAIRS, s, LANES), F32),) * 2,
        scratch_shapes=[pltpu.VMEM((2 * ATT_BLK, LANES), F32), pltpu.VMEM((2 * ATT_BLK, LANES), F32),
                        pltpu.VMEM((npat, ATT_BLK, LANES), F32), pltpu.VMEM((npat, ATT_BLK, LANES), F32),
                        pltpu.VMEM((Q_BLOCK, 2 * Q_BLOCK), F32)],
    )(qr, kr, vv)


def _out_loss(o_dn, z_dn, o_at, z_at, dnw, atw2, x, tgt, w_out, gate, fw, ts):
    s = x.shape[0]

    def body(odn, zdn, oat, zat, dnw_ref, atw_ref, x_ref, t_ref, w_ref, g_ref, fw_ref,
             dx2_ref, dcat_ref, gw_ref, dfw_ref, dgate_ref, loss_ref):
        @pl.when(pl.program_id(0) == 0)
        def _():
            gw_ref[...] = jnp.zeros_like(gw_ref)
            dfw_ref[...] = jnp.zeros_like(dfw_ref)
            dgate_ref[...] = jnp.zeros_like(dgate_ref)
            loss_ref[...] = jnp.zeros_like(loss_ref)

        parts = [_bf(_gate_dn(odn[:, h * DN_DIM:(h + 1) * DN_DIM], zdn[:, h * DN_DIM:(h + 1) * DN_DIM], dnw_ref[...]))
                 for h in range(DN_HEADS)]
        parts += [_bf(_gate_at(oat[j], zat[:, j * LANES:(j + 1) * LANES], atw_ref[...], _head_sum))
                  for j in range(AT_PAIRS)]
        catb = jnp.concatenate(parts, axis=1)
        wb = w_ref[...]
        gate, fwv = g_ref[...], fw_ref[...]
        mix = jnp.dot(catb, wb, preferred_element_type=F32)
        x2 = x_ref[...] + gate * mix
        r2 = lax.rsqrt(jnp.mean(x2 * x2, axis=-1, keepdims=True) + EPS)
        xn2 = x2 * r2
        err = xn2 * fwv - t_ref[...]
        row = jnp.sum(err * err, axis=-1, keepdims=True) * (1.0 / D_MODEL)
        loss_ref[...] += 0.5 * jnp.sum(row, axis=0, keepdims=True)
        dy = err * (1.0 / D_MODEL)
        dfw_ref[...] += jnp.sum(dy * xn2, axis=0, keepdims=True)
        dxn = dy * fwv
        dx2 = r2 * (dxn - xn2 * jnp.mean(dxn * xn2, axis=-1, keepdims=True))
        dx2_ref[...] = dx2
        dgate_ref[...] += jnp.sum(dx2 * mix, axis=0, keepdims=True)
        dmix = _bf(gate * dx2)
        dcat_ref[...] = lax.dot_general(dmix, wb, _NT, preferred_element_type=F32)
        gw_ref[...] += lax.dot_general(catb, dmix, _TN, preferred_element_type=F32)

    tok = lambda w: pl.BlockSpec((ts, w), lambda i: (i, 0))
    full = lambda a: pl.BlockSpec(a.shape, lambda i: (0, 0))
    row = pl.BlockSpec((1, D_MODEL), lambda i: (0, 0))
    pairs = pl.BlockSpec((AT_PAIRS, ts, LANES), lambda i: (0, i, 0))
    return pl.pallas_call(
        body, name="out_loss", grid=(s // ts,), compiler_params=_params("arbitrary"),
        in_specs=[tok(DN_WIDTH), tok(DN_WIDTH), pairs, tok(AT_WIDTH), full(dnw), full(atw2),
                  tok(D_MODEL), tok(D_MODEL), full(w_out), full(gate), full(fw)],
        out_specs=(tok(D_MODEL), tok(D_MODEL), pl.BlockSpec((D_MODEL, D_MODEL), lambda i: (0, 0)), row, row,
                   pl.BlockSpec((1, 1), lambda i: (0, 0))),
        out_shape=(SDS((s, D_MODEL), F32), SDS((s, D_MODEL), F32), SDS((D_MODEL, D_MODEL), F32),
                   SDS((1, D_MODEL), F32), SDS((1, D_MODEL), F32), SDS((1, 1), F32)),
    )(o_dn, z_dn, o_at, z_at, dnw, atw2, x, tgt, w_out, gate, fw)


def _mix_bwd(dcat, o_dn, z_dn, o_at, z_at, dnw, atw2, ts):
    s = dcat.shape[0]

    def body(dcat_ref, odn, zdn, oat, zat, dnw_ref, atw_ref, dodn, dzdn, doat, dzat, delta, ddnw, datw):
        @pl.when(pl.program_id(0) == 0)
        def _():
            ddnw[...] = jnp.zeros_like(ddnw)
            datw[...] = jnp.zeros_like(datw)

        for h in range(DN_HEADS):
            cols = slice(h * DN_DIM, (h + 1) * DN_DIM)
            _, vjp = jax.vjp(_gate_dn, odn[:, cols], zdn[:, cols], dnw_ref[...])
            do, dz, dw = vjp(dcat_ref[:, cols])
            dodn[:, cols] = do
            dzdn[:, cols] = _bf(dz)
            ddnw[...] += dw
        for j in range(AT_PAIRS):
            cols = slice(j * LANES, (j + 1) * LANES)
            o = oat[j]
            _, vjp = jax.vjp(functools.partial(_gate_at, head_sum=_d_head_sum), o, zat[:, cols], atw_ref[...])
            do, dz, dw = vjp(dcat_ref[:, DN_WIDTH + j * LANES:DN_WIDTH + (j + 1) * LANES])
            doat[j] = do
            dzat[:, cols] = _bf(dz)
            datw[...] += dw
            delta[j] = _head_sum(do * o)

    tok = lambda w: pl.BlockSpec((ts, w), lambda i: (i, 0))
    full = lambda a: pl.BlockSpec(a.shape, lambda i: (0, 0))
    row = pl.BlockSpec((1, LANES), lambda i: (0, 0))
    pairs = pl.BlockSpec((AT_PAIRS, ts, LANES), lambda i: (0, i, 0))
    return pl.pallas_call(
        body, name="mix_bwd", grid=(s // ts,), compiler_params=_params("arbitrary"),
        in_specs=[tok(D_MODEL), tok(DN_WIDTH), tok(DN_WIDTH), pairs, tok(AT_WIDTH), full(dnw), full(atw2)],
        out_specs=(tok(DN_WIDTH), tok(DN_WIDTH), pairs, tok(AT_WIDTH), pairs, row, row),
        out_shape=(SDS((s, DN_WIDTH), F32), SDS((s, DN_WIDTH), BF16), SDS((AT_PAIRS, s, LANES), F32),
                   SDS((s, AT_WIDTH), BF16), SDS((AT_PAIRS, s, LANES), F32), SDS((1, LANES), F32),
                   SDS((1, LANES), F32)),
    )(dcat, o_dn, z_dn, o_at, z_at, dnw, atw2)


def _shift_acc(ext, n):
    @pl.when(n == 0)
    def _():
        ext[0:ATT_BLK, :] = jnp.zeros((ATT_BLK, LANES), F32)

    @pl.when(n > 0)
    def _():
        ext[0:ATT_BLK, :] = ext[ATT_BLK:2 * ATT_BLK, :]

    ext[ATT_BLK:2 * ATT_BLK, :] = jnp.zeros((ATT_BLK, LANES), F32)


def _attn_bwd(qr, kr, vv, do, lse, delta):
    s = qr.shape[1]
    nblk = s // ATT_BLK
    scale = AT_DIM ** -0.5

    def body(q_ref, k_ref, v_ref, do_ref, lse_ref, dl_ref, dq_ref, dk_ref, dv_ref, kext, vext, dkext, dvext,
             bias_ref):
        n = pl.program_id(1)
        _shift_in(kext, k_ref[0], n)
        _shift_in(vext, v_ref[0], n)
        _shift_acc(dkext, n)
        _shift_acc(dvext, n)
        bias_ref[...] = _band_bias()

        @pl.when(n < nblk)
        def _():
            dq_ref[0] = jnp.zeros((ATT_BLK, LANES), F32)
            lo = lax.broadcasted_iota(jnp.int32, (Q_BLOCK, LANES), 1) < AT_DIM
            for d in DILATIONS:
                def group(g, carry, d=d):
                    nu = ATT_UNROLL_BWD
                    cs = [_attn_combo(g * nu + u, d) for u in range(nu)]
                    heads = [(i, sel) for i in range(nu) for sel in (lo, ~lo)]
                    qrows = [_rows(qs, Q_BLOCK, d) for qs, _, _ in cs]
                    krows = [_rows(ks, 2 * Q_BLOCK, d) for _, ks, _ in cs]
                    band = bias_ref[...]
                    bias = [band + _first_bias((n == 0) & m0) for _, _, m0 in cs]
                    qb = [_bf(q_ref[0, r, :]) for r in qrows]
                    dob = [_bf(do_ref[0, r, :]) for r in qrows]
                    kk = [_bf(kext[r, :]) for r in krows]
                    vb = [_bf(vext[r, :]) for r in krows]
                    lse2 = [lse_ref[0, r, :] * LOG2E for r in qrows]
                    dl2 = [dl_ref[0, r, :] for r in qrows]
                    qm = [jnp.where(sel, qb[i], jnp.zeros_like(qb[i])) for i, sel in heads]
                    dom = [jnp.where(sel, dob[i], jnp.zeros_like(dob[i])) for i, sel in heads]
                    lse_c = [jnp.max(jnp.where(sel, lse2[i], -jnp.inf), axis=-1, keepdims=True) for i, sel in heads]
                    dl_c = [jnp.max(jnp.where(sel, dl2[i], -jnp.inf), axis=-1, keepdims=True) for i, sel in heads]
                    sc = [lax.dot_general(a, kk[i], _NT, preferred_element_type=F32) for a, (i, _) in zip(qm, heads)]
                    dp = [lax.dot_general(a, vb[i], _NT, preferred_element_type=F32) for a, (i, _) in zip(dom, heads)]
                    pr = [jnp.exp2(x * (scale * LOG2E) + bias[i] - l) for x, l, (i, _) in zip(sc, lse_c, heads)]
                    ds = [_bf(p * (x - dl) * scale) for p, x, dl in zip(pr, dp, dl_c)]
                    prb = [_bf(p) for p in pr]
                    dq = [jnp.dot(x, kk[i], preferred_element_type=F32) for x, (i, _) in zip(ds, heads)]
                    dk = [lax.dot_general(x, a, _TN, preferred_element_type=F32) for x, a in zip(ds, qm)]
                    dv = [lax.dot_general(x, a, _TN, preferred_element_type=F32) for x, a in zip(prb, dom)]
                    for i in range(nu):
                        dq_ref[0, qrows[i], :] += jnp.where(lo, dq[2 * i], dq[2 * i + 1])
                        dkext[krows[i], :] += dk[2 * i] + dk[2 * i + 1]
                        dvext[krows[i], :] += dv[2 * i] + dv[2 * i + 1]
                    return carry

                lax.fori_loop(0, ATT_BLK // Q_BLOCK // ATT_UNROLL_BWD, group, 0)

        dk_ref[0] = dkext[0:ATT_BLK, :]
        dv_ref[0] = dvext[0:ATT_BLK, :]

    cur = pl.BlockSpec((1, ATT_BLK, LANES), lambda j, n: (j, jnp.minimum(n, nblk - 1), 0))
    done = pl.BlockSpec((1, ATT_BLK, LANES), lambda j, n: (j, jnp.maximum(n - 1, 0), 0))
    return pl.pallas_call(
        body, name="attn_bwd", grid=(AT_PAIRS, nblk + 1), compiler_params=_params("arbitrary", "arbitrary"),
        in_specs=[cur] * 6, out_specs=(cur, done, done),
        out_shape=(SDS((AT_PAIRS, s, LANES), F32),) * 3,
        scratch_shapes=[pltpu.VMEM((2 * ATT_BLK, LANES), F32)] * 4 + [pltpu.VMEM((Q_BLOCK, 2 * Q_BLOCK), F32)],
    )(qr, kr, vv, do, lse, delta)


def _rope_bwd(dq, dk, dv, cos_t, sin_t, ts):
    s = cos_t.shape[0]

    def body(q_ref, k_ref, v_ref, cos_ref, sin_ref, oq, ok, ov):
        cs, sn = cos_ref[...], sin_ref[...]
        for j in range(AT_PAIRS):
            cols = slice(j * LANES, (j + 1) * LANES)
            for g_ref, o_ref in ((q_ref, oq), (k_ref, ok)):
                g = g_ref[j]
                o_ref[:, cols] = _bf(g * cs + _swap_half64(g * sn))
            ov[:, cols] = _bf(v_ref[j])

    tok = lambda w: pl.BlockSpec((ts, w), lambda i: (i, 0))
    pairs = pl.BlockSpec((AT_PAIRS, ts, LANES), lambda i: (0, i, 0))
    return pl.pallas_call(
        body, name="rope_bwd", grid=(s // ts,), compiler_params=_params("arbitrary"),
        in_specs=[pairs] * 3 + [tok(LANES)] * 2, out_specs=(tok(AT_WIDTH),) * 3,
        out_shape=(SDS((s, AT_WIDTH), BF16),) * 3,
    )(dq, dk, dv, cos_t, sin_t)


def _dn_scan_bwd(do, st, vn, w, qd, kd, p, gl, ts):
    s = do.shape[0]
    ncs = ts // CHUNK
    nt = s // ts

    def body(do_ref, st_ref, vn_ref, w_ref, qd_ref, kd_ref, p_ref, gl_ref,
             du_ref, dw_ref, dqd_ref, dkd_ref, dp_ref, dgl_ref, dstate):
        @pl.when(pl.program_id(0) == 0)
        def _():
            dstate[...] = jnp.zeros_like(dstate)

        def chunk(jr, carry):
            ci = ncs - 1 - jr
            rows = pl.ds(pl.multiple_of(ci * CHUNK, CHUNK), CHUNK)
            rows8 = pl.ds(pl.multiple_of(ci * 8, 8), 8)
            srows = pl.ds(pl.multiple_of(ci * DN_DIM, DN_DIM), DN_DIM)
            hs = range(DN_HEADS)
            sl = [slice(h * DN_DIM, (h + 1) * DN_DIM) for h in hs]
            ds_ = [dstate[h] for h in hs]
            dsb = [_bf(x) for x in ds_]
            dob = [_bf(do_ref[rows, c]) for c in sl]
            pdo = [_tn(p_ref[h, rows, :], b) for h, b in zip(hs, dob)]
            qdo = [_tn(qd_ref[rows, c], b) for c, b in zip(sl, dob)]
            dvn = [_nn(kd_ref[rows, c], b) + x for c, b, x in zip(sl, dsb, pdo)]
            dvb = [_bf(x) for x in dvn]
            wdv = [_tn(w_ref[rows, c], b) for c, b in zip(sl, dvb)]
            for h in hs:
                dstate[h] = ds_[h] * gl_ref[rows8, sl[h]][0:1] + qdo[h] - wdv[h]
            sfs = [st_ref[srows, c] for c in sl]
            sbs = [_bf(x) for x in sfs]
            vnb = [_bf(vn_ref[rows, c]) for c in sl]
            for h in hs:
                du_ref[rows, sl[h]] = dvn[h]
                dw_ref[rows, sl[h]] = -_nt(dvb[h], sbs[h])
                dqd_ref[rows, sl[h]] = _nt(dob[h], sbs[h])
                dkd_ref[rows, sl[h]] = _nt(vnb[h], dsb[h])
                dp_ref[h, rows, :] = _nt(dob[h], vnb[h])
                dgl = jnp.sum(jnp.sum(ds_[h] * sfs[h], axis=1, keepdims=True), axis=0, keepdims=True)
                dgl_ref[rows8, sl[h]] = jnp.broadcast_to(dgl, (8, DN_DIM))
            return carry

        lax.fori_loop(0, ncs, chunk, 0)

    tok = lambda wd: pl.BlockSpec((ts, wd), lambda i: (nt - 1 - i, 0))
    pspec = pl.BlockSpec((DN_HEADS, ts, CHUNK), lambda i: (0, nt - 1 - i, 0))
    g8 = pl.BlockSpec((ncs * 8, DN_WIDTH), lambda i: (nt - 1 - i, 0))
    return pl.pallas_call(
        body, name="dn_scan_bwd", grid=(nt,), compiler_params=_params("arbitrary"),
        in_specs=[tok(DN_WIDTH), pl.BlockSpec((ncs * DN_DIM, DN_WIDTH), lambda i: (nt - 1 - i, 0))]
        + [tok(DN_WIDTH)] * 4 + [pspec, g8],
        out_specs=(tok(DN_WIDTH),) * 4 + (pspec, g8),
        out_shape=(SDS((s, DN_WIDTH), F32),) * 4 + (SDS((DN_HEADS, s, CHUNK), F32),
                                                     SDS((s // CHUNK * 8, DN_WIDTH), F32)),
        scratch_shapes=[pltpu.VMEM((DN_HEADS, DN_DIM, DN_DIM), F32)],
    )(do, st, vn, w, qd, kd, p, gl)


def _dn_chunk_bwd(q, k, v, bg, t, du, dw, dqd, dkd, dp, dgl, ts):
    s = q.shape[0]
    ncs = ts // CHUNK

    def body(q_ref, k_ref, v_ref, bg_ref, t_ref, du_ref, dw_ref, dqd_ref, dkd_ref, dp_ref, dgl_ref,
             dq_ref, dk_ref, dv_ref, dbg_ref):
        def chunks(cg, carry):
            lane = lax.broadcasted_iota(jnp.int32, (CHUNK, BA_PAD), 1)
            where = []
            for ci in (cg * CH_UNROLL + i for i in range(CH_UNROLL)):
                rows = pl.ds(pl.multiple_of(ci * CHUNK, CHUNK), CHUNK)
                rows8 = pl.ds(pl.multiple_of(ci * 8, 8), 8)
                where += [(rows, rows8, h, slice(h * DN_DIM, (h + 1) * DN_DIM)) for h in range(DN_HEADS)]
            bgs = [bg_ref[rows, :] for rows, _, _, _ in where]
            cots = [(du_ref[rows, c], dw_ref[rows, c], dp_ref[h, rows, :], dqd_ref[rows, c], dkd_ref[rows, c],
                     dgl_ref[rows8, c][0:1, 0:1]) for rows, rows8, h, c in where]
            outs = _chunk_bwd([q_ref[rows, c] for rows, _, _, c in where], [k_ref[rows, c] for rows, _, _, c in where],
                              [v_ref[rows, c] for rows, _, _, c in where],
                              [b[:, h:h + 1] for b, (_, _, h, _) in zip(bgs, where)],
                              [b[:, GC_LANE + h:GC_LANE + h + 1] for b, (_, _, h, _) in zip(bgs, where)],
                              [t_ref[h, rows, :] for rows, _, h, _ in where], cots)
            for i in range(CH_UNROLL):
                dbg = jnp.zeros((CHUNK, BA_PAD), F32)
                for (rows, _, h, c), (dq, dk, dv, dbeta, dgc) in list(zip(where, outs))[i * DN_HEADS:(i + 1) * DN_HEADS]:
                    dq_ref[rows, c] = dq
                    dk_ref[rows, c] = dk
                    dv_ref[rows, c] = dv
                    dbg = dbg + jnp.where(lane == h, dbeta, 0.0) + jnp.where(lane == GC_LANE + h, dgc, 0.0)
                dbg_ref[where[i * DN_HEADS][0], :] = dbg
            return carry

        lax.fori_loop(0, ncs // CH_UNROLL, chunks, 0)

    tok = lambda wd: pl.BlockSpec((ts, wd), lambda i: (i, 0))
    pspec = pl.BlockSpec((DN_HEADS, ts, CHUNK), lambda i: (0, i, 0))
    g8 = pl.BlockSpec((ncs * 8, DN_WIDTH), lambda i: (i, 0))
    return pl.pallas_call(
        body, name="dn_chunk_bwd", grid=(s // ts,), compiler_params=_params("arbitrary"),
        in_specs=[tok(DN_WIDTH)] * 3 + [tok(BA_PAD), pspec] + [tok(DN_WIDTH)] * 4 + [pspec, g8],
        out_specs=(tok(DN_WIDTH),) * 3 + (tok(BA_PAD),),
        out_shape=(SDS((s, DN_WIDTH), F32),) * 3 + (SDS((s, BA_PAD), F32),),
    )(q, k, v, bg, t, du, dw, dqd, dkd, dp, dgl)


def _dn_prep_bwd(qkv_pre, ba, dq, dk, dv, dbg, conv_w8, alog_row, dtb_row, ts):
    s = qkv_pre.shape[0]
    cw = 3 * DN_WIDTH
    nt = s // ts

    def body(pre_ref, ph_ref, nh_ref, ba_ref, dq_ref, dqh_ref, dk_ref, dkh_ref, dv_ref, dvh_ref, dbg_ref,
             cw_ref, al_ref, dtb_ref, dpre_ref, dba_ref, dcw_ref, dal_ref, ddtb_ref):
        n = pl.program_id(0)

        @pl.when(n == 0)
        def _():
            dcw_ref[...] = jnp.zeros_like(dcw_ref)
            dal_ref[...] = jnp.zeros_like(dal_ref)
            ddtb_ref[...] = jnp.zeros_like(ddtb_ref)

        last = n == nt - 1
        prev = jnp.where(n == 0, 0.0, ph_ref[...])
        ext = jnp.concatenate([prev, pre_ref[...], nh_ref[...]], axis=0)
        taps = _conv_taps(ext, ts + 8)
        conv = taps[0] * cw_ref[0:1, :]
        for j in range(1, CONV_K):
            conv = conv + taps[j] * cw_ref[j:j + 1, :]

        def cot(main, halo, cols):
            return jnp.concatenate([main[:, cols], jnp.where(last, 0.0, halo[:, cols])], axis=0)

        pieces = []
        for grp, (fn, mref, href) in enumerate(((_post_q, dq_ref, dqh_ref), (_post_k, dk_ref, dkh_ref),
                                                (_post_v, dv_ref, dvh_ref))):
            for h in range(DN_HEADS):
                cols = slice(h * DN_DIM, (h + 1) * DN_DIM)
                c0 = grp * DN_WIDTH + h * DN_DIM
                _, vjp = jax.vjp(fn, conv[:, c0:c0 + DN_DIM])
                pieces.append(vjp(cot(mref, href, cols))[0])
        dconv = jnp.concatenate(pieces, axis=1)
        rows = ts + 8
        dpre = dconv[:ts] * cw_ref[CONV_K - 1:CONV_K, :]
        for j in range(CONV_K - 1):
            sh = CONV_K - 1 - j
            dpre = dpre + pltpu.roll(dconv, rows - sh, 0)[:ts] * cw_ref[j:j + 1, :]
        dpre_ref[...] = _bf(dpre)
        for j in range(CONV_K):
            dcw_ref[j:j + 1, :] += jnp.sum(dconv[:ts] * taps[j][:ts], axis=0, keepdims=True)

        dbg = dbg_ref[...]
        lane = lax.broadcasted_iota(jnp.int32, dbg.shape, 1)
        dg = pltpu.roll(_chunk_cumsum(dbg, reverse=True), BA_PAD - DN_HEADS, 1)
        cot_bg = jnp.where(lane < DN_HEADS, dbg, jnp.where(lane < GC_LANE, dg, 0.0))
        _, vjp = jax.vjp(_beta_decay, ba_ref[...], al_ref[...], dtb_ref[...])
        dba, dal, ddtb = vjp(cot_bg)
        dba_ref[...] = _bf(dba)
        dal_ref[...] += dal
        ddtb_ref[...] += ddtb

    tok = lambda w: pl.BlockSpec((ts, w), lambda i: (i, 0))
    full = lambda a: pl.BlockSpec(a.shape, lambda i: (0, 0))
    prevh = lambda w: pl.BlockSpec((8, w), lambda i: (jnp.maximum(i * (ts // 8) - 1, 0), 0))
    nexth = lambda w: pl.BlockSpec((8, w), lambda i: (jnp.minimum((i + 1) * (ts // 8), s // 8 - 1), 0))
    row = pl.BlockSpec((1, LANES), lambda i: (0, 0))
    return pl.pallas_call(
        body, name="dn_prep_bwd", grid=(nt,), compiler_params=_params("arbitrary"),
        in_specs=[tok(cw), prevh(cw), nexth(cw), tok(BA_PAD),
                  tok(DN_WIDTH), nexth(DN_WIDTH), tok(DN_WIDTH), nexth(DN_WIDTH), tok(DN_WIDTH), nexth(DN_WIDTH),
                  tok(BA_PAD), full(conv_w8), full(alog_row), full(dtb_row)],
        out_specs=(tok(cw), tok(BA_PAD), pl.BlockSpec((8, cw), lambda i: (0, 0)), row, row),
        out_shape=(SDS((s, cw), BF16), SDS((s, BA_PAD), BF16), SDS((8, cw), F32), SDS((1, LANES), F32),
                   SDS((1, LANES), F32)),
    )(qkv_pre, qkv_pre, qkv_pre, ba, dq, dq, dk, dk, dv, dv, dbg, conv_w8, alog_row, dtb_row)


def _dh_dx(dps, ws, x, mod, norm_w, dx2, ts):
    s = x.shape[0]
    widths = [w.shape[1] for w in ws]
    np_ = len(ws)

    def body(*refs):
        dp_refs, w_refs = refs[:np_], refs[np_:2 * np_]
        x_ref, mod_ref, nw_ref, dx2_ref, gx_ref, dshift, dscale, dnw = refs[2 * np_:]

        @pl.when(pl.program_id(0) == 0)
        def _():
            dshift[...] = jnp.zeros_like(dshift)
            dscale[...] = jnp.zeros_like(dscale)
            dnw[...] = jnp.zeros_like(dnw)

        dh = lax.dot_general(dp_refs[0][...], w_refs[0][...], _NT, preferred_element_type=F32)
        for a, b in zip(dp_refs[1:], w_refs[1:]):
            dh = dh + lax.dot_general(a[...], b[...], _NT, preferred_element_type=F32)
        xt = x_ref[...]
        r = lax.rsqrt(jnp.mean(xt * xt, axis=-1, keepdims=True) + EPS)
        xn = xt * r
        nw = nw_ref[...]
        sc1 = 1.0 + mod_ref[:, D_MODEL:2 * D_MODEL]
        dshift[...] += jnp.sum(dh, axis=0, keepdims=True)
        dscale[...] += jnp.sum(dh * (xn * nw), axis=0, keepdims=True)
        dnw[...] += jnp.sum(dh * sc1 * xn, axis=0, keepdims=True)
        dxn = dh * sc1 * nw
        gx_ref[...] = r * (dxn - xn * jnp.mean(dxn * xn, axis=-1, keepdims=True)) + dx2_ref[...]

    tok = lambda w: pl.BlockSpec((ts, w), lambda i: (i, 0))
    full = lambda a: pl.BlockSpec(a.shape, lambda i: (0, 0))
    row = pl.BlockSpec((1, D_MODEL), lambda i: (0, 0))
    return pl.pallas_call(
        body, name="dh_dx", grid=(s // ts,), compiler_params=_params("arbitrary"),
        in_specs=[tok(w) for w in widths] + [full(w) for w in ws] + [tok(D_MODEL), full(mod), full(norm_w),
                                                                    tok(D_MODEL)],
        out_specs=(tok(D_MODEL), row, row, row),
        out_shape=(SDS((s, D_MODEL), F32),) + (SDS((1, D_MODEL), F32),) * 3,
    )(*dps, *ws, x, mod, norm_w, dx2)


def _grad_w_in(h, dps, ts, name):
    s = h.shape[0]
    widths = [p.shape[1] for p in dps]
    np_ = len(dps)

    def body(*refs):
        h_ref, dp_refs, outs = refs[0], refs[1:1 + np_], refs[1 + np_:]

        @pl.when(pl.program_id(0) == 0)
        def _():
            for o in outs:
                o[...] = jnp.zeros_like(o)

        hb = h_ref[...]
        for p, o in zip(dp_refs, outs):
            o[...] += lax.dot_general(hb, p[...], _TN, preferred_element_type=F32)

    tok = lambda w: pl.BlockSpec((ts, w), lambda i: (i, 0))
    return pl.pallas_call(
        body, name=name, grid=(s // ts,), compiler_params=_params("arbitrary"),
        in_specs=[tok(D_MODEL)] + [tok(w) for w in widths],
        out_specs=tuple(pl.BlockSpec((D_MODEL, w), lambda i: (0, 0)) for w in widths),
        out_shape=tuple(SDS((D_MODEL, w), F32) for w in widths),
    )(h, *dps)


def _adamw_math(w, g, m, v):
    m = ADAM_B1 * m + (1.0 - ADAM_B1) * g
    v = ADAM_B2 * v + (1.0 - ADAM_B2) * (g * g)
    m_hat = m / (1.0 - ADAM_B1 ** ADAM_STEP)
    v_hat = v / (1.0 - ADAM_B2 ** ADAM_STEP)
    delta = -ADAM_LR * (m_hat / (jnp.sqrt(v_hat) + ADAM_EPS) + ADAM_WD * w)
    return delta, m, v


def _adamw(w, m, v, g, name, own=None):
    def body(w_ref, m_ref, v_ref, g_ref, *rest):
        g_out, d_out, m_out, v_out = rest[-4:]
        if own is None:
            g = g_ref[...]
        else:
            g = g_ref[0].astype(F32)
            for k in range(1, N_DEV):
                g = g + g_ref[k].astype(F32)
            g = g + rest[0][...].astype(F32)
        g_out[...] = g
        d_out[...], m_out[...], v_out[...] = _adamw_math(w_ref[...], g, m_ref[...], v_ref[...])

    args = (w, m, v, g) if own is None else (w, m, v, g, own)
    return pl.pallas_call(body, name=name, compiler_params=_params(),
                          out_shape=(SDS(w.shape, F32),) * 4)(*args)


def _adamw_w_mod(w, m, v, siluc_all, dmod_mine):
    def body(w_ref, m_ref, v_ref, sc_ref, dm_ref, g_out, d_out, m_out, v_out):
        g = _htn(sc_ref[...], dm_ref[...])
        g_out[...] = g
        d_out[...], m_out[...], v_out[...] = _adamw_math(w_ref[...], g, m_ref[...], v_ref[...])

    return pl.pallas_call(body, name="adamw_w_mod", compiler_params=_params(),
                          out_shape=(SDS(w.shape, F32),) * 4)(w, m, v, siluc_all, dmod_mine)


def _pack_sum(pack_all):
    def body(p_ref, o_ref):
        t = p_ref[0]
        for k in range(1, N_DEV):
            t = t + p_ref[k]
        o_ref[...] = t

    return pl.pallas_call(body, name="pack_sum", out_shape=SDS(pack_all.shape[1:], F32))(pack_all)


def _tile(s, want):
    t = min(want, s)
    assert s % t == 0
    return t


def _local_step(x, c, positions, w_mod_bf, b_mod, norm_w, w_in_bf, conv_w, a_log, dt_bias, dn_norm_w, at_norm_w,
                w_out_bf, final_norm_w, tgt):
    s = x.shape[0]
    o = [0]
    for wdt in IN_SPLITS:
        o.append(o[-1] + wdt)
    w_ba = jnp.pad(w_in_bf[:, o[2]:o[4]], ((0, 0), (0, BA_PAD - 2 * DN_HEADS)))
    ws = [w_in_bf[:, o[0]:o[1]], w_in_bf[:, o[1]:o[2]], w_ba, w_in_bf[:, o[4]:o[5]], w_in_bf[:, o[5]:o[6]],
          w_in_bf[:, o[6]:o[7]], w_in_bf[:, o[7]:o[8]]]
    conv_w8 = jnp.pad(conv_w, ((0, 8 - CONV_K), (0, 0)))
    alog_row = jnp.pad(a_log, ((0, 0), (DN_HEADS, BA_PAD - 2 * DN_HEADS)))
    dtb_row = jnp.pad(dt_bias, ((0, 0), (DN_HEADS, BA_PAD - 2 * DN_HEADS)))
    atw2 = jnp.concatenate([at_norm_w, at_norm_w], axis=1)

    half = AT_DIM // 2
    lane = jnp.arange(LANES)
    inv_freq = ROPE_THETA ** (-(lane % half).astype(F32) / half)
    ang = positions.astype(F32)[:, None] * inv_freq
    cos_t = jnp.cos(ang)
    sin_t = jnp.sin(ang) * jnp.where((lane // half) % 2 == 0, -1.0, 1.0)

    mod, siluc = _adaln_mod(c, w_mod_bf, b_mod)
    gate = mod[:, 2 * D_MODEL:]
    hbf, qkv_pre, z_dn, ba, qr, kr, vb, z_at = _ln_proj(x, mod, norm_w, ws, cos_t, sin_t, _tile(s, 512))
    q, k, v, bg = _dn_prep(qkv_pre, ba, conv_w8, alog_row, dtb_row, _tile(s, 256))
    u, w, qd, kd, p, gl, tinv = _dn_chunk_prep(q, k, v, bg, _tile(s, 512))
    o_dn, vn, st = _dn_scan(u, w, qd, kd, p, gl, _tile(s, 512))
    o_at, lse = _attn_fwd(qr, kr, vb)
    dx2, dcat, gw_out, dfw, dgate, loss = _out_loss(o_dn, z_dn, o_at, z_at, dn_norm_w, atw2, x, tgt, w_out_bf, gate,
                                                    final_norm_w, _tile(s, 512))

    do_dn, dz_dn, do_at, dz_at, delta, ddnw, datw = _mix_bwd(dcat, o_dn, z_dn, o_at, z_at, dn_norm_w, atw2,
                                                             _tile(s, 512))
    daq, dak, dav = _rope_bwd(*_attn_bwd(qr, kr, vb, do_at, lse, delta), cos_t, sin_t, _tile(s, 512))
    du, dw, dqd, dkd, dp, dgl = _dn_scan_bwd(do_dn, st, vn, w, qd, kd, p, gl, _tile(s, 512))
    dq, dk, dv, dbg = _dn_chunk_bwd(q, k, v, bg, tinv, du, dw, dqd, dkd, dp, dgl, _tile(s, 512))
    dqkv, dba, dcw, dal, ddtb = _dn_prep_bwd(qkv_pre, ba, dq, dk, dv, dbg, conv_w8, alog_row, dtb_row, _tile(s, 256))
    dps = [dqkv, dz_dn, dba, daq, dak, dav, dz_at]
    g_qkv, g_z, g_ba = _grad_w_in(hbf, dps[:3], _tile(s, 512), "grad_w_in_dn")
    g_aq, g_ak, g_av, g_az = _grad_w_in(hbf, dps[3:], _tile(s, 512), "grad_w_in_at")
    gw_in = jnp.concatenate([g_qkv, g_z, g_ba[:, :2 * DN_HEADS], g_aq, g_ak, g_av, g_az], axis=1)
    small = dict(conv=dcw[:CONV_K], dgate=dgate, siluc=siluc, dfw=dfw, alog=dal, dtb=ddtb, dnn=ddnw, atn=datw)

    def input_grad(token):
        gx, dshift, dscale, dnw = _dh_dx(dps, ws, x, mod + token, norm_w, dx2, _tile(s, 512))
        return gx, jnp.concatenate([dshift, dscale, small["dgate"]], axis=1), dnw

    return loss, gw_in, gw_out, small, input_grad


def kernel(x, c, positions, w_mod, b_mod, norm_w, w_in, conv_w, a_log, dt_bias, dn_norm_w, at_norm_w, w_out, final_norm_w, loss_target, m_w_mod, m_b_mod, m_norm_w, m_w_in, m_conv_w, m_a_log, m_dt_bias, m_dn_norm_w, m_at_norm_w, m_w_out, m_final_norm_w, v_w_mod, v_b_mod, v_norm_w, v_w_in, v_conv_w, v_a_log, v_dt_bias, v_dn_norm_w, v_at_norm_w, v_w_out, v_final_norm_w):
    me = 4 * lax.axis_index("x") + 2 * lax.axis_index("y") + lax.axis_index("c")
    s = x.shape[1]

    g_mod, g_in, g_conv, g_out = _all_gather(
        [_bf(w_mod[0]), _bf(w_in[0]), conv_w[0], _bf(w_out[0])], "gather_weights")
    w_mod_bf = g_mod.transpose(1, 0, 2).reshape(D_MODEL, 3 * D_MODEL)
    w_in_bf = g_in.transpose(1, 0, 2).reshape(D_MODEL, IN_COLS)
    conv_full = g_conv.transpose(1, 0, 2).reshape(CONV_K, 3 * DN_WIDTH)
    w_out_bf = g_out.reshape(D_MODEL, D_MODEL)

    loss, gw_in, gw_out, small, input_grad = _local_step(
        x[0], c, positions[0], w_mod_bf, b_mod, norm_w, w_in_bf, conv_full, a_log, dt_bias, dn_norm_w, at_norm_w,
        w_out_bf, final_norm_w.reshape(1, D_MODEL), loss_target[0])

    gw_in_slabs = _bf(gw_in).reshape(D_MODEL, N_DEV, IN_SHARD).transpose(1, 0, 2)
    gw_out_slabs = _bf(gw_out).reshape(N_DEV, D_MODEL // N_DEV, D_MODEL)
    send_sems, recv_sems, srcs, lands, token = _scatter_start([gw_in_slabs, gw_out_slabs])
    gx, dmod, dnw = input_grad(token[0, 0])
    r_in, r_out = _scatter_wait(send_sems, recv_sems, srcs, lands, gx)
    own_in = lax.dynamic_index_in_dim(gw_in_slabs, me, 0, keepdims=False)
    own_out = lax.dynamic_index_in_dim(gw_out_slabs, me, 0, keepdims=False)

    pack = jnp.concatenate([small["conv"].reshape(1, -1), dmod, small["siluc"], dnw, small["dfw"],
                            small["alog"], small["dtb"], small["dnn"], small["atn"],
                            jnp.pad(loss, ((0, 0), (0, LANES - 1)))], axis=1).reshape(PK_ROWS, LANES)
    (pack_all,) = _exchange([pack], [False], "exchange_small")

    res = {}
    res["w_in"] = _adamw(w_in[0], m_w_in[0], v_w_in[0], r_in, "adamw_w_in", own=own_in)
    res["w_out"] = _adamw(w_out[0], m_w_out[0], v_w_out[0], r_out, "adamw_w_out", own=own_out)
    flat_all = pack_all.reshape(N_DEV, PK_END)
    dmod_mine = lax.dynamic_slice(flat_all, (0, PK_DMOD + me * (3 * D_MODEL // N_DEV)), (N_DEV, 3 * D_MODEL // N_DEV))
    res["w_mod"] = _adamw_w_mod(w_mod[0], m_w_mod[0], v_w_mod[0], flat_all[:, PK_SILUC:PK_DNW], dmod_mine)
    tot = _pack_sum(pack_all).reshape(1, PK_END)
    g_conv_full = tot[:, PK_CONV:PK_DMOD].reshape(CONV_K, 3 * DN_WIDTH)
    g_conv_mine = lax.dynamic_slice(g_conv_full, (0, me * (3 * DN_WIDTH // N_DEV)), (CONV_K, 3 * DN_WIDTH // N_DEV))
    res["conv_w"] = _adamw(conv_w[0], m_conv_w[0], v_conv_w[0], g_conv_mine, "adamw_conv_w")
    res["b_mod"] = _adamw(b_mod, m_b_mod, v_b_mod, tot[:, PK_DMOD:PK_SILUC], "adamw_b_mod")
    res["norm_w"] = _adamw(norm_w, m_norm_w, v_norm_w, tot[:, PK_DNW:PK_DFW], "adamw_norm_w")
    res["a_log"] = _adamw(a_log, m_a_log, v_a_log, tot[:, PK_ALOG + DN_HEADS:PK_ALOG + 2 * DN_HEADS], "adamw_a_log")
    res["dt_bias"] = _adamw(dt_bias, m_dt_bias, v_dt_bias, tot[:, PK_DTB + DN_HEADS:PK_DTB + 2 * DN_HEADS],
                            "adamw_dt_bias")
    res["dn_norm_w"] = _adamw(dn_norm_w, m_dn_norm_w, v_dn_norm_w, tot[:, PK_DNN:PK_ATN], "adamw_dn_norm_w")
    g_atn = tot[:, PK_ATN:PK_ATN + AT_DIM] + tot[:, PK_ATN + AT_DIM:PK_LOSS]
    res["at_norm_w"] = _adamw(at_norm_w, m_at_norm_w, v_at_norm_w, g_atn, "adamw_at_norm_w")
    fin = _adamw(final_norm_w.reshape(1, D_MODEL), m_final_norm_w.reshape(1, D_MODEL),
                 v_final_norm_w.reshape(1, D_MODEL), tot[:, PK_DFW:PK_ALOG], "adamw_final_norm_w")
    res["final_norm_w"] = tuple(a.reshape(D_MODEL) for a in fin)

    lead = ("w_mod", "w_in", "conv_w", "w_out")
    names = ("w_mod", "b_mod", "norm_w", "w_in", "conv_w", "a_log", "dt_bias", "dn_norm_w", "at_norm_w", "w_out",
             "final_norm_w")
    out = [tot[0, PK_LOSS], gx.reshape(1, s, D_MODEL)]
    for kind in range(4):
        for nm in names:
            a = res[nm][kind]
            out.append(a[None] if nm in lead else a)
    return tuple(out)
```

```python
import functools

import jax
import jax.numpy as jnp
from jax import lax
from jax.experimental import pallas as pl
from jax.experimental.pallas import tpu as pltpu

F32, BF16 = jnp.float32, jnp.bfloat16
HI = lax.Precision.HIGHEST
SDS = jax.ShapeDtypeStruct

D_MODEL = 1024
DN_HEADS, DN_DIM, DN_WIDTH = 4, 128, 512
AT_HEADS, AT_DIM, AT_WIDTH = 8, 64, 512
CONV_K = 4
CHUNK = 64
Q_BLOCK = 128
W_SUB = 128
DILATIONS = (1, 4, 16)
AT_PAIRS = 4
ATT_BLK = Q_BLOCK * max(DILATIONS)
ATT_UNROLL, ATT_UNROLL_BWD = 8, 4
CH_UNROLL = 4
ROPE_THETA = 10000.0
EPS = 1e-6
N_DEV = 8
LANES = 128
BA_PAD = 128
IN_SPLITS = (1536, 512, 4, 4, 512, 512, 512, 512)
IN_COLS = sum(IN_SPLITS)
IN_SHARD = IN_COLS // N_DEV
VMEM_LIMIT = 56 * 2 ** 20

ADAM_LR, ADAM_B1, ADAM_B2, ADAM_EPS, ADAM_WD, ADAM_STEP = 0.001, 0.9, 0.999, 1e-08, 0.01, 10

PK_CONV, PK_DMOD, PK_SILUC, PK_DNW, PK_DFW, PK_ALOG, PK_DTB, PK_DNN, PK_ATN, PK_LOSS, PK_END = (
    0, 6144, 9216, 10240, 11264, 12288, 12416, 12544, 12672, 12800, 12928)
PK_ROWS = PK_END // LANES

_NT = (((1,), (1,)), ((), ()))
_TN = (((0,), (0,)), ((), ()))


def _params(*sem):
    return pltpu.CompilerParams(dimension_semantics=sem or None, vmem_limit_bytes=VMEM_LIMIT)


def _bf(x):
    return x.astype(BF16)


def _nn(a, b):
    return jnp.dot(_bf(a), _bf(b), preferred_element_type=F32)


def _nt(a, b):
    return lax.dot_general(_bf(a), _bf(b), _NT, preferred_element_type=F32)


def _tn(a, b):
    return lax.dot_general(_bf(a), _bf(b), _TN, preferred_element_type=F32)


def _htn(a, b):
    return lax.dot_general(a, b, _TN, precision=HI, preferred_element_type=F32)


def _head_sum(x):
    r = lax.broadcasted_iota(jnp.int32, (LANES, LANES), 0)
    c = lax.broadcasted_iota(jnp.int32, (LANES, LANES), 1)
    same = jnp.where((r // AT_DIM) == (c // AT_DIM), 1.0, 0.0).astype(BF16)
    hi, lo = _hl(x)
    return jnp.dot(hi, same, preferred_element_type=F32) + jnp.dot(lo, same, preferred_element_type=F32)


@jax.custom_vjp
def _d_head_sum(x):
    return _head_sum(x)


_d_head_sum.defvjp(lambda x: (_head_sum(x), None), lambda _, g: (_head_sum(g),))


def _silu(x):
    return x * jax.nn.sigmoid(x)


def _softplus(x):
    return jnp.maximum(x, 0.0) + jnp.log(1.0 + jnp.exp(-jnp.abs(x)))


def _l2n(x):
    return x * lax.rsqrt(jnp.sum(x * x, axis=-1, keepdims=True) + EPS)


def _post_q(x):
    return _l2n(_silu(x)) * (DN_DIM ** -0.5)


def _post_k(x):
    return _l2n(_silu(x))


def _post_v(x):
    return _silu(x)


def _beta_decay(ba, alog_row, dtb_row):
    lane = lax.broadcasted_iota(jnp.int32, ba.shape, 1)
    return jnp.where(lane < DN_HEADS, jax.nn.sigmoid(ba), -jnp.exp(alog_row) * _softplus(ba + dtb_row))


def _gate_dn(o, z, w):
    return (o * lax.rsqrt(jnp.mean(o * o, axis=-1, keepdims=True) + EPS)) * w * _silu(z)


def _gate_at(o, z, w2, head_sum):
    ms = head_sum(o * o) * (1.0 / AT_DIM)
    return (o * lax.rsqrt(ms + EPS)) * w2 * _silu(z)


def _swap_half64(x):
    lane = lax.broadcasted_iota(jnp.int32, x.shape, 1)
    return jnp.where((lane & (AT_DIM - 1)) < AT_DIM // 2, pltpu.roll(x, LANES - AT_DIM // 2, 1),
                     pltpu.roll(x, AT_DIM // 2, 1))


_NN = (((1,), (0,)), ((), ()))


def _hl(a):
    hi = a.astype(BF16)
    return hi, (a - hi.astype(F32)).astype(BF16)


def _mm3(a, b, dims=_NN):
    (ah, al), (bh, bl) = a, b
    f = lambda x, y: lax.dot_general(x, y, dims, preferred_element_type=F32)
    return f(ah, bh) + (f(ah, bl) + f(al, bh))


def _chunk_masks():
    r = lax.broadcasted_iota(jnp.int32, (CHUNK, CHUNK), 0)
    c = lax.broadcasted_iota(jnp.int32, (CHUNK, CHUNK), 1)
    return r >= c, r > c, (r == c).astype(F32), (r // 16) == (c // 16)


def _tri_inv(mats):
    _, _, eye, blk = _chunk_masks()
    dg = [jnp.where(blk, a, 0.0) for a in mats]
    lo = [jnp.where(blk, 0.0, a) for a in mats]
    sdg = [_hl(x) for x in dg]
    d2 = [_mm3(s, s) for s in sdg]
    sd2 = [_hl(x) for x in d2]
    d4 = [_mm3(s, s) for s in sd2]
    sd4 = [_hl(x) for x in d4]
    d8 = [_mm3(s, s) for s in sd4]
    p1 = [_mm3(_hl(eye - a), _hl(eye + b)) for a, b in zip(dg, d2)]
    p2 = [_mm3(_hl(a), _hl(eye + b)) for a, b in zip(p1, d4)]
    dinv = [_mm3(_hl(a), _hl(eye + b)) for a, b in zip(p2, d8)]
    sdinv = [_hl(x) for x in dinv]
    n1 = [_mm3(s, _hl(b)) for s, b in zip(sdinv, lo)]
    sn1 = [_hl(x) for x in n1]
    n2 = [_mm3(s, s) for s in sn1]
    q1 = [_mm3(_hl(eye - a), _hl(eye + b)) for a, b in zip(n1, n2)]
    return [_mm3(_hl(a), s) for a, s in zip(q1, sdinv)]


def _chunk_common(qs, ks, vs, betas, gcs):
    tril, _, _, _ = _chunk_masks()
    out = []
    for q, k, v, beta, gc in zip(qs, ks, vs, betas, gcs):
        gb = jnp.broadcast_to(gc, (CHUNK, DN_DIM))
        gt = gb.T[:CHUNK, :]
        gam = jnp.where(tril, jnp.exp(jnp.where(tril, gb[:, :CHUNK] - gt, 0.0)), 0.0)
        last = gb[CHUNK - 1:CHUNK, :]
        eg, e2 = jnp.exp(gb), jnp.exp(last - gb)
        kb, vb = k * beta, v * beta
        out.append(dict(gam=gam, eg=eg, e2=e2, gl=jnp.exp(last[:, 0:1]), kb=kb, vb=vb, kbg=kb * eg,
                        m=_nt(kb, k), qk=_nt(q, k)))
    return out


def _chunk_fwd(qs, ks, vs, betas, gcs):
    tril, strict, _, _ = _chunk_masks()
    cm = _chunk_common(qs, ks, vs, betas, gcs)
    ts = _tri_inv([jnp.where(strict, c["m"] * c["gam"], 0.0) for c in cm])
    outs = []
    for q, k, c, t in zip(qs, ks, cm, ts):
        uw = _nn(t, jnp.concatenate([c["vb"], c["kbg"]], axis=1))
        p = jnp.where(tril, c["qk"] * c["gam"], 0.0)
        outs.append((uw[:, :DN_DIM], uw[:, DN_DIM:], p, q * c["eg"], k * c["e2"], c["gl"], t.T))
    return outs


def _chunk_bwd(qs, ks, vs, betas, gcs, ts, cots):
    tril, strict, _, _ = _chunk_masks()
    cm = _chunk_common(qs, ks, vs, betas, gcs)
    row = lax.broadcasted_iota(jnp.int32, (CHUNK, 1), 0)
    ones = jnp.ones((CHUNK, DN_DIM), BF16)
    rs = lambda x: jnp.sum(x, axis=-1, keepdims=True)
    tts = [_bf(t) for t in ts]
    duw = [_bf(jnp.concatenate([ct[0], ct[1]], axis=1)) for ct in cots]
    dts = [_nt(a, jnp.concatenate([c["vb"], c["kbg"]], axis=1)) for a, c in zip(duw, cm)]
    xs = [_nn(t, d) for t, d in zip(tts, dts)]
    das = [jnp.where(strict, -_nn(x, t), 0.0) for x, t in zip(xs, tts)]
    dvks = [_nn(t, a) for t, a in zip(tts, duw)]
    outs = []
    for q, k, v, beta, c, ct, da, dvk in zip(qs, ks, vs, betas, cm, cots, das, dvks):
        _, _, dp, dqd, dkd, dgl = ct
        dvb, dkbg = dvk[:, :DN_DIM], dvk[:, DN_DIM:]
        dm = da * c["gam"]
        dqk = jnp.where(tril, dp, 0.0) * c["gam"]
        e = dm * c["m"] + dqk * c["qk"]
        dmq = jnp.concatenate([dm, dqk], axis=0)
        r1 = _nn(dmq, k)
        dkb = r1[:CHUNK] + dkbg * c["eg"]
        dq = r1[CHUNK:] + dqd * c["eg"]
        dk = _tn(dmq, jnp.concatenate([c["kb"], q], axis=0)) + dkd * c["e2"] + dkb * beta
        dbeta = rs(dkb * k + dvb * v)
        eh, el = _hl(e)
        colsum = (lax.dot_general(eh, ones, _TN, preferred_element_type=F32)
                  + lax.dot_general(el, ones, _TN, preferred_element_type=F32))[:, 0:1]
        pkd = dkd * (k * c["e2"])
        dgc = rs(e) - colsum + rs(dqd * q * c["eg"] + dkbg * c["kbg"] - pkd)
        tail = rs(jnp.sum(pkd, axis=0, keepdims=True)) + dgl * c["gl"]
        dgc = dgc + jnp.where(row == CHUNK - 1, tail, 0.0)
        outs.append((dq, dk, dvb * beta, dbeta, dgc))
    return outs


def _chunk_cumsum(x, reverse=False):
    n = x.shape[0]
    pos = lax.broadcasted_iota(jnp.int32, x.shape, 0) & (CHUNK - 1)
    sh = 1
    while sh < CHUNK:
        if reverse:
            x = x + jnp.where(pos < CHUNK - sh, pltpu.roll(x, n - sh, 0), 0.0)
        else:
            x = x + jnp.where(pos >= sh, pltpu.roll(x, sh, 0), 0.0)
        sh *= 2
    return x


GC_LANE = 2 * DN_HEADS


def _exchange(arrays, scatter, name):
    n = len(arrays)
    out_shapes = []
    for a, sc in zip(arrays, scatter):
        out_shapes.append(SDS(a.shape if sc else (N_DEV,) + a.shape, a.dtype))

    def body(*refs):
        ins, outs = refs[:n], refs[n:2 * n]
        send_sems, recv_sems, loc_sems = refs[2 * n:]
        x, y, c = lax.axis_index("x"), lax.axis_index("y"), lax.axis_index("c")
        me = 4 * x + 2 * y + c
        local, remote = [], []
        for i in range(n):
            src = ins[i].at[me] if scatter[i] else ins[i]
            cp = pltpu.make_async_copy(src, outs[i].at[me], loc_sems.at[i])
            cp.start()
            local.append(cp)
        for dlt in range(1, N_DEV):
            px = 1 - x if dlt & 4 else x
            py = 1 - y if dlt & 2 else y
            pc = 1 - c if dlt & 1 else c
            peer = 4 * px + 2 * py + pc
            for i in range(n):
                src = ins[i].at[peer] if scatter[i] else ins[i]
                cp = pltpu.make_async_remote_copy(
                    src_ref=src, dst_ref=outs[i].at[me],
                    send_sem=send_sems.at[i, dlt - 1], recv_sem=recv_sems.at[i, dlt - 1],
                    device_id=(px, py, pc), device_id_type=pl.DeviceIdType.MESH)
                cp.start()
                arrive = pltpu.make_async_remote_copy(
                    src_ref=src, dst_ref=outs[i].at[peer],
                    send_sem=send_sems.at[i, dlt - 1], recv_sem=recv_sems.at[i, dlt - 1],
                    device_id=(px, py, pc), device_id_type=pl.DeviceIdType.MESH)
                remote.append((cp, arrive))
        for cp, arrive in remote:
            cp.wait_send()
            arrive.wait_recv()
        for cp in local:
            cp.wait()

    any_spec = pl.BlockSpec(memory_space=pl.ANY)
    return pl.pallas_call(
        body, name=name, out_shape=tuple(out_shapes),
        in_specs=[any_spec] * n, out_specs=tuple([any_spec] * n),
        scratch_shapes=[pltpu.SemaphoreType.DMA((n, N_DEV - 1)), pltpu.SemaphoreType.DMA((n, N_DEV - 1)),
                        pltpu.SemaphoreType.DMA((n,))],
    )(*arrays)


def _all_gather(arrays, name):
    n = len(arrays)

    def body(*refs):
        ins, outs = refs[:n], refs[n:2 * n]
        send_sems, recv_sems, loc_sems = refs[2 * n:]
        x, y, c = lax.axis_index("x"), lax.axis_index("y"), lax.axis_index("c")
        me, sibling = (x, y, c), (x, y, 1 - c)
        chips = [(1 - x, y), (x, 1 - y), (1 - x, 1 - y)]

        def copy(i, k, block, to, src=None):
            slot = outs[i].at[4 * block[0] + 2 * block[1] + block[2]]
            return pltpu.make_async_remote_copy(
                src_ref=slot if src is None else src, dst_ref=slot,
                send_sem=send_sems.at[i, k], recv_sem=recv_sems.at[i, k],
                device_id=to, device_id_type=pl.DeviceIdType.MESH)

        mine = [pltpu.make_async_copy(ins[i], outs[i].at[4 * x + 2 * y + c], loc_sems.at[i]) for i in range(n)]
        for cp in mine:
            cp.start()
        first = []
        for i in range(n):
            first.append(copy(i, 0, me, sibling, src=ins[i]))
            first += [copy(i, 1 + j, me, (*chip, c), src=ins[i]) for j, chip in enumerate(chips)]
        for cp in first:
            cp.start()
        passed = []
        for j, chip in enumerate(chips):
            for i in range(n):
                copy(i, 1 + j, (*chip, c), me).wait_recv()
                fwd = copy(i, 4 + j, (*chip, c), sibling)
                fwd.start()
                passed.append(fwd)
        for i in range(n):
            copy(i, 0, sibling, me).wait_recv()
        for j, chip in enumerate(chips):
            for i in range(n):
                copy(i, 4 + j, (*chip, 1 - c), me).wait_recv()
        for cp in first + passed:
            cp.wait_send()
        for cp in mine:
            cp.wait()

    any_spec = pl.BlockSpec(memory_space=pl.ANY)
    return pl.pallas_call(
        body, name=name, out_shape=tuple(SDS((N_DEV,) + a.shape, a.dtype) for a in arrays),
        in_specs=[any_spec] * n, out_specs=tuple([any_spec] * n),
        scratch_shapes=[pltpu.SemaphoreType.DMA((n, N_DEV - 1)), pltpu.SemaphoreType.DMA((n, N_DEV - 1)),
                        pltpu.SemaphoreType.DMA((n,))],
    )(*arrays)


_HBM = pl.BlockSpec(memory_space=pltpu.HBM)
_SEM = pl.BlockSpec(memory_space=pltpu.SEMAPHORE)


def _peers(x, y, c):
    out = []
    for dlt in range(1, N_DEV):
        px = 1 - x if dlt & 4 else x
        py = 1 - y if dlt & 2 else y
        pc = 1 - c if dlt & 1 else c
        out.append((dlt, (px, py, pc), 4 * px + 2 * py + pc))
    return out


def _scatter_start(arrays):
    n = len(arrays)
    ns = n * (N_DEV - 1)

    def body(*refs):
        ins, lands = refs[:n], refs[n:2 * n]
        send_sems, recv_sems = refs[2 * n:2 * n + ns], refs[2 * n + ns:2 * n + 2 * ns]
        token = refs[-1]
        x, y, c = lax.axis_index("x"), lax.axis_index("y"), lax.axis_index("c")
        me = 4 * x + 2 * y + c
        for dlt, peer, pi in _peers(x, y, c):
            for i in range(n):
                k = i * (N_DEV - 1) + dlt - 1
                pltpu.make_async_remote_copy(
                    src_ref=ins[i].at[pi], dst_ref=lands[i].at[me], send_sem=send_sems[k], recv_sem=recv_sems[k],
                    device_id=peer, device_id_type=pl.DeviceIdType.MESH).start()
        token[...] = jnp.zeros_like(token)

    sem = pltpu.SemaphoreType.DMA(())
    thru = tuple(pltpu.HBM(a.shape, a.dtype) for a in arrays)
    hbm = lambda a: pltpu.with_memory_space_constraint(a, pltpu.HBM)
    outs = pl.pallas_call(
        body, name="scatter_start", out_shape=(sem,) * (2 * ns) + thru + thru + (SDS((8, LANES), F32),),
        in_specs=[_HBM] * (2 * n),
        out_specs=(_SEM,) * (2 * ns) + (_HBM,) * (2 * n) + (pl.BlockSpec(memory_space=pltpu.VMEM),),
        input_output_aliases={i: 2 * ns + i for i in range(2 * n)},
        compiler_params=pltpu.CompilerParams(has_side_effects=pltpu.SideEffectType.DATAFLOW_SIDE_EFFECTING),
    )(*[hbm(a) for a in arrays], *[hbm(jnp.zeros(a.shape, a.dtype)) for a in arrays])
    return outs[:ns], outs[ns:2 * ns], outs[2 * ns:2 * ns + n], outs[2 * ns + n:2 * ns + 2 * n], outs[-1]


def _scatter_wait(send_sems, recv_sems, srcs, lands, after):
    n = len(srcs)
    ns = n * (N_DEV - 1)

    def body(*refs):
        ins, lands_ = refs[:n], refs[n:2 * n]
        send, recv = refs[2 * n:2 * n + ns], refs[2 * n + ns:2 * n + 2 * ns]
        x, y, c = lax.axis_index("x"), lax.axis_index("y"), lax.axis_index("c")
        for dlt, peer, pi in _peers(x, y, c):
            for i in range(n):
                k = i * (N_DEV - 1) + dlt - 1
                cp = pltpu.make_async_remote_copy(
                    src_ref=ins[i].at[pi], dst_ref=lands_[i].at[pi], send_sem=send[k], recv_sem=recv[k],
                    device_id=peer, device_id_type=pl.DeviceIdType.MESH)
                cp.wait_send()
                cp.wait_recv()

    thru = tuple(pltpu.HBM(a.shape, a.dtype) for a in srcs)
    outs = pl.pallas_call(
        body, name="scatter_wait", out_shape=thru + thru,
        in_specs=[_HBM] * (2 * n) + [_SEM] * (2 * ns) + [pl.BlockSpec(memory_space=pl.ANY)],
        out_specs=(_HBM,) * (2 * n), input_output_aliases={i: i for i in range(2 * n)},
        compiler_params=pltpu.CompilerParams(has_side_effects=pltpu.SideEffectType.DATAFLOW_SIDE_EFFECTING),
    )(*srcs, *lands, *send_sems, *recv_sems, after)
    return outs[n:]


def _adaln_mod(c, w_mod, b_mod):
    def body(c_ref, w_ref, b_ref, mod_ref, sc_ref):
        sc = _silu(c_ref[...])
        sc8 = jnp.broadcast_to(sc, (8, D_MODEL))
        mod_ref[...] = _nn(sc8, w_ref[...])[0:1] + b_ref[...]
        sc_ref[...] = sc

    return pl.pallas_call(body, name="adaln_mod", compiler_params=_params(),
                          out_shape=(SDS((1, 3 * D_MODEL), F32), SDS((1, D_MODEL), F32)))(c, w_mod, b_mod)


def _ln_proj(x, mod, norm_w, ws, cos_t, sin_t, conv_w8, alog_row, dtb_row, ts):
    s = x.shape[0]
    widths = [w.shape[1] for w in ws]

    def body(x_ref, mod_ref, nw_ref, cos_ref, sin_ref, cw_ref, al_ref, dtb_ref, wqkv, wz, wba, waq, wak, wav, waz,
             h_ref, oqkv, oz, oba, oq, ok, ov, oaz, q_ref, k_ref, v_ref, bg_ref, halo):
        n = pl.program_id(0)
        xt = x_ref[...]
        r = lax.rsqrt(jnp.mean(xt * xt, axis=-1, keepdims=True) + EPS)
        shift, scale = mod_ref[:, 0:D_MODEL], mod_ref[:, D_MODEL:2 * D_MODEL]
        h = ((xt * r) * nw_ref[...]) * (1.0 + scale) + shift
        hb = _bf(h)
        h_ref[...] = hb
        pre = jnp.dot(hb, wqkv[...], preferred_element_type=F32)
        oqkv[...] = pre
        ext = jnp.concatenate([jnp.where(n == 0, 0.0, halo[...]), pre], axis=0)
        halo[...] = pre[ts - 8:ts]
        taps = _conv_taps(ext, ts)
        conv = taps[0] * cw_ref[0:1, :]
        for j in range(1, CONV_K):
            conv = conv + taps[j] * cw_ref[j:j + 1, :]
        for hd in range(DN_HEADS):
            cols = slice(hd * DN_DIM, (hd + 1) * DN_DIM)
            q_ref[:, cols] = _post_q(conv[:, hd * DN_DIM:(hd + 1) * DN_DIM])
            k_ref[:, cols] = _post_k(conv[:, DN_WIDTH + hd * DN_DIM:DN_WIDTH + (hd + 1) * DN_DIM])
            v_ref[:, cols] = _post_v(conv[:, 2 * DN_WIDTH + hd * DN_DIM:2 * DN_WIDTH + (hd + 1) * DN_DIM])
        ba = jnp.dot(hb, wba[...], preferred_element_type=F32)
        oba[...] = ba
        bg = _beta_decay(ba, al_ref[...], dtb_ref[...])
        lane = lax.broadcasted_iota(jnp.int32, bg.shape, 1)
        run = pltpu.roll(_chunk_cumsum(bg), DN_HEADS, 1)
        bg_ref[...] = jnp.where((lane >= GC_LANE) & (lane < GC_LANE + DN_HEADS), run, bg)
        oz[...] = jnp.dot(hb, wz[...], preferred_element_type=F32)
        oaz[...] = jnp.dot(hb, waz[...], preferred_element_type=F32)
        tv = jnp.dot(hb, wav[...], preferred_element_type=F32)
        for j in range(AT_PAIRS):
            ov[j] = tv[:, j * LANES:(j + 1) * LANES]
        cs, sn = cos_ref[...], sin_ref[...]
        for w_ref, o_ref in ((waq, oq), (wak, ok)):
            t = jnp.dot(hb, w_ref[...], preferred_element_type=F32)
            for j in range(AT_PAIRS):
                tj = t[:, j * LANES:(j + 1) * LANES]
                o_ref[j] = tj * cs + _swap_half64(tj) * sn

    tok = lambda w: pl.BlockSpec((ts, w), lambda i: (i, 0))
    full = lambda a: pl.BlockSpec(a.shape, lambda i: (0, 0))
    pairs = pl.BlockSpec((AT_PAIRS, ts, LANES), lambda i: (0, i, 0))
    return pl.pallas_call(
        body, name="ln_proj", grid=(s // ts,), compiler_params=_params("arbitrary"),
        in_specs=[tok(D_MODEL), full(mod), full(norm_w), tok(LANES), tok(LANES), full(conv_w8), full(alog_row),
                  full(dtb_row)] + [full(w) for w in ws],
        out_specs=(tok(D_MODEL), tok(widths[0]), tok(widths[1]), tok(widths[2]), pairs, pairs, pairs,
                   tok(widths[6]), tok(DN_WIDTH), tok(DN_WIDTH), tok(DN_WIDTH), tok(BA_PAD)),
        out_shape=(SDS((s, D_MODEL), BF16), SDS((s, widths[0]), F32), SDS((s, widths[1]), F32),
                   SDS((s, widths[2]), F32)) + (SDS((AT_PAIRS, s, LANES), F32),) * 3 + (SDS((s, widths[6]), F32),)
        + (SDS((s, DN_WIDTH), F32),) * 3 + (SDS((s, BA_PAD), F32),),
        scratch_shapes=[pltpu.VMEM((8, widths[0]), F32)],
    )(x, mod, norm_w, cos_t, sin_t, conv_w8, alog_row, dtb_row, *ws)


def _conv_taps(ext, rows):
    taps = []
    for j in range(CONV_K):
        sh = CONV_K - 1 - j
        rolled = pltpu.roll(ext, sh, 0) if sh else ext
        taps.append(rolled[8:8 + rows])
    return taps


def _dn_chunk_prep(q, k, v, bg, ts):
    s = q.shape[0]
    ncs = ts // CHUNK

    def body(q_ref, k_ref, v_ref, bg_ref, u_ref, w_ref, qd_ref, kd_ref, p_ref, gl_ref, t_ref):
        def chunks(cg, carry):
            where = []
            for ci in (cg * CH_UNROLL + i for i in range(CH_UNROLL)):
                rows = pl.ds(pl.multiple_of(ci * CHUNK, CHUNK), CHUNK)
                rows8 = pl.ds(pl.multiple_of(ci * 8, 8), 8)
                where += [(rows, rows8, h, slice(h * DN_DIM, (h + 1) * DN_DIM)) for h in range(DN_HEADS)]
            bgs = [bg_ref[rows, :] for rows, _, _, _ in where]
            outs = _chunk_fwd([q_ref[rows, c] for rows, _, _, c in where], [k_ref[rows, c] for rows, _, _, c in where],
                              [v_ref[rows, c] for rows, _, _, c in where],
                              [b[:, h:h + 1] for b, (_, _, h, _) in zip(bgs, where)],
                              [b[:, GC_LANE + h:GC_LANE + h + 1] for b, (_, _, h, _) in zip(bgs, where)])
            for (rows, rows8, h, c), (u, w, p, qd, kd, gl, t) in zip(where, outs):
                u_ref[rows, c] = u
                w_ref[rows, c] = w
                qd_ref[rows, c] = qd
                kd_ref[rows, c] = kd
                p_ref[h, rows, :] = p
                t_ref[h, rows, :] = t
                gl_ref[rows8, c] = jnp.broadcast_to(gl, (8, DN_DIM))
            return carry

        lax.fori_loop(0, ncs // CH_UNROLL, chunks, 0)

    tok = lambda w: pl.BlockSpec((ts, w), lambda i: (i, 0))
    sq = pl.BlockSpec((DN_HEADS, ts, CHUNK), lambda i: (0, i, 0))
    return pl.pallas_call(
        body, name="dn_chunk_prep", grid=(s // ts,), compiler_params=_params("arbitrary"),
        in_specs=[tok(DN_WIDTH)] * 3 + [tok(BA_PAD)],
        out_specs=(tok(DN_WIDTH),) * 4 + (sq, pl.BlockSpec((ncs * 8, DN_WIDTH), lambda i: (i, 0)), sq),
        out_shape=(SDS((s, DN_WIDTH), F32),) * 4 + (SDS((DN_HEADS, s, CHUNK), F32),
                                                     SDS((s // CHUNK * 8, DN_WIDTH), F32),
                                                     SDS((DN_HEADS, s, CHUNK), F32)),
    )(q, k, v, bg)


def _dn_scan(u, w, qd, kd, p, gl, ts):
    s = u.shape[0]
    ncs = ts // CHUNK

    def body(u_ref, w_ref, qd_ref, kd_ref, p_ref, gl_ref, o_ref, vn_ref, st_ref, state):
        @pl.when(pl.program_id(0) == 0)
        def _():
            state[...] = jnp.zeros_like(state)

        def chunk(ci, carry):
            rows = pl.ds(pl.multiple_of(ci * CHUNK, CHUNK), CHUNK)
            rows8 = pl.ds(pl.multiple_of(ci * 8, 8), 8)
            srows = pl.ds(pl.multiple_of(ci * DN_DIM, DN_DIM), DN_DIM)
            hs = range(DN_HEADS)
            sl = [slice(h * DN_DIM, (h + 1) * DN_DIM) for h in hs]
            sf = [state[h] for h in hs]
            sb = [_bf(x) for x in sf]
            ws = [_nn(w_ref[rows, c], b) for c, b in zip(sl, sb)]
            qs = [_nn(qd_ref[rows, c], b) for c, b in zip(sl, sb)]
            vn = [u_ref[rows, c] - x for c, x in zip(sl, ws)]
            vb = [_bf(x) for x in vn]
            kv = [_tn(kd_ref[rows, c], b) for c, b in zip(sl, vb)]
            pv = [_nn(p_ref[h, rows, :], b) for h, b in zip(hs, vb)]
            for h in hs:
                state[h] = sf[h] * gl_ref[rows8, sl[h]][0:1] + kv[h]
            for h in hs:
                st_ref[srows, sl[h]] = sf[h]
                vn_ref[rows, sl[h]] = vn[h]
                o_ref[rows, sl[h]] = qs[h] + pv[h]
            return carry

        lax.fori_loop(0, ncs, chunk, 0)

    tok = lambda wd: pl.BlockSpec((ts, wd), lambda i: (i, 0))
    return pl.pallas_call(
        body, name="dn_scan", grid=(s // ts,), compiler_params=_params("arbitrary"),
        in_specs=[tok(DN_WIDTH)] * 4 + [pl.BlockSpec((DN_HEADS, ts, CHUNK), lambda i: (0, i, 0)),
                                        pl.BlockSpec((ncs * 8, DN_WIDTH), lambda i: (i, 0))],
        out_specs=(tok(DN_WIDTH), tok(DN_WIDTH), pl.BlockSpec((ncs * DN_DIM, DN_WIDTH), lambda i: (i, 0))),
        out_shape=(SDS((s, DN_WIDTH), F32), SDS((s, DN_WIDTH), F32), SDS((s // CHUNK * DN_DIM, DN_WIDTH), F32)),
        scratch_shapes=[pltpu.VMEM((DN_HEADS, DN_DIM, DN_DIM), F32)],
    )(u, w, qd, kd, p, gl)


LOG2E, LN2 = 1.4426950408889634, 0.6931471805599453
MASKED = -1e30


def _band_bias():
    qi = lax.broadcasted_iota(jnp.int32, (Q_BLOCK, 2 * Q_BLOCK), 0)
    kj = lax.broadcasted_iota(jnp.int32, (Q_BLOCK, 2 * Q_BLOCK), 1)
    rel = Q_BLOCK + qi - kj
    return jnp.where((rel >= 0) & (rel <= W_SUB), 0.0, MASKED)


def _first_bias(first):
    kj = lax.broadcasted_iota(jnp.int32, (1, 2 * Q_BLOCK), 1)
    return jnp.where((kj < Q_BLOCK) & first, MASKED, 0.0)


def _attn_combo(c, d):
    if d == 1:
        qs = pl.multiple_of(c * Q_BLOCK, Q_BLOCK)
        return qs, pl.multiple_of(ATT_BLK - Q_BLOCK + c * Q_BLOCK, Q_BLOCK), c == 0
    r, m = c % d, c // d
    qs = r + (d * Q_BLOCK) * m
    return qs, ATT_BLK + qs - d * Q_BLOCK, m == 0


def _rows(start, size, d):
    return pl.ds(pl.multiple_of(start, Q_BLOCK), size) if d == 1 else pl.ds(start, size, stride=d)


def _shift_in(ext, cur, n):
    @pl.when(n == 0)
    def _():
        ext[0:ATT_BLK, :] = jnp.zeros((ATT_BLK, LANES), F32)

    @pl.when(n > 0)
    def _():
        ext[0:ATT_BLK, :] = ext[ATT_BLK:2 * ATT_BLK, :]

    ext[ATT_BLK:2 * ATT_BLK, :] = cur


def _attn_fwd(qr, kr, vv):
    s = qr.shape[1]
    nblk = s // ATT_BLK
    scale = AT_DIM ** -0.5
    npat = len(DILATIONS)

    def body(q_ref, k_ref, v_ref, o_ref, lse_ref, kext, vext, o_p, l_p, bias_ref):
        n = pl.program_id(1)
        _shift_in(kext, k_ref[0], n)
        _shift_in(vext, v_ref[0], n)
        bias_ref[...] = _band_bias()
        lo = lax.broadcasted_iota(jnp.int32, (Q_BLOCK, LANES), 1) < AT_DIM
        for pi, d in enumerate(DILATIONS):
            def group(g, carry, pi=pi, d=d):
                cs = [_attn_combo(g * ATT_UNROLL + u, d) for u in range(ATT_UNROLL)]
                heads = [(i, sel) for i in range(ATT_UNROLL) for sel in (lo, ~lo)]
                band = bias_ref[...]
                bias = [band + _first_bias((n == 0) & m0) for _, _, m0 in cs]
                qb = [_bf(q_ref[0, _rows(qs, Q_BLOCK, d), :]) for qs, _, _ in cs]
                kk = [_bf(kext[_rows(ks, 2 * Q_BLOCK, d), :]) for _, ks, _ in cs]
                vb = [_bf(vext[_rows(ks, 2 * Q_BLOCK, d), :]) for _, ks, _ in cs]
                sc = [lax.dot_general(jnp.where(sel, qb[i], jnp.zeros_like(qb[i])), kk[i], _NT,
                                      preferred_element_type=F32) for i, sel in heads]
                sc = [x * (scale * LOG2E) + bias[i] for x, (i, _) in zip(sc, heads)]
                mx = [jnp.max(x, axis=-1, keepdims=True) for x in sc]
                pr = [jnp.exp2(x - m) for x, m in zip(sc, mx)]
                ls = [jnp.sum(x, axis=-1, keepdims=True) for x in pr]
                pv = [jnp.dot(_bf(x), vb[i], preferred_element_type=F32) for x, (i, _) in zip(pr, heads)]
                outs = [x / l for x, l in zip(pv, ls)]
                lses = [m * LN2 + jnp.log(l) for m, l in zip(mx, ls)]
                for i, (qs, _, _) in enumerate(cs):
                    o_p[pi, _rows(qs, Q_BLOCK, d), :] = jnp.where(lo, outs[2 * i], outs[2 * i + 1])
                    l_p[pi, _rows(qs, Q_BLOCK, d), :] = jnp.where(lo, lses[2 * i], lses[2 * i + 1])
                return carry

            lax.fori_loop(0, ATT_BLK // Q_BLOCK // ATT_UNROLL, group, 0)

        def merge(i, carry):
            rows = pl.ds(pl.multiple_of(i * 256, 256), 256)
            ls = [l_p[pi, rows, :] for pi in range(npat)]
            mx = jnp.maximum(jnp.maximum(ls[0], ls[1]), ls[2])
            es = [jnp.exp(l - mx) for l in ls]
            den = es[0] + es[1] + es[2]
            o_ref[0, rows, :] = (es[0] * o_p[0, rows, :] + es[1] * o_p[1, rows, :] + es[2] * o_p[2, rows, :]) / den
            lse_ref[0, rows, :] = mx + jnp.log(den)
            return carry

        lax.fori_loop(0, ATT_BLK // 256, merge, 0)

    blk = pl.BlockSpec((1, ATT_BLK, LANES), lambda j, n: (j, n, 0))
    return pl.pallas_call(
        body, name="attn_fwd", grid=(AT_PAIRS, nblk), compiler_params=_params("arbitrary", "arbitrary"),
        in_specs=[blk] * 3, out_specs=(blk, blk),
        out_shape=(SDS((AT_PAIRS, s, LANES), F32),) * 2,
        scratch_shapes=[pltpu.VMEM((2 * ATT_BLK, LANES), F32), pltpu.VMEM((2 * ATT_BLK, LANES), F32),
                        pltpu.VMEM((npat, ATT_BLK, LANES), F32), pltpu.VMEM((npat, ATT_BLK, LANES), F32),
                        pltpu.VMEM((Q_BLOCK, 2 * Q_BLOCK), F32)],
    )(qr, kr, vv)


def _out_loss(o_dn, z_dn, o_at, z_at, dnw, atw2, x, tgt, w_out, gate, fw, ts):
    s = x.shape[0]

    def body(odn, zdn, oat, zat, dnw_ref, atw_ref, x_ref, t_ref, w_ref, g_ref, fw_ref,
             dx2_ref, dcat_ref, gw_ref, dfw_ref, dgate_ref, loss_ref):
        @pl.when(pl.program_id(0) == 0)
        def _():
            gw_ref[...] = jnp.zeros_like(gw_ref)
            dfw_ref[...] = jnp.zeros_like(dfw_ref)
            dgate_ref[...] = jnp.zeros_like(dgate_ref)
            loss_ref[...] = jnp.zeros_like(loss_ref)

        parts = [_bf(_gate_dn(odn[:, h * DN_DIM:(h + 1) * DN_DIM], zdn[:, h * DN_DIM:(h + 1) * DN_DIM], dnw_ref[...]))
                 for h in range(DN_HEADS)]
        parts += [_bf(_gate_at(oat[j], zat[:, j * LANES:(j + 1) * LANES], atw_ref[...], _head_sum))
                  for j in range(AT_PAIRS)]
        catb = jnp.concatenate(parts, axis=1)
        wb = w_ref[...]
        gate, fwv = g_ref[...], fw_ref[...]
        mix = jnp.dot(catb, wb, preferred_element_type=F32)
        x2 = x_ref[...] + gate * mix
        r2 = lax.rsqrt(jnp.mean(x2 * x2, axis=-1, keepdims=True) + EPS)
        xn2 = x2 * r2
        err = xn2 * fwv - t_ref[...]
        row = jnp.sum(err * err, axis=-1, keepdims=True) * (1.0 / D_MODEL)
        loss_ref[...] += 0.5 * jnp.sum(row, axis=0, keepdims=True)
        dy = err * (1.0 / D_MODEL)
        dfw_ref[...] += jnp.sum(dy * xn2, axis=0, keepdims=True)
        dxn = dy * fwv
        dx2 = r2 * (dxn - xn2 * jnp.mean(dxn * xn2, axis=-1, keepdims=True))
        dx2_ref[...] = dx2
        dgate_ref[...] += jnp.sum(dx2 * mix, axis=0, keepdims=True)
        dmix = _bf(gate * dx2)
        dcat_ref[...] = lax.dot_general(dmix, wb, _NT, preferred_element_type=F32)
        gw_ref[...] += lax.dot_general(catb, dmix, _TN, preferred_element_type=F32)

    tok = lambda w: pl.BlockSpec((ts, w), lambda i: (i, 0))
    full = lambda a: pl.BlockSpec(a.shape, lambda i: (0, 0))
    row = pl.BlockSpec((1, D_MODEL), lambda i: (0, 0))
    pairs = pl.BlockSpec((AT_PAIRS, ts, LANES), lambda i: (0, i, 0))
    return pl.pallas_call(
        body, name="out_loss", grid=(s // ts,), compiler_params=_params("arbitrary"),
        in_specs=[tok(DN_WIDTH), tok(DN_WIDTH), pairs, tok(AT_WIDTH), full(dnw), full(atw2),
                  tok(D_MODEL), tok(D_MODEL), full(w_out), full(gate), full(fw)],
        out_specs=(tok(D_MODEL), tok(D_MODEL), pl.BlockSpec((D_MODEL, D_MODEL), lambda i: (0, 0)), row, row,
                   pl.BlockSpec((1, 1), lambda i: (0, 0))),
        out_shape=(SDS((s, D_MODEL), F32), SDS((s, D_MODEL), F32), SDS((D_MODEL, D_MODEL), F32),
                   SDS((1, D_MODEL), F32), SDS((1, D_MODEL), F32), SDS((1, 1), F32)),
    )(o_dn, z_dn, o_at, z_at, dnw, atw2, x, tgt, w_out, gate, fw)


def _mix_bwd(dcat, o_dn, z_dn, o_at, z_at, dnw, atw2, ts):
    s = dcat.shape[0]

    def body(dcat_ref, odn, zdn, oat, zat, dnw_ref, atw_ref, dodn, dzdn, doat, dzat, delta, ddnw, datw):
        @pl.when(pl.program_id(0) == 0)
        def _():
            ddnw[...] = jnp.zeros_like(ddnw)
            datw[...] = jnp.zeros_like(datw)

        for h in range(DN_HEADS):
            cols = slice(h * DN_DIM, (h + 1) * DN_DIM)
            _, vjp = jax.vjp(_gate_dn, odn[:, cols], zdn[:, cols], dnw_ref[...])
            do, dz, dw = vjp(dcat_ref[:, cols])
            dodn[:, cols] = do
            dzdn[:, cols] = _bf(dz)
            ddnw[...] += dw
        for j in range(AT_PAIRS):
            cols = slice(j * LANES, (j + 1) * LANES)
            o = oat[j]
            _, vjp = jax.vjp(functools.partial(_gate_at, head_sum=_d_head_sum), o, zat[:, cols], atw_ref[...])
            do, dz, dw = vjp(dcat_ref[:, DN_WIDTH + j * LANES:DN_WIDTH + (j + 1) * LANES])
            doat[j] = do
            dzat[:, cols] = _bf(dz)
            datw[...] += dw
            delta[j] = _head_sum(do * o)

    tok = lambda w: pl.BlockSpec((ts, w), lambda i: (i, 0))
    full = lambda a: pl.BlockSpec(a.shape, lambda i: (0, 0))
    row = pl.BlockSpec((1, LANES), lambda i: (0, 0))
    pairs = pl.BlockSpec((AT_PAIRS, ts, LANES), lambda i: (0, i, 0))
    return pl.pallas_call(
        body, name="mix_bwd", grid=(s // ts,), compiler_params=_params("arbitrary"),
        in_specs=[tok(D_MODEL), tok(DN_WIDTH), tok(DN_WIDTH), pairs, tok(AT_WIDTH), full(dnw), full(atw2)],
        out_specs=(tok(DN_WIDTH), tok(DN_WIDTH), pairs, tok(AT_WIDTH), pairs, row, row),
        out_shape=(SDS((s, DN_WIDTH), F32), SDS((s, DN_WIDTH), BF16), SDS((AT_PAIRS, s, LANES), F32),
                   SDS((s, AT_WIDTH), BF16), SDS((AT_PAIRS, s, LANES), F32), SDS((1, LANES), F32),
                   SDS((1, LANES), F32)),
    )(dcat, o_dn, z_dn, o_at, z_at, dnw, atw2)


def _shift_acc(ext, n):
    @pl.when(n == 0)
    def _():
        ext[0:ATT_BLK, :] = jnp.zeros((ATT_BLK, LANES), F32)

    @pl.when(n > 0)
    def _():
        ext[0:ATT_BLK, :] = ext[ATT_BLK:2 * ATT_BLK, :]

    ext[ATT_BLK:2 * ATT_BLK, :] = jnp.zeros((ATT_BLK, LANES), F32)


def _attn_bwd(qr, kr, vv, do, lse, delta):
    s = qr.shape[1]
    nblk = s // ATT_BLK
    scale = AT_DIM ** -0.5

    def body(q_ref, k_ref, v_ref, do_ref, lse_ref, dl_ref, dq_ref, dk_ref, dv_ref, kext, vext, dkext, dvext,
             bias_ref):
        n = pl.program_id(1)
        _shift_in(kext, k_ref[0], n)
        _shift_in(vext, v_ref[0], n)
        _shift_acc(dkext, n)
        _shift_acc(dvext, n)
        bias_ref[...] = _band_bias()

        @pl.when(n < nblk)
        def _():
            dq_ref[0] = jnp.zeros((ATT_BLK, LANES), F32)
            lo = lax.broadcasted_iota(jnp.int32, (Q_BLOCK, LANES), 1) < AT_DIM
            for d in DILATIONS:
                def group(g, carry, d=d):
                    nu = ATT_UNROLL_BWD
                    cs = [_attn_combo(g * nu + u, d) for u in range(nu)]
                    heads = [(i, sel) for i in range(nu) for sel in (lo, ~lo)]
                    qrows = [_rows(qs, Q_BLOCK, d) for qs, _, _ in cs]
                    krows = [_rows(ks, 2 * Q_BLOCK, d) for _, ks, _ in cs]
                    band = bias_ref[...]
                    bias = [band + _first_bias((n == 0) & m0) for _, _, m0 in cs]
                    qb = [_bf(q_ref[0, r, :]) for r in qrows]
                    dob = [_bf(do_ref[0, r, :]) for r in qrows]
                    kk = [_bf(kext[r, :]) for r in krows]
                    vb = [_bf(vext[r, :]) for r in krows]
                    lse2 = [lse_ref[0, r, :] * LOG2E for r in qrows]
                    dl2 = [dl_ref[0, r, :] for r in qrows]
                    qm = [jnp.where(sel, qb[i], jnp.zeros_like(qb[i])) for i, sel in heads]
                    dom = [jnp.where(sel, dob[i], jnp.zeros_like(dob[i])) for i, sel in heads]
                    lse_c = [jnp.max(jnp.where(sel, lse2[i], -jnp.inf), axis=-1, keepdims=True) for i, sel in heads]
                    dl_c = [jnp.max(jnp.where(sel, dl2[i], -jnp.inf), axis=-1, keepdims=True) for i, sel in heads]
                    sc = [lax.dot_general(a, kk[i], _NT, preferred_element_type=F32) for a, (i, _) in zip(qm, heads)]
                    dp = [lax.dot_general(a, vb[i], _NT, preferred_element_type=F32) for a, (i, _) in zip(dom, heads)]
                    pr = [jnp.exp2(x * (scale * LOG2E) + bias[i] - l) for x, l, (i, _) in zip(sc, lse_c, heads)]
                    ds = [_bf(p * (x - dl) * scale) for p, x, dl in zip(pr, dp, dl_c)]
                    prb = [_bf(p) for p in pr]
                    dq = [jnp.dot(x, kk[i], preferred_element_type=F32) for x, (i, _) in zip(ds, heads)]
                    dk = [lax.dot_general(x, a, _TN, preferred_element_type=F32) for x, a in zip(ds, qm)]
                    dv = [lax.dot_general(x, a, _TN, preferred_element_type=F32) for x, a in zip(prb, dom)]
                    for i in range(nu):
                        dq_ref[0, qrows[i], :] += jnp.where(lo, dq[2 * i], dq[2 * i + 1])
                        dkext[krows[i], :] += dk[2 * i] + dk[2 * i + 1]
                        dvext[krows[i], :] += dv[2 * i] + dv[2 * i + 1]
                    return carry

                lax.fori_loop(0, ATT_BLK // Q_BLOCK // ATT_UNROLL_BWD, group, 0)

        dk_ref[0] = dkext[0:ATT_BLK, :]
        dv_ref[0] = dvext[0:ATT_BLK, :]

    cur = pl.BlockSpec((1, ATT_BLK, LANES), lambda j, n: (j, jnp.minimum(n, nblk - 1), 0))
    done = pl.BlockSpec((1, ATT_BLK, LANES), lambda j, n: (j, jnp.maximum(n - 1, 0), 0))
    return pl.pallas_call(
        body, name="attn_bwd", grid=(AT_PAIRS, nblk + 1), compiler_params=_params("arbitrary", "arbitrary"),
        in_specs=[cur] * 6, out_specs=(cur, done, done),
        out_shape=(SDS((AT_PAIRS, s, LANES), F32),) * 3,
        scratch_shapes=[pltpu.VMEM((2 * ATT_BLK, LANES), F32)] * 4 + [pltpu.VMEM((Q_BLOCK, 2 * Q_BLOCK), F32)],
    )(qr, kr, vv, do, lse, delta)


def _rope_bwd(dq, dk, dv, cos_t, sin_t, ts):
    s = cos_t.shape[0]

    def body(q_ref, k_ref, v_ref, cos_ref, sin_ref, oq, ok, ov):
        cs, sn = cos_ref[...], sin_ref[...]
        for j in range(AT_PAIRS):
            cols = slice(j * LANES, (j + 1) * LANES)
            for g_ref, o_ref in ((q_ref, oq), (k_ref, ok)):
                g = g_ref[j]
                o_ref[:, cols] = _bf(g * cs + _swap_half64(g * sn))
            ov[:, cols] = _bf(v_ref[j])

    tok = lambda w: pl.BlockSpec((ts, w), lambda i: (i, 0))
    pairs = pl.BlockSpec((AT_PAIRS, ts, LANES), lambda i: (0, i, 0))
    return pl.pallas_call(
        body, name="rope_bwd", grid=(s // ts,), compiler_params=_params("arbitrary"),
        in_specs=[pairs] * 3 + [tok(LANES)] * 2, out_specs=(tok(AT_WIDTH),) * 3,
        out_shape=(SDS((s, AT_WIDTH), BF16),) * 3,
    )(dq, dk, dv, cos_t, sin_t)


def _dn_scan_bwd(do, st, vn, w, qd, kd, p, gl, ts):
    s = do.shape[0]
    ncs = ts // CHUNK
    nt = s // ts

    def body(do_ref, st_ref, vn_ref, w_ref, qd_ref, kd_ref, p_ref, gl_ref,
             du_ref, dw_ref, dqd_ref, dkd_ref, dp_ref, dgl_ref, dstate):
        @pl.when(pl.program_id(0) == 0)
        def _():
            dstate[...] = jnp.zeros_like(dstate)

        def chunk(jr, carry):
            ci = ncs - 1 - jr
            rows = pl.ds(pl.multiple_of(ci * CHUNK, CHUNK), CHUNK)
            rows8 = pl.ds(pl.multiple_of(ci * 8, 8), 8)
            srows = pl.ds(pl.multiple_of(ci * DN_DIM, DN_DIM), DN_DIM)
            hs = range(DN_HEADS)
            sl = [slice(h * DN_DIM, (h + 1) * DN_DIM) for h in hs]
            ds_ = [dstate[h] for h in hs]
            dsb = [_bf(x) for x in ds_]
            dob = [_bf(do_ref[rows, c]) for c in sl]
            pdo = [_tn(p_ref[h, rows, :], b) for h, b in zip(hs, dob)]
            qdo = [_tn(qd_ref[rows, c], b) for c, b in zip(sl, dob)]
            dvn = [_nn(kd_ref[rows, c], b) + x for c, b, x in zip(sl, dsb, pdo)]
            dvb = [_bf(x) for x in dvn]
            wdv = [_tn(w_ref[rows, c], b) for c, b in zip(sl, dvb)]
            for h in hs:
                dstate[h] = ds_[h] * gl_ref[rows8, sl[h]][0:1] + qdo[h] - wdv[h]
            sfs = [st_ref[srows, c] for c in sl]
            sbs = [_bf(x) for x in sfs]
            vnb = [_bf(vn_ref[rows, c]) for c in sl]
            for h in hs:
                du_ref[rows, sl[h]] = dvn[h]
                dw_ref[rows, sl[h]] = -_nt(dvb[h], sbs[h])
                dqd_ref[rows, sl[h]] = _nt(dob[h], sbs[h])
                dkd_ref[rows, sl[h]] = _nt(vnb[h], dsb[h])
                dp_ref[h, rows, :] = _nt(dob[h], vnb[h])
                dgl = jnp.sum(jnp.sum(ds_[h] * sfs[h], axis=1, keepdims=True), axis=0, keepdims=True)
                dgl_ref[rows8, sl[h]] = jnp.broadcast_to(dgl, (8, DN_DIM))
            return carry

        lax.fori_loop(0, ncs, chunk, 0)

    tok = lambda wd: pl.BlockSpec((ts, wd), lambda i: (nt - 1 - i, 0))
    pspec = pl.BlockSpec((DN_HEADS, ts, CHUNK), lambda i: (0, nt - 1 - i, 0))
    g8 = pl.BlockSpec((ncs * 8, DN_WIDTH), lambda i: (nt - 1 - i, 0))
    return pl.pallas_call(
        body, name="dn_scan_bwd", grid=(nt,), compiler_params=_params("arbitrary"),
        in_specs=[tok(DN_WIDTH), pl.BlockSpec((ncs * DN_DIM, DN_WIDTH), lambda i: (nt - 1 - i, 0))]
        + [tok(DN_WIDTH)] * 4 + [pspec, g8],
        out_specs=(tok(DN_WIDTH),) * 4 + (pspec, g8),
        out_shape=(SDS((s, DN_WIDTH), F32),) * 4 + (SDS((DN_HEADS, s, CHUNK), F32),
                                                     SDS((s // CHUNK * 8, DN_WIDTH), F32)),
        scratch_shapes=[pltpu.VMEM((DN_HEADS, DN_DIM, DN_DIM), F32)],
    )(do, st, vn, w, qd, kd, p, gl)


def _dn_chunk_bwd(q, k, v, bg, t, du, dw, dqd, dkd, dp, dgl, ts):
    s = q.shape[0]
    ncs = ts // CHUNK

    def body(q_ref, k_ref, v_ref, bg_ref, t_ref, du_ref, dw_ref, dqd_ref, dkd_ref, dp_ref, dgl_ref,
             dq_ref, dk_ref, dv_ref, dbg_ref):
        def chunks(cg, carry):
            lane = lax.broadcasted_iota(jnp.int32, (CHUNK, BA_PAD), 1)
            where = []
            for ci in (cg * CH_UNROLL + i for i in range(CH_UNROLL)):
                rows = pl.ds(pl.multiple_of(ci * CHUNK, CHUNK), CHUNK)
                rows8 = pl.ds(pl.multiple_of(ci * 8, 8), 8)
                where += [(rows, rows8, h, slice(h * DN_DIM, (h + 1) * DN_DIM)) for h in range(DN_HEADS)]
            bgs = [bg_ref[rows, :] for rows, _, _, _ in where]
            cots = [(du_ref[rows, c], dw_ref[rows, c], dp_ref[h, rows, :], dqd_ref[rows, c], dkd_ref[rows, c],
                     dgl_ref[rows8, c][0:1, 0:1]) for rows, rows8, h, c in where]
            outs = _chunk_bwd([q_ref[rows, c] for rows, _, _, c in where], [k_ref[rows, c] for rows, _, _, c in where],
                              [v_ref[rows, c] for rows, _, _, c in where],
                              [b[:, h:h + 1] for b, (_, _, h, _) in zip(bgs, where)],
                              [b[:, GC_LANE + h:GC_LANE + h + 1] for b, (_, _, h, _) in zip(bgs, where)],
                              [t_ref[h, rows, :] for rows, _, h, _ in where], cots)
            for i in range(CH_UNROLL):
                dbg = jnp.zeros((CHUNK, BA_PAD), F32)
                for (rows, _, h, c), (dq, dk, dv, dbeta, dgc) in list(zip(where, outs))[i * DN_HEADS:(i + 1) * DN_HEADS]:
                    dq_ref[rows, c] = dq
                    dk_ref[rows, c] = dk
                    dv_ref[rows, c] = dv
                    dbg = dbg + jnp.where(lane == h, dbeta, 0.0) + jnp.where(lane == GC_LANE + h, dgc, 0.0)
                dbg_ref[where[i * DN_HEADS][0], :] = dbg
            return carry

        lax.fori_loop(0, ncs // CH_UNROLL, chunks, 0)

    tok = lambda wd: pl.BlockSpec((ts, wd), lambda i: (i, 0))
    pspec = pl.BlockSpec((DN_HEADS, ts, CHUNK), lambda i: (0, i, 0))
    g8 = pl.BlockSpec((ncs * 8, DN_WIDTH), lambda i: (i, 0))
    return pl.pallas_call(
        body, name="dn_chunk_bwd", grid=(s // ts,), compiler_params=_params("arbitrary"),
        in_specs=[tok(DN_WIDTH)] * 3 + [tok(BA_PAD), pspec] + [tok(DN_WIDTH)] * 4 + [pspec, g8],
        out_specs=(tok(DN_WIDTH),) * 3 + (tok(BA_PAD),),
        out_shape=(SDS((s, DN_WIDTH), F32),) * 3 + (SDS((s, BA_PAD), F32),),
    )(q, k, v, bg, t, du, dw, dqd, dkd, dp, dgl)


def _dn_prep_bwd(qkv_pre, ba, dq, dk, dv, dbg, conv_w8, alog_row, dtb_row, hbf, dz_dn, ts):
    s = qkv_pre.shape[0]
    cw = 3 * DN_WIDTH
    nt = s // ts

    def body(pre_ref, ph_ref, nh_ref, ba_ref, dq_ref, dqh_ref, dk_ref, dkh_ref, dv_ref, dvh_ref, dbg_ref,
             cw_ref, al_ref, dtb_ref, h_ref, dz_ref, dpre_ref, dba_ref, dcw_ref, dal_ref, ddtb_ref,
             gqkv_ref, gz_ref, gba_ref):
        n = pl.program_id(0)

        @pl.when(n == 0)
        def _():
            gqkv_ref[...] = jnp.zeros_like(gqkv_ref)
            gz_ref[...] = jnp.zeros_like(gz_ref)
            gba_ref[...] = jnp.zeros_like(gba_ref)
            dcw_ref[...] = jnp.zeros_like(dcw_ref)
            dal_ref[...] = jnp.zeros_like(dal_ref)
            ddtb_ref[...] = jnp.zeros_like(ddtb_ref)

        last = n == nt - 1
        prev = jnp.where(n == 0, 0.0, ph_ref[...])
        ext = jnp.concatenate([prev, pre_ref[...], nh_ref[...]], axis=0)
        taps = _conv_taps(ext, ts + 8)
        conv = taps[0] * cw_ref[0:1, :]
        for j in range(1, CONV_K):
            conv = conv + taps[j] * cw_ref[j:j + 1, :]

        def cot(main, halo, cols):
            return jnp.concatenate([main[:, cols], jnp.where(last, 0.0, halo[:, cols])], axis=0)

        pieces = []
        for grp, (fn, mref, href) in enumerate(((_post_q, dq_ref, dqh_ref), (_post_k, dk_ref, dkh_ref),
                                                (_post_v, dv_ref, dvh_ref))):
            for h in range(DN_HEADS):
                cols = slice(h * DN_DIM, (h + 1) * DN_DIM)
                c0 = grp * DN_WIDTH + h * DN_DIM
                _, vjp = jax.vjp(fn, conv[:, c0:c0 + DN_DIM])
                pieces.append(vjp(cot(mref, href, cols))[0])
        dconv = jnp.concatenate(pieces, axis=1)
        rows = ts + 8
        dpre = dconv[:ts] * cw_ref[CONV_K - 1:CONV_K, :]
        for j in range(CONV_K - 1):
            sh = CONV_K - 1 - j
            dpre = dpre + pltpu.roll(dconv, rows - sh, 0)[:ts] * cw_ref[j:j + 1, :]
        dpre_b = _bf(dpre)
        dpre_ref[...] = dpre_b
        hb = h_ref[...]
        gqkv_ref[...] += lax.dot_general(hb, dpre_b, _TN, preferred_element_type=F32)
        gz_ref[...] += lax.dot_general(hb, dz_ref[...], _TN, preferred_element_type=F32)
        for j in range(CONV_K):
            dcw_ref[j:j + 1, :] += jnp.sum(dconv[:ts] * taps[j][:ts], axis=0, keepdims=True)

        dbg = dbg_ref[...]
        lane = lax.broadcasted_iota(jnp.int32, dbg.shape, 1)
        dg = pltpu.roll(_chunk_cumsum(dbg, reverse=True), BA_PAD - DN_HEADS, 1)
        cot_bg = jnp.where(lane < DN_HEADS, dbg, jnp.where(lane < GC_LANE, dg, 0.0))
        _, vjp = jax.vjp(_beta_decay, ba_ref[...], al_ref[...], dtb_ref[...])
        dba, dal, ddtb = vjp(cot_bg)
        dba_b = _bf(dba)
        dba_ref[...] = dba_b
        gba_ref[...] += lax.dot_general(hb, dba_b, _TN, preferred_element_type=F32)
        dal_ref[...] += dal
        ddtb_ref[...] += ddtb

    tok = lambda w: pl.BlockSpec((ts, w), lambda i: (i, 0))
    full = lambda a: pl.BlockSpec(a.shape, lambda i: (0, 0))
    prevh = lambda w: pl.BlockSpec((8, w), lambda i: (jnp.maximum(i * (ts // 8) - 1, 0), 0))
    nexth = lambda w: pl.BlockSpec((8, w), lambda i: (jnp.minimum((i + 1) * (ts // 8), s // 8 - 1), 0))
    row = pl.BlockSpec((1, LANES), lambda i: (0, 0))
    return pl.pallas_call(
        body, name="dn_prep_bwd", grid=(nt,), compiler_params=_params("arbitrary"),
        in_specs=[tok(cw), prevh(cw), nexth(cw), tok(BA_PAD),
                  tok(DN_WIDTH), nexth(DN_WIDTH), tok(DN_WIDTH), nexth(DN_WIDTH), tok(DN_WIDTH), nexth(DN_WIDTH),
                  tok(BA_PAD), full(conv_w8), full(alog_row), full(dtb_row), tok(D_MODEL), tok(DN_WIDTH)],
        out_specs=(tok(cw), tok(BA_PAD), pl.BlockSpec((8, cw), lambda i: (0, 0)), row, row)
        + tuple(pl.BlockSpec((D_MODEL, w), lambda i: (0, 0)) for w in (cw, DN_WIDTH, BA_PAD)),
        out_shape=(SDS((s, cw), BF16), SDS((s, BA_PAD), BF16), SDS((8, cw), F32), SDS((1, LANES), F32),
                   SDS((1, LANES), F32)) + tuple(SDS((D_MODEL, w), F32) for w in (cw, DN_WIDTH, BA_PAD)),
    )(qkv_pre, qkv_pre, qkv_pre, ba, dq, dq, dk, dk, dv, dv, dbg, conv_w8, alog_row, dtb_row, hbf, dz_dn)


def _dh_dx(dps, ws, x, mod, norm_w, dx2, ts):
    s = x.shape[0]
    widths = [w.shape[1] for w in ws]
    np_ = len(ws)

    def body(*refs):
        dp_refs, w_refs = refs[:np_], refs[np_:2 * np_]
        x_ref, mod_ref, nw_ref, dx2_ref, gx_ref, dshift, dscale, dnw = refs[2 * np_:]

        @pl.when(pl.program_id(0) == 0)
        def _():
            dshift[...] = jnp.zeros_like(dshift)
            dscale[...] = jnp.zeros_like(dscale)
            dnw[...] = jnp.zeros_like(dnw)

        dh = lax.dot_general(dp_refs[0][...], w_refs[0][...], _NT, preferred_element_type=F32)
        for a, b in zip(dp_refs[1:], w_refs[1:]):
            dh = dh + lax.dot_general(a[...], b[...], _NT, preferred_element_type=F32)
        xt = x_ref[...]
        r = lax.rsqrt(jnp.mean(xt * xt, axis=-1, keepdims=True) + EPS)
        xn = xt * r
        nw = nw_ref[...]
        sc1 = 1.0 + mod_ref[:, D_MODEL:2 * D_MODEL]
        dshift[...] += jnp.sum(dh, axis=0, keepdims=True)
        dscale[...] += jnp.sum(dh * (xn * nw), axis=0, keepdims=True)
        dnw[...] += jnp.sum(dh * sc1 * xn, axis=0, keepdims=True)
        dxn = dh * sc1 * nw
        gx_ref[...] = r * (dxn - xn * jnp.mean(dxn * xn, axis=-1, keepdims=True)) + dx2_ref[...]

    tok = lambda w: pl.BlockSpec((ts, w), lambda i: (i, 0))
    full = lambda a: pl.BlockSpec(a.shape, lambda i: (0, 0))
    row = pl.BlockSpec((1, D_MODEL), lambda i: (0, 0))
    return pl.pallas_call(
        body, name="dh_dx", grid=(s // ts,), compiler_params=_params("arbitrary"),
        in_specs=[tok(w) for w in widths] + [full(w) for w in ws] + [tok(D_MODEL), full(mod), full(norm_w),
                                                                    tok(D_MODEL)],
        out_specs=(tok(D_MODEL), row, row, row),
        out_shape=(SDS((s, D_MODEL), F32),) + (SDS((1, D_MODEL), F32),) * 3,
    )(*dps, *ws, x, mod, norm_w, dx2)


def _grad_w_in(h, dps, ts, name):
    s = h.shape[0]
    widths = [p.shape[1] for p in dps]
    np_ = len(dps)

    def body(*refs):
        h_ref, dp_refs, outs = refs[0], refs[1:1 + np_], refs[1 + np_:]

        @pl.when(pl.program_id(0) == 0)
        def _():
            for o in outs:
                o[...] = jnp.zeros_like(o)

        hb = h_ref[...]
        for p, o in zip(dp_refs, outs):
            o[...] += lax.dot_general(hb, p[...], _TN, preferred_element_type=F32)

    tok = lambda w: pl.BlockSpec((ts, w), lambda i: (i, 0))
    return pl.pallas_call(
        body, name=name, grid=(s // ts,), compiler_params=_params("arbitrary"),
        in_specs=[tok(D_MODEL)] + [tok(w) for w in widths],
        out_specs=tuple(pl.BlockSpec((D_MODEL, w), lambda i: (0, 0)) for w in widths),
        out_shape=tuple(SDS((D_MODEL, w), F32) for w in widths),
    )(h, *dps)


def _adamw_math(w, g, m, v):
    m = ADAM_B1 * m + (1.0 - ADAM_B1) * g
    v = ADAM_B2 * v + (1.0 - ADAM_B2) * (g * g)
    m_hat = m / (1.0 - ADAM_B1 ** ADAM_STEP)
    v_hat = v / (1.0 - ADAM_B2 ** ADAM_STEP)
    delta = -ADAM_LR * (m_hat / (jnp.sqrt(v_hat) + ADAM_EPS) + ADAM_WD * w)
    return delta, m, v


def _adamw(w, m, v, g, name, own=None):
    def body(w_ref, m_ref, v_ref, g_ref, *rest):
        g_out, d_out, m_out, v_out = rest[-4:]
        if own is None:
            g = g_ref[...]
        else:
            g = g_ref[0].astype(F32)
            for k in range(1, N_DEV):
                g = g + g_ref[k].astype(F32)
            g = g + rest[0][...].astype(F32)
        g_out[...] = g
        d_out[...], m_out[...], v_out[...] = _adamw_math(w_ref[...], g, m_ref[...], v_ref[...])

    args = (w, m, v, g) if own is None else (w, m, v, g, own)
    return pl.pallas_call(body, name=name, compiler_params=_params(),
                          out_shape=(SDS(w.shape, F32),) * 4)(*args)


def _adamw_w_mod(w, m, v, siluc_all, dmod_mine):
    def body(w_ref, m_ref, v_ref, sc_ref, dm_ref, g_out, d_out, m_out, v_out):
        g = _htn(sc_ref[...], dm_ref[...])
        g_out[...] = g
        d_out[...], m_out[...], v_out[...] = _adamw_math(w_ref[...], g, m_ref[...], v_ref[...])

    return pl.pallas_call(body, name="adamw_w_mod", compiler_params=_params(),
                          out_shape=(SDS(w.shape, F32),) * 4)(w, m, v, siluc_all, dmod_mine)


def _pack_sum(pack_all):
    def body(p_ref, o_ref):
        t = p_ref[0]
        for k in range(1, N_DEV):
            t = t + p_ref[k]
        o_ref[...] = t

    return pl.pallas_call(body, name="pack_sum", out_shape=SDS(pack_all.shape[1:], F32))(pack_all)


def _tile(s, want):
    t = min(want, s)
    assert s % t == 0
    return t


def _local_step(x, c, positions, w_mod_bf, b_mod, norm_w, w_in_bf, conv_w, a_log, dt_bias, dn_norm_w, at_norm_w,
                w_out_bf, final_norm_w, tgt):
    s = x.shape[0]
    o = [0]
    for wdt in IN_SPLITS:
        o.append(o[-1] + wdt)
    w_ba = jnp.pad(w_in_bf[:, o[2]:o[4]], ((0, 0), (0, BA_PAD - 2 * DN_HEADS)))
    ws = [w_in_bf[:, o[0]:o[1]], w_in_bf[:, o[1]:o[2]], w_ba, w_in_bf[:, o[4]:o[5]], w_in_bf[:, o[5]:o[6]],
          w_in_bf[:, o[6]:o[7]], w_in_bf[:, o[7]:o[8]]]
    conv_w8 = jnp.pad(conv_w, ((0, 8 - CONV_K), (0, 0)))
    alog_row = jnp.pad(a_log, ((0, 0), (DN_HEADS, BA_PAD - 2 * DN_HEADS)))
    dtb_row = jnp.pad(dt_bias, ((0, 0), (DN_HEADS, BA_PAD - 2 * DN_HEADS)))
    atw2 = jnp.concatenate([at_norm_w, at_norm_w], axis=1)

    half = AT_DIM // 2
    lane = jnp.arange(LANES)
    inv_freq = ROPE_THETA ** (-(lane % half).astype(F32) / half)
    ang = positions.astype(F32)[:, None] * inv_freq
    cos_t = jnp.cos(ang)
    sin_t = jnp.sin(ang) * jnp.where((lane // half) % 2 == 0, -1.0, 1.0)

    mod, siluc = _adaln_mod(c, w_mod_bf, b_mod)
    gate = mod[:, 2 * D_MODEL:]
    hbf, qkv_pre, z_dn, ba, qr, kr, vb, z_at, q, k, v, bg = _ln_proj(
        x, mod, norm_w, ws, cos_t, sin_t, conv_w8, alog_row, dtb_row, _tile(s, 256))
    u, w, qd, kd, p, gl, tinv = _dn_chunk_prep(q, k, v, bg, _tile(s, 512))
    o_dn, vn, st = _dn_scan(u, w, qd, kd, p, gl, _tile(s, 512))
    o_at, lse = _attn_fwd(qr, kr, vb)
    dx2, dcat, gw_out, dfw, dgate, loss = _out_loss(o_dn, z_dn, o_at, z_at, dn_norm_w, atw2, x, tgt, w_out_bf, gate,
                                                    final_norm_w, _tile(s, 512))

    do_dn, dz_dn, do_at, dz_at, delta, ddnw, datw = _mix_bwd(dcat, o_dn, z_dn, o_at, z_at, dn_norm_w, atw2,
                                                             _tile(s, 512))
    daq, dak, dav = _rope_bwd(*_attn_bwd(qr, kr, vb, do_at, lse, delta), cos_t, sin_t, _tile(s, 512))
    du, dw, dqd, dkd, dp, dgl = _dn_scan_bwd(do_dn, st, vn, w, qd, kd, p, gl, _tile(s, 512))
    dq, dk, dv, dbg = _dn_chunk_bwd(q, k, v, bg, tinv, du, dw, dqd, dkd, dp, dgl, _tile(s, 512))
    dqkv, dba, dcw, dal, ddtb, g_qkv, g_z, g_ba = _dn_prep_bwd(qkv_pre, ba, dq, dk, dv, dbg, conv_w8, alog_row, dtb_row,
                                                               hbf, dz_dn, _tile(s, 512))
    dps = [dqkv, dz_dn, dba, daq, dak, dav, dz_at]
    g_aq, g_ak, g_av, g_az = _grad_w_in(hbf, dps[3:], _tile(s, 512), "grad_w_in_at")
    gw_in = jnp.concatenate([g_qkv, g_z, g_ba[:, :2 * DN_HEADS], g_aq, g_ak, g_av, g_az], axis=1)
    small = dict(conv=dcw[:CONV_K], dgate=dgate, siluc=siluc, dfw=dfw, alog=dal, dtb=ddtb, dnn=ddnw, atn=datw)

    def input_grad(token):
        gx, dshift, dscale, dnw = _dh_dx(dps, ws, x, mod + token, norm_w, dx2, _tile(s, 512))
        return gx, jnp.concatenate([dshift, dscale, small["dgate"]], axis=1), dnw

    return loss, gw_in, gw_out, small, input_grad


def kernel(x, c, positions, w_mod, b_mod, norm_w, w_in, conv_w, a_log, dt_bias, dn_norm_w, at_norm_w, w_out, final_norm_w, loss_target, m_w_mod, m_b_mod, m_norm_w, m_w_in, m_conv_w, m_a_log, m_dt_bias, m_dn_norm_w, m_at_norm_w, m_w_out, m_final_norm_w, v_w_mod, v_b_mod, v_norm_w, v_w_in, v_conv_w, v_a_log, v_dt_bias, v_dn_norm_w, v_at_norm_w, v_w_out, v_final_norm_w):
    me = 4 * lax.axis_index("x") + 2 * lax.axis_index("y") + lax.axis_index("c")
    s = x.shape[1]

    g_mod, g_in, g_conv, g_out = _all_gather(
        [_bf(w_mod[0]), _bf(w_in[0]), conv_w[0], _bf(w_out[0])], "gather_weights")
    w_mod_bf = g_mod.transpose(1, 0, 2).reshape(D_MODEL, 3 * D_MODEL)
    w_in_bf = g_in.transpose(1, 0, 2).reshape(D_MODEL, IN_COLS)
    conv_full = g_conv.transpose(1, 0, 2).reshape(CONV_K, 3 * DN_WIDTH)
    w_out_bf = g_out.reshape(D_MODEL, D_MODEL)

    loss, gw_in, gw_out, small, input_grad = _local_step(
        x[0], c, positions[0], w_mod_bf, b_mod, norm_w, w_in_bf, conv_full, a_log, dt_bias, dn_norm_w, at_norm_w,
        w_out_bf, final_norm_w.reshape(1, D_MODEL), loss_target[0])

    gw_in_slabs = _bf(gw_in).reshape(D_MODEL, N_DEV, IN_SHARD).transpose(1, 0, 2)
    gw_out_slabs = _bf(gw_out).reshape(N_DEV, D_MODEL // N_DEV, D_MODEL)
    send_sems, recv_sems, srcs, lands, token = _scatter_start([gw_in_slabs, gw_out_slabs])
    gx, dmod, dnw = input_grad(token[0, 0])
    r_in, r_out = _scatter_wait(send_sems, recv_sems, srcs, lands, gx)
    own_in = lax.dynamic_index_in_dim(gw_in_slabs, me, 0, keepdims=False)
    own_out = lax.dynamic_index_in_dim(gw_out_slabs, me, 0, keepdims=False)

    pack = jnp.concatenate([small["conv"].reshape(1, -1), dmod, small["siluc"], dnw, small["dfw"],
                            small["alog"], small["dtb"], small["dnn"], small["atn"],
                            jnp.pad(loss, ((0, 0), (0, LANES - 1)))], axis=1).reshape(PK_ROWS, LANES)
    (pack_all,) = _exchange([pack], [False], "exchange_small")

    res = {}
    res["w_in"] = _adamw(w_in[0], m_w_in[0], v_w_in[0], r_in, "adamw_w_in", own=own_in)
    res["w_out"] = _adamw(w_out[0], m_w_out[0], v_w_out[0], r_out, "adamw_w_out", own=own_out)
    flat_all = pack_all.reshape(N_DEV, PK_END)
    dmod_mine = lax.dynamic_slice(flat_all, (0, PK_DMOD + me * (3 * D_MODEL // N_DEV)), (N_DEV, 3 * D_MODEL // N_DEV))
    res["w_mod"] = _adamw_w_mod(w_mod[0], m_w_mod[0], v_w_mod[0], flat_all[:, PK_SILUC:PK_DNW], dmod_mine)
    tot = _pack_sum(pack_all).reshape(1, PK_END)
    g_conv_full = tot[:, PK_CONV:PK_DMOD].reshape(CONV_K, 3 * DN_WIDTH)
    g_conv_mine = lax.dynamic_slice(g_conv_full, (0, me * (3 * DN_WIDTH // N_DEV)), (CONV_K, 3 * DN_WIDTH // N_DEV))
    res["conv_w"] = _adamw(conv_w[0], m_conv_w[0], v_conv_w[0], g_conv_mine, "adamw_conv_w")
    res["b_mod"] = _adamw(b_mod, m_b_mod, v_b_mod, tot[:, PK_DMOD:PK_SILUC], "adamw_b_mod")
    res["norm_w"] = _adamw(norm_w, m_norm_w, v_norm_w, tot[:, PK_DNW:PK_DFW], "adamw_norm_w")
    res["a_log"] = _adamw(a_log, m_a_log, v_a_log, tot[:, PK_ALOG + DN_HEADS:PK_ALOG + 2 * DN_HEADS], "adamw_a_log")
    res["dt_bias"] = _adamw(dt_bias, m_dt_bias, v_dt_bias, tot[:, PK_DTB + DN_HEADS:PK_DTB + 2 * DN_HEADS],
                            "adamw_dt_bias")
    res["dn_norm_w"] = _adamw(dn_norm_w, m_dn_norm_w, v_dn_norm_w, tot[:, PK_DNN:PK_ATN], "adamw_dn_norm_w")
    g_atn = tot[:, PK_ATN:PK_ATN + AT_DIM] + tot[:, PK_ATN + AT_DIM:PK_LOSS]
    res["at_norm_w"] = _adamw(at_norm_w, m_at_norm_w, v_at_norm_w, g_atn, "adamw_at_norm_w")
    fin = _adamw(final_norm_w.reshape(1, D_MODEL), m_final_norm_w.reshape(1, D_MODEL),
                 v_final_norm_w.reshape(1, D_MODEL), tot[:, PK_DFW:PK_ALOG], "adamw_final_norm_w")
    res["final_norm_w"] = tuple(a.reshape(D_MODEL) for a in fin)

    lead = ("w_mod", "w_in", "conv_w", "w_out")
    names = ("w_mod", "b_mod", "norm_w", "w_in", "conv_w", "a_log", "dt_bias", "dn_norm_w", "at_norm_w", "w_out",
             "final_norm_w")
    out = [tot[0, PK_LOSS], gx.reshape(1, s, D_MODEL)]
    for kind in range(4):
        for nm in names:
            a = res[nm][kind]
            out.append(a[None] if nm in lead else a)
    return tuple(out)
```

```python
import functools

import jax
import jax.numpy as jnp
from jax import lax
from jax.experimental import pallas as pl
from jax.experimental.pallas import tpu as pltpu

F32, BF16 = jnp.float32, jnp.bfloat16
HI = lax.Precision.HIGHEST
SDS = jax.ShapeDtypeStruct

D_MODEL = 1024
DN_HEADS, DN_DIM, DN_WIDTH = 4, 128, 512
AT_HEADS, AT_DIM, AT_WIDTH = 8, 64, 512
CONV_K = 4
CHUNK = 64
Q_BLOCK = 128
W_SUB = 128
DILATIONS = (1, 4, 16)
AT_PAIRS = 4
ATT_BLK = Q_BLOCK * max(DILATIONS)
ATT_UNROLL, ATT_UNROLL_BWD = 8, 4
CH_UNROLL = 4
ROPE_THETA = 10000.0
EPS = 1e-6
N_DEV = 8
LANES = 128
BA_PAD = 128
IN_SPLITS = (1536, 512, 4, 4, 512, 512, 512, 512)
IN_COLS = sum(IN_SPLITS)
IN_SHARD = IN_COLS // N_DEV
VMEM_LIMIT = 58 * 2 ** 20

ADAM_LR, ADAM_B1, ADAM_B2, ADAM_EPS, ADAM_WD, ADAM_STEP = 0.001, 0.9, 0.999, 1e-08, 0.01, 10

PK_CONV, PK_DMOD, PK_SILUC, PK_DNW, PK_DFW, PK_ALOG, PK_DTB, PK_DNN, PK_ATN, PK_LOSS, PK_END = (
    0, 6144, 9216, 10240, 11264, 12288, 12416, 12544, 12672, 12800, 12928)
PK_ROWS = PK_END // LANES

_NT = (((1,), (1,)), ((), ()))
_TN = (((0,), (0,)), ((), ()))


def _params(*sem):
    return pltpu.CompilerParams(dimension_semantics=sem or None, vmem_limit_bytes=VMEM_LIMIT)


def _bf(x):
    return x.astype(BF16)


def _nn(a, b):
    return jnp.dot(_bf(a), _bf(b), preferred_element_type=F32)


def _nt(a, b):
    return lax.dot_general(_bf(a), _bf(b), _NT, preferred_element_type=F32)


def _tn(a, b):
    return lax.dot_general(_bf(a), _bf(b), _TN, preferred_element_type=F32)


def _htn(a, b):
    return lax.dot_general(a, b, _TN, precision=HI, preferred_element_type=F32)


def _head_sum(x):
    r = lax.broadcasted_iota(jnp.int32, (LANES, LANES), 0)
    c = lax.broadcasted_iota(jnp.int32, (LANES, LANES), 1)
    same = jnp.where((r // AT_DIM) == (c // AT_DIM), 1.0, 0.0).astype(BF16)
    hi, lo = _hl(x)
    return jnp.dot(hi, same, preferred_element_type=F32) + jnp.dot(lo, same, preferred_element_type=F32)


@jax.custom_vjp
def _d_head_sum(x):
    return _head_sum(x)


_d_head_sum.defvjp(lambda x: (_head_sum(x), None), lambda _, g: (_head_sum(g),))


def _silu(x):
    return x * jax.nn.sigmoid(x)


def _softplus(x):
    return jnp.maximum(x, 0.0) + jnp.log(1.0 + jnp.exp(-jnp.abs(x)))


def _l2n(x):
    return x * lax.rsqrt(jnp.sum(x * x, axis=-1, keepdims=True) + EPS)


def _post_q(x):
    return _l2n(_silu(x)) * (DN_DIM ** -0.5)


def _post_k(x):
    return _l2n(_silu(x))


def _post_v(x):
    return _silu(x)


def _beta_decay(ba, alog_row, dtb_row):
    lane = lax.broadcasted_iota(jnp.int32, ba.shape, 1)
    return jnp.where(lane < DN_HEADS, jax.nn.sigmoid(ba), -jnp.exp(alog_row) * _softplus(ba + dtb_row))


def _gate_dn(o, z, w):
    return (o * lax.rsqrt(jnp.mean(o * o, axis=-1, keepdims=True) + EPS)) * w * _silu(z)


def _gate_at(o, z, w2, head_sum):
    ms = head_sum(o * o) * (1.0 / AT_DIM)
    return (o * lax.rsqrt(ms + EPS)) * w2 * _silu(z)


def _swap_half64(x):
    lane = lax.broadcasted_iota(jnp.int32, x.shape, 1)
    return jnp.where((lane & (AT_DIM - 1)) < AT_DIM // 2, pltpu.roll(x, LANES - AT_DIM // 2, 1),
                     pltpu.roll(x, AT_DIM // 2, 1))


_NN = (((1,), (0,)), ((), ()))


def _hl(a):
    hi = a.astype(BF16)
    return hi, (a - hi.astype(F32)).astype(BF16)


def _mm3(a, b, dims=_NN):
    (ah, al), (bh, bl) = a, b
    f = lambda x, y: lax.dot_general(x, y, dims, preferred_element_type=F32)
    return f(ah, bh) + (f(ah, bl) + f(al, bh))


def _chunk_masks():
    r = lax.broadcasted_iota(jnp.int32, (CHUNK, CHUNK), 0)
    c = lax.broadcasted_iota(jnp.int32, (CHUNK, CHUNK), 1)
    return r >= c, r > c, (r == c).astype(F32), (r // 16) == (c // 16)


def _tri_inv(mats):
    _, _, eye, blk = _chunk_masks()
    dg = [jnp.where(blk, a, 0.0) for a in mats]
    lo = [jnp.where(blk, 0.0, a) for a in mats]
    sdg = [_hl(x) for x in dg]
    d2 = [_mm3(s, s) for s in sdg]
    sd2 = [_hl(x) for x in d2]
    d4 = [_mm3(s, s) for s in sd2]
    sd4 = [_hl(x) for x in d4]
    d8 = [_mm3(s, s) for s in sd4]
    p1 = [_mm3(_hl(eye - a), _hl(eye + b)) for a, b in zip(dg, d2)]
    p2 = [_mm3(_hl(a), _hl(eye + b)) for a, b in zip(p1, d4)]
    dinv = [_mm3(_hl(a), _hl(eye + b)) for a, b in zip(p2, d8)]
    sdinv = [_hl(x) for x in dinv]
    n1 = [_mm3(s, _hl(b)) for s, b in zip(sdinv, lo)]
    sn1 = [_hl(x) for x in n1]
    n2 = [_mm3(s, s) for s in sn1]
    q1 = [_mm3(_hl(eye - a), _hl(eye + b)) for a, b in zip(n1, n2)]
    return [_mm3(_hl(a), s) for a, s in zip(q1, sdinv)]


def _chunk_common(qs, ks, vs, betas, gcs):
    tril, _, _, _ = _chunk_masks()
    out = []
    for q, k, v, beta, gc in zip(qs, ks, vs, betas, gcs):
        gb = jnp.broadcast_to(gc, (CHUNK, DN_DIM))
        gt = gb.T[:CHUNK, :]
        gam = jnp.where(tril, jnp.exp(jnp.where(tril, gb[:, :CHUNK] - gt, 0.0)), 0.0)
        last = gb[CHUNK - 1:CHUNK, :]
        eg, e2 = jnp.exp(gb), jnp.exp(last - gb)
        kb, vb = k * beta, v * beta
        out.append(dict(gam=gam, eg=eg, e2=e2, gl=jnp.exp(last[:, 0:1]), kb=kb, vb=vb, kbg=kb * eg,
                        m=_nt(kb, k), qk=_nt(q, k)))
    return out


def _chunk_fwd(qs, ks, vs, betas, gcs):
    tril, strict, _, _ = _chunk_masks()
    cm = _chunk_common(qs, ks, vs, betas, gcs)
    ts = _tri_inv([jnp.where(strict, c["m"] * c["gam"], 0.0) for c in cm])
    outs = []
    for q, k, c, t in zip(qs, ks, cm, ts):
        uw = _nn(t, jnp.concatenate([c["vb"], c["kbg"]], axis=1))
        p = jnp.where(tril, c["qk"] * c["gam"], 0.0)
        outs.append((uw[:, :DN_DIM], uw[:, DN_DIM:], p, q * c["eg"], k * c["e2"], c["gl"], t.T))
    return outs


def _chunk_bwd(qs, ks, vs, betas, gcs, ts, cots):
    tril, strict, _, _ = _chunk_masks()
    cm = _chunk_common(qs, ks, vs, betas, gcs)
    row = lax.broadcasted_iota(jnp.int32, (CHUNK, 1), 0)
    ones = jnp.ones((CHUNK, DN_DIM), BF16)
    rs = lambda x: jnp.sum(x, axis=-1, keepdims=True)
    tts = [_bf(t) for t in ts]
    duw = [_bf(jnp.concatenate([ct[0], ct[1]], axis=1)) for ct in cots]
    dts = [_nt(a, jnp.concatenate([c["vb"], c["kbg"]], axis=1)) for a, c in zip(duw, cm)]
    xs = [_nn(t, d) for t, d in zip(tts, dts)]
    das = [jnp.where(strict, -_nn(x, t), 0.0) for x, t in zip(xs, tts)]
    dvks = [_nn(t, a) for t, a in zip(tts, duw)]
    outs = []
    for q, k, v, beta, c, ct, da, dvk in zip(qs, ks, vs, betas, cm, cots, das, dvks):
        _, _, dp, dqd, dkd, dgl = ct
        dvb, dkbg = dvk[:, :DN_DIM], dvk[:, DN_DIM:]
        dm = da * c["gam"]
        dqk = jnp.where(tril, dp, 0.0) * c["gam"]
        e = dm * c["m"] + dqk * c["qk"]
        dmq = jnp.concatenate([dm, dqk], axis=0)
        r1 = _nn(dmq, k)
        dkb = r1[:CHUNK] + dkbg * c["eg"]
        dq = r1[CHUNK:] + dqd * c["eg"]
        dk = _tn(dmq, jnp.concatenate([c["kb"], q], axis=0)) + dkd * c["e2"] + dkb * beta
        dbeta = rs(dkb * k + dvb * v)
        eh, el = _hl(e)
        colsum = (lax.dot_general(eh, ones, _TN, preferred_element_type=F32)
                  + lax.dot_general(el, ones, _TN, preferred_element_type=F32))[:, 0:1]
        pkd = dkd * (k * c["e2"])
        dgc = rs(e) - colsum + rs(dqd * q * c["eg"] + dkbg * c["kbg"] - pkd)
        tail = rs(jnp.sum(pkd, axis=0, keepdims=True)) + dgl * c["gl"]
        dgc = dgc + jnp.where(row == CHUNK - 1, tail, 0.0)
        outs.append((dq, dk, dvb * beta, dbeta, dgc))
    return outs


def _chunk_cumsum(x, reverse=False):
    n = x.shape[0]
    pos = lax.broadcasted_iota(jnp.int32, x.shape, 0) & (CHUNK - 1)
    sh = 1
    while sh < CHUNK:
        if reverse:
            x = x + jnp.where(pos < CHUNK - sh, pltpu.roll(x, n - sh, 0), 0.0)
        else:
            x = x + jnp.where(pos >= sh, pltpu.roll(x, sh, 0), 0.0)
        sh *= 2
    return x


GC_LANE = 2 * DN_HEADS


def _exchange(arrays, scatter, name):
    n = len(arrays)
    out_shapes = []
    for a, sc in zip(arrays, scatter):
        out_shapes.append(SDS(a.shape if sc else (N_DEV,) + a.shape, a.dtype))

    def body(*refs):
        ins, outs = refs[:n], refs[n:2 * n]
        send_sems, recv_sems, loc_sems = refs[2 * n:]
        x, y, c = lax.axis_index("x"), lax.axis_index("y"), lax.axis_index("c")
        me = 4 * x + 2 * y + c
        local, remote = [], []
        for i in range(n):
            src = ins[i].at[me] if scatter[i] else ins[i]
            cp = pltpu.make_async_copy(src, outs[i].at[me], loc_sems.at[i])
            cp.start()
            local.append(cp)
        for dlt in range(1, N_DEV):
            px = 1 - x if dlt & 4 else x
            py = 1 - y if dlt & 2 else y
            pc = 1 - c if dlt & 1 else c
            peer = 4 * px + 2 * py + pc
            for i in range(n):
                src = ins[i].at[peer] if scatter[i] else ins[i]
                cp = pltpu.make_async_remote_copy(
                    src_ref=src, dst_ref=outs[i].at[me],
                    send_sem=send_sems.at[i, dlt - 1], recv_sem=recv_sems.at[i, dlt - 1],
                    device_id=(px, py, pc), device_id_type=pl.DeviceIdType.MESH)
                cp.start()
                arrive = pltpu.make_async_remote_copy(
                    src_ref=src, dst_ref=outs[i].at[peer],
                    send_sem=send_sems.at[i, dlt - 1], recv_sem=recv_sems.at[i, dlt - 1],
                    device_id=(px, py, pc), device_id_type=pl.DeviceIdType.MESH)
                remote.append((cp, arrive))
        for cp, arrive in remote:
            cp.wait_send()
            arrive.wait_recv()
        for cp in local:
            cp.wait()

    any_spec = pl.BlockSpec(memory_space=pl.ANY)
    return pl.pallas_call(
        body, name=name, out_shape=tuple(out_shapes),
        in_specs=[any_spec] * n, out_specs=tuple([any_spec] * n),
        scratch_shapes=[pltpu.SemaphoreType.DMA((n, N_DEV - 1)), pltpu.SemaphoreType.DMA((n, N_DEV - 1)),
                        pltpu.SemaphoreType.DMA((n,))],
    )(*arrays)


def _all_gather(arrays, name):
    n = len(arrays)

    def body(*refs):
        ins, outs = refs[:n], refs[n:2 * n]
        send_sems, recv_sems, loc_sems = refs[2 * n:]
        x, y, c = lax.axis_index("x"), lax.axis_index("y"), lax.axis_index("c")
        me, sibling = (x, y, c), (x, y, 1 - c)
        chips = [(1 - x, y), (x, 1 - y), (1 - x, 1 - y)]

        def copy(i, k, block, to, src=None):
            slot = outs[i].at[4 * block[0] + 2 * block[1] + block[2]]
            return pltpu.make_async_remote_copy(
                src_ref=slot if src is None else src, dst_ref=slot,
                send_sem=send_sems.at[i, k], recv_sem=recv_sems.at[i, k],
                device_id=to, device_id_type=pl.DeviceIdType.MESH)

        mine = [pltpu.make_async_copy(ins[i], outs[i].at[4 * x + 2 * y + c], loc_sems.at[i]) for i in range(n)]
        for cp in mine:
            cp.start()
        first = []
        for i in range(n):
            first.append(copy(i, 0, me, sibling, src=ins[i]))
            first += [copy(i, 1 + j, me, (*chip, c), src=ins[i]) for j, chip in enumerate(chips)]
        for cp in first:
            cp.start()
        passed = []
        for j, chip in enumerate(chips):
            for i in range(n):
                copy(i, 1 + j, (*chip, c), me).wait_recv()
                fwd = copy(i, 4 + j, (*chip, c), sibling)
                fwd.start()
                passed.append(fwd)
        for i in range(n):
            copy(i, 0, sibling, me).wait_recv()
        for j, chip in enumerate(chips):
            for i in range(n):
                copy(i, 4 + j, (*chip, 1 - c), me).wait_recv()
        for cp in first + passed:
            cp.wait_send()
        for cp in mine:
            cp.wait()

    any_spec = pl.BlockSpec(memory_space=pl.ANY)
    return pl.pallas_call(
        body, name=name, out_shape=tuple(SDS((N_DEV,) + a.shape, a.dtype) for a in arrays),
        in_specs=[any_spec] * n, out_specs=tuple([any_spec] * n),
        scratch_shapes=[pltpu.SemaphoreType.DMA((n, N_DEV - 1)), pltpu.SemaphoreType.DMA((n, N_DEV - 1)),
                        pltpu.SemaphoreType.DMA((n,))],
    )(*arrays)


_HBM = pl.BlockSpec(memory_space=pltpu.HBM)
_SEM = pl.BlockSpec(memory_space=pltpu.SEMAPHORE)


def _peers(x, y, c):
    out = []
    for dlt in range(1, N_DEV):
        px = 1 - x if dlt & 4 else x
        py = 1 - y if dlt & 2 else y
        pc = 1 - c if dlt & 1 else c
        out.append((dlt, (px, py, pc), 4 * px + 2 * py + pc))
    return out


def _scatter_start(arrays):
    n = len(arrays)
    ns = n * (N_DEV - 1)

    def body(*refs):
        ins, lands = refs[:n], refs[n:2 * n]
        send_sems, recv_sems = refs[2 * n:2 * n + ns], refs[2 * n + ns:2 * n + 2 * ns]
        token = refs[-1]
        x, y, c = lax.axis_index("x"), lax.axis_index("y"), lax.axis_index("c")
        me = 4 * x + 2 * y + c
        for dlt, peer, pi in _peers(x, y, c):
            for i in range(n):
                k = i * (N_DEV - 1) + dlt - 1
                pltpu.make_async_remote_copy(
                    src_ref=ins[i].at[pi], dst_ref=lands[i].at[me], send_sem=send_sems[k], recv_sem=recv_sems[k],
                    device_id=peer, device_id_type=pl.DeviceIdType.MESH).start()
        token[...] = jnp.zeros_like(token)

    sem = pltpu.SemaphoreType.DMA(())
    thru = tuple(pltpu.HBM(a.shape, a.dtype) for a in arrays)
    hbm = lambda a: pltpu.with_memory_space_constraint(a, pltpu.HBM)
    outs = pl.pallas_call(
        body, name="scatter_start", out_shape=(sem,) * (2 * ns) + thru + thru + (SDS((8, LANES), F32),),
        in_specs=[_HBM] * (2 * n),
        out_specs=(_SEM,) * (2 * ns) + (_HBM,) * (2 * n) + (pl.BlockSpec(memory_space=pltpu.VMEM),),
        input_output_aliases={i: 2 * ns + i for i in range(2 * n)},
        compiler_params=pltpu.CompilerParams(has_side_effects=pltpu.SideEffectType.DATAFLOW_SIDE_EFFECTING),
    )(*[hbm(a) for a in arrays], *[hbm(jnp.zeros(a.shape, a.dtype)) for a in arrays])
    return outs[:ns], outs[ns:2 * ns], outs[2 * ns:2 * ns + n], outs[2 * ns + n:2 * ns + 2 * n], outs[-1]


def _scatter_wait(send_sems, recv_sems, srcs, lands, after):
    n = len(srcs)
    ns = n * (N_DEV - 1)

    def body(*refs):
        ins, lands_ = refs[:n], refs[n:2 * n]
        send, recv = refs[2 * n:2 * n + ns], refs[2 * n + ns:2 * n + 2 * ns]
        x, y, c = lax.axis_index("x"), lax.axis_index("y"), lax.axis_index("c")
        for dlt, peer, pi in _peers(x, y, c):
            for i in range(n):
                k = i * (N_DEV - 1) + dlt - 1
                cp = pltpu.make_async_remote_copy(
                    src_ref=ins[i].at[pi], dst_ref=lands_[i].at[pi], send_sem=send[k], recv_sem=recv[k],
                    device_id=peer, device_id_type=pl.DeviceIdType.MESH)
                cp.wait_send()
                cp.wait_recv()

    thru = tuple(pltpu.HBM(a.shape, a.dtype) for a in srcs)
    outs = pl.pallas_call(
        body, name="scatter_wait", out_shape=thru + thru,
        in_specs=[_HBM] * (2 * n) + [_SEM] * (2 * ns) + [pl.BlockSpec(memory_space=pl.ANY)],
        out_specs=(_HBM,) * (2 * n), input_output_aliases={i: i for i in range(2 * n)},
        compiler_params=pltpu.CompilerParams(has_side_effects=pltpu.SideEffectType.DATAFLOW_SIDE_EFFECTING),
    )(*srcs, *lands, *send_sems, *recv_sems, after)
    return outs[n:]


def _adaln_mod(c, w_mod, b_mod):
    def body(c_ref, w_ref, b_ref, mod_ref, sc_ref):
        sc = _silu(c_ref[...])
        sc8 = jnp.broadcast_to(sc, (8, D_MODEL))
        mod_ref[...] = _nn(sc8, w_ref[...])[0:1] + b_ref[...]
        sc_ref[...] = sc

    return pl.pallas_call(body, name="adaln_mod", compiler_params=_params(),
                          out_shape=(SDS((1, 3 * D_MODEL), F32), SDS((1, D_MODEL), F32)))(c, w_mod, b_mod)


def _ln_proj(x, mod, norm_w, ws, cos_t, sin_t, conv_w8, alog_row, dtb_row, ts):
    s = x.shape[0]
    widths = [w.shape[1] for w in ws]

    def body(x_ref, mod_ref, nw_ref, cos_ref, sin_ref, cw_ref, al_ref, dtb_ref, wqkv, wz, wba, waq, wak, wav, waz,
             h_ref, oqkv, oz, oba, oq, ok, ov, oaz, q_ref, k_ref, v_ref, bg_ref, halo):
        n = pl.program_id(0)
        xt = x_ref[...]
        r = lax.rsqrt(jnp.mean(xt * xt, axis=-1, keepdims=True) + EPS)
        shift, scale = mod_ref[:, 0:D_MODEL], mod_ref[:, D_MODEL:2 * D_MODEL]
        h = ((xt * r) * nw_ref[...]) * (1.0 + scale) + shift
        hb = _bf(h)
        h_ref[...] = hb
        pre = jnp.dot(hb, wqkv[...], preferred_element_type=F32)
        oqkv[...] = pre
        ext = jnp.concatenate([jnp.where(n == 0, 0.0, halo[...]), pre], axis=0)
        halo[...] = pre[ts - 8:ts]
        taps = _conv_taps(ext, ts)
        conv = taps[0] * cw_ref[0:1, :]
        for j in range(1, CONV_K):
            conv = conv + taps[j] * cw_ref[j:j + 1, :]
        for hd in range(DN_HEADS):
            cols = slice(hd * DN_DIM, (hd + 1) * DN_DIM)
            q_ref[:, cols] = _post_q(conv[:, hd * DN_DIM:(hd + 1) * DN_DIM])
            k_ref[:, cols] = _post_k(conv[:, DN_WIDTH + hd * DN_DIM:DN_WIDTH + (hd + 1) * DN_DIM])
            v_ref[:, cols] = _post_v(conv[:, 2 * DN_WIDTH + hd * DN_DIM:2 * DN_WIDTH + (hd + 1) * DN_DIM])
        ba = jnp.dot(hb, wba[...], preferred_element_type=F32)
        oba[...] = ba
        bg = _beta_decay(ba, al_ref[...], dtb_ref[...])
        lane = lax.broadcasted_iota(jnp.int32, bg.shape, 1)
        run = pltpu.roll(_chunk_cumsum(bg), DN_HEADS, 1)
        bg_ref[...] = jnp.where((lane >= GC_LANE) & (lane < GC_LANE + DN_HEADS), run, bg)
        oz[...] = jnp.dot(hb, wz[...], preferred_element_type=F32)
        oaz[...] = jnp.dot(hb, waz[...], preferred_element_type=F32)
        tv = jnp.dot(hb, wav[...], preferred_element_type=F32)
        for j in range(AT_PAIRS):
            ov[j] = tv[:, j * LANES:(j + 1) * LANES]
        cs, sn = cos_ref[...], sin_ref[...]
        for w_ref, o_ref in ((waq, oq), (wak, ok)):
            t = jnp.dot(hb, w_ref[...], preferred_element_type=F32)
            for j in range(AT_PAIRS):
                tj = t[:, j * LANES:(j + 1) * LANES]
                o_ref[j] = tj * cs + _swap_half64(tj) * sn

    tok = lambda w: pl.BlockSpec((ts, w), lambda i: (i, 0))
    full = lambda a: pl.BlockSpec(a.shape, lambda i: (0, 0))
    pairs = pl.BlockSpec((AT_PAIRS, ts, LANES), lambda i: (0, i, 0))
    return pl.pallas_call(
        body, name="ln_proj", grid=(s // ts,), compiler_params=_params("arbitrary"),
        in_specs=[tok(D_MODEL), full(mod), full(norm_w), tok(LANES), tok(LANES), full(conv_w8), full(alog_row),
                  full(dtb_row)] + [full(w) for w in ws],
        out_specs=(tok(D_MODEL), tok(widths[0]), tok(widths[1]), tok(widths[2]), pairs, pairs, pairs,
                   tok(widths[6]), tok(DN_WIDTH), tok(DN_WIDTH), tok(DN_WIDTH), tok(BA_PAD)),
        out_shape=(SDS((s, D_MODEL), BF16), SDS((s, widths[0]), F32), SDS((s, widths[1]), F32),
                   SDS((s, widths[2]), F32)) + (SDS((AT_PAIRS, s, LANES), F32),) * 3 + (SDS((s, widths[6]), F32),)
        + (SDS((s, DN_WIDTH), F32),) * 3 + (SDS((s, BA_PAD), F32),),
        scratch_shapes=[pltpu.VMEM((8, widths[0]), F32)],
    )(x, mod, norm_w, cos_t, sin_t, conv_w8, alog_row, dtb_row, *ws)


def _conv_taps(ext, rows):
    taps = []
    for j in range(CONV_K):
        sh = CONV_K - 1 - j
        rolled = pltpu.roll(ext, sh, 0) if sh else ext
        taps.append(rolled[8:8 + rows])
    return taps


def _dn_chunk_prep(q, k, v, bg, ts):
    s = q.shape[0]
    ncs = ts // CHUNK

    def body(q_ref, k_ref, v_ref, bg_ref, u_ref, w_ref, qd_ref, kd_ref, p_ref, gl_ref, t_ref):
        def chunks(cg, carry):
            where = []
            for ci in (cg * CH_UNROLL + i for i in range(CH_UNROLL)):
                rows = pl.ds(pl.multiple_of(ci * CHUNK, CHUNK), CHUNK)
                rows8 = pl.ds(pl.multiple_of(ci * 8, 8), 8)
                where += [(rows, rows8, h, slice(h * DN_DIM, (h + 1) * DN_DIM)) for h in range(DN_HEADS)]
            bgs = [bg_ref[rows, :] for rows, _, _, _ in where]
            outs = _chunk_fwd([q_ref[rows, c] for rows, _, _, c in where], [k_ref[rows, c] for rows, _, _, c in where],
                              [v_ref[rows, c] for rows, _, _, c in where],
                              [b[:, h:h + 1] for b, (_, _, h, _) in zip(bgs, where)],
                              [b[:, GC_LANE + h:GC_LANE + h + 1] for b, (_, _, h, _) in zip(bgs, where)])
            for (rows, rows8, h, c), (u, w, p, qd, kd, gl, t) in zip(where, outs):
                u_ref[rows, c] = u
                w_ref[rows, c] = w
                qd_ref[rows, c] = qd
                kd_ref[rows, c] = kd
                p_ref[h, rows, :] = p
                t_ref[h, rows, :] = t
                gl_ref[rows8, c] = jnp.broadcast_to(gl, (8, DN_DIM))
            return carry

        lax.fori_loop(0, ncs // CH_UNROLL, chunks, 0)

    tok = lambda w: pl.BlockSpec((ts, w), lambda i: (i, 0))
    sq = pl.BlockSpec((DN_HEADS, ts, CHUNK), lambda i: (0, i, 0))
    return pl.pallas_call(
        body, name="dn_chunk_prep", grid=(s // ts,), compiler_params=_params("arbitrary"),
        in_specs=[tok(DN_WIDTH)] * 3 + [tok(BA_PAD)],
        out_specs=(tok(DN_WIDTH),) * 4 + (sq, pl.BlockSpec((ncs * 8, DN_WIDTH), lambda i: (i, 0)), sq),
        out_shape=(SDS((s, DN_WIDTH), F32),) * 4 + (SDS((DN_HEADS, s, CHUNK), F32),
                                                     SDS((s // CHUNK * 8, DN_WIDTH), F32),
                                                     SDS((DN_HEADS, s, CHUNK), F32)),
    )(q, k, v, bg)


def _dn_scan(u, w, qd, kd, p, gl, ts):
    s = u.shape[0]
    ncs = ts // CHUNK

    def body(u_ref, w_ref, qd_ref, kd_ref, p_ref, gl_ref, o_ref, vn_ref, st_ref, state):
        @pl.when(pl.program_id(0) == 0)
        def _():
            state[...] = jnp.zeros_like(state)

        def chunk(ci, carry):
            rows = pl.ds(pl.multiple_of(ci * CHUNK, CHUNK), CHUNK)
            rows8 = pl.ds(pl.multiple_of(ci * 8, 8), 8)
            srows = pl.ds(pl.multiple_of(ci * DN_DIM, DN_DIM), DN_DIM)
            hs = range(DN_HEADS)
            sl = [slice(h * DN_DIM, (h + 1) * DN_DIM) for h in hs]
            sf = [state[h] for h in hs]
            sb = [_bf(x) for x in sf]
            ws = [_nn(w_ref[rows, c], b) for c, b in zip(sl, sb)]
            qs = [_nn(qd_ref[rows, c], b) for c, b in zip(sl, sb)]
            vn = [u_ref[rows, c] - x for c, x in zip(sl, ws)]
            vb = [_bf(x) for x in vn]
            kv = [_tn(kd_ref[rows, c], b) for c, b in zip(sl, vb)]
            pv = [_nn(p_ref[h, rows, :], b) for h, b in zip(hs, vb)]
            for h in hs:
                state[h] = sf[h] * gl_ref[rows8, sl[h]][0:1] + kv[h]
            for h in hs:
                st_ref[srows, sl[h]] = sf[h]
                vn_ref[rows, sl[h]] = vn[h]
                o_ref[rows, sl[h]] = qs[h] + pv[h]
            return carry

        lax.fori_loop(0, ncs, chunk, 0)

    tok = lambda wd: pl.BlockSpec((ts, wd), lambda i: (i, 0))
    return pl.pallas_call(
        body, name="dn_scan", grid=(s // ts,), compiler_params=_params("arbitrary"),
        in_specs=[tok(DN_WIDTH)] * 4 + [pl.BlockSpec((DN_HEADS, ts, CHUNK), lambda i: (0, i, 0)),
                                        pl.BlockSpec((ncs * 8, DN_WIDTH), lambda i: (i, 0))],
        out_specs=(tok(DN_WIDTH), tok(DN_WIDTH), pl.BlockSpec((ncs * DN_DIM, DN_WIDTH), lambda i: (i, 0))),
        out_shape=(SDS((s, DN_WIDTH), F32), SDS((s, DN_WIDTH), F32), SDS((s // CHUNK * DN_DIM, DN_WIDTH), F32)),
        scratch_shapes=[pltpu.VMEM((DN_HEADS, DN_DIM, DN_DIM), F32)],
    )(u, w, qd, kd, p, gl)


LOG2E, LN2 = 1.4426950408889634, 0.6931471805599453
MASKED = -1e30


def _band_bias():
    qi = lax.broadcasted_iota(jnp.int32, (Q_BLOCK, 2 * Q_BLOCK), 0)
    kj = lax.broadcasted_iota(jnp.int32, (Q_BLOCK, 2 * Q_BLOCK), 1)
    rel = Q_BLOCK + qi - kj
    return jnp.where((rel >= 0) & (rel <= W_SUB), 0.0, MASKED)


def _first_bias(first):
    kj = lax.broadcasted_iota(jnp.int32, (1, 2 * Q_BLOCK), 1)
    return jnp.where((kj < Q_BLOCK) & first, MASKED, 0.0)


def _attn_combo(c, d):
    if d == 1:
        qs = pl.multiple_of(c * Q_BLOCK, Q_BLOCK)
        return qs, pl.multiple_of(ATT_BLK - Q_BLOCK + c * Q_BLOCK, Q_BLOCK), c == 0
    r, m = c % d, c // d
    qs = r + (d * Q_BLOCK) * m
    return qs, ATT_BLK + qs - d * Q_BLOCK, m == 0


def _rows(start, size, d):
    return pl.ds(pl.multiple_of(start, Q_BLOCK), size) if d == 1 else pl.ds(start, size, stride=d)


def _shift_in(ext, cur, n):
    @pl.when(n == 0)
    def _():
        ext[0:ATT_BLK, :] = jnp.zeros((ATT_BLK, LANES), F32)

    @pl.when(n > 0)
    def _():
        ext[0:ATT_BLK, :] = ext[ATT_BLK:2 * ATT_BLK, :]

    ext[ATT_BLK:2 * ATT_BLK, :] = cur


def _attn_fwd(qr, kr, vv):
    s = qr.shape[1]
    nblk = s // ATT_BLK
    scale = AT_DIM ** -0.5
    npat = len(DILATIONS)

    def body(q_ref, k_ref, v_ref, o_ref, lse_ref, kext, vext, o_p, l_p, bias_ref):
        n = pl.program_id(1)
        _shift_in(kext, k_ref[0], n)
        _shift_in(vext, v_ref[0], n)
        bias_ref[...] = _band_bias()
        lo = lax.broadcasted_iota(jnp.int32, (Q_BLOCK, LANES), 1) < AT_DIM
        for pi, d in enumerate(DILATIONS):
            def group(g, carry, pi=pi, d=d):
                cs = [_attn_combo(g * ATT_UNROLL + u, d) for u in range(ATT_UNROLL)]
                heads = [(i, sel) for i in range(ATT_UNROLL) for sel in (lo, ~lo)]
                band = bias_ref[...]
                bias = [band + _first_bias((n == 0) & m0) for _, _, m0 in cs]
                qb = [_bf(q_ref[0, _rows(qs, Q_BLOCK, d), :]) for qs, _, _ in cs]
                kk = [_bf(kext[_rows(ks, 2 * Q_BLOCK, d), :]) for _, ks, _ in cs]
                vb = [_bf(vext[_rows(ks, 2 * Q_BLOCK, d), :]) for _, ks, _ in cs]
                sc = [lax.dot_general(jnp.where(sel, qb[i], jnp.zeros_like(qb[i])), kk[i], _NT,
                                      preferred_element_type=F32) for i, sel in heads]
                sc = [x * (scale * LOG2E) + bias[i] for x, (i, _) in zip(sc, heads)]
                mx = [jnp.max(x, axis=-1, keepdims=True) for x in sc]
                pr = [jnp.exp2(x - m) for x, m in zip(sc, mx)]
                ls = [jnp.sum(x, axis=-1, keepdims=True) for x in pr]
                pv = [jnp.dot(_bf(x), vb[i], preferred_element_type=F32) for x, (i, _) in zip(pr, heads)]
                outs = [x / l for x, l in zip(pv, ls)]
                lses = [m * LN2 + jnp.log(l) for m, l in zip(mx, ls)]
                for i, (qs, _, _) in enumerate(cs):
                    o_p[pi, _rows(qs, Q_BLOCK, d), :] = jnp.where(lo, outs[2 * i], outs[2 * i + 1])
                    l_p[pi, _rows(qs, Q_BLOCK, d), :] = jnp.where(lo, lses[2 * i], lses[2 * i + 1])
                return carry

            lax.fori_loop(0, ATT_BLK // Q_BLOCK // ATT_UNROLL, group, 0)

        def merge(i, carry):
            rows = pl.ds(pl.multiple_of(i * 256, 256), 256)
            ls = [l_p[pi, rows, :] for pi in range(npat)]
            mx = jnp.maximum(jnp.maximum(ls[0], ls[1]), ls[2])
            es = [jnp.exp(l - mx) for l in ls]
            den = es[0] + es[1] + es[2]
            o_ref[0, rows, :] = (es[0] * o_p[0, rows, :] + es[1] * o_p[1, rows, :] + es[2] * o_p[2, rows, :]) / den
            lse_ref[0, rows, :] = mx + jnp.log(den)
            return carry

        lax.fori_loop(0, ATT_BLK // 256, merge, 0)

    blk = pl.BlockSpec((1, ATT_BLK, LANES), lambda j, n: (j, n, 0))
    return pl.pallas_call(
        body, name="attn_fwd", grid=(AT_PAIRS, nblk), compiler_params=_params("arbitrary", "arbitrary"),
        in_specs=[blk] * 3, out_specs=(blk, blk),
        out_shape=(SDS((AT_PAIRS, s, LANES), F32),) * 2,
        scratch_shapes=[pltpu.VMEM((2 * ATT_BLK, LANES), F32), pltpu.VMEM((2 * ATT_BLK, LANES), F32),
                        pltpu.VMEM((npat, ATT_BLK, LANES), F32), pltpu.VMEM((npat, ATT_BLK, LANES), F32),
                        pltpu.VMEM((Q_BLOCK, 2 * Q_BLOCK), F32)],
    )(qr, kr, vv)


def _out_loss(o_dn, z_dn, o_at, z_at, dnw, atw2, x, tgt, w_out, gate, fw, ts):
    s = x.shape[0]

    def body(odn, zdn, oat, zat, dnw_ref, atw_ref, x_ref, t_ref, w_ref, g_ref, fw_ref,
             dx2_ref, gw_ref, dfw_ref, dgate_ref, loss_ref, dodn, dzdn, doat, dzat, delta, ddnw, datw):
        @pl.when(pl.program_id(0) == 0)
        def _():
            for ref in (gw_ref, dfw_ref, dgate_ref, loss_ref, ddnw, datw):
                ref[...] = jnp.zeros_like(ref)

        parts, vjps = [], []
        for h in range(DN_HEADS):
            cols = slice(h * DN_DIM, (h + 1) * DN_DIM)
            y, vjp = jax.vjp(_gate_dn, odn[:, cols], zdn[:, cols], dnw_ref[...])
            parts.append(_bf(y))
            vjps.append(vjp)
        for j in range(AT_PAIRS):
            y, vjp = jax.vjp(functools.partial(_gate_at, head_sum=_d_head_sum), oat[j],
                             zat[:, j * LANES:(j + 1) * LANES], atw_ref[...])
            parts.append(_bf(y))
            vjps.append(vjp)
        catb = jnp.concatenate(parts, axis=1)
        wb = w_ref[...]
        gate, fwv = g_ref[...], fw_ref[...]
        mix = jnp.dot(catb, wb, preferred_element_type=F32)
        x2 = x_ref[...] + gate * mix
        r2 = lax.rsqrt(jnp.mean(x2 * x2, axis=-1, keepdims=True) + EPS)
        xn2 = x2 * r2
        err = xn2 * fwv - t_ref[...]
        row = jnp.sum(err * err, axis=-1, keepdims=True) * (1.0 / D_MODEL)
        loss_ref[...] += 0.5 * jnp.sum(row, axis=0, keepdims=True)
        dy = err * (1.0 / D_MODEL)
        dfw_ref[...] += jnp.sum(dy * xn2, axis=0, keepdims=True)
        dxn = dy * fwv
        dx2 = r2 * (dxn - xn2 * jnp.mean(dxn * xn2, axis=-1, keepdims=True))
        dx2_ref[...] = dx2
        dgate_ref[...] += jnp.sum(dx2 * mix, axis=0, keepdims=True)
        dmix = _bf(gate * dx2)
        dcat = lax.dot_general(dmix, wb, _NT, preferred_element_type=F32)
        gw_ref[...] += lax.dot_general(catb, dmix, _TN, preferred_element_type=F32)
        for h in range(DN_HEADS):
            cols = slice(h * DN_DIM, (h + 1) * DN_DIM)
            do, dz, dw = vjps[h](dcat[:, cols])
            dodn[:, cols] = do
            dzdn[:, cols] = _bf(dz)
            ddnw[...] += dw
        for j in range(AT_PAIRS):
            cols = slice(j * LANES, (j + 1) * LANES)
            do, dz, dw = vjps[DN_HEADS + j](dcat[:, DN_WIDTH + j * LANES:DN_WIDTH + (j + 1) * LANES])
            doat[j] = do
            dzat[:, cols] = _bf(dz)
            datw[...] += dw
            delta[j] = _head_sum(do * oat[j])

    tok = lambda w: pl.BlockSpec((ts, w), lambda i: (i, 0))
    full = lambda a: pl.BlockSpec(a.shape, lambda i: (0, 0))
    row = pl.BlockSpec((1, D_MODEL), lambda i: (0, 0))
    lrow = pl.BlockSpec((1, LANES), lambda i: (0, 0))
    pairs = pl.BlockSpec((AT_PAIRS, ts, LANES), lambda i: (0, i, 0))
    return pl.pallas_call(
        body, name="out_loss", grid=(s // ts,), compiler_params=_params("arbitrary"),
        in_specs=[tok(DN_WIDTH), tok(DN_WIDTH), pairs, tok(AT_WIDTH), full(dnw), full(atw2),
                  tok(D_MODEL), tok(D_MODEL), full(w_out), full(gate), full(fw)],
        out_specs=(tok(D_MODEL), pl.BlockSpec((D_MODEL, D_MODEL), lambda i: (0, 0)), row, row,
                   pl.BlockSpec((1, 1), lambda i: (0, 0)), tok(DN_WIDTH), tok(DN_WIDTH), pairs, tok(AT_WIDTH), pairs,
                   lrow, lrow),
        out_shape=(SDS((s, D_MODEL), F32), SDS((D_MODEL, D_MODEL), F32), SDS((1, D_MODEL), F32),
                   SDS((1, D_MODEL), F32), SDS((1, 1), F32), SDS((s, DN_WIDTH), F32), SDS((s, DN_WIDTH), BF16),
                   SDS((AT_PAIRS, s, LANES), F32), SDS((s, AT_WIDTH), BF16), SDS((AT_PAIRS, s, LANES), F32),
                   SDS((1, LANES), F32), SDS((1, LANES), F32)),
    )(o_dn, z_dn, o_at, z_at, dnw, atw2, x, tgt, w_out, gate, fw)


def _shift_acc(ext, n):
    @pl.when(n == 0)
    def _():
        ext[0:ATT_BLK, :] = jnp.zeros((ATT_BLK, LANES), F32)

    @pl.when(n > 0)
    def _():
        ext[0:ATT_BLK, :] = ext[ATT_BLK:2 * ATT_BLK, :]

    ext[ATT_BLK:2 * ATT_BLK, :] = jnp.zeros((ATT_BLK, LANES), F32)


def _attn_bwd(qr, kr, vv, do, lse, delta):
    s = qr.shape[1]
    nblk = s // ATT_BLK
    scale = AT_DIM ** -0.5

    def body(q_ref, k_ref, v_ref, do_ref, lse_ref, dl_ref, dq_ref, dk_ref, dv_ref, kext, vext, dkext, dvext,
             bias_ref):
        n = pl.program_id(1)
        _shift_in(kext, k_ref[0], n)
        _shift_in(vext, v_ref[0], n)
        _shift_acc(dkext, n)
        _shift_acc(dvext, n)
        bias_ref[...] = _band_bias()

        @pl.when(n < nblk)
        def _():
            dq_ref[0] = jnp.zeros((ATT_BLK, LANES), F32)
            lo = lax.broadcasted_iota(jnp.int32, (Q_BLOCK, LANES), 1) < AT_DIM
            for d in DILATIONS:
                def group(g, carry, d=d):
                    nu = ATT_UNROLL_BWD
                    cs = [_attn_combo(g * nu + u, d) for u in range(nu)]
                    heads = [(i, sel) for i in range(nu) for sel in (lo, ~lo)]
                    qrows = [_rows(qs, Q_BLOCK, d) for qs, _, _ in cs]
                    krows = [_rows(ks, 2 * Q_BLOCK, d) for _, ks, _ in cs]
                    band = bias_ref[...]
                    bias = [band + _first_bias((n == 0) & m0) for _, _, m0 in cs]
                    qb = [_bf(q_ref[0, r, :]) for r in qrows]
                    dob = [_bf(do_ref[0, r, :]) for r in qrows]
                    kk = [_bf(kext[r, :]) for r in krows]
                    vb = [_bf(vext[r, :]) for r in krows]
                    lse2 = [lse_ref[0, r, :] * LOG2E for r in qrows]
                    dl2 = [dl_ref[0, r, :] for r in qrows]
                    qm = [jnp.where(sel, qb[i], jnp.zeros_like(qb[i])) for i, sel in heads]
                    dom = [jnp.where(sel, dob[i], jnp.zeros_like(dob[i])) for i, sel in heads]
                    lse_c = [jnp.max(jnp.where(sel, lse2[i], -jnp.inf), axis=-1, keepdims=True) for i, sel in heads]
                    dl_c = [jnp.max(jnp.where(sel, dl2[i], -jnp.inf), axis=-1, keepdims=True) for i, sel in heads]
                    sc = [lax.dot_general(a, kk[i], _NT, preferred_element_type=F32) for a, (i, _) in zip(qm, heads)]
                    dp = [lax.dot_general(a, vb[i], _NT, preferred_element_type=F32) for a, (i, _) in zip(dom, heads)]
                    pr = [jnp.exp2(x * (scale * LOG2E) + bias[i] - l) for x, l, (i, _) in zip(sc, lse_c, heads)]
                    ds = [_bf(p * (x - dl) * scale) for p, x, dl in zip(pr, dp, dl_c)]
                    prb = [_bf(p) for p in pr]
                    dq = [jnp.dot(x, kk[i], preferred_element_type=F32) for x, (i, _) in zip(ds, heads)]
                    dk = [lax.dot_general(x, a, _TN, preferred_element_type=F32) for x, a in zip(ds, qm)]
                    dv = [lax.dot_general(x, a, _TN, preferred_element_type=F32) for x, a in zip(prb, dom)]
                    for i in range(nu):
                        dq_ref[0, qrows[i], :] += jnp.where(lo, dq[2 * i], dq[2 * i + 1])
                        dkext[krows[i], :] += dk[2 * i] + dk[2 * i + 1]
                        dvext[krows[i], :] += dv[2 * i] + dv[2 * i + 1]
                    return carry

                lax.fori_loop(0, ATT_BLK // Q_BLOCK // ATT_UNROLL_BWD, group, 0)

        dk_ref[0] = dkext[0:ATT_BLK, :]
        dv_ref[0] = dvext[0:ATT_BLK, :]

    cur = pl.BlockSpec((1, ATT_BLK, LANES), lambda j, n: (j, jnp.minimum(n, nblk - 1), 0))
    done = pl.BlockSpec((1, ATT_BLK, LANES), lambda j, n: (j, jnp.maximum(n - 1, 0), 0))
    return pl.pallas_call(
        body, name="attn_bwd", grid=(AT_PAIRS, nblk + 1), compiler_params=_params("arbitrary", "arbitrary"),
        in_specs=[cur] * 6, out_specs=(cur, done, done),
        out_shape=(SDS((AT_PAIRS, s, LANES), F32),) * 3,
        scratch_shapes=[pltpu.VMEM((2 * ATT_BLK, LANES), F32)] * 4 + [pltpu.VMEM((Q_BLOCK, 2 * Q_BLOCK), F32)],
    )(qr, kr, vv, do, lse, delta)


def _dn_scan_bwd(do, st, vn, w, qd, kd, p, gl, ts):
    s = do.shape[0]
    ncs = ts // CHUNK
    nt = s // ts

    def body(do_ref, st_ref, vn_ref, w_ref, qd_ref, kd_ref, p_ref, gl_ref,
             du_ref, dw_ref, dqd_ref, dkd_ref, dp_ref, dgl_ref, dstate):
        @pl.when(pl.program_id(0) == 0)
        def _():
            dstate[...] = jnp.zeros_like(dstate)

        def chunk(jr, carry):
            ci = ncs - 1 - jr
            rows = pl.ds(pl.multiple_of(ci * CHUNK, CHUNK), CHUNK)
            rows8 = pl.ds(pl.multiple_of(ci * 8, 8), 8)
            srows = pl.ds(pl.multiple_of(ci * DN_DIM, DN_DIM), DN_DIM)
            hs = range(DN_HEADS)
            sl = [slice(h * DN_DIM, (h + 1) * DN_DIM) for h in hs]
            ds_ = [dstate[h] for h in hs]
            dsb = [_bf(x) for x in ds_]
            dob = [_bf(do_ref[rows, c]) for c in sl]
            pdo = [_tn(p_ref[h, rows, :], b) for h, b in zip(hs, dob)]
            qdo = [_tn(qd_ref[rows, c], b) for c, b in zip(sl, dob)]
            dvn = [_nn(kd_ref[rows, c], b) + x for c, b, x in zip(sl, dsb, pdo)]
            dvb = [_bf(x) for x in dvn]
            wdv = [_tn(w_ref[rows, c], b) for c, b in zip(sl, dvb)]
            for h in hs:
                dstate[h] = ds_[h] * gl_ref[rows8, sl[h]][0:1] + qdo[h] - wdv[h]
            sfs = [st_ref[srows, c] for c in sl]
            sbs = [_bf(x) for x in sfs]
            vnb = [_bf(vn_ref[rows, c]) for c in sl]
            for h in hs:
                du_ref[rows, sl[h]] = dvn[h]
                dw_ref[rows, sl[h]] = -_nt(dvb[h], sbs[h])
                dqd_ref[rows, sl[h]] = _nt(dob[h], sbs[h])
                dkd_ref[rows, sl[h]] = _nt(vnb[h], dsb[h])
                dp_ref[h, rows, :] = _nt(dob[h], vnb[h])
                dgl = jnp.sum(jnp.sum(ds_[h] * sfs[h], axis=1, keepdims=True), axis=0, keepdims=True)
                dgl_ref[rows8, sl[h]] = jnp.broadcast_to(dgl, (8, DN_DIM))
            return carry

        lax.fori_loop(0, ncs, chunk, 0)

    tok = lambda wd: pl.BlockSpec((ts, wd), lambda i: (nt - 1 - i, 0))
    pspec = pl.BlockSpec((DN_HEADS, ts, CHUNK), lambda i: (0, nt - 1 - i, 0))
    g8 = pl.BlockSpec((ncs * 8, DN_WIDTH), lambda i: (nt - 1 - i, 0))
    return pl.pallas_call(
        body, name="dn_scan_bwd", grid=(nt,), compiler_params=_params("arbitrary"),
        in_specs=[tok(DN_WIDTH), pl.BlockSpec((ncs * DN_DIM, DN_WIDTH), lambda i: (nt - 1 - i, 0))]
        + [tok(DN_WIDTH)] * 4 + [pspec, g8],
        out_specs=(tok(DN_WIDTH),) * 4 + (pspec, g8),
        out_shape=(SDS((s, DN_WIDTH), F32),) * 4 + (SDS((DN_HEADS, s, CHUNK), F32),
                                                     SDS((s // CHUNK * 8, DN_WIDTH), F32)),
        scratch_shapes=[pltpu.VMEM((DN_HEADS, DN_DIM, DN_DIM), F32)],
    )(do, st, vn, w, qd, kd, p, gl)


def _dn_chunk_bwd(q, k, v, bg, t, du, dw, dqd, dkd, dp, dgl, ts):
    s = q.shape[0]
    ncs = ts // CHUNK

    def body(q_ref, k_ref, v_ref, bg_ref, t_ref, du_ref, dw_ref, dqd_ref, dkd_ref, dp_ref, dgl_ref,
             dq_ref, dk_ref, dv_ref, dbg_ref):
        def chunks(cg, carry):
            lane = lax.broadcasted_iota(jnp.int32, (CHUNK, BA_PAD), 1)
            where = []
            for ci in (cg * CH_UNROLL + i for i in range(CH_UNROLL)):
                rows = pl.ds(pl.multiple_of(ci * CHUNK, CHUNK), CHUNK)
                rows8 = pl.ds(pl.multiple_of(ci * 8, 8), 8)
                where += [(rows, rows8, h, slice(h * DN_DIM, (h + 1) * DN_DIM)) for h in range(DN_HEADS)]
            bgs = [bg_ref[rows, :] for rows, _, _, _ in where]
            cots = [(du_ref[rows, c], dw_ref[rows, c], dp_ref[h, rows, :], dqd_ref[rows, c], dkd_ref[rows, c],
                     dgl_ref[rows8, c][0:1, 0:1]) for rows, rows8, h, c in where]
            outs = _chunk_bwd([q_ref[rows, c] for rows, _, _, c in where], [k_ref[rows, c] for rows, _, _, c in where],
                              [v_ref[rows, c] for rows, _, _, c in where],
                              [b[:, h:h + 1] for b, (_, _, h, _) in zip(bgs, where)],
                              [b[:, GC_LANE + h:GC_LANE + h + 1] for b, (_, _, h, _) in zip(bgs, where)],
                              [t_ref[h, rows, :] for rows, _, h, _ in where], cots)
            for i in range(CH_UNROLL):
                dbg = jnp.zeros((CHUNK, BA_PAD), F32)
                for (rows, _, h, c), (dq, dk, dv, dbeta, dgc) in list(zip(where, outs))[i * DN_HEADS:(i + 1) * DN_HEADS]:
                    dq_ref[rows, c] = dq
                    dk_ref[rows, c] = dk
                    dv_ref[rows, c] = dv
                    dbg = dbg + jnp.where(lane == h, dbeta, 0.0) + jnp.where(lane == GC_LANE + h, dgc, 0.0)
                dbg_ref[where[i * DN_HEADS][0], :] = dbg
            return carry

        lax.fori_loop(0, ncs // CH_UNROLL, chunks, 0)

    tok = lambda wd: pl.BlockSpec((ts, wd), lambda i: (i, 0))
    pspec = pl.BlockSpec((DN_HEADS, ts, CHUNK), lambda i: (0, i, 0))
    g8 = pl.BlockSpec((ncs * 8, DN_WIDTH), lambda i: (i, 0))
    return pl.pallas_call(
        body, name="dn_chunk_bwd", grid=(s // ts,), compiler_params=_params("arbitrary"),
        in_specs=[tok(DN_WIDTH)] * 3 + [tok(BA_PAD), pspec] + [tok(DN_WIDTH)] * 4 + [pspec, g8],
        out_specs=(tok(DN_WIDTH),) * 3 + (tok(BA_PAD),),
        out_shape=(SDS((s, DN_WIDTH), F32),) * 3 + (SDS((s, BA_PAD), F32),),
    )(q, k, v, bg, t, du, dw, dqd, dkd, dp, dgl)


def _dn_prep_bwd(qkv_pre, ba, dq, dk, dv, dbg, conv_w8, alog_row, dtb_row, hbf, dz_dn, ts):
    s = qkv_pre.shape[0]
    cw = 3 * DN_WIDTH
    nt = s // ts

    def body(pre_ref, ph_ref, nh_ref, ba_ref, dq_ref, dqh_ref, dk_ref, dkh_ref, dv_ref, dvh_ref, dbg_ref,
             cw_ref, al_ref, dtb_ref, h_ref, dz_ref, dpre_ref, dba_ref, dcw_ref, dal_ref, ddtb_ref,
             gqkv_ref, gz_ref, gba_ref):
        n = pl.program_id(0)

        @pl.when(n == 0)
        def _():
            gqkv_ref[...] = jnp.zeros_like(gqkv_ref)
            gz_ref[...] = jnp.zeros_like(gz_ref)
            gba_ref[...] = jnp.zeros_like(gba_ref)
            dcw_ref[...] = jnp.zeros_like(dcw_ref)
            dal_ref[...] = jnp.zeros_like(dal_ref)
            ddtb_ref[...] = jnp.zeros_like(ddtb_ref)

        last = n == nt - 1
        prev = jnp.where(n == 0, 0.0, ph_ref[...])
        ext = jnp.concatenate([prev, pre_ref[...], nh_ref[...]], axis=0)
        taps = _conv_taps(ext, ts + 8)
        conv = taps[0] * cw_ref[0:1, :]
        for j in range(1, CONV_K):
            conv = conv + taps[j] * cw_ref[j:j + 1, :]

        def cot(main, halo, cols):
            return jnp.concatenate([main[:, cols], jnp.where(last, 0.0, halo[:, cols])], axis=0)

        pieces = []
        for grp, (fn, mref, href) in enumerate(((_post_q, dq_ref, dqh_ref), (_post_k, dk_ref, dkh_ref),
                                                (_post_v, dv_ref, dvh_ref))):
            for h in range(DN_HEADS):
                cols = slice(h * DN_DIM, (h + 1) * DN_DIM)
                c0 = grp * DN_WIDTH + h * DN_DIM
                _, vjp = jax.vjp(fn, conv[:, c0:c0 + DN_DIM])
                pieces.append(vjp(cot(mref, href, cols))[0])
        dconv = jnp.concatenate(pieces, axis=1)
        rows = ts + 8
        dpre = dconv[:ts] * cw_ref[CONV_K - 1:CONV_K, :]
        for j in range(CONV_K - 1):
            sh = CONV_K - 1 - j
            dpre = dpre + pltpu.roll(dconv, rows - sh, 0)[:ts] * cw_ref[j:j + 1, :]
        dpre_b = _bf(dpre)
        dpre_ref[...] = dpre_b
        hb = h_ref[...]
        gqkv_ref[...] += lax.dot_general(hb, dpre_b, _TN, preferred_element_type=F32)
        gz_ref[...] += lax.dot_general(hb, dz_ref[...], _TN, preferred_element_type=F32)
        for j in range(CONV_K):
            dcw_ref[j:j + 1, :] += jnp.sum(dconv[:ts] * taps[j][:ts], axis=0, keepdims=True)

        dbg = dbg_ref[...]
        lane = lax.broadcasted_iota(jnp.int32, dbg.shape, 1)
        dg = pltpu.roll(_chunk_cumsum(dbg, reverse=True), BA_PAD - DN_HEADS, 1)
        cot_bg = jnp.where(lane < DN_HEADS, dbg, jnp.where(lane < GC_LANE, dg, 0.0))
        _, vjp = jax.vjp(_beta_decay, ba_ref[...], al_ref[...], dtb_ref[...])
        dba, dal, ddtb = vjp(cot_bg)
        dba_b = _bf(dba)
        dba_ref[...] = dba_b
        gba_ref[...] += lax.dot_general(hb, dba_b, _TN, preferred_element_type=F32)
        dal_ref[...] += dal
        ddtb_ref[...] += ddtb

    tok = lambda w: pl.BlockSpec((ts, w), lambda i: (i, 0))
    full = lambda a: pl.BlockSpec(a.shape, lambda i: (0, 0))
    prevh = lambda w: pl.BlockSpec((8, w), lambda i: (jnp.maximum(i * (ts // 8) - 1, 0), 0))
    nexth = lambda w: pl.BlockSpec((8, w), lambda i: (jnp.minimum((i + 1) * (ts // 8), s // 8 - 1), 0))
    row = pl.BlockSpec((1, LANES), lambda i: (0, 0))
    return pl.pallas_call(
        body, name="dn_prep_bwd", grid=(nt,), compiler_params=_params("arbitrary"),
        in_specs=[tok(cw), prevh(cw), nexth(cw), tok(BA_PAD),
                  tok(DN_WIDTH), nexth(DN_WIDTH), tok(DN_WIDTH), nexth(DN_WIDTH), tok(DN_WIDTH), nexth(DN_WIDTH),
                  tok(BA_PAD), full(conv_w8), full(alog_row), full(dtb_row), tok(D_MODEL), tok(DN_WIDTH)],
        out_specs=(tok(cw), tok(BA_PAD), pl.BlockSpec((8, cw), lambda i: (0, 0)), row, row)
        + tuple(pl.BlockSpec((D_MODEL, w), lambda i: (0, 0)) for w in (cw, DN_WIDTH, BA_PAD)),
        out_shape=(SDS((s, cw), BF16), SDS((s, BA_PAD), BF16), SDS((8, cw), F32), SDS((1, LANES), F32),
                   SDS((1, LANES), F32)) + tuple(SDS((D_MODEL, w), F32) for w in (cw, DN_WIDTH, BA_PAD)),
    )(qkv_pre, qkv_pre, qkv_pre, ba, dq, dq, dk, dk, dv, dv, dbg, conv_w8, alog_row, dtb_row, hbf, dz_dn)


def _dh_dx(dps, ws, x, mod, norm_w, dx2, ts):
    s = x.shape[0]
    widths = [w.shape[1] for w in ws]
    np_ = len(ws)

    def body(*refs):
        dp_refs, w_refs = refs[:np_], refs[np_:2 * np_]
        x_ref, mod_ref, nw_ref, dx2_ref, gx_ref, dshift, dscale, dnw = refs[2 * np_:]

        @pl.when(pl.program_id(0) == 0)
        def _():
            dshift[...] = jnp.zeros_like(dshift)
            dscale[...] = jnp.zeros_like(dscale)
            dnw[...] = jnp.zeros_like(dnw)

        dh = lax.dot_general(dp_refs[0][...], w_refs[0][...], _NT, preferred_element_type=F32)
        for a, b in zip(dp_refs[1:], w_refs[1:]):
            dh = dh + lax.dot_general(a[...], b[...], _NT, preferred_element_type=F32)
        xt = x_ref[...]
        r = lax.rsqrt(jnp.mean(xt * xt, axis=-1, keepdims=True) + EPS)
        xn = xt * r
        nw = nw_ref[...]
        sc1 = 1.0 + mod_ref[:, D_MODEL:2 * D_MODEL]
        dshift[...] += jnp.sum(dh, axis=0, keepdims=True)
        dscale[...] += jnp.sum(dh * (xn * nw), axis=0, keepdims=True)
        dnw[...] += jnp.sum(dh * sc1 * xn, axis=0, keepdims=True)
        dxn = dh * sc1 * nw
        gx_ref[...] = r * (dxn - xn * jnp.mean(dxn * xn, axis=-1, keepdims=True)) + dx2_ref[...]

    tok = lambda w: pl.BlockSpec((ts, w), lambda i: (i, 0))
    full = lambda a: pl.BlockSpec(a.shape, lambda i: (0, 0))
    row = pl.BlockSpec((1, D_MODEL), lambda i: (0, 0))
    return pl.pallas_call(
        body, name="dh_dx", grid=(s // ts,), compiler_params=_params("arbitrary"),
        in_specs=[tok(w) for w in widths] + [full(w) for w in ws] + [tok(D_MODEL), full(mod), full(norm_w),
                                                                    tok(D_MODEL)],
        out_specs=(tok(D_MODEL), row, row, row),
        out_shape=(SDS((s, D_MODEL), F32),) + (SDS((1, D_MODEL), F32),) * 3,
    )(*dps, *ws, x, mod, norm_w, dx2)


def _grad_w_in_at(h, dq, dk, dv, dz_at, cos_t, sin_t, ts):
    s = h.shape[0]

    def body(h_ref, q_ref, k_ref, v_ref, dz_ref, cos_ref, sin_ref, oq, ok, ov, gq, gk, gv, gz):
        @pl.when(pl.program_id(0) == 0)
        def _():
            for o in (gq, gk, gv, gz):
                o[...] = jnp.zeros_like(o)

        cs, sn = cos_ref[...], sin_ref[...]
        for j in range(AT_PAIRS):
            cols = slice(j * LANES, (j + 1) * LANES)
            for g_ref, o_ref in ((q_ref, oq), (k_ref, ok)):
                g = g_ref[j]
                o_ref[:, cols] = _bf(g * cs + _swap_half64(g * sn))
            ov[:, cols] = _bf(v_ref[j])
        hb = h_ref[...]
        for p, o in ((oq, gq), (ok, gk), (ov, gv), (dz_ref, gz)):
            o[...] += lax.dot_general(hb, p[...], _TN, preferred_element_type=F32)

    tok = lambda w: pl.BlockSpec((ts, w), lambda i: (i, 0))
    pairs = pl.BlockSpec((AT_PAIRS, ts, LANES), lambda i: (0, i, 0))
    acc = pl.BlockSpec((D_MODEL, AT_WIDTH), lambda i: (0, 0))
    return pl.pallas_call(
        body, name="grad_w_in_at", grid=(s // ts,), compiler_params=_params("arbitrary"),
        in_specs=[tok(D_MODEL), pairs, pairs, pairs, tok(AT_WIDTH), tok(LANES), tok(LANES)],
        out_specs=(tok(AT_WIDTH),) * 3 + (acc,) * 4,
        out_shape=(SDS((s, AT_WIDTH), BF16),) * 3 + (SDS((D_MODEL, AT_WIDTH), F32),) * 4,
    )(h, dq, dk, dv, dz_at, cos_t, sin_t)


def _adamw_math(w, g, m, v):
    m = ADAM_B1 * m + (1.0 - ADAM_B1) * g
    v = ADAM_B2 * v + (1.0 - ADAM_B2) * (g * g)
    m_hat = m / (1.0 - ADAM_B1 ** ADAM_STEP)
    v_hat = v / (1.0 - ADAM_B2 ** ADAM_STEP)
    delta = -ADAM_LR * (m_hat / (jnp.sqrt(v_hat) + ADAM_EPS) + ADAM_WD * w)
    return delta, m, v


def _adamw(w, m, v, g, name, own=None):
    def body(w_ref, m_ref, v_ref, g_ref, *rest):
        g_out, d_out, m_out, v_out = rest[-4:]
        if own is None:
            g = g_ref[...]
        else:
            g = g_ref[0].astype(F32)
            for k in range(1, N_DEV):
                g = g + g_ref[k].astype(F32)
            g = g + rest[0][...].astype(F32)
        g_out[...] = g
        d_out[...], m_out[...], v_out[...] = _adamw_math(w_ref[...], g, m_ref[...], v_ref[...])

    args = (w, m, v, g) if own is None else (w, m, v, g, own)
    return pl.pallas_call(body, name=name, compiler_params=_params(),
                          out_shape=(SDS(w.shape, F32),) * 4)(*args)


def _adamw_w_mod(w, m, v, siluc_all, dmod_mine):
    def body(w_ref, m_ref, v_ref, sc_ref, dm_ref, g_out, d_out, m_out, v_out):
        g = _htn(sc_ref[...], dm_ref[...])
        g_out[...] = g
        d_out[...], m_out[...], v_out[...] = _adamw_math(w_ref[...], g, m_ref[...], v_ref[...])

    return pl.pallas_call(body, name="adamw_w_mod", compiler_params=_params(),
                          out_shape=(SDS(w.shape, F32),) * 4)(w, m, v, siluc_all, dmod_mine)


def _pack_sum(pack_all):
    def body(p_ref, o_ref):
        t = p_ref[0]
        for k in range(1, N_DEV):
            t = t + p_ref[k]
        o_ref[...] = t

    return pl.pallas_call(body, name="pack_sum", out_shape=SDS(pack_all.shape[1:], F32))(pack_all)


def _tile(s, want):
    t = min(want, s)
    assert s % t == 0
    return t


def _local_step(x, c, positions, w_mod_bf, b_mod, norm_w, w_in_bf, conv_w, a_log, dt_bias, dn_norm_w, at_norm_w,
                w_out_bf, final_norm_w, tgt):
    s = x.shape[0]
    o = [0]
    for wdt in IN_SPLITS:
        o.append(o[-1] + wdt)
    w_ba = jnp.pad(w_in_bf[:, o[2]:o[4]], ((0, 0), (0, BA_PAD - 2 * DN_HEADS)))
    ws = [w_in_bf[:, o[0]:o[1]], w_in_bf[:, o[1]:o[2]], w_ba, w_in_bf[:, o[4]:o[5]], w_in_bf[:, o[5]:o[6]],
          w_in_bf[:, o[6]:o[7]], w_in_bf[:, o[7]:o[8]]]
    conv_w8 = jnp.pad(conv_w, ((0, 8 - CONV_K), (0, 0)))
    alog_row = jnp.pad(a_log, ((0, 0), (DN_HEADS, BA_PAD - 2 * DN_HEADS)))
    dtb_row = jnp.pad(dt_bias, ((0, 0), (DN_HEADS, BA_PAD - 2 * DN_HEADS)))
    atw2 = jnp.concatenate([at_norm_w, at_norm_w], axis=1)

    half = AT_DIM // 2
    lane = jnp.arange(LANES)
    inv_freq = ROPE_THETA ** (-(lane % half).astype(F32) / half)
    ang = positions.astype(F32)[:, None] * inv_freq
    cos_t = jnp.cos(ang)
    sin_t = jnp.sin(ang) * jnp.where((lane // half) % 2 == 0, -1.0, 1.0)

    mod, siluc = _adaln_mod(c, w_mod_bf, b_mod)
    gate = mod[:, 2 * D_MODEL:]
    hbf, qkv_pre, z_dn, ba, qr, kr, vb, z_at, q, k, v, bg = _ln_proj(
        x, mod, norm_w, ws, cos_t, sin_t, conv_w8, alog_row, dtb_row, _tile(s, 256))
    u, w, qd, kd, p, gl, tinv = _dn_chunk_prep(q, k, v, bg, _tile(s, 512))
    o_dn, vn, st = _dn_scan(u, w, qd, kd, p, gl, _tile(s, 512))
    o_at, lse = _attn_fwd(qr, kr, vb)
    (dx2, gw_out, dfw, dgate, loss, do_dn, dz_dn, do_at, dz_at, delta, ddnw, datw) = _out_loss(
        o_dn, z_dn, o_at, z_at, dn_norm_w, atw2, x, tgt, w_out_bf, gate, final_norm_w, _tile(s, 512))

    daq, dak, dav, g_aq, g_ak, g_av, g_az = _grad_w_in_at(hbf, *_attn_bwd(qr, kr, vb, do_at, lse, delta), dz_at,
                                                           cos_t, sin_t, _tile(s, 512))
    du, dw, dqd, dkd, dp, dgl = _dn_scan_bwd(do_dn, st, vn, w, qd, kd, p, gl, _tile(s, 512))
    dq, dk, dv, dbg = _dn_chunk_bwd(q, k, v, bg, tinv, du, dw, dqd, dkd, dp, dgl, _tile(s, 512))
    dqkv, dba, dcw, dal, ddtb, g_qkv, g_z, g_ba = _dn_prep_bwd(qkv_pre, ba, dq, dk, dv, dbg, conv_w8, alog_row, dtb_row,
                                                               hbf, dz_dn, _tile(s, 512))
    dps = [dqkv, dz_dn, dba, daq, dak, dav, dz_at]
    gw_in = jnp.concatenate([g_qkv, g_z, g_ba[:, :2 * DN_HEADS], g_aq, g_ak, g_av, g_az], axis=1)
    small = dict(conv=dcw[:CONV_K], dgate=dgate, siluc=siluc, dfw=dfw, alog=dal, dtb=ddtb, dnn=ddnw, atn=datw)

    def input_grad(token):
        gx, dshift, dscale, dnw = _dh_dx(dps, ws, x, mod + token, norm_w, dx2, _tile(s, 512))
        return gx, jnp.concatenate([dshift, dscale, small["dgate"]], axis=1), dnw

    return loss, gw_in, gw_out, small, input_grad


def kernel(x, c, positions, w_mod, b_mod, norm_w, w_in, conv_w, a_log, dt_bias, dn_norm_w, at_norm_w, w_out, final_norm_w, loss_target, m_w_mod, m_b_mod, m_norm_w, m_w_in, m_conv_w, m_a_log, m_dt_bias, m_dn_norm_w, m_at_norm_w, m_w_out, m_final_norm_w, v_w_mod, v_b_mod, v_norm_w, v_w_in, v_conv_w, v_a_log, v_dt_bias, v_dn_norm_w, v_at_norm_w, v_w_out, v_final_norm_w):
    me = 4 * lax.axis_index("x") + 2 * lax.axis_index("y") + lax.axis_index("c")
    s = x.shape[1]

    g_mod, g_in, g_conv, g_out = _all_gather(
        [_bf(w_mod[0]), _bf(w_in[0]), conv_w[0], _bf(w_out[0])], "gather_weights")
    w_mod_bf = g_mod.transpose(1, 0, 2).reshape(D_MODEL, 3 * D_MODEL)
    w_in_bf = g_in.transpose(1, 0, 2).reshape(D_MODEL, IN_COLS)
    conv_full = g_conv.transpose(1, 0, 2).reshape(CONV_K, 3 * DN_WIDTH)
    w_out_bf = g_out.reshape(D_MODEL, D_MODEL)

    loss, gw_in, gw_out, small, input_grad = _local_step(
        x[0], c, positions[0], w_mod_bf, b_mod, norm_w, w_in_bf, conv_full, a_log, dt_bias, dn_norm_w, at_norm_w,
        w_out_bf, final_norm_w.reshape(1, D_MODEL), loss_target[0])

    gw_in_slabs = _bf(gw_in).reshape(D_MODEL, N_DEV, IN_SHARD).transpose(1, 0, 2)
    gw_out_slabs = _bf(gw_out).reshape(N_DEV, D_MODEL // N_DEV, D_MODEL)
    send_sems, recv_sems, srcs, lands, token = _scatter_start([gw_in_slabs, gw_out_slabs])
    gx, dmod, dnw = input_grad(token[0, 0])
    r_in, r_out = _scatter_wait(send_sems, recv_sems, srcs, lands, gx)
    own_in = lax.dynamic_index_in_dim(gw_in_slabs, me, 0, keepdims=False)
    own_out = lax.dynamic_index_in_dim(gw_out_slabs, me, 0, keepdims=False)

    pack = jnp.concatenate([small["conv"].reshape(1, -1), dmod, small["siluc"], dnw, small["dfw"],
                            small["alog"], small["dtb"], small["dnn"], small["atn"],
                            jnp.pad(loss, ((0, 0), (0, LANES - 1)))], axis=1).reshape(PK_ROWS, LANES)
    (pack_all,) = _exchange([pack], [False], "exchange_small")

    res = {}
    res["w_in"] = _adamw(w_in[0], m_w_in[0], v_w_in[0], r_in, "adamw_w_in", own=own_in)
    res["w_out"] = _adamw(w_out[0], m_w_out[0], v_w_out[0], r_out, "adamw_w_out", own=own_out)
    flat_all = pack_all.reshape(N_DEV, PK_END)
    dmod_mine = lax.dynamic_slice(flat_all, (0, PK_DMOD + me * (3 * D_MODEL // N_DEV)), (N_DEV, 3 * D_MODEL // N_DEV))
    res["w_mod"] = _adamw_w_mod(w_mod[0], m_w_mod[0], v_w_mod[0], flat_all[:, PK_SILUC:PK_DNW], dmod_mine)
    tot = _pack_sum(pack_all).reshape(1, PK_END)
    g_conv_full = tot[:, PK_CONV:PK_DMOD].reshape(CONV_K, 3 * DN_WIDTH)
    g_conv_mine = lax.dynamic_slice(g_conv_full, (0, me * (3 * DN_WIDTH // N_DEV)), (CONV_K, 3 * DN_WIDTH // N_DEV))
    res["conv_w"] = _adamw(conv_w[0], m_conv_w[0], v_conv_w[0], g_conv_mine, "adamw_conv_w")
    res["b_mod"] = _adamw(b_mod, m_b_mod, v_b_mod, tot[:, PK_DMOD:PK_SILUC], "adamw_b_mod")
    res["norm_w"] = _adamw(norm_w, m_norm_w, v_norm_w, tot[:, PK_DNW:PK_DFW], "adamw_norm_w")
    res["a_log"] = _adamw(a_log, m_a_log, v_a_log, tot[:, PK_ALOG + DN_HEADS:PK_ALOG + 2 * DN_HEADS], "adamw_a_log")
    res["dt_bias"] = _adamw(dt_bias, m_dt_bias, v_dt_bias, tot[:, PK_DTB + DN_HEADS:PK_DTB + 2 * DN_HEADS],
                            "adamw_dt_bias")
    res["dn_norm_w"] = _adamw(dn_norm_w, m_dn_norm_w, v_dn_norm_w, tot[:, PK_DNN:PK_ATN], "adamw_dn_norm_w")
    g_atn = tot[:, PK_ATN:PK_ATN + AT_DIM] + tot[:, PK_ATN + AT_DIM:PK_LOSS]
    res["at_norm_w"] = _adamw(at_norm_w, m_at_norm_w, v_at_norm_w, g_atn, "adamw_at_norm_w")
    fin = _adamw(final_norm_w.reshape(1, D_MODEL), m_final_norm_w.reshape(1, D_MODEL),
                 v_final_norm_w.reshape(1, D_MODEL), tot[:, PK_DFW:PK_ALOG], "adamw_final_norm_w")
    res["final_norm_w"] = tuple(a.reshape(D_MODEL) for a in fin)

    lead = ("w_mod", "w_in", "conv_w", "w_out")
    names = ("w_mod", "b_mod", "norm_w", "w_in", "conv_w", "a_log", "dt_bias", "dn_norm_w", "at_norm_w", "w_out",
             "final_norm_w")
    out = [tot[0, PK_LOSS], gx.reshape(1, s, D_MODEL)]
    for kind in range(4):
        for nm in names:
            a = res[nm][kind]
            out.append(a[None] if nm in lead else a)
    return tuple(out)
```

```python
import functools

import jax
import jax.numpy as jnp
from jax import lax
from jax.experimental import pallas as pl
from jax.experimental.pallas import tpu as pltpu

F32, BF16 = jnp.float32, jnp.bfloat16
HI = lax.Precision.HIGHEST
SDS = jax.ShapeDtypeStruct

D_MODEL = 1024
DN_HEADS, DN_DIM, DN_WIDTH = 4, 128, 512
AT_HEADS, AT_DIM, AT_WIDTH = 8, 64, 512
CONV_K = 4
CHUNK = 64
Q_BLOCK = 128
W_SUB = 128
DILATIONS = (1, 4, 16)
AT_PAIRS = 4
ATT_BLK = Q_BLOCK * max(DILATIONS)
ATT_UNROLL, ATT_UNROLL_BWD = 8, 4
CH_UNROLL = 4
ROPE_THETA = 10000.0
EPS = 1e-6
N_DEV = 8
LANES = 128
BA_PAD = 128
IN_SPLITS = (1536, 512, 4, 4, 512, 512, 512, 512)
IN_COLS = sum(IN_SPLITS)
IN_SHARD = IN_COLS // N_DEV
VMEM_LIMIT = 58 * 2 ** 20

ADAM_LR, ADAM_B1, ADAM_B2, ADAM_EPS, ADAM_WD, ADAM_STEP = 0.001, 0.9, 0.999, 1e-08, 0.01, 10

PK_CONV, PK_DMOD, PK_SILUC, PK_DNW, PK_DFW, PK_ALOG, PK_DTB, PK_DNN, PK_ATN, PK_LOSS, PK_END = (
    0, 6144, 9216, 10240, 11264, 12288, 12416, 12544, 12672, 12800, 12928)
PK_ROWS = PK_END // LANES

_NT = (((1,), (1,)), ((), ()))
_TN = (((0,), (0,)), ((), ()))


def _params(*sem):
    return pltpu.CompilerParams(dimension_semantics=sem or None, vmem_limit_bytes=VMEM_LIMIT)


def _bf(x):
    return x.astype(BF16)


def _nn(a, b):
    return jnp.dot(_bf(a), _bf(b), preferred_element_type=F32)


def _nt(a, b):
    return lax.dot_general(_bf(a), _bf(b), _NT, preferred_element_type=F32)


def _tn(a, b):
    return lax.dot_general(_bf(a), _bf(b), _TN, preferred_element_type=F32)


def _htn(a, b):
    return lax.dot_general(a, b, _TN, precision=HI, preferred_element_type=F32)


def _head_sum(x):
    r = lax.broadcasted_iota(jnp.int32, (LANES, LANES), 0)
    c = lax.broadcasted_iota(jnp.int32, (LANES, LANES), 1)
    same = jnp.where((r // AT_DIM) == (c // AT_DIM), 1.0, 0.0).astype(BF16)
    hi, lo = _hl(x)
    return jnp.dot(hi, same, preferred_element_type=F32) + jnp.dot(lo, same, preferred_element_type=F32)


@jax.custom_vjp
def _d_head_sum(x):
    return _head_sum(x)


_d_head_sum.defvjp(lambda x: (_head_sum(x), None), lambda _, g: (_head_sum(g),))


def _silu(x):
    return x * jax.nn.sigmoid(x)


def _softplus(x):
    return jnp.maximum(x, 0.0) + jnp.log(1.0 + jnp.exp(-jnp.abs(x)))


def _l2n(x):
    return x * lax.rsqrt(jnp.sum(x * x, axis=-1, keepdims=True) + EPS)


def _post_q(x):
    return _l2n(_silu(x)) * (DN_DIM ** -0.5)


def _post_k(x):
    return _l2n(_silu(x))


def _post_v(x):
    return _silu(x)


def _beta_decay(ba, alog_row, dtb_row):
    lane = lax.broadcasted_iota(jnp.int32, ba.shape, 1)
    return jnp.where(lane < DN_HEADS, jax.nn.sigmoid(ba), -jnp.exp(alog_row) * _softplus(ba + dtb_row))


def _gate_dn(o, z, w):
    return (o * lax.rsqrt(jnp.mean(o * o, axis=-1, keepdims=True) + EPS)) * w * _silu(z)


def _gate_at(o, z, w2, head_sum):
    ms = head_sum(o * o) * (1.0 / AT_DIM)
    return (o * lax.rsqrt(ms + EPS)) * w2 * _silu(z)


def _swap_half64(x):
    lane = lax.broadcasted_iota(jnp.int32, x.shape, 1)
    return jnp.where((lane & (AT_DIM - 1)) < AT_DIM // 2, pltpu.roll(x, LANES - AT_DIM // 2, 1),
                     pltpu.roll(x, AT_DIM // 2, 1))


_NN = (((1,), (0,)), ((), ()))


def _hl(a):
    hi = a.astype(BF16)
    return hi, (a - hi.astype(F32)).astype(BF16)


def _mm3(a, b, dims=_NN):
    (ah, al), (bh, bl) = a, b
    f = lambda x, y: lax.dot_general(x, y, dims, preferred_element_type=F32)
    return f(ah, bh) + (f(ah, bl) + f(al, bh))


def _chunk_masks():
    r = lax.broadcasted_iota(jnp.int32, (CHUNK, CHUNK), 0)
    c = lax.broadcasted_iota(jnp.int32, (CHUNK, CHUNK), 1)
    return r >= c, r > c, (r == c).astype(F32), (r // 16) == (c // 16)


def _tri_inv(mats, tick=lambda: None):
    _, _, eye, blk = _chunk_masks()
    dg = [jnp.where(blk, a, 0.0) for a in mats]
    lo = [jnp.where(blk, 0.0, a) for a in mats]
    sdg = [_hl(x) for x in dg]
    d2 = [_mm3(s, s) for s in sdg]
    tick()
    sd2 = [_hl(x) for x in d2]
    d4 = [_mm3(s, s) for s in sd2]
    tick()
    sd4 = [_hl(x) for x in d4]
    d8 = [_mm3(s, s) for s in sd4]
    tick()
    p1 = [_mm3(_hl(eye - a), _hl(eye + b)) for a, b in zip(dg, d2)]
    tick()
    p2 = [_mm3(_hl(a), _hl(eye + b)) for a, b in zip(p1, d4)]
    tick()
    dinv = [_mm3(_hl(a), _hl(eye + b)) for a, b in zip(p2, d8)]
    tick()
    sdinv = [_hl(x) for x in dinv]
    n1 = [_mm3(s, _hl(b)) for s, b in zip(sdinv, lo)]
    tick()
    sn1 = [_hl(x) for x in n1]
    n2 = [_mm3(s, s) for s in sn1]
    tick()
    q1 = [_mm3(_hl(eye - a), _hl(eye + b)) for a, b in zip(n1, n2)]
    return [_mm3(_hl(a), s) for a, s in zip(q1, sdinv)]


def _chunk_common(qs, ks, vs, betas, gcs):
    tril, _, _, _ = _chunk_masks()
    out = []
    for q, k, v, beta, gc in zip(qs, ks, vs, betas, gcs):
        gb = jnp.broadcast_to(gc, (CHUNK, DN_DIM))
        gt = gb.T[:CHUNK, :]
        gam = jnp.where(tril, jnp.exp(jnp.where(tril, gb[:, :CHUNK] - gt, 0.0)), 0.0)
        last = gb[CHUNK - 1:CHUNK, :]
        eg, e2 = jnp.exp(gb), jnp.exp(last - gb)
        kb, vb = k * beta, v * beta
        out.append(dict(gam=gam, eg=eg, e2=e2, gl=jnp.exp(last[:, 0:1]), kb=kb, vb=vb, kbg=kb * eg,
                        m=_nt(kb, k), qk=_nt(q, k)))
    return out


def _chunk_fwd(qs, ks, vs, betas, gcs, tick=lambda: None):
    tril, strict, _, _ = _chunk_masks()
    cm = _chunk_common(qs, ks, vs, betas, gcs)
    ts = _tri_inv([jnp.where(strict, c["m"] * c["gam"], 0.0) for c in cm], tick)
    outs = []
    for q, k, c, t in zip(qs, ks, cm, ts):
        uw = _nn(t, jnp.concatenate([c["vb"], c["kbg"]], axis=1))
        p = jnp.where(tril, c["qk"] * c["gam"], 0.0)
        outs.append((uw[:, :DN_DIM], uw[:, DN_DIM:], p, q * c["eg"], k * c["e2"], c["gl"], t.T))
    return outs


def _chunk_bwd(qs, ks, vs, betas, gcs, ts, cots, tick=lambda: None):
    tril, strict, _, _ = _chunk_masks()
    cm = _chunk_common(qs, ks, vs, betas, gcs)
    tick()
    row = lax.broadcasted_iota(jnp.int32, (CHUNK, 1), 0)
    ones = jnp.ones((CHUNK, DN_DIM), BF16)
    rs = lambda x: jnp.sum(x, axis=-1, keepdims=True)
    tts = [_bf(t) for t in ts]
    duw = [_bf(jnp.concatenate([ct[0], ct[1]], axis=1)) for ct in cots]
    dts = [_nt(a, jnp.concatenate([c["vb"], c["kbg"]], axis=1)) for a, c in zip(duw, cm)]
    tick()
    xs = [_nn(t, d) for t, d in zip(tts, dts)]
    tick()
    das = [jnp.where(strict, -_nn(x, t), 0.0) for x, t in zip(xs, tts)]
    dvks = [_nn(t, a) for t, a in zip(tts, duw)]
    tick()
    outs = []
    every = max(1, len(qs) // 5)
    for idx, (q, k, v, beta, c, ct, da, dvk) in enumerate(zip(qs, ks, vs, betas, cm, cots, das, dvks)):
        if idx and idx % every == 0:
            tick()
        _, _, dp, dqd, dkd, dgl = ct
        dvb, dkbg = dvk[:, :DN_DIM], dvk[:, DN_DIM:]
        dm = da * c["gam"]
        dqk = jnp.where(tril, dp, 0.0) * c["gam"]
        e = dm * c["m"] + dqk * c["qk"]
        dmq = jnp.concatenate([dm, dqk], axis=0)
        r1 = _nn(dmq, k)
        dkb = r1[:CHUNK] + dkbg * c["eg"]
        dq = r1[CHUNK:] + dqd * c["eg"]
        dk = _tn(dmq, jnp.concatenate([c["kb"], q], axis=0)) + dkd * c["e2"] + dkb * beta
        dbeta = rs(dkb * k + dvb * v)
        eh, el = _hl(e)
        colsum = (lax.dot_general(eh, ones, _TN, preferred_element_type=F32)
                  + lax.dot_general(el, ones, _TN, preferred_element_type=F32))[:, 0:1]
        pkd = dkd * (k * c["e2"])
        dgc = rs(e) - colsum + rs(dqd * q * c["eg"] + dkbg * c["kbg"] - pkd)
        tail = rs(jnp.sum(pkd, axis=0, keepdims=True)) + dgl * c["gl"]
        dgc = dgc + jnp.where(row == CHUNK - 1, tail, 0.0)
        outs.append((dq, dk, dvb * beta, dbeta, dgc))
    return outs


def _chunk_cumsum(x, reverse=False):
    n = x.shape[0]
    pos = lax.broadcasted_iota(jnp.int32, x.shape, 0) & (CHUNK - 1)
    sh = 1
    while sh < CHUNK:
        if reverse:
            x = x + jnp.where(pos < CHUNK - sh, pltpu.roll(x, n - sh, 0), 0.0)
        else:
            x = x + jnp.where(pos >= sh, pltpu.roll(x, sh, 0), 0.0)
        sh *= 2
    return x


GC_LANE = 2 * DN_HEADS


def _exchange(arrays, scatter, name):
    n = len(arrays)
    out_shapes = []
    for a, sc in zip(arrays, scatter):
        out_shapes.append(SDS(a.shape if sc else (N_DEV,) + a.shape, a.dtype))

    def body(*refs):
        ins, outs = refs[:n], refs[n:2 * n]
        send_sems, recv_sems, loc_sems = refs[2 * n:]
        x, y, c = lax.axis_index("x"), lax.axis_index("y"), lax.axis_index("c")
        me = 4 * x + 2 * y + c
        local, remote = [], []
        for i in range(n):
            src = ins[i].at[me] if scatter[i] else ins[i]
            cp = pltpu.make_async_copy(src, outs[i].at[me], loc_sems.at[i])
            cp.start()
            local.append(cp)
        for dlt in range(1, N_DEV):
            px = 1 - x if dlt & 4 else x
            py = 1 - y if dlt & 2 else y
            pc = 1 - c if dlt & 1 else c
            peer = 4 * px + 2 * py + pc
            for i in range(n):
                src = ins[i].at[peer] if scatter[i] else ins[i]
                cp = pltpu.make_async_remote_copy(
                    src_ref=src, dst_ref=outs[i].at[me],
                    send_sem=send_sems.at[i, dlt - 1], recv_sem=recv_sems.at[i, dlt - 1],
                    device_id=(px, py, pc), device_id_type=pl.DeviceIdType.MESH)
                cp.start()
                arrive = pltpu.make_async_remote_copy(
                    src_ref=src, dst_ref=outs[i].at[peer],
                    send_sem=send_sems.at[i, dlt - 1], recv_sem=recv_sems.at[i, dlt - 1],
                    device_id=(px, py, pc), device_id_type=pl.DeviceIdType.MESH)
                remote.append((cp, arrive))
        for cp, arrive in remote:
            cp.wait_send()
            arrive.wait_recv()
        for cp in local:
            cp.wait()

    any_spec = pl.BlockSpec(memory_space=pl.ANY)
    return pl.pallas_call(
        body, name=name, out_shape=tuple(out_shapes),
        in_specs=[any_spec] * n, out_specs=tuple([any_spec] * n),
        scratch_shapes=[pltpu.SemaphoreType.DMA((n, N_DEV - 1)), pltpu.SemaphoreType.DMA((n, N_DEV - 1)),
                        pltpu.SemaphoreType.DMA((n,))],
    )(*arrays)


def _all_gather(arrays, name):
    n = len(arrays)

    def body(*refs):
        ins, outs = refs[:n], refs[n:2 * n]
        send_sems, recv_sems, loc_sems = refs[2 * n:]
        x, y, c = lax.axis_index("x"), lax.axis_index("y"), lax.axis_index("c")
        me, sibling = (x, y, c), (x, y, 1 - c)
        chips = [(1 - x, y), (x, 1 - y), (1 - x, 1 - y)]

        def copy(i, k, block, to, src=None):
            slot = outs[i].at[4 * block[0] + 2 * block[1] + block[2]]
            return pltpu.make_async_remote_copy(
                src_ref=slot if src is None else src, dst_ref=slot,
                send_sem=send_sems.at[i, k], recv_sem=recv_sems.at[i, k],
                device_id=to, device_id_type=pl.DeviceIdType.MESH)

        mine = [pltpu.make_async_copy(ins[i], outs[i].at[4 * x + 2 * y + c], loc_sems.at[i]) for i in range(n)]
        for cp in mine:
            cp.start()
        first = []
        for i in range(n):
            first.append(copy(i, 0, me, sibling, src=ins[i]))
            first += [copy(i, 1 + j, me, (*chip, c), src=ins[i]) for j, chip in enumerate(chips)]
        for cp in first:
            cp.start()
        passed = []
        for j, chip in enumerate(chips):
            for i in range(n):
                copy(i, 1 + j, (*chip, c), me).wait_recv()
                fwd = copy(i, 4 + j, (*chip, c), sibling)
                fwd.start()
                passed.append(fwd)
        for i in range(n):
            copy(i, 0, sibling, me).wait_recv()
        for j, chip in enumerate(chips):
            for i in range(n):
                copy(i, 4 + j, (*chip, 1 - c), me).wait_recv()
        for cp in first + passed:
            cp.wait_send()
        for cp in mine:
            cp.wait()

    any_spec = pl.BlockSpec(memory_space=pl.ANY)
    return pl.pallas_call(
        body, name=name, out_shape=tuple(SDS((N_DEV,) + a.shape, a.dtype) for a in arrays),
        in_specs=[any_spec] * n, out_specs=tuple([any_spec] * n),
        scratch_shapes=[pltpu.SemaphoreType.DMA((n, N_DEV - 1)), pltpu.SemaphoreType.DMA((n, N_DEV - 1)),
                        pltpu.SemaphoreType.DMA((n,))],
    )(*arrays)


_HBM = pl.BlockSpec(memory_space=pltpu.HBM)
_SEM = pl.BlockSpec(memory_space=pltpu.SEMAPHORE)


def _peers(x, y, c):
    out = []
    for dlt in range(1, N_DEV):
        px = 1 - x if dlt & 4 else x
        py = 1 - y if dlt & 2 else y
        pc = 1 - c if dlt & 1 else c
        out.append((dlt, (px, py, pc), 4 * px + 2 * py + pc))
    return out


def _scatter_start(arrays):
    n = len(arrays)
    ns = n * (N_DEV - 1)

    def body(*refs):
        ins, lands = refs[:n], refs[n:2 * n]
        send_sems, recv_sems = refs[2 * n:2 * n + ns], refs[2 * n + ns:2 * n + 2 * ns]
        token = refs[-1]
        x, y, c = lax.axis_index("x"), lax.axis_index("y"), lax.axis_index("c")
        me = 4 * x + 2 * y + c
        for dlt, peer, pi in _peers(x, y, c):
            for i in range(n):
                k = i * (N_DEV - 1) + dlt - 1
                pltpu.make_async_remote_copy(
                    src_ref=ins[i].at[pi], dst_ref=lands[i].at[me], send_sem=send_sems[k], recv_sem=recv_sems[k],
                    device_id=peer, device_id_type=pl.DeviceIdType.MESH).start()
        token[...] = jnp.zeros_like(token)

    sem = pltpu.SemaphoreType.DMA(())
    thru = tuple(pltpu.HBM(a.shape, a.dtype) for a in arrays)
    hbm = lambda a: pltpu.with_memory_space_constraint(a, pltpu.HBM)
    outs = pl.pallas_call(
        body, name="scatter_start", out_shape=(sem,) * (2 * ns) + thru + thru + (SDS((8, LANES), F32),),
        in_specs=[_HBM] * (2 * n),
        out_specs=(_SEM,) * (2 * ns) + (_HBM,) * (2 * n) + (pl.BlockSpec(memory_space=pltpu.VMEM),),
        input_output_aliases={i: 2 * ns + i for i in range(2 * n)},
        compiler_params=pltpu.CompilerParams(has_side_effects=pltpu.SideEffectType.DATAFLOW_SIDE_EFFECTING),
    )(*[hbm(a) for a in arrays], *[hbm(jnp.zeros(a.shape, a.dtype)) for a in arrays])
    return outs[:ns], outs[ns:2 * ns], outs[2 * ns:2 * ns + n], outs[2 * ns + n:2 * ns + 2 * n], outs[-1]


def _scatter_wait(send_sems, recv_sems, srcs, lands, after):
    n = len(srcs)
    ns = n * (N_DEV - 1)

    def body(*refs):
        ins, lands_ = refs[:n], refs[n:2 * n]
        send, recv = refs[2 * n:2 * n + ns], refs[2 * n + ns:2 * n + 2 * ns]
        x, y, c = lax.axis_index("x"), lax.axis_index("y"), lax.axis_index("c")
        for dlt, peer, pi in _peers(x, y, c):
            for i in range(n):
                k = i * (N_DEV - 1) + dlt - 1
                cp = pltpu.make_async_remote_copy(
                    src_ref=ins[i].at[pi], dst_ref=lands_[i].at[pi], send_sem=send[k], recv_sem=recv[k],
                    device_id=peer, device_id_type=pl.DeviceIdType.MESH)
                cp.wait_send()
                cp.wait_recv()

    thru = tuple(pltpu.HBM(a.shape, a.dtype) for a in srcs)
    outs = pl.pallas_call(
        body, name="scatter_wait", out_shape=thru + thru,
        in_specs=[_HBM] * (2 * n) + [_SEM] * (2 * ns) + [pl.BlockSpec(memory_space=pl.ANY)],
        out_specs=(_HBM,) * (2 * n), input_output_aliases={i: i for i in range(2 * n)},
        compiler_params=pltpu.CompilerParams(has_side_effects=pltpu.SideEffectType.DATAFLOW_SIDE_EFFECTING),
    )(*srcs, *lands, *send_sems, *recv_sems, after)
    return outs[n:]


def _adaln_mod(c, w_mod, b_mod):
    def body(c_ref, w_ref, b_ref, mod_ref, sc_ref):
        sc = _silu(c_ref[...])
        sc8 = jnp.broadcast_to(sc, (8, D_MODEL))
        mod_ref[...] = _nn(sc8, w_ref[...])[0:1] + b_ref[...]
        sc_ref[...] = sc

    return pl.pallas_call(body, name="adaln_mod", compiler_params=_params(),
                          out_shape=(SDS((1, 3 * D_MODEL), F32), SDS((1, D_MODEL), F32)))(c, w_mod, b_mod)


def _ln_proj(x, mod, norm_w, ws, cos_t, sin_t, conv_w8, alog_row, dtb_row, ts):
    s = x.shape[0]
    widths = [w.shape[1] for w in ws]

    def body(x_ref, mod_ref, nw_ref, cos_ref, sin_ref, cw_ref, al_ref, dtb_ref, wqkv, wz, wba, waq, wak, wav, waz,
             h_ref, oqkv, oz, oba, oq, ok, ov, oaz, q_ref, k_ref, v_ref, bg_ref, halo):
        n = pl.program_id(0)
        xt = x_ref[...]
        r = lax.rsqrt(jnp.mean(xt * xt, axis=-1, keepdims=True) + EPS)
        shift, scale = mod_ref[:, 0:D_MODEL], mod_ref[:, D_MODEL:2 * D_MODEL]
        h = ((xt * r) * nw_ref[...]) * (1.0 + scale) + shift
        hb = _bf(h)
        h_ref[...] = hb
        pre = jnp.dot(hb, wqkv[...], preferred_element_type=F32)
        oqkv[...] = pre
        ext = jnp.concatenate([jnp.where(n == 0, 0.0, halo[...]), pre], axis=0)
        halo[...] = pre[ts - 8:ts]
        taps = _conv_taps(ext, ts)
        conv = taps[0] * cw_ref[0:1, :]
        for j in range(1, CONV_K):
            conv = conv + taps[j] * cw_ref[j:j + 1, :]
        for hd in range(DN_HEADS):
            cols = slice(hd * DN_DIM, (hd + 1) * DN_DIM)
            q_ref[:, cols] = _post_q(conv[:, hd * DN_DIM:(hd + 1) * DN_DIM])
            k_ref[:, cols] = _post_k(conv[:, DN_WIDTH + hd * DN_DIM:DN_WIDTH + (hd + 1) * DN_DIM])
            v_ref[:, cols] = _post_v(conv[:, 2 * DN_WIDTH + hd * DN_DIM:2 * DN_WIDTH + (hd + 1) * DN_DIM])
        ba = jnp.dot(hb, wba[...], preferred_element_type=F32)
        oba[...] = ba
        bg = _beta_decay(ba, al_ref[...], dtb_ref[...])
        lane = lax.broadcasted_iota(jnp.int32, bg.shape, 1)
        run = pltpu.roll(_chunk_cumsum(bg), DN_HEADS, 1)
        bg_ref[...] = jnp.where((lane >= GC_LANE) & (lane < GC_LANE + DN_HEADS), run, bg)
        oz[...] = jnp.dot(hb, wz[...], preferred_element_type=F32)
        oaz[...] = jnp.dot(hb, waz[...], preferred_element_type=F32)
        tv = jnp.dot(hb, wav[...], preferred_element_type=F32)
        for j in range(AT_PAIRS):
            ov[j] = tv[:, j * LANES:(j + 1) * LANES]
        cs, sn = cos_ref[...], sin_ref[...]
        for w_ref, o_ref in ((waq, oq), (wak, ok)):
            t = jnp.dot(hb, w_ref[...], preferred_element_type=F32)
            for j in range(AT_PAIRS):
                tj = t[:, j * LANES:(j + 1) * LANES]
                o_ref[j] = tj * cs + _swap_half64(tj) * sn

    tok = lambda w: pl.BlockSpec((ts, w), lambda i: (i, 0))
    full = lambda a: pl.BlockSpec(a.shape, lambda i: (0, 0))
    pairs = pl.BlockSpec((AT_PAIRS, ts, LANES), lambda i: (0, i, 0))
    return pl.pallas_call(
        body, name="ln_proj", grid=(s // ts,), compiler_params=_params("arbitrary"),
        in_specs=[tok(D_MODEL), full(mod), full(norm_w), tok(LANES), tok(LANES), full(conv_w8), full(alog_row),
                  full(dtb_row)] + [full(w) for w in ws],
        out_specs=(tok(D_MODEL), tok(widths[0]), tok(widths[1]), tok(widths[2]), pairs, pairs, pairs,
                   tok(widths[6]), tok(DN_WIDTH), tok(DN_WIDTH), tok(DN_WIDTH), tok(BA_PAD)),
        out_shape=(SDS((s, D_MODEL), BF16), SDS((s, widths[0]), F32), SDS((s, widths[1]), F32),
                   SDS((s, widths[2]), F32)) + (SDS((AT_PAIRS, s, LANES), F32),) * 3 + (SDS((s, widths[6]), F32),)
        + (SDS((s, DN_WIDTH), F32),) * 3 + (SDS((s, BA_PAD), F32),),
        scratch_shapes=[pltpu.VMEM((8, widths[0]), F32)],
    )(x, mod, norm_w, cos_t, sin_t, conv_w8, alog_row, dtb_row, *ws)


def _conv_taps(ext, rows):
    taps = []
    for j in range(CONV_K):
        sh = CONV_K - 1 - j
        rolled = pltpu.roll(ext, sh, 0) if sh else ext
        taps.append(rolled[8:8 + rows])
    return taps


def _dn_forward(q, k, v, bg):
    s = q.shape[0]
    tp = CH_UNROLL * CHUNK
    npass = s // tp
    hs = range(DN_HEADS)
    sl = [slice(h * DN_DIM, (h + 1) * DN_DIM) for h in hs]

    def body(q_ref, k_ref, v_ref, bg_ref, w_ref, qd_ref, kd_ref, p_ref, gl_ref, t_ref, o_ref, vn_ref, st_ref,
             state, u_s, w_s, qd_s, kd_s, p_s, gl_s):
        @pl.when(pl.program_id(0) == 0)
        def _():
            for ref in (state, u_s, w_s, qd_s, kd_s, p_s, gl_s):
                ref[...] = jnp.zeros_like(ref)

        def recurrence():
            for c in range(CH_UNROLL):
                rows = slice(c * CHUNK, (c + 1) * CHUNK)
                rows8 = slice(c * 8, (c + 1) * 8)
                srows = slice(c * DN_DIM, (c + 1) * DN_DIM)
                sf = [state[h] for h in hs]
                sb = [_bf(x) for x in sf]
                ws = [_nn(w_s[rows, cl], b) for cl, b in zip(sl, sb)]
                qs = [_nn(qd_s[rows, cl], b) for cl, b in zip(sl, sb)]
                yield
                vn = [u_s[rows, cl] - x for cl, x in zip(sl, ws)]
                vb = [_bf(x) for x in vn]
                kv = [_tn(kd_s[rows, cl], b) for cl, b in zip(sl, vb)]
                pv = [_nn(p_s[h, rows, :], b) for h, b in zip(hs, vb)]
                for h in hs:
                    state[h] = sf[h] * gl_s[rows8, sl[h]][0:1] + kv[h]
                for h in hs:
                    st_ref[srows, sl[h]] = sf[h]
                    vn_ref[rows, sl[h]] = vn[h]
                    o_ref[rows, sl[h]] = qs[h] + pv[h]
                yield

        steps = recurrence()

        where = [(slice(c * CHUNK, (c + 1) * CHUNK), slice(c * 8, (c + 1) * 8), h, sl[h])
                 for c in range(CH_UNROLL) for h in hs]
        bgs = [bg_ref[rows, :] for rows, _, _, _ in where]
        outs = _chunk_fwd([q_ref[rows, cl] for rows, _, _, cl in where], [k_ref[rows, cl] for rows, _, _, cl in where],
                          [v_ref[rows, cl] for rows, _, _, cl in where],
                          [b[:, h:h + 1] for b, (_, _, h, _) in zip(bgs, where)],
                          [b[:, GC_LANE + h:GC_LANE + h + 1] for b, (_, _, h, _) in zip(bgs, where)],
                          tick=lambda: next(steps, None))
        for _ in steps:
            pass
        for (rows, rows8, h, cl), (u, w, p, qd, kd, gl, t) in zip(where, outs):
            g8 = jnp.broadcast_to(gl, (8, DN_DIM))
            u_s[rows, cl] = u
            w_ref[rows, cl] = w
            w_s[rows, cl] = w
            qd_ref[rows, cl] = qd
            qd_s[rows, cl] = qd
            kd_ref[rows, cl] = kd
            kd_s[rows, cl] = kd
            p_ref[h, rows, :] = p
            p_s[h, rows, :] = p
            gl_ref[rows8, cl] = g8
            gl_s[rows8, cl] = g8
            t_ref[h, rows, :] = t

    cur = lambda i: jnp.minimum(i, npass - 1)
    done = lambda i: jnp.maximum(i - 1, 0)
    tokc = pl.BlockSpec((tp, DN_WIDTH), lambda i: (cur(i), 0))
    tokd = pl.BlockSpec((tp, DN_WIDTH), lambda i: (done(i), 0))
    sq = pl.BlockSpec((DN_HEADS, tp, CHUNK), lambda i: (0, cur(i), 0))
    return pl.pallas_call(
        body, name="dn_forward", grid=(npass + 1,), compiler_params=_params("arbitrary"),
        in_specs=[tokc] * 3 + [pl.BlockSpec((tp, BA_PAD), lambda i: (cur(i), 0))],
        out_specs=(tokc, tokc, tokc, sq, pl.BlockSpec((CH_UNROLL * 8, DN_WIDTH), lambda i: (cur(i), 0)), sq,
                   tokd, tokd, pl.BlockSpec((CH_UNROLL * DN_DIM, DN_WIDTH), lambda i: (done(i), 0))),
        out_shape=(SDS((s, DN_WIDTH), F32),) * 3 + (SDS((DN_HEADS, s, CHUNK), F32),
                                                     SDS((s // CHUNK * 8, DN_WIDTH), F32),
                                                     SDS((DN_HEADS, s, CHUNK), F32),
                                                     SDS((s, DN_WIDTH), F32), SDS((s, DN_WIDTH), F32),
                                                     SDS((s // CHUNK * DN_DIM, DN_WIDTH), F32)),
        scratch_shapes=[pltpu.VMEM((DN_HEADS, DN_DIM, DN_DIM), F32)] + [pltpu.VMEM((tp, DN_WIDTH), F32)] * 4
        + [pltpu.VMEM((DN_HEADS, tp, CHUNK), F32), pltpu.VMEM((CH_UNROLL * 8, DN_WIDTH), F32)],
    )(q, k, v, bg)


LOG2E, LN2 = 1.4426950408889634, 0.6931471805599453
MASKED = -1e30


def _band_bias():
    qi = lax.broadcasted_iota(jnp.int32, (Q_BLOCK, 2 * Q_BLOCK), 0)
    kj = lax.broadcasted_iota(jnp.int32, (Q_BLOCK, 2 * Q_BLOCK), 1)
    rel = Q_BLOCK + qi - kj
    return jnp.where((rel >= 0) & (rel <= W_SUB), 0.0, MASKED)


def _first_bias(first):
    kj = lax.broadcasted_iota(jnp.int32, (1, 2 * Q_BLOCK), 1)
    return jnp.where((kj < Q_BLOCK) & first, MASKED, 0.0)


def _attn_combo(c, d):
    if d == 1:
        qs = pl.multiple_of(c * Q_BLOCK, Q_BLOCK)
        return qs, pl.multiple_of(ATT_BLK - Q_BLOCK + c * Q_BLOCK, Q_BLOCK), c == 0
    r, m = c % d, c // d
    qs = r + (d * Q_BLOCK) * m
    return qs, ATT_BLK + qs - d * Q_BLOCK, m == 0


def _rows(start, size, d):
    return pl.ds(pl.multiple_of(start, Q_BLOCK), size) if d == 1 else pl.ds(start, size, stride=d)


def _shift_in(ext, cur, n):
    @pl.when(n == 0)
    def _():
        ext[0:ATT_BLK, :] = jnp.zeros((ATT_BLK, LANES), F32)

    @pl.when(n > 0)
    def _():
        ext[0:ATT_BLK, :] = ext[ATT_BLK:2 * ATT_BLK, :]

    ext[ATT_BLK:2 * ATT_BLK, :] = cur


def _attn_fwd(qr, kr, vv):
    s = qr.shape[1]
    nblk = s // ATT_BLK
    scale = AT_DIM ** -0.5
    npat = len(DILATIONS)

    def body(q_ref, k_ref, v_ref, o_ref, lse_ref, kext, vext, o_p, l_p, bias_ref):
        n = pl.program_id(1)
        _shift_in(kext, k_ref[0], n)
        _shift_in(vext, v_ref[0], n)
        bias_ref[...] = _band_bias()
        lo = lax.broadcasted_iota(jnp.int32, (Q_BLOCK, LANES), 1) < AT_DIM
        for pi, d in enumerate(DILATIONS):
            def group(g, carry, pi=pi, d=d):
                cs = [_attn_combo(g * ATT_UNROLL + u, d) for u in range(ATT_UNROLL)]
                heads = [(i, sel) for i in range(ATT_UNROLL) for sel in (lo, ~lo)]
                band = bias_ref[...]
                bias = [band + _first_bias((n == 0) & m0) for _, _, m0 in cs]
                qb = [_bf(q_ref[0, _rows(qs, Q_BLOCK, d), :]) for qs, _, _ in cs]
                kk = [_bf(kext[_rows(ks, 2 * Q_BLOCK, d), :]) for _, ks, _ in cs]
                vb = [_bf(vext[_rows(ks, 2 * Q_BLOCK, d), :]) for _, ks, _ in cs]
                sc = [lax.dot_general(jnp.where(sel, qb[i], jnp.zeros_like(qb[i])), kk[i], _NT,
                                      preferred_element_type=F32) for i, sel in heads]
                sc = [x * (scale * LOG2E) + bias[i] for x, (i, _) in zip(sc, heads)]
                mx = [jnp.max(x, axis=-1, keepdims=True) for x in sc]
                pr = [jnp.exp2(x - m) for x, m in zip(sc, mx)]
                ls = [jnp.sum(x, axis=-1, keepdims=True) for x in pr]
                pv = [jnp.dot(_bf(x), vb[i], preferred_element_type=F32) for x, (i, _) in zip(pr, heads)]
                outs = [x / l for x, l in zip(pv, ls)]
                lses = [m * LN2 + jnp.log(l) for m, l in zip(mx, ls)]
                for i, (qs, _, _) in enumerate(cs):
                    o_p[pi, _rows(qs, Q_BLOCK, d), :] = jnp.where(lo, outs[2 * i], outs[2 * i + 1])
                    l_p[pi, _rows(qs, Q_BLOCK, d), :] = jnp.where(lo, lses[2 * i], lses[2 * i + 1])
                return carry

            lax.fori_loop(0, ATT_BLK // Q_BLOCK // ATT_UNROLL, group, 0)

        def merge(i, carry):
            rows = pl.ds(pl.multiple_of(i * 256, 256), 256)
            ls = [l_p[pi, rows, :] for pi in range(npat)]
            mx = jnp.maximum(jnp.maximum(ls[0], ls[1]), ls[2])
            es = [jnp.exp(l - mx) for l in ls]
            den = es[0] + es[1] + es[2]
            o_ref[0, rows, :] = (es[0] * o_p[0, rows, :] + es[1] * o_p[1, rows, :] + es[2] * o_p[2, rows, :]) / den
            lse_ref[0, rows, :] = mx + jnp.log(den)
            return carry

        lax.fori_loop(0, ATT_BLK // 256, merge, 0)

    blk = pl.BlockSpec((1, ATT_BLK, LANES), lambda j, n: (j, n, 0))
    return pl.pallas_call(
        body, name="attn_fwd", grid=(AT_PAIRS, nblk), compiler_params=_params("arbitrary", "arbitrary"),
        in_specs=[blk] * 3, out_specs=(blk, blk),
        out_shape=(SDS((AT_PAIRS, s, LANES), F32),) * 2,
        scratch_shapes=[pltpu.VMEM((2 * ATT_BLK, LANES), F32), pltpu.VMEM((2 * ATT_BLK, LANES), F32),
                        pltpu.VMEM((npat, ATT_BLK, LANES), F32), pltpu.VMEM((npat, ATT_BLK, LANES), F32),
                        pltpu.VMEM((Q_BLOCK, 2 * Q_BLOCK), F32)],
    )(qr, kr, vv)


def _out_loss(o_dn, z_dn, o_at, z_at, dnw, atw2, x, tgt, w_out, gate, fw, ts):
    s = x.shape[0]

    def body(odn, zdn, oat, zat, dnw_ref, atw_ref, x_ref, t_ref, w_ref, g_ref, fw_ref,
             dx2_ref, gw_ref, dfw_ref, dgate_ref, loss_ref, dodn, dzdn, doat, dzat, delta, ddnw, datw):
        @pl.when(pl.program_id(0) == 0)
        def _():
            for ref in (gw_ref, dfw_ref, dgate_ref, loss_ref, ddnw, datw):
                ref[...] = jnp.zeros_like(ref)

        parts, vjps = [], []
        for h in range(DN_HEADS):
            cols = slice(h * DN_DIM, (h + 1) * DN_DIM)
            y, vjp = jax.vjp(_gate_dn, odn[:, cols], zdn[:, cols], dnw_ref[...])
            parts.append(_bf(y))
            vjps.append(vjp)
        for j in range(AT_PAIRS):
            y, vjp = jax.vjp(functools.partial(_gate_at, head_sum=_d_head_sum), oat[j],
                             zat[:, j * LANES:(j + 1) * LANES], atw_ref[...])
            parts.append(_bf(y))
            vjps.append(vjp)
        catb = jnp.concatenate(parts, axis=1)
        wb = w_ref[...]
        gate, fwv = g_ref[...], fw_ref[...]
        mix = jnp.dot(catb, wb, preferred_element_type=F32)
        x2 = x_ref[...] + gate * mix
        r2 = lax.rsqrt(jnp.mean(x2 * x2, axis=-1, keepdims=True) + EPS)
        xn2 = x2 * r2
        err = xn2 * fwv - t_ref[...]
        row = jnp.sum(err * err, axis=-1, keepdims=True) * (1.0 / D_MODEL)
        loss_ref[...] += 0.5 * jnp.sum(row, axis=0, keepdims=True)
        dy = err * (1.0 / D_MODEL)
        dfw_ref[...] += jnp.sum(dy * xn2, axis=0, keepdims=True)
        dxn = dy * fwv
        dx2 = r2 * (dxn - xn2 * jnp.mean(dxn * xn2, axis=-1, keepdims=True))
        dx2_ref[...] = dx2
        dgate_ref[...] += jnp.sum(dx2 * mix, axis=0, keepdims=True)
        dmix = _bf(gate * dx2)
        dcat = lax.dot_general(dmix, wb, _NT, preferred_element_type=F32)
        gw_ref[...] += lax.dot_general(catb, dmix, _TN, preferred_element_type=F32)
        for h in range(DN_HEADS):
            cols = slice(h * DN_DIM, (h + 1) * DN_DIM)
            do, dz, dw = vjps[h](dcat[:, cols])
            dodn[:, cols] = do
            dzdn[:, cols] = _bf(dz)
            ddnw[...] += dw
        for j in range(AT_PAIRS):
            cols = slice(j * LANES, (j + 1) * LANES)
            do, dz, dw = vjps[DN_HEADS + j](dcat[:, DN_WIDTH + j * LANES:DN_WIDTH + (j + 1) * LANES])
            doat[j] = do
            dzat[:, cols] = _bf(dz)
            datw[...] += dw
            delta[j] = _head_sum(do * oat[j])

    tok = lambda w: pl.BlockSpec((ts, w), lambda i: (i, 0))
    full = lambda a: pl.BlockSpec(a.shape, lambda i: (0, 0))
    row = pl.BlockSpec((1, D_MODEL), lambda i: (0, 0))
    lrow = pl.BlockSpec((1, LANES), lambda i: (0, 0))
    pairs = pl.BlockSpec((AT_PAIRS, ts, LANES), lambda i: (0, i, 0))
    return pl.pallas_call(
        body, name="out_loss", grid=(s // ts,), compiler_params=_params("arbitrary"),
        in_specs=[tok(DN_WIDTH), tok(DN_WIDTH), pairs, tok(AT_WIDTH), full(dnw), full(atw2),
                  tok(D_MODEL), tok(D_MODEL), full(w_out), full(gate), full(fw)],
        out_specs=(tok(D_MODEL), pl.BlockSpec((D_MODEL, D_MODEL), lambda i: (0, 0)), row, row,
                   pl.BlockSpec((1, 1), lambda i: (0, 0)), tok(DN_WIDTH), tok(DN_WIDTH), pairs, tok(AT_WIDTH), pairs,
                   lrow, lrow),
        out_shape=(SDS((s, D_MODEL), F32), SDS((D_MODEL, D_MODEL), F32), SDS((1, D_MODEL), F32),
                   SDS((1, D_MODEL), F32), SDS((1, 1), F32), SDS((s, DN_WIDTH), F32), SDS((s, DN_WIDTH), BF16),
                   SDS((AT_PAIRS, s, LANES), F32), SDS((s, AT_WIDTH), BF16), SDS((AT_PAIRS, s, LANES), F32),
                   SDS((1, LANES), F32), SDS((1, LANES), F32)),
    )(o_dn, z_dn, o_at, z_at, dnw, atw2, x, tgt, w_out, gate, fw)


def _shift_acc(ext, n):
    @pl.when(n == 0)
    def _():
        ext[0:ATT_BLK, :] = jnp.zeros((ATT_BLK, LANES), F32)

    @pl.when(n > 0)
    def _():
        ext[0:ATT_BLK, :] = ext[ATT_BLK:2 * ATT_BLK, :]

    ext[ATT_BLK:2 * ATT_BLK, :] = jnp.zeros((ATT_BLK, LANES), F32)


def _attn_bwd(qr, kr, vv, do, lse, delta):
    s = qr.shape[1]
    nblk = s // ATT_BLK
    scale = AT_DIM ** -0.5

    def body(q_ref, k_ref, v_ref, do_ref, lse_ref, dl_ref, dq_ref, dk_ref, dv_ref, kext, vext, dkext, dvext,
             bias_ref):
        n = pl.program_id(1)
        _shift_in(kext, k_ref[0], n)
        _shift_in(vext, v_ref[0], n)
        _shift_acc(dkext, n)
        _shift_acc(dvext, n)
        bias_ref[...] = _band_bias()

        @pl.when(n < nblk)
        def _():
            dq_ref[0] = jnp.zeros((ATT_BLK, LANES), F32)
            lo = lax.broadcasted_iota(jnp.int32, (Q_BLOCK, LANES), 1) < AT_DIM
            for d in DILATIONS:
                def group(g, carry, d=d):
                    nu = ATT_UNROLL_BWD
                    cs = [_attn_combo(g * nu + u, d) for u in range(nu)]
                    heads = [(i, sel) for i in range(nu) for sel in (lo, ~lo)]
                    qrows = [_rows(qs, Q_BLOCK, d) for qs, _, _ in cs]
                    krows = [_rows(ks, 2 * Q_BLOCK, d) for _, ks, _ in cs]
                    band = bias_ref[...]
                    bias = [band + _first_bias((n == 0) & m0) for _, _, m0 in cs]
                    qb = [_bf(q_ref[0, r, :]) for r in qrows]
                    dob = [_bf(do_ref[0, r, :]) for r in qrows]
                    kk = [_bf(kext[r, :]) for r in krows]
                    vb = [_bf(vext[r, :]) for r in krows]
                    lse2 = [lse_ref[0, r, :] * LOG2E for r in qrows]
                    dl2 = [dl_ref[0, r, :] for r in qrows]
                    qm = [jnp.where(sel, qb[i], jnp.zeros_like(qb[i])) for i, sel in heads]
                    dom = [jnp.where(sel, dob[i], jnp.zeros_like(dob[i])) for i, sel in heads]
                    lse_c = [jnp.max(jnp.where(sel, lse2[i], -jnp.inf), axis=-1, keepdims=True) for i, sel in heads]
                    dl_c = [jnp.max(jnp.where(sel, dl2[i], -jnp.inf), axis=-1, keepdims=True) for i, sel in heads]
                    sc = [lax.dot_general(a, kk[i], _NT, preferred_element_type=F32) for a, (i, _) in zip(qm, heads)]
                    dp = [lax.dot_general(a, vb[i], _NT, preferred_element_type=F32) for a, (i, _) in zip(dom, heads)]
                    pr = [jnp.exp2(x * (scale * LOG2E) + bias[i] - l) for x, l, (i, _) in zip(sc, lse_c, heads)]
                    ds = [_bf(p * (x - dl) * scale) for p, x, dl in zip(pr, dp, dl_c)]
                    prb = [_bf(p) for p in pr]
                    dq = [jnp.dot(x, kk[i], preferred_element_type=F32) for x, (i, _) in zip(ds, heads)]
                    dk = [lax.dot_general(x, a, _TN, preferred_element_type=F32) for x, a in zip(ds, qm)]
                    dv = [lax.dot_general(x, a, _TN, preferred_element_type=F32) for x, a in zip(prb, dom)]
                    for i in range(nu):
                        dq_ref[0, qrows[i], :] += jnp.where(lo, dq[2 * i], dq[2 * i + 1])
                        dkext[krows[i], :] += dk[2 * i] + dk[2 * i + 1]
                        dvext[krows[i], :] += dv[2 * i] + dv[2 * i + 1]
                    return carry

                lax.fori_loop(0, ATT_BLK // Q_BLOCK // ATT_UNROLL_BWD, group, 0)

        dk_ref[0] = dkext[0:ATT_BLK, :]
        dv_ref[0] = dvext[0:ATT_BLK, :]

    cur = pl.BlockSpec((1, ATT_BLK, LANES), lambda j, n: (j, jnp.minimum(n, nblk - 1), 0))
    done = pl.BlockSpec((1, ATT_BLK, LANES), lambda j, n: (j, jnp.maximum(n - 1, 0), 0))
    return pl.pallas_call(
        body, name="attn_bwd", grid=(AT_PAIRS, nblk + 1), compiler_params=_params("arbitrary", "arbitrary"),
        in_specs=[cur] * 6, out_specs=(cur, done, done),
        out_shape=(SDS((AT_PAIRS, s, LANES), F32),) * 3,
        scratch_shapes=[pltpu.VMEM((2 * ATT_BLK, LANES), F32)] * 4 + [pltpu.VMEM((Q_BLOCK, 2 * Q_BLOCK), F32)],
    )(qr, kr, vv, do, lse, delta)


def _dn_backward(do, st, vn, w, qd, kd, p, gl, q, k, v, bg, t):
    s = do.shape[0]
    tp = CH_UNROLL * CHUNK
    npass = s // tp
    hs = range(DN_HEADS)
    sl = [slice(h * DN_DIM, (h + 1) * DN_DIM) for h in hs]

    def body(do_ref, st_ref, vn_ref, w_ref, qd_ref, kd_ref, p_ref, gl_ref, q_ref, k_ref, v_ref, bg_ref, t_ref,
             dq_ref, dk_ref, dv_ref, dbg_ref, dstate, du_s, dw_s, dqd_s, dkd_s, dp_s, dgl_s):
        @pl.when(pl.program_id(0) == 0)
        def _():
            for ref in (dstate, du_s, dw_s, dqd_s, dkd_s, dp_s, dgl_s):
                ref[...] = jnp.zeros_like(ref)

        where = [(slice(c * CHUNK, (c + 1) * CHUNK), slice(c * 8, (c + 1) * 8), h, sl[h])
                 for c in range(CH_UNROLL) for h in hs]
        cots = [(du_s[rows, cl], dw_s[rows, cl], dp_s[h, rows, :], dqd_s[rows, cl], dkd_s[rows, cl],
                 dgl_s[rows8, cl][0:1, 0:1]) for rows, rows8, h, cl in where]

        def recurrence():
            for c in reversed(range(CH_UNROLL)):
                rows = slice(c * CHUNK, (c + 1) * CHUNK)
                rows8 = slice(c * 8, (c + 1) * 8)
                srows = slice(c * DN_DIM, (c + 1) * DN_DIM)
                ds_ = [dstate[h] for h in hs]
                dsb = [_bf(x) for x in ds_]
                dob = [_bf(do_ref[rows, cl]) for cl in sl]
                pdo = [_tn(p_ref[h, rows, :], b) for h, b in zip(hs, dob)]
                qdo = [_tn(qd_ref[rows, cl], b) for cl, b in zip(sl, dob)]
                kds = [_nn(kd_ref[rows, cl], b) for cl, b in zip(sl, dsb)]
                yield
                dvn = [a + b for a, b in zip(kds, pdo)]
                dvb = [_bf(x) for x in dvn]
                wdv = [_tn(w_ref[rows, cl], b) for cl, b in zip(sl, dvb)]
                for h in hs:
                    dstate[h] = ds_[h] * gl_ref[rows8, sl[h]][0:1] + qdo[h] - wdv[h]
                sfs = [st_ref[srows, cl] for cl in sl]
                sbs = [_bf(x) for x in sfs]
                vnb = [_bf(vn_ref[rows, cl]) for cl in sl]
                for h in hs:
                    du_s[rows, sl[h]] = dvn[h]
                    dw_s[rows, sl[h]] = -_nt(dvb[h], sbs[h])
                    dqd_s[rows, sl[h]] = _nt(dob[h], sbs[h])
                    dkd_s[rows, sl[h]] = _nt(vnb[h], dsb[h])
                    dp_s[h, rows, :] = _nt(dob[h], vnb[h])
                    dgl = jnp.sum(jnp.sum(ds_[h] * sfs[h], axis=1, keepdims=True), axis=0, keepdims=True)
                    dgl_s[rows8, sl[h]] = jnp.broadcast_to(dgl, (8, DN_DIM))
                yield

        steps = recurrence()

        bgs = [bg_ref[rows, :] for rows, _, _, _ in where]
        outs = _chunk_bwd([q_ref[rows, cl] for rows, _, _, cl in where], [k_ref[rows, cl] for rows, _, _, cl in where],
                          [v_ref[rows, cl] for rows, _, _, cl in where],
                          [b[:, h:h + 1] for b, (_, _, h, _) in zip(bgs, where)],
                          [b[:, GC_LANE + h:GC_LANE + h + 1] for b, (_, _, h, _) in zip(bgs, where)],
                          [t_ref[h, rows, :] for rows, _, h, _ in where], cots, tick=lambda: next(steps, None))
        for _ in steps:
            pass
        lane = lax.broadcasted_iota(jnp.int32, (CHUNK, BA_PAD), 1)
        for c in range(CH_UNROLL):
            dbg = jnp.zeros((CHUNK, BA_PAD), F32)
            for (rows, _, h, cl), (dq, dk, dv, dbeta, dgc) in list(zip(where, outs))[c * DN_HEADS:(c + 1) * DN_HEADS]:
                dq_ref[rows, cl] = dq
                dk_ref[rows, cl] = dk
                dv_ref[rows, cl] = dv
                dbg = dbg + jnp.where(lane == h, dbeta, 0.0) + jnp.where(lane == GC_LANE + h, dgc, 0.0)
            dbg_ref[where[c * DN_HEADS][0], :] = dbg

    rec = lambda i: jnp.maximum(npass - 1 - i, 0)
    loc = lambda i: jnp.minimum(npass - i, npass - 1)
    tok_r = pl.BlockSpec((tp, DN_WIDTH), lambda i: (rec(i), 0))
    tok_l = pl.BlockSpec((tp, DN_WIDTH), lambda i: (loc(i), 0))
    sq_r = pl.BlockSpec((DN_HEADS, tp, CHUNK), lambda i: (0, rec(i), 0))
    sq_l = pl.BlockSpec((DN_HEADS, tp, CHUNK), lambda i: (0, loc(i), 0))
    ba_l = pl.BlockSpec((tp, BA_PAD), lambda i: (loc(i), 0))
    return pl.pallas_call(
        body, name="dn_backward", grid=(npass + 1,), compiler_params=_params("arbitrary"),
        in_specs=[tok_r, pl.BlockSpec((CH_UNROLL * DN_DIM, DN_WIDTH), lambda i: (rec(i), 0)), tok_r, tok_r, tok_r, tok_r,
                  sq_r, pl.BlockSpec((CH_UNROLL * 8, DN_WIDTH), lambda i: (rec(i), 0)),
                  tok_l, tok_l, tok_l, ba_l, sq_l],
        out_specs=(tok_l, tok_l, tok_l, ba_l),
        out_shape=(SDS((s, DN_WIDTH), F32),) * 3 + (SDS((s, BA_PAD), F32),),
        scratch_shapes=[pltpu.VMEM((DN_HEADS, DN_DIM, DN_DIM), F32)] + [pltpu.VMEM((tp, DN_WIDTH), F32)] * 4
        + [pltpu.VMEM((DN_HEADS, tp, CHUNK), F32), pltpu.VMEM((CH_UNROLL * 8, DN_WIDTH), F32)],
    )(do, st, vn, w, qd, kd, p, gl, q, k, v, bg, t)


def _dn_prep_bwd(qkv_pre, ba, dq, dk, dv, dbg, conv_w8, alog_row, dtb_row, hbf, dz_dn, ts):
    s = qkv_pre.shape[0]
    cw = 3 * DN_WIDTH
    nt = s // ts

    def body(pre_ref, ph_ref, nh_ref, ba_ref, dq_ref, dqh_ref, dk_ref, dkh_ref, dv_ref, dvh_ref, dbg_ref,
             cw_ref, al_ref, dtb_ref, h_ref, dz_ref, dpre_ref, dba_ref, dcw_ref, dal_ref, ddtb_ref,
             gqkv_ref, gz_ref, gba_ref):
        n = pl.program_id(0)

        @pl.when(n == 0)
        def _():
            gqkv_ref[...] = jnp.zeros_like(gqkv_ref)
            gz_ref[...] = jnp.zeros_like(gz_ref)
            gba_ref[...] = jnp.zeros_like(gba_ref)
            dcw_ref[...] = jnp.zeros_like(dcw_ref)
            dal_ref[...] = jnp.zeros_like(dal_ref)
            ddtb_ref[...] = jnp.zeros_like(ddtb_ref)

        last = n == nt - 1
        prev = jnp.where(n == 0, 0.0, ph_ref[...])
        ext = jnp.concatenate([prev, pre_ref[...], nh_ref[...]], axis=0)
        taps = _conv_taps(ext, ts + 8)
        conv = taps[0] * cw_ref[0:1, :]
        for j in range(1, CONV_K):
            conv = conv + taps[j] * cw_ref[j:j + 1, :]

        def cot(main, halo, cols):
            return jnp.concatenate([main[:, cols], jnp.where(last, 0.0, halo[:, cols])], axis=0)

        pieces = []
        for grp, (fn, mref, href) in enumerate(((_post_q, dq_ref, dqh_ref), (_post_k, dk_ref, dkh_ref),
                                                (_post_v, dv_ref, dvh_ref))):
            for h in range(DN_HEADS):
                cols = slice(h * DN_DIM, (h + 1) * DN_DIM)
                c0 = grp * DN_WIDTH + h * DN_DIM
                _, vjp = jax.vjp(fn, conv[:, c0:c0 + DN_DIM])
                pieces.append(vjp(cot(mref, href, cols))[0])
        dconv = jnp.concatenate(pieces, axis=1)
        rows = ts + 8
        dpre = dconv[:ts] * cw_ref[CONV_K - 1:CONV_K, :]
        for j in range(CONV_K - 1):
            sh = CONV_K - 1 - j
            dpre = dpre + pltpu.roll(dconv, rows - sh, 0)[:ts] * cw_ref[j:j + 1, :]
        dpre_b = _bf(dpre)
        dpre_ref[...] = dpre_b
        hb = h_ref[...]
        gqkv_ref[...] += lax.dot_general(hb, dpre_b, _TN, preferred_element_type=F32)
        gz_ref[...] += lax.dot_general(hb, dz_ref[...], _TN, preferred_element_type=F32)
        for j in range(CONV_K):
            dcw_ref[j:j + 1, :] += jnp.sum(dconv[:ts] * taps[j][:ts], axis=0, keepdims=True)

        dbg = dbg_ref[...]
        lane = lax.broadcasted_iota(jnp.int32, dbg.shape, 1)
        dg = pltpu.roll(_chunk_cumsum(dbg, reverse=True), BA_PAD - DN_HEADS, 1)
        cot_bg = jnp.where(lane < DN_HEADS, dbg, jnp.where(lane < GC_LANE, dg, 0.0))
        _, vjp = jax.vjp(_beta_decay, ba_ref[...], al_ref[...], dtb_ref[...])
        dba, dal, ddtb = vjp(cot_bg)
        dba_b = _bf(dba)
        dba_ref[...] = dba_b
        gba_ref[...] += lax.dot_general(hb, dba_b, _TN, preferred_element_type=F32)
        dal_ref[...] += dal
        ddtb_ref[...] += ddtb

    tok = lambda w: pl.BlockSpec((ts, w), lambda i: (i, 0))
    full = lambda a: pl.BlockSpec(a.shape, lambda i: (0, 0))
    prevh = lambda w: pl.BlockSpec((8, w), lambda i: (jnp.maximum(i * (ts // 8) - 1, 0), 0))
    nexth = lambda w: pl.BlockSpec((8, w), lambda i: (jnp.minimum((i + 1) * (ts // 8), s // 8 - 1), 0))
    row = pl.BlockSpec((1, LANES), lambda i: (0, 0))
    return pl.pallas_call(
        body, name="dn_prep_bwd", grid=(nt,), compiler_params=_params("arbitrary"),
        in_specs=[tok(cw), prevh(cw), nexth(cw), tok(BA_PAD),
                  tok(DN_WIDTH), nexth(DN_WIDTH), tok(DN_WIDTH), nexth(DN_WIDTH), tok(DN_WIDTH), nexth(DN_WIDTH),
                  tok(BA_PAD), full(conv_w8), full(alog_row), full(dtb_row), tok(D_MODEL), tok(DN_WIDTH)],
        out_specs=(tok(cw), tok(BA_PAD), pl.BlockSpec((8, cw), lambda i: (0, 0)), row, row)
        + tuple(pl.BlockSpec((D_MODEL, w), lambda i: (0, 0)) for w in (cw, DN_WIDTH, BA_PAD)),
        out_shape=(SDS((s, cw), BF16), SDS((s, BA_PAD), BF16), SDS((8, cw), F32), SDS((1, LANES), F32),
                   SDS((1, LANES), F32)) + tuple(SDS((D_MODEL, w), F32) for w in (cw, DN_WIDTH, BA_PAD)),
    )(qkv_pre, qkv_pre, qkv_pre, ba, dq, dq, dk, dk, dv, dv, dbg, conv_w8, alog_row, dtb_row, hbf, dz_dn)


def _dh_dx(dps, ws, x, mod, norm_w, dx2, ts):
    s = x.shape[0]
    widths = [w.shape[1] for w in ws]
    np_ = len(ws)

    def body(*refs):
        dp_refs, w_refs = refs[:np_], refs[np_:2 * np_]
        x_ref, mod_ref, nw_ref, dx2_ref, gx_ref, dshift, dscale, dnw = refs[2 * np_:]

        @pl.when(pl.program_id(0) == 0)
        def _():
            dshift[...] = jnp.zeros_like(dshift)
            dscale[...] = jnp.zeros_like(dscale)
            dnw[...] = jnp.zeros_like(dnw)

        dh = lax.dot_general(dp_refs[0][...], w_refs[0][...], _NT, preferred_element_type=F32)
        for a, b in zip(dp_refs[1:], w_refs[1:]):
            dh = dh + lax.dot_general(a[...], b[...], _NT, preferred_element_type=F32)
        xt = x_ref[...]
        r = lax.rsqrt(jnp.mean(xt * xt, axis=-1, keepdims=True) + EPS)
        xn = xt * r
        nw = nw_ref[...]
        sc1 = 1.0 + mod_ref[:, D_MODEL:2 * D_MODEL]
        dshift[...] += jnp.sum(dh, axis=0, keepdims=True)
        dscale[...] += jnp.sum(dh * (xn * nw), axis=0, keepdims=True)
        dnw[...] += jnp.sum(dh * sc1 * xn, axis=0, keepdims=True)
        dxn = dh * sc1 * nw
        gx_ref[...] = r * (dxn - xn * jnp.mean(dxn * xn, axis=-1, keepdims=True)) + dx2_ref[...]

    tok = lambda w: pl.BlockSpec((ts, w), lambda i: (i, 0))
    full = lambda a: pl.BlockSpec(a.shape, lambda i: (0, 0))
    row = pl.BlockSpec((1, D_MODEL), lambda i: (0, 0))
    return pl.pallas_call(
        body, name="dh_dx", grid=(s // ts,), compiler_params=_params("arbitrary"),
        in_specs=[tok(w) for w in widths] + [full(w) for w in ws] + [tok(D_MODEL), full(mod), full(norm_w),
                                                                    tok(D_MODEL)],
        out_specs=(tok(D_MODEL), row, row, row),
        out_shape=(SDS((s, D_MODEL), F32),) + (SDS((1, D_MODEL), F32),) * 3,
    )(*dps, *ws, x, mod, norm_w, dx2)


def _grad_w_in_at(h, dq, dk, dv, dz_at, cos_t, sin_t, ts):
    s = h.shape[0]

    def body(h_ref, q_ref, k_ref, v_ref, dz_ref, cos_ref, sin_ref, oq, ok, ov, gq, gk, gv, gz):
        @pl.when(pl.program_id(0) == 0)
        def _():
            for o in (gq, gk, gv, gz):
                o[...] = jnp.zeros_like(o)

        cs, sn = cos_ref[...], sin_ref[...]
        for j in range(AT_PAIRS):
            cols = slice(j * LANES, (j + 1) * LANES)
            for g_ref, o_ref in ((q_ref, oq), (k_ref, ok)):
                g = g_ref[j]
                o_ref[:, cols] = _bf(g * cs + _swap_half64(g * sn))
            ov[:, cols] = _bf(v_ref[j])
        hb = h_ref[...]
        for p, o in ((oq, gq), (ok, gk), (ov, gv), (dz_ref, gz)):
            o[...] += lax.dot_general(hb, p[...], _TN, preferred_element_type=F32)

    tok = lambda w: pl.BlockSpec((ts, w), lambda i: (i, 0))
    pairs = pl.BlockSpec((AT_PAIRS, ts, LANES), lambda i: (0, i, 0))
    acc = pl.BlockSpec((D_MODEL, AT_WIDTH), lambda i: (0, 0))
    return pl.pallas_call(
        body, name="grad_w_in_at", grid=(s // ts,), compiler_params=_params("arbitrary"),
        in_specs=[tok(D_MODEL), pairs, pairs, pairs, tok(AT_WIDTH), tok(LANES), tok(LANES)],
        out_specs=(tok(AT_WIDTH),) * 3 + (acc,) * 4,
        out_shape=(SDS((s, AT_WIDTH), BF16),) * 3 + (SDS((D_MODEL, AT_WIDTH), F32),) * 4,
    )(h, dq, dk, dv, dz_at, cos_t, sin_t)


def _adamw_math(w, g, m, v):
    m = ADAM_B1 * m + (1.0 - ADAM_B1) * g
    v = ADAM_B2 * v + (1.0 - ADAM_B2) * (g * g)
    m_hat = m / (1.0 - ADAM_B1 ** ADAM_STEP)
    v_hat = v / (1.0 - ADAM_B2 ** ADAM_STEP)
    delta = -ADAM_LR * (m_hat / (jnp.sqrt(v_hat) + ADAM_EPS) + ADAM_WD * w)
    return delta, m, v


def _adamw(w, m, v, g, name, own=None):
    def body(w_ref, m_ref, v_ref, g_ref, *rest):
        g_out, d_out, m_out, v_out = rest[-4:]
        if own is None:
            g = g_ref[...]
        else:
            g = g_ref[0].astype(F32)
            for k in range(1, N_DEV):
                g = g + g_ref[k].astype(F32)
            g = g + rest[0][...].astype(F32)
        g_out[...] = g
        d_out[...], m_out[...], v_out[...] = _adamw_math(w_ref[...], g, m_ref[...], v_ref[...])

    args = (w, m, v, g) if own is None else (w, m, v, g, own)
    return pl.pallas_call(body, name=name, compiler_params=_params(),
                          out_shape=(SDS(w.shape, F32),) * 4)(*args)


def _adamw_w_mod(w, m, v, siluc_all, dmod_mine):
    def body(w_ref, m_ref, v_ref, sc_ref, dm_ref, g_out, d_out, m_out, v_out):
        g = _htn(sc_ref[...], dm_ref[...])
        g_out[...] = g
        d_out[...], m_out[...], v_out[...] = _adamw_math(w_ref[...], g, m_ref[...], v_ref[...])

    return pl.pallas_call(body, name="adamw_w_mod", compiler_params=_params(),
                          out_shape=(SDS(w.shape, F32),) * 4)(w, m, v, siluc_all, dmod_mine)


def _pack_sum(pack_all):
    def body(p_ref, o_ref):
        t = p_ref[0]
        for k in range(1, N_DEV):
            t = t + p_ref[k]
        o_ref[...] = t

    return pl.pallas_call(body, name="pack_sum", out_shape=SDS(pack_all.shape[1:], F32))(pack_all)


def _tile(s, want):
    t = min(want, s)
    assert s % t == 0
    return t


def _local_step(x, c, positions, w_mod_bf, b_mod, norm_w, w_in_bf, conv_w, a_log, dt_bias, dn_norm_w, at_norm_w,
                w_out_bf, final_norm_w, tgt):
    s = x.shape[0]
    o = [0]
    for wdt in IN_SPLITS:
        o.append(o[-1] + wdt)
    w_ba = jnp.pad(w_in_bf[:, o[2]:o[4]], ((0, 0), (0, BA_PAD - 2 * DN_HEADS)))
    ws = [w_in_bf[:, o[0]:o[1]], w_in_bf[:, o[1]:o[2]], w_ba, w_in_bf[:, o[4]:o[5]], w_in_bf[:, o[5]:o[6]],
          w_in_bf[:, o[6]:o[7]], w_in_bf[:, o[7]:o[8]]]
    conv_w8 = jnp.pad(conv_w, ((0, 8 - CONV_K), (0, 0)))
    alog_row = jnp.pad(a_log, ((0, 0), (DN_HEADS, BA_PAD - 2 * DN_HEADS)))
    dtb_row = jnp.pad(dt_bias, ((0, 0), (DN_HEADS, BA_PAD - 2 * DN_HEADS)))
    atw2 = jnp.concatenate([at_norm_w, at_norm_w], axis=1)

    half = AT_DIM // 2
    lane = jnp.arange(LANES)
    inv_freq = ROPE_THETA ** (-(lane % half).astype(F32) / half)
    ang = positions.astype(F32)[:, None] * inv_freq
    cos_t = jnp.cos(ang)
    sin_t = jnp.sin(ang) * jnp.where((lane // half) % 2 == 0, -1.0, 1.0)

    mod, siluc = _adaln_mod(c, w_mod_bf, b_mod)
    gate = mod[:, 2 * D_MODEL:]
    hbf, qkv_pre, z_dn, ba, qr, kr, vb, z_at, q, k, v, bg = _ln_proj(
        x, mod, norm_w, ws, cos_t, sin_t, conv_w8, alog_row, dtb_row, _tile(s, 256))
    w, qd, kd, p, gl, tinv, o_dn, vn, st = _dn_forward(q, k, v, bg)
    o_at, lse = _attn_fwd(qr, kr, vb)
    (dx2, gw_out, dfw, dgate, loss, do_dn, dz_dn, do_at, dz_at, delta, ddnw, datw) = _out_loss(
        o_dn, z_dn, o_at, z_at, dn_norm_w, atw2, x, tgt, w_out_bf, gate, final_norm_w, _tile(s, 512))

    daq, dak, dav, g_aq, g_ak, g_av, g_az = _grad_w_in_at(hbf, *_attn_bwd(qr, kr, vb, do_at, lse, delta), dz_at,
                                                           cos_t, sin_t, _tile(s, 512))
    dq, dk, dv, dbg = _dn_backward(do_dn, st, vn, w, qd, kd, p, gl, q, k, v, bg, tinv)
    dqkv, dba, dcw, dal, ddtb, g_qkv, g_z, g_ba = _dn_prep_bwd(qkv_pre, ba, dq, dk, dv, dbg, conv_w8, alog_row, dtb_row,
                                                               hbf, dz_dn, _tile(s, 512))
    dps = [dqkv, dz_dn, dba, daq, dak, dav, dz_at]
    gw_in = jnp.concatenate([g_qkv, g_z, g_ba[:, :2 * DN_HEADS], g_aq, g_ak, g_av, g_az], axis=1)
    small = dict(conv=dcw[:CONV_K], dgate=dgate, siluc=siluc, dfw=dfw, alog=dal, dtb=ddtb, dnn=ddnw, atn=datw)

    def input_grad(token):
        gx, dshift, dscale, dnw = _dh_dx(dps, ws, x, mod + token, norm_w, dx2, _tile(s, 512))
        return gx, jnp.concatenate([dshift, dscale, small["dgate"]], axis=1), dnw

    return loss, gw_in, gw_out, small, input_grad


def kernel(x, c, positions, w_mod, b_mod, norm_w, w_in, conv_w, a_log, dt_bias, dn_norm_w, at_norm_w, w_out, final_norm_w, loss_target, m_w_mod, m_b_mod, m_norm_w, m_w_in, m_conv_w, m_a_log, m_dt_bias, m_dn_norm_w, m_at_norm_w, m_w_out, m_final_norm_w, v_w_mod, v_b_mod, v_norm_w, v_w_in, v_conv_w, v_a_log, v_dt_bias, v_dn_norm_w, v_at_norm_w, v_w_out, v_final_norm_w):
    me = 4 * lax.axis_index("x") + 2 * lax.axis_index("y") + lax.axis_index("c")
    s = x.shape[1]

    g_mod, g_in, g_conv, g_out = _all_gather(
        [_bf(w_mod[0]), _bf(w_in[0]), conv_w[0], _bf(w_out[0])], "gather_weights")
    w_mod_bf = g_mod.transpose(1, 0, 2).reshape(D_MODEL, 3 * D_MODEL)
    w_in_bf = g_in.transpose(1, 0, 2).reshape(D_MODEL, IN_COLS)
    conv_full = g_conv.transpose(1, 0, 2).reshape(CONV_K, 3 * DN_WIDTH)
    w_out_bf = g_out.reshape(D_MODEL, D_MODEL)

    loss, gw_in, gw_out, small, input_grad = _local_step(
        x[0], c, positions[0], w_mod_bf, b_mod, norm_w, w_in_bf, conv_full, a_log, dt_bias, dn_norm_w, at_norm_w,
        w_out_bf, final_norm_w.reshape(1, D_MODEL), loss_target[0])

    gw_in_slabs = _bf(gw_in).reshape(D_MODEL, N_DEV, IN_SHARD).transpose(1, 0, 2)
    gw_out_slabs = _bf(gw_out).reshape(N_DEV, D_MODEL // N_DEV, D_MODEL)
    send_sems, recv_sems, srcs, lands, token = _scatter_start([gw_in_slabs, gw_out_slabs])
    gx, dmod, dnw = input_grad(token[0, 0])
    r_in, r_out = _scatter_wait(send_sems, recv_sems, srcs, lands, gx)
    own_in = lax.dynamic_index_in_dim(gw_in_slabs, me, 0, keepdims=False)
    own_out = lax.dynamic_index_in_dim(gw_out_slabs, me, 0, keepdims=False)

    pack = jnp.concatenate([small["conv"].reshape(1, -1), dmod, small["siluc"], dnw, small["dfw"],
                            small["alog"], small["dtb"], small["dnn"], small["atn"],
                            jnp.pad(loss, ((0, 0), (0, LANES - 1)))], axis=1).reshape(PK_ROWS, LANES)
    (pack_all,) = _exchange([pack], [False], "exchange_small")

    res = {}
    res["w_in"] = _adamw(w_in[0], m_w_in[0], v_w_in[0], r_in, "adamw_w_in", own=own_in)
    res["w_out"] = _adamw(w_out[0], m_w_out[0], v_w_out[0], r_out, "adamw_w_out", own=own_out)
    flat_all = pack_all.reshape(N_DEV, PK_END)
    dmod_mine = lax.dynamic_slice(flat_all, (0, PK_DMOD + me * (3 * D_MODEL // N_DEV)), (N_DEV, 3 * D_MODEL // N_DEV))
    res["w_mod"] = _adamw_w_mod(w_mod[0], m_w_mod[0], v_w_mod[0], flat_all[:, PK_SILUC:PK_DNW], dmod_mine)
    tot = _pack_sum(pack_all).reshape(1, PK_END)
    g_conv_full = tot[:, PK_CONV:PK_DMOD].reshape(CONV_K, 3 * DN_WIDTH)
    g_conv_mine = lax.dynamic_slice(g_conv_full, (0, me * (3 * DN_WIDTH // N_DEV)), (CONV_K, 3 * DN_WIDTH // N_DEV))
    res["conv_w"] = _adamw(conv_w[0], m_conv_w[0], v_conv_w[0], g_conv_mine, "adamw_conv_w")
    res["b_mod"] = _adamw(b_mod, m_b_mod, v_b_mod, tot[:, PK_DMOD:PK_SILUC], "adamw_b_mod")
    res["norm_w"] = _adamw(norm_w, m_norm_w, v_norm_w, tot[:, PK_DNW:PK_DFW], "adamw_norm_w")
    res["a_log"] = _adamw(a_log, m_a_log, v_a_log, tot[:, PK_ALOG + DN_HEADS:PK_ALOG + 2 * DN_HEADS], "adamw_a_log")
    res["dt_bias"] = _adamw(dt_bias, m_dt_bias, v_dt_bias, tot[:, PK_DTB + DN_HEADS:PK_DTB + 2 * DN_HEADS],
                            "adamw_dt_bias")
    res["dn_norm_w"] = _adamw(dn_norm_w, m_dn_norm_w, v_dn_norm_w, tot[:, PK_DNN:PK_ATN], "adamw_dn_norm_w")
    g_atn = tot[:, PK_ATN:PK_ATN + AT_DIM] + tot[:, PK_ATN + AT_DIM:PK_LOSS]
    res["at_norm_w"] = _adamw(at_norm_w, m_at_norm_w, v_at_norm_w, g_atn, "adamw_at_norm_w")
    fin = _adamw(final_norm_w.reshape(1, D_MODEL), m_final_norm_w.reshape(1, D_MODEL),
                 v_final_norm_w.reshape(1, D_MODEL), tot[:, PK_DFW:PK_ALOG], "adamw_final_norm_w")
    res["final_norm_w"] = tuple(a.reshape(D_MODEL) for a in fin)

    lead = ("w_mod", "w_in", "conv_w", "w_out")
    names = ("w_mod", "b_mod", "norm_w", "w_in", "conv_w", "a_log", "dt_bias", "dn_norm_w", "at_norm_w", "w_out",
             "final_norm_w")
    out = [tot[0, PK_LOSS], gx.reshape(1, s, D_MODEL)]
    for kind in range(4):
        for nm in names:
            a = res[nm][kind]
            out.append(a[None] if nm in lead else a)
    return tuple(out)
```

```python
import functools

import jax
import jax.numpy as jnp
from jax import lax
from jax.experimental import pallas as pl
from jax.experimental.pallas import tpu as pltpu

F32, BF16 = jnp.float32, jnp.bfloat16
HI = lax.Precision.HIGHEST
SDS = jax.ShapeDtypeStruct

D_MODEL = 1024
DN_HEADS, DN_DIM, DN_WIDTH = 4, 128, 512
AT_HEADS, AT_DIM, AT_WIDTH = 8, 64, 512
CONV_K = 4
CHUNK = 64
Q_BLOCK = 128
W_SUB = 128
DILATIONS = (1, 4, 16)
AT_PAIRS = 4
ATT_BLK = Q_BLOCK * max(DILATIONS)
ATT_UNROLL, ATT_UNROLL_BWD = 8, 4
CH_UNROLL = 4
ROPE_THETA = 10000.0
EPS = 1e-6
N_DEV = 8
LANES = 128
BA_PAD = 128
IN_SPLITS = (1536, 512, 4, 4, 512, 512, 512, 512)
IN_COLS = sum(IN_SPLITS)
IN_SHARD = IN_COLS // N_DEV
VMEM_LIMIT = 58 * 2 ** 20

ADAM_LR, ADAM_B1, ADAM_B2, ADAM_EPS, ADAM_WD, ADAM_STEP = 0.001, 0.9, 0.999, 1e-08, 0.01, 10

PK_CONV, PK_DMOD, PK_SILUC, PK_DNW, PK_DFW, PK_ALOG, PK_DTB, PK_DNN, PK_ATN, PK_LOSS, PK_END = (
    0, 6144, 9216, 10240, 11264, 12288, 12416, 12544, 12672, 12800, 12928)
PK_ROWS = PK_END // LANES

_NT = (((1,), (1,)), ((), ()))
_TN = (((0,), (0,)), ((), ()))


def _params(*sem):
    return pltpu.CompilerParams(dimension_semantics=sem or None, vmem_limit_bytes=VMEM_LIMIT)


def _bf(x):
    return x.astype(BF16)


def _nn(a, b):
    return jnp.dot(_bf(a), _bf(b), preferred_element_type=F32)


def _nt(a, b):
    return lax.dot_general(_bf(a), _bf(b), _NT, preferred_element_type=F32)


def _tn(a, b):
    return lax.dot_general(_bf(a), _bf(b), _TN, preferred_element_type=F32)


def _htn(a, b):
    return lax.dot_general(a, b, _TN, precision=HI, preferred_element_type=F32)


def _head_sum(x):
    r = lax.broadcasted_iota(jnp.int32, (LANES, LANES), 0)
    c = lax.broadcasted_iota(jnp.int32, (LANES, LANES), 1)
    same = jnp.where((r // AT_DIM) == (c // AT_DIM), 1.0, 0.0).astype(BF16)
    hi, lo = _hl(x)
    return jnp.dot(hi, same, preferred_element_type=F32) + jnp.dot(lo, same, preferred_element_type=F32)


@jax.custom_vjp
def _d_head_sum(x):
    return _head_sum(x)


_d_head_sum.defvjp(lambda x: (_head_sum(x), None), lambda _, g: (_head_sum(g),))


def _silu(x):
    return x * jax.nn.sigmoid(x)


def _softplus(x):
    return jnp.maximum(x, 0.0) + jnp.log(1.0 + jnp.exp(-jnp.abs(x)))


def _l2n(x):
    return x * lax.rsqrt(jnp.sum(x * x, axis=-1, keepdims=True) + EPS)


def _post_q(x):
    return _l2n(_silu(x)) * (DN_DIM ** -0.5)


def _post_k(x):
    return _l2n(_silu(x))


def _post_v(x):
    return _silu(x)


def _beta_decay(ba, alog_row, dtb_row):
    lane = lax.broadcasted_iota(jnp.int32, ba.shape, 1)
    return jnp.where(lane < DN_HEADS, jax.nn.sigmoid(ba), -jnp.exp(alog_row) * _softplus(ba + dtb_row))


def _gate_dn(o, z, w):
    return (o * lax.rsqrt(jnp.mean(o * o, axis=-1, keepdims=True) + EPS)) * w * _silu(z)


def _gate_at(o, z, w2, head_sum):
    ms = head_sum(o * o) * (1.0 / AT_DIM)
    return (o * lax.rsqrt(ms + EPS)) * w2 * _silu(z)


def _swap_half64(x):
    lane = lax.broadcasted_iota(jnp.int32, x.shape, 1)
    return jnp.where((lane & (AT_DIM - 1)) < AT_DIM // 2, pltpu.roll(x, LANES - AT_DIM // 2, 1),
                     pltpu.roll(x, AT_DIM // 2, 1))


_NN = (((1,), (0,)), ((), ()))


def _hl(a):
    hi = a.astype(BF16)
    return hi, (a - hi.astype(F32)).astype(BF16)


def _mm3(a, b, dims=_NN):
    (ah, al), (bh, bl) = a, b
    f = lambda x, y: lax.dot_general(x, y, dims, preferred_element_type=F32)
    return f(ah, bh) + (f(ah, bl) + f(al, bh))


def _chunk_masks():
    r = lax.broadcasted_iota(jnp.int32, (CHUNK, CHUNK), 0)
    c = lax.broadcasted_iota(jnp.int32, (CHUNK, CHUNK), 1)
    return r >= c, r > c, (r == c).astype(F32), (r // 16) == (c // 16)


def _tri_inv(mats, tick=lambda: None):
    _, _, eye, blk = _chunk_masks()
    dg = [jnp.where(blk, a, 0.0) for a in mats]
    lo = [jnp.where(blk, 0.0, a) for a in mats]
    sdg = [_hl(x) for x in dg]
    d2 = [_mm3(s, s) for s in sdg]
    tick()
    sd2 = [_hl(x) for x in d2]
    d4 = [_mm3(s, s) for s in sd2]
    tick()
    sd4 = [_hl(x) for x in d4]
    d8 = [_mm3(s, s) for s in sd4]
    tick()
    p1 = [_mm3(_hl(eye - a), _hl(eye + b)) for a, b in zip(dg, d2)]
    tick()
    p2 = [_mm3(_hl(a), _hl(eye + b)) for a, b in zip(p1, d4)]
    tick()
    dinv = [_mm3(_hl(a), _hl(eye + b)) for a, b in zip(p2, d8)]
    tick()
    sdinv = [_hl(x) for x in dinv]
    n1 = [_mm3(s, _hl(b)) for s, b in zip(sdinv, lo)]
    tick()
    sn1 = [_hl(x) for x in n1]
    n2 = [_mm3(s, s) for s in sn1]
    tick()
    q1 = [_mm3(_hl(eye - a), _hl(eye + b)) for a, b in zip(n1, n2)]
    return [_mm3(_hl(a), s) for a, s in zip(q1, sdinv)]


def _chunk_common(qs, ks, vs, betas, gcs):
    tril, _, _, _ = _chunk_masks()
    out = []
    for q, k, v, beta, gc in zip(qs, ks, vs, betas, gcs):
        gb = jnp.broadcast_to(gc, (CHUNK, DN_DIM))
        gt = gb.T[:CHUNK, :]
        gam = jnp.where(tril, jnp.exp(jnp.where(tril, gb[:, :CHUNK] - gt, 0.0)), 0.0)
        last = gb[CHUNK - 1:CHUNK, :]
        eg, e2 = jnp.exp(gb), jnp.exp(last - gb)
        kb, vb = k * beta, v * beta
        out.append(dict(gam=gam, eg=eg, e2=e2, gl=jnp.exp(last[:, 0:1]), kb=kb, vb=vb, kbg=kb * eg,
                        m=_nt(kb, k), qk=_nt(q, k)))
    return out


def _chunk_fwd(qs, ks, vs, betas, gcs, tick=lambda: None):
    tril, strict, _, _ = _chunk_masks()
    cm = _chunk_common(qs, ks, vs, betas, gcs)
    ts = _tri_inv([jnp.where(strict, c["m"] * c["gam"], 0.0) for c in cm], tick)
    outs = []
    for q, k, c, t in zip(qs, ks, cm, ts):
        uw = _nn(t, jnp.concatenate([c["vb"], c["kbg"]], axis=1))
        p = jnp.where(tril, c["qk"] * c["gam"], 0.0)
        outs.append((uw[:, :DN_DIM], uw[:, DN_DIM:], p, q * c["eg"], k * c["e2"], c["gl"], t.T))
    return outs


def _chunk_bwd(qs, ks, vs, betas, gcs, ts, cots, tick=lambda: None):
    tril, strict, _, _ = _chunk_masks()
    cm = _chunk_common(qs, ks, vs, betas, gcs)
    tick()
    row = lax.broadcasted_iota(jnp.int32, (CHUNK, 1), 0)
    ones = jnp.ones((CHUNK, DN_DIM), BF16)
    rs = lambda x: jnp.sum(x, axis=-1, keepdims=True)
    tts = [_bf(t) for t in ts]
    duw = [_bf(jnp.concatenate([ct[0], ct[1]], axis=1)) for ct in cots]
    dts = [_nt(a, jnp.concatenate([c["vb"], c["kbg"]], axis=1)) for a, c in zip(duw, cm)]
    tick()
    xs = [_nn(t, d) for t, d in zip(tts, dts)]
    tick()
    das = [jnp.where(strict, -_nn(x, t), 0.0) for x, t in zip(xs, tts)]
    dvks = [_nn(t, a) for t, a in zip(tts, duw)]
    tick()
    outs = []
    every = max(1, len(qs) // 5)
    for idx, (q, k, v, beta, c, ct, da, dvk) in enumerate(zip(qs, ks, vs, betas, cm, cots, das, dvks)):
        if idx and idx % every == 0:
            tick()
        _, _, dp, dqd, dkd, dgl = ct
        dvb, dkbg = dvk[:, :DN_DIM], dvk[:, DN_DIM:]
        dm = da * c["gam"]
        dqk = jnp.where(tril, dp, 0.0) * c["gam"]
        e = dm * c["m"] + dqk * c["qk"]
        dmq = jnp.concatenate([dm, dqk], axis=0)
        r1 = _nn(dmq, k)
        dkb = r1[:CHUNK] + dkbg * c["eg"]
        dq = r1[CHUNK:] + dqd * c["eg"]
        dk = _tn(dmq, jnp.concatenate([c["kb"], q], axis=0)) + dkd * c["e2"] + dkb * beta
        dbeta = rs(dkb * k + dvb * v)
        eh, el = _hl(e)
        colsum = (lax.dot_general(eh, ones, _TN, preferred_element_type=F32)
                  + lax.dot_general(el, ones, _TN, preferred_element_type=F32))[:, 0:1]
        pkd = dkd * (k * c["e2"])
        dgc = rs(e) - colsum + rs(dqd * q * c["eg"] + dkbg * c["kbg"] - pkd)
        tail = rs(jnp.sum(pkd, axis=0, keepdims=True)) + dgl * c["gl"]
        dgc = dgc + jnp.where(row == CHUNK - 1, tail, 0.0)
        outs.append((dq, dk, dvb * beta, dbeta, dgc))
    return outs


def _chunk_cumsum(x, reverse=False):
    n = x.shape[0]
    pos = lax.broadcasted_iota(jnp.int32, x.shape, 0) & (CHUNK - 1)
    sh = 1
    while sh < CHUNK:
        if reverse:
            x = x + jnp.where(pos < CHUNK - sh, pltpu.roll(x, n - sh, 0), 0.0)
        else:
            x = x + jnp.where(pos >= sh, pltpu.roll(x, sh, 0), 0.0)
        sh *= 2
    return x


GC_LANE = 2 * DN_HEADS


def _exchange(arrays, scatter, name):
    n = len(arrays)
    out_shapes = []
    for a, sc in zip(arrays, scatter):
        out_shapes.append(SDS(a.shape if sc else (N_DEV,) + a.shape, a.dtype))

    def body(*refs):
        ins, outs = refs[:n], refs[n:2 * n]
        send_sems, recv_sems, loc_sems = refs[2 * n:]
        x, y, c = lax.axis_index("x"), lax.axis_index("y"), lax.axis_index("c")
        me = 4 * x + 2 * y + c
        local, remote = [], []
        for i in range(n):
            src = ins[i].at[me] if scatter[i] else ins[i]
            cp = pltpu.make_async_copy(src, outs[i].at[me], loc_sems.at[i])
            cp.start()
            local.append(cp)
        for dlt in range(1, N_DEV):
            px = 1 - x if dlt & 4 else x
            py = 1 - y if dlt & 2 else y
            pc = 1 - c if dlt & 1 else c
            peer = 4 * px + 2 * py + pc
            for i in range(n):
                src = ins[i].at[peer] if scatter[i] else ins[i]
                cp = pltpu.make_async_remote_copy(
                    src_ref=src, dst_ref=outs[i].at[me],
                    send_sem=send_sems.at[i, dlt - 1], recv_sem=recv_sems.at[i, dlt - 1],
                    device_id=(px, py, pc), device_id_type=pl.DeviceIdType.MESH)
                cp.start()
                arrive = pltpu.make_async_remote_copy(
                    src_ref=src, dst_ref=outs[i].at[peer],
                    send_sem=send_sems.at[i, dlt - 1], recv_sem=recv_sems.at[i, dlt - 1],
                    device_id=(px, py, pc), device_id_type=pl.DeviceIdType.MESH)
                remote.append((cp, arrive))
        for cp, arrive in remote:
            cp.wait_send()
            arrive.wait_recv()
        for cp in local:
            cp.wait()

    any_spec = pl.BlockSpec(memory_space=pl.ANY)
    return pl.pallas_call(
        body, name=name, out_shape=tuple(out_shapes),
        in_specs=[any_spec] * n, out_specs=tuple([any_spec] * n),
        scratch_shapes=[pltpu.SemaphoreType.DMA((n, N_DEV - 1)), pltpu.SemaphoreType.DMA((n, N_DEV - 1)),
                        pltpu.SemaphoreType.DMA((n,))],
    )(*arrays)


def _all_gather(arrays, name):
    n = len(arrays)

    def body(*refs):
        ins, outs = refs[:n], refs[n:2 * n]
        send_sems, recv_sems, loc_sems = refs[2 * n:]
        x, y, c = lax.axis_index("x"), lax.axis_index("y"), lax.axis_index("c")
        me, sibling = (x, y, c), (x, y, 1 - c)
        chips = [(1 - x, y), (x, 1 - y), (1 - x, 1 - y)]

        def copy(i, k, block, to, src=None):
            slot = outs[i].at[4 * block[0] + 2 * block[1] + block[2]]
            return pltpu.make_async_remote_copy(
                src_ref=slot if src is None else src, dst_ref=slot,
                send_sem=send_sems.at[i, k], recv_sem=recv_sems.at[i, k],
                device_id=to, device_id_type=pl.DeviceIdType.MESH)

        mine = [pltpu.make_async_copy(ins[i], outs[i].at[4 * x + 2 * y + c], loc_sems.at[i]) for i in range(n)]
        for cp in mine:
            cp.start()
        first = []
        for i in range(n):
            first.append(copy(i, 0, me, sibling, src=ins[i]))
            first += [copy(i, 1 + j, me, (*chip, c), src=ins[i]) for j, chip in enumerate(chips)]
        for cp in first:
            cp.start()
        passed = []
        for j, chip in enumerate(chips):
            for i in range(n):
                copy(i, 1 + j, (*chip, c), me).wait_recv()
                fwd = copy(i, 4 + j, (*chip, c), sibling)
                fwd.start()
                passed.append(fwd)
        for i in range(n):
            copy(i, 0, sibling, me).wait_recv()
        for j, chip in enumerate(chips):
            for i in range(n):
                copy(i, 4 + j, (*chip, 1 - c), me).wait_recv()
        for cp in first + passed:
            cp.wait_send()
        for cp in mine:
            cp.wait()

    any_spec = pl.BlockSpec(memory_space=pl.ANY)
    return pl.pallas_call(
        body, name=name, out_shape=tuple(SDS((N_DEV,) + a.shape, a.dtype) for a in arrays),
        in_specs=[any_spec] * n, out_specs=tuple([any_spec] * n),
        scratch_shapes=[pltpu.SemaphoreType.DMA((n, N_DEV - 1)), pltpu.SemaphoreType.DMA((n, N_DEV - 1)),
                        pltpu.SemaphoreType.DMA((n,))],
    )(*arrays)


_HBM = pl.BlockSpec(memory_space=pltpu.HBM)
_SEM = pl.BlockSpec(memory_space=pltpu.SEMAPHORE)


def _peers(x, y, c):
    out = []
    for dlt in range(1, N_DEV):
        px = 1 - x if dlt & 4 else x
        py = 1 - y if dlt & 2 else y
        pc = 1 - c if dlt & 1 else c
        out.append((dlt, (px, py, pc), 4 * px + 2 * py + pc))
    return out


def _scatter_start(arrays):
    n = len(arrays)
    ns = n * (N_DEV - 1)

    def body(*refs):
        ins, lands = refs[:n], refs[n:2 * n]
        send_sems, recv_sems = refs[2 * n:2 * n + ns], refs[2 * n + ns:2 * n + 2 * ns]
        token = refs[-1]
        x, y, c = lax.axis_index("x"), lax.axis_index("y"), lax.axis_index("c")
        me = 4 * x + 2 * y + c
        for dlt, peer, pi in _peers(x, y, c):
            for i in range(n):
                k = i * (N_DEV - 1) + dlt - 1
                pltpu.make_async_remote_copy(
                    src_ref=ins[i].at[pi], dst_ref=lands[i].at[me], send_sem=send_sems[k], recv_sem=recv_sems[k],
                    device_id=peer, device_id_type=pl.DeviceIdType.MESH).start()
        token[...] = jnp.zeros_like(token)

    sem = pltpu.SemaphoreType.DMA(())
    thru = tuple(pltpu.HBM(a.shape, a.dtype) for a in arrays)
    hbm = lambda a: pltpu.with_memory_space_constraint(a, pltpu.HBM)
    outs = pl.pallas_call(
        body, name="scatter_start", out_shape=(sem,) * (2 * ns) + thru + thru + (SDS((8, LANES), F32),),
        in_specs=[_HBM] * (2 * n),
        out_specs=(_SEM,) * (2 * ns) + (_HBM,) * (2 * n) + (pl.BlockSpec(memory_space=pltpu.VMEM),),
        input_output_aliases={i: 2 * ns + i for i in range(2 * n)},
        compiler_params=pltpu.CompilerParams(has_side_effects=pltpu.SideEffectType.DATAFLOW_SIDE_EFFECTING),
    )(*[hbm(a) for a in arrays], *[hbm(jnp.zeros(a.shape, a.dtype)) for a in arrays])
    return outs[:ns], outs[ns:2 * ns], outs[2 * ns:2 * ns + n], outs[2 * ns + n:2 * ns + 2 * n], outs[-1]


def _scatter_wait(send_sems, recv_sems, srcs, lands, after):
    n = len(srcs)
    ns = n * (N_DEV - 1)

    def body(*refs):
        ins, lands_ = refs[:n], refs[n:2 * n]
        send, recv = refs[2 * n:2 * n + ns], refs[2 * n + ns:2 * n + 2 * ns]
        x, y, c = lax.axis_index("x"), lax.axis_index("y"), lax.axis_index("c")
        for dlt, peer, pi in _peers(x, y, c):
            for i in range(n):
                k = i * (N_DEV - 1) + dlt - 1
                cp = pltpu.make_async_remote_copy(
                    src_ref=ins[i].at[pi], dst_ref=lands_[i].at[pi], send_sem=send[k], recv_sem=recv[k],
                    device_id=peer, device_id_type=pl.DeviceIdType.MESH)
                cp.wait_send()
                cp.wait_recv()

    thru = tuple(pltpu.HBM(a.shape, a.dtype) for a in srcs)
    outs = pl.pallas_call(
        body, name="scatter_wait", out_shape=thru + thru,
        in_specs=[_HBM] * (2 * n) + [_SEM] * (2 * ns) + [pl.BlockSpec(memory_space=pl.ANY)],
        out_specs=(_HBM,) * (2 * n), input_output_aliases={i: i for i in range(2 * n)},
        compiler_params=pltpu.CompilerParams(has_side_effects=pltpu.SideEffectType.DATAFLOW_SIDE_EFFECTING),
    )(*srcs, *lands, *send_sems, *recv_sems, after)
    return outs[n:]


def _adaln_mod(c, w_mod, b_mod):
    def body(c_ref, w_ref, b_ref, mod_ref, sc_ref):
        sc = _silu(c_ref[...])
        sc8 = jnp.broadcast_to(sc, (8, D_MODEL))
        mod_ref[...] = _nn(sc8, w_ref[...])[0:1] + b_ref[...]
        sc_ref[...] = sc

    return pl.pallas_call(body, name="adaln_mod", compiler_params=_params(),
                          out_shape=(SDS((1, 3 * D_MODEL), F32), SDS((1, D_MODEL), F32)))(c, w_mod, b_mod)


def _ln_proj(x, mod, norm_w, ws, cos_t, sin_t, conv_w8, alog_row, dtb_row, ts):
    s = x.shape[0]
    widths = [w.shape[1] for w in ws]

    def body(x_ref, mod_ref, nw_ref, cos_ref, sin_ref, cw_ref, al_ref, dtb_ref, wqkv, wz, wba, waq, wak, wav, waz,
             h_ref, oqkv, oz, oba, oq, ok, ov, oaz, q_ref, k_ref, v_ref, bg_ref, halo):
        n = pl.program_id(0)
        xt = x_ref[...]
        r = lax.rsqrt(jnp.mean(xt * xt, axis=-1, keepdims=True) + EPS)
        shift, scale = mod_ref[:, 0:D_MODEL], mod_ref[:, D_MODEL:2 * D_MODEL]
        h = ((xt * r) * nw_ref[...]) * (1.0 + scale) + shift
        hb = _bf(h)
        h_ref[...] = hb
        tq = jnp.dot(hb, waq[...], preferred_element_type=F32)
        tk = jnp.dot(hb, wak[...], preferred_element_type=F32)
        pre = jnp.dot(hb, wqkv[...], preferred_element_type=F32)
        ba = jnp.dot(hb, wba[...], preferred_element_type=F32)
        tv = jnp.dot(hb, wav[...], preferred_element_type=F32)
        tz = jnp.dot(hb, wz[...], preferred_element_type=F32)
        taz = jnp.dot(hb, waz[...], preferred_element_type=F32)
        cs, sn = cos_ref[...], sin_ref[...]
        for t, o_ref in ((tq, oq), (tk, ok)):
            for j in range(AT_PAIRS):
                tj = t[:, j * LANES:(j + 1) * LANES]
                o_ref[j] = tj * cs + _swap_half64(tj) * sn
        oqkv[...] = pre
        ext = jnp.concatenate([jnp.where(n == 0, 0.0, halo[...]), pre], axis=0)
        halo[...] = pre[ts - 8:ts]
        taps = _conv_taps(ext, ts)
        conv = taps[0] * cw_ref[0:1, :]
        for j in range(1, CONV_K):
            conv = conv + taps[j] * cw_ref[j:j + 1, :]
        for hd in range(DN_HEADS):
            cols = slice(hd * DN_DIM, (hd + 1) * DN_DIM)
            q_ref[:, cols] = _post_q(conv[:, hd * DN_DIM:(hd + 1) * DN_DIM])
            k_ref[:, cols] = _post_k(conv[:, DN_WIDTH + hd * DN_DIM:DN_WIDTH + (hd + 1) * DN_DIM])
            v_ref[:, cols] = _post_v(conv[:, 2 * DN_WIDTH + hd * DN_DIM:2 * DN_WIDTH + (hd + 1) * DN_DIM])
        oba[...] = ba
        bg = _beta_decay(ba, al_ref[...], dtb_ref[...])
        lane = lax.broadcasted_iota(jnp.int32, bg.shape, 1)
        run = pltpu.roll(_chunk_cumsum(bg), DN_HEADS, 1)
        bg_ref[...] = jnp.where((lane >= GC_LANE) & (lane < GC_LANE + DN_HEADS), run, bg)
        for j in range(AT_PAIRS):
            ov[j] = tv[:, j * LANES:(j + 1) * LANES]
        oz[...] = tz
        oaz[...] = taz

    tok = lambda w: pl.BlockSpec((ts, w), lambda i: (i, 0))
    full = lambda a: pl.BlockSpec(a.shape, lambda i: (0, 0))
    pairs = pl.BlockSpec((AT_PAIRS, ts, LANES), lambda i: (0, i, 0))
    return pl.pallas_call(
        body, name="ln_proj", grid=(s // ts,), compiler_params=_params("arbitrary"),
        in_specs=[tok(D_MODEL), full(mod), full(norm_w), tok(LANES), tok(LANES), full(conv_w8), full(alog_row),
                  full(dtb_row)] + [full(w) for w in ws],
        out_specs=(tok(D_MODEL), tok(widths[0]), tok(widths[1]), tok(widths[2]), pairs, pairs, pairs,
                   tok(widths[6]), tok(DN_WIDTH), tok(DN_WIDTH), tok(DN_WIDTH), tok(BA_PAD)),
        out_shape=(SDS((s, D_MODEL), BF16), SDS((s, widths[0]), F32), SDS((s, widths[1]), F32),
                   SDS((s, widths[2]), F32)) + (SDS((AT_PAIRS, s, LANES), F32),) * 3 + (SDS((s, widths[6]), F32),)
        + (SDS((s, DN_WIDTH), F32),) * 3 + (SDS((s, BA_PAD), F32),),
        scratch_shapes=[pltpu.VMEM((8, widths[0]), F32)],
    )(x, mod, norm_w, cos_t, sin_t, conv_w8, alog_row, dtb_row, *ws)


def _conv_taps(ext, rows):
    taps = []
    for j in range(CONV_K):
        sh = CONV_K - 1 - j
        rolled = pltpu.roll(ext, sh, 0) if sh else ext
        taps.append(rolled[8:8 + rows])
    return taps


def _dn_forward(q, k, v, bg):
    s = q.shape[0]
    tp = CH_UNROLL * CHUNK
    npass = s // tp
    hs = range(DN_HEADS)
    sl = [slice(h * DN_DIM, (h + 1) * DN_DIM) for h in hs]

    def body(q_ref, k_ref, v_ref, bg_ref, w_ref, qd_ref, kd_ref, p_ref, gl_ref, t_ref, o_ref, vn_ref, st_ref,
             state, u_s, w_s, qd_s, kd_s, p_s, gl_s):
        @pl.when(pl.program_id(0) == 0)
        def _():
            for ref in (state, u_s, w_s, qd_s, kd_s, p_s, gl_s):
                ref[...] = jnp.zeros_like(ref)

        def recurrence():
            for c in range(CH_UNROLL):
                rows = slice(c * CHUNK, (c + 1) * CHUNK)
                rows8 = slice(c * 8, (c + 1) * 8)
                srows = slice(c * DN_DIM, (c + 1) * DN_DIM)
                sf = [state[h] for h in hs]
                sb = [_bf(x) for x in sf]
                ws = [_nn(w_s[rows, cl], b) for cl, b in zip(sl, sb)]
                qs = [_nn(qd_s[rows, cl], b) for cl, b in zip(sl, sb)]
                yield
                vn = [u_s[rows, cl] - x for cl, x in zip(sl, ws)]
                vb = [_bf(x) for x in vn]
                kv = [_tn(kd_s[rows, cl], b) for cl, b in zip(sl, vb)]
                pv = [_nn(p_s[h, rows, :], b) for h, b in zip(hs, vb)]
                for h in hs:
                    state[h] = sf[h] * gl_s[rows8, sl[h]][0:1] + kv[h]
                for h in hs:
                    st_ref[srows, sl[h]] = sf[h]
                    vn_ref[rows, sl[h]] = vn[h]
                    o_ref[rows, sl[h]] = qs[h] + pv[h]
                yield

        steps = recurrence()

        where = [(slice(c * CHUNK, (c + 1) * CHUNK), slice(c * 8, (c + 1) * 8), h, sl[h])
                 for c in range(CH_UNROLL) for h in hs]
        bgs = [bg_ref[rows, :] for rows, _, _, _ in where]
        outs = _chunk_fwd([q_ref[rows, cl] for rows, _, _, cl in where], [k_ref[rows, cl] for rows, _, _, cl in where],
                          [v_ref[rows, cl] for rows, _, _, cl in where],
                          [b[:, h:h + 1] for b, (_, _, h, _) in zip(bgs, where)],
                          [b[:, GC_LANE + h:GC_LANE + h + 1] for b, (_, _, h, _) in zip(bgs, where)],
                          tick=lambda: next(steps, None))
        for _ in steps:
            pass
        for (rows, rows8, h, cl), (u, w, p, qd, kd, gl, t) in zip(where, outs):
            g8 = jnp.broadcast_to(gl, (8, DN_DIM))
            u_s[rows, cl] = u
            w_ref[rows, cl] = w
            w_s[rows, cl] = w
            qd_ref[rows, cl] = qd
            qd_s[rows, cl] = qd
            kd_ref[rows, cl] = kd
            kd_s[rows, cl] = kd
            p_ref[h, rows, :] = p
            p_s[h, rows, :] = p
            gl_ref[rows8, cl] = g8
            gl_s[rows8, cl] = g8
            t_ref[h, rows, :] = t

    cur = lambda i: jnp.minimum(i, npass - 1)
    done = lambda i: jnp.maximum(i - 1, 0)
    tokc = pl.BlockSpec((tp, DN_WIDTH), lambda i: (cur(i), 0))
    tokd = pl.BlockSpec((tp, DN_WIDTH), lambda i: (done(i), 0))
    sq = pl.BlockSpec((DN_HEADS, tp, CHUNK), lambda i: (0, cur(i), 0))
    return pl.pallas_call(
        body, name="dn_forward", grid=(npass + 1,), compiler_params=_params("arbitrary"),
        in_specs=[tokc] * 3 + [pl.BlockSpec((tp, BA_PAD), lambda i: (cur(i), 0))],
        out_specs=(tokc, tokc, tokc, sq, pl.BlockSpec((CH_UNROLL * 8, DN_WIDTH), lambda i: (cur(i), 0)), sq,
                   tokd, tokd, pl.BlockSpec((CH_UNROLL * DN_DIM, DN_WIDTH), lambda i: (done(i), 0))),
        out_shape=(SDS((s, DN_WIDTH), F32),) * 3 + (SDS((DN_HEADS, s, CHUNK), F32),
                                                     SDS((s // CHUNK * 8, DN_WIDTH), F32),
                                                     SDS((DN_HEADS, s, CHUNK), F32),
                                                     SDS((s, DN_WIDTH), F32), SDS((s, DN_WIDTH), F32),
                                                     SDS((s // CHUNK * DN_DIM, DN_WIDTH), F32)),
        scratch_shapes=[pltpu.VMEM((DN_HEADS, DN_DIM, DN_DIM), F32)] + [pltpu.VMEM((tp, DN_WIDTH), F32)] * 4
        + [pltpu.VMEM((DN_HEADS, tp, CHUNK), F32), pltpu.VMEM((CH_UNROLL * 8, DN_WIDTH), F32)],
    )(q, k, v, bg)


LOG2E, LN2 = 1.4426950408889634, 0.6931471805599453
MASKED = -1e30


def _band_bias():
    qi = lax.broadcasted_iota(jnp.int32, (Q_BLOCK, 2 * Q_BLOCK), 0)
    kj = lax.broadcasted_iota(jnp.int32, (Q_BLOCK, 2 * Q_BLOCK), 1)
    rel = Q_BLOCK + qi - kj
    return jnp.where((rel >= 0) & (rel <= W_SUB), 0.0, MASKED)


def _first_bias(first):
    kj = lax.broadcasted_iota(jnp.int32, (1, 2 * Q_BLOCK), 1)
    return jnp.where((kj < Q_BLOCK) & first, MASKED, 0.0)


def _attn_combo(c, d):
    if d == 1:
        qs = pl.multiple_of(c * Q_BLOCK, Q_BLOCK)
        return qs, pl.multiple_of(ATT_BLK - Q_BLOCK + c * Q_BLOCK, Q_BLOCK), c == 0
    r, m = c % d, c // d
    qs = r + (d * Q_BLOCK) * m
    return qs, ATT_BLK + qs - d * Q_BLOCK, m == 0


def _rows(start, size, d):
    return pl.ds(pl.multiple_of(start, Q_BLOCK), size) if d == 1 else pl.ds(start, size, stride=d)


def _shift_in(ext, cur, n):
    @pl.when(n == 0)
    def _():
        ext[0:ATT_BLK, :] = jnp.zeros((ATT_BLK, LANES), F32)

    @pl.when(n > 0)
    def _():
        ext[0:ATT_BLK, :] = ext[ATT_BLK:2 * ATT_BLK, :]

    ext[ATT_BLK:2 * ATT_BLK, :] = cur


def _attn_fwd(qr, kr, vv):
    s = qr.shape[1]
    nblk = s // ATT_BLK
    scale = AT_DIM ** -0.5
    npat = len(DILATIONS)

    def body(q_ref, k_ref, v_ref, o_ref, lse_ref, kext, vext, o_p, l_p, bias_ref):
        n = pl.program_id(1)
        _shift_in(kext, k_ref[0], n)
        _shift_in(vext, v_ref[0], n)
        bias_ref[...] = _band_bias()
        lo = lax.broadcasted_iota(jnp.int32, (Q_BLOCK, LANES), 1) < AT_DIM
        for pi, d in enumerate(DILATIONS):
            def group(g, carry, pi=pi, d=d):
                cs = [_attn_combo(g * ATT_UNROLL + u, d) for u in range(ATT_UNROLL)]
                heads = [(i, sel) for i in range(ATT_UNROLL) for sel in (lo, ~lo)]
                band = bias_ref[...]
                bias = [band + _first_bias((n == 0) & m0) for _, _, m0 in cs]
                qb = [_bf(q_ref[0, _rows(qs, Q_BLOCK, d), :]) for qs, _, _ in cs]
                kk = [_bf(kext[_rows(ks, 2 * Q_BLOCK, d), :]) for _, ks, _ in cs]
                vb = [_bf(vext[_rows(ks, 2 * Q_BLOCK, d), :]) for _, ks, _ in cs]
                sc = [lax.dot_general(jnp.where(sel, qb[i], jnp.zeros_like(qb[i])), kk[i], _NT,
                                      preferred_element_type=F32) for i, sel in heads]
                sc = [x * (scale * LOG2E) + bias[i] for x, (i, _) in zip(sc, heads)]
                mx = [jnp.max(x, axis=-1, keepdims=True) for x in sc]
                pr = [jnp.exp2(x - m) for x, m in zip(sc, mx)]
                ls = [jnp.sum(x, axis=-1, keepdims=True) for x in pr]
                pv = [jnp.dot(_bf(x), vb[i], preferred_element_type=F32) for x, (i, _) in zip(pr, heads)]
                outs = [x / l for x, l in zip(pv, ls)]
                lses = [m * LN2 + jnp.log(l) for m, l in zip(mx, ls)]
                for i, (qs, _, _) in enumerate(cs):
                    o_p[pi, _rows(qs, Q_BLOCK, d), :] = jnp.where(lo, outs[2 * i], outs[2 * i + 1])
                    l_p[pi, _rows(qs, Q_BLOCK, d), :] = jnp.where(lo, lses[2 * i], lses[2 * i + 1])
                return carry

            lax.fori_loop(0, ATT_BLK // Q_BLOCK // ATT_UNROLL, group, 0)

        def merge(i, carry):
            rows = pl.ds(pl.multiple_of(i * 256, 256), 256)
            ls = [l_p[pi, rows, :] for pi in range(npat)]
            mx = jnp.maximum(jnp.maximum(ls[0], ls[1]), ls[2])
            es = [jnp.exp(l - mx) for l in ls]
            den = es[0] + es[1] + es[2]
            o_ref[0, rows, :] = (es[0] * o_p[0, rows, :] + es[1] * o_p[1, rows, :] + es[2] * o_p[2, rows, :]) / den
            lse_ref[0, rows, :] = mx + jnp.log(den)
            return carry

        lax.fori_loop(0, ATT_BLK // 256, merge, 0)

    blk = pl.BlockSpec((1, ATT_BLK, LANES), lambda j, n: (j, n, 0))
    return pl.pallas_call(
        body, name="attn_fwd", grid=(AT_PAIRS, nblk), compiler_params=_params("arbitrary", "arbitrary"),
        in_specs=[blk] * 3, out_specs=(blk, blk),
        out_shape=(SDS((AT_PAIRS, s, LANES), F32),) * 2,
        scratch_shapes=[pltpu.VMEM((2 * ATT_BLK, LANES), F32), pltpu.VMEM((2 * ATT_BLK, LANES), F32),
                        pltpu.VMEM((npat, ATT_BLK, LANES), F32), pltpu.VMEM((npat, ATT_BLK, LANES), F32),
                        pltpu.VMEM((Q_BLOCK, 2 * Q_BLOCK), F32)],
    )(qr, kr, vv)


def _out_loss(o_dn, z_dn, o_at, z_at, dnw, atw2, x, tgt, w_out, gate, fw, ts):
    s = x.shape[0]

    def body(odn, zdn, oat, zat, dnw_ref, atw_ref, x_ref, t_ref, w_ref, g_ref, fw_ref,
             dx2_ref, gw_ref, dfw_ref, dgate_ref, loss_ref, dodn, dzdn, doat, dzat, delta, ddnw, datw):
        @pl.when(pl.program_id(0) == 0)
        def _():
            for ref in (gw_ref, dfw_ref, dgate_ref, loss_ref, ddnw, datw):
                ref[...] = jnp.zeros_like(ref)

        parts, vjps = [], []
        for h in range(DN_HEADS):
            cols = slice(h * DN_DIM, (h + 1) * DN_DIM)
            y, vjp = jax.vjp(_gate_dn, odn[:, cols], zdn[:, cols], dnw_ref[...])
            parts.append(_bf(y))
            vjps.append(vjp)
        for j in range(AT_PAIRS):
            y, vjp = jax.vjp(functools.partial(_gate_at, head_sum=_d_head_sum), oat[j],
                             zat[:, j * LANES:(j + 1) * LANES], atw_ref[...])
            parts.append(_bf(y))
            vjps.append(vjp)
        catb = jnp.concatenate(parts, axis=1)
        wb = w_ref[...]
        gate, fwv = g_ref[...], fw_ref[...]
        mix = jnp.dot(catb, wb, preferred_element_type=F32)
        x2 = x_ref[...] + gate * mix
        r2 = lax.rsqrt(jnp.mean(x2 * x2, axis=-1, keepdims=True) + EPS)
        xn2 = x2 * r2
        err = xn2 * fwv - t_ref[...]
        row = jnp.sum(err * err, axis=-1, keepdims=True) * (1.0 / D_MODEL)
        loss_ref[...] += 0.5 * jnp.sum(row, axis=0, keepdims=True)
        dy = err * (1.0 / D_MODEL)
        dfw_ref[...] += jnp.sum(dy * xn2, axis=0, keepdims=True)
        dxn = dy * fwv
        dx2 = r2 * (dxn - xn2 * jnp.mean(dxn * xn2, axis=-1, keepdims=True))
        dx2_ref[...] = dx2
        dgate_ref[...] += jnp.sum(dx2 * mix, axis=0, keepdims=True)
        dmix = _bf(gate * dx2)
        dcat = lax.dot_general(dmix, wb, _NT, preferred_element_type=F32)
        gw_ref[...] += lax.dot_general(catb, dmix, _TN, preferred_element_type=F32)
        for h in range(DN_HEADS):
            cols = slice(h * DN_DIM, (h + 1) * DN_DIM)
            do, dz, dw = vjps[h](dcat[:, cols])
            dodn[:, cols] = do
            dzdn[:, cols] = _bf(dz)
            ddnw[...] += dw
        for j in range(AT_PAIRS):
            cols = slice(j * LANES, (j + 1) * LANES)
            do, dz, dw = vjps[DN_HEADS + j](dcat[:, DN_WIDTH + j * LANES:DN_WIDTH + (j + 1) * LANES])
            doat[j] = do
            dzat[:, cols] = _bf(dz)
            datw[...] += dw
            delta[j] = _head_sum(do * oat[j])

    tok = lambda w: pl.BlockSpec((ts, w), lambda i: (i, 0))
    full = lambda a: pl.BlockSpec(a.shape, lambda i: (0, 0))
    row = pl.BlockSpec((1, D_MODEL), lambda i: (0, 0))
    lrow = pl.BlockSpec((1, LANES), lambda i: (0, 0))
    pairs = pl.BlockSpec((AT_PAIRS, ts, LANES), lambda i: (0, i, 0))
    return pl.pallas_call(
        body, name="out_loss", grid=(s // ts,), compiler_params=_params("arbitrary"),
        in_specs=[tok(DN_WIDTH), tok(DN_WIDTH), pairs, tok(AT_WIDTH), full(dnw), full(atw2),
                  tok(D_MODEL), tok(D_MODEL), full(w_out), full(gate), full(fw)],
        out_specs=(tok(D_MODEL), pl.BlockSpec((D_MODEL, D_MODEL), lambda i: (0, 0)), row, row,
                   pl.BlockSpec((1, 1), lambda i: (0, 0)), tok(DN_WIDTH), tok(DN_WIDTH), pairs, tok(AT_WIDTH), pairs,
                   lrow, lrow),
        out_shape=(SDS((s, D_MODEL), F32), SDS((D_MODEL, D_MODEL), F32), SDS((1, D_MODEL), F32),
                   SDS((1, D_MODEL), F32), SDS((1, 1), F32), SDS((s, DN_WIDTH), F32), SDS((s, DN_WIDTH), BF16),
                   SDS((AT_PAIRS, s, LANES), F32), SDS((s, AT_WIDTH), BF16), SDS((AT_PAIRS, s, LANES), F32),
                   SDS((1, LANES), F32), SDS((1, LANES), F32)),
    )(o_dn, z_dn, o_at, z_at, dnw, atw2, x, tgt, w_out, gate, fw)


def _shift_acc(ext, n):
    @pl.when(n == 0)
    def _():
        ext[0:ATT_BLK, :] = jnp.zeros((ATT_BLK, LANES), F32)

    @pl.when(n > 0)
    def _():
        ext[0:ATT_BLK, :] = ext[ATT_BLK:2 * ATT_BLK, :]

    ext[ATT_BLK:2 * ATT_BLK, :] = jnp.zeros((ATT_BLK, LANES), F32)


def _attn_bwd(qr, kr, vv, do, lse, delta):
    s = qr.shape[1]
    nblk = s // ATT_BLK
    scale = AT_DIM ** -0.5

    def body(q_ref, k_ref, v_ref, do_ref, lse_ref, dl_ref, dq_ref, dk_ref, dv_ref, kext, vext, dkext, dvext,
             bias_ref):
        n = pl.program_id(1)
        _shift_in(kext, k_ref[0], n)
        _shift_in(vext, v_ref[0], n)
        _shift_acc(dkext, n)
        _shift_acc(dvext, n)
        bias_ref[...] = _band_bias()

        @pl.when(n < nblk)
        def _():
            dq_ref[0] = jnp.zeros((ATT_BLK, LANES), F32)
            lo = lax.broadcasted_iota(jnp.int32, (Q_BLOCK, LANES), 1) < AT_DIM
            for d in DILATIONS:
                def group(g, carry, d=d):
                    nu = ATT_UNROLL_BWD
                    cs = [_attn_combo(g * nu + u, d) for u in range(nu)]
                    heads = [(i, sel) for i in range(nu) for sel in (lo, ~lo)]
                    qrows = [_rows(qs, Q_BLOCK, d) for qs, _, _ in cs]
                    krows = [_rows(ks, 2 * Q_BLOCK, d) for _, ks, _ in cs]
                    band = bias_ref[...]
                    bias = [band + _first_bias((n == 0) & m0) for _, _, m0 in cs]
                    qb = [_bf(q_ref[0, r, :]) for r in qrows]
                    dob = [_bf(do_ref[0, r, :]) for r in qrows]
                    kk = [_bf(kext[r, :]) for r in krows]
                    vb = [_bf(vext[r, :]) for r in krows]
                    lse2 = [lse_ref[0, r, :] * LOG2E for r in qrows]
                    dl2 = [dl_ref[0, r, :] for r in qrows]
                    qm = [jnp.where(sel, qb[i], jnp.zeros_like(qb[i])) for i, sel in heads]
                    dom = [jnp.where(sel, dob[i], jnp.zeros_like(dob[i])) for i, sel in heads]
                    lse_c = [jnp.max(jnp.where(sel, lse2[i], -jnp.inf), axis=-1, keepdims=True) for i, sel in heads]
                    dl_c = [jnp.max(jnp.where(sel, dl2[i], -jnp.inf), axis=-1, keepdims=True) for i, sel in heads]
                    sc = [lax.dot_general(a, kk[i], _NT, preferred_element_type=F32) for a, (i, _) in zip(qm, heads)]
                    dp = [lax.dot_general(a, vb[i], _NT, preferred_element_type=F32) for a, (i, _) in zip(dom, heads)]
                    pr = [jnp.exp2(x * (scale * LOG2E) + bias[i] - l) for x, l, (i, _) in zip(sc, lse_c, heads)]
                    ds = [_bf(p * (x - dl) * scale) for p, x, dl in zip(pr, dp, dl_c)]
                    prb = [_bf(p) for p in pr]
                    dq = [jnp.dot(x, kk[i], preferred_element_type=F32) for x, (i, _) in zip(ds, heads)]
                    dk = [lax.dot_general(x, a, _TN, preferred_element_type=F32) for x, a in zip(ds, qm)]
                    dv = [lax.dot_general(x, a, _TN, preferred_element_type=F32) for x, a in zip(prb, dom)]
                    for i in range(nu):
                        dq_ref[0, qrows[i], :] += jnp.where(lo, dq[2 * i], dq[2 * i + 1])
                        dkext[krows[i], :] += dk[2 * i] + dk[2 * i + 1]
                        dvext[krows[i], :] += dv[2 * i] + dv[2 * i + 1]
                    return carry

                lax.fori_loop(0, ATT_BLK // Q_BLOCK // ATT_UNROLL_BWD, group, 0)

        dk_ref[0] = dkext[0:ATT_BLK, :]
        dv_ref[0] = dvext[0:ATT_BLK, :]

    cur = pl.BlockSpec((1, ATT_BLK, LANES), lambda j, n: (j, jnp.minimum(n, nblk - 1), 0))
    done = pl.BlockSpec((1, ATT_BLK, LANES), lambda j, n: (j, jnp.maximum(n - 1, 0), 0))
    return pl.pallas_call(
        body, name="attn_bwd", grid=(AT_PAIRS, nblk + 1), compiler_params=_params("arbitrary", "arbitrary"),
        in_specs=[cur] * 6, out_specs=(cur, done, done),
        out_shape=(SDS((AT_PAIRS, s, LANES), F32),) * 3,
        scratch_shapes=[pltpu.VMEM((2 * ATT_BLK, LANES), F32)] * 4 + [pltpu.VMEM((Q_BLOCK, 2 * Q_BLOCK), F32)],
    )(qr, kr, vv, do, lse, delta)


def _dn_backward(do, st, vn, w, qd, kd, p, gl, q, k, v, bg, t):
    s = do.shape[0]
    tp = CH_UNROLL * CHUNK
    npass = s // tp
    hs = range(DN_HEADS)
    sl = [slice(h * DN_DIM, (h + 1) * DN_DIM) for h in hs]

    def body(do_ref, st_ref, vn_ref, w_ref, qd_ref, kd_ref, p_ref, gl_ref, q_ref, k_ref, v_ref, bg_ref, t_ref,
             dq_ref, dk_ref, dv_ref, dbg_ref, dstate, du_s, dw_s, dqd_s, dkd_s, dp_s, dgl_s):
        @pl.when(pl.program_id(0) == 0)
        def _():
            for ref in (dstate, du_s, dw_s, dqd_s, dkd_s, dp_s, dgl_s):
                ref[...] = jnp.zeros_like(ref)

        where = [(slice(c * CHUNK, (c + 1) * CHUNK), slice(c * 8, (c + 1) * 8), h, sl[h])
                 for c in range(CH_UNROLL) for h in hs]
        cots = [(du_s[rows, cl], dw_s[rows, cl], dp_s[h, rows, :], dqd_s[rows, cl], dkd_s[rows, cl],
                 dgl_s[rows8, cl][0:1, 0:1]) for rows, rows8, h, cl in where]

        def recurrence():
            for c in reversed(range(CH_UNROLL)):
                rows = slice(c * CHUNK, (c + 1) * CHUNK)
                rows8 = slice(c * 8, (c + 1) * 8)
                srows = slice(c * DN_DIM, (c + 1) * DN_DIM)
                ds_ = [dstate[h] for h in hs]
                dsb = [_bf(x) for x in ds_]
                dob = [_bf(do_ref[rows, cl]) for cl in sl]
                pdo = [_tn(p_ref[h, rows, :], b) for h, b in zip(hs, dob)]
                qdo = [_tn(qd_ref[rows, cl], b) for cl, b in zip(sl, dob)]
                kds = [_nn(kd_ref[rows, cl], b) for cl, b in zip(sl, dsb)]
                yield
                dvn = [a + b for a, b in zip(kds, pdo)]
                dvb = [_bf(x) for x in dvn]
                wdv = [_tn(w_ref[rows, cl], b) for cl, b in zip(sl, dvb)]
                for h in hs:
                    dstate[h] = ds_[h] * gl_ref[rows8, sl[h]][0:1] + qdo[h] - wdv[h]
                sfs = [st_ref[srows, cl] for cl in sl]
                sbs = [_bf(x) for x in sfs]
                vnb = [_bf(vn_ref[rows, cl]) for cl in sl]
                for h in hs:
                    du_s[rows, sl[h]] = dvn[h]
                    dw_s[rows, sl[h]] = -_nt(dvb[h], sbs[h])
                    dqd_s[rows, sl[h]] = _nt(dob[h], sbs[h])
                    dkd_s[rows, sl[h]] = _nt(vnb[h], dsb[h])
                    dp_s[h, rows, :] = _nt(dob[h], vnb[h])
                    dgl = jnp.sum(jnp.sum(ds_[h] * sfs[h], axis=1, keepdims=True), axis=0, keepdims=True)
                    dgl_s[rows8, sl[h]] = jnp.broadcast_to(dgl, (8, DN_DIM))
                yield

        steps = recurrence()

        bgs = [bg_ref[rows, :] for rows, _, _, _ in where]
        outs = _chunk_bwd([q_ref[rows, cl] for rows, _, _, cl in where], [k_ref[rows, cl] for rows, _, _, cl in where],
                          [v_ref[rows, cl] for rows, _, _, cl in where],
                          [b[:, h:h + 1] for b, (_, _, h, _) in zip(bgs, where)],
                          [b[:, GC_LANE + h:GC_LANE + h + 1] for b, (_, _, h, _) in zip(bgs, where)],
                          [t_ref[h, rows, :] for rows, _, h, _ in where], cots, tick=lambda: next(steps, None))
        for _ in steps:
            pass
        lane = lax.broadcasted_iota(jnp.int32, (CHUNK, BA_PAD), 1)
        for c in range(CH_UNROLL):
            dbg = jnp.zeros((CHUNK, BA_PAD), F32)
            for (rows, _, h, cl), (dq, dk, dv, dbeta, dgc) in list(zip(where, outs))[c * DN_HEADS:(c + 1) * DN_HEADS]:
                dq_ref[rows, cl] = dq
                dk_ref[rows, cl] = dk
                dv_ref[rows, cl] = dv
                dbg = dbg + jnp.where(lane == h, dbeta, 0.0) + jnp.where(lane == GC_LANE + h, dgc, 0.0)
            dbg_ref[where[c * DN_HEADS][0], :] = dbg

    rec = lambda i: jnp.maximum(npass - 1 - i, 0)
    loc = lambda i: jnp.minimum(npass - i, npass - 1)
    tok_r = pl.BlockSpec((tp, DN_WIDTH), lambda i: (rec(i), 0))
    tok_l = pl.BlockSpec((tp, DN_WIDTH), lambda i: (loc(i), 0))
    sq_r = pl.BlockSpec((DN_HEADS, tp, CHUNK), lambda i: (0, rec(i), 0))
    sq_l = pl.BlockSpec((DN_HEADS, tp, CHUNK), lambda i: (0, loc(i), 0))
    ba_l = pl.BlockSpec((tp, BA_PAD), lambda i: (loc(i), 0))
    return pl.pallas_call(
        body, name="dn_backward", grid=(npass + 1,), compiler_params=_params("arbitrary"),
        in_specs=[tok_r, pl.BlockSpec((CH_UNROLL * DN_DIM, DN_WIDTH), lambda i: (rec(i), 0)), tok_r, tok_r, tok_r, tok_r,
                  sq_r, pl.BlockSpec((CH_UNROLL * 8, DN_WIDTH), lambda i: (rec(i), 0)),
                  tok_l, tok_l, tok_l, ba_l, sq_l],
        out_specs=(tok_l, tok_l, tok_l, ba_l),
        out_shape=(SDS((s, DN_WIDTH), F32),) * 3 + (SDS((s, BA_PAD), F32),),
        scratch_shapes=[pltpu.VMEM((DN_HEADS, DN_DIM, DN_DIM), F32)] + [pltpu.VMEM((tp, DN_WIDTH), F32)] * 4
        + [pltpu.VMEM((DN_HEADS, tp, CHUNK), F32), pltpu.VMEM((CH_UNROLL * 8, DN_WIDTH), F32)],
    )(do, st, vn, w, qd, kd, p, gl, q, k, v, bg, t)


def _dn_prep_bwd(qkv_pre, ba, dq, dk, dv, dbg, conv_w8, alog_row, dtb_row, hbf, dz_dn, ts):
    s = qkv_pre.shape[0]
    cw = 3 * DN_WIDTH
    nt = s // ts

    def body(pre_ref, ph_ref, nh_ref, ba_ref, dq_ref, dqh_ref, dk_ref, dkh_ref, dv_ref, dvh_ref, dbg_ref,
             cw_ref, al_ref, dtb_ref, h_ref, dz_ref, dpre_ref, dba_ref, dcw_ref, dal_ref, ddtb_ref,
             gqkv_ref, gz_ref, gba_ref):
        n = pl.program_id(0)

        @pl.when(n == 0)
        def _():
            gqkv_ref[...] = jnp.zeros_like(gqkv_ref)
            gz_ref[...] = jnp.zeros_like(gz_ref)
            gba_ref[...] = jnp.zeros_like(gba_ref)
            dcw_ref[...] = jnp.zeros_like(dcw_ref)
            dal_ref[...] = jnp.zeros_like(dal_ref)
            ddtb_ref[...] = jnp.zeros_like(ddtb_ref)

        hb = h_ref[...]
        gz_ref[...] += lax.dot_general(hb, dz_ref[...], _TN, preferred_element_type=F32)
        last = n == nt - 1
        prev = jnp.where(n == 0, 0.0, ph_ref[...])
        ext = jnp.concatenate([prev, pre_ref[...], nh_ref[...]], axis=0)
        taps = _conv_taps(ext, ts + 8)
        conv = taps[0] * cw_ref[0:1, :]
        for j in range(1, CONV_K):
            conv = conv + taps[j] * cw_ref[j:j + 1, :]

        def cot(main, halo, cols):
            return jnp.concatenate([main[:, cols], jnp.where(last, 0.0, halo[:, cols])], axis=0)

        rows = ts + 8
        for grp, (fn, mref, href) in enumerate(((_post_q, dq_ref, dqh_ref), (_post_k, dk_ref, dkh_ref),
                                                (_post_v, dv_ref, dvh_ref))):
            gcols = slice(grp * DN_WIDTH, (grp + 1) * DN_WIDTH)
            pieces = []
            for h in range(DN_HEADS):
                cols = slice(h * DN_DIM, (h + 1) * DN_DIM)
                c0 = grp * DN_WIDTH + h * DN_DIM
                _, vjp = jax.vjp(fn, conv[:, c0:c0 + DN_DIM])
                pieces.append(vjp(cot(mref, href, cols))[0])
            dconv = jnp.concatenate(pieces, axis=1)
            dpre = dconv[:ts] * cw_ref[CONV_K - 1:CONV_K, gcols]
            for j in range(CONV_K - 1):
                sh = CONV_K - 1 - j
                dpre = dpre + pltpu.roll(dconv, rows - sh, 0)[:ts] * cw_ref[j:j + 1, gcols]
            dpre_b = _bf(dpre)
            dpre_ref[:, gcols] = dpre_b
            gqkv_ref[:, gcols] += lax.dot_general(hb, dpre_b, _TN, preferred_element_type=F32)
            for j in range(CONV_K):
                dcw_ref[j:j + 1, gcols] += jnp.sum(dconv[:ts] * taps[j][:ts, gcols], axis=0, keepdims=True)

        dbg = dbg_ref[...]
        lane = lax.broadcasted_iota(jnp.int32, dbg.shape, 1)
        dg = pltpu.roll(_chunk_cumsum(dbg, reverse=True), BA_PAD - DN_HEADS, 1)
        cot_bg = jnp.where(lane < DN_HEADS, dbg, jnp.where(lane < GC_LANE, dg, 0.0))
        _, vjp = jax.vjp(_beta_decay, ba_ref[...], al_ref[...], dtb_ref[...])
        dba, dal, ddtb = vjp(cot_bg)
        dba_b = _bf(dba)
        dba_ref[...] = dba_b
        gba_ref[...] += lax.dot_general(hb, dba_b, _TN, preferred_element_type=F32)
        dal_ref[...] += dal
        ddtb_ref[...] += ddtb

    tok = lambda w: pl.BlockSpec((ts, w), lambda i: (i, 0))
    full = lambda a: pl.BlockSpec(a.shape, lambda i: (0, 0))
    prevh = lambda w: pl.BlockSpec((8, w), lambda i: (jnp.maximum(i * (ts // 8) - 1, 0), 0))
    nexth = lambda w: pl.BlockSpec((8, w), lambda i: (jnp.minimum((i + 1) * (ts // 8), s // 8 - 1), 0))
    row = pl.BlockSpec((1, LANES), lambda i: (0, 0))
    return pl.pallas_call(
        body, name="dn_prep_bwd", grid=(nt,), compiler_params=_params("arbitrary"),
        in_specs=[tok(cw), prevh(cw), nexth(cw), tok(BA_PAD),
                  tok(DN_WIDTH), nexth(DN_WIDTH), tok(DN_WIDTH), nexth(DN_WIDTH), tok(DN_WIDTH), nexth(DN_WIDTH),
                  tok(BA_PAD), full(conv_w8), full(alog_row), full(dtb_row), tok(D_MODEL), tok(DN_WIDTH)],
        out_specs=(tok(cw), tok(BA_PAD), pl.BlockSpec((8, cw), lambda i: (0, 0)), row, row)
        + tuple(pl.BlockSpec((D_MODEL, w), lambda i: (0, 0)) for w in (cw, DN_WIDTH, BA_PAD)),
        out_shape=(SDS((s, cw), BF16), SDS((s, BA_PAD), BF16), SDS((8, cw), F32), SDS((1, LANES), F32),
                   SDS((1, LANES), F32)) + tuple(SDS((D_MODEL, w), F32) for w in (cw, DN_WIDTH, BA_PAD)),
    )(qkv_pre, qkv_pre, qkv_pre, ba, dq, dq, dk, dk, dv, dv, dbg, conv_w8, alog_row, dtb_row, hbf, dz_dn)


def _dh_dx(dps, ws, x, mod, norm_w, dx2, ts):
    s = x.shape[0]
    widths = [w.shape[1] for w in ws]
    np_ = len(ws)

    def body(*refs):
        dp_refs, w_refs = refs[:np_], refs[np_:2 * np_]
        x_ref, mod_ref, nw_ref, dx2_ref, gx_ref, dshift, dscale, dnw = refs[2 * np_:]

        @pl.when(pl.program_id(0) == 0)
        def _():
            dshift[...] = jnp.zeros_like(dshift)
            dscale[...] = jnp.zeros_like(dscale)
            dnw[...] = jnp.zeros_like(dnw)

        dh = lax.dot_general(dp_refs[0][...], w_refs[0][...], _NT, preferred_element_type=F32)
        for a, b in zip(dp_refs[1:], w_refs[1:]):
            dh = dh + lax.dot_general(a[...], b[...], _NT, preferred_element_type=F32)
        xt = x_ref[...]
        r = lax.rsqrt(jnp.mean(xt * xt, axis=-1, keepdims=True) + EPS)
        xn = xt * r
        nw = nw_ref[...]
        sc1 = 1.0 + mod_ref[:, D_MODEL:2 * D_MODEL]
        dshift[...] += jnp.sum(dh, axis=0, keepdims=True)
        dscale[...] += jnp.sum(dh * (xn * nw), axis=0, keepdims=True)
        dnw[...] += jnp.sum(dh * sc1 * xn, axis=0, keepdims=True)
        dxn = dh * sc1 * nw
        gx_ref[...] = r * (dxn - xn * jnp.mean(dxn * xn, axis=-1, keepdims=True)) + dx2_ref[...]

    tok = lambda w: pl.BlockSpec((ts, w), lambda i: (i, 0))
    full = lambda a: pl.BlockSpec(a.shape, lambda i: (0, 0))
    row = pl.BlockSpec((1, D_MODEL), lambda i: (0, 0))
    return pl.pallas_call(
        body, name="dh_dx", grid=(s // ts,), compiler_params=_params("arbitrary"),
        in_specs=[tok(w) for w in widths] + [full(w) for w in ws] + [tok(D_MODEL), full(mod), full(norm_w),
                                                                    tok(D_MODEL)],
        out_specs=(tok(D_MODEL), row, row, row),
        out_shape=(SDS((s, D_MODEL), F32),) + (SDS((1, D_MODEL), F32),) * 3,
    )(*dps, *ws, x, mod, norm_w, dx2)


def _grad_w_in_at(h, dq, dk, dv, dz_at, cos_t, sin_t, ts):
    s = h.shape[0]

    def body(h_ref, q_ref, k_ref, v_ref, dz_ref, cos_ref, sin_ref, oq, ok, ov, gq, gk, gv, gz):
        @pl.when(pl.program_id(0) == 0)
        def _():
            for o in (gq, gk, gv, gz):
                o[...] = jnp.zeros_like(o)

        hb = h_ref[...]
        gz[...] += lax.dot_general(hb, dz_ref[...], _TN, preferred_element_type=F32)
        cs, sn = cos_ref[...], sin_ref[...]
        for j in range(AT_PAIRS):
            ov[:, j * LANES:(j + 1) * LANES] = _bf(v_ref[j])
        gv[...] += lax.dot_general(hb, ov[...], _TN, preferred_element_type=F32)
        for g_ref, o_ref, acc in ((q_ref, oq, gq), (k_ref, ok, gk)):
            for j in range(AT_PAIRS):
                g = g_ref[j]
                o_ref[:, j * LANES:(j + 1) * LANES] = _bf(g * cs + _swap_half64(g * sn))
            acc[...] += lax.dot_general(hb, o_ref[...], _TN, preferred_element_type=F32)

    tok = lambda w: pl.BlockSpec((ts, w), lambda i: (i, 0))
    pairs = pl.BlockSpec((AT_PAIRS, ts, LANES), lambda i: (0, i, 0))
    acc = pl.BlockSpec((D_MODEL, AT_WIDTH), lambda i: (0, 0))
    return pl.pallas_call(
        body, name="grad_w_in_at", grid=(s // ts,), compiler_params=_params("arbitrary"),
        in_specs=[tok(D_MODEL), pairs, pairs, pairs, tok(AT_WIDTH), tok(LANES), tok(LANES)],
        out_specs=(tok(AT_WIDTH),) * 3 + (acc,) * 4,
        out_shape=(SDS((s, AT_WIDTH), BF16),) * 3 + (SDS((D_MODEL, AT_WIDTH), F32),) * 4,
    )(h, dq, dk, dv, dz_at, cos_t, sin_t)


def _adamw_math(w, g, m, v):
    m = ADAM_B1 * m + (1.0 - ADAM_B1) * g
    v = ADAM_B2 * v + (1.0 - ADAM_B2) * (g * g)
    m_hat = m / (1.0 - ADAM_B1 ** ADAM_STEP)
    v_hat = v / (1.0 - ADAM_B2 ** ADAM_STEP)
    delta = -ADAM_LR * (m_hat / (jnp.sqrt(v_hat) + ADAM_EPS) + ADAM_WD * w)
    return delta, m, v


def _adamw(w, m, v, g, name, own=None):
    def body(w_ref, m_ref, v_ref, g_ref, *rest):
        g_out, d_out, m_out, v_out = rest[-4:]
        if own is None:
            g = g_ref[...]
        else:
            g = g_ref[0].astype(F32)
            for k in range(1, N_DEV):
                g = g + g_ref[k].astype(F32)
            g = g + rest[0][...].astype(F32)
        g_out[...] = g
        d_out[...], m_out[...], v_out[...] = _adamw_math(w_ref[...], g, m_ref[...], v_ref[...])

    args = (w, m, v, g) if own is None else (w, m, v, g, own)
    return pl.pallas_call(body, name=name, compiler_params=_params(),
                          out_shape=(SDS(w.shape, F32),) * 4)(*args)


def _adamw_w_mod(w, m, v, siluc_all, dmod_mine):
    def body(w_ref, m_ref, v_ref, sc_ref, dm_ref, g_out, d_out, m_out, v_out):
        g = _htn(sc_ref[...], dm_ref[...])
        g_out[...] = g
        d_out[...], m_out[...], v_out[...] = _adamw_math(w_ref[...], g, m_ref[...], v_ref[...])

    return pl.pallas_call(body, name="adamw_w_mod", compiler_params=_params(),
                          out_shape=(SDS(w.shape, F32),) * 4)(w, m, v, siluc_all, dmod_mine)


def _pack_sum(pack_all):
    def body(p_ref, o_ref):
        t = p_ref[0]
        for k in range(1, N_DEV):
            t = t + p_ref[k]
        o_ref[...] = t

    return pl.pallas_call(body, name="pack_sum", out_shape=SDS(pack_all.shape[1:], F32))(pack_all)


def _tile(s, want):
    t = min(want, s)
    assert s % t == 0
    return t


def _local_step(x, c, positions, w_mod_bf, b_mod, norm_w, w_in_bf, conv_w, a_log, dt_bias, dn_norm_w, at_norm_w,
                w_out_bf, final_norm_w, tgt):
    s = x.shape[0]
    o = [0]
    for wdt in IN_SPLITS:
        o.append(o[-1] + wdt)
    w_ba = jnp.pad(w_in_bf[:, o[2]:o[4]], ((0, 0), (0, BA_PAD - 2 * DN_HEADS)))
    ws = [w_in_bf[:, o[0]:o[1]], w_in_bf[:, o[1]:o[2]], w_ba, w_in_bf[:, o[4]:o[5]], w_in_bf[:, o[5]:o[6]],
          w_in_bf[:, o[6]:o[7]], w_in_bf[:, o[7]:o[8]]]
    conv_w8 = jnp.pad(conv_w, ((0, 8 - CONV_K), (0, 0)))
    alog_row = jnp.pad(a_log, ((0, 0), (DN_HEADS, BA_PAD - 2 * DN_HEADS)))
    dtb_row = jnp.pad(dt_bias, ((0, 0), (DN_HEADS, BA_PAD - 2 * DN_HEADS)))
    atw2 = jnp.concatenate([at_norm_w, at_norm_w], axis=1)

    half = AT_DIM // 2
    lane = jnp.arange(LANES)
    inv_freq = ROPE_THETA ** (-(lane % half).astype(F32) / half)
    ang = positions.astype(F32)[:, None] * inv_freq
    cos_t = jnp.cos(ang)
    sin_t = jnp.sin(ang) * jnp.where((lane // half) % 2 == 0, -1.0, 1.0)

    mod, siluc = _adaln_mod(c, w_mod_bf, b_mod)
    gate = mod[:, 2 * D_MODEL:]
    hbf, qkv_pre, z_dn, ba, qr, kr, vb, z_at, q, k, v, bg = _ln_proj(
        x, mod, norm_w, ws, cos_t, sin_t, conv_w8, alog_row, dtb_row, _tile(s, 256))
    w, qd, kd, p, gl, tinv, o_dn, vn, st = _dn_forward(q, k, v, bg)
    o_at, lse = _attn_fwd(qr, kr, vb)
    (dx2, gw_out, dfw, dgate, loss, do_dn, dz_dn, do_at, dz_at, delta, ddnw, datw) = _out_loss(
        o_dn, z_dn, o_at, z_at, dn_norm_w, atw2, x, tgt, w_out_bf, gate, final_norm_w, _tile(s, 512))

    daq, dak, dav, g_aq, g_ak, g_av, g_az = _grad_w_in_at(hbf, *_attn_bwd(qr, kr, vb, do_at, lse, delta), dz_at,
                                                           cos_t, sin_t, _tile(s, 512))
    dq, dk, dv, dbg = _dn_backward(do_dn, st, vn, w, qd, kd, p, gl, q, k, v, bg, tinv)
    dqkv, dba, dcw, dal, ddtb, g_qkv, g_z, g_ba = _dn_prep_bwd(qkv_pre, ba, dq, dk, dv, dbg, conv_w8, alog_row, dtb_row,
                                                               hbf, dz_dn, _tile(s, 512))
    dps = [dqkv, dz_dn, dba, daq, dak, dav, dz_at]
    gw_in = jnp.concatenate([g_qkv, g_z, g_ba[:, :2 * DN_HEADS], g_aq, g_ak, g_av, g_az], axis=1)
    small = dict(conv=dcw[:CONV_K], dgate=dgate, siluc=siluc, dfw=dfw, alog=dal, dtb=ddtb, dnn=ddnw, atn=datw)

    def input_grad(token):
        gx, dshift, dscale, dnw = _dh_dx(dps, ws, x, mod + token, norm_w, dx2, _tile(s, 512))
        return gx, jnp.concatenate([dshift, dscale, small["dgate"]], axis=1), dnw

    return loss, gw_in, gw_out, small, input_grad


def kernel(x, c, positions, w_mod, b_mod, norm_w, w_in, conv_w, a_log, dt_bias, dn_norm_w, at_norm_w, w_out, final_norm_w, loss_target, m_w_mod, m_b_mod, m_norm_w, m_w_in, m_conv_w, m_a_log, m_dt_bias, m_dn_norm_w, m_at_norm_w, m_w_out, m_final_norm_w, v_w_mod, v_b_mod, v_norm_w, v_w_in, v_conv_w, v_a_log, v_dt_bias, v_dn_norm_w, v_at_norm_w, v_w_out, v_final_norm_w):
    me = 4 * lax.axis_index("x") + 2 * lax.axis_index("y") + lax.axis_index("c")
    s = x.shape[1]

    g_mod, g_in, g_conv, g_out = _all_gather(
        [_bf(w_mod[0]), _bf(w_in[0]), conv_w[0], _bf(w_out[0])], "gather_weights")
    w_mod_bf = g_mod.transpose(1, 0, 2).reshape(D_MODEL, 3 * D_MODEL)
    w_in_bf = g_in.transpose(1, 0, 2).reshape(D_MODEL, IN_COLS)
    conv_full = g_conv.transpose(1, 0, 2).reshape(CONV_K, 3 * DN_WIDTH)
    w_out_bf = g_out.reshape(D_MODEL, D_MODEL)

    loss, gw_in, gw_out, small, input_grad = _local_step(
        x[0], c, positions[0], w_mod_bf, b_mod, norm_w, w_in_bf, conv_full, a_log, dt_bias, dn_norm_w, at_norm_w,
        w_out_bf, final_norm_w.reshape(1, D_MODEL), loss_target[0])

    gw_in_slabs = _bf(gw_in).reshape(D_MODEL, N_DEV, IN_SHARD).transpose(1, 0, 2)
    gw_out_slabs = _bf(gw_out).reshape(N_DEV, D_MODEL // N_DEV, D_MODEL)
    send_sems, recv_sems, srcs, lands, token = _scatter_start([gw_in_slabs, gw_out_slabs])
    gx, dmod, dnw = input_grad(token[0, 0])
    r_in, r_out = _scatter_wait(send_sems, recv_sems, srcs, lands, gx)
    own_in = lax.dynamic_index_in_dim(gw_in_slabs, me, 0, keepdims=False)
    own_out = lax.dynamic_index_in_dim(gw_out_slabs, me, 0, keepdims=False)

    pack = jnp.concatenate([small["conv"].reshape(1, -1), dmod, small["siluc"], dnw, small["dfw"],
                            small["alog"], small["dtb"], small["dnn"], small["atn"],
                            jnp.pad(loss, ((0, 0), (0, LANES - 1)))], axis=1).reshape(PK_ROWS, LANES)
    (pack_all,) = _exchange([pack], [False], "exchange_small")

    res = {}
    res["w_in"] = _adamw(w_in[0], m_w_in[0], v_w_in[0], r_in, "adamw_w_in", own=own_in)
    res["w_out"] = _adamw(w_out[0], m_w_out[0], v_w_out[0], r_out, "adamw_w_out", own=own_out)
    flat_all = pack_all.reshape(N_DEV, PK_END)
    dmod_mine = lax.dynamic_slice(flat_all, (0, PK_DMOD + me * (3 * D_MODEL // N_DEV)), (N_DEV, 3 * D_MODEL // N_DEV))
    res["w_mod"] = _adamw_w_mod(w_mod[0], m_w_mod[0], v_w_mod[0], flat_all[:, PK_SILUC:PK_DNW], dmod_mine)
    tot = _pack_sum(pack_all).reshape(1, PK_END)
    g_conv_full = tot[:, PK_CONV:PK_DMOD].reshape(CONV_K, 3 * DN_WIDTH)
    g_conv_mine = lax.dynamic_slice(g_conv_full, (0, me * (3 * DN_WIDTH // N_DEV)), (CONV_K, 3 * DN_WIDTH // N_DEV))
    res["conv_w"] = _adamw(conv_w[0], m_conv_w[0], v_conv_w[0], g_conv_mine, "adamw_conv_w")
    res["b_mod"] = _adamw(b_mod, m_b_mod, v_b_mod, tot[:, PK_DMOD:PK_SILUC], "adamw_b_mod")
    res["norm_w"] = _adamw(norm_w, m_norm_w, v_norm_w, tot[:, PK_DNW:PK_DFW], "adamw_norm_w")
    res["a_log"] = _adamw(a_log, m_a_log, v_a_log, tot[:, PK_ALOG + DN_HEADS:PK_ALOG + 2 * DN_HEADS], "adamw_a_log")
    res["dt_bias"] = _adamw(dt_bias, m_dt_bias, v_dt_bias, tot[:, PK_DTB + DN_HEADS:PK_DTB + 2 * DN_HEADS],
                            "adamw_dt_bias")
    res["dn_norm_w"] = _adamw(dn_norm_w, m_dn_norm_w, v_dn_norm_w, tot[:, PK_DNN:PK_ATN], "adamw_dn_norm_w")
    g_atn = tot[:, PK_ATN:PK_ATN + AT_DIM] + tot[:, PK_ATN + AT_DIM:PK_LOSS]
    res["at_norm_w"] = _adamw(at_norm_w, m_at_norm_w, v_at_norm_w, g_atn, "adamw_at_norm_w")
    fin = _adamw(final_norm_w.reshape(1, D_MODEL), m_final_norm_w.reshape(1, D_MODEL),
                 v_final_norm_w.reshape(1, D_MODEL), tot[:, PK_DFW:PK_ALOG], "adamw_final_norm_w")
    res["final_norm_w"] = tuple(a.reshape(D_MODEL) for a in fin)

    lead = ("w_mod", "w_in", "conv_w", "w_out")
    names = ("w_mod", "b_mod", "norm_w", "w_in", "conv_w", "a_log", "dt_bias", "dn_norm_w", "at_norm_w", "w_out",
             "final_norm_w")
    out = [tot[0, PK_LOSS], gx.reshape(1, s, D_MODEL)]
    for kind in range(4):
        for nm in names:
            a = res[nm][kind]
            out.append(a[None] if nm in lead else a)
    return tuple(out)
```

```python
import functools

import jax
import jax.numpy as jnp
from jax import lax
from jax.experimental import pallas as pl
from jax.experimental.pallas import tpu as pltpu

F32, BF16 = jnp.float32, jnp.bfloat16
HI = lax.Precision.HIGHEST
SDS = jax.ShapeDtypeStruct

D_MODEL = 1024
DN_HEADS, DN_DIM, DN_WIDTH = 4, 128, 512
AT_HEADS, AT_DIM, AT_WIDTH = 8, 64, 512
CONV_K = 4
CHUNK = 64
Q_BLOCK = 128
W_SUB = 128
DILATIONS = (1, 4, 16)
AT_PAIRS = 4
ATT_BLK = Q_BLOCK * max(DILATIONS)
ATT_UNROLL, ATT_UNROLL_BWD = 8, 4
CH_UNROLL, CH_UNROLL_BWD = 4, 8
ROPE_THETA = 10000.0
EPS = 1e-6
N_DEV = 8
LANES = 128
BA_PAD = 128
IN_SPLITS = (1536, 512, 4, 4, 512, 512, 512, 512)
IN_COLS = sum(IN_SPLITS)
IN_SHARD = IN_COLS // N_DEV
VMEM_LIMIT = 58 * 2 ** 20

ADAM_LR, ADAM_B1, ADAM_B2, ADAM_EPS, ADAM_WD, ADAM_STEP = 0.001, 0.9, 0.999, 1e-08, 0.01, 10

PK_CONV, PK_DMOD, PK_SILUC, PK_DNW, PK_DFW, PK_ALOG, PK_DTB, PK_DNN, PK_ATN, PK_LOSS, PK_END = (
    0, 6144, 9216, 10240, 11264, 12288, 12416, 12544, 12672, 12800, 12928)
PK_ROWS = PK_END // LANES

_NT = (((1,), (1,)), ((), ()))
_TN = (((0,), (0,)), ((), ()))


def _params(*sem):
    return pltpu.CompilerParams(dimension_semantics=sem or None, vmem_limit_bytes=VMEM_LIMIT)


def _bf(x):
    return x.astype(BF16)


def _nn(a, b):
    return jnp.dot(_bf(a), _bf(b), preferred_element_type=F32)


def _nt(a, b):
    return lax.dot_general(_bf(a), _bf(b), _NT, preferred_element_type=F32)


def _tn(a, b):
    return lax.dot_general(_bf(a), _bf(b), _TN, preferred_element_type=F32)


def _htn(a, b):
    return lax.dot_general(a, b, _TN, precision=HI, preferred_element_type=F32)


def _head_sum(x):
    r = lax.broadcasted_iota(jnp.int32, (LANES, LANES), 0)
    c = lax.broadcasted_iota(jnp.int32, (LANES, LANES), 1)
    same = jnp.where((r // AT_DIM) == (c // AT_DIM), 1.0, 0.0).astype(BF16)
    hi, lo = _hl(x)
    return jnp.dot(hi, same, preferred_element_type=F32) + jnp.dot(lo, same, preferred_element_type=F32)


@jax.custom_vjp
def _d_head_sum(x):
    return _head_sum(x)


_d_head_sum.defvjp(lambda x: (_head_sum(x), None), lambda _, g: (_head_sum(g),))


def _silu(x):
    return x * jax.nn.sigmoid(x)


def _softplus(x):
    return jnp.maximum(x, 0.0) + jnp.log(1.0 + jnp.exp(-jnp.abs(x)))


def _l2n(x):
    return x * lax.rsqrt(jnp.sum(x * x, axis=-1, keepdims=True) + EPS)


def _post_q(x):
    return _l2n(_silu(x)) * (DN_DIM ** -0.5)


def _post_k(x):
    return _l2n(_silu(x))


def _post_v(x):
    return _silu(x)


def _beta_decay(ba, alog_row, dtb_row):
    lane = lax.broadcasted_iota(jnp.int32, ba.shape, 1)
    return jnp.where(lane < DN_HEADS, jax.nn.sigmoid(ba), -jnp.exp(alog_row) * _softplus(ba + dtb_row))


def _gate_dn(o, z, w):
    return (o * lax.rsqrt(jnp.mean(o * o, axis=-1, keepdims=True) + EPS)) * w * _silu(z)


def _gate_at(o, z, w2, head_sum):
    ms = head_sum(o * o) * (1.0 / AT_DIM)
    return (o * lax.rsqrt(ms + EPS)) * w2 * _silu(z)


def _swap_half64(x):
    lane = lax.broadcasted_iota(jnp.int32, x.shape, 1)
    return jnp.where((lane & (AT_DIM - 1)) < AT_DIM // 2, pltpu.roll(x, LANES - AT_DIM // 2, 1),
                     pltpu.roll(x, AT_DIM // 2, 1))


_NN = (((1,), (0,)), ((), ()))


def _hl(a):
    hi = a.astype(BF16)
    return hi, (a - hi.astype(F32)).astype(BF16)


def _mm3(a, b, dims=_NN):
    (ah, al), (bh, bl) = a, b
    f = lambda x, y: lax.dot_general(x, y, dims, preferred_element_type=F32)
    return f(ah, bh) + (f(ah, bl) + f(al, bh))


def _chunk_masks():
    r = lax.broadcasted_iota(jnp.int32, (CHUNK, CHUNK), 0)
    c = lax.broadcasted_iota(jnp.int32, (CHUNK, CHUNK), 1)
    return r >= c, r > c, (r == c).astype(F32), (r // 16) == (c // 16)


def _tri_inv(mats, tick=lambda: None):
    _, _, eye, blk = _chunk_masks()
    dg = [jnp.where(blk, a, 0.0) for a in mats]
    lo = [jnp.where(blk, 0.0, a) for a in mats]
    sdg = [_hl(x) for x in dg]
    d2 = [_mm3(s, s) for s in sdg]
    tick()
    sd2 = [_hl(x) for x in d2]
    d4 = [_mm3(s, s) for s in sd2]
    tick()
    sd4 = [_hl(x) for x in d4]
    d8 = [_mm3(s, s) for s in sd4]
    tick()
    p1 = [_mm3(_hl(eye - a), _hl(eye + b)) for a, b in zip(dg, d2)]
    tick()
    p2 = [_mm3(_hl(a), _hl(eye + b)) for a, b in zip(p1, d4)]
    tick()
    dinv = [_mm3(_hl(a), _hl(eye + b)) for a, b in zip(p2, d8)]
    tick()
    sdinv = [_hl(x) for x in dinv]
    n1 = [_mm3(s, _hl(b)) for s, b in zip(sdinv, lo)]
    tick()
    sn1 = [_hl(x) for x in n1]
    n2 = [_mm3(s, s) for s in sn1]
    tick()
    q1 = [_mm3(_hl(eye - a), _hl(eye + b)) for a, b in zip(n1, n2)]
    return [_mm3(_hl(a), s) for a, s in zip(q1, sdinv)]


def _chunk_common(qs, ks, vs, betas, gcs):
    tril, _, _, _ = _chunk_masks()
    out = []
    for q, k, v, beta, gc in zip(qs, ks, vs, betas, gcs):
        gb = jnp.broadcast_to(gc, (CHUNK, DN_DIM))
        gt = gb.T[:CHUNK, :]
        gam = jnp.where(tril, jnp.exp(jnp.where(tril, gb[:, :CHUNK] - gt, 0.0)), 0.0)
        last = gb[CHUNK - 1:CHUNK, :]
        eg, e2 = jnp.exp(gb), jnp.exp(last - gb)
        kb, vb = k * beta, v * beta
        out.append(dict(gam=gam, eg=eg, e2=e2, gl=jnp.exp(last[:, 0:1]), kb=kb, vb=vb, kbg=kb * eg,
                        m=_nt(kb, k), qk=_nt(q, k)))
    return out


def _chunk_fwd(qs, ks, vs, betas, gcs, tick=lambda: None):
    tril, strict, _, _ = _chunk_masks()
    cm = _chunk_common(qs, ks, vs, betas, gcs)
    ts = _tri_inv([jnp.where(strict, c["m"] * c["gam"], 0.0) for c in cm], tick)
    outs = []
    for q, k, c, t in zip(qs, ks, cm, ts):
        uw = _nn(t, jnp.concatenate([c["vb"], c["kbg"]], axis=1))
        p = jnp.where(tril, c["qk"] * c["gam"], 0.0)
        outs.append((uw[:, :DN_DIM], uw[:, DN_DIM:], p, q * c["eg"], k * c["e2"], c["gl"], t.T))
    return outs


def _chunk_bwd(qs, ks, vs, betas, gcs, ts, cots, tick=lambda: None):
    tril, strict, _, _ = _chunk_masks()
    cm = _chunk_common(qs, ks, vs, betas, gcs)
    tick()
    row = lax.broadcasted_iota(jnp.int32, (CHUNK, 1), 0)
    ones = jnp.ones((CHUNK, DN_DIM), BF16)
    rs = lambda x: jnp.sum(x, axis=-1, keepdims=True)
    tts = [_bf(t) for t in ts]
    duw = [_bf(jnp.concatenate([ct[0], ct[1]], axis=1)) for ct in cots]
    dts = [_nt(a, jnp.concatenate([c["vb"], c["kbg"]], axis=1)) for a, c in zip(duw, cm)]
    tick()
    xs = [_nn(t, d) for t, d in zip(tts, dts)]
    tick()
    das = [jnp.where(strict, -_nn(x, t), 0.0) for x, t in zip(xs, tts)]
    dvks = [_nn(t, a) for t, a in zip(tts, duw)]
    tick()
    outs = []
    every = max(1, len(qs) // 5)
    for idx, (q, k, v, beta, c, ct, da, dvk) in enumerate(zip(qs, ks, vs, betas, cm, cots, das, dvks)):
        if idx and idx % every == 0:
            tick()
        _, _, dp, dqd, dkd, dgl = ct
        dvb, dkbg = dvk[:, :DN_DIM], dvk[:, DN_DIM:]
        dm = da * c["gam"]
        dqk = jnp.where(tril, dp, 0.0) * c["gam"]
        e = dm * c["m"] + dqk * c["qk"]
        dmq = jnp.concatenate([dm, dqk], axis=0)
        r1 = _nn(dmq, k)
        dkb = r1[:CHUNK] + dkbg * c["eg"]
        dq = r1[CHUNK:] + dqd * c["eg"]
        dk = _tn(dmq, jnp.concatenate([c["kb"], q], axis=0)) + dkd * c["e2"] + dkb * beta
        dbeta = rs(dkb * k + dvb * v)
        eh, el = _hl(e)
        colsum = (lax.dot_general(eh, ones, _TN, preferred_element_type=F32)
                  + lax.dot_general(el, ones, _TN, preferred_element_type=F32))[:, 0:1]
        pkd = dkd * (k * c["e2"])
        dgc = rs(e) - colsum + rs(dqd * q * c["eg"] + dkbg * c["kbg"] - pkd)
        tail = rs(jnp.sum(pkd, axis=0, keepdims=True)) + dgl * c["gl"]
        dgc = dgc + jnp.where(row == CHUNK - 1, tail, 0.0)
        outs.append((dq, dk, dvb * beta, dbeta, dgc))
    return outs


def _chunk_cumsum(x, reverse=False):
    n = x.shape[0]
    pos = lax.broadcasted_iota(jnp.int32, x.shape, 0) & (CHUNK - 1)
    sh = 1
    while sh < CHUNK:
        if reverse:
            x = x + jnp.where(pos < CHUNK - sh, pltpu.roll(x, n - sh, 0), 0.0)
        else:
            x = x + jnp.where(pos >= sh, pltpu.roll(x, sh, 0), 0.0)
        sh *= 2
    return x


GC_LANE = 2 * DN_HEADS


def _exchange(arrays, scatter, name):
    n = len(arrays)
    out_shapes = []
    for a, sc in zip(arrays, scatter):
        out_shapes.append(SDS(a.shape if sc else (N_DEV,) + a.shape, a.dtype))

    def body(*refs):
        ins, outs = refs[:n], refs[n:2 * n]
        send_sems, recv_sems, loc_sems = refs[2 * n:]
        x, y, c = lax.axis_index("x"), lax.axis_index("y"), lax.axis_index("c")
        me = 4 * x + 2 * y + c
        local, remote = [], []
        for i in range(n):
            src = ins[i].at[me] if scatter[i] else ins[i]
            cp = pltpu.make_async_copy(src, outs[i].at[me], loc_sems.at[i])
            cp.start()
            local.append(cp)
        for dlt in range(1, N_DEV):
            px = 1 - x if dlt & 4 else x
            py = 1 - y if dlt & 2 else y
            pc = 1 - c if dlt & 1 else c
            peer = 4 * px + 2 * py + pc
            for i in range(n):
                src = ins[i].at[peer] if scatter[i] else ins[i]
                cp = pltpu.make_async_remote_copy(
                    src_ref=src, dst_ref=outs[i].at[me],
                    send_sem=send_sems.at[i, dlt - 1], recv_sem=recv_sems.at[i, dlt - 1],
                    device_id=(px, py, pc), device_id_type=pl.DeviceIdType.MESH)
                cp.start()
                arrive = pltpu.make_async_remote_copy(
                    src_ref=src, dst_ref=outs[i].at[peer],
                    send_sem=send_sems.at[i, dlt - 1], recv_sem=recv_sems.at[i, dlt - 1],
                    device_id=(px, py, pc), device_id_type=pl.DeviceIdType.MESH)
                remote.append((cp, arrive))
        for cp, arrive in remote:
            cp.wait_send()
            arrive.wait_recv()
        for cp in local:
            cp.wait()

    any_spec = pl.BlockSpec(memory_space=pl.ANY)
    return pl.pallas_call(
        body, name=name, out_shape=tuple(out_shapes),
        in_specs=[any_spec] * n, out_specs=tuple([any_spec] * n),
        scratch_shapes=[pltpu.SemaphoreType.DMA((n, N_DEV - 1)), pltpu.SemaphoreType.DMA((n, N_DEV - 1)),
                        pltpu.SemaphoreType.DMA((n,))],
    )(*arrays)


def _all_gather(arrays, name):
    n = len(arrays)

    def body(*refs):
        ins, outs = refs[:n], refs[n:2 * n]
        send_sems, recv_sems, loc_sems = refs[2 * n:]
        x, y, c = lax.axis_index("x"), lax.axis_index("y"), lax.axis_index("c")
        me, sibling = (x, y, c), (x, y, 1 - c)
        chips = [(1 - x, y), (x, 1 - y), (1 - x, 1 - y)]

        def copy(i, k, block, to, src=None):
            slot = outs[i].at[4 * block[0] + 2 * block[1] + block[2]]
            return pltpu.make_async_remote_copy(
                src_ref=slot if src is None else src, dst_ref=slot,
                send_sem=send_sems.at[i, k], recv_sem=recv_sems.at[i, k],
                device_id=to, device_id_type=pl.DeviceIdType.MESH)

        mine = [pltpu.make_async_copy(ins[i], outs[i].at[4 * x + 2 * y + c], loc_sems.at[i]) for i in range(n)]
        for cp in mine:
            cp.start()
        first = []
        for i in range(n):
            first.append(copy(i, 0, me, sibling, src=ins[i]))
            first += [copy(i, 1 + j, me, (*chip, c), src=ins[i]) for j, chip in enumerate(chips)]
        for cp in first:
            cp.start()
        passed = []
        for j, chip in enumerate(chips):
            for i in range(n):
                copy(i, 1 + j, (*chip, c), me).wait_recv()
                fwd = copy(i, 4 + j, (*chip, c), sibling)
                fwd.start()
                passed.append(fwd)
        for i in range(n):
            copy(i, 0, sibling, me).wait_recv()
        for j, chip in enumerate(chips):
            for i in range(n):
                copy(i, 4 + j, (*chip, 1 - c), me).wait_recv()
        for cp in first + passed:
            cp.wait_send()
        for cp in mine:
            cp.wait()

    any_spec = pl.BlockSpec(memory_space=pl.ANY)
    return pl.pallas_call(
        body, name=name, out_shape=tuple(SDS((N_DEV,) + a.shape, a.dtype) for a in arrays),
        in_specs=[any_spec] * n, out_specs=tuple([any_spec] * n),
        scratch_shapes=[pltpu.SemaphoreType.DMA((n, N_DEV - 1)), pltpu.SemaphoreType.DMA((n, N_DEV - 1)),
                        pltpu.SemaphoreType.DMA((n,))],
    )(*arrays)


_HBM = pl.BlockSpec(memory_space=pltpu.HBM)
_SEM = pl.BlockSpec(memory_space=pltpu.SEMAPHORE)


def _peers(x, y, c):
    out = []
    for dlt in range(1, N_DEV):
        px = 1 - x if dlt & 4 else x
        py = 1 - y if dlt & 2 else y
        pc = 1 - c if dlt & 1 else c
        out.append((dlt, (px, py, pc), 4 * px + 2 * py + pc))
    return out


def _scatter_start(arrays):
    n = len(arrays)
    ns = n * (N_DEV - 1)

    def body(*refs):
        ins, lands = refs[:n], refs[n:2 * n]
        send_sems, recv_sems = refs[2 * n:2 * n + ns], refs[2 * n + ns:2 * n + 2 * ns]
        token = refs[-1]
        x, y, c = lax.axis_index("x"), lax.axis_index("y"), lax.axis_index("c")
        me = 4 * x + 2 * y + c
        for dlt, peer, pi in _peers(x, y, c):
            for i in range(n):
                k = i * (N_DEV - 1) + dlt - 1
                pltpu.make_async_remote_copy(
                    src_ref=ins[i].at[pi], dst_ref=lands[i].at[me], send_sem=send_sems[k], recv_sem=recv_sems[k],
                    device_id=peer, device_id_type=pl.DeviceIdType.MESH).start()
        token[...] = jnp.zeros_like(token)

    sem = pltpu.SemaphoreType.DMA(())
    thru = tuple(pltpu.HBM(a.shape, a.dtype) for a in arrays)
    hbm = lambda a: pltpu.with_memory_space_constraint(a, pltpu.HBM)
    outs = pl.pallas_call(
        body, name="scatter_start", out_shape=(sem,) * (2 * ns) + thru + thru + (SDS((8, LANES), F32),),
        in_specs=[_HBM] * (2 * n),
        out_specs=(_SEM,) * (2 * ns) + (_HBM,) * (2 * n) + (pl.BlockSpec(memory_space=pltpu.VMEM),),
        input_output_aliases={i: 2 * ns + i for i in range(2 * n)},
        compiler_params=pltpu.CompilerParams(has_side_effects=pltpu.SideEffectType.DATAFLOW_SIDE_EFFECTING),
    )(*[hbm(a) for a in arrays], *[hbm(jnp.zeros(a.shape, a.dtype)) for a in arrays])
    return outs[:ns], outs[ns:2 * ns], outs[2 * ns:2 * ns + n], outs[2 * ns + n:2 * ns + 2 * n], outs[-1]


def _scatter_wait(send_sems, recv_sems, srcs, lands, after):
    n = len(srcs)
    ns = n * (N_DEV - 1)

    def body(*refs):
        ins, lands_ = refs[:n], refs[n:2 * n]
        send, recv = refs[2 * n:2 * n + ns], refs[2 * n + ns:2 * n + 2 * ns]
        x, y, c = lax.axis_index("x"), lax.axis_index("y"), lax.axis_index("c")
        for dlt, peer, pi in _peers(x, y, c):
            for i in range(n):
                k = i * (N_DEV - 1) + dlt - 1
                cp = pltpu.make_async_remote_copy(
                    src_ref=ins[i].at[pi], dst_ref=lands_[i].at[pi], send_sem=send[k], recv_sem=recv[k],
                    device_id=peer, device_id_type=pl.DeviceIdType.MESH)
                cp.wait_send()
                cp.wait_recv()

    thru = tuple(pltpu.HBM(a.shape, a.dtype) for a in srcs)
    outs = pl.pallas_call(
        body, name="scatter_wait", out_shape=thru + thru,
        in_specs=[_HBM] * (2 * n) + [_SEM] * (2 * ns) + [pl.BlockSpec(memory_space=pl.ANY)],
        out_specs=(_HBM,) * (2 * n), input_output_aliases={i: i for i in range(2 * n)},
        compiler_params=pltpu.CompilerParams(has_side_effects=pltpu.SideEffectType.DATAFLOW_SIDE_EFFECTING),
    )(*srcs, *lands, *send_sems, *recv_sems, after)
    return outs[n:]


def _adaln_mod(c, w_mod, b_mod):
    def body(c_ref, w_ref, b_ref, mod_ref, sc_ref):
        sc = _silu(c_ref[...])
        sc8 = jnp.broadcast_to(sc, (8, D_MODEL))
        mod_ref[...] = _nn(sc8, w_ref[...])[0:1] + b_ref[...]
        sc_ref[...] = sc

    return pl.pallas_call(body, name="adaln_mod", compiler_params=_params(),
                          out_shape=(SDS((1, 3 * D_MODEL), F32), SDS((1, D_MODEL), F32)))(c, w_mod, b_mod)


def _ln_proj(x, mod, norm_w, ws, cos_t, sin_t, conv_w8, alog_row, dtb_row, ts):
    s = x.shape[0]
    widths = [w.shape[1] for w in ws]

    def body(x_ref, mod_ref, nw_ref, cos_ref, sin_ref, cw_ref, al_ref, dtb_ref, wqkv, wz, wba, waq, wak, wav, waz,
             h_ref, oqkv, oz, oba, oq, ok, ov, oaz, q_ref, k_ref, v_ref, bg_ref, halo):
        n = pl.program_id(0)
        xt = x_ref[...]
        r = lax.rsqrt(jnp.mean(xt * xt, axis=-1, keepdims=True) + EPS)
        shift, scale = mod_ref[:, 0:D_MODEL], mod_ref[:, D_MODEL:2 * D_MODEL]
        h = ((xt * r) * nw_ref[...]) * (1.0 + scale) + shift
        hb = _bf(h)
        h_ref[...] = hb
        tq = jnp.dot(hb, waq[...], preferred_element_type=F32)
        tk = jnp.dot(hb, wak[...], preferred_element_type=F32)
        pre = jnp.dot(hb, wqkv[...], preferred_element_type=F32)
        ba = jnp.dot(hb, wba[...], preferred_element_type=F32)
        tv = jnp.dot(hb, wav[...], preferred_element_type=F32)
        tz = jnp.dot(hb, wz[...], preferred_element_type=F32)
        taz = jnp.dot(hb, waz[...], preferred_element_type=F32)
        cs, sn = cos_ref[...], sin_ref[...]
        for t, o_ref in ((tq, oq), (tk, ok)):
            for j in range(AT_PAIRS):
                tj = t[:, j * LANES:(j + 1) * LANES]
                o_ref[j] = tj * cs + _swap_half64(tj) * sn
        oqkv[...] = pre
        ext = jnp.concatenate([jnp.where(n == 0, 0.0, halo[...]), pre], axis=0)
        halo[...] = pre[ts - 8:ts]
        taps = _conv_taps(ext, ts)
        conv = taps[0] * cw_ref[0:1, :]
        for j in range(1, CONV_K):
            conv = conv + taps[j] * cw_ref[j:j + 1, :]
        for hd in range(DN_HEADS):
            cols = slice(hd * DN_DIM, (hd + 1) * DN_DIM)
            q_ref[:, cols] = _post_q(conv[:, hd * DN_DIM:(hd + 1) * DN_DIM])
            k_ref[:, cols] = _post_k(conv[:, DN_WIDTH + hd * DN_DIM:DN_WIDTH + (hd + 1) * DN_DIM])
            v_ref[:, cols] = _post_v(conv[:, 2 * DN_WIDTH + hd * DN_DIM:2 * DN_WIDTH + (hd + 1) * DN_DIM])
        oba[...] = ba
        bg = _beta_decay(ba, al_ref[...], dtb_ref[...])
        lane = lax.broadcasted_iota(jnp.int32, bg.shape, 1)
        run = pltpu.roll(_chunk_cumsum(bg), DN_HEADS, 1)
        bg_ref[...] = jnp.where((lane >= GC_LANE) & (lane < GC_LANE + DN_HEADS), run, bg)
        for j in range(AT_PAIRS):
            ov[j] = tv[:, j * LANES:(j + 1) * LANES]
        oz[...] = tz
        oaz[...] = taz

    tok = lambda w: pl.BlockSpec((ts, w), lambda i: (i, 0))
    full = lambda a: pl.BlockSpec(a.shape, lambda i: (0, 0))
    pairs = pl.BlockSpec((AT_PAIRS, ts, LANES), lambda i: (0, i, 0))
    return pl.pallas_call(
        body, name="ln_proj", grid=(s // ts,), compiler_params=_params("arbitrary"),
        in_specs=[tok(D_MODEL), full(mod), full(norm_w), tok(LANES), tok(LANES), full(conv_w8), full(alog_row),
                  full(dtb_row)] + [full(w) for w in ws],
        out_specs=(tok(D_MODEL), tok(widths[0]), tok(widths[1]), tok(widths[2]), pairs, pairs, pairs,
                   tok(widths[6]), tok(DN_WIDTH), tok(DN_WIDTH), tok(DN_WIDTH), tok(BA_PAD)),
        out_shape=(SDS((s, D_MODEL), BF16), SDS((s, widths[0]), F32), SDS((s, widths[1]), F32),
                   SDS((s, widths[2]), F32)) + (SDS((AT_PAIRS, s, LANES), F32),) * 3 + (SDS((s, widths[6]), F32),)
        + (SDS((s, DN_WIDTH), F32),) * 3 + (SDS((s, BA_PAD), F32),),
        scratch_shapes=[pltpu.VMEM((8, widths[0]), F32)],
    )(x, mod, norm_w, cos_t, sin_t, conv_w8, alog_row, dtb_row, *ws)


def _conv_taps(ext, rows):
    taps = []
    for j in range(CONV_K):
        sh = CONV_K - 1 - j
        rolled = pltpu.roll(ext, sh, 0) if sh else ext
        taps.append(rolled[8:8 + rows])
    return taps


def _dn_forward(q, k, v, bg):
    s = q.shape[0]
    tp = CH_UNROLL * CHUNK
    npass = s // tp
    hs = range(DN_HEADS)
    sl = [slice(h * DN_DIM, (h + 1) * DN_DIM) for h in hs]

    def body(q_ref, k_ref, v_ref, bg_ref, w_ref, qd_ref, kd_ref, p_ref, gl_ref, t_ref, o_ref, vn_ref, st_ref,
             state, u_s, w_s, qd_s, kd_s, p_s, gl_s):
        @pl.when(pl.program_id(0) == 0)
        def _():
            for ref in (state, u_s, w_s, qd_s, kd_s, p_s, gl_s):
                ref[...] = jnp.zeros_like(ref)

        def recurrence():
            for c in range(CH_UNROLL):
                rows = slice(c * CHUNK, (c + 1) * CHUNK)
                rows8 = slice(c * 8, (c + 1) * 8)
                srows = slice(c * DN_DIM, (c + 1) * DN_DIM)
                sf = [state[h] for h in hs]
                sb = [_bf(x) for x in sf]
                ws = [_nn(w_s[rows, cl], b) for cl, b in zip(sl, sb)]
                qs = [_nn(qd_s[rows, cl], b) for cl, b in zip(sl, sb)]
                yield
                vn = [u_s[rows, cl] - x for cl, x in zip(sl, ws)]
                vb = [_bf(x) for x in vn]
                kv = [_tn(kd_s[rows, cl], b) for cl, b in zip(sl, vb)]
                pv = [_nn(p_s[h, rows, :], b) for h, b in zip(hs, vb)]
                for h in hs:
                    state[h] = sf[h] * gl_s[rows8, sl[h]][0:1] + kv[h]
                for h in hs:
                    st_ref[srows, sl[h]] = sf[h]
                    vn_ref[rows, sl[h]] = vn[h]
                    o_ref[rows, sl[h]] = qs[h] + pv[h]
                yield

        steps = recurrence()

        where = [(slice(c * CHUNK, (c + 1) * CHUNK), slice(c * 8, (c + 1) * 8), h, sl[h])
                 for c in range(CH_UNROLL) for h in hs]
        bgs = [bg_ref[rows, :] for rows, _, _, _ in where]
        outs = _chunk_fwd([q_ref[rows, cl] for rows, _, _, cl in where], [k_ref[rows, cl] for rows, _, _, cl in where],
                          [v_ref[rows, cl] for rows, _, _, cl in where],
                          [b[:, h:h + 1] for b, (_, _, h, _) in zip(bgs, where)],
                          [b[:, GC_LANE + h:GC_LANE + h + 1] for b, (_, _, h, _) in zip(bgs, where)],
                          tick=lambda: next(steps, None))
        for _ in steps:
            pass
        for (rows, rows8, h, cl), (u, w, p, qd, kd, gl, t) in zip(where, outs):
            g8 = jnp.broadcast_to(gl, (8, DN_DIM))
            u_s[rows, cl] = u
            w_ref[rows, cl] = w
            w_s[rows, cl] = w
            qd_ref[rows, cl] = qd
            qd_s[rows, cl] = qd
            kd_ref[rows, cl] = kd
            kd_s[rows, cl] = kd
            p_ref[h, rows, :] = p
            p_s[h, rows, :] = p
            gl_ref[rows8, cl] = g8
            gl_s[rows8, cl] = g8
            t_ref[h, rows, :] = t

    cur = lambda i: jnp.minimum(i, npass - 1)
    done = lambda i: jnp.maximum(i - 1, 0)
    tokc = pl.BlockSpec((tp, DN_WIDTH), lambda i: (cur(i), 0))
    tokd = pl.BlockSpec((tp, DN_WIDTH), lambda i: (done(i), 0))
    sq = pl.BlockSpec((DN_HEADS, tp, CHUNK), lambda i: (0, cur(i), 0))
    return pl.pallas_call(
        body, name="dn_forward", grid=(npass + 1,), compiler_params=_params("arbitrary"),
        in_specs=[tokc] * 3 + [pl.BlockSpec((tp, BA_PAD), lambda i: (cur(i), 0))],
        out_specs=(tokc, tokc, tokc, sq, pl.BlockSpec((CH_UNROLL * 8, DN_WIDTH), lambda i: (cur(i), 0)), sq,
                   tokd, tokd, pl.BlockSpec((CH_UNROLL * DN_DIM, DN_WIDTH), lambda i: (done(i), 0))),
        out_shape=(SDS((s, DN_WIDTH), F32),) * 3 + (SDS((DN_HEADS, s, CHUNK), F32),
                                                     SDS((s // CHUNK * 8, DN_WIDTH), F32),
                                                     SDS((DN_HEADS, s, CHUNK), F32),
                                                     SDS((s, DN_WIDTH), F32), SDS((s, DN_WIDTH), F32),
                                                     SDS((s // CHUNK * DN_DIM, DN_WIDTH), F32)),
        scratch_shapes=[pltpu.VMEM((DN_HEADS, DN_DIM, DN_DIM), F32)] + [pltpu.VMEM((tp, DN_WIDTH), F32)] * 4
        + [pltpu.VMEM((DN_HEADS, tp, CHUNK), F32), pltpu.VMEM((CH_UNROLL * 8, DN_WIDTH), F32)],
    )(q, k, v, bg)


LOG2E, LN2 = 1.4426950408889634, 0.6931471805599453
MASKED = -1e30


def _band_bias():
    qi = lax.broadcasted_iota(jnp.int32, (Q_BLOCK, 2 * Q_BLOCK), 0)
    kj = lax.broadcasted_iota(jnp.int32, (Q_BLOCK, 2 * Q_BLOCK), 1)
    rel = Q_BLOCK + qi - kj
    return jnp.where((rel >= 0) & (rel <= W_SUB), 0.0, MASKED)


def _first_bias(first):
    kj = lax.broadcasted_iota(jnp.int32, (1, 2 * Q_BLOCK), 1)
    return jnp.where((kj < Q_BLOCK) & first, MASKED, 0.0)


def _attn_combo(c, d):
    if d == 1:
        qs = pl.multiple_of(c * Q_BLOCK, Q_BLOCK)
        return qs, pl.multiple_of(ATT_BLK - Q_BLOCK + c * Q_BLOCK, Q_BLOCK), c == 0
    r, m = c % d, c // d
    qs = r + (d * Q_BLOCK) * m
    return qs, ATT_BLK + qs - d * Q_BLOCK, m == 0


def _rows(start, size, d):
    return pl.ds(pl.multiple_of(start, Q_BLOCK), size) if d == 1 else pl.ds(start, size, stride=d)


def _shift_in(ext, cur, n):
    @pl.when(n == 0)
    def _():
        ext[0:ATT_BLK, :] = jnp.zeros((ATT_BLK, LANES), F32)

    @pl.when(n > 0)
    def _():
        ext[0:ATT_BLK, :] = ext[ATT_BLK:2 * ATT_BLK, :]

    ext[ATT_BLK:2 * ATT_BLK, :] = cur


def _attn_fwd(qr, kr, vv):
    s = qr.shape[1]
    nblk = s // ATT_BLK
    scale = AT_DIM ** -0.5
    npat = len(DILATIONS)

    def body(q_ref, k_ref, v_ref, o_ref, lse_ref, kext, vext, o_p, l_p, bias_ref):
        n = pl.program_id(1)
        _shift_in(kext, k_ref[0], n)
        _shift_in(vext, v_ref[0], n)
        bias_ref[...] = _band_bias()
        lo = lax.broadcasted_iota(jnp.int32, (Q_BLOCK, LANES), 1) < AT_DIM
        for pi, d in enumerate(DILATIONS):
            def group(g, carry, pi=pi, d=d):
                cs = [_attn_combo(g * ATT_UNROLL + u, d) for u in range(ATT_UNROLL)]
                heads = [(i, sel) for i in range(ATT_UNROLL) for sel in (lo, ~lo)]
                band = bias_ref[...]
                bias = [band + _first_bias((n == 0) & m0) for _, _, m0 in cs]
                qb = [_bf(q_ref[0, _rows(qs, Q_BLOCK, d), :]) for qs, _, _ in cs]
                kk = [_bf(kext[_rows(ks, 2 * Q_BLOCK, d), :]) for _, ks, _ in cs]
                vb = [_bf(vext[_rows(ks, 2 * Q_BLOCK, d), :]) for _, ks, _ in cs]
                sc = [lax.dot_general(jnp.where(sel, qb[i], jnp.zeros_like(qb[i])), kk[i], _NT,
                                      preferred_element_type=F32) for i, sel in heads]
                sc = [x * (scale * LOG2E) + bias[i] for x, (i, _) in zip(sc, heads)]
                mx = [jnp.max(x, axis=-1, keepdims=True) for x in sc]
                pr = [jnp.exp2(x - m) for x, m in zip(sc, mx)]
                ls = [jnp.sum(x, axis=-1, keepdims=True) for x in pr]
                pv = [jnp.dot(_bf(x), vb[i], preferred_element_type=F32) for x, (i, _) in zip(pr, heads)]
                outs = [x / l for x, l in zip(pv, ls)]
                lses = [m * LN2 + jnp.log(l) for m, l in zip(mx, ls)]
                for i, (qs, _, _) in enumerate(cs):
                    o_p[pi, _rows(qs, Q_BLOCK, d), :] = jnp.where(lo, outs[2 * i], outs[2 * i + 1])
                    l_p[pi, _rows(qs, Q_BLOCK, d), :] = jnp.where(lo, lses[2 * i], lses[2 * i + 1])
                return carry

            lax.fori_loop(0, ATT_BLK // Q_BLOCK // ATT_UNROLL, group, 0)

        def merge(i, carry):
            rows = pl.ds(pl.multiple_of(i * 256, 256), 256)
            ls = [l_p[pi, rows, :] for pi in range(npat)]
            mx = jnp.maximum(jnp.maximum(ls[0], ls[1]), ls[2])
            es = [jnp.exp(l - mx) for l in ls]
            den = es[0] + es[1] + es[2]
            o_ref[0, rows, :] = (es[0] * o_p[0, rows, :] + es[1] * o_p[1, rows, :] + es[2] * o_p[2, rows, :]) / den
            lse_ref[0, rows, :] = mx + jnp.log(den)
            return carry

        lax.fori_loop(0, ATT_BLK // 256, merge, 0)

    blk = pl.BlockSpec((1, ATT_BLK, LANES), lambda j, n: (j, n, 0))
    return pl.pallas_call(
        body, name="attn_fwd", grid=(AT_PAIRS, nblk), compiler_params=_params("arbitrary", "arbitrary"),
        in_specs=[blk] * 3, out_specs=(blk, blk),
        out_shape=(SDS((AT_PAIRS, s, LANES), F32),) * 2,
        scratch_shapes=[pltpu.VMEM((2 * ATT_BLK, LANES), F32), pltpu.VMEM((2 * ATT_BLK, LANES), F32),
                        pltpu.VMEM((npat, ATT_BLK, LANES), F32), pltpu.VMEM((npat, ATT_BLK, LANES), F32),
                        pltpu.VMEM((Q_BLOCK, 2 * Q_BLOCK), F32)],
    )(qr, kr, vv)


def _out_loss(o_dn, z_dn, o_at, z_at, dnw, atw2, x, tgt, w_out, gate, fw, ts):
    s = x.shape[0]

    def body(odn, zdn, oat, zat, dnw_ref, atw_ref, x_ref, t_ref, w_ref, g_ref, fw_ref,
             dx2_ref, gw_ref, dfw_ref, dgate_ref, loss_ref, dodn, dzdn, doat, dzat, delta, ddnw, datw):
        @pl.when(pl.program_id(0) == 0)
        def _():
            for ref in (gw_ref, dfw_ref, dgate_ref, loss_ref, ddnw, datw):
                ref[...] = jnp.zeros_like(ref)

        parts, vjps = [], []
        for h in range(DN_HEADS):
            cols = slice(h * DN_DIM, (h + 1) * DN_DIM)
            y, vjp = jax.vjp(_gate_dn, odn[:, cols], zdn[:, cols], dnw_ref[...])
            parts.append(_bf(y))
            vjps.append(vjp)
        for j in range(AT_PAIRS):
            y, vjp = jax.vjp(functools.partial(_gate_at, head_sum=_d_head_sum), oat[j],
                             zat[:, j * LANES:(j + 1) * LANES], atw_ref[...])
            parts.append(_bf(y))
            vjps.append(vjp)
        catb = jnp.concatenate(parts, axis=1)
        wb = w_ref[...]
        gate, fwv = g_ref[...], fw_ref[...]
        mix = jnp.dot(catb, wb, preferred_element_type=F32)
        x2 = x_ref[...] + gate * mix
        r2 = lax.rsqrt(jnp.mean(x2 * x2, axis=-1, keepdims=True) + EPS)
        xn2 = x2 * r2
        err = xn2 * fwv - t_ref[...]
        row = jnp.sum(err * err, axis=-1, keepdims=True) * (1.0 / D_MODEL)
        loss_ref[...] += 0.5 * jnp.sum(row, axis=0, keepdims=True)
        dy = err * (1.0 / D_MODEL)
        dfw_ref[...] += jnp.sum(dy * xn2, axis=0, keepdims=True)
        dxn = dy * fwv
        dx2 = r2 * (dxn - xn2 * jnp.mean(dxn * xn2, axis=-1, keepdims=True))
        dx2_ref[...] = dx2
        dgate_ref[...] += jnp.sum(dx2 * mix, axis=0, keepdims=True)
        dmix = _bf(gate * dx2)
        dcat = lax.dot_general(dmix, wb, _NT, preferred_element_type=F32)
        gw_ref[...] += lax.dot_general(catb, dmix, _TN, preferred_element_type=F32)
        for h in range(DN_HEADS):
            cols = slice(h * DN_DIM, (h + 1) * DN_DIM)
            do, dz, dw = vjps[h](dcat[:, cols])
            dodn[:, cols] = do
            dzdn[:, cols] = _bf(dz)
            ddnw[...] += dw
        for j in range(AT_PAIRS):
            cols = slice(j * LANES, (j + 1) * LANES)
            do, dz, dw = vjps[DN_HEADS + j](dcat[:, DN_WIDTH + j * LANES:DN_WIDTH + (j + 1) * LANES])
            doat[j] = do
            dzat[:, cols] = _bf(dz)
            datw[...] += dw
            delta[j] = _head_sum(do * oat[j])

    tok = lambda w: pl.BlockSpec((ts, w), lambda i: (i, 0))
    full = lambda a: pl.BlockSpec(a.shape, lambda i: (0, 0))
    row = pl.BlockSpec((1, D_MODEL), lambda i: (0, 0))
    lrow = pl.BlockSpec((1, LANES), lambda i: (0, 0))
    pairs = pl.BlockSpec((AT_PAIRS, ts, LANES), lambda i: (0, i, 0))
    return pl.pallas_call(
        body, name="out_loss", grid=(s // ts,), compiler_params=_params("arbitrary"),
        in_specs=[tok(DN_WIDTH), tok(DN_WIDTH), pairs, tok(AT_WIDTH), full(dnw), full(atw2),
                  tok(D_MODEL), tok(D_MODEL), full(w_out), full(gate), full(fw)],
        out_specs=(tok(D_MODEL), pl.BlockSpec((D_MODEL, D_MODEL), lambda i: (0, 0)), row, row,
                   pl.BlockSpec((1, 1), lambda i: (0, 0)), tok(DN_WIDTH), tok(DN_WIDTH), pairs, tok(AT_WIDTH), pairs,
                   lrow, lrow),
        out_shape=(SDS((s, D_MODEL), F32), SDS((D_MODEL, D_MODEL), F32), SDS((1, D_MODEL), F32),
                   SDS((1, D_MODEL), F32), SDS((1, 1), F32), SDS((s, DN_WIDTH), F32), SDS((s, DN_WIDTH), BF16),
                   SDS((AT_PAIRS, s, LANES), F32), SDS((s, AT_WIDTH), BF16), SDS((AT_PAIRS, s, LANES), F32),
                   SDS((1, LANES), F32), SDS((1, LANES), F32)),
    )(o_dn, z_dn, o_at, z_at, dnw, atw2, x, tgt, w_out, gate, fw)


def _shift_acc(ext, n):
    @pl.when(n == 0)
    def _():
        ext[0:ATT_BLK, :] = jnp.zeros((ATT_BLK, LANES), F32)

    @pl.when(n > 0)
    def _():
        ext[0:ATT_BLK, :] = ext[ATT_BLK:2 * ATT_BLK, :]

    ext[ATT_BLK:2 * ATT_BLK, :] = jnp.zeros((ATT_BLK, LANES), F32)


def _attn_bwd(qr, kr, vv, do, lse, delta):
    s = qr.shape[1]
    nblk = s // ATT_BLK
    scale = AT_DIM ** -0.5

    def body(q_ref, k_ref, v_ref, do_ref, lse_ref, dl_ref, dq_ref, dk_ref, dv_ref, kext, vext, dkext, dvext,
             bias_ref):
        n = pl.program_id(1)
        _shift_in(kext, k_ref[0], n)
        _shift_in(vext, v_ref[0], n)
        _shift_acc(dkext, n)
        _shift_acc(dvext, n)
        bias_ref[...] = _band_bias()

        @pl.when(n < nblk)
        def _():
            dq_ref[0] = jnp.zeros((ATT_BLK, LANES), F32)
            lo = lax.broadcasted_iota(jnp.int32, (Q_BLOCK, LANES), 1) < AT_DIM
            for d in DILATIONS:
                def group(g, carry, d=d):
                    nu = ATT_UNROLL_BWD
                    cs = [_attn_combo(g * nu + u, d) for u in range(nu)]
                    heads = [(i, sel) for i in range(nu) for sel in (lo, ~lo)]
                    qrows = [_rows(qs, Q_BLOCK, d) for qs, _, _ in cs]
                    krows = [_rows(ks, 2 * Q_BLOCK, d) for _, ks, _ in cs]
                    band = bias_ref[...]
                    bias = [band + _first_bias((n == 0) & m0) for _, _, m0 in cs]
                    qb = [_bf(q_ref[0, r, :]) for r in qrows]
                    dob = [_bf(do_ref[0, r, :]) for r in qrows]
                    kk = [_bf(kext[r, :]) for r in krows]
                    vb = [_bf(vext[r, :]) for r in krows]
                    lse2 = [lse_ref[0, r, :] * LOG2E for r in qrows]
                    dl2 = [dl_ref[0, r, :] for r in qrows]
                    qm = [jnp.where(sel, qb[i], jnp.zeros_like(qb[i])) for i, sel in heads]
                    dom = [jnp.where(sel, dob[i], jnp.zeros_like(dob[i])) for i, sel in heads]
                    lse_c = [jnp.max(jnp.where(sel, lse2[i], -jnp.inf), axis=-1, keepdims=True) for i, sel in heads]
                    dl_c = [jnp.max(jnp.where(sel, dl2[i], -jnp.inf), axis=-1, keepdims=True) for i, sel in heads]
                    sc = [lax.dot_general(a, kk[i], _NT, preferred_element_type=F32) for a, (i, _) in zip(qm, heads)]
                    dp = [lax.dot_general(a, vb[i], _NT, preferred_element_type=F32) for a, (i, _) in zip(dom, heads)]
                    pr = [jnp.exp2(x * (scale * LOG2E) + bias[i] - l) for x, l, (i, _) in zip(sc, lse_c, heads)]
                    ds = [_bf(p * (x - dl) * scale) for p, x, dl in zip(pr, dp, dl_c)]
                    prb = [_bf(p) for p in pr]
                    dq = [jnp.dot(x, kk[i], preferred_element_type=F32) for x, (i, _) in zip(ds, heads)]
                    dk = [lax.dot_general(x, a, _TN, preferred_element_type=F32) for x, a in zip(ds, qm)]
                    dv = [lax.dot_general(x, a, _TN, preferred_element_type=F32) for x, a in zip(prb, dom)]
                    for i in range(nu):
                        dq_ref[0, qrows[i], :] += jnp.where(lo, dq[2 * i], dq[2 * i + 1])
                        dkext[krows[i], :] += dk[2 * i] + dk[2 * i + 1]
                        dvext[krows[i], :] += dv[2 * i] + dv[2 * i + 1]
                    return carry

                lax.fori_loop(0, ATT_BLK // Q_BLOCK // ATT_UNROLL_BWD, group, 0)

        dk_ref[0] = dkext[0:ATT_BLK, :]
        dv_ref[0] = dvext[0:ATT_BLK, :]

    cur = pl.BlockSpec((1, ATT_BLK, LANES), lambda j, n: (j, jnp.minimum(n, nblk - 1), 0))
    done = pl.BlockSpec((1, ATT_BLK, LANES), lambda j, n: (j, jnp.maximum(n - 1, 0), 0))
    return pl.pallas_call(
        body, name="attn_bwd", grid=(AT_PAIRS, nblk + 1), compiler_params=_params("arbitrary", "arbitrary"),
        in_specs=[cur] * 6, out_specs=(cur, done, done),
        out_shape=(SDS((AT_PAIRS, s, LANES), F32),) * 3,
        scratch_shapes=[pltpu.VMEM((2 * ATT_BLK, LANES), F32)] * 4 + [pltpu.VMEM((Q_BLOCK, 2 * Q_BLOCK), F32)],
    )(qr, kr, vv, do, lse, delta)


def _dn_backward(do, st, vn, w, qd, kd, p, gl, q, k, v, bg, t):
    s = do.shape[0]
    nc = CH_UNROLL_BWD
    tp = nc * CHUNK
    npass = s // tp
    hs = range(DN_HEADS)
    sl = [slice(h * DN_DIM, (h + 1) * DN_DIM) for h in hs]

    def body(do_ref, st_ref, vn_ref, w_ref, qd_ref, kd_ref, p_ref, gl_ref, q_ref, k_ref, v_ref, bg_ref, t_ref,
             dq_ref, dk_ref, dv_ref, dbg_ref, dstate, du_s, dw_s, dqd_s, dkd_s, dp_s, dgl_s):
        @pl.when(pl.program_id(0) == 0)
        def _():
            for ref in (dstate, du_s, dw_s, dqd_s, dkd_s, dp_s, dgl_s):
                ref[...] = jnp.zeros_like(ref)

        where = [(slice(c * CHUNK, (c + 1) * CHUNK), slice(c * 8, (c + 1) * 8), h, sl[h])
                 for c in range(nc) for h in hs]
        cots = [(du_s[rows, cl], dw_s[rows, cl], dp_s[h, rows, :], dqd_s[rows, cl], dkd_s[rows, cl],
                 dgl_s[rows8, cl][0:1, 0:1]) for rows, rows8, h, cl in where]

        def recurrence():
            for c in reversed(range(nc)):
                rows = slice(c * CHUNK, (c + 1) * CHUNK)
                rows8 = slice(c * 8, (c + 1) * 8)
                srows = slice(c * DN_DIM, (c + 1) * DN_DIM)
                ds_ = [dstate[h] for h in hs]
                dsb = [_bf(x) for x in ds_]
                dob = [_bf(do_ref[rows, cl]) for cl in sl]
                pdo = [_tn(p_ref[h, rows, :], b) for h, b in zip(hs, dob)]
                qdo = [_tn(qd_ref[rows, cl], b) for cl, b in zip(sl, dob)]
                kds = [_nn(kd_ref[rows, cl], b) for cl, b in zip(sl, dsb)]
                yield
                dvn = [a + b for a, b in zip(kds, pdo)]
                dvb = [_bf(x) for x in dvn]
                wdv = [_tn(w_ref[rows, cl], b) for cl, b in zip(sl, dvb)]
                for h in hs:
                    dstate[h] = ds_[h] * gl_ref[rows8, sl[h]][0:1] + qdo[h] - wdv[h]
                sfs = [st_ref[srows, cl] for cl in sl]
                sbs = [_bf(x) for x in sfs]
                vnb = [_bf(vn_ref[rows, cl]) for cl in sl]
                for h in hs:
                    du_s[rows, sl[h]] = dvn[h]
                    dw_s[rows, sl[h]] = -_nt(dvb[h], sbs[h])
                    dqd_s[rows, sl[h]] = _nt(dob[h], sbs[h])
                    dkd_s[rows, sl[h]] = _nt(vnb[h], dsb[h])
                    dp_s[h, rows, :] = _nt(dob[h], vnb[h])
                    dgl = jnp.sum(jnp.sum(ds_[h] * sfs[h], axis=1, keepdims=True), axis=0, keepdims=True)
                    dgl_s[rows8, sl[h]] = jnp.broadcast_to(dgl, (8, DN_DIM))
                yield

        steps = recurrence()

        bgs = [bg_ref[rows, :] for rows, _, _, _ in where]
        outs = _chunk_bwd([q_ref[rows, cl] for rows, _, _, cl in where], [k_ref[rows, cl] for rows, _, _, cl in where],
                          [v_ref[rows, cl] for rows, _, _, cl in where],
                          [b[:, h:h + 1] for b, (_, _, h, _) in zip(bgs, where)],
                          [b[:, GC_LANE + h:GC_LANE + h + 1] for b, (_, _, h, _) in zip(bgs, where)],
                          [t_ref[h, rows, :] for rows, _, h, _ in where], cots, tick=lambda: next(steps, None))
        for _ in steps:
            pass
        lane = lax.broadcasted_iota(jnp.int32, (CHUNK, BA_PAD), 1)
        for c in range(nc):
            dbg = jnp.zeros((CHUNK, BA_PAD), F32)
            for (rows, _, h, cl), (dq, dk, dv, dbeta, dgc) in list(zip(where, outs))[c * DN_HEADS:(c + 1) * DN_HEADS]:
                dq_ref[rows, cl] = dq
                dk_ref[rows, cl] = dk
                dv_ref[rows, cl] = dv
                dbg = dbg + jnp.where(lane == h, dbeta, 0.0) + jnp.where(lane == GC_LANE + h, dgc, 0.0)
            dbg_ref[where[c * DN_HEADS][0], :] = dbg

    rec = lambda i: jnp.maximum(npass - 1 - i, 0)
    loc = lambda i: jnp.minimum(npass - i, npass - 1)
    tok_r = pl.BlockSpec((tp, DN_WIDTH), lambda i: (rec(i), 0))
    tok_l = pl.BlockSpec((tp, DN_WIDTH), lambda i: (loc(i), 0))
    sq_r = pl.BlockSpec((DN_HEADS, tp, CHUNK), lambda i: (0, rec(i), 0))
    sq_l = pl.BlockSpec((DN_HEADS, tp, CHUNK), lambda i: (0, loc(i), 0))
    ba_l = pl.BlockSpec((tp, BA_PAD), lambda i: (loc(i), 0))
    return pl.pallas_call(
        body, name="dn_backward", grid=(npass + 1,), compiler_params=_params("arbitrary"),
        in_specs=[tok_r, pl.BlockSpec((nc * DN_DIM, DN_WIDTH), lambda i: (rec(i), 0)), tok_r, tok_r, tok_r, tok_r,
                  sq_r, pl.BlockSpec((nc * 8, DN_WIDTH), lambda i: (rec(i), 0)),
                  tok_l, tok_l, tok_l, ba_l, sq_l],
        out_specs=(tok_l, tok_l, tok_l, ba_l),
        out_shape=(SDS((s, DN_WIDTH), F32),) * 3 + (SDS((s, BA_PAD), F32),),
        scratch_shapes=[pltpu.VMEM((DN_HEADS, DN_DIM, DN_DIM), F32)] + [pltpu.VMEM((tp, DN_WIDTH), F32)] * 4
        + [pltpu.VMEM((DN_HEADS, tp, CHUNK), F32), pltpu.VMEM((nc * 8, DN_WIDTH), F32)],
    )(do, st, vn, w, qd, kd, p, gl, q, k, v, bg, t)


def _dn_prep_bwd(qkv_pre, ba, dq, dk, dv, dbg, conv_w8, alog_row, dtb_row, hbf, dz_dn, ts):
    s = qkv_pre.shape[0]
    cw = 3 * DN_WIDTH
    nt = s // ts

    def body(pre_ref, ph_ref, nh_ref, ba_ref, dq_ref, dqh_ref, dk_ref, dkh_ref, dv_ref, dvh_ref, dbg_ref,
             cw_ref, al_ref, dtb_ref, h_ref, dz_ref, dpre_ref, dba_ref, dcw_ref, dal_ref, ddtb_ref,
             gqkv_ref, gz_ref, gba_ref):
        n = pl.program_id(0)

        @pl.when(n == 0)
        def _():
            gqkv_ref[...] = jnp.zeros_like(gqkv_ref)
            gz_ref[...] = jnp.zeros_like(gz_ref)
            gba_ref[...] = jnp.zeros_like(gba_ref)
            dcw_ref[...] = jnp.zeros_like(dcw_ref)
            dal_ref[...] = jnp.zeros_like(dal_ref)
            ddtb_ref[...] = jnp.zeros_like(ddtb_ref)

        hb = h_ref[...]
        gz_ref[...] += lax.dot_general(hb, dz_ref[...], _TN, preferred_element_type=F32)
        last = n == nt - 1
        prev = jnp.where(n == 0, 0.0, ph_ref[...])
        ext = jnp.concatenate([prev, pre_ref[...], nh_ref[...]], axis=0)
        taps = _conv_taps(ext, ts + 8)
        conv = taps[0] * cw_ref[0:1, :]
        for j in range(1, CONV_K):
            conv = conv + taps[j] * cw_ref[j:j + 1, :]

        def cot(main, halo, cols):
            return jnp.concatenate([main[:, cols], jnp.where(last, 0.0, halo[:, cols])], axis=0)

        rows = ts + 8
        for grp, (fn, mref, href) in enumerate(((_post_q, dq_ref, dqh_ref), (_post_k, dk_ref, dkh_ref),
                                                (_post_v, dv_ref, dvh_ref))):
            gcols = slice(grp * DN_WIDTH, (grp + 1) * DN_WIDTH)
            pieces = []
            for h in range(DN_HEADS):
                cols = slice(h * DN_DIM, (h + 1) * DN_DIM)
                c0 = grp * DN_WIDTH + h * DN_DIM
                _, vjp = jax.vjp(fn, conv[:, c0:c0 + DN_DIM])
                pieces.append(vjp(cot(mref, href, cols))[0])
            dconv = jnp.concatenate(pieces, axis=1)
            dpre = dconv[:ts] * cw_ref[CONV_K - 1:CONV_K, gcols]
            for j in range(CONV_K - 1):
                sh = CONV_K - 1 - j
                dpre = dpre + pltpu.roll(dconv, rows - sh, 0)[:ts] * cw_ref[j:j + 1, gcols]
            dpre_b = _bf(dpre)
            dpre_ref[:, gcols] = dpre_b
            gqkv_ref[:, gcols] += lax.dot_general(hb, dpre_b, _TN, preferred_element_type=F32)
            for j in range(CONV_K):
                dcw_ref[j:j + 1, gcols] += jnp.sum(dconv[:ts] * taps[j][:ts, gcols], axis=0, keepdims=True)

        dbg = dbg_ref[...]
        lane = lax.broadcasted_iota(jnp.int32, dbg.shape, 1)
        dg = pltpu.roll(_chunk_cumsum(dbg, reverse=True), BA_PAD - DN_HEADS, 1)
        cot_bg = jnp.where(lane < DN_HEADS, dbg, jnp.where(lane < GC_LANE, dg, 0.0))
        _, vjp = jax.vjp(_beta_decay, ba_ref[...], al_ref[...], dtb_ref[...])
        dba, dal, ddtb = vjp(cot_bg)
        dba_b = _bf(dba)
        dba_ref[...] = dba_b
        gba_ref[...] += lax.dot_general(hb, dba_b, _TN, preferred_element_type=F32)
        dal_ref[...] += dal
        ddtb_ref[...] += ddtb

    tok = lambda w: pl.BlockSpec((ts, w), lambda i: (i, 0))
    full = lambda a: pl.BlockSpec(a.shape, lambda i: (0, 0))
    prevh = lambda w: pl.BlockSpec((8, w), lambda i: (jnp.maximum(i * (ts // 8) - 1, 0), 0))
    nexth = lambda w: pl.BlockSpec((8, w), lambda i: (jnp.minimum((i + 1) * (ts // 8), s // 8 - 1), 0))
    row = pl.BlockSpec((1, LANES), lambda i: (0, 0))
    return pl.pallas_call(
        body, name="dn_prep_bwd", grid=(nt,), compiler_params=_params("arbitrary"),
        in_specs=[tok(cw), prevh(cw), nexth(cw), tok(BA_PAD),
                  tok(DN_WIDTH), nexth(DN_WIDTH), tok(DN_WIDTH), nexth(DN_WIDTH), tok(DN_WIDTH), nexth(DN_WIDTH),
                  tok(BA_PAD), full(conv_w8), full(alog_row), full(dtb_row), tok(D_MODEL), tok(DN_WIDTH)],
        out_specs=(tok(cw), tok(BA_PAD), pl.BlockSpec((8, cw), lambda i: (0, 0)), row, row)
        + tuple(pl.BlockSpec((D_MODEL, w), lambda i: (0, 0)) for w in (cw, DN_WIDTH, BA_PAD)),
        out_shape=(SDS((s, cw), BF16), SDS((s, BA_PAD), BF16), SDS((8, cw), F32), SDS((1, LANES), F32),
                   SDS((1, LANES), F32)) + tuple(SDS((D_MODEL, w), F32) for w in (cw, DN_WIDTH, BA_PAD)),
    )(qkv_pre, qkv_pre, qkv_pre, ba, dq, dq, dk, dk, dv, dv, dbg, conv_w8, alog_row, dtb_row, hbf, dz_dn)


def _dh_dx(dps, ws, x, mod, norm_w, dx2, ts):
    s = x.shape[0]
    widths = [w.shape[1] for w in ws]
    np_ = len(ws)

    def body(*refs):
        dp_refs, w_refs = refs[:np_], refs[np_:2 * np_]
        x_ref, mod_ref, nw_ref, dx2_ref, gx_ref, dshift, dscale, dnw = refs[2 * np_:]

        @pl.when(pl.program_id(0) == 0)
        def _():
            dshift[...] = jnp.zeros_like(dshift)
            dscale[...] = jnp.zeros_like(dscale)
            dnw[...] = jnp.zeros_like(dnw)

        dh = lax.dot_general(dp_refs[0][...], w_refs[0][...], _NT, preferred_element_type=F32)
        for a, b in zip(dp_refs[1:], w_refs[1:]):
            dh = dh + lax.dot_general(a[...], b[...], _NT, preferred_element_type=F32)
        xt = x_ref[...]
        r = lax.rsqrt(jnp.mean(xt * xt, axis=-1, keepdims=True) + EPS)
        xn = xt * r
        nw = nw_ref[...]
        sc1 = 1.0 + mod_ref[:, D_MODEL:2 * D_MODEL]
        dshift[...] += jnp.sum(dh, axis=0, keepdims=True)
        dscale[...] += jnp.sum(dh * (xn * nw), axis=0, keepdims=True)
        dnw[...] += jnp.sum(dh * sc1 * xn, axis=0, keepdims=True)
        dxn = dh * sc1 * nw
        gx_ref[...] = r * (dxn - xn * jnp.mean(dxn * xn, axis=-1, keepdims=True)) + dx2_ref[...]

    tok = lambda w: pl.BlockSpec((ts, w), lambda i: (i, 0))
    full = lambda a: pl.BlockSpec(a.shape, lambda i: (0, 0))
    row = pl.BlockSpec((1, D_MODEL), lambda i: (0, 0))
    return pl.pallas_call(
        body, name="dh_dx", grid=(s // ts,), compiler_params=_params("arbitrary"),
        in_specs=[tok(w) for w in widths] + [full(w) for w in ws] + [tok(D_MODEL), full(mod), full(norm_w),
                                                                    tok(D_MODEL)],
        out_specs=(tok(D_MODEL), row, row, row),
        out_shape=(SDS((s, D_MODEL), F32),) + (SDS((1, D_MODEL), F32),) * 3,
    )(*dps, *ws, x, mod, norm_w, dx2)


def _grad_w_in_at(h, dq, dk, dv, dz_at, cos_t, sin_t, ts):
    s = h.shape[0]

    def body(h_ref, q_ref, k_ref, v_ref, dz_ref, cos_ref, sin_ref, oq, ok, ov, gq, gk, gv, gz):
        @pl.when(pl.program_id(0) == 0)
        def _():
            for o in (gq, gk, gv, gz):
                o[...] = jnp.zeros_like(o)

        hb = h_ref[...]
        gz[...] += lax.dot_general(hb, dz_ref[...], _TN, preferred_element_type=F32)
        cs, sn = cos_ref[...], sin_ref[...]
        for j in range(AT_PAIRS):
            ov[:, j * LANES:(j + 1) * LANES] = _bf(v_ref[j])
        gv[...] += lax.dot_general(hb, ov[...], _TN, preferred_element_type=F32)
        for g_ref, o_ref, acc in ((q_ref, oq, gq), (k_ref, ok, gk)):
            for j in range(AT_PAIRS):
                g = g_ref[j]
                o_ref[:, j * LANES:(j + 1) * LANES] = _bf(g * cs + _swap_half64(g * sn))
            acc[...] += lax.dot_general(hb, o_ref[...], _TN, preferred_element_type=F32)

    tok = lambda w: pl.BlockSpec((ts, w), lambda i: (i, 0))
    pairs = pl.BlockSpec((AT_PAIRS, ts, LANES), lambda i: (0, i, 0))
    acc = pl.BlockSpec((D_MODEL, AT_WIDTH), lambda i: (0, 0))
    return pl.pallas_call(
        body, name="grad_w_in_at", grid=(s // ts,), compiler_params=_params("arbitrary"),
        in_specs=[tok(D_MODEL), pairs, pairs, pairs, tok(AT_WIDTH), tok(LANES), tok(LANES)],
        out_specs=(tok(AT_WIDTH),) * 3 + (acc,) * 4,
        out_shape=(SDS((s, AT_WIDTH), BF16),) * 3 + (SDS((D_MODEL, AT_WIDTH), F32),) * 4,
    )(h, dq, dk, dv, dz_at, cos_t, sin_t)


def _adamw_math(w, g, m, v):
    m = ADAM_B1 * m + (1.0 - ADAM_B1) * g
    v = ADAM_B2 * v + (1.0 - ADAM_B2) * (g * g)
    m_hat = m / (1.0 - ADAM_B1 ** ADAM_STEP)
    v_hat = v / (1.0 - ADAM_B2 ** ADAM_STEP)
    delta = -ADAM_LR * (m_hat / (jnp.sqrt(v_hat) + ADAM_EPS) + ADAM_WD * w)
    return delta, m, v


def _adamw(w, m, v, g, name, own=None):
    def body(w_ref, m_ref, v_ref, g_ref, *rest):
        g_out, d_out, m_out, v_out = rest[-4:]
        if own is None:
            g = g_ref[...]
        else:
            g = g_ref[0].astype(F32)
            for k in range(1, N_DEV):
                g = g + g_ref[k].astype(F32)
            g = g + rest[0][...].astype(F32)
        g_out[...] = g
        d_out[...], m_out[...], v_out[...] = _adamw_math(w_ref[...], g, m_ref[...], v_ref[...])

    args = (w, m, v, g) if own is None else (w, m, v, g, own)
    return pl.pallas_call(body, name=name, compiler_params=_params(),
                          out_shape=(SDS(w.shape, F32),) * 4)(*args)


def _adamw_w_mod(w, m, v, siluc_all, dmod_mine):
    def body(w_ref, m_ref, v_ref, sc_ref, dm_ref, g_out, d_out, m_out, v_out):
        g = _htn(sc_ref[...], dm_ref[...])
        g_out[...] = g
        d_out[...], m_out[...], v_out[...] = _adamw_math(w_ref[...], g, m_ref[...], v_ref[...])

    return pl.pallas_call(body, name="adamw_w_mod", compiler_params=_params(),
                          out_shape=(SDS(w.shape, F32),) * 4)(w, m, v, siluc_all, dmod_mine)


def _pack_sum(pack_all):
    def body(p_ref, o_ref):
        t = p_ref[0]
        for k in range(1, N_DEV):
            t = t + p_ref[k]
        o_ref[...] = t

    return pl.pallas_call(body, name="pack_sum", out_shape=SDS(pack_all.shape[1:], F32))(pack_all)


def _tile(s, want):
    t = min(want, s)
    assert s % t == 0
    return t


def _local_step(x, c, positions, w_mod_bf, b_mod, norm_w, w_in_bf, conv_w, a_log, dt_bias, dn_norm_w, at_norm_w,
                w_out_bf, final_norm_w, tgt):
    s = x.shape[0]
    o = [0]
    for wdt in IN_SPLITS:
        o.append(o[-1] + wdt)
    w_ba = jnp.pad(w_in_bf[:, o[2]:o[4]], ((0, 0), (0, BA_PAD - 2 * DN_HEADS)))
    ws = [w_in_bf[:, o[0]:o[1]], w_in_bf[:, o[1]:o[2]], w_ba, w_in_bf[:, o[4]:o[5]], w_in_bf[:, o[5]:o[6]],
          w_in_bf[:, o[6]:o[7]], w_in_bf[:, o[7]:o[8]]]
    conv_w8 = jnp.pad(conv_w, ((0, 8 - CONV_K), (0, 0)))
    alog_row = jnp.pad(a_log, ((0, 0), (DN_HEADS, BA_PAD - 2 * DN_HEADS)))
    dtb_row = jnp.pad(dt_bias, ((0, 0), (DN_HEADS, BA_PAD - 2 * DN_HEADS)))
    atw2 = jnp.concatenate([at_norm_w, at_norm_w], axis=1)

    half = AT_DIM // 2
    lane = jnp.arange(LANES)
    inv_freq = ROPE_THETA ** (-(lane % half).astype(F32) / half)
    ang = positions.astype(F32)[:, None] * inv_freq
    cos_t = jnp.cos(ang)
    sin_t = jnp.sin(ang) * jnp.where((lane // half) % 2 == 0, -1.0, 1.0)

    mod, siluc = _adaln_mod(c, w_mod_bf, b_mod)
    gate = mod[:, 2 * D_MODEL:]
    hbf, qkv_pre, z_dn, ba, qr, kr, vb, z_at, q, k, v, bg = _ln_proj(
        x, mod, norm_w, ws, cos_t, sin_t, conv_w8, alog_row, dtb_row, _tile(s, 256))
    w, qd, kd, p, gl, tinv, o_dn, vn, st = _dn_forward(q, k, v, bg)
    o_at, lse = _attn_fwd(qr, kr, vb)
    (dx2, gw_out, dfw, dgate, loss, do_dn, dz_dn, do_at, dz_at, delta, ddnw, datw) = _out_loss(
        o_dn, z_dn, o_at, z_at, dn_norm_w, atw2, x, tgt, w_out_bf, gate, final_norm_w, _tile(s, 512))

    daq, dak, dav, g_aq, g_ak, g_av, g_az = _grad_w_in_at(hbf, *_attn_bwd(qr, kr, vb, do_at, lse, delta), dz_at,
                                                           cos_t, sin_t, _tile(s, 512))
    dq, dk, dv, dbg = _dn_backward(do_dn, st, vn, w, qd, kd, p, gl, q, k, v, bg, tinv)
    dqkv, dba, dcw, dal, ddtb, g_qkv, g_z, g_ba = _dn_prep_bwd(qkv_pre, ba, dq, dk, dv, dbg, conv_w8, alog_row, dtb_row,
                                                               hbf, dz_dn, _tile(s, 512))
    dps = [dqkv, dz_dn, dba, daq, dak, dav, dz_at]
    gw_in = jnp.concatenate([g_qkv, g_z, g_ba[:, :2 * DN_HEADS], g_aq, g_ak, g_av, g_az], axis=1)
    small = dict(conv=dcw[:CONV_K], dgate=dgate, siluc=siluc, dfw=dfw, alog=dal, dtb=ddtb, dnn=ddnw, atn=datw)

    def input_grad(token):
        gx, dshift, dscale, dnw = _dh_dx(dps, ws, x, mod + token, norm_w, dx2, _tile(s, 512))
        return gx, jnp.concatenate([dshift, dscale, small["dgate"]], axis=1), dnw

    return loss, gw_in, gw_out, small, input_grad


def kernel(x, c, positions, w_mod, b_mod, norm_w, w_in, conv_w, a_log, dt_bias, dn_norm_w, at_norm_w, w_out, final_norm_w, loss_target, m_w_mod, m_b_mod, m_norm_w, m_w_in, m_conv_w, m_a_log, m_dt_bias, m_dn_norm_w, m_at_norm_w, m_w_out, m_final_norm_w, v_w_mod, v_b_mod, v_norm_w, v_w_in, v_conv_w, v_a_log, v_dt_bias, v_dn_norm_w, v_at_norm_w, v_w_out, v_final_norm_w):
    me = 4 * lax.axis_index("x") + 2 * lax.axis_index("y") + lax.axis_index("c")
    s = x.shape[1]

    g_mod, g_in, g_conv, g_out = _all_gather(
        [_bf(w_mod[0]), _bf(w_in[0]), conv_w[0], _bf(w_out[0])], "gather_weights")
    w_mod_bf = g_mod.transpose(1, 0, 2).reshape(D_MODEL, 3 * D_MODEL)
    w_in_bf = g_in.transpose(1, 0, 2).reshape(D_MODEL, IN_COLS)
    conv_full = g_conv.transpose(1, 0, 2).reshape(CONV_K, 3 * DN_WIDTH)
    w_out_bf = g_out.reshape(D_MODEL, D_MODEL)

    loss, gw_in, gw_out, small, input_grad = _local_step(
        x[0], c, positions[0], w_mod_bf, b_mod, norm_w, w_in_bf, conv_full, a_log, dt_bias, dn_norm_w, at_norm_w,
        w_out_bf, final_norm_w.reshape(1, D_MODEL), loss_target[0])

    gw_in_slabs = _bf(gw_in).reshape(D_MODEL, N_DEV, IN_SHARD).transpose(1, 0, 2)
    gw_out_slabs = _bf(gw_out).reshape(N_DEV, D_MODEL // N_DEV, D_MODEL)
    send_sems, recv_sems, srcs, lands, token = _scatter_start([gw_in_slabs, gw_out_slabs])
    gx, dmod, dnw = input_grad(token[0, 0])
    r_in, r_out = _scatter_wait(send_sems, recv_sems, srcs, lands, gx)
    own_in = lax.dynamic_index_in_dim(gw_in_slabs, me, 0, keepdims=False)
    own_out = lax.dynamic_index_in_dim(gw_out_slabs, me, 0, keepdims=False)

    pack = jnp.concatenate([small["conv"].reshape(1, -1), dmod, small["siluc"], dnw, small["dfw"],
                            small["alog"], small["dtb"], small["dnn"], small["atn"],
                            jnp.pad(loss, ((0, 0), (0, LANES - 1)))], axis=1).reshape(PK_ROWS, LANES)
    (pack_all,) = _exchange([pack], [False], "exchange_small")

    res = {}
    res["w_in"] = _adamw(w_in[0], m_w_in[0], v_w_in[0], r_in, "adamw_w_in", own=own_in)
    res["w_out"] = _adamw(w_out[0], m_w_out[0], v_w_out[0], r_out, "adamw_w_out", own=own_out)
    flat_all = pack_all.reshape(N_DEV, PK_END)
    dmod_mine = lax.dynamic_slice(flat_all, (0, PK_DMOD + me * (3 * D_MODEL // N_DEV)), (N_DEV, 3 * D_MODEL // N_DEV))
    res["w_mod"] = _adamw_w_mod(w_mod[0], m_w_mod[0], v_w_mod[0], flat_all[:, PK_SILUC:PK_DNW], dmod_mine)
    tot = _pack_sum(pack_all).reshape(1, PK_END)
    g_conv_full = tot[:, PK_CONV:PK_DMOD].reshape(CONV_K, 3 * DN_WIDTH)
    g_conv_mine = lax.dynamic_slice(g_conv_full, (0, me * (3 * DN_WIDTH // N_DEV)), (CONV_K, 3 * DN_WIDTH // N_DEV))
    res["conv_w"] = _adamw(conv_w[0], m_conv_w[0], v_conv_w[0], g_conv_mine, "adamw_conv_w")
    res["b_mod"] = _adamw(b_mod, m_b_mod, v_b_mod, tot[:, PK_DMOD:PK_SILUC], "adamw_b_mod")
    res["norm_w"] = _adamw(norm_w, m_norm_w, v_norm_w, tot[:, PK_DNW:PK_DFW], "adamw_norm_w")
    res["a_log"] = _adamw(a_log, m_a_log, v_a_log, tot[:, PK_ALOG + DN_HEADS:PK_ALOG + 2 * DN_HEADS], "adamw_a_log")
    res["dt_bias"] = _adamw(dt_bias, m_dt_bias, v_dt_bias, tot[:, PK_DTB + DN_HEADS:PK_DTB + 2 * DN_HEADS],
                            "adamw_dt_bias")
    res["dn_norm_w"] = _adamw(dn_norm_w, m_dn_norm_w, v_dn_norm_w, tot[:, PK_DNN:PK_ATN], "adamw_dn_norm_w")
    g_atn = tot[:, PK_ATN:PK_ATN + AT_DIM] + tot[:, PK_ATN + AT_DIM:PK_LOSS]
    res["at_norm_w"] = _adamw(at_norm_w, m_at_norm_w, v_at_norm_w, g_atn, "adamw_at_norm_w")
    fin = _adamw(final_norm_w.reshape(1, D_MODEL), m_final_norm_w.reshape(1, D_MODEL),
                 v_final_norm_w.reshape(1, D_MODEL), tot[:, PK_DFW:PK_ALOG], "adamw_final_norm_w")
    res["final_norm_w"] = tuple(a.reshape(D_MODEL) for a in fin)

    lead = ("w_mod", "w_in", "conv_w", "w_out")
    names = ("w_mod", "b_mod", "norm_w", "w_in", "conv_w", "a_log", "dt_bias", "dn_norm_w", "at_norm_w", "w_out",
             "final_norm_w")
    out = [tot[0, PK_LOSS], gx.reshape(1, s, D_MODEL)]
    for kind in range(4):
        for nm in names:
            a = res[nm][kind]
            out.append(a[None] if nm in lead else a)
    return tuple(out)
```

```python
import functools

import jax
import jax.numpy as jnp
from jax import lax
from jax.experimental import pallas as pl
from jax.experimental.pallas import tpu as pltpu

F32, BF16 = jnp.float32, jnp.bfloat16
HI = lax.Precision.HIGHEST
SDS = jax.ShapeDtypeStruct

D_MODEL = 1024
DN_HEADS, DN_DIM, DN_WIDTH = 4, 128, 512
AT_HEADS, AT_DIM, AT_WIDTH = 8, 64, 512
CONV_K = 4
CHUNK = 64
Q_BLOCK = 128
W_SUB = 128
DILATIONS = (1, 4, 16)
AT_PAIRS = 4
PLANES = 16
ATT_BLK = Q_BLOCK * max(DILATIONS)
ATT_UNROLL, ATT_UNROLL_BWD = 8, 4
CH_UNROLL, CH_UNROLL_BWD = 4, 8
ROPE_THETA = 10000.0
EPS = 1e-6
N_DEV = 8
LANES = 128
BA_PAD = 128
IN_SPLITS = (1536, 512, 4, 4, 512, 512, 512, 512)
IN_COLS = sum(IN_SPLITS)
IN_SHARD = IN_COLS // N_DEV
VMEM_LIMIT = 58 * 2 ** 20

ADAM_LR, ADAM_B1, ADAM_B2, ADAM_EPS, ADAM_WD, ADAM_STEP = 0.001, 0.9, 0.999, 1e-08, 0.01, 10

PK_CONV, PK_DMOD, PK_SILUC, PK_DNW, PK_DFW, PK_ALOG, PK_DTB, PK_DNN, PK_ATN, PK_LOSS, PK_END = (
    0, 6144, 9216, 10240, 11264, 12288, 12416, 12544, 12672, 12800, 12928)
PK_ROWS = PK_END // LANES

_NT = (((1,), (1,)), ((), ()))
_TN = (((0,), (0,)), ((), ()))


def _params(*sem):
    return pltpu.CompilerParams(dimension_semantics=sem or None, vmem_limit_bytes=VMEM_LIMIT)


def _bf(x):
    return x.astype(BF16)


def _nn(a, b):
    return jnp.dot(_bf(a), _bf(b), preferred_element_type=F32)


def _nt(a, b):
    return lax.dot_general(_bf(a), _bf(b), _NT, preferred_element_type=F32)


def _tn(a, b):
    return lax.dot_general(_bf(a), _bf(b), _TN, preferred_element_type=F32)


def _htn(a, b):
    return lax.dot_general(a, b, _TN, precision=HI, preferred_element_type=F32)


def _head_sum(x):
    r = lax.broadcasted_iota(jnp.int32, (LANES, LANES), 0)
    c = lax.broadcasted_iota(jnp.int32, (LANES, LANES), 1)
    same = jnp.where((r // AT_DIM) == (c // AT_DIM), 1.0, 0.0).astype(BF16)
    hi, lo = _hl(x)
    return jnp.dot(hi, same, preferred_element_type=F32) + jnp.dot(lo, same, preferred_element_type=F32)


@jax.custom_vjp
def _d_head_sum(x):
    return _head_sum(x)


_d_head_sum.defvjp(lambda x: (_head_sum(x), None), lambda _, g: (_head_sum(g),))


def _silu(x):
    return x * jax.nn.sigmoid(x)


def _softplus(x):
    return jnp.maximum(x, 0.0) + jnp.log(1.0 + jnp.exp(-jnp.abs(x)))


def _l2n(x):
    return x * lax.rsqrt(jnp.sum(x * x, axis=-1, keepdims=True) + EPS)


def _post_q(x):
    return _l2n(_silu(x)) * (DN_DIM ** -0.5)


def _post_k(x):
    return _l2n(_silu(x))


def _post_v(x):
    return _silu(x)


def _beta_decay(ba, alog_row, dtb_row):
    lane = lax.broadcasted_iota(jnp.int32, ba.shape, 1)
    return jnp.where(lane < DN_HEADS, jax.nn.sigmoid(ba), -jnp.exp(alog_row) * _softplus(ba + dtb_row))


def _gate_dn(o, z, w):
    return (o * lax.rsqrt(jnp.mean(o * o, axis=-1, keepdims=True) + EPS)) * w * _silu(z)


def _gate_at(o, z, w2, head_sum):
    ms = head_sum(o * o) * (1.0 / AT_DIM)
    return (o * lax.rsqrt(ms + EPS)) * w2 * _silu(z)


def _swap_half64(x):
    lane = lax.broadcasted_iota(jnp.int32, x.shape, 1)
    return jnp.where((lane & (AT_DIM - 1)) < AT_DIM // 2, pltpu.roll(x, LANES - AT_DIM // 2, 1),
                     pltpu.roll(x, AT_DIM // 2, 1))


_NN = (((1,), (0,)), ((), ()))


def _hl(a):
    hi = a.astype(BF16)
    return hi, (a - hi.astype(F32)).astype(BF16)


def _mm3(a, b, dims=_NN):
    (ah, al), (bh, bl) = a, b
    f = lambda x, y: lax.dot_general(x, y, dims, preferred_element_type=F32)
    return f(ah, bh) + (f(ah, bl) + f(al, bh))


def _chunk_masks():
    r = lax.broadcasted_iota(jnp.int32, (CHUNK, CHUNK), 0)
    c = lax.broadcasted_iota(jnp.int32, (CHUNK, CHUNK), 1)
    return r >= c, r > c, (r == c).astype(F32), (r // 16) == (c // 16)


def _tri_inv(mats, tick=lambda: None):
    _, _, eye, blk = _chunk_masks()
    dg = [jnp.where(blk, a, 0.0) for a in mats]
    lo = [jnp.where(blk, 0.0, a) for a in mats]
    sdg = [_hl(x) for x in dg]
    d2 = [_mm3(s, s) for s in sdg]
    tick()
    sd2 = [_hl(x) for x in d2]
    d4 = [_mm3(s, s) for s in sd2]
    tick()
    sd4 = [_hl(x) for x in d4]
    d8 = [_mm3(s, s) for s in sd4]
    tick()
    p1 = [_mm3(_hl(eye - a), _hl(eye + b)) for a, b in zip(dg, d2)]
    tick()
    p2 = [_mm3(_hl(a), _hl(eye + b)) for a, b in zip(p1, d4)]
    tick()
    dinv = [_mm3(_hl(a), _hl(eye + b)) for a, b in zip(p2, d8)]
    tick()
    sdinv = [_hl(x) for x in dinv]
    n1 = [_mm3(s, _hl(b)) for s, b in zip(sdinv, lo)]
    tick()
    sn1 = [_hl(x) for x in n1]
    n2 = [_mm3(s, s) for s in sn1]
    tick()
    q1 = [_mm3(_hl(eye - a), _hl(eye + b)) for a, b in zip(n1, n2)]
    return [_mm3(_hl(a), s) for a, s in zip(q1, sdinv)]


def _chunk_common(qs, ks, vs, betas, gcs):
    tril, _, _, _ = _chunk_masks()
    out = []
    for q, k, v, beta, gc in zip(qs, ks, vs, betas, gcs):
        gb = jnp.broadcast_to(gc, (CHUNK, DN_DIM))
        gt = gb.T[:CHUNK, :]
        gam = jnp.where(tril, jnp.exp(jnp.where(tril, gb[:, :CHUNK] - gt, 0.0)), 0.0)
        last = gb[CHUNK - 1:CHUNK, :]
        eg, e2 = jnp.exp(gb), jnp.exp(last - gb)
        kb, vb = k * beta, v * beta
        out.append(dict(gam=gam, eg=eg, e2=e2, gl=jnp.exp(last[:, 0:1]), kb=kb, vb=vb, kbg=kb * eg,
                        m=_nt(kb, k), qk=_nt(q, k)))
    return out


def _chunk_fwd(qs, ks, vs, betas, gcs, tick=lambda: None):
    tril, strict, _, _ = _chunk_masks()
    cm = _chunk_common(qs, ks, vs, betas, gcs)
    ts = _tri_inv([jnp.where(strict, c["m"] * c["gam"], 0.0) for c in cm], tick)
    outs = []
    for q, k, c, t in zip(qs, ks, cm, ts):
        uw = _nn(t, jnp.concatenate([c["vb"], c["kbg"]], axis=1))
        p = jnp.where(tril, c["qk"] * c["gam"], 0.0)
        outs.append((uw[:, :DN_DIM], uw[:, DN_DIM:], p, q * c["eg"], k * c["e2"], c["gl"], t.T))
    return outs


def _chunk_bwd(qs, ks, vs, betas, gcs, ts, cots, tick=lambda: None):
    tril, strict, _, _ = _chunk_masks()
    cm = _chunk_common(qs, ks, vs, betas, gcs)
    tick()
    row = lax.broadcasted_iota(jnp.int32, (CHUNK, 1), 0)
    ones = jnp.ones((CHUNK, DN_DIM), BF16)
    rs = lambda x: jnp.sum(x, axis=-1, keepdims=True)
    tts = [_bf(t) for t in ts]
    duw = [_bf(jnp.concatenate([ct[0], ct[1]], axis=1)) for ct in cots]
    dts = [_nt(a, jnp.concatenate([c["vb"], c["kbg"]], axis=1)) for a, c in zip(duw, cm)]
    tick()
    xs = [_nn(t, d) for t, d in zip(tts, dts)]
    tick()
    das = [jnp.where(strict, -_nn(x, t), 0.0) for x, t in zip(xs, tts)]
    dvks = [_nn(t, a) for t, a in zip(tts, duw)]
    tick()
    outs = []
    every = max(1, len(qs) // 5)
    for idx, (q, k, v, beta, c, ct, da, dvk) in enumerate(zip(qs, ks, vs, betas, cm, cots, das, dvks)):
        if idx and idx % every == 0:
            tick()
        _, _, dp, dqd, dkd, dgl = ct
        dvb, dkbg = dvk[:, :DN_DIM], dvk[:, DN_DIM:]
        dm = da * c["gam"]
        dqk = jnp.where(tril, dp, 0.0) * c["gam"]
        e = dm * c["m"] + dqk * c["qk"]
        dmq = jnp.concatenate([dm, dqk], axis=0)
        r1 = _nn(dmq, k)
        dkb = r1[:CHUNK] + dkbg * c["eg"]
        dq = r1[CHUNK:] + dqd * c["eg"]
        dk = _tn(dmq, jnp.concatenate([c["kb"], q], axis=0)) + dkd * c["e2"] + dkb * beta
        dbeta = rs(dkb * k + dvb * v)
        eh, el = _hl(e)
        colsum = (lax.dot_general(eh, ones, _TN, preferred_element_type=F32)
                  + lax.dot_general(el, ones, _TN, preferred_element_type=F32))[:, 0:1]
        pkd = dkd * (k * c["e2"])
        dgc = rs(e) - colsum + rs(dqd * q * c["eg"] + dkbg * c["kbg"] - pkd)
        tail = rs(jnp.sum(pkd, axis=0, keepdims=True)) + dgl * c["gl"]
        dgc = dgc + jnp.where(row == CHUNK - 1, tail, 0.0)
        outs.append((dq, dk, dvb * beta, dbeta, dgc))
    return outs


def _chunk_cumsum(x, reverse=False):
    n = x.shape[0]
    pos = lax.broadcasted_iota(jnp.int32, x.shape, 0) & (CHUNK - 1)
    sh = 1
    while sh < CHUNK:
        if reverse:
            x = x + jnp.where(pos < CHUNK - sh, pltpu.roll(x, n - sh, 0), 0.0)
        else:
            x = x + jnp.where(pos >= sh, pltpu.roll(x, sh, 0), 0.0)
        sh *= 2
    return x


GC_LANE = 2 * DN_HEADS


def _exchange(arrays, scatter, name):
    n = len(arrays)
    out_shapes = []
    for a, sc in zip(arrays, scatter):
        out_shapes.append(SDS(a.shape if sc else (N_DEV,) + a.shape, a.dtype))

    def body(*refs):
        ins, outs = refs[:n], refs[n:2 * n]
        send_sems, recv_sems, loc_sems = refs[2 * n:]
        x, y, c = lax.axis_index("x"), lax.axis_index("y"), lax.axis_index("c")
        me = 4 * x + 2 * y + c
        local, remote = [], []
        for i in range(n):
            src = ins[i].at[me] if scatter[i] else ins[i]
            cp = pltpu.make_async_copy(src, outs[i].at[me], loc_sems.at[i])
            cp.start()
            local.append(cp)
        for dlt in range(1, N_DEV):
            px = 1 - x if dlt & 4 else x
            py = 1 - y if dlt & 2 else y
            pc = 1 - c if dlt & 1 else c
            peer = 4 * px + 2 * py + pc
            for i in range(n):
                src = ins[i].at[peer] if scatter[i] else ins[i]
                cp = pltpu.make_async_remote_copy(
                    src_ref=src, dst_ref=outs[i].at[me],
                    send_sem=send_sems.at[i, dlt - 1], recv_sem=recv_sems.at[i, dlt - 1],
                    device_id=(px, py, pc), device_id_type=pl.DeviceIdType.MESH)
                cp.start()
                arrive = pltpu.make_async_remote_copy(
                    src_ref=src, dst_ref=outs[i].at[peer],
                    send_sem=send_sems.at[i, dlt - 1], recv_sem=recv_sems.at[i, dlt - 1],
                    device_id=(px, py, pc), device_id_type=pl.DeviceIdType.MESH)
                remote.append((cp, arrive))
        for cp, arrive in remote:
            cp.wait_send()
            arrive.wait_recv()
        for cp in local:
            cp.wait()

    any_spec = pl.BlockSpec(memory_space=pl.ANY)
    return pl.pallas_call(
        body, name=name, out_shape=tuple(out_shapes),
        in_specs=[any_spec] * n, out_specs=tuple([any_spec] * n),
        scratch_shapes=[pltpu.SemaphoreType.DMA((n, N_DEV - 1)), pltpu.SemaphoreType.DMA((n, N_DEV - 1)),
                        pltpu.SemaphoreType.DMA((n,))],
    )(*arrays)


def _all_gather(arrays, name):
    n = len(arrays)

    def body(*refs):
        ins, outs = refs[:n], refs[n:2 * n]
        send_sems, recv_sems, loc_sems = refs[2 * n:]
        x, y, c = lax.axis_index("x"), lax.axis_index("y"), lax.axis_index("c")
        me, sibling = (x, y, c), (x, y, 1 - c)
        chips = [(1 - x, y), (x, 1 - y), (1 - x, 1 - y)]

        def copy(i, k, block, to, src=None):
            slot = outs[i].at[4 * block[0] + 2 * block[1] + block[2]]
            return pltpu.make_async_remote_copy(
                src_ref=slot if src is None else src, dst_ref=slot,
                send_sem=send_sems.at[i, k], recv_sem=recv_sems.at[i, k],
                device_id=to, device_id_type=pl.DeviceIdType.MESH)

        mine = [pltpu.make_async_copy(ins[i], outs[i].at[4 * x + 2 * y + c], loc_sems.at[i]) for i in range(n)]
        for cp in mine:
            cp.start()
        first = []
        for i in range(n):
            first.append(copy(i, 0, me, sibling, src=ins[i]))
            first += [copy(i, 1 + j, me, (*chip, c), src=ins[i]) for j, chip in enumerate(chips)]
        for cp in first:
            cp.start()
        passed = []
        for j, chip in enumerate(chips):
            for i in range(n):
                copy(i, 1 + j, (*chip, c), me).wait_recv()
                fwd = copy(i, 4 + j, (*chip, c), sibling)
                fwd.start()
                passed.append(fwd)
        for i in range(n):
            copy(i, 0, sibling, me).wait_recv()
        for j, chip in enumerate(chips):
            for i in range(n):
                copy(i, 4 + j, (*chip, 1 - c), me).wait_recv()
        for cp in first + passed:
            cp.wait_send()
        for cp in mine:
            cp.wait()

    any_spec = pl.BlockSpec(memory_space=pl.ANY)
    return pl.pallas_call(
        body, name=name, out_shape=tuple(SDS((N_DEV,) + a.shape, a.dtype) for a in arrays),
        in_specs=[any_spec] * n, out_specs=tuple([any_spec] * n),
        scratch_shapes=[pltpu.SemaphoreType.DMA((n, N_DEV - 1)), pltpu.SemaphoreType.DMA((n, N_DEV - 1)),
                        pltpu.SemaphoreType.DMA((n,))],
    )(*arrays)


_HBM = pl.BlockSpec(memory_space=pltpu.HBM)
_SEM = pl.BlockSpec(memory_space=pltpu.SEMAPHORE)


def _peers(x, y, c):
    out = []
    for dlt in range(1, N_DEV):
        px = 1 - x if dlt & 4 else x
        py = 1 - y if dlt & 2 else y
        pc = 1 - c if dlt & 1 else c
        out.append((dlt, (px, py, pc), 4 * px + 2 * py + pc))
    return out


def _scatter_start(arrays):
    n = len(arrays)
    ns = n * (N_DEV - 1)

    def body(*refs):
        ins, lands = refs[:n], refs[n:2 * n]
        send_sems, recv_sems = refs[2 * n:2 * n + ns], refs[2 * n + ns:2 * n + 2 * ns]
        token = refs[-1]
        x, y, c = lax.axis_index("x"), lax.axis_index("y"), lax.axis_index("c")
        me = 4 * x + 2 * y + c
        for dlt, peer, pi in _peers(x, y, c):
            for i in range(n):
                k = i * (N_DEV - 1) + dlt - 1
                pltpu.make_async_remote_copy(
                    src_ref=ins[i].at[pi], dst_ref=lands[i].at[me], send_sem=send_sems[k], recv_sem=recv_sems[k],
                    device_id=peer, device_id_type=pl.DeviceIdType.MESH).start()
        token[...] = jnp.zeros_like(token)

    sem = pltpu.SemaphoreType.DMA(())
    thru = tuple(pltpu.HBM(a.shape, a.dtype) for a in arrays)
    hbm = lambda a: pltpu.with_memory_space_constraint(a, pltpu.HBM)
    outs = pl.pallas_call(
        body, name="scatter_start", out_shape=(sem,) * (2 * ns) + thru + thru + (SDS((8, LANES), F32),),
        in_specs=[_HBM] * (2 * n),
        out_specs=(_SEM,) * (2 * ns) + (_HBM,) * (2 * n) + (pl.BlockSpec(memory_space=pltpu.VMEM),),
        input_output_aliases={i: 2 * ns + i for i in range(2 * n)},
        compiler_params=pltpu.CompilerParams(has_side_effects=pltpu.SideEffectType.DATAFLOW_SIDE_EFFECTING),
    )(*[hbm(a) for a in arrays], *[hbm(jnp.zeros(a.shape, a.dtype)) for a in arrays])
    return outs[:ns], outs[ns:2 * ns], outs[2 * ns:2 * ns + n], outs[2 * ns + n:2 * ns + 2 * n], outs[-1]


def _scatter_wait(send_sems, recv_sems, srcs, lands, after):
    n = len(srcs)
    ns = n * (N_DEV - 1)

    def body(*refs):
        ins, lands_ = refs[:n], refs[n:2 * n]
        send, recv = refs[2 * n:2 * n + ns], refs[2 * n + ns:2 * n + 2 * ns]
        x, y, c = lax.axis_index("x"), lax.axis_index("y"), lax.axis_index("c")
        for dlt, peer, pi in _peers(x, y, c):
            for i in range(n):
                k = i * (N_DEV - 1) + dlt - 1
                cp = pltpu.make_async_remote_copy(
                    src_ref=ins[i].at[pi], dst_ref=lands_[i].at[pi], send_sem=send[k], recv_sem=recv[k],
                    device_id=peer, device_id_type=pl.DeviceIdType.MESH)
                cp.wait_send()
                cp.wait_recv()

    thru = tuple(pltpu.HBM(a.shape, a.dtype) for a in srcs)
    outs = pl.pallas_call(
        body, name="scatter_wait", out_shape=thru + thru,
        in_specs=[_HBM] * (2 * n) + [_SEM] * (2 * ns) + [pl.BlockSpec(memory_space=pl.ANY)],
        out_specs=(_HBM,) * (2 * n), input_output_aliases={i: i for i in range(2 * n)},
        compiler_params=pltpu.CompilerParams(has_side_effects=pltpu.SideEffectType.DATAFLOW_SIDE_EFFECTING),
    )(*srcs, *lands, *send_sems, *recv_sems, after)
    return outs[n:]


def _adaln_mod(c, w_mod, b_mod):
    def body(c_ref, w_ref, b_ref, mod_ref, sc_ref):
        sc = _silu(c_ref[...])
        sc8 = jnp.broadcast_to(sc, (8, D_MODEL))
        mod_ref[...] = _nn(sc8, w_ref[...])[0:1] + b_ref[...]
        sc_ref[...] = sc

    return pl.pallas_call(body, name="adaln_mod", compiler_params=_params(),
                          out_shape=(SDS((1, 3 * D_MODEL), F32), SDS((1, D_MODEL), F32)))(c, w_mod, b_mod)


def _ln_proj(x, mod, norm_w, ws, cos_t, sin_t, conv_w8, alog_row, dtb_row, ts):
    s = x.shape[0]
    widths = [w.shape[1] for w in ws]

    def body(x_ref, mod_ref, nw_ref, cos_ref, sin_ref, cw_ref, al_ref, dtb_ref, wqkv, wz, wba, waq, wak, wav, waz,
             h_ref, oqkv, oz, oba, oq, ok, ov, oaz, q_ref, k_ref, v_ref, bg_ref, halo, perm):
        n = pl.program_id(0)
        xt = x_ref[...]
        r = lax.rsqrt(jnp.mean(xt * xt, axis=-1, keepdims=True) + EPS)
        shift, scale = mod_ref[:, 0:D_MODEL], mod_ref[:, D_MODEL:2 * D_MODEL]
        h = ((xt * r) * nw_ref[...]) * (1.0 + scale) + shift
        hb = _bf(h)
        h_ref[...] = hb
        tq = jnp.dot(hb, waq[...], preferred_element_type=F32)
        tk = jnp.dot(hb, wak[...], preferred_element_type=F32)
        pre = jnp.dot(hb, wqkv[...], preferred_element_type=F32)
        ba = jnp.dot(hb, wba[...], preferred_element_type=F32)
        tv = jnp.dot(hb, wav[...], preferred_element_type=F32)
        tz = jnp.dot(hb, wz[...], preferred_element_type=F32)
        taz = jnp.dot(hb, waz[...], preferred_element_type=F32)
        cs, sn = cos_ref[...], sin_ref[...]
        slot = 0
        for t, o_ref in ((tq, oq), (tk, ok)):
            for j in range(AT_PAIRS):
                tj = t[:, j * LANES:(j + 1) * LANES]
                for r, x in enumerate(_to_planes(tj * cs + _swap_half64(tj) * sn, perm.at[slot])):
                    o_ref[j, r] = x
                slot += 1
        oqkv[...] = pre
        ext = jnp.concatenate([jnp.where(n == 0, 0.0, halo[...]), pre], axis=0)
        halo[...] = pre[ts - 8:ts]
        taps = _conv_taps(ext, ts)
        conv = taps[0] * cw_ref[0:1, :]
        for j in range(1, CONV_K):
            conv = conv + taps[j] * cw_ref[j:j + 1, :]
        for hd in range(DN_HEADS):
            cols = slice(hd * DN_DIM, (hd + 1) * DN_DIM)
            q_ref[:, cols] = _post_q(conv[:, hd * DN_DIM:(hd + 1) * DN_DIM])
            k_ref[:, cols] = _post_k(conv[:, DN_WIDTH + hd * DN_DIM:DN_WIDTH + (hd + 1) * DN_DIM])
            v_ref[:, cols] = _post_v(conv[:, 2 * DN_WIDTH + hd * DN_DIM:2 * DN_WIDTH + (hd + 1) * DN_DIM])
        oba[...] = ba
        bg = _beta_decay(ba, al_ref[...], dtb_ref[...])
        lane = lax.broadcasted_iota(jnp.int32, bg.shape, 1)
        run = pltpu.roll(_chunk_cumsum(bg), DN_HEADS, 1)
        bg_ref[...] = jnp.where((lane >= GC_LANE) & (lane < GC_LANE + DN_HEADS), run, bg)
        for j in range(AT_PAIRS):
            for r, x in enumerate(_to_planes(tv[:, j * LANES:(j + 1) * LANES], perm.at[slot])):
                ov[j, r] = x
            slot += 1
        oz[...] = tz
        oaz[...] = taz

    tok = lambda w: pl.BlockSpec((ts, w), lambda i: (i, 0))
    full = lambda a: pl.BlockSpec(a.shape, lambda i: (0, 0))
    pairs = pl.BlockSpec((AT_PAIRS, PLANES, ts // PLANES, LANES), lambda i: (0, 0, i, 0))
    return pl.pallas_call(
        body, name="ln_proj", grid=(s // ts,), compiler_params=_params("arbitrary"),
        in_specs=[tok(D_MODEL), full(mod), full(norm_w), tok(LANES), tok(LANES), full(conv_w8), full(alog_row),
                  full(dtb_row)] + [full(w) for w in ws],
        out_specs=(tok(D_MODEL), tok(widths[0]), tok(widths[1]), tok(widths[2]), pairs, pairs, pairs,
                   tok(widths[6]), tok(DN_WIDTH), tok(DN_WIDTH), tok(DN_WIDTH), tok(BA_PAD)),
        out_shape=(SDS((s, D_MODEL), BF16), SDS((s, widths[0]), F32), SDS((s, widths[1]), F32),
                   SDS((s, widths[2]), F32)) + (SDS((AT_PAIRS, PLANES, s // PLANES, LANES), F32),) * 3 + (SDS((s, widths[6]), F32),)
        + (SDS((s, DN_WIDTH), F32),) * 3 + (SDS((s, BA_PAD), F32),),
        scratch_shapes=[pltpu.VMEM((8, widths[0]), F32), pltpu.VMEM((3 * AT_PAIRS, ts, LANES), F32)],
    )(x, mod, norm_w, cos_t, sin_t, conv_w8, alog_row, dtb_row, *ws)


def _conv_taps(ext, rows):
    taps = []
    for j in range(CONV_K):
        sh = CONV_K - 1 - j
        rolled = pltpu.roll(ext, sh, 0) if sh else ext
        taps.append(rolled[8:8 + rows])
    return taps


def _dn_forward(q, k, v, bg):
    s = q.shape[0]
    tp = CH_UNROLL * CHUNK
    npass = s // tp
    hs = range(DN_HEADS)
    sl = [slice(h * DN_DIM, (h + 1) * DN_DIM) for h in hs]

    def body(q_ref, k_ref, v_ref, bg_ref, w_ref, qd_ref, kd_ref, p_ref, gl_ref, t_ref, o_ref, vn_ref, st_ref,
             state, u_s, w_s, qd_s, kd_s, p_s, gl_s):
        @pl.when(pl.program_id(0) == 0)
        def _():
            for ref in (state, u_s, w_s, qd_s, kd_s, p_s, gl_s):
                ref[...] = jnp.zeros_like(ref)

        def recurrence():
            for c in range(CH_UNROLL):
                rows = slice(c * CHUNK, (c + 1) * CHUNK)
                rows8 = slice(c * 8, (c + 1) * 8)
                srows = slice(c * DN_DIM, (c + 1) * DN_DIM)
                sf = [state[h] for h in hs]
                sb = [_bf(x) for x in sf]
                ws = [_nn(w_s[rows, cl], b) for cl, b in zip(sl, sb)]
                qs = [_nn(qd_s[rows, cl], b) for cl, b in zip(sl, sb)]
                yield
                vn = [u_s[rows, cl] - x for cl, x in zip(sl, ws)]
                vb = [_bf(x) for x in vn]
                kv = [_tn(kd_s[rows, cl], b) for cl, b in zip(sl, vb)]
                pv = [_nn(p_s[h, rows, :], b) for h, b in zip(hs, vb)]
                for h in hs:
                    state[h] = sf[h] * gl_s[rows8, sl[h]][0:1] + kv[h]
                for h in hs:
                    st_ref[srows, sl[h]] = sf[h]
                    vn_ref[rows, sl[h]] = vn[h]
                    o_ref[rows, sl[h]] = qs[h] + pv[h]
                yield

        steps = recurrence()

        where = [(slice(c * CHUNK, (c + 1) * CHUNK), slice(c * 8, (c + 1) * 8), h, sl[h])
                 for c in range(CH_UNROLL) for h in hs]
        bgs = [bg_ref[rows, :] for rows, _, _, _ in where]
        outs = _chunk_fwd([q_ref[rows, cl] for rows, _, _, cl in where], [k_ref[rows, cl] for rows, _, _, cl in where],
                          [v_ref[rows, cl] for rows, _, _, cl in where],
                          [b[:, h:h + 1] for b, (_, _, h, _) in zip(bgs, where)],
                          [b[:, GC_LANE + h:GC_LANE + h + 1] for b, (_, _, h, _) in zip(bgs, where)],
                          tick=lambda: next(steps, None))
        for _ in steps:
            pass
        for (rows, rows8, h, cl), (u, w, p, qd, kd, gl, t) in zip(where, outs):
            g8 = jnp.broadcast_to(gl, (8, DN_DIM))
            u_s[rows, cl] = u
            w_ref[rows, cl] = w
            w_s[rows, cl] = w
            qd_ref[rows, cl] = qd
            qd_s[rows, cl] = qd
            kd_ref[rows, cl] = kd
            kd_s[rows, cl] = kd
            p_ref[h, rows, :] = p
            p_s[h, rows, :] = p
            gl_ref[rows8, cl] = g8
            gl_s[rows8, cl] = g8
            t_ref[h, rows, :] = t

    cur = lambda i: jnp.minimum(i, npass - 1)
    done = lambda i: jnp.maximum(i - 1, 0)
    tokc = pl.BlockSpec((tp, DN_WIDTH), lambda i: (cur(i), 0))
    tokd = pl.BlockSpec((tp, DN_WIDTH), lambda i: (done(i), 0))
    sq = pl.BlockSpec((DN_HEADS, tp, CHUNK), lambda i: (0, cur(i), 0))
    return pl.pallas_call(
        body, name="dn_forward", grid=(npass + 1,), compiler_params=_params("arbitrary"),
        in_specs=[tokc] * 3 + [pl.BlockSpec((tp, BA_PAD), lambda i: (cur(i), 0))],
        out_specs=(tokc, tokc, tokc, sq, pl.BlockSpec((CH_UNROLL * 8, DN_WIDTH), lambda i: (cur(i), 0)), sq,
                   tokd, tokd, pl.BlockSpec((CH_UNROLL * DN_DIM, DN_WIDTH), lambda i: (done(i), 0))),
        out_shape=(SDS((s, DN_WIDTH), F32),) * 3 + (SDS((DN_HEADS, s, CHUNK), F32),
                                                     SDS((s // CHUNK * 8, DN_WIDTH), F32),
                                                     SDS((DN_HEADS, s, CHUNK), F32),
                                                     SDS((s, DN_WIDTH), F32), SDS((s, DN_WIDTH), F32),
                                                     SDS((s // CHUNK * DN_DIM, DN_WIDTH), F32)),
        scratch_shapes=[pltpu.VMEM((DN_HEADS, DN_DIM, DN_DIM), F32)] + [pltpu.VMEM((tp, DN_WIDTH), F32)] * 4
        + [pltpu.VMEM((DN_HEADS, tp, CHUNK), F32), pltpu.VMEM((CH_UNROLL * 8, DN_WIDTH), F32)],
    )(q, k, v, bg)


LOG2E, LN2 = 1.4426950408889634, 0.6931471805599453
MASKED = -1e30


PLANE_ROWS = ATT_BLK // PLANES


def _to_planes(tile, scr):
    scr[...] = tile
    return [scr[pl.ds(r, tile.shape[0] // PLANES, stride=PLANES), :] for r in range(PLANES)]


def _from_planes(planes, scr):
    n = planes[0].shape[0]
    for r in range(PLANES):
        scr[pl.ds(r, n, stride=PLANES), :] = planes[r]
    return scr[...]


def _geom(d):
    nchunk = PLANES // d
    return nchunk, Q_BLOCK // nchunk


def _pattern_bias(d):
    nchunk, qlen = _geom(d)
    row = lax.broadcasted_iota(jnp.int32, (Q_BLOCK, 2 * Q_BLOCK), 0)
    col = lax.broadcasted_iota(jnp.int32, (Q_BLOCK, 2 * Q_BLOCK), 1)
    uq, aq = row // qlen, row % qlen
    uk, ak = col // (2 * qlen), col % (2 * qlen)
    rel = nchunk * (aq - ak + qlen) + (uq - uk)
    band = jnp.where((rel >= 0) & (rel <= W_SUB), 0.0, MASKED)
    col1 = lax.broadcasted_iota(jnp.int32, (1, 2 * Q_BLOCK), 1)
    return band, (col1 % (2 * qlen)) < qlen


def _combo(c, d):
    nchunk, qlen = _geom(d)
    r0, mm = c % d, c // d
    planes = [r0 + d * u for u in range(nchunk)]
    qs = pl.multiple_of(qlen * mm, 8)
    ks = pl.multiple_of(PLANE_ROWS + qlen * mm - qlen, 8)
    return planes, qs, ks, qlen, mm == 0


def _gather(ref, lead, planes, start, n):
    parts = [ref[lead + (p, pl.ds(start, n), slice(None))] for p in planes]
    return parts[0] if len(parts) == 1 else jnp.concatenate(parts, axis=0)


def _scatter(ref, lead, planes, start, n, val, add):
    for u, p in enumerate(planes):
        idx = lead + (p, pl.ds(start, n), slice(None))
        if add:
            ref[idx] += val[u * n:(u + 1) * n]
        else:
            ref[idx] = val[u * n:(u + 1) * n]


def _shift_in(ext, cur, n):
    @pl.when(n == 0)
    def _():
        ext[:, 0:PLANE_ROWS, :] = jnp.zeros((PLANES, PLANE_ROWS, LANES), F32)

    @pl.when(n > 0)
    def _():
        ext[:, 0:PLANE_ROWS, :] = ext[:, PLANE_ROWS:2 * PLANE_ROWS, :]

    ext[:, PLANE_ROWS:2 * PLANE_ROWS, :] = cur


def _attn_fwd(qr, kr, vv):
    s16 = qr.shape[2]
    nblk = s16 // PLANE_ROWS
    scale = AT_DIM ** -0.5
    npat = len(DILATIONS)

    def body(q_ref, k_ref, v_ref, o_ref, lse_ref, kext, vext, o_p, l_p):
        n = pl.program_id(1)
        _shift_in(kext, k_ref[0], n)
        _shift_in(vext, v_ref[0], n)
        lo = lax.broadcasted_iota(jnp.int32, (Q_BLOCK, LANES), 1) < AT_DIM
        for pi, d in enumerate(DILATIONS):
            band, prev_cols = _pattern_bias(d)

            def group(g, carry, pi=pi, d=d, band=band, prev_cols=prev_cols):
                cs = [_combo(g * ATT_UNROLL + u, d) for u in range(ATT_UNROLL)]
                heads = [(i, sel) for i in range(ATT_UNROLL) for sel in (lo, ~lo)]
                bias = [band + jnp.where(prev_cols & ((n == 0) & m0), MASKED, 0.0) for _, _, _, _, m0 in cs]
                qb = [_bf(_gather(q_ref, (0,), pls, qs, ql)) for pls, qs, _, ql, _ in cs]
                kk = [_bf(_gather(kext, (), pls, ks, 2 * ql)) for pls, _, ks, ql, _ in cs]
                vb = [_bf(_gather(vext, (), pls, ks, 2 * ql)) for pls, _, ks, ql, _ in cs]
                sc = [lax.dot_general(jnp.where(sel, qb[i], jnp.zeros_like(qb[i])), kk[i], _NT,
                                      preferred_element_type=F32) for i, sel in heads]
                sc = [x * (scale * LOG2E) + bias[i] for x, (i, _) in zip(sc, heads)]
                mx = [jnp.max(x, axis=-1, keepdims=True) for x in sc]
                pr = [jnp.exp2(x - m) for x, m in zip(sc, mx)]
                ls = [jnp.sum(x, axis=-1, keepdims=True) for x in pr]
                pv = [jnp.dot(_bf(x), vb[i], preferred_element_type=F32) for x, (i, _) in zip(pr, heads)]
                outs = [x / l for x, l in zip(pv, ls)]
                lses = [m * LN2 + jnp.log(l) for m, l in zip(mx, ls)]
                for i, (pls, qs, _, ql, _) in enumerate(cs):
                    _scatter(o_p, (pi,), pls, qs, ql, jnp.where(lo, outs[2 * i], outs[2 * i + 1]), False)
                    _scatter(l_p, (pi,), pls, qs, ql, jnp.where(lo, lses[2 * i], lses[2 * i + 1]), False)
                return carry

            lax.fori_loop(0, ATT_BLK // Q_BLOCK // ATT_UNROLL, group, 0)

        def merge(r, carry):
            ls = [l_p[pi, r] for pi in range(npat)]
            mx = jnp.maximum(jnp.maximum(ls[0], ls[1]), ls[2])
            es = [jnp.exp(l - mx) for l in ls]
            den = es[0] + es[1] + es[2]
            o_ref[0, r] = (es[0] * o_p[0, r] + es[1] * o_p[1, r] + es[2] * o_p[2, r]) / den
            lse_ref[0, r] = mx + jnp.log(den)
            return carry

        lax.fori_loop(0, PLANES, merge, 0)

    blk = pl.BlockSpec((1, PLANES, PLANE_ROWS, LANES), lambda j, n: (j, 0, n, 0))
    return pl.pallas_call(
        body, name="attn_fwd", grid=(AT_PAIRS, nblk), compiler_params=_params("arbitrary", "arbitrary"),
        in_specs=[blk] * 3, out_specs=(blk, blk),
        out_shape=(SDS(qr.shape, F32),) * 2,
        scratch_shapes=[pltpu.VMEM((PLANES, 2 * PLANE_ROWS, LANES), F32)] * 2
        + [pltpu.VMEM((npat, PLANES, PLANE_ROWS, LANES), F32)] * 2,
    )(qr, kr, vv)


def _out_loss(o_dn, z_dn, o_at, z_at, dnw, atw2, x, tgt, w_out, gate, fw, ts):
    s = x.shape[0]

    def body(odn, zdn, oat, zat, dnw_ref, atw_ref, x_ref, t_ref, w_ref, g_ref, fw_ref,
             dx2_ref, gw_ref, dfw_ref, dgate_ref, loss_ref, dodn, dzdn, doat, dzat, delta, ddnw, datw, perm):
        @pl.when(pl.program_id(0) == 0)
        def _():
            for ref in (gw_ref, dfw_ref, dgate_ref, loss_ref, ddnw, datw):
                ref[...] = jnp.zeros_like(ref)

        parts, vjps = [], []
        for h in range(DN_HEADS):
            cols = slice(h * DN_DIM, (h + 1) * DN_DIM)
            y, vjp = jax.vjp(_gate_dn, odn[:, cols], zdn[:, cols], dnw_ref[...])
            parts.append(_bf(y))
            vjps.append(vjp)
        oats = [_from_planes([oat[j, r] for r in range(PLANES)], perm.at[j]) for j in range(AT_PAIRS)]
        for j in range(AT_PAIRS):
            y, vjp = jax.vjp(functools.partial(_gate_at, head_sum=_d_head_sum), oats[j],
                             zat[:, j * LANES:(j + 1) * LANES], atw_ref[...])
            parts.append(_bf(y))
            vjps.append(vjp)
        catb = jnp.concatenate(parts, axis=1)
        wb = w_ref[...]
        gate, fwv = g_ref[...], fw_ref[...]
        mix = jnp.dot(catb, wb, preferred_element_type=F32)
        x2 = x_ref[...] + gate * mix
        r2 = lax.rsqrt(jnp.mean(x2 * x2, axis=-1, keepdims=True) + EPS)
        xn2 = x2 * r2
        err = xn2 * fwv - t_ref[...]
        row = jnp.sum(err * err, axis=-1, keepdims=True) * (1.0 / D_MODEL)
        loss_ref[...] += 0.5 * jnp.sum(row, axis=0, keepdims=True)
        dy = err * (1.0 / D_MODEL)
        dfw_ref[...] += jnp.sum(dy * xn2, axis=0, keepdims=True)
        dxn = dy * fwv
        dx2 = r2 * (dxn - xn2 * jnp.mean(dxn * xn2, axis=-1, keepdims=True))
        dx2_ref[...] = dx2
        dgate_ref[...] += jnp.sum(dx2 * mix, axis=0, keepdims=True)
        dmix = _bf(gate * dx2)
        dcat = lax.dot_general(dmix, wb, _NT, preferred_element_type=F32)
        gw_ref[...] += lax.dot_general(catb, dmix, _TN, preferred_element_type=F32)
        for h in range(DN_HEADS):
            cols = slice(h * DN_DIM, (h + 1) * DN_DIM)
            do, dz, dw = vjps[h](dcat[:, cols])
            dodn[:, cols] = do
            dzdn[:, cols] = _bf(dz)
            ddnw[...] += dw
        for j in range(AT_PAIRS):
            cols = slice(j * LANES, (j + 1) * LANES)
            do, dz, dw = vjps[DN_HEADS + j](dcat[:, DN_WIDTH + j * LANES:DN_WIDTH + (j + 1) * LANES])
            for r, x in enumerate(_to_planes(do, perm.at[j])):
                doat[j, r] = x
            dzat[:, cols] = _bf(dz)
            datw[...] += dw
            for r, x in enumerate(_to_planes(_head_sum(do * oats[j]), perm.at[j])):
                delta[j, r] = x

    tok = lambda w: pl.BlockSpec((ts, w), lambda i: (i, 0))
    full = lambda a: pl.BlockSpec(a.shape, lambda i: (0, 0))
    row = pl.BlockSpec((1, D_MODEL), lambda i: (0, 0))
    lrow = pl.BlockSpec((1, LANES), lambda i: (0, 0))
    pairs = pl.BlockSpec((AT_PAIRS, PLANES, ts // PLANES, LANES), lambda i: (0, 0, i, 0))
    return pl.pallas_call(
        body, name="out_loss", grid=(s // ts,), compiler_params=_params("arbitrary"),
        in_specs=[tok(DN_WIDTH), tok(DN_WIDTH), pairs, tok(AT_WIDTH), full(dnw), full(atw2),
                  tok(D_MODEL), tok(D_MODEL), full(w_out), full(gate), full(fw)],
        out_specs=(tok(D_MODEL), pl.BlockSpec((D_MODEL, D_MODEL), lambda i: (0, 0)), row, row,
                   pl.BlockSpec((1, 1), lambda i: (0, 0)), tok(DN_WIDTH), tok(DN_WIDTH), pairs, tok(AT_WIDTH), pairs,
                   lrow, lrow),
        out_shape=(SDS((s, D_MODEL), F32), SDS((D_MODEL, D_MODEL), F32), SDS((1, D_MODEL), F32),
                   SDS((1, D_MODEL), F32), SDS((1, 1), F32), SDS((s, DN_WIDTH), F32), SDS((s, DN_WIDTH), BF16),
                   SDS((AT_PAIRS, PLANES, s // PLANES, LANES), F32), SDS((s, AT_WIDTH), BF16),
                   SDS((AT_PAIRS, PLANES, s // PLANES, LANES), F32), SDS((1, LANES), F32), SDS((1, LANES), F32)),
        scratch_shapes=[pltpu.VMEM((AT_PAIRS, ts, LANES), F32)],
    )(o_dn, z_dn, o_at, z_at, dnw, atw2, x, tgt, w_out, gate, fw)


def _shift_acc(ext, n):
    @pl.when(n == 0)
    def _():
        ext[:, 0:PLANE_ROWS, :] = jnp.zeros((PLANES, PLANE_ROWS, LANES), F32)

    @pl.when(n > 0)
    def _():
        ext[:, 0:PLANE_ROWS, :] = ext[:, PLANE_ROWS:2 * PLANE_ROWS, :]

    ext[:, PLANE_ROWS:2 * PLANE_ROWS, :] = jnp.zeros((PLANES, PLANE_ROWS, LANES), F32)


def _attn_bwd(qr, kr, vv, do, lse, delta):
    s16 = qr.shape[2]
    nblk = s16 // PLANE_ROWS
    scale = AT_DIM ** -0.5

    def body(q_ref, k_ref, v_ref, do_ref, lse_ref, dl_ref, dq_ref, dk_ref, dv_ref, kext, vext, dkext, dvext):
        n = pl.program_id(1)
        _shift_in(kext, k_ref[0], n)
        _shift_in(vext, v_ref[0], n)
        _shift_acc(dkext, n)
        _shift_acc(dvext, n)

        @pl.when(n < nblk)
        def _():
            dq_ref[0] = jnp.zeros((PLANES, PLANE_ROWS, LANES), F32)
            lo = lax.broadcasted_iota(jnp.int32, (Q_BLOCK, LANES), 1) < AT_DIM
            for d in DILATIONS:
                band, prev_cols = _pattern_bias(d)

                def group(g, carry, d=d, band=band, prev_cols=prev_cols):
                    nu = ATT_UNROLL_BWD
                    cs = [_combo(g * nu + u, d) for u in range(nu)]
                    heads = [(i, sel) for i in range(nu) for sel in (lo, ~lo)]
                    bias = [band + jnp.where(prev_cols & ((n == 0) & m0), MASKED, 0.0) for _, _, _, _, m0 in cs]
                    qb = [_bf(_gather(q_ref, (0,), pls, qs, ql)) for pls, qs, _, ql, _ in cs]
                    dob = [_bf(_gather(do_ref, (0,), pls, qs, ql)) for pls, qs, _, ql, _ in cs]
                    kk = [_bf(_gather(kext, (), pls, ks, 2 * ql)) for pls, _, ks, ql, _ in cs]
                    vb = [_bf(_gather(vext, (), pls, ks, 2 * ql)) for pls, _, ks, ql, _ in cs]
                    lse2 = [_gather(lse_ref, (0,), pls, qs, ql) * LOG2E for pls, qs, _, ql, _ in cs]
                    dl2 = [_gather(dl_ref, (0,), pls, qs, ql) for pls, qs, _, ql, _ in cs]
                    qm = [jnp.where(sel, qb[i], jnp.zeros_like(qb[i])) for i, sel in heads]
                    dom = [jnp.where(sel, dob[i], jnp.zeros_like(dob[i])) for i, sel in heads]
                    lse_c = [jnp.max(jnp.where(sel, lse2[i], -jnp.inf), axis=-1, keepdims=True) for i, sel in heads]
                    dl_c = [jnp.max(jnp.where(sel, dl2[i], -jnp.inf), axis=-1, keepdims=True) for i, sel in heads]
                    sc = [lax.dot_general(a, kk[i], _NT, preferred_element_type=F32) for a, (i, _) in zip(qm, heads)]
                    dp = [lax.dot_general(a, vb[i], _NT, preferred_element_type=F32) for a, (i, _) in zip(dom, heads)]
                    pr = [jnp.exp2(x * (scale * LOG2E) + bias[i] - l) for x, l, (i, _) in zip(sc, lse_c, heads)]
                    ds = [_bf(p * (x - dl) * scale) for p, x, dl in zip(pr, dp, dl_c)]
                    prb = [_bf(p) for p in pr]
                    dq = [jnp.dot(x, kk[i], preferred_element_type=F32) for x, (i, _) in zip(ds, heads)]
                    dk = [lax.dot_general(x, a, _TN, preferred_element_type=F32) for x, a in zip(ds, qm)]
                    dv = [lax.dot_general(x, a, _TN, preferred_element_type=F32) for x, a in zip(prb, dom)]
                    for i, (pls, qs, ks, ql, _) in enumerate(cs):
                        _scatter(dq_ref, (0,), pls, qs, ql, jnp.where(lo, dq[2 * i], dq[2 * i + 1]), True)
                        _scatter(dkext, (), pls, ks, 2 * ql, dk[2 * i] + dk[2 * i + 1], True)
                        _scatter(dvext, (), pls, ks, 2 * ql, dv[2 * i] + dv[2 * i + 1], True)
                    return carry

                lax.fori_loop(0, ATT_BLK // Q_BLOCK // ATT_UNROLL_BWD, group, 0)

        dk_ref[0] = dkext[:, 0:PLANE_ROWS, :]
        dv_ref[0] = dvext[:, 0:PLANE_ROWS, :]

    cur = pl.BlockSpec((1, PLANES, PLANE_ROWS, LANES), lambda j, n: (j, 0, jnp.minimum(n, nblk - 1), 0))
    done = pl.BlockSpec((1, PLANES, PLANE_ROWS, LANES), lambda j, n: (j, 0, jnp.maximum(n - 1, 0), 0))
    return pl.pallas_call(
        body, name="attn_bwd", grid=(AT_PAIRS, nblk + 1), compiler_params=_params("arbitrary", "arbitrary"),
        in_specs=[cur] * 6, out_specs=(cur, done, done),
        out_shape=(SDS(qr.shape, F32),) * 3,
        scratch_shapes=[pltpu.VMEM((PLANES, 2 * PLANE_ROWS, LANES), F32)] * 4,
    )(qr, kr, vv, do, lse, delta)


def _dn_backward(do, st, vn, w, qd, kd, p, gl, q, k, v, bg, t):
    s = do.shape[0]
    nc = CH_UNROLL_BWD
    tp = nc * CHUNK
    npass = s // tp
    hs = range(DN_HEADS)
    sl = [slice(h * DN_DIM, (h + 1) * DN_DIM) for h in hs]

    def body(do_ref, st_ref, vn_ref, w_ref, qd_ref, kd_ref, p_ref, gl_ref, q_ref, k_ref, v_ref, bg_ref, t_ref,
             dq_ref, dk_ref, dv_ref, dbg_ref, dstate, du_s, dw_s, dqd_s, dkd_s, dp_s, dgl_s):
        @pl.when(pl.program_id(0) == 0)
        def _():
            for ref in (dstate, du_s, dw_s, dqd_s, dkd_s, dp_s, dgl_s):
                ref[...] = jnp.zeros_like(ref)

        where = [(slice(c * CHUNK, (c + 1) * CHUNK), slice(c * 8, (c + 1) * 8), h, sl[h])
                 for c in range(nc) for h in hs]
        cots = [(du_s[rows, cl], dw_s[rows, cl], dp_s[h, rows, :], dqd_s[rows, cl], dkd_s[rows, cl],
                 dgl_s[rows8, cl][0:1, 0:1]) for rows, rows8, h, cl in where]

        def recurrence():
            for c in reversed(range(nc)):
                rows = slice(c * CHUNK, (c + 1) * CHUNK)
                rows8 = slice(c * 8, (c + 1) * 8)
                srows = slice(c * DN_DIM, (c + 1) * DN_DIM)
                ds_ = [dstate[h] for h in hs]
                dsb = [_bf(x) for x in ds_]
                dob = [_bf(do_ref[rows, cl]) for cl in sl]
                pdo = [_tn(p_ref[h, rows, :], b) for h, b in zip(hs, dob)]
                qdo = [_tn(qd_ref[rows, cl], b) for cl, b in zip(sl, dob)]
                kds = [_nn(kd_ref[rows, cl], b) for cl, b in zip(sl, dsb)]
                yield
                dvn = [a + b for a, b in zip(kds, pdo)]
                dvb = [_bf(x) for x in dvn]
                wdv = [_tn(w_ref[rows, cl], b) for cl, b in zip(sl, dvb)]
                for h in hs:
                    dstate[h] = ds_[h] * gl_ref[rows8, sl[h]][0:1] + qdo[h] - wdv[h]
                sfs = [st_ref[srows, cl] for cl in sl]
                sbs = [_bf(x) for x in sfs]
                vnb = [_bf(vn_ref[rows, cl]) for cl in sl]
                for h in hs:
                    du_s[rows, sl[h]] = dvn[h]
                    dw_s[rows, sl[h]] = -_nt(dvb[h], sbs[h])
                    dqd_s[rows, sl[h]] = _nt(dob[h], sbs[h])
                    dkd_s[rows, sl[h]] = _nt(vnb[h], dsb[h])
                    dp_s[h, rows, :] = _nt(dob[h], vnb[h])
                    dgl = jnp.sum(jnp.sum(ds_[h] * sfs[h], axis=1, keepdims=True), axis=0, keepdims=True)
                    dgl_s[rows8, sl[h]] = jnp.broadcast_to(dgl, (8, DN_DIM))
                yield

        steps = recurrence()

        bgs = [bg_ref[rows, :] for rows, _, _, _ in where]
        outs = _chunk_bwd([q_ref[rows, cl] for rows, _, _, cl in where], [k_ref[rows, cl] for rows, _, _, cl in where],
                          [v_ref[rows, cl] for rows, _, _, cl in where],
                          [b[:, h:h + 1] for b, (_, _, h, _) in zip(bgs, where)],
                          [b[:, GC_LANE + h:GC_LANE + h + 1] for b, (_, _, h, _) in zip(bgs, where)],
                          [t_ref[h, rows, :] for rows, _, h, _ in where], cots, tick=lambda: next(steps, None))
        for _ in steps:
            pass
        lane = lax.broadcasted_iota(jnp.int32, (CHUNK, BA_PAD), 1)
        for c in range(nc):
            dbg = jnp.zeros((CHUNK, BA_PAD), F32)
            for (rows, _, h, cl), (dq, dk, dv, dbeta, dgc) in list(zip(where, outs))[c * DN_HEADS:(c + 1) * DN_HEADS]:
                dq_ref[rows, cl] = dq
                dk_ref[rows, cl] = dk
                dv_ref[rows, cl] = dv
                dbg = dbg + jnp.where(lane == h, dbeta, 0.0) + jnp.where(lane == GC_LANE + h, dgc, 0.0)
            dbg_ref[where[c * DN_HEADS][0], :] = dbg

    rec = lambda i: jnp.maximum(npass - 1 - i, 0)
    loc = lambda i: jnp.minimum(npass - i, npass - 1)
    tok_r = pl.BlockSpec((tp, DN_WIDTH), lambda i: (rec(i), 0))
    tok_l = pl.BlockSpec((tp, DN_WIDTH), lambda i: (loc(i), 0))
    sq_r = pl.BlockSpec((DN_HEADS, tp, CHUNK), lambda i: (0, rec(i), 0))
    sq_l = pl.BlockSpec((DN_HEADS, tp, CHUNK), lambda i: (0, loc(i), 0))
    ba_l = pl.BlockSpec((tp, BA_PAD), lambda i: (loc(i), 0))
    return pl.pallas_call(
        body, name="dn_backward", grid=(npass + 1,), compiler_params=_params("arbitrary"),
        in_specs=[tok_r, pl.BlockSpec((nc * DN_DIM, DN_WIDTH), lambda i: (rec(i), 0)), tok_r, tok_r, tok_r, tok_r,
                  sq_r, pl.BlockSpec((nc * 8, DN_WIDTH), lambda i: (rec(i), 0)),
                  tok_l, tok_l, tok_l, ba_l, sq_l],
        out_specs=(tok_l, tok_l, tok_l, ba_l),
        out_shape=(SDS((s, DN_WIDTH), F32),) * 3 + (SDS((s, BA_PAD), F32),),
        scratch_shapes=[pltpu.VMEM((DN_HEADS, DN_DIM, DN_DIM), F32)] + [pltpu.VMEM((tp, DN_WIDTH), F32)] * 4
        + [pltpu.VMEM((DN_HEADS, tp, CHUNK), F32), pltpu.VMEM((nc * 8, DN_WIDTH), F32)],
    )(do, st, vn, w, qd, kd, p, gl, q, k, v, bg, t)


def _dn_prep_bwd(qkv_pre, ba, dq, dk, dv, dbg, conv_w8, alog_row, dtb_row, hbf, dz_dn, ts):
    s = qkv_pre.shape[0]
    cw = 3 * DN_WIDTH
    nt = s // ts

    def body(pre_ref, ph_ref, nh_ref, ba_ref, dq_ref, dqh_ref, dk_ref, dkh_ref, dv_ref, dvh_ref, dbg_ref,
             cw_ref, al_ref, dtb_ref, h_ref, dz_ref, dpre_ref, dba_ref, dcw_ref, dal_ref, ddtb_ref,
             gqkv_ref, gz_ref, gba_ref):
        n = pl.program_id(0)

        @pl.when(n == 0)
        def _():
            gqkv_ref[...] = jnp.zeros_like(gqkv_ref)
            gz_ref[...] = jnp.zeros_like(gz_ref)
            gba_ref[...] = jnp.zeros_like(gba_ref)
            dcw_ref[...] = jnp.zeros_like(dcw_ref)
            dal_ref[...] = jnp.zeros_like(dal_ref)
            ddtb_ref[...] = jnp.zeros_like(ddtb_ref)

        hb = h_ref[...]
        gz_ref[...] += lax.dot_general(hb, dz_ref[...], _TN, preferred_element_type=F32)
        last = n == nt - 1
        prev = jnp.where(n == 0, 0.0, ph_ref[...])
        ext = jnp.concatenate([prev, pre_ref[...], nh_ref[...]], axis=0)
        taps = _conv_taps(ext, ts + 8)
        conv = taps[0] * cw_ref[0:1, :]
        for j in range(1, CONV_K):
            conv = conv + taps[j] * cw_ref[j:j + 1, :]

        def cot(main, halo, cols):
            return jnp.concatenate([main[:, cols], jnp.where(last, 0.0, halo[:, cols])], axis=0)

        rows = ts + 8
        for grp, (fn, mref, href) in enumerate(((_post_q, dq_ref, dqh_ref), (_post_k, dk_ref, dkh_ref),
                                                (_post_v, dv_ref, dvh_ref))):
            gcols = slice(grp * DN_WIDTH, (grp + 1) * DN_WIDTH)
            pieces = []
            for h in range(DN_HEADS):
                cols = slice(h * DN_DIM, (h + 1) * DN_DIM)
                c0 = grp * DN_WIDTH + h * DN_DIM
                _, vjp = jax.vjp(fn, conv[:, c0:c0 + DN_DIM])
                pieces.append(vjp(cot(mref, href, cols))[0])
            dconv = jnp.concatenate(pieces, axis=1)
            dpre = dconv[:ts] * cw_ref[CONV_K - 1:CONV_K, gcols]
            for j in range(CONV_K - 1):
                sh = CONV_K - 1 - j
                dpre = dpre + pltpu.roll(dconv, rows - sh, 0)[:ts] * cw_ref[j:j + 1, gcols]
            dpre_b = _bf(dpre)
            dpre_ref[:, gcols] = dpre_b
            gqkv_ref[:, gcols] += lax.dot_general(hb, dpre_b, _TN, preferred_element_type=F32)
            for j in range(CONV_K):
                dcw_ref[j:j + 1, gcols] += jnp.sum(dconv[:ts] * taps[j][:ts, gcols], axis=0, keepdims=True)

        dbg = dbg_ref[...]
        lane = lax.broadcasted_iota(jnp.int32, dbg.shape, 1)
        dg = pltpu.roll(_chunk_cumsum(dbg, reverse=True), BA_PAD - DN_HEADS, 1)
        cot_bg = jnp.where(lane < DN_HEADS, dbg, jnp.where(lane < GC_LANE, dg, 0.0))
        _, vjp = jax.vjp(_beta_decay, ba_ref[...], al_ref[...], dtb_ref[...])
        dba, dal, ddtb = vjp(cot_bg)
        dba_b = _bf(dba)
        dba_ref[...] = dba_b
        gba_ref[...] += lax.dot_general(hb, dba_b, _TN, preferred_element_type=F32)
        dal_ref[...] += dal
        ddtb_ref[...] += ddtb

    tok = lambda w: pl.BlockSpec((ts, w), lambda i: (i, 0))
    full = lambda a: pl.BlockSpec(a.shape, lambda i: (0, 0))
    prevh = lambda w: pl.BlockSpec((8, w), lambda i: (jnp.maximum(i * (ts // 8) - 1, 0), 0))
    nexth = lambda w: pl.BlockSpec((8, w), lambda i: (jnp.minimum((i + 1) * (ts // 8), s // 8 - 1), 0))
    row = pl.BlockSpec((1, LANES), lambda i: (0, 0))
    return pl.pallas_call(
        body, name="dn_prep_bwd", grid=(nt,), compiler_params=_params("arbitrary"),
        in_specs=[tok(cw), prevh(cw), nexth(cw), tok(BA_PAD),
                  tok(DN_WIDTH), nexth(DN_WIDTH), tok(DN_WIDTH), nexth(DN_WIDTH), tok(DN_WIDTH), nexth(DN_WIDTH),
                  tok(BA_PAD), full(conv_w8), full(alog_row), full(dtb_row), tok(D_MODEL), tok(DN_WIDTH)],
        out_specs=(tok(cw), tok(BA_PAD), pl.BlockSpec((8, cw), lambda i: (0, 0)), row, row)
        + tuple(pl.BlockSpec((D_MODEL, w), lambda i: (0, 0)) for w in (cw, DN_WIDTH, BA_PAD)),
        out_shape=(SDS((s, cw), BF16), SDS((s, BA_PAD), BF16), SDS((8, cw), F32), SDS((1, LANES), F32),
                   SDS((1, LANES), F32)) + tuple(SDS((D_MODEL, w), F32) for w in (cw, DN_WIDTH, BA_PAD)),
    )(qkv_pre, qkv_pre, qkv_pre, ba, dq, dq, dk, dk, dv, dv, dbg, conv_w8, alog_row, dtb_row, hbf, dz_dn)


def _dh_dx(dps, ws, x, mod, norm_w, dx2, ts):
    s = x.shape[0]
    widths = [w.shape[1] for w in ws]
    np_ = len(ws)

    def body(*refs):
        dp_refs, w_refs = refs[:np_], refs[np_:2 * np_]
        x_ref, mod_ref, nw_ref, dx2_ref, gx_ref, dshift, dscale, dnw = refs[2 * np_:]

        @pl.when(pl.program_id(0) == 0)
        def _():
            dshift[...] = jnp.zeros_like(dshift)
            dscale[...] = jnp.zeros_like(dscale)
            dnw[...] = jnp.zeros_like(dnw)

        dh = lax.dot_general(dp_refs[0][...], w_refs[0][...], _NT, preferred_element_type=F32)
        for a, b in zip(dp_refs[1:], w_refs[1:]):
            dh = dh + lax.dot_general(a[...], b[...], _NT, preferred_element_type=F32)
        xt = x_ref[...]
        r = lax.rsqrt(jnp.mean(xt * xt, axis=-1, keepdims=True) + EPS)
        xn = xt * r
        nw = nw_ref[...]
        sc1 = 1.0 + mod_ref[:, D_MODEL:2 * D_MODEL]
        dshift[...] += jnp.sum(dh, axis=0, keepdims=True)
        dscale[...] += jnp.sum(dh * (xn * nw), axis=0, keepdims=True)
        dnw[...] += jnp.sum(dh * sc1 * xn, axis=0, keepdims=True)
        dxn = dh * sc1 * nw
        gx_ref[...] = r * (dxn - xn * jnp.mean(dxn * xn, axis=-1, keepdims=True)) + dx2_ref[...]

    tok = lambda w: pl.BlockSpec((ts, w), lambda i: (i, 0))
    full = lambda a: pl.BlockSpec(a.shape, lambda i: (0, 0))
    row = pl.BlockSpec((1, D_MODEL), lambda i: (0, 0))
    return pl.pallas_call(
        body, name="dh_dx", grid=(s // ts,), compiler_params=_params("arbitrary"),
        in_specs=[tok(w) for w in widths] + [full(w) for w in ws] + [tok(D_MODEL), full(mod), full(norm_w),
                                                                    tok(D_MODEL)],
        out_specs=(tok(D_MODEL), row, row, row),
        out_shape=(SDS((s, D_MODEL), F32),) + (SDS((1, D_MODEL), F32),) * 3,
    )(*dps, *ws, x, mod, norm_w, dx2)


def _grad_w_in_at(h, dq, dk, dv, dz_at, cos_t, sin_t, ts):
    s = h.shape[0]

    def body(h_ref, q_ref, k_ref, v_ref, dz_ref, cos_ref, sin_ref, oq, ok, ov, gq, gk, gv, gz, perm):
        @pl.when(pl.program_id(0) == 0)
        def _():
            for o in (gq, gk, gv, gz):
                o[...] = jnp.zeros_like(o)

        hb = h_ref[...]
        gz[...] += lax.dot_general(hb, dz_ref[...], _TN, preferred_element_type=F32)
        cs, sn = cos_ref[...], sin_ref[...]
        tokens = lambda ref, j, slot: _from_planes([ref[j, r] for r in range(PLANES)], perm.at[slot])
        for j in range(AT_PAIRS):
            ov[:, j * LANES:(j + 1) * LANES] = _bf(tokens(v_ref, j, j))
        gv[...] += lax.dot_general(hb, ov[...], _TN, preferred_element_type=F32)
        for gi, (g_ref, o_ref, acc) in enumerate(((q_ref, oq, gq), (k_ref, ok, gk))):
            for j in range(AT_PAIRS):
                g = tokens(g_ref, j, AT_PAIRS * (gi + 1) + j)
                o_ref[:, j * LANES:(j + 1) * LANES] = _bf(g * cs + _swap_half64(g * sn))
            acc[...] += lax.dot_general(hb, o_ref[...], _TN, preferred_element_type=F32)

    tok = lambda w: pl.BlockSpec((ts, w), lambda i: (i, 0))
    pairs = pl.BlockSpec((AT_PAIRS, PLANES, ts // PLANES, LANES), lambda i: (0, 0, i, 0))
    acc = pl.BlockSpec((D_MODEL, AT_WIDTH), lambda i: (0, 0))
    return pl.pallas_call(
        body, name="grad_w_in_at", grid=(s // ts,), compiler_params=_params("arbitrary"),
        in_specs=[tok(D_MODEL), pairs, pairs, pairs, tok(AT_WIDTH), tok(LANES), tok(LANES)],
        out_specs=(tok(AT_WIDTH),) * 3 + (acc,) * 4,
        out_shape=(SDS((s, AT_WIDTH), BF16),) * 3 + (SDS((D_MODEL, AT_WIDTH), F32),) * 4,
        scratch_shapes=[pltpu.VMEM((3 * AT_PAIRS, ts, LANES), F32)],
    )(h, dq, dk, dv, dz_at, cos_t, sin_t)


def _adamw_math(w, g, m, v):
    m = ADAM_B1 * m + (1.0 - ADAM_B1) * g
    v = ADAM_B2 * v + (1.0 - ADAM_B2) * (g * g)
    m_hat = m / (1.0 - ADAM_B1 ** ADAM_STEP)
    v_hat = v / (1.0 - ADAM_B2 ** ADAM_STEP)
    delta = -ADAM_LR * (m_hat / (jnp.sqrt(v_hat) + ADAM_EPS) + ADAM_WD * w)
    return delta, m, v


def _adamw(w, m, v, g, name, own=None):
    def body(w_ref, m_ref, v_ref, g_ref, *rest):
        g_out, d_out, m_out, v_out = rest[-4:]
        if own is None:
            g = g_ref[...]
        else:
            g = g_ref[0].astype(F32)
            for k in range(1, N_DEV):
                g = g + g_ref[k].astype(F32)
            g = g + rest[0][...].astype(F32)
        g_out[...] = g
        d_out[...], m_out[...], v_out[...] = _adamw_math(w_ref[...], g, m_ref[...], v_ref[...])

    args = (w, m, v, g) if own is None else (w, m, v, g, own)
    return pl.pallas_call(body, name=name, compiler_params=_params(),
                          out_shape=(SDS(w.shape, F32),) * 4)(*args)


def _adamw_w_mod(w, m, v, siluc_all, dmod_mine):
    def body(w_ref, m_ref, v_ref, sc_ref, dm_ref, g_out, d_out, m_out, v_out):
        g = _htn(sc_ref[...], dm_ref[...])
        g_out[...] = g
        d_out[...], m_out[...], v_out[...] = _adamw_math(w_ref[...], g, m_ref[...], v_ref[...])

    return pl.pallas_call(body, name="adamw_w_mod", compiler_params=_params(),
                          out_shape=(SDS(w.shape, F32),) * 4)(w, m, v, siluc_all, dmod_mine)


def _pack_sum(pack_all):
    def body(p_ref, o_ref):
        t = p_ref[0]
        for k in range(1, N_DEV):
            t = t + p_ref[k]
        o_ref[...] = t

    return pl.pallas_call(body, name="pack_sum", out_shape=SDS(pack_all.shape[1:], F32))(pack_all)


def _tile(s, want):
    t = min(want, s)
    assert s % t == 0
    return t


def _local_step(x, c, positions, w_mod_bf, b_mod, norm_w, w_in_bf, conv_w, a_log, dt_bias, dn_norm_w, at_norm_w,
                w_out_bf, final_norm_w, tgt):
    s = x.shape[0]
    o = [0]
    for wdt in IN_SPLITS:
        o.append(o[-1] + wdt)
    w_ba = jnp.pad(w_in_bf[:, o[2]:o[4]], ((0, 0), (0, BA_PAD - 2 * DN_HEADS)))
    ws = [w_in_bf[:, o[0]:o[1]], w_in_bf[:, o[1]:o[2]], w_ba, w_in_bf[:, o[4]:o[5]], w_in_bf[:, o[5]:o[6]],
          w_in_bf[:, o[6]:o[7]], w_in_bf[:, o[7]:o[8]]]
    conv_w8 = jnp.pad(conv_w, ((0, 8 - CONV_K), (0, 0)))
    alog_row = jnp.pad(a_log, ((0, 0), (DN_HEADS, BA_PAD - 2 * DN_HEADS)))
    dtb_row = jnp.pad(dt_bias, ((0, 0), (DN_HEADS, BA_PAD - 2 * DN_HEADS)))
    atw2 = jnp.concatenate([at_norm_w, at_norm_w], axis=1)

    half = AT_DIM // 2
    lane = jnp.arange(LANES)
    inv_freq = ROPE_THETA ** (-(lane % half).astype(F32) / half)
    ang = positions.astype(F32)[:, None] * inv_freq
    cos_t = jnp.cos(ang)
    sin_t = jnp.sin(ang) * jnp.where((lane // half) % 2 == 0, -1.0, 1.0)

    mod, siluc = _adaln_mod(c, w_mod_bf, b_mod)
    gate = mod[:, 2 * D_MODEL:]
    hbf, qkv_pre, z_dn, ba, qr, kr, vb, z_at, q, k, v, bg = _ln_proj(
        x, mod, norm_w, ws, cos_t, sin_t, conv_w8, alog_row, dtb_row, _tile(s, 256))
    w, qd, kd, p, gl, tinv, o_dn, vn, st = _dn_forward(q, k, v, bg)
    o_at, lse = _attn_fwd(qr, kr, vb)
    (dx2, gw_out, dfw, dgate, loss, do_dn, dz_dn, do_at, dz_at, delta, ddnw, datw) = _out_loss(
        o_dn, z_dn, o_at, z_at, dn_norm_w, atw2, x, tgt, w_out_bf, gate, final_norm_w, _tile(s, 512))

    daq, dak, dav, g_aq, g_ak, g_av, g_az = _grad_w_in_at(hbf, *_attn_bwd(qr, kr, vb, do_at, lse, delta), dz_at,
                                                           cos_t, sin_t, _tile(s, 512))
    dq, dk, dv, dbg = _dn_backward(do_dn, st, vn, w, qd, kd, p, gl, q, k, v, bg, tinv)
    dqkv, dba, dcw, dal, ddtb, g_qkv, g_z, g_ba = _dn_prep_bwd(qkv_pre, ba, dq, dk, dv, dbg, conv_w8, alog_row, dtb_row,
                                                               hbf, dz_dn, _tile(s, 512))
    dps = [dqkv, dz_dn, dba, daq, dak, dav, dz_at]
    gw_in = jnp.concatenate([g_qkv, g_z, g_ba[:, :2 * DN_HEADS], g_aq, g_ak, g_av, g_az], axis=1)
    small = dict(conv=dcw[:CONV_K], dgate=dgate, siluc=siluc, dfw=dfw, alog=dal, dtb=ddtb, dnn=ddnw, atn=datw)

    def input_grad(token):
        gx, dshift, dscale, dnw = _dh_dx(dps, ws, x, mod + token, norm_w, dx2, _tile(s, 512))
        return gx, jnp.concatenate([dshift, dscale, small["dgate"]], axis=1), dnw

    return loss, gw_in, gw_out, small, input_grad


def kernel(x, c, positions, w_mod, b_mod, norm_w, w_in, conv_w, a_log, dt_bias, dn_norm_w, at_norm_w, w_out, final_norm_w, loss_target, m_w_mod, m_b_mod, m_norm_w, m_w_in, m_conv_w, m_a_log, m_dt_bias, m_dn_norm_w, m_at_norm_w, m_w_out, m_final_norm_w, v_w_mod, v_b_mod, v_norm_w, v_w_in, v_conv_w, v_a_log, v_dt_bias, v_dn_norm_w, v_at_norm_w, v_w_out, v_final_norm_w):
    me = 4 * lax.axis_index("x") + 2 * lax.axis_index("y") + lax.axis_index("c")
    s = x.shape[1]

    g_mod, g_in, g_conv, g_out = _all_gather(
        [_bf(w_mod[0]), _bf(w_in[0]), conv_w[0], _bf(w_out[0])], "gather_weights")
    w_mod_bf = g_mod.transpose(1, 0, 2).reshape(D_MODEL, 3 * D_MODEL)
    w_in_bf = g_in.transpose(1, 0, 2).reshape(D_MODEL, IN_COLS)
    conv_full = g_conv.transpose(1, 0, 2).reshape(CONV_K, 3 * DN_WIDTH)
    w_out_bf = g_out.reshape(D_MODEL, D_MODEL)

    loss, gw_in, gw_out, small, input_grad = _local_step(
        x[0], c, positions[0], w_mod_bf, b_mod, norm_w, w_in_bf, conv_full, a_log, dt_bias, dn_norm_w, at_norm_w,
        w_out_bf, final_norm_w.reshape(1, D_MODEL), loss_target[0])

    gw_in_slabs = _bf(gw_in).reshape(D_MODEL, N_DEV, IN_SHARD).transpose(1, 0, 2)
    gw_out_slabs = _bf(gw_out).reshape(N_DEV, D_MODEL // N_DEV, D_MODEL)
    send_sems, recv_sems, srcs, lands, token = _scatter_start([gw_in_slabs, gw_out_slabs])
    gx, dmod, dnw = input_grad(token[0, 0])
    r_in, r_out = _scatter_wait(send_sems, recv_sems, srcs, lands, gx)
    own_in = lax.dynamic_index_in_dim(gw_in_slabs, me, 0, keepdims=False)
    own_out = lax.dynamic_index_in_dim(gw_out_slabs, me, 0, keepdims=False)

    pack = jnp.concatenate([small["conv"].reshape(1, -1), dmod, small["siluc"], dnw, small["dfw"],
                            small["alog"], small["dtb"], small["dnn"], small["atn"],
                            jnp.pad(loss, ((0, 0), (0, LANES - 1)))], axis=1).reshape(PK_ROWS, LANES)
    (pack_all,) = _exchange([pack], [False], "exchange_small")

    res = {}
    res["w_in"] = _adamw(w_in[0], m_w_in[0], v_w_in[0], r_in, "adamw_w_in", own=own_in)
    res["w_out"] = _adamw(w_out[0], m_w_out[0], v_w_out[0], r_out, "adamw_w_out", own=own_out)
    flat_all = pack_all.reshape(N_DEV, PK_END)
    dmod_mine = lax.dynamic_slice(flat_all, (0, PK_DMOD + me * (3 * D_MODEL // N_DEV)), (N_DEV, 3 * D_MODEL // N_DEV))
    res["w_mod"] = _adamw_w_mod(w_mod[0], m_w_mod[0], v_w_mod[0], flat_all[:, PK_SILUC:PK_DNW], dmod_mine)
    tot = _pack_sum(pack_all).reshape(1, PK_END)
    g_conv_full = tot[:, PK_CONV:PK_DMOD].reshape(CONV_K, 3 * DN_WIDTH)
    g_conv_mine = lax.dynamic_slice(g_conv_full, (0, me * (3 * DN_WIDTH // N_DEV)), (CONV_K, 3 * DN_WIDTH // N_DEV))
    res["conv_w"] = _adamw(conv_w[0], m_conv_w[0], v_conv_w[0], g_conv_mine, "adamw_conv_w")
    res["b_mod"] = _adamw(b_mod, m_b_mod, v_b_mod, tot[:, PK_DMOD:PK_SILUC], "adamw_b_mod")
    res["norm_w"] = _adamw(norm_w, m_norm_w, v_norm_w, tot[:, PK_DNW:PK_DFW], "adamw_norm_w")
    res["a_log"] = _adamw(a_log, m_a_log, v_a_log, tot[:, PK_ALOG + DN_HEADS:PK_ALOG + 2 * DN_HEADS], "adamw_a_log")
    res["dt_bias"] = _adamw(dt_bias, m_dt_bias, v_dt_bias, tot[:, PK_DTB + DN_HEADS:PK_DTB + 2 * DN_HEADS],
                            "adamw_dt_bias")
    res["dn_norm_w"] = _adamw(dn_norm_w, m_dn_norm_w, v_dn_norm_w, tot[:, PK_DNN:PK_ATN], "adamw_dn_norm_w")
    g_atn = tot[:, PK_ATN:PK_ATN + AT_DIM] + tot[:, PK_ATN + AT_DIM:PK_LOSS]
    res["at_norm_w"] = _adamw(at_norm_w, m_at_norm_w, v_at_norm_w, g_atn, "adamw_at_norm_w")
    fin = _adamw(final_norm_w.reshape(1, D_MODEL), m_final_norm_w.reshape(1, D_MODEL),
                 v_final_norm_w.reshape(1, D_MODEL), tot[:, PK_DFW:PK_ALOG], "adamw_final_norm_w")
    res["final_norm_w"] = tuple(a.reshape(D_MODEL) for a in fin)

    lead = ("w_mod", "w_in", "conv_w", "w_out")
    names = ("w_mod", "b_mod", "norm_w", "w_in", "conv_w", "a_log", "dt_bias", "dn_norm_w", "at_norm_w", "w_out",
             "final_norm_w")
    out = [tot[0, PK_LOSS], gx.reshape(1, s, D_MODEL)]
    for kind in range(4):
        for nm in names:
            a = res[nm][kind]
            out.append(a[None] if nm in lead else a)
    return tuple(out)
```

```python
import functools

import jax
import jax.numpy as jnp
from jax import lax
from jax.experimental import pallas as pl
from jax.experimental.pallas import tpu as pltpu

F32, BF16 = jnp.float32, jnp.bfloat16
HI = lax.Precision.HIGHEST
SDS = jax.ShapeDtypeStruct

D_MODEL = 1024
DN_HEADS, DN_DIM, DN_WIDTH = 4, 128, 512
AT_HEADS, AT_DIM, AT_WIDTH = 8, 64, 512
CONV_K = 4
CHUNK = 64
Q_BLOCK = 128
W_SUB = 128
DILATIONS = (1, 4, 16)
AT_PAIRS = 4
PLANES = 16
PERM_BLK = 256
ATT_BLK = Q_BLOCK * max(DILATIONS)
ATT_UNROLL, ATT_UNROLL_BWD = 16, 4
CH_UNROLL, CH_UNROLL_BWD = 4, 8
ROPE_THETA = 10000.0
EPS = 1e-6
N_DEV = 8
LANES = 128
BA_PAD = 128
IN_SPLITS = (1536, 512, 4, 4, 512, 512, 512, 512)
IN_COLS = sum(IN_SPLITS)
IN_SHARD = IN_COLS // N_DEV
VMEM_LIMIT = 58 * 2 ** 20

ADAM_LR, ADAM_B1, ADAM_B2, ADAM_EPS, ADAM_WD, ADAM_STEP = 0.001, 0.9, 0.999, 1e-08, 0.01, 10

PK_CONV, PK_DMOD, PK_SILUC, PK_DNW, PK_DFW, PK_ALOG, PK_DTB, PK_DNN, PK_ATN, PK_LOSS, PK_END = (
    0, 6144, 9216, 10240, 11264, 12288, 12416, 12544, 12672, 12800, 12928)
PK_ROWS = PK_END // LANES

_NT = (((1,), (1,)), ((), ()))
_TN = (((0,), (0,)), ((), ()))


def _params(*sem):
    return pltpu.CompilerParams(dimension_semantics=sem or None, vmem_limit_bytes=VMEM_LIMIT)


def _bf(x):
    return x.astype(BF16)


def _nn(a, b):
    return jnp.dot(_bf(a), _bf(b), preferred_element_type=F32)


def _nt(a, b):
    return lax.dot_general(_bf(a), _bf(b), _NT, preferred_element_type=F32)


def _tn(a, b):
    return lax.dot_general(_bf(a), _bf(b), _TN, preferred_element_type=F32)


def _htn(a, b):
    return lax.dot_general(a, b, _TN, precision=HI, preferred_element_type=F32)


def _head_sum(x):
    r = lax.broadcasted_iota(jnp.int32, (LANES, LANES), 0)
    c = lax.broadcasted_iota(jnp.int32, (LANES, LANES), 1)
    same = jnp.where((r // AT_DIM) == (c // AT_DIM), 1.0, 0.0).astype(BF16)
    hi, lo = _hl(x)
    return jnp.dot(hi, same, preferred_element_type=F32) + jnp.dot(lo, same, preferred_element_type=F32)


@jax.custom_vjp
def _d_head_sum(x):
    return _head_sum(x)


_d_head_sum.defvjp(lambda x: (_head_sum(x), None), lambda _, g: (_head_sum(g),))


def _silu(x):
    return x * jax.nn.sigmoid(x)


def _softplus(x):
    return jnp.maximum(x, 0.0) + jnp.log(1.0 + jnp.exp(-jnp.abs(x)))


def _l2n(x):
    return x * lax.rsqrt(jnp.sum(x * x, axis=-1, keepdims=True) + EPS)


def _post_q(x):
    return _l2n(_silu(x)) * (DN_DIM ** -0.5)


def _post_k(x):
    return _l2n(_silu(x))


def _post_v(x):
    return _silu(x)


def _beta_decay(ba, alog_row, dtb_row):
    lane = lax.broadcasted_iota(jnp.int32, ba.shape, 1)
    return jnp.where(lane < DN_HEADS, jax.nn.sigmoid(ba), -jnp.exp(alog_row) * _softplus(ba + dtb_row))


def _gate_dn(o, z, w):
    return (o * lax.rsqrt(jnp.mean(o * o, axis=-1, keepdims=True) + EPS)) * w * _silu(z)


def _gate_at(o, z, w2, head_sum):
    ms = head_sum(o * o) * (1.0 / AT_DIM)
    return (o * lax.rsqrt(ms + EPS)) * w2 * _silu(z)


def _swap_half64(x):
    lane = lax.broadcasted_iota(jnp.int32, x.shape, 1)
    return jnp.where((lane & (AT_DIM - 1)) < AT_DIM // 2, pltpu.roll(x, LANES - AT_DIM // 2, 1),
                     pltpu.roll(x, AT_DIM // 2, 1))


_NN = (((1,), (0,)), ((), ()))


def _hl(a):
    hi = a.astype(BF16)
    return hi, (a - hi.astype(F32)).astype(BF16)


def _mm3(a, b, dims=_NN):
    (ah, al), (bh, bl) = a, b
    f = lambda x, y: lax.dot_general(x, y, dims, preferred_element_type=F32)
    return f(ah, bh) + (f(ah, bl) + f(al, bh))


def _chunk_masks():
    r = lax.broadcasted_iota(jnp.int32, (CHUNK, CHUNK), 0)
    c = lax.broadcasted_iota(jnp.int32, (CHUNK, CHUNK), 1)
    return r >= c, r > c, (r == c).astype(F32), (r // 16) == (c // 16)


def _tri_inv(mats, tick=lambda: None):
    _, _, eye, blk = _chunk_masks()
    dg = [jnp.where(blk, a, 0.0) for a in mats]
    lo = [jnp.where(blk, 0.0, a) for a in mats]
    sdg = [_hl(x) for x in dg]
    d2 = [_mm3(s, s) for s in sdg]
    tick()
    sd2 = [_hl(x) for x in d2]
    d4 = [_mm3(s, s) for s in sd2]
    tick()
    sd4 = [_hl(x) for x in d4]
    d8 = [_mm3(s, s) for s in sd4]
    tick()
    p1 = [_mm3(_hl(eye - a), _hl(eye + b)) for a, b in zip(dg, d2)]
    tick()
    p2 = [_mm3(_hl(a), _hl(eye + b)) for a, b in zip(p1, d4)]
    tick()
    dinv = [_mm3(_hl(a), _hl(eye + b)) for a, b in zip(p2, d8)]
    tick()
    sdinv = [_hl(x) for x in dinv]
    n1 = [_mm3(s, _hl(b)) for s, b in zip(sdinv, lo)]
    tick()
    sn1 = [_hl(x) for x in n1]
    n2 = [_mm3(s, s) for s in sn1]
    tick()
    q1 = [_mm3(_hl(eye - a), _hl(eye + b)) for a, b in zip(n1, n2)]
    return [_mm3(_hl(a), s) for a, s in zip(q1, sdinv)]


def _chunk_common(qs, ks, vs, betas, gcs):
    tril, _, _, _ = _chunk_masks()
    out = []
    for q, k, v, beta, gc in zip(qs, ks, vs, betas, gcs):
        gb = jnp.broadcast_to(gc, (CHUNK, DN_DIM))
        gt = gb.T[:CHUNK, :]
        gam = jnp.where(tril, jnp.exp(jnp.where(tril, gb[:, :CHUNK] - gt, 0.0)), 0.0)
        last = gb[CHUNK - 1:CHUNK, :]
        eg, e2 = jnp.exp(gb), jnp.exp(last - gb)
        kb, vb = k * beta, v * beta
        out.append(dict(gam=gam, eg=eg, e2=e2, gl=jnp.exp(last[:, 0:1]), kb=kb, vb=vb, kbg=kb * eg,
                        m=_nt(kb, k), qk=_nt(q, k)))
    return out


def _chunk_fwd(qs, ks, vs, betas, gcs, tick=lambda: None):
    tril, strict, _, _ = _chunk_masks()
    cm = _chunk_common(qs, ks, vs, betas, gcs)
    ts = _tri_inv([jnp.where(strict, c["m"] * c["gam"], 0.0) for c in cm], tick)
    outs = []
    for q, k, c, t in zip(qs, ks, cm, ts):
        uw = _nn(t, jnp.concatenate([c["vb"], c["kbg"]], axis=1))
        p = jnp.where(tril, c["qk"] * c["gam"], 0.0)
        outs.append((uw[:, :DN_DIM], uw[:, DN_DIM:], p, q * c["eg"], k * c["e2"], c["gl"], t.T))
    return outs


def _chunk_bwd(qs, ks, vs, betas, gcs, ts, cots, tick=lambda: None):
    tril, strict, _, _ = _chunk_masks()
    cm = _chunk_common(qs, ks, vs, betas, gcs)
    tick()
    row = lax.broadcasted_iota(jnp.int32, (CHUNK, 1), 0)
    ones = jnp.ones((CHUNK, DN_DIM), BF16)
    rs = lambda x: jnp.sum(x, axis=-1, keepdims=True)
    tts = [_bf(t) for t in ts]
    duw = [_bf(jnp.concatenate([ct[0], ct[1]], axis=1)) for ct in cots]
    dts = [_nt(a, jnp.concatenate([c["vb"], c["kbg"]], axis=1)) for a, c in zip(duw, cm)]
    tick()
    xs = [_nn(t, d) for t, d in zip(tts, dts)]
    tick()
    das = [jnp.where(strict, -_nn(x, t), 0.0) for x, t in zip(xs, tts)]
    dvks = [_nn(t, a) for t, a in zip(tts, duw)]
    tick()
    outs = []
    every = max(1, len(qs) // 5)
    for idx, (q, k, v, beta, c, ct, da, dvk) in enumerate(zip(qs, ks, vs, betas, cm, cots, das, dvks)):
        if idx and idx % every == 0:
            tick()
        _, _, dp, dqd, dkd, dgl = ct
        dvb, dkbg = dvk[:, :DN_DIM], dvk[:, DN_DIM:]
        dm = da * c["gam"]
        dqk = jnp.where(tril, dp, 0.0) * c["gam"]
        e = dm * c["m"] + dqk * c["qk"]
        dmq = jnp.concatenate([dm, dqk], axis=0)
        r1 = _nn(dmq, k)
        dkb = r1[:CHUNK] + dkbg * c["eg"]
        dq = r1[CHUNK:] + dqd * c["eg"]
        dk = _tn(dmq, jnp.concatenate([c["kb"], q], axis=0)) + dkd * c["e2"] + dkb * beta
        dbeta = rs(dkb * k + dvb * v)
        eh, el = _hl(e)
        colsum = (lax.dot_general(eh, ones, _TN, preferred_element_type=F32)
                  + lax.dot_general(el, ones, _TN, preferred_element_type=F32))[:, 0:1]
        pkd = dkd * (k * c["e2"])
        dgc = rs(e) - colsum + rs(dqd * q * c["eg"] + dkbg * c["kbg"] - pkd)
        tail = rs(jnp.sum(pkd, axis=0, keepdims=True)) + dgl * c["gl"]
        dgc = dgc + jnp.where(row == CHUNK - 1, tail, 0.0)
        outs.append((dq, dk, dvb * beta, dbeta, dgc))
    return outs


def _chunk_cumsum(x, reverse=False):
    n = x.shape[0]
    pos = lax.broadcasted_iota(jnp.int32, x.shape, 0) & (CHUNK - 1)
    sh = 1
    while sh < CHUNK:
        if reverse:
            x = x + jnp.where(pos < CHUNK - sh, pltpu.roll(x, n - sh, 0), 0.0)
        else:
            x = x + jnp.where(pos >= sh, pltpu.roll(x, sh, 0), 0.0)
        sh *= 2
    return x


GC_LANE = 2 * DN_HEADS


def _exchange(arrays, scatter, name):
    n = len(arrays)
    out_shapes = []
    for a, sc in zip(arrays, scatter):
        out_shapes.append(SDS(a.shape if sc else (N_DEV,) + a.shape, a.dtype))

    def body(*refs):
        ins, outs = refs[:n], refs[n:2 * n]
        send_sems, recv_sems, loc_sems = refs[2 * n:]
        x, y, c = lax.axis_index("x"), lax.axis_index("y"), lax.axis_index("c")
        me = 4 * x + 2 * y + c
        local, remote = [], []
        for i in range(n):
            src = ins[i].at[me] if scatter[i] else ins[i]
            cp = pltpu.make_async_copy(src, outs[i].at[me], loc_sems.at[i])
            cp.start()
            local.append(cp)
        for dlt in range(1, N_DEV):
            px = 1 - x if dlt & 4 else x
            py = 1 - y if dlt & 2 else y
            pc = 1 - c if dlt & 1 else c
            peer = 4 * px + 2 * py + pc
            for i in range(n):
                src = ins[i].at[peer] if scatter[i] else ins[i]
                cp = pltpu.make_async_remote_copy(
                    src_ref=src, dst_ref=outs[i].at[me],
                    send_sem=send_sems.at[i, dlt - 1], recv_sem=recv_sems.at[i, dlt - 1],
                    device_id=(px, py, pc), device_id_type=pl.DeviceIdType.MESH)
                cp.start()
                arrive = pltpu.make_async_remote_copy(
                    src_ref=src, dst_ref=outs[i].at[peer],
                    send_sem=send_sems.at[i, dlt - 1], recv_sem=recv_sems.at[i, dlt - 1],
                    device_id=(px, py, pc), device_id_type=pl.DeviceIdType.MESH)
                remote.append((cp, arrive))
        for cp, arrive in remote:
            cp.wait_send()
            arrive.wait_recv()
        for cp in local:
            cp.wait()

    any_spec = pl.BlockSpec(memory_space=pl.ANY)
    return pl.pallas_call(
        body, name=name, out_shape=tuple(out_shapes),
        in_specs=[any_spec] * n, out_specs=tuple([any_spec] * n),
        scratch_shapes=[pltpu.SemaphoreType.DMA((n, N_DEV - 1)), pltpu.SemaphoreType.DMA((n, N_DEV - 1)),
                        pltpu.SemaphoreType.DMA((n,))],
    )(*arrays)


def _all_gather(arrays, name):
    n = len(arrays)

    def body(*refs):
        ins, outs = refs[:n], refs[n:2 * n]
        send_sems, recv_sems, loc_sems = refs[2 * n:]
        x, y, c = lax.axis_index("x"), lax.axis_index("y"), lax.axis_index("c")
        me, sibling = (x, y, c), (x, y, 1 - c)
        chips = [(1 - x, y), (x, 1 - y), (1 - x, 1 - y)]

        def copy(i, k, block, to, src=None):
            slot = outs[i].at[4 * block[0] + 2 * block[1] + block[2]]
            return pltpu.make_async_remote_copy(
                src_ref=slot if src is None else src, dst_ref=slot,
                send_sem=send_sems.at[i, k], recv_sem=recv_sems.at[i, k],
                device_id=to, device_id_type=pl.DeviceIdType.MESH)

        mine = [pltpu.make_async_copy(ins[i], outs[i].at[4 * x + 2 * y + c], loc_sems.at[i]) for i in range(n)]
        for cp in mine:
            cp.start()
        first = []
        for i in range(n):
            first.append(copy(i, 0, me, sibling, src=ins[i]))
            first += [copy(i, 1 + j, me, (*chip, c), src=ins[i]) for j, chip in enumerate(chips)]
        for cp in first:
            cp.start()
        passed = []
        for j, chip in enumerate(chips):
            for i in range(n):
                copy(i, 1 + j, (*chip, c), me).wait_recv()
                fwd = copy(i, 4 + j, (*chip, c), sibling)
                fwd.start()
                passed.append(fwd)
        for i in range(n):
            copy(i, 0, sibling, me).wait_recv()
        for j, chip in enumerate(chips):
            for i in range(n):
                copy(i, 4 + j, (*chip, 1 - c), me).wait_recv()
        for cp in first + passed:
            cp.wait_send()
        for cp in mine:
            cp.wait()

    any_spec = pl.BlockSpec(memory_space=pl.ANY)
    return pl.pallas_call(
        body, name=name, out_shape=tuple(SDS((N_DEV,) + a.shape, a.dtype) for a in arrays),
        in_specs=[any_spec] * n, out_specs=tuple([any_spec] * n),
        scratch_shapes=[pltpu.SemaphoreType.DMA((n, N_DEV - 1)), pltpu.SemaphoreType.DMA((n, N_DEV - 1)),
                        pltpu.SemaphoreType.DMA((n,))],
    )(*arrays)


_HBM = pl.BlockSpec(memory_space=pltpu.HBM)
_SEM = pl.BlockSpec(memory_space=pltpu.SEMAPHORE)


def _peers(x, y, c):
    out = []
    for dlt in range(1, N_DEV):
        px = 1 - x if dlt & 4 else x
        py = 1 - y if dlt & 2 else y
        pc = 1 - c if dlt & 1 else c
        out.append((dlt, (px, py, pc), 4 * px + 2 * py + pc))
    return out


def _scatter_start(arrays):
    n = len(arrays)
    ns = n * (N_DEV - 1)

    def body(*refs):
        ins, lands = refs[:n], refs[n:2 * n]
        send_sems, recv_sems = refs[2 * n:2 * n + ns], refs[2 * n + ns:2 * n + 2 * ns]
        token = refs[-1]
        x, y, c = lax.axis_index("x"), lax.axis_index("y"), lax.axis_index("c")
        me = 4 * x + 2 * y + c
        for dlt, peer, pi in _peers(x, y, c):
            for i in range(n):
                k = i * (N_DEV - 1) + dlt - 1
                pltpu.make_async_remote_copy(
                    src_ref=ins[i].at[pi], dst_ref=lands[i].at[me], send_sem=send_sems[k], recv_sem=recv_sems[k],
                    device_id=peer, device_id_type=pl.DeviceIdType.MESH).start()
        token[...] = jnp.zeros_like(token)

    sem = pltpu.SemaphoreType.DMA(())
    thru = tuple(pltpu.HBM(a.shape, a.dtype) for a in arrays)
    hbm = lambda a: pltpu.with_memory_space_constraint(a, pltpu.HBM)
    outs = pl.pallas_call(
        body, name="scatter_start", out_shape=(sem,) * (2 * ns) + thru + thru + (SDS((8, LANES), F32),),
        in_specs=[_HBM] * (2 * n),
        out_specs=(_SEM,) * (2 * ns) + (_HBM,) * (2 * n) + (pl.BlockSpec(memory_space=pltpu.VMEM),),
        input_output_aliases={i: 2 * ns + i for i in range(2 * n)},
        compiler_params=pltpu.CompilerParams(has_side_effects=pltpu.SideEffectType.DATAFLOW_SIDE_EFFECTING),
    )(*[hbm(a) for a in arrays], *[hbm(jnp.zeros(a.shape, a.dtype)) for a in arrays])
    return outs[:ns], outs[ns:2 * ns], outs[2 * ns:2 * ns + n], outs[2 * ns + n:2 * ns + 2 * n], outs[-1]


def _scatter_wait(send_sems, recv_sems, srcs, lands, after):
    n = len(srcs)
    ns = n * (N_DEV - 1)

    def body(*refs):
        ins, lands_ = refs[:n], refs[n:2 * n]
        send, recv = refs[2 * n:2 * n + ns], refs[2 * n + ns:2 * n + 2 * ns]
        x, y, c = lax.axis_index("x"), lax.axis_index("y"), lax.axis_index("c")
        for dlt, peer, pi in _peers(x, y, c):
            for i in range(n):
                k = i * (N_DEV - 1) + dlt - 1
                cp = pltpu.make_async_remote_copy(
                    src_ref=ins[i].at[pi], dst_ref=lands_[i].at[pi], send_sem=send[k], recv_sem=recv[k],
                    device_id=peer, device_id_type=pl.DeviceIdType.MESH)
                cp.wait_send()
                cp.wait_recv()

    thru = tuple(pltpu.HBM(a.shape, a.dtype) for a in srcs)
    outs = pl.pallas_call(
        body, name="scatter_wait", out_shape=thru + thru,
        in_specs=[_HBM] * (2 * n) + [_SEM] * (2 * ns) + [pl.BlockSpec(memory_space=pl.ANY)],
        out_specs=(_HBM,) * (2 * n), input_output_aliases={i: i for i in range(2 * n)},
        compiler_params=pltpu.CompilerParams(has_side_effects=pltpu.SideEffectType.DATAFLOW_SIDE_EFFECTING),
    )(*srcs, *lands, *send_sems, *recv_sems, after)
    return outs[n:]


def _adaln_mod(c, w_mod, b_mod):
    def body(c_ref, w_ref, b_ref, mod_ref, sc_ref):
        sc = _silu(c_ref[...])
        sc8 = jnp.broadcast_to(sc, (8, D_MODEL))
        mod_ref[...] = _nn(sc8, w_ref[...])[0:1] + b_ref[...]
        sc_ref[...] = sc

    return pl.pallas_call(body, name="adaln_mod", compiler_params=_params(),
                          out_shape=(SDS((1, 3 * D_MODEL), F32), SDS((1, D_MODEL), F32)))(c, w_mod, b_mod)


def _ln_proj(x, mod, norm_w, ws, cos_t, sin_t, conv_w8, alog_row, dtb_row, ts):
    s = x.shape[0]
    widths = [w.shape[1] for w in ws]
    assert ts == PERM_BLK

    def body(x_ref, mod_ref, nw_ref, cos_ref, sin_ref, cw_ref, al_ref, dtb_ref, wqkv, wz, wba, waq, wak, wav, waz,
             h_ref, oqkv, oz, oba, oq, ok, ov, oaz, q_ref, k_ref, v_ref, bg_ref, halo):
        n = pl.program_id(0)
        xt = x_ref[...]
        r = lax.rsqrt(jnp.mean(xt * xt, axis=-1, keepdims=True) + EPS)
        shift, scale = mod_ref[:, 0:D_MODEL], mod_ref[:, D_MODEL:2 * D_MODEL]
        h = ((xt * r) * nw_ref[...]) * (1.0 + scale) + shift
        hb = _bf(h)
        h_ref[...] = hb
        hp = jnp.dot(_plane_perm(ts, False), hb, preferred_element_type=F32)
        pre = jnp.dot(hb, wqkv[...], preferred_element_type=F32)
        ba = jnp.dot(hb, wba[...], preferred_element_type=F32)
        hp = _bf(hp)
        tq = jnp.dot(hp, waq[...], preferred_element_type=F32)
        tk = jnp.dot(hp, wak[...], preferred_element_type=F32)
        tv = jnp.dot(hp, wav[...], preferred_element_type=F32)
        tz = jnp.dot(hb, wz[...], preferred_element_type=F32)
        taz = jnp.dot(hb, waz[...], preferred_element_type=F32)
        cs, sn = cos_ref[...], sin_ref[...]
        rows = ts // PLANES
        for t, o_ref in ((tq, oq), (tk, ok)):
            for j in range(AT_PAIRS):
                tj = t[:, j * LANES:(j + 1) * LANES]
                rot = tj * cs + _swap_half64(tj) * sn
                for r in range(PLANES):
                    o_ref[j, r] = rot[r * rows:(r + 1) * rows]
        oqkv[...] = pre
        ext = jnp.concatenate([jnp.where(n == 0, 0.0, halo[...]), pre], axis=0)
        halo[...] = pre[ts - 8:ts]
        taps = _conv_taps(ext, ts)
        conv = taps[0] * cw_ref[0:1, :]
        for j in range(1, CONV_K):
            conv = conv + taps[j] * cw_ref[j:j + 1, :]
        for hd in range(DN_HEADS):
            cols = slice(hd * DN_DIM, (hd + 1) * DN_DIM)
            q_ref[:, cols] = _post_q(conv[:, hd * DN_DIM:(hd + 1) * DN_DIM])
            k_ref[:, cols] = _post_k(conv[:, DN_WIDTH + hd * DN_DIM:DN_WIDTH + (hd + 1) * DN_DIM])
            v_ref[:, cols] = _post_v(conv[:, 2 * DN_WIDTH + hd * DN_DIM:2 * DN_WIDTH + (hd + 1) * DN_DIM])
        oba[...] = ba
        bg = _beta_decay(ba, al_ref[...], dtb_ref[...])
        lane = lax.broadcasted_iota(jnp.int32, bg.shape, 1)
        run = pltpu.roll(_chunk_cumsum(bg), DN_HEADS, 1)
        bg_ref[...] = jnp.where((lane >= GC_LANE) & (lane < GC_LANE + DN_HEADS), run, bg)
        for j in range(AT_PAIRS):
            for r in range(PLANES):
                ov[j, r] = tv[r * rows:(r + 1) * rows, j * LANES:(j + 1) * LANES]
        oz[...] = tz
        oaz[...] = taz

    tok = lambda w: pl.BlockSpec((ts, w), lambda i: (i, 0))
    full = lambda a: pl.BlockSpec(a.shape, lambda i: (0, 0))
    pairs = pl.BlockSpec((AT_PAIRS, PLANES, ts // PLANES, LANES), lambda i: (0, 0, i, 0))
    return pl.pallas_call(
        body, name="ln_proj", grid=(s // ts,), compiler_params=_params("arbitrary"),
        in_specs=[tok(D_MODEL), full(mod), full(norm_w), tok(LANES), tok(LANES), full(conv_w8), full(alog_row),
                  full(dtb_row)] + [full(w) for w in ws],
        out_specs=(tok(D_MODEL), tok(widths[0]), tok(widths[1]), tok(widths[2]), pairs, pairs, pairs,
                   tok(widths[6]), tok(DN_WIDTH), tok(DN_WIDTH), tok(DN_WIDTH), tok(BA_PAD)),
        out_shape=(SDS((s, D_MODEL), BF16), SDS((s, widths[0]), F32), SDS((s, widths[1]), F32),
                   SDS((s, widths[2]), F32)) + (SDS((AT_PAIRS, PLANES, s // PLANES, LANES), F32),) * 3 + (SDS((s, widths[6]), F32),)
        + (SDS((s, DN_WIDTH), F32),) * 3 + (SDS((s, BA_PAD), F32),),
        scratch_shapes=[pltpu.VMEM((8, widths[0]), F32)],
    )(x, mod, norm_w, cos_t, sin_t, conv_w8, alog_row, dtb_row, *ws)


def _conv_taps(ext, rows):
    taps = []
    for j in range(CONV_K):
        sh = CONV_K - 1 - j
        rolled = pltpu.roll(ext, sh, 0) if sh else ext
        taps.append(rolled[8:8 + rows])
    return taps


def _dn_forward(q, k, v, bg):
    s = q.shape[0]
    tp = CH_UNROLL * CHUNK
    npass = s // tp
    hs = range(DN_HEADS)
    sl = [slice(h * DN_DIM, (h + 1) * DN_DIM) for h in hs]

    def body(q_ref, k_ref, v_ref, bg_ref, w_ref, qd_ref, kd_ref, p_ref, gl_ref, t_ref, o_ref, vn_ref, st_ref,
             state, u_s, w_s, qd_s, kd_s, p_s, gl_s):
        @pl.when(pl.program_id(0) == 0)
        def _():
            for ref in (state, u_s, w_s, qd_s, kd_s, p_s, gl_s):
                ref[...] = jnp.zeros_like(ref)

        def recurrence():
            for c in range(CH_UNROLL):
                rows = slice(c * CHUNK, (c + 1) * CHUNK)
                rows8 = slice(c * 8, (c + 1) * 8)
                srows = slice(c * DN_DIM, (c + 1) * DN_DIM)
                sf = [state[h] for h in hs]
                sb = [_bf(x) for x in sf]
                ws = [_nn(w_s[rows, cl], b) for cl, b in zip(sl, sb)]
                qs = [_nn(qd_s[rows, cl], b) for cl, b in zip(sl, sb)]
                yield
                vn = [u_s[rows, cl] - x for cl, x in zip(sl, ws)]
                vb = [_bf(x) for x in vn]
                kv = [_tn(kd_s[rows, cl], b) for cl, b in zip(sl, vb)]
                pv = [_nn(p_s[h, rows, :], b) for h, b in zip(hs, vb)]
                for h in hs:
                    state[h] = sf[h] * gl_s[rows8, sl[h]][0:1] + kv[h]
                for h in hs:
                    st_ref[srows, sl[h]] = sf[h]
                    vn_ref[rows, sl[h]] = vn[h]
                    o_ref[rows, sl[h]] = qs[h] + pv[h]
                yield

        steps = recurrence()

        where = [(slice(c * CHUNK, (c + 1) * CHUNK), slice(c * 8, (c + 1) * 8), h, sl[h])
                 for c in range(CH_UNROLL) for h in hs]
        bgs = [bg_ref[rows, :] for rows, _, _, _ in where]
        outs = _chunk_fwd([q_ref[rows, cl] for rows, _, _, cl in where], [k_ref[rows, cl] for rows, _, _, cl in where],
                          [v_ref[rows, cl] for rows, _, _, cl in where],
                          [b[:, h:h + 1] for b, (_, _, h, _) in zip(bgs, where)],
                          [b[:, GC_LANE + h:GC_LANE + h + 1] for b, (_, _, h, _) in zip(bgs, where)],
                          tick=lambda: next(steps, None))
        for _ in steps:
            pass
        for (rows, rows8, h, cl), (u, w, p, qd, kd, gl, t) in zip(where, outs):
            g8 = jnp.broadcast_to(gl, (8, DN_DIM))
            u_s[rows, cl] = u
            w_ref[rows, cl] = w
            w_s[rows, cl] = w
            qd_ref[rows, cl] = qd
            qd_s[rows, cl] = qd
            kd_ref[rows, cl] = kd
            kd_s[rows, cl] = kd
            p_ref[h, rows, :] = p
            p_s[h, rows, :] = p
            gl_ref[rows8, cl] = g8
            gl_s[rows8, cl] = g8
            t_ref[h, rows, :] = t

    cur = lambda i: jnp.minimum(i, npass - 1)
    done = lambda i: jnp.maximum(i - 1, 0)
    tokc = pl.BlockSpec((tp, DN_WIDTH), lambda i: (cur(i), 0))
    tokd = pl.BlockSpec((tp, DN_WIDTH), lambda i: (done(i), 0))
    sq = pl.BlockSpec((DN_HEADS, tp, CHUNK), lambda i: (0, cur(i), 0))
    return pl.pallas_call(
        body, name="dn_forward", grid=(npass + 1,), compiler_params=_params("arbitrary"),
        in_specs=[tokc] * 3 + [pl.BlockSpec((tp, BA_PAD), lambda i: (cur(i), 0))],
        out_specs=(tokc, tokc, tokc, sq, pl.BlockSpec((CH_UNROLL * 8, DN_WIDTH), lambda i: (cur(i), 0)), sq,
                   tokd, tokd, pl.BlockSpec((CH_UNROLL * DN_DIM, DN_WIDTH), lambda i: (done(i), 0))),
        out_shape=(SDS((s, DN_WIDTH), F32),) * 3 + (SDS((DN_HEADS, s, CHUNK), F32),
                                                     SDS((s // CHUNK * 8, DN_WIDTH), F32),
                                                     SDS((DN_HEADS, s, CHUNK), F32),
                                                     SDS((s, DN_WIDTH), F32), SDS((s, DN_WIDTH), F32),
                                                     SDS((s // CHUNK * DN_DIM, DN_WIDTH), F32)),
        scratch_shapes=[pltpu.VMEM((DN_HEADS, DN_DIM, DN_DIM), F32)] + [pltpu.VMEM((tp, DN_WIDTH), F32)] * 4
        + [pltpu.VMEM((DN_HEADS, tp, CHUNK), F32), pltpu.VMEM((CH_UNROLL * 8, DN_WIDTH), F32)],
    )(q, k, v, bg)


LOG2E, LN2 = 1.4426950408889634, 0.6931471805599453
MASKED = -1e30


PLANE_ROWS = ATT_BLK // PLANES


def _to_planes(tile, scr):
    scr[...] = tile
    return [scr[pl.ds(r, tile.shape[0] // PLANES, stride=PLANES), :] for r in range(PLANES)]


def _from_planes(planes, scr):
    n = planes[0].shape[0]
    for r in range(PLANES):
        scr[pl.ds(r, n, stride=PLANES), :] = planes[r]
    return scr[...]


def _plane_perm(n, back):
    row = lax.broadcasted_iota(jnp.int32, (n, n), 0)
    col = lax.broadcasted_iota(jnp.int32, (n, n), 1)
    m, c = (col, row) if back else (row, col)
    return jnp.where(c == PLANES * (m % (n // PLANES)) + m // (n // PLANES), 1.0, 0.0).astype(BF16)


def _geom(d):
    nchunk = PLANES // d
    return nchunk, Q_BLOCK // nchunk


def _pattern_bias(d):
    nchunk, qlen = _geom(d)
    row = lax.broadcasted_iota(jnp.int32, (Q_BLOCK, 2 * Q_BLOCK), 0)
    col = lax.broadcasted_iota(jnp.int32, (Q_BLOCK, 2 * Q_BLOCK), 1)
    uq, aq = row // qlen, row % qlen
    uk, ak = col // (2 * qlen), col % (2 * qlen)
    rel = nchunk * (aq - ak + qlen) + (uq - uk)
    band = jnp.where((rel >= 0) & (rel <= W_SUB), 0.0, MASKED)
    col1 = lax.broadcasted_iota(jnp.int32, (1, 2 * Q_BLOCK), 1)
    return band, (col1 % (2 * qlen)) < qlen


def _combo(c, d):
    nchunk, qlen = _geom(d)
    r0, mm = c % d, c // d
    planes = [r0 + d * u for u in range(nchunk)]
    qs = pl.multiple_of(qlen * mm, 8)
    ks = pl.multiple_of(PLANE_ROWS + qlen * mm - qlen, 8)
    return planes, qs, ks, qlen, mm == 0


def _gather(ref, lead, planes, start, n):
    parts = [ref[lead + (p, pl.ds(start, n), slice(None))] for p in planes]
    return parts[0] if len(parts) == 1 else jnp.concatenate(parts, axis=0)


def _scatter(ref, lead, planes, start, n, val, add):
    for u, p in enumerate(planes):
        idx = lead + (p, pl.ds(start, n), slice(None))
        if add:
            ref[idx] += val[u * n:(u + 1) * n]
        else:
            ref[idx] = val[u * n:(u + 1) * n]


def _shift_in(ext, cur, n):
    @pl.when(n == 0)
    def _():
        ext[:, 0:PLANE_ROWS, :] = jnp.zeros((PLANES, PLANE_ROWS, LANES), F32)

    @pl.when(n > 0)
    def _():
        ext[:, 0:PLANE_ROWS, :] = ext[:, PLANE_ROWS:2 * PLANE_ROWS, :]

    ext[:, PLANE_ROWS:2 * PLANE_ROWS, :] = cur


def _attn_fwd(qr, kr, vv):
    s16 = qr.shape[2]
    nblk = s16 // PLANE_ROWS
    scale = AT_DIM ** -0.5
    npat = len(DILATIONS)

    def body(q_ref, k_ref, v_ref, o_ref, lse_ref, kext, vext, o_p, l_p):
        n = pl.program_id(1)
        _shift_in(kext, k_ref[0], n)
        _shift_in(vext, v_ref[0], n)
        lo = lax.broadcasted_iota(jnp.int32, (Q_BLOCK, LANES), 1) < AT_DIM
        for pi, d in enumerate(DILATIONS):
            band, prev_cols = _pattern_bias(d)

            def group(g, carry, pi=pi, d=d, band=band, prev_cols=prev_cols):
                cs = [_combo(g * ATT_UNROLL + u, d) for u in range(ATT_UNROLL)]
                heads = [(i, sel) for i in range(ATT_UNROLL) for sel in (lo, ~lo)]
                bias = [band + jnp.where(prev_cols & ((n == 0) & m0), MASKED, 0.0) for _, _, _, _, m0 in cs]
                qb = [_bf(_gather(q_ref, (0,), pls, qs, ql)) for pls, qs, _, ql, _ in cs]
                kk = [_bf(_gather(kext, (), pls, ks, 2 * ql)) for pls, _, ks, ql, _ in cs]
                vb = [_bf(_gather(vext, (), pls, ks, 2 * ql)) for pls, _, ks, ql, _ in cs]
                sc = [lax.dot_general(jnp.where(sel, qb[i], jnp.zeros_like(qb[i])), kk[i], _NT,
                                      preferred_element_type=F32) for i, sel in heads]
                sc = [x * (scale * LOG2E) + bias[i] for x, (i, _) in zip(sc, heads)]
                mx = [jnp.max(x, axis=-1, keepdims=True) for x in sc]
                pr = [jnp.exp2(x - m) for x, m in zip(sc, mx)]
                ls = [jnp.sum(x, axis=-1, keepdims=True) for x in pr]
                pv = [jnp.dot(_bf(x), vb[i], preferred_element_type=F32) for x, (i, _) in zip(pr, heads)]
                outs = [x / l for x, l in zip(pv, ls)]
                lses = [m * LN2 + jnp.log(l) for m, l in zip(mx, ls)]
                for i, (pls, qs, _, ql, _) in enumerate(cs):
                    _scatter(o_p, (pi,), pls, qs, ql, jnp.where(lo, outs[2 * i], outs[2 * i + 1]), False)
                    _scatter(l_p, (pi,), pls, qs, ql, jnp.where(lo, lses[2 * i], lses[2 * i + 1]), False)
                return carry

            lax.fori_loop(0, ATT_BLK // Q_BLOCK // ATT_UNROLL, group, 0)

        def merge(r, carry):
            ls = [l_p[pi, r] for pi in range(npat)]
            mx = jnp.maximum(jnp.maximum(ls[0], ls[1]), ls[2])
            es = [jnp.exp(l - mx) for l in ls]
            den = es[0] + es[1] + es[2]
            o_ref[0, r] = (es[0] * o_p[0, r] + es[1] * o_p[1, r] + es[2] * o_p[2, r]) / den
            lse_ref[0, r] = mx + jnp.log(den)
            return carry

        lax.fori_loop(0, PLANES, merge, 0)

    blk = pl.BlockSpec((1, PLANES, PLANE_ROWS, LANES), lambda j, n: (j, 0, n, 0))
    return pl.pallas_call(
        body, name="attn_fwd", grid=(AT_PAIRS, nblk), compiler_params=_params("arbitrary", "arbitrary"),
        in_specs=[blk] * 3, out_specs=(blk, blk),
        out_shape=(SDS(qr.shape, F32),) * 2,
        scratch_shapes=[pltpu.VMEM((PLANES, 2 * PLANE_ROWS, LANES), F32)] * 2
        + [pltpu.VMEM((npat, PLANES, PLANE_ROWS, LANES), F32)] * 2,
    )(qr, kr, vv)


def _out_loss(o_dn, z_dn, o_at, z_at, dnw, atw2, x, tgt, w_out, gate, fw, ts):
    s = x.shape[0]

    def body(odn, zdn, oat, zat, dnw_ref, atw_ref, x_ref, t_ref, w_ref, g_ref, fw_ref,
             dx2_ref, gw_ref, dfw_ref, dgate_ref, loss_ref, dodn, dzdn, doat, dzat, delta, ddnw, datw, perm):
        @pl.when(pl.program_id(0) == 0)
        def _():
            for ref in (gw_ref, dfw_ref, dgate_ref, loss_ref, ddnw, datw):
                ref[...] = jnp.zeros_like(ref)

        parts, vjps = [], []
        for h in range(DN_HEADS):
            cols = slice(h * DN_DIM, (h + 1) * DN_DIM)
            y, vjp = jax.vjp(_gate_dn, odn[:, cols], zdn[:, cols], dnw_ref[...])
            parts.append(_bf(y))
            vjps.append(vjp)
        oats = [_from_planes([oat[j, r] for r in range(PLANES)], perm.at[j]) for j in range(AT_PAIRS)]
        for j in range(AT_PAIRS):
            y, vjp = jax.vjp(functools.partial(_gate_at, head_sum=_d_head_sum), oats[j],
                             zat[:, j * LANES:(j + 1) * LANES], atw_ref[...])
            parts.append(_bf(y))
            vjps.append(vjp)
        catb = jnp.concatenate(parts, axis=1)
        wb = w_ref[...]
        gate, fwv = g_ref[...], fw_ref[...]
        mix = jnp.dot(catb, wb, preferred_element_type=F32)
        x2 = x_ref[...] + gate * mix
        r2 = lax.rsqrt(jnp.mean(x2 * x2, axis=-1, keepdims=True) + EPS)
        xn2 = x2 * r2
        err = xn2 * fwv - t_ref[...]
        row = jnp.sum(err * err, axis=-1, keepdims=True) * (1.0 / D_MODEL)
        loss_ref[...] += 0.5 * jnp.sum(row, axis=0, keepdims=True)
        dy = err * (1.0 / D_MODEL)
        dfw_ref[...] += jnp.sum(dy * xn2, axis=0, keepdims=True)
        dxn = dy * fwv
        dx2 = r2 * (dxn - xn2 * jnp.mean(dxn * xn2, axis=-1, keepdims=True))
        dx2_ref[...] = dx2
        dgate_ref[...] += jnp.sum(dx2 * mix, axis=0, keepdims=True)
        dmix = _bf(gate * dx2)
        dcat = lax.dot_general(dmix, wb, _NT, preferred_element_type=F32)
        gw_ref[...] += lax.dot_general(catb, dmix, _TN, preferred_element_type=F32)
        for h in range(DN_HEADS):
            cols = slice(h * DN_DIM, (h + 1) * DN_DIM)
            do, dz, dw = vjps[h](dcat[:, cols])
            dodn[:, cols] = do
            dzdn[:, cols] = _bf(dz)
            ddnw[...] += dw
        for j in range(AT_PAIRS):
            cols = slice(j * LANES, (j + 1) * LANES)
            do, dz, dw = vjps[DN_HEADS + j](dcat[:, DN_WIDTH + j * LANES:DN_WIDTH + (j + 1) * LANES])
            for r, x in enumerate(_to_planes(do, perm.at[j])):
                doat[j, r] = x
            dzat[:, cols] = _bf(dz)
            datw[...] += dw
            for r, x in enumerate(_to_planes(_head_sum(do * oats[j]), perm.at[j])):
                delta[j, r] = x

    tok = lambda w: pl.BlockSpec((ts, w), lambda i: (i, 0))
    full = lambda a: pl.BlockSpec(a.shape, lambda i: (0, 0))
    row = pl.BlockSpec((1, D_MODEL), lambda i: (0, 0))
    lrow = pl.BlockSpec((1, LANES), lambda i: (0, 0))
    pairs = pl.BlockSpec((AT_PAIRS, PLANES, ts // PLANES, LANES), lambda i: (0, 0, i, 0))
    return pl.pallas_call(
        body, name="out_loss", grid=(s // ts,), compiler_params=_params("arbitrary"),
        in_specs=[tok(DN_WIDTH), tok(DN_WIDTH), pairs, tok(AT_WIDTH), full(dnw), full(atw2),
                  tok(D_MODEL), tok(D_MODEL), full(w_out), full(gate), full(fw)],
        out_specs=(tok(D_MODEL), pl.BlockSpec((D_MODEL, D_MODEL), lambda i: (0, 0)), row, row,
                   pl.BlockSpec((1, 1), lambda i: (0, 0)), tok(DN_WIDTH), tok(DN_WIDTH), pairs, tok(AT_WIDTH), pairs,
                   lrow, lrow),
        out_shape=(SDS((s, D_MODEL), F32), SDS((D_MODEL, D_MODEL), F32), SDS((1, D_MODEL), F32),
                   SDS((1, D_MODEL), F32), SDS((1, 1), F32), SDS((s, DN_WIDTH), F32), SDS((s, DN_WIDTH), BF16),
                   SDS((AT_PAIRS, PLANES, s // PLANES, LANES), F32), SDS((s, AT_WIDTH), BF16),
                   SDS((AT_PAIRS, PLANES, s // PLANES, LANES), F32), SDS((1, LANES), F32), SDS((1, LANES), F32)),
        scratch_shapes=[pltpu.VMEM((AT_PAIRS, ts, LANES), F32)],
    )(o_dn, z_dn, o_at, z_at, dnw, atw2, x, tgt, w_out, gate, fw)


def _shift_acc(ext, n):
    @pl.when(n == 0)
    def _():
        ext[:, 0:PLANE_ROWS, :] = jnp.zeros((PLANES, PLANE_ROWS, LANES), F32)

    @pl.when(n > 0)
    def _():
        ext[:, 0:PLANE_ROWS, :] = ext[:, PLANE_ROWS:2 * PLANE_ROWS, :]

    ext[:, PLANE_ROWS:2 * PLANE_ROWS, :] = jnp.zeros((PLANES, PLANE_ROWS, LANES), F32)


def _attn_bwd(qr, kr, vv, do, lse, delta):
    s16 = qr.shape[2]
    nblk = s16 // PLANE_ROWS
    scale = AT_DIM ** -0.5

    def body(q_ref, k_ref, v_ref, do_ref, lse_ref, dl_ref, dq_ref, dk_ref, dv_ref, kext, vext, dkext, dvext):
        n = pl.program_id(1)
        _shift_in(kext, k_ref[0], n)
        _shift_in(vext, v_ref[0], n)
        _shift_acc(dkext, n)
        _shift_acc(dvext, n)

        @pl.when(n < nblk)
        def _():
            dq_ref[0] = jnp.zeros((PLANES, PLANE_ROWS, LANES), F32)
            lo = lax.broadcasted_iota(jnp.int32, (Q_BLOCK, LANES), 1) < AT_DIM
            for d in DILATIONS:
                band, prev_cols = _pattern_bias(d)

                def group(g, carry, d=d, band=band, prev_cols=prev_cols):
                    nu = ATT_UNROLL_BWD
                    cs = [_combo(g * nu + u, d) for u in range(nu)]
                    heads = [(i, sel) for i in range(nu) for sel in (lo, ~lo)]
                    bias = [band + jnp.where(prev_cols & ((n == 0) & m0), MASKED, 0.0) for _, _, _, _, m0 in cs]
                    qb = [_bf(_gather(q_ref, (0,), pls, qs, ql)) for pls, qs, _, ql, _ in cs]
                    dob = [_bf(_gather(do_ref, (0,), pls, qs, ql)) for pls, qs, _, ql, _ in cs]
                    kk = [_bf(_gather(kext, (), pls, ks, 2 * ql)) for pls, _, ks, ql, _ in cs]
                    vb = [_bf(_gather(vext, (), pls, ks, 2 * ql)) for pls, _, ks, ql, _ in cs]
                    lse2 = [_gather(lse_ref, (0,), pls, qs, ql) * LOG2E for pls, qs, _, ql, _ in cs]
                    dl2 = [_gather(dl_ref, (0,), pls, qs, ql) for pls, qs, _, ql, _ in cs]
                    qm = [jnp.where(sel, qb[i], jnp.zeros_like(qb[i])) for i, sel in heads]
                    dom = [jnp.where(sel, dob[i], jnp.zeros_like(dob[i])) for i, sel in heads]
                    lse_c = [jnp.max(jnp.where(sel, lse2[i], -jnp.inf), axis=-1, keepdims=True) for i, sel in heads]
                    dl_c = [jnp.max(jnp.where(sel, dl2[i], -jnp.inf), axis=-1, keepdims=True) for i, sel in heads]
                    sc = [lax.dot_general(a, kk[i], _NT, preferred_element_type=F32) for a, (i, _) in zip(qm, heads)]
                    dp = [lax.dot_general(a, vb[i], _NT, preferred_element_type=F32) for a, (i, _) in zip(dom, heads)]
                    pr = [jnp.exp2(x * (scale * LOG2E) + bias[i] - l) for x, l, (i, _) in zip(sc, lse_c, heads)]
                    ds = [_bf(p * (x - dl) * scale) for p, x, dl in zip(pr, dp, dl_c)]
                    prb = [_bf(p) for p in pr]
                    dq = [jnp.dot(x, kk[i], preferred_element_type=F32) for x, (i, _) in zip(ds, heads)]
                    dk = [lax.dot_general(x, a, _TN, preferred_element_type=F32) for x, a in zip(ds, qm)]
                    dv = [lax.dot_general(x, a, _TN, preferred_element_type=F32) for x, a in zip(prb, dom)]
                    for i, (pls, qs, ks, ql, _) in enumerate(cs):
                        _scatter(dq_ref, (0,), pls, qs, ql, jnp.where(lo, dq[2 * i], dq[2 * i + 1]), True)
                        _scatter(dkext, (), pls, ks, 2 * ql, dk[2 * i] + dk[2 * i + 1], True)
                        _scatter(dvext, (), pls, ks, 2 * ql, dv[2 * i] + dv[2 * i + 1], True)
                    return carry

                lax.fori_loop(0, ATT_BLK // Q_BLOCK // ATT_UNROLL_BWD, group, 0)

        dk_ref[0] = dkext[:, 0:PLANE_ROWS, :]
        dv_ref[0] = dvext[:, 0:PLANE_ROWS, :]

    cur = pl.BlockSpec((1, PLANES, PLANE_ROWS, LANES), lambda j, n: (j, 0, jnp.minimum(n, nblk - 1), 0))
    done = pl.BlockSpec((1, PLANES, PLANE_ROWS, LANES), lambda j, n: (j, 0, jnp.maximum(n - 1, 0), 0))
    return pl.pallas_call(
        body, name="attn_bwd", grid=(AT_PAIRS, nblk + 1), compiler_params=_params("arbitrary", "arbitrary"),
        in_specs=[cur] * 6, out_specs=(cur, done, done),
        out_shape=(SDS(qr.shape, F32),) * 3,
        scratch_shapes=[pltpu.VMEM((PLANES, 2 * PLANE_ROWS, LANES), F32)] * 4,
    )(qr, kr, vv, do, lse, delta)


def _dn_backward(do, st, vn, w, qd, kd, p, gl, q, k, v, bg, t):
    s = do.shape[0]
    nc = CH_UNROLL_BWD
    tp = nc * CHUNK
    npass = s // tp
    hs = range(DN_HEADS)
    sl = [slice(h * DN_DIM, (h + 1) * DN_DIM) for h in hs]

    def body(do_ref, st_ref, vn_ref, w_ref, qd_ref, kd_ref, p_ref, gl_ref, q_ref, k_ref, v_ref, bg_ref, t_ref,
             dq_ref, dk_ref, dv_ref, dbg_ref, dstate, du_s, dw_s, dqd_s, dkd_s, dp_s, dgl_s):
        @pl.when(pl.program_id(0) == 0)
        def _():
            for ref in (dstate, du_s, dw_s, dqd_s, dkd_s, dp_s, dgl_s):
                ref[...] = jnp.zeros_like(ref)

        where = [(slice(c * CHUNK, (c + 1) * CHUNK), slice(c * 8, (c + 1) * 8), h, sl[h])
                 for c in range(nc) for h in hs]
        cots = [(du_s[rows, cl], dw_s[rows, cl], dp_s[h, rows, :], dqd_s[rows, cl], dkd_s[rows, cl],
                 dgl_s[rows8, cl][0:1, 0:1]) for rows, rows8, h, cl in where]

        def recurrence():
            for c in reversed(range(nc)):
                rows = slice(c * CHUNK, (c + 1) * CHUNK)
                rows8 = slice(c * 8, (c + 1) * 8)
                srows = slice(c * DN_DIM, (c + 1) * DN_DIM)
                ds_ = [dstate[h] for h in hs]
                dsb = [_bf(x) for x in ds_]
                dob = [_bf(do_ref[rows, cl]) for cl in sl]
                pdo = [_tn(p_ref[h, rows, :], b) for h, b in zip(hs, dob)]
                qdo = [_tn(qd_ref[rows, cl], b) for cl, b in zip(sl, dob)]
                kds = [_nn(kd_ref[rows, cl], b) for cl, b in zip(sl, dsb)]
                yield
                dvn = [a + b for a, b in zip(kds, pdo)]
                dvb = [_bf(x) for x in dvn]
                wdv = [_tn(w_ref[rows, cl], b) for cl, b in zip(sl, dvb)]
                for h in hs:
                    dstate[h] = ds_[h] * gl_ref[rows8, sl[h]][0:1] + qdo[h] - wdv[h]
                sfs = [st_ref[srows, cl] for cl in sl]
                sbs = [_bf(x) for x in sfs]
                vnb = [_bf(vn_ref[rows, cl]) for cl in sl]
                for h in hs:
                    du_s[rows, sl[h]] = dvn[h]
                    dw_s[rows, sl[h]] = -_nt(dvb[h], sbs[h])
                    dqd_s[rows, sl[h]] = _nt(dob[h], sbs[h])
                    dkd_s[rows, sl[h]] = _nt(vnb[h], dsb[h])
                    dp_s[h, rows, :] = _nt(dob[h], vnb[h])
                    dgl = jnp.sum(jnp.sum(ds_[h] * sfs[h], axis=1, keepdims=True), axis=0, keepdims=True)
                    dgl_s[rows8, sl[h]] = jnp.broadcast_to(dgl, (8, DN_DIM))
                yield

        steps = recurrence()

        bgs = [bg_ref[rows, :] for rows, _, _, _ in where]
        outs = _chunk_bwd([q_ref[rows, cl] for rows, _, _, cl in where], [k_ref[rows, cl] for rows, _, _, cl in where],
                          [v_ref[rows, cl] for rows, _, _, cl in where],
                          [b[:, h:h + 1] for b, (_, _, h, _) in zip(bgs, where)],
                          [b[:, GC_LANE + h:GC_LANE + h + 1] for b, (_, _, h, _) in zip(bgs, where)],
                          [t_ref[h, rows, :] for rows, _, h, _ in where], cots, tick=lambda: next(steps, None))
        for _ in steps:
            pass
        lane = lax.broadcasted_iota(jnp.int32, (CHUNK, BA_PAD), 1)
        for c in range(nc):
            dbg = jnp.zeros((CHUNK, BA_PAD), F32)
            for (rows, _, h, cl), (dq, dk, dv, dbeta, dgc) in list(zip(where, outs))[c * DN_HEADS:(c + 1) * DN_HEADS]:
                dq_ref[rows, cl] = dq
                dk_ref[rows, cl] = dk
                dv_ref[rows, cl] = dv
                dbg = dbg + jnp.where(lane == h, dbeta, 0.0) + jnp.where(lane == GC_LANE + h, dgc, 0.0)
            dbg_ref[where[c * DN_HEADS][0], :] = dbg

    rec = lambda i: jnp.maximum(npass - 1 - i, 0)
    loc = lambda i: jnp.minimum(npass - i, npass - 1)
    tok_r = pl.BlockSpec((tp, DN_WIDTH), lambda i: (rec(i), 0))
    tok_l = pl.BlockSpec((tp, DN_WIDTH), lambda i: (loc(i), 0))
    sq_r = pl.BlockSpec((DN_HEADS, tp, CHUNK), lambda i: (0, rec(i), 0))
    sq_l = pl.BlockSpec((DN_HEADS, tp, CHUNK), lambda i: (0, loc(i), 0))
    ba_l = pl.BlockSpec((tp, BA_PAD), lambda i: (loc(i), 0))
    return pl.pallas_call(
        body, name="dn_backward", grid=(npass + 1,), compiler_params=_params("arbitrary"),
        in_specs=[tok_r, pl.BlockSpec((nc * DN_DIM, DN_WIDTH), lambda i: (rec(i), 0)), tok_r, tok_r, tok_r, tok_r,
                  sq_r, pl.BlockSpec((nc * 8, DN_WIDTH), lambda i: (rec(i), 0)),
                  tok_l, tok_l, tok_l, ba_l, sq_l],
        out_specs=(tok_l, tok_l, tok_l, ba_l),
        out_shape=(SDS((s, DN_WIDTH), F32),) * 3 + (SDS((s, BA_PAD), F32),),
        scratch_shapes=[pltpu.VMEM((DN_HEADS, DN_DIM, DN_DIM), F32)] + [pltpu.VMEM((tp, DN_WIDTH), F32)] * 4
        + [pltpu.VMEM((DN_HEADS, tp, CHUNK), F32), pltpu.VMEM((nc * 8, DN_WIDTH), F32)],
    )(do, st, vn, w, qd, kd, p, gl, q, k, v, bg, t)


def _dn_prep_bwd(qkv_pre, ba, dq, dk, dv, dbg, conv_w8, alog_row, dtb_row, hbf, dz_dn, ts):
    s = qkv_pre.shape[0]
    cw = 3 * DN_WIDTH
    nt = s // ts

    def body(pre_ref, ph_ref, nh_ref, ba_ref, dq_ref, dqh_ref, dk_ref, dkh_ref, dv_ref, dvh_ref, dbg_ref,
             cw_ref, al_ref, dtb_ref, h_ref, dz_ref, dpre_ref, dba_ref, dcw_ref, dal_ref, ddtb_ref,
             gqkv_ref, gz_ref, gba_ref):
        n = pl.program_id(0)

        @pl.when(n == 0)
        def _():
            gqkv_ref[...] = jnp.zeros_like(gqkv_ref)
            gz_ref[...] = jnp.zeros_like(gz_ref)
            gba_ref[...] = jnp.zeros_like(gba_ref)
            dcw_ref[...] = jnp.zeros_like(dcw_ref)
            dal_ref[...] = jnp.zeros_like(dal_ref)
            ddtb_ref[...] = jnp.zeros_like(ddtb_ref)

        hb = h_ref[...]
        gz_ref[...] += lax.dot_general(hb, dz_ref[...], _TN, preferred_element_type=F32)
        last = n == nt - 1
        prev = jnp.where(n == 0, 0.0, ph_ref[...])
        ext = jnp.concatenate([prev, pre_ref[...], nh_ref[...]], axis=0)
        taps = _conv_taps(ext, ts + 8)
        conv = taps[0] * cw_ref[0:1, :]
        for j in range(1, CONV_K):
            conv = conv + taps[j] * cw_ref[j:j + 1, :]

        def cot(main, halo, cols):
            return jnp.concatenate([main[:, cols], jnp.where(last, 0.0, halo[:, cols])], axis=0)

        rows = ts + 8
        for grp, (fn, mref, href) in enumerate(((_post_q, dq_ref, dqh_ref), (_post_k, dk_ref, dkh_ref),
                                                (_post_v, dv_ref, dvh_ref))):
            gcols = slice(grp * DN_WIDTH, (grp + 1) * DN_WIDTH)
            pieces = []
            for h in range(DN_HEADS):
                cols = slice(h * DN_DIM, (h + 1) * DN_DIM)
                c0 = grp * DN_WIDTH + h * DN_DIM
                _, vjp = jax.vjp(fn, conv[:, c0:c0 + DN_DIM])
                pieces.append(vjp(cot(mref, href, cols))[0])
            dconv = jnp.concatenate(pieces, axis=1)
            dpre = dconv[:ts] * cw_ref[CONV_K - 1:CONV_K, gcols]
            for j in range(CONV_K - 1):
                sh = CONV_K - 1 - j
                dpre = dpre + pltpu.roll(dconv, rows - sh, 0)[:ts] * cw_ref[j:j + 1, gcols]
            dpre_b = _bf(dpre)
            dpre_ref[:, gcols] = dpre_b
            gqkv_ref[:, gcols] += lax.dot_general(hb, dpre_b, _TN, preferred_element_type=F32)
            for j in range(CONV_K):
                dcw_ref[j:j + 1, gcols] += jnp.sum(dconv[:ts] * taps[j][:ts, gcols], axis=0, keepdims=True)

        dbg = dbg_ref[...]
        lane = lax.broadcasted_iota(jnp.int32, dbg.shape, 1)
        dg = pltpu.roll(_chunk_cumsum(dbg, reverse=True), BA_PAD - DN_HEADS, 1)
        cot_bg = jnp.where(lane < DN_HEADS, dbg, jnp.where(lane < GC_LANE, dg, 0.0))
        _, vjp = jax.vjp(_beta_decay, ba_ref[...], al_ref[...], dtb_ref[...])
        dba, dal, ddtb = vjp(cot_bg)
        dba_b = _bf(dba)
        dba_ref[...] = dba_b
        gba_ref[...] += lax.dot_general(hb, dba_b, _TN, preferred_element_type=F32)
        dal_ref[...] += dal
        ddtb_ref[...] += ddtb

    tok = lambda w: pl.BlockSpec((ts, w), lambda i: (i, 0))
    full = lambda a: pl.BlockSpec(a.shape, lambda i: (0, 0))
    prevh = lambda w: pl.BlockSpec((8, w), lambda i: (jnp.maximum(i * (ts // 8) - 1, 0), 0))
    nexth = lambda w: pl.BlockSpec((8, w), lambda i: (jnp.minimum((i + 1) * (ts // 8), s // 8 - 1), 0))
    row = pl.BlockSpec((1, LANES), lambda i: (0, 0))
    return pl.pallas_call(
        body, name="dn_prep_bwd", grid=(nt,), compiler_params=_params("arbitrary"),
        in_specs=[tok(cw), prevh(cw), nexth(cw), tok(BA_PAD),
                  tok(DN_WIDTH), nexth(DN_WIDTH), tok(DN_WIDTH), nexth(DN_WIDTH), tok(DN_WIDTH), nexth(DN_WIDTH),
                  tok(BA_PAD), full(conv_w8), full(alog_row), full(dtb_row), tok(D_MODEL), tok(DN_WIDTH)],
        out_specs=(tok(cw), tok(BA_PAD), pl.BlockSpec((8, cw), lambda i: (0, 0)), row, row)
        + tuple(pl.BlockSpec((D_MODEL, w), lambda i: (0, 0)) for w in (cw, DN_WIDTH, BA_PAD)),
        out_shape=(SDS((s, cw), BF16), SDS((s, BA_PAD), BF16), SDS((8, cw), F32), SDS((1, LANES), F32),
                   SDS((1, LANES), F32)) + tuple(SDS((D_MODEL, w), F32) for w in (cw, DN_WIDTH, BA_PAD)),
    )(qkv_pre, qkv_pre, qkv_pre, ba, dq, dq, dk, dk, dv, dv, dbg, conv_w8, alog_row, dtb_row, hbf, dz_dn)


def _dh_dx(dps, ws, x, mod, norm_w, dx2, ts):
    s = x.shape[0]
    widths = [w.shape[1] for w in ws]
    np_ = len(ws)

    def body(*refs):
        dp_refs, w_refs = refs[:np_], refs[np_:2 * np_]
        x_ref, mod_ref, nw_ref, dx2_ref, gx_ref, dshift, dscale, dnw = refs[2 * np_:]

        @pl.when(pl.program_id(0) == 0)
        def _():
            dshift[...] = jnp.zeros_like(dshift)
            dscale[...] = jnp.zeros_like(dscale)
            dnw[...] = jnp.zeros_like(dnw)

        dh = lax.dot_general(dp_refs[0][...], w_refs[0][...], _NT, preferred_element_type=F32)
        for a, b in zip(dp_refs[1:], w_refs[1:]):
            dh = dh + lax.dot_general(a[...], b[...], _NT, preferred_element_type=F32)
        xt = x_ref[...]
        r = lax.rsqrt(jnp.mean(xt * xt, axis=-1, keepdims=True) + EPS)
        xn = xt * r
        nw = nw_ref[...]
        sc1 = 1.0 + mod_ref[:, D_MODEL:2 * D_MODEL]
        dshift[...] += jnp.sum(dh, axis=0, keepdims=True)
        dscale[...] += jnp.sum(dh * (xn * nw), axis=0, keepdims=True)
        dnw[...] += jnp.sum(dh * sc1 * xn, axis=0, keepdims=True)
        dxn = dh * sc1 * nw
        gx_ref[...] = r * (dxn - xn * jnp.mean(dxn * xn, axis=-1, keepdims=True)) + dx2_ref[...]

    tok = lambda w: pl.BlockSpec((ts, w), lambda i: (i, 0))
    full = lambda a: pl.BlockSpec(a.shape, lambda i: (0, 0))
    row = pl.BlockSpec((1, D_MODEL), lambda i: (0, 0))
    return pl.pallas_call(
        body, name="dh_dx", grid=(s // ts,), compiler_params=_params("arbitrary"),
        in_specs=[tok(w) for w in widths] + [full(w) for w in ws] + [tok(D_MODEL), full(mod), full(norm_w),
                                                                    tok(D_MODEL)],
        out_specs=(tok(D_MODEL), row, row, row),
        out_shape=(SDS((s, D_MODEL), F32),) + (SDS((1, D_MODEL), F32),) * 3,
    )(*dps, *ws, x, mod, norm_w, dx2)


def _grad_w_in_at(h, dq, dk, dv, dz_at, cos_t, sin_t, ts):
    s = h.shape[0]
    assert ts % PERM_BLK == 0

    def body(h_ref, q_ref, k_ref, v_ref, dz_ref, cos_ref, sin_ref, oq, ok, ov, gq, gk, gv, gz):
        @pl.when(pl.program_id(0) == 0)
        def _():
            for o in (gq, gk, gv, gz):
                o[...] = jnp.zeros_like(o)

        hb = h_ref[...]
        gz[...] += lax.dot_general(hb, dz_ref[...], _TN, preferred_element_type=F32)
        back = _plane_perm(PERM_BLK, True)
        rows = PERM_BLK // PLANES
        halves = [(slice(i * PERM_BLK, (i + 1) * PERM_BLK), slice(i * rows, (i + 1) * rows))
                  for i in range(ts // PERM_BLK)]

        def planes(ref, j, prow):
            return jnp.concatenate([ref[j, r, prow, :] for r in range(PLANES)], axis=0)

        for trow, prow in halves:
            vp = jnp.concatenate([_bf(planes(v_ref, j, prow)) for j in range(AT_PAIRS)], axis=1)
            ov[trow, :] = _bf(jnp.dot(back, vp, preferred_element_type=F32))
        gv[...] += lax.dot_general(hb, ov[...], _TN, preferred_element_type=F32)
        for g_ref, o_ref, acc in ((q_ref, oq, gq), (k_ref, ok, gk)):
            for trow, prow in halves:
                cs, sn = cos_ref[trow, :], sin_ref[trow, :]
                gs = [planes(g_ref, j, prow) for j in range(AT_PAIRS)]
                gp = jnp.concatenate([_bf(g * cs + _swap_half64(g * sn)) for g in gs], axis=1)
                o_ref[trow, :] = _bf(jnp.dot(back, gp, preferred_element_type=F32))
            acc[...] += lax.dot_general(hb, o_ref[...], _TN, preferred_element_type=F32)

    tok = lambda w: pl.BlockSpec((ts, w), lambda i: (i, 0))
    pairs = pl.BlockSpec((AT_PAIRS, PLANES, ts // PLANES, LANES), lambda i: (0, 0, i, 0))
    acc = pl.BlockSpec((D_MODEL, AT_WIDTH), lambda i: (0, 0))
    return pl.pallas_call(
        body, name="grad_w_in_at", grid=(s // ts,), compiler_params=_params("arbitrary"),
        in_specs=[tok(D_MODEL), pairs, pairs, pairs, tok(AT_WIDTH), tok(LANES), tok(LANES)],
        out_specs=(tok(AT_WIDTH),) * 3 + (acc,) * 4,
        out_shape=(SDS((s, AT_WIDTH), BF16),) * 3 + (SDS((D_MODEL, AT_WIDTH), F32),) * 4,
    )(h, dq, dk, dv, dz_at, cos_t, sin_t)


def _adamw_math(w, g, m, v):
    m = ADAM_B1 * m + (1.0 - ADAM_B1) * g
    v = ADAM_B2 * v + (1.0 - ADAM_B2) * (g * g)
    m_hat = m / (1.0 - ADAM_B1 ** ADAM_STEP)
    v_hat = v / (1.0 - ADAM_B2 ** ADAM_STEP)
    delta = -ADAM_LR * (m_hat / (jnp.sqrt(v_hat) + ADAM_EPS) + ADAM_WD * w)
    return delta, m, v


def _adamw(w, m, v, g, name, own=None):
    def body(w_ref, m_ref, v_ref, g_ref, *rest):
        g_out, d_out, m_out, v_out = rest[-4:]
        if own is None:
            g = g_ref[...]
        else:
            g = g_ref[0].astype(F32)
            for k in range(1, N_DEV):
                g = g + g_ref[k].astype(F32)
            g = g + rest[0][...].astype(F32)
        g_out[...] = g
        d_out[...], m_out[...], v_out[...] = _adamw_math(w_ref[...], g, m_ref[...], v_ref[...])

    args = (w, m, v, g) if own is None else (w, m, v, g, own)
    return pl.pallas_call(body, name=name, compiler_params=_params(),
                          out_shape=(SDS(w.shape, F32),) * 4)(*args)


def _adamw_w_mod(w, m, v, siluc_all, dmod_mine):
    def body(w_ref, m_ref, v_ref, sc_ref, dm_ref, g_out, d_out, m_out, v_out):
        g = _htn(sc_ref[...], dm_ref[...])
        g_out[...] = g
        d_out[...], m_out[...], v_out[...] = _adamw_math(w_ref[...], g, m_ref[...], v_ref[...])

    return pl.pallas_call(body, name="adamw_w_mod", compiler_params=_params(),
                          out_shape=(SDS(w.shape, F32),) * 4)(w, m, v, siluc_all, dmod_mine)


def _pack_sum(pack_all):
    def body(p_ref, o_ref):
        t = p_ref[0]
        for k in range(1, N_DEV):
            t = t + p_ref[k]
        o_ref[...] = t

    return pl.pallas_call(body, name="pack_sum", out_shape=SDS(pack_all.shape[1:], F32))(pack_all)


def _tile(s, want):
    t = min(want, s)
    assert s % t == 0
    return t


def _local_step(x, c, positions, w_mod_bf, b_mod, norm_w, w_in_bf, conv_w, a_log, dt_bias, dn_norm_w, at_norm_w,
                w_out_bf, final_norm_w, tgt):
    s = x.shape[0]
    o = [0]
    for wdt in IN_SPLITS:
        o.append(o[-1] + wdt)
    w_ba = jnp.pad(w_in_bf[:, o[2]:o[4]], ((0, 0), (0, BA_PAD - 2 * DN_HEADS)))
    ws = [w_in_bf[:, o[0]:o[1]], w_in_bf[:, o[1]:o[2]], w_ba, w_in_bf[:, o[4]:o[5]], w_in_bf[:, o[5]:o[6]],
          w_in_bf[:, o[6]:o[7]], w_in_bf[:, o[7]:o[8]]]
    conv_w8 = jnp.pad(conv_w, ((0, 8 - CONV_K), (0, 0)))
    alog_row = jnp.pad(a_log, ((0, 0), (DN_HEADS, BA_PAD - 2 * DN_HEADS)))
    dtb_row = jnp.pad(dt_bias, ((0, 0), (DN_HEADS, BA_PAD - 2 * DN_HEADS)))
    atw2 = jnp.concatenate([at_norm_w, at_norm_w], axis=1)

    half = AT_DIM // 2
    lane = jnp.arange(LANES)
    inv_freq = ROPE_THETA ** (-(lane % half).astype(F32) / half)
    pos = positions.reshape(s // PERM_BLK, PERM_BLK // PLANES, PLANES).transpose(0, 2, 1).reshape(s)
    ang = pos.astype(F32)[:, None] * inv_freq
    cos_t = jnp.cos(ang)
    sin_t = jnp.sin(ang) * jnp.where((lane // half) % 2 == 0, -1.0, 1.0)

    mod, siluc = _adaln_mod(c, w_mod_bf, b_mod)
    gate = mod[:, 2 * D_MODEL:]
    hbf, qkv_pre, z_dn, ba, qr, kr, vb, z_at, q, k, v, bg = _ln_proj(
        x, mod, norm_w, ws, cos_t, sin_t, conv_w8, alog_row, dtb_row, _tile(s, 256))
    w, qd, kd, p, gl, tinv, o_dn, vn, st = _dn_forward(q, k, v, bg)
    o_at, lse = _attn_fwd(qr, kr, vb)
    (dx2, gw_out, dfw, dgate, loss, do_dn, dz_dn, do_at, dz_at, delta, ddnw, datw) = _out_loss(
        o_dn, z_dn, o_at, z_at, dn_norm_w, atw2, x, tgt, w_out_bf, gate, final_norm_w, _tile(s, 512))

    daq, dak, dav, g_aq, g_ak, g_av, g_az = _grad_w_in_at(hbf, *_attn_bwd(qr, kr, vb, do_at, lse, delta), dz_at,
                                                           cos_t, sin_t, _tile(s, 512))
    dq, dk, dv, dbg = _dn_backward(do_dn, st, vn, w, qd, kd, p, gl, q, k, v, bg, tinv)
    dqkv, dba, dcw, dal, ddtb, g_qkv, g_z, g_ba = _dn_prep_bwd(qkv_pre, ba, dq, dk, dv, dbg, conv_w8, alog_row, dtb_row,
                                                               hbf, dz_dn, _tile(s, 512))
    dps = [dqkv, dz_dn, dba, daq, dak, dav, dz_at]
    gw_in = jnp.concatenate([g_qkv, g_z, g_ba[:, :2 * DN_HEADS], g_aq, g_ak, g_av, g_az], axis=1)
    small = dict(conv=dcw[:CONV_K], dgate=dgate, siluc=siluc, dfw=dfw, alog=dal, dtb=ddtb, dnn=ddnw, atn=datw)

    def input_grad(token):
        gx, dshift, dscale, dnw = _dh_dx(dps, ws, x, mod + token, norm_w, dx2, _tile(s, 512))
        return gx, jnp.concatenate([dshift, dscale, small["dgate"]], axis=1), dnw

    return loss, gw_in, gw_out, small, input_grad


def kernel(x, c, positions, w_mod, b_mod, norm_w, w_in, conv_w, a_log, dt_bias, dn_norm_w, at_norm_w, w_out, final_norm_w, loss_target, m_w_mod, m_b_mod, m_norm_w, m_w_in, m_conv_w, m_a_log, m_dt_bias, m_dn_norm_w, m_at_norm_w, m_w_out, m_final_norm_w, v_w_mod, v_b_mod, v_norm_w, v_w_in, v_conv_w, v_a_log, v_dt_bias, v_dn_norm_w, v_at_norm_w, v_w_out, v_final_norm_w):
    me = 4 * lax.axis_index("x") + 2 * lax.axis_index("y") + lax.axis_index("c")
    s = x.shape[1]

    g_mod, g_in, g_conv, g_out = _all_gather(
        [_bf(w_mod[0]), _bf(w_in[0]), conv_w[0], _bf(w_out[0])], "gather_weights")
    w_mod_bf = g_mod.transpose(1, 0, 2).reshape(D_MODEL, 3 * D_MODEL)
    w_in_bf = g_in.transpose(1, 0, 2).reshape(D_MODEL, IN_COLS)
    conv_full = g_conv.transpose(1, 0, 2).reshape(CONV_K, 3 * DN_WIDTH)
    w_out_bf = g_out.reshape(D_MODEL, D_MODEL)

    loss, gw_in, gw_out, small, input_grad = _local_step(
        x[0], c, positions[0], w_mod_bf, b_mod, norm_w, w_in_bf, conv_full, a_log, dt_bias, dn_norm_w, at_norm_w,
        w_out_bf, final_norm_w.reshape(1, D_MODEL), loss_target[0])

    gw_in_slabs = _bf(gw_in).reshape(D_MODEL, N_DEV, IN_SHARD).transpose(1, 0, 2)
    gw_out_slabs = _bf(gw_out).reshape(N_DEV, D_MODEL // N_DEV, D_MODEL)
    send_sems, recv_sems, srcs, lands, token = _scatter_start([gw_in_slabs, gw_out_slabs])
    gx, dmod, dnw = input_grad(token[0, 0])
    r_in, r_out = _scatter_wait(send_sems, recv_sems, srcs, lands, gx)
    own_in = lax.dynamic_index_in_dim(gw_in_slabs, me, 0, keepdims=False)
    own_out = lax.dynamic_index_in_dim(gw_out_slabs, me, 0, keepdims=False)

    pack = jnp.concatenate([small["conv"].reshape(1, -1), dmod, small["siluc"], dnw, small["dfw"],
                            small["alog"], small["dtb"], small["dnn"], small["atn"],
                            jnp.pad(loss, ((0, 0), (0, LANES - 1)))], axis=1).reshape(PK_ROWS, LANES)
    (pack_all,) = _exchange([pack], [False], "exchange_small")

    res = {}
    res["w_in"] = _adamw(w_in[0], m_w_in[0], v_w_in[0], r_in, "adamw_w_in", own=own_in)
    res["w_out"] = _adamw(w_out[0], m_w_out[0], v_w_out[0], r_out, "adamw_w_out", own=own_out)
    flat_all = pack_all.reshape(N_DEV, PK_END)
    dmod_mine = lax.dynamic_slice(flat_all, (0, PK_DMOD + me * (3 * D_MODEL // N_DEV)), (N_DEV, 3 * D_MODEL // N_DEV))
    res["w_mod"] = _adamw_w_mod(w_mod[0], m_w_mod[0], v_w_mod[0], flat_all[:, PK_SILUC:PK_DNW], dmod_mine)
    tot = _pack_sum(pack_all).reshape(1, PK_END)
    g_conv_full = tot[:, PK_CONV:PK_DMOD].reshape(CONV_K, 3 * DN_WIDTH)
    g_conv_mine = lax.dynamic_slice(g_conv_full, (0, me * (3 * DN_WIDTH // N_DEV)), (CONV_K, 3 * DN_WIDTH // N_DEV))
    res["conv_w"] = _adamw(conv_w[0], m_conv_w[0], v_conv_w[0], g_conv_mine, "adamw_conv_w")
    res["b_mod"] = _adamw(b_mod, m_b_mod, v_b_mod, tot[:, PK_DMOD:PK_SILUC], "adamw_b_mod")
    res["norm_w"] = _adamw(norm_w, m_norm_w, v_norm_w, tot[:, PK_DNW:PK_DFW], "adamw_norm_w")
    res["a_log"] = _adamw(a_log, m_a_log, v_a_log, tot[:, PK_ALOG + DN_HEADS:PK_ALOG + 2 * DN_HEADS], "adamw_a_log")
    res["dt_bias"] = _adamw(dt_bias, m_dt_bias, v_dt_bias, tot[:, PK_DTB + DN_HEADS:PK_DTB + 2 * DN_HEADS],
                            "adamw_dt_bias")
    res["dn_norm_w"] = _adamw(dn_norm_w, m_dn_norm_w, v_dn_norm_w, tot[:, PK_DNN:PK_ATN], "adamw_dn_norm_w")
    g_atn = tot[:, PK_ATN:PK_ATN + AT_DIM] + tot[:, PK_ATN + AT_DIM:PK_LOSS]
    res["at_norm_w"] = _adamw(at_norm_w, m_at_norm_w, v_at_norm_w, g_atn, "adamw_at_norm_w")
    fin = _adamw(final_norm_w.reshape(1, D_MODEL), m_final_norm_w.reshape(1, D_MODEL),
                 v_final_norm_w.reshape(1, D_MODEL), tot[:, PK_DFW:PK_ALOG], "adamw_final_norm_w")
    res["final_norm_w"] = tuple(a.reshape(D_MODEL) for a in fin)

    lead = ("w_mod", "w_in", "conv_w", "w_out")
    names = ("w_mod", "b_mod", "norm_w", "w_in", "conv_w", "a_log", "dt_bias", "dn_norm_w", "at_norm_w", "w_out",
             "final_norm_w")
    out = [tot[0, PK_LOSS], gx.reshape(1, s, D_MODEL)]
    for kind in range(4):
        for nm in names:
            a = res[nm][kind]
            out.append(a[None] if nm in lead else a)
    return tuple(out)
```

```python
import functools

import jax
import jax.numpy as jnp
from jax import lax
from jax.experimental import pallas as pl
from jax.experimental.pallas import tpu as pltpu

F32, BF16 = jnp.float32, jnp.bfloat16
HI = lax.Precision.HIGHEST
SDS = jax.ShapeDtypeStruct

D_MODEL = 1024
DN_HEADS, DN_DIM, DN_WIDTH = 4, 128, 512
AT_HEADS, AT_DIM, AT_WIDTH = 8, 64, 512
CONV_K = 4
CHUNK = 64
Q_BLOCK = 128
W_SUB = 128
DILATIONS = (1, 4, 16)
AT_PAIRS = 4
PLANES = 16
PERM_BLK = 256
ATT_BLK = Q_BLOCK * max(DILATIONS)
ATT_UNROLL_BWD = 4
CH_UNROLL, CH_UNROLL_BWD = 4, 8
ROPE_THETA = 10000.0
EPS = 1e-6
N_DEV = 8
LANES = 128
BA_PAD = 128
IN_SPLITS = (1536, 512, 4, 4, 512, 512, 512, 512)
IN_COLS = sum(IN_SPLITS)
IN_SHARD = IN_COLS // N_DEV
VMEM_LIMIT = 58 * 2 ** 20

ADAM_LR, ADAM_B1, ADAM_B2, ADAM_EPS, ADAM_WD, ADAM_STEP = 0.001, 0.9, 0.999, 1e-08, 0.01, 10

PK_CONV, PK_DMOD, PK_SILUC, PK_DNW, PK_DFW, PK_ALOG, PK_DTB, PK_DNN, PK_ATN, PK_LOSS, PK_END = (
    0, 6144, 9216, 10240, 11264, 12288, 12416, 12544, 12672, 12800, 12928)
PK_ROWS = PK_END // LANES

_NT = (((1,), (1,)), ((), ()))
_TN = (((0,), (0,)), ((), ()))


def _params(*sem):
    return pltpu.CompilerParams(dimension_semantics=sem or None, vmem_limit_bytes=VMEM_LIMIT)


def _bf(x):
    return x.astype(BF16)


def _nn(a, b):
    return jnp.dot(_bf(a), _bf(b), preferred_element_type=F32)


def _nt(a, b):
    return lax.dot_general(_bf(a), _bf(b), _NT, preferred_element_type=F32)


def _tn(a, b):
    return lax.dot_general(_bf(a), _bf(b), _TN, preferred_element_type=F32)


def _htn(a, b):
    return lax.dot_general(a, b, _TN, precision=HI, preferred_element_type=F32)


def _head_sum(x):
    r = lax.broadcasted_iota(jnp.int32, (LANES, LANES), 0)
    c = lax.broadcasted_iota(jnp.int32, (LANES, LANES), 1)
    same = jnp.where((r // AT_DIM) == (c // AT_DIM), 1.0, 0.0).astype(BF16)
    hi, lo = _hl(x)
    return jnp.dot(hi, same, preferred_element_type=F32) + jnp.dot(lo, same, preferred_element_type=F32)


@jax.custom_vjp
def _d_head_sum(x):
    return _head_sum(x)


_d_head_sum.defvjp(lambda x: (_head_sum(x), None), lambda _, g: (_head_sum(g),))


def _silu(x):
    return x * jax.nn.sigmoid(x)


def _softplus(x):
    return jnp.maximum(x, 0.0) + jnp.log(1.0 + jnp.exp(-jnp.abs(x)))


def _l2n(x):
    return x * lax.rsqrt(jnp.sum(x * x, axis=-1, keepdims=True) + EPS)


def _post_q(x):
    return _l2n(_silu(x)) * (DN_DIM ** -0.5)


def _post_k(x):
    return _l2n(_silu(x))


def _post_v(x):
    return _silu(x)


def _beta_decay(ba, alog_row, dtb_row):
    lane = lax.broadcasted_iota(jnp.int32, ba.shape, 1)
    return jnp.where(lane < DN_HEADS, jax.nn.sigmoid(ba), -jnp.exp(alog_row) * _softplus(ba + dtb_row))


def _gate_dn(o, z, w):
    return (o * lax.rsqrt(jnp.mean(o * o, axis=-1, keepdims=True) + EPS)) * w * _silu(z)


def _gate_at(o, z, w2, head_sum):
    ms = head_sum(o * o) * (1.0 / AT_DIM)
    return (o * lax.rsqrt(ms + EPS)) * w2 * _silu(z)


def _swap_half64(x):
    lane = lax.broadcasted_iota(jnp.int32, x.shape, 1)
    return jnp.where((lane & (AT_DIM - 1)) < AT_DIM // 2, pltpu.roll(x, LANES - AT_DIM // 2, 1),
                     pltpu.roll(x, AT_DIM // 2, 1))


_NN = (((1,), (0,)), ((), ()))


def _hl(a):
    hi = a.astype(BF16)
    return hi, (a - hi.astype(F32)).astype(BF16)


def _mm3(a, b, dims=_NN):
    (ah, al), (bh, bl) = a, b
    f = lambda x, y: lax.dot_general(x, y, dims, preferred_element_type=F32)
    return f(ah, bh) + (f(ah, bl) + f(al, bh))


def _chunk_masks():
    r = lax.broadcasted_iota(jnp.int32, (CHUNK, CHUNK), 0)
    c = lax.broadcasted_iota(jnp.int32, (CHUNK, CHUNK), 1)
    return r >= c, r > c, (r == c).astype(F32), (r // 16) == (c // 16)


def _tri_inv(mats, tick=lambda: None):
    _, _, eye, blk = _chunk_masks()
    dg = [jnp.where(blk, a, 0.0) for a in mats]
    lo = [jnp.where(blk, 0.0, a) for a in mats]
    sdg = [_hl(x) for x in dg]
    d2 = [_mm3(s, s) for s in sdg]
    tick()
    sd2 = [_hl(x) for x in d2]
    d4 = [_mm3(s, s) for s in sd2]
    tick()
    sd4 = [_hl(x) for x in d4]
    d8 = [_mm3(s, s) for s in sd4]
    tick()
    p1 = [_mm3(_hl(eye - a), _hl(eye + b)) for a, b in zip(dg, d2)]
    tick()
    p2 = [_mm3(_hl(a), _hl(eye + b)) for a, b in zip(p1, d4)]
    tick()
    dinv = [_mm3(_hl(a), _hl(eye + b)) for a, b in zip(p2, d8)]
    tick()
    sdinv = [_hl(x) for x in dinv]
    n1 = [_mm3(s, _hl(b)) for s, b in zip(sdinv, lo)]
    tick()
    sn1 = [_hl(x) for x in n1]
    n2 = [_mm3(s, s) for s in sn1]
    tick()
    q1 = [_mm3(_hl(eye - a), _hl(eye + b)) for a, b in zip(n1, n2)]
    return [_mm3(_hl(a), s) for a, s in zip(q1, sdinv)]


def _chunk_common(qs, ks, vs, betas, gcs):
    tril, _, _, _ = _chunk_masks()
    out = []
    for q, k, v, beta, gc in zip(qs, ks, vs, betas, gcs):
        gb = jnp.broadcast_to(gc, (CHUNK, DN_DIM))
        gt = gb.T[:CHUNK, :]
        gam = jnp.where(tril, jnp.exp(jnp.where(tril, gb[:, :CHUNK] - gt, 0.0)), 0.0)
        last = gb[CHUNK - 1:CHUNK, :]
        eg, e2 = jnp.exp(gb), jnp.exp(last - gb)
        kb, vb = k * beta, v * beta
        out.append(dict(gam=gam, eg=eg, e2=e2, gl=jnp.exp(last[:, 0:1]), kb=kb, vb=vb, kbg=kb * eg,
                        m=_nt(kb, k), qk=_nt(q, k)))
    return out


def _chunk_fwd(qs, ks, vs, betas, gcs, tick=lambda: None):
    tril, strict, _, _ = _chunk_masks()
    cm = _chunk_common(qs, ks, vs, betas, gcs)
    ts = _tri_inv([jnp.where(strict, c["m"] * c["gam"], 0.0) for c in cm], tick)
    outs = []
    for q, k, c, t in zip(qs, ks, cm, ts):
        uw = _nn(t, jnp.concatenate([c["vb"], c["kbg"]], axis=1))
        p = jnp.where(tril, c["qk"] * c["gam"], 0.0)
        outs.append((uw[:, :DN_DIM], uw[:, DN_DIM:], p, q * c["eg"], k * c["e2"], c["gl"], t.T))
    return outs


def _chunk_bwd(qs, ks, vs, betas, gcs, ts, cots, tick=lambda: None):
    tril, strict, _, _ = _chunk_masks()
    cm = _chunk_common(qs, ks, vs, betas, gcs)
    tick()
    row = lax.broadcasted_iota(jnp.int32, (CHUNK, 1), 0)
    ones = jnp.ones((CHUNK, DN_DIM), BF16)
    rs = lambda x: jnp.sum(x, axis=-1, keepdims=True)
    tts = [_bf(t) for t in ts]
    duw = [_bf(jnp.concatenate([ct[0], ct[1]], axis=1)) for ct in cots]
    dts = [_nt(a, jnp.concatenate([c["vb"], c["kbg"]], axis=1)) for a, c in zip(duw, cm)]
    tick()
    xs = [_nn(t, d) for t, d in zip(tts, dts)]
    tick()
    das = [jnp.where(strict, -_nn(x, t), 0.0) for x, t in zip(xs, tts)]
    dvks = [_nn(t, a) for t, a in zip(tts, duw)]
    tick()
    outs = []
    every = max(1, len(qs) // 5)
    for idx, (q, k, v, beta, c, ct, da, dvk) in enumerate(zip(qs, ks, vs, betas, cm, cots, das, dvks)):
        if idx and idx % every == 0:
            tick()
        _, _, dp, dqd, dkd, dgl = ct
        dvb, dkbg = dvk[:, :DN_DIM], dvk[:, DN_DIM:]
        dm = da * c["gam"]
        dqk = jnp.where(tril, dp, 0.0) * c["gam"]
        e = dm * c["m"] + dqk * c["qk"]
        dmq = jnp.concatenate([dm, dqk], axis=0)
        r1 = _nn(dmq, k)
        dkb = r1[:CHUNK] + dkbg * c["eg"]
        dq = r1[CHUNK:] + dqd * c["eg"]
        dk = _tn(dmq, jnp.concatenate([c["kb"], q], axis=0)) + dkd * c["e2"] + dkb * beta
        dbeta = rs(dkb * k + dvb * v)
        eh, el = _hl(e)
        colsum = (lax.dot_general(eh, ones, _TN, preferred_element_type=F32)
                  + lax.dot_general(el, ones, _TN, preferred_element_type=F32))[:, 0:1]
        pkd = dkd * (k * c["e2"])
        dgc = rs(e) - colsum + rs(dqd * q * c["eg"] + dkbg * c["kbg"] - pkd)
        tail = rs(jnp.sum(pkd, axis=0, keepdims=True)) + dgl * c["gl"]
        dgc = dgc + jnp.where(row == CHUNK - 1, tail, 0.0)
        outs.append((dq, dk, dvb * beta, dbeta, dgc))
    return outs


def _chunk_cumsum(x, reverse=False):
    n = x.shape[0]
    pos = lax.broadcasted_iota(jnp.int32, x.shape, 0) & (CHUNK - 1)
    sh = 1
    while sh < CHUNK:
        if reverse:
            x = x + jnp.where(pos < CHUNK - sh, pltpu.roll(x, n - sh, 0), 0.0)
        else:
            x = x + jnp.where(pos >= sh, pltpu.roll(x, sh, 0), 0.0)
        sh *= 2
    return x


GC_LANE = 2 * DN_HEADS


def _exchange(arrays, scatter, name):
    n = len(arrays)
    out_shapes = []
    for a, sc in zip(arrays, scatter):
        out_shapes.append(SDS(a.shape if sc else (N_DEV,) + a.shape, a.dtype))

    def body(*refs):
        ins, outs = refs[:n], refs[n:2 * n]
        send_sems, recv_sems, loc_sems = refs[2 * n:]
        x, y, c = lax.axis_index("x"), lax.axis_index("y"), lax.axis_index("c")
        me = 4 * x + 2 * y + c
        local, remote = [], []
        for i in range(n):
            src = ins[i].at[me] if scatter[i] else ins[i]
            cp = pltpu.make_async_copy(src, outs[i].at[me], loc_sems.at[i])
            cp.start()
            local.append(cp)
        for dlt in range(1, N_DEV):
            px = 1 - x if dlt & 4 else x
            py = 1 - y if dlt & 2 else y
            pc = 1 - c if dlt & 1 else c
            peer = 4 * px + 2 * py + pc
            for i in range(n):
                src = ins[i].at[peer] if scatter[i] else ins[i]
                cp = pltpu.make_async_remote_copy(
                    src_ref=src, dst_ref=outs[i].at[me],
                    send_sem=send_sems.at[i, dlt - 1], recv_sem=recv_sems.at[i, dlt - 1],
                    device_id=(px, py, pc), device_id_type=pl.DeviceIdType.MESH)
                cp.start()
                arrive = pltpu.make_async_remote_copy(
                    src_ref=src, dst_ref=outs[i].at[peer],
                    send_sem=send_sems.at[i, dlt - 1], recv_sem=recv_sems.at[i, dlt - 1],
                    device_id=(px, py, pc), device_id_type=pl.DeviceIdType.MESH)
                remote.append((cp, arrive))
        for cp, arrive in remote:
            cp.wait_send()
            arrive.wait_recv()
        for cp in local:
            cp.wait()

    any_spec = pl.BlockSpec(memory_space=pl.ANY)
    return pl.pallas_call(
        body, name=name, out_shape=tuple(out_shapes),
        in_specs=[any_spec] * n, out_specs=tuple([any_spec] * n),
        scratch_shapes=[pltpu.SemaphoreType.DMA((n, N_DEV - 1)), pltpu.SemaphoreType.DMA((n, N_DEV - 1)),
                        pltpu.SemaphoreType.DMA((n,))],
    )(*arrays)


def _all_gather(arrays, name):
    n = len(arrays)

    def body(*refs):
        ins, outs = refs[:n], refs[n:2 * n]
        send_sems, recv_sems, loc_sems = refs[2 * n:]
        x, y, c = lax.axis_index("x"), lax.axis_index("y"), lax.axis_index("c")
        me, sibling = (x, y, c), (x, y, 1 - c)
        chips = [(1 - x, y), (x, 1 - y), (1 - x, 1 - y)]

        def copy(i, k, block, to, src=None):
            slot = outs[i].at[4 * block[0] + 2 * block[1] + block[2]]
            return pltpu.make_async_remote_copy(
                src_ref=slot if src is None else src, dst_ref=slot,
                send_sem=send_sems.at[i, k], recv_sem=recv_sems.at[i, k],
                device_id=to, device_id_type=pl.DeviceIdType.MESH)

        mine = [pltpu.make_async_copy(ins[i], outs[i].at[4 * x + 2 * y + c], loc_sems.at[i]) for i in range(n)]
        for cp in mine:
            cp.start()
        first = []
        for i in range(n):
            first.append(copy(i, 0, me, sibling, src=ins[i]))
            first += [copy(i, 1 + j, me, (*chip, c), src=ins[i]) for j, chip in enumerate(chips)]
        for cp in first:
            cp.start()
        passed = []
        for j, chip in enumerate(chips):
            for i in range(n):
                copy(i, 1 + j, (*chip, c), me).wait_recv()
                fwd = copy(i, 4 + j, (*chip, c), sibling)
                fwd.start()
                passed.append(fwd)
        for i in range(n):
            copy(i, 0, sibling, me).wait_recv()
        for j, chip in enumerate(chips):
            for i in range(n):
                copy(i, 4 + j, (*chip, 1 - c), me).wait_recv()
        for cp in first + passed:
            cp.wait_send()
        for cp in mine:
            cp.wait()

    any_spec = pl.BlockSpec(memory_space=pl.ANY)
    return pl.pallas_call(
        body, name=name, out_shape=tuple(SDS((N_DEV,) + a.shape, a.dtype) for a in arrays),
        in_specs=[any_spec] * n, out_specs=tuple([any_spec] * n),
        scratch_shapes=[pltpu.SemaphoreType.DMA((n, N_DEV - 1)), pltpu.SemaphoreType.DMA((n, N_DEV - 1)),
                        pltpu.SemaphoreType.DMA((n,))],
    )(*arrays)


_HBM = pl.BlockSpec(memory_space=pltpu.HBM)
_SEM = pl.BlockSpec(memory_space=pltpu.SEMAPHORE)


def _peers(x, y, c):
    out = []
    for dlt in range(1, N_DEV):
        px = 1 - x if dlt & 4 else x
        py = 1 - y if dlt & 2 else y
        pc = 1 - c if dlt & 1 else c
        out.append((dlt, (px, py, pc), 4 * px + 2 * py + pc))
    return out


def _scatter_start(arrays):
    n = len(arrays)
    ns = n * (N_DEV - 1)

    def body(*refs):
        ins, lands = refs[:n], refs[n:2 * n]
        send_sems, recv_sems = refs[2 * n:2 * n + ns], refs[2 * n + ns:2 * n + 2 * ns]
        token = refs[-1]
        x, y, c = lax.axis_index("x"), lax.axis_index("y"), lax.axis_index("c")
        me = 4 * x + 2 * y + c
        for dlt, peer, pi in _peers(x, y, c):
            for i in range(n):
                k = i * (N_DEV - 1) + dlt - 1
                pltpu.make_async_remote_copy(
                    src_ref=ins[i].at[pi], dst_ref=lands[i].at[me], send_sem=send_sems[k], recv_sem=recv_sems[k],
                    device_id=peer, device_id_type=pl.DeviceIdType.MESH).start()
        token[...] = jnp.zeros_like(token)

    sem = pltpu.SemaphoreType.DMA(())
    thru = tuple(pltpu.HBM(a.shape, a.dtype) for a in arrays)
    hbm = lambda a: pltpu.with_memory_space_constraint(a, pltpu.HBM)
    outs = pl.pallas_call(
        body, name="scatter_start", out_shape=(sem,) * (2 * ns) + thru + thru + (SDS((8, LANES), F32),),
        in_specs=[_HBM] * (2 * n),
        out_specs=(_SEM,) * (2 * ns) + (_HBM,) * (2 * n) + (pl.BlockSpec(memory_space=pltpu.VMEM),),
        input_output_aliases={i: 2 * ns + i for i in range(2 * n)},
        compiler_params=pltpu.CompilerParams(has_side_effects=pltpu.SideEffectType.DATAFLOW_SIDE_EFFECTING),
    )(*[hbm(a) for a in arrays], *[hbm(jnp.zeros(a.shape, a.dtype)) for a in arrays])
    return outs[:ns], outs[ns:2 * ns], outs[2 * ns:2 * ns + n], outs[2 * ns + n:2 * ns + 2 * n], outs[-1]


def _scatter_wait(send_sems, recv_sems, srcs, lands, after):
    n = len(srcs)
    ns = n * (N_DEV - 1)

    def body(*refs):
        ins, lands_ = refs[:n], refs[n:2 * n]
        send, recv = refs[2 * n:2 * n + ns], refs[2 * n + ns:2 * n + 2 * ns]
        x, y, c = lax.axis_index("x"), lax.axis_index("y"), lax.axis_index("c")
        for dlt, peer, pi in _peers(x, y, c):
            for i in range(n):
                k = i * (N_DEV - 1) + dlt - 1
                cp = pltpu.make_async_remote_copy(
                    src_ref=ins[i].at[pi], dst_ref=lands_[i].at[pi], send_sem=send[k], recv_sem=recv[k],
                    device_id=peer, device_id_type=pl.DeviceIdType.MESH)
                cp.wait_send()
                cp.wait_recv()

    thru = tuple(pltpu.HBM(a.shape, a.dtype) for a in srcs)
    outs = pl.pallas_call(
        body, name="scatter_wait", out_shape=thru + thru,
        in_specs=[_HBM] * (2 * n) + [_SEM] * (2 * ns) + [pl.BlockSpec(memory_space=pl.ANY)],
        out_specs=(_HBM,) * (2 * n), input_output_aliases={i: i for i in range(2 * n)},
        compiler_params=pltpu.CompilerParams(has_side_effects=pltpu.SideEffectType.DATAFLOW_SIDE_EFFECTING),
    )(*srcs, *lands, *send_sems, *recv_sems, after)
    return outs[n:]


def _adaln_mod(c, w_mod, b_mod):
    def body(c_ref, w_ref, b_ref, mod_ref, sc_ref):
        sc = _silu(c_ref[...])
        sc8 = jnp.broadcast_to(sc, (8, D_MODEL))
        mod_ref[...] = _nn(sc8, w_ref[...])[0:1] + b_ref[...]
        sc_ref[...] = sc

    return pl.pallas_call(body, name="adaln_mod", compiler_params=_params(),
                          out_shape=(SDS((1, 3 * D_MODEL), F32), SDS((1, D_MODEL), F32)))(c, w_mod, b_mod)


def _ln_proj(x, mod, norm_w, ws, cos_t, sin_t, conv_w8, alog_row, dtb_row, ts):
    s = x.shape[0]
    widths = [w.shape[1] for w in ws]
    assert ts == PERM_BLK

    def body(x_ref, mod_ref, nw_ref, cos_ref, sin_ref, cw_ref, al_ref, dtb_ref, wqkv, wz, wba, waq, wak, wav, waz,
             h_ref, oqkv, oz, oba, oq, ok, ov, oaz, q_ref, k_ref, v_ref, bg_ref, halo):
        n = pl.program_id(0)
        xt = x_ref[...]
        r = lax.rsqrt(jnp.mean(xt * xt, axis=-1, keepdims=True) + EPS)
        shift, scale = mod_ref[:, 0:D_MODEL], mod_ref[:, D_MODEL:2 * D_MODEL]
        h = ((xt * r) * nw_ref[...]) * (1.0 + scale) + shift
        hb = _bf(h)
        h_ref[...] = hb
        hp = jnp.dot(_plane_perm(ts, False), hb, preferred_element_type=F32)
        pre = jnp.dot(hb, wqkv[...], preferred_element_type=F32)
        ba = jnp.dot(hb, wba[...], preferred_element_type=F32)
        hp = _bf(hp)
        tq = jnp.dot(hp, waq[...], preferred_element_type=F32)
        tk = jnp.dot(hp, wak[...], preferred_element_type=F32)
        tv = jnp.dot(hp, wav[...], preferred_element_type=F32)
        tz = jnp.dot(hb, wz[...], preferred_element_type=F32)
        taz = jnp.dot(hb, waz[...], preferred_element_type=F32)
        cs, sn = cos_ref[...], sin_ref[...]
        rows = ts // PLANES
        for t, o_ref in ((tq, oq), (tk, ok)):
            for j in range(AT_PAIRS):
                tj = t[:, j * LANES:(j + 1) * LANES]
                rot = tj * cs + _swap_half64(tj) * sn
                for r in range(PLANES):
                    o_ref[j, r] = rot[r * rows:(r + 1) * rows]
        oqkv[...] = pre
        ext = jnp.concatenate([jnp.where(n == 0, 0.0, halo[...]), pre], axis=0)
        halo[...] = pre[ts - 8:ts]
        taps = _conv_taps(ext, ts)
        conv = taps[0] * cw_ref[0:1, :]
        for j in range(1, CONV_K):
            conv = conv + taps[j] * cw_ref[j:j + 1, :]
        for hd in range(DN_HEADS):
            cols = slice(hd * DN_DIM, (hd + 1) * DN_DIM)
            q_ref[:, cols] = _post_q(conv[:, hd * DN_DIM:(hd + 1) * DN_DIM])
            k_ref[:, cols] = _post_k(conv[:, DN_WIDTH + hd * DN_DIM:DN_WIDTH + (hd + 1) * DN_DIM])
            v_ref[:, cols] = _post_v(conv[:, 2 * DN_WIDTH + hd * DN_DIM:2 * DN_WIDTH + (hd + 1) * DN_DIM])
        oba[...] = ba
        bg = _beta_decay(ba, al_ref[...], dtb_ref[...])
        lane = lax.broadcasted_iota(jnp.int32, bg.shape, 1)
        run = pltpu.roll(_chunk_cumsum(bg), DN_HEADS, 1)
        bg_ref[...] = jnp.where((lane >= GC_LANE) & (lane < GC_LANE + DN_HEADS), run, bg)
        for j in range(AT_PAIRS):
            for r in range(PLANES):
                ov[j, r] = tv[r * rows:(r + 1) * rows, j * LANES:(j + 1) * LANES]
        oz[...] = tz
        oaz[...] = taz

    tok = lambda w: pl.BlockSpec((ts, w), lambda i: (i, 0))
    full = lambda a: pl.BlockSpec(a.shape, lambda i: (0, 0))
    pairs = pl.BlockSpec((AT_PAIRS, PLANES, ts // PLANES, LANES), lambda i: (0, 0, i, 0))
    return pl.pallas_call(
        body, name="ln_proj", grid=(s // ts,), compiler_params=_params("arbitrary"),
        in_specs=[tok(D_MODEL), full(mod), full(norm_w), tok(LANES), tok(LANES), full(conv_w8), full(alog_row),
                  full(dtb_row)] + [full(w) for w in ws],
        out_specs=(tok(D_MODEL), tok(widths[0]), tok(widths[1]), tok(widths[2]), pairs, pairs, pairs,
                   tok(widths[6]), tok(DN_WIDTH), tok(DN_WIDTH), tok(DN_WIDTH), tok(BA_PAD)),
        out_shape=(SDS((s, D_MODEL), BF16), SDS((s, widths[0]), F32), SDS((s, widths[1]), F32),
                   SDS((s, widths[2]), F32)) + (SDS((AT_PAIRS, PLANES, s // PLANES, LANES), F32),) * 3 + (SDS((s, widths[6]), F32),)
        + (SDS((s, DN_WIDTH), F32),) * 3 + (SDS((s, BA_PAD), F32),),
        scratch_shapes=[pltpu.VMEM((8, widths[0]), F32)],
    )(x, mod, norm_w, cos_t, sin_t, conv_w8, alog_row, dtb_row, *ws)


def _conv_taps(ext, rows):
    taps = []
    for j in range(CONV_K):
        sh = CONV_K - 1 - j
        rolled = pltpu.roll(ext, sh, 0) if sh else ext
        taps.append(rolled[8:8 + rows])
    return taps


def _dn_forward(q, k, v, bg):
    s = q.shape[0]
    tp = CH_UNROLL * CHUNK
    npass = s // tp
    hs = range(DN_HEADS)
    sl = [slice(h * DN_DIM, (h + 1) * DN_DIM) for h in hs]

    def body(q_ref, k_ref, v_ref, bg_ref, w_ref, qd_ref, kd_ref, p_ref, gl_ref, t_ref, o_ref, vn_ref, st_ref,
             state, u_s, w_s, qd_s, kd_s, p_s, gl_s):
        @pl.when(pl.program_id(0) == 0)
        def _():
            for ref in (state, u_s, w_s, qd_s, kd_s, p_s, gl_s):
                ref[...] = jnp.zeros_like(ref)

        def recurrence():
            for c in range(CH_UNROLL):
                rows = slice(c * CHUNK, (c + 1) * CHUNK)
                rows8 = slice(c * 8, (c + 1) * 8)
                srows = slice(c * DN_DIM, (c + 1) * DN_DIM)
                sf = [state[h] for h in hs]
                sb = [_bf(x) for x in sf]
                ws = [_nn(w_s[rows, cl], b) for cl, b in zip(sl, sb)]
                qs = [_nn(qd_s[rows, cl], b) for cl, b in zip(sl, sb)]
                yield
                vn = [u_s[rows, cl] - x for cl, x in zip(sl, ws)]
                vb = [_bf(x) for x in vn]
                kv = [_tn(kd_s[rows, cl], b) for cl, b in zip(sl, vb)]
                pv = [_nn(p_s[h, rows, :], b) for h, b in zip(hs, vb)]
                for h in hs:
                    state[h] = sf[h] * gl_s[rows8, sl[h]][0:1] + kv[h]
                for h in hs:
                    st_ref[srows, sl[h]] = sf[h]
                    vn_ref[rows, sl[h]] = vn[h]
                    o_ref[rows, sl[h]] = qs[h] + pv[h]
                yield

        steps = recurrence()

        where = [(slice(c * CHUNK, (c + 1) * CHUNK), slice(c * 8, (c + 1) * 8), h, sl[h])
                 for c in range(CH_UNROLL) for h in hs]
        bgs = [bg_ref[rows, :] for rows, _, _, _ in where]
        outs = _chunk_fwd([q_ref[rows, cl] for rows, _, _, cl in where], [k_ref[rows, cl] for rows, _, _, cl in where],
                          [v_ref[rows, cl] for rows, _, _, cl in where],
                          [b[:, h:h + 1] for b, (_, _, h, _) in zip(bgs, where)],
                          [b[:, GC_LANE + h:GC_LANE + h + 1] for b, (_, _, h, _) in zip(bgs, where)],
                          tick=lambda: next(steps, None))
        for _ in steps:
            pass
        for (rows, rows8, h, cl), (u, w, p, qd, kd, gl, t) in zip(where, outs):
            g8 = jnp.broadcast_to(gl, (8, DN_DIM))
            u_s[rows, cl] = u
            w_ref[rows, cl] = w
            w_s[rows, cl] = w
            qd_ref[rows, cl] = qd
            qd_s[rows, cl] = qd
            kd_ref[rows, cl] = kd
            kd_s[rows, cl] = kd
            p_ref[h, rows, :] = p
            p_s[h, rows, :] = p
            gl_ref[rows8, cl] = g8
            gl_s[rows8, cl] = g8
            t_ref[h, rows, :] = t

    cur = lambda i: jnp.minimum(i, npass - 1)
    done = lambda i: jnp.maximum(i - 1, 0)
    tokc = pl.BlockSpec((tp, DN_WIDTH), lambda i: (cur(i), 0))
    tokd = pl.BlockSpec((tp, DN_WIDTH), lambda i: (done(i), 0))
    sq = pl.BlockSpec((DN_HEADS, tp, CHUNK), lambda i: (0, cur(i), 0))
    return pl.pallas_call(
        body, name="dn_forward", grid=(npass + 1,), compiler_params=_params("arbitrary"),
        in_specs=[tokc] * 3 + [pl.BlockSpec((tp, BA_PAD), lambda i: (cur(i), 0))],
        out_specs=(tokc, tokc, tokc, sq, pl.BlockSpec((CH_UNROLL * 8, DN_WIDTH), lambda i: (cur(i), 0)), sq,
                   tokd, tokd, pl.BlockSpec((CH_UNROLL * DN_DIM, DN_WIDTH), lambda i: (done(i), 0))),
        out_shape=(SDS((s, DN_WIDTH), F32),) * 3 + (SDS((DN_HEADS, s, CHUNK), F32),
                                                     SDS((s // CHUNK * 8, DN_WIDTH), F32),
                                                     SDS((DN_HEADS, s, CHUNK), F32),
                                                     SDS((s, DN_WIDTH), F32), SDS((s, DN_WIDTH), F32),
                                                     SDS((s // CHUNK * DN_DIM, DN_WIDTH), F32)),
        scratch_shapes=[pltpu.VMEM((DN_HEADS, DN_DIM, DN_DIM), F32)] + [pltpu.VMEM((tp, DN_WIDTH), F32)] * 4
        + [pltpu.VMEM((DN_HEADS, tp, CHUNK), F32), pltpu.VMEM((CH_UNROLL * 8, DN_WIDTH), F32)],
    )(q, k, v, bg)


LOG2E, LN2 = 1.4426950408889634, 0.6931471805599453
MASKED = -1e30


PLANE_ROWS = ATT_BLK // PLANES


def _to_planes(tile, scr):
    scr[...] = tile
    return [scr[pl.ds(r, tile.shape[0] // PLANES, stride=PLANES), :] for r in range(PLANES)]


def _from_planes(planes, scr):
    n = planes[0].shape[0]
    for r in range(PLANES):
        scr[pl.ds(r, n, stride=PLANES), :] = planes[r]
    return scr[...]


def _plane_perm(n, back):
    row = lax.broadcasted_iota(jnp.int32, (n, n), 0)
    col = lax.broadcasted_iota(jnp.int32, (n, n), 1)
    m, c = (col, row) if back else (row, col)
    return jnp.where(c == PLANES * (m % (n // PLANES)) + m // (n // PLANES), 1.0, 0.0).astype(BF16)


def _geom(d):
    nchunk = PLANES // d
    return nchunk, Q_BLOCK // nchunk


def _pattern_bias(d):
    nchunk, qlen = _geom(d)
    row = lax.broadcasted_iota(jnp.int32, (Q_BLOCK, 2 * Q_BLOCK), 0)
    col = lax.broadcasted_iota(jnp.int32, (Q_BLOCK, 2 * Q_BLOCK), 1)
    uq, aq = row // qlen, row % qlen
    uk, ak = col // (2 * qlen), col % (2 * qlen)
    rel = nchunk * (aq - ak + qlen) + (uq - uk)
    band = jnp.where((rel >= 0) & (rel <= W_SUB), 0.0, MASKED)
    col1 = lax.broadcasted_iota(jnp.int32, (1, 2 * Q_BLOCK), 1)
    return band, (col1 % (2 * qlen)) < qlen


def _combo(c, d):
    nchunk, qlen = _geom(d)
    r0, mm = c % d, c // d
    planes = [r0 + d * u for u in range(nchunk)]
    qs = pl.multiple_of(qlen * mm, 8)
    ks = pl.multiple_of(PLANE_ROWS + qlen * mm - qlen, 8)
    return planes, qs, ks, qlen, mm == 0


def _gather(ref, lead, planes, start, n):
    parts = [ref[lead + (p, pl.ds(start, n), slice(None))] for p in planes]
    return parts[0] if len(parts) == 1 else jnp.concatenate(parts, axis=0)


def _scatter(ref, lead, planes, start, n, val, add):
    for u, p in enumerate(planes):
        idx = lead + (p, pl.ds(start, n), slice(None))
        if add:
            ref[idx] += val[u * n:(u + 1) * n]
        else:
            ref[idx] = val[u * n:(u + 1) * n]


def _shift_in(ext, cur, n):
    @pl.when(n == 0)
    def _():
        ext[:, 0:PLANE_ROWS, :] = jnp.zeros((PLANES, PLANE_ROWS, LANES), F32)

    @pl.when(n > 0)
    def _():
        ext[:, 0:PLANE_ROWS, :] = ext[:, PLANE_ROWS:2 * PLANE_ROWS, :]

    ext[:, PLANE_ROWS:2 * PLANE_ROWS, :] = cur


def _attn_fwd(qr, kr, vv):
    s16 = qr.shape[2]
    nblk = s16 // PLANE_ROWS
    scale = AT_DIM ** -0.5
    npat = len(DILATIONS)

    def keys(prev_ref, cur_ref, planes, mm, ql):
        parts = []
        for p in planes:
            if mm == 0:
                parts += [prev_ref[0, p, PLANE_ROWS - ql:PLANE_ROWS, :], cur_ref[0, p, 0:ql, :]]
            else:
                parts.append(cur_ref[0, p, ql * (mm - 1):ql * (mm + 1), :])
        return jnp.concatenate(parts, axis=0)

    def body(q_ref, kp_ref, k_ref, vp_ref, v_ref, o_ref, lse_ref, o_p, l_p):
        n = pl.program_id(1)
        lo = lax.broadcasted_iota(jnp.int32, (Q_BLOCK, LANES), 1) < AT_DIM
        nq = ATT_BLK // Q_BLOCK
        heads = [(i, sel) for i in range(nq) for sel in (lo, ~lo)]
        for pi, d in enumerate(DILATIONS):
            band, prev_cols = _pattern_bias(d)
            nchunk, ql = _geom(d)
            cs = [([c % d + d * u for u in range(nchunk)], c // d) for c in range(nq)]
            band0 = band + jnp.where(prev_cols & (n == 0), MASKED, 0.0)
            bias = [band0 if mm == 0 else band for _, mm in cs]
            qb = [_bf(_gather(q_ref, (0,), pls, ql * mm, ql)) for pls, mm in cs]
            kk = [_bf(keys(kp_ref, k_ref, pls, mm, ql)) for pls, mm in cs]
            vb = [_bf(keys(vp_ref, v_ref, pls, mm, ql)) for pls, mm in cs]
            sc = [lax.dot_general(jnp.where(sel, qb[i], jnp.zeros_like(qb[i])), kk[i], _NT,
                                  preferred_element_type=F32) for i, sel in heads]
            sc = [x * (scale * LOG2E) + bias[i] for x, (i, _) in zip(sc, heads)]
            mx = [jnp.max(x, axis=-1, keepdims=True) for x in sc]
            pr = [jnp.exp2(x - m) for x, m in zip(sc, mx)]
            ls = [jnp.sum(x, axis=-1, keepdims=True) for x in pr]
            pv = [jnp.dot(_bf(x), vb[i], preferred_element_type=F32) for x, (i, _) in zip(pr, heads)]
            outs = [x / l for x, l in zip(pv, ls)]
            lses = [m * LN2 + jnp.log(l) for m, l in zip(mx, ls)]
            for i, (pls, mm) in enumerate(cs):
                _scatter(o_p, (pi,), pls, ql * mm, ql, jnp.where(lo, outs[2 * i], outs[2 * i + 1]), False)
                _scatter(l_p, (pi,), pls, ql * mm, ql, jnp.where(lo, lses[2 * i], lses[2 * i + 1]), False)

        def merge(r, carry):
            ls = [l_p[pi, r] for pi in range(npat)]
            mx = jnp.maximum(jnp.maximum(ls[0], ls[1]), ls[2])
            es = [jnp.exp(l - mx) for l in ls]
            den = es[0] + es[1] + es[2]
            o_ref[0, r] = (es[0] * o_p[0, r] + es[1] * o_p[1, r] + es[2] * o_p[2, r]) / den
            lse_ref[0, r] = mx + jnp.log(den)
            return carry

        lax.fori_loop(0, PLANES, merge, 0)

    blk = pl.BlockSpec((1, PLANES, PLANE_ROWS, LANES), lambda j, n: (j, 0, n, 0))
    prev = pl.BlockSpec((1, PLANES, PLANE_ROWS, LANES), lambda j, n: (j, 0, jnp.maximum(n - 1, 0), 0))
    return pl.pallas_call(
        body, name="attn_fwd", grid=(AT_PAIRS, nblk), compiler_params=_params("arbitrary", "arbitrary"),
        in_specs=[blk, prev, blk, prev, blk], out_specs=(blk, blk),
        out_shape=(SDS(qr.shape, F32),) * 2,
        scratch_shapes=[pltpu.VMEM((npat, PLANES, PLANE_ROWS, LANES), F32)] * 2,
    )(qr, kr, kr, vv, vv)


def _out_loss(o_dn, z_dn, o_at, z_at, dnw, atw2, x, tgt, w_out, gate, fw, ts):
    s = x.shape[0]

    def body(odn, zdn, oat, zat, dnw_ref, atw_ref, x_ref, t_ref, w_ref, g_ref, fw_ref,
             dx2_ref, gw_ref, dfw_ref, dgate_ref, loss_ref, dodn, dzdn, doat, dzat, delta, ddnw, datw, perm):
        @pl.when(pl.program_id(0) == 0)
        def _():
            for ref in (gw_ref, dfw_ref, dgate_ref, loss_ref, ddnw, datw):
                ref[...] = jnp.zeros_like(ref)

        parts, vjps = [], []
        for h in range(DN_HEADS):
            cols = slice(h * DN_DIM, (h + 1) * DN_DIM)
            y, vjp = jax.vjp(_gate_dn, odn[:, cols], zdn[:, cols], dnw_ref[...])
            parts.append(_bf(y))
            vjps.append(vjp)
        oats = [_from_planes([oat[j, r] for r in range(PLANES)], perm.at[j]) for j in range(AT_PAIRS)]
        for j in range(AT_PAIRS):
            y, vjp = jax.vjp(functools.partial(_gate_at, head_sum=_d_head_sum), oats[j],
                             zat[:, j * LANES:(j + 1) * LANES], atw_ref[...])
            parts.append(_bf(y))
            vjps.append(vjp)
        catb = jnp.concatenate(parts, axis=1)
        wb = w_ref[...]
        gate, fwv = g_ref[...], fw_ref[...]
        mix = jnp.dot(catb, wb, preferred_element_type=F32)
        x2 = x_ref[...] + gate * mix
        r2 = lax.rsqrt(jnp.mean(x2 * x2, axis=-1, keepdims=True) + EPS)
        xn2 = x2 * r2
        err = xn2 * fwv - t_ref[...]
        row = jnp.sum(err * err, axis=-1, keepdims=True) * (1.0 / D_MODEL)
        loss_ref[...] += 0.5 * jnp.sum(row, axis=0, keepdims=True)
        dy = err * (1.0 / D_MODEL)
        dfw_ref[...] += jnp.sum(dy * xn2, axis=0, keepdims=True)
        dxn = dy * fwv
        dx2 = r2 * (dxn - xn2 * jnp.mean(dxn * xn2, axis=-1, keepdims=True))
        dx2_ref[...] = dx2
        dgate_ref[...] += jnp.sum(dx2 * mix, axis=0, keepdims=True)
        dmix = _bf(gate * dx2)
        dcat = lax.dot_general(dmix, wb, _NT, preferred_element_type=F32)
        gw_ref[...] += lax.dot_general(catb, dmix, _TN, preferred_element_type=F32)
        for h in range(DN_HEADS):
            cols = slice(h * DN_DIM, (h + 1) * DN_DIM)
            do, dz, dw = vjps[h](dcat[:, cols])
            dodn[:, cols] = do
            dzdn[:, cols] = _bf(dz)
            ddnw[...] += dw
        for j in range(AT_PAIRS):
            cols = slice(j * LANES, (j + 1) * LANES)
            do, dz, dw = vjps[DN_HEADS + j](dcat[:, DN_WIDTH + j * LANES:DN_WIDTH + (j + 1) * LANES])
            for r, x in enumerate(_to_planes(do, perm.at[j])):
                doat[j, r] = x
            dzat[:, cols] = _bf(dz)
            datw[...] += dw
            for r, x in enumerate(_to_planes(_head_sum(do * oats[j]), perm.at[j])):
                delta[j, r] = x

    tok = lambda w: pl.BlockSpec((ts, w), lambda i: (i, 0))
    full = lambda a: pl.BlockSpec(a.shape, lambda i: (0, 0))
    row = pl.BlockSpec((1, D_MODEL), lambda i: (0, 0))
    lrow = pl.BlockSpec((1, LANES), lambda i: (0, 0))
    pairs = pl.BlockSpec((AT_PAIRS, PLANES, ts // PLANES, LANES), lambda i: (0, 0, i, 0))
    return pl.pallas_call(
        body, name="out_loss", grid=(s // ts,), compiler_params=_params("arbitrary"),
        in_specs=[tok(DN_WIDTH), tok(DN_WIDTH), pairs, tok(AT_WIDTH), full(dnw), full(atw2),
                  tok(D_MODEL), tok(D_MODEL), full(w_out), full(gate), full(fw)],
        out_specs=(tok(D_MODEL), pl.BlockSpec((D_MODEL, D_MODEL), lambda i: (0, 0)), row, row,
                   pl.BlockSpec((1, 1), lambda i: (0, 0)), tok(DN_WIDTH), tok(DN_WIDTH), pairs, tok(AT_WIDTH), pairs,
                   lrow, lrow),
        out_shape=(SDS((s, D_MODEL), F32), SDS((D_MODEL, D_MODEL), F32), SDS((1, D_MODEL), F32),
                   SDS((1, D_MODEL), F32), SDS((1, 1), F32), SDS((s, DN_WIDTH), F32), SDS((s, DN_WIDTH), BF16),
                   SDS((AT_PAIRS, PLANES, s // PLANES, LANES), F32), SDS((s, AT_WIDTH), BF16),
                   SDS((AT_PAIRS, PLANES, s // PLANES, LANES), F32), SDS((1, LANES), F32), SDS((1, LANES), F32)),
        scratch_shapes=[pltpu.VMEM((AT_PAIRS, ts, LANES), F32)],
    )(o_dn, z_dn, o_at, z_at, dnw, atw2, x, tgt, w_out, gate, fw)


def _shift_acc(ext, n):
    @pl.when(n == 0)
    def _():
        ext[:, 0:PLANE_ROWS, :] = jnp.zeros((PLANES, PLANE_ROWS, LANES), F32)

    @pl.when(n > 0)
    def _():
        ext[:, 0:PLANE_ROWS, :] = ext[:, PLANE_ROWS:2 * PLANE_ROWS, :]

    ext[:, PLANE_ROWS:2 * PLANE_ROWS, :] = jnp.zeros((PLANES, PLANE_ROWS, LANES), F32)


def _attn_bwd(qr, kr, vv, do, lse, delta):
    s16 = qr.shape[2]
    nblk = s16 // PLANE_ROWS
    scale = AT_DIM ** -0.5

    def body(q_ref, k_ref, v_ref, do_ref, lse_ref, dl_ref, dq_ref, dk_ref, dv_ref, kext, vext, dkext, dvext):
        n = pl.program_id(1)
        _shift_in(kext, k_ref[0], n)
        _shift_in(vext, v_ref[0], n)
        _shift_acc(dkext, n)
        _shift_acc(dvext, n)

        @pl.when(n < nblk)
        def _():
            dq_ref[0] = jnp.zeros((PLANES, PLANE_ROWS, LANES), F32)
            lo = lax.broadcasted_iota(jnp.int32, (Q_BLOCK, LANES), 1) < AT_DIM
            for d in DILATIONS:
                band, prev_cols = _pattern_bias(d)

                def group(g, carry, d=d, band=band, prev_cols=prev_cols):
                    nu = ATT_UNROLL_BWD
                    cs = [_combo(g * nu + u, d) for u in range(nu)]
                    heads = [(i, sel) for i in range(nu) for sel in (lo, ~lo)]
                    bias = [band + jnp.where(prev_cols & ((n == 0) & m0), MASKED, 0.0) for _, _, _, _, m0 in cs]
                    qb = [_bf(_gather(q_ref, (0,), pls, qs, ql)) for pls, qs, _, ql, _ in cs]
                    dob = [_bf(_gather(do_ref, (0,), pls, qs, ql)) for pls, qs, _, ql, _ in cs]
                    kk = [_bf(_gather(kext, (), pls, ks, 2 * ql)) for pls, _, ks, ql, _ in cs]
                    vb = [_bf(_gather(vext, (), pls, ks, 2 * ql)) for pls, _, ks, ql, _ in cs]
                    lse2 = [_gather(lse_ref, (0,), pls, qs, ql) * LOG2E for pls, qs, _, ql, _ in cs]
                    dl2 = [_gather(dl_ref, (0,), pls, qs, ql) for pls, qs, _, ql, _ in cs]
                    qm = [jnp.where(sel, qb[i], jnp.zeros_like(qb[i])) for i, sel in heads]
                    dom = [jnp.where(sel, dob[i], jnp.zeros_like(dob[i])) for i, sel in heads]
                    lse_c = [jnp.max(jnp.where(sel, lse2[i], -jnp.inf), axis=-1, keepdims=True) for i, sel in heads]
                    dl_c = [jnp.max(jnp.where(sel, dl2[i], -jnp.inf), axis=-1, keepdims=True) for i, sel in heads]
                    sc = [lax.dot_general(a, kk[i], _NT, preferred_element_type=F32) for a, (i, _) in zip(qm, heads)]
                    dp = [lax.dot_general(a, vb[i], _NT, preferred_element_type=F32) for a, (i, _) in zip(dom, heads)]
                    pr = [jnp.exp2(x * (scale * LOG2E) + bias[i] - l) for x, l, (i, _) in zip(sc, lse_c, heads)]
                    ds = [_bf(p * (x - dl) * scale) for p, x, dl in zip(pr, dp, dl_c)]
                    prb = [_bf(p) for p in pr]
                    dq = [jnp.dot(x, kk[i], preferred_element_type=F32) for x, (i, _) in zip(ds, heads)]
                    dk = [lax.dot_general(x, a, _TN, preferred_element_type=F32) for x, a in zip(ds, qm)]
                    dv = [lax.dot_general(x, a, _TN, preferred_element_type=F32) for x, a in zip(prb, dom)]
                    for i, (pls, qs, ks, ql, _) in enumerate(cs):
                        _scatter(dq_ref, (0,), pls, qs, ql, jnp.where(lo, dq[2 * i], dq[2 * i + 1]), True)
                        _scatter(dkext, (), pls, ks, 2 * ql, dk[2 * i] + dk[2 * i + 1], True)
                        _scatter(dvext, (), pls, ks, 2 * ql, dv[2 * i] + dv[2 * i + 1], True)
                    return carry

                lax.fori_loop(0, ATT_BLK // Q_BLOCK // ATT_UNROLL_BWD, group, 0)

        dk_ref[0] = dkext[:, 0:PLANE_ROWS, :]
        dv_ref[0] = dvext[:, 0:PLANE_ROWS, :]

    cur = pl.BlockSpec((1, PLANES, PLANE_ROWS, LANES), lambda j, n: (j, 0, jnp.minimum(n, nblk - 1), 0))
    done = pl.BlockSpec((1, PLANES, PLANE_ROWS, LANES), lambda j, n: (j, 0, jnp.maximum(n - 1, 0), 0))
    return pl.pallas_call(
        body, name="attn_bwd", grid=(AT_PAIRS, nblk + 1), compiler_params=_params("arbitrary", "arbitrary"),
        in_specs=[cur] * 6, out_specs=(cur, done, done),
        out_shape=(SDS(qr.shape, F32),) * 3,
        scratch_shapes=[pltpu.VMEM((PLANES, 2 * PLANE_ROWS, LANES), F32)] * 4,
    )(qr, kr, vv, do, lse, delta)


def _dn_backward(do, st, vn, w, qd, kd, p, gl, q, k, v, bg, t):
    s = do.shape[0]
    nc = CH_UNROLL_BWD
    tp = nc * CHUNK
    npass = s // tp
    hs = range(DN_HEADS)
    sl = [slice(h * DN_DIM, (h + 1) * DN_DIM) for h in hs]

    def body(do_ref, st_ref, vn_ref, w_ref, qd_ref, kd_ref, p_ref, gl_ref, q_ref, k_ref, v_ref, bg_ref, t_ref,
             dq_ref, dk_ref, dv_ref, dbg_ref, dstate, du_s, dw_s, dqd_s, dkd_s, dp_s, dgl_s):
        @pl.when(pl.program_id(0) == 0)
        def _():
            for ref in (dstate, du_s, dw_s, dqd_s, dkd_s, dp_s, dgl_s):
                ref[...] = jnp.zeros_like(ref)

        where = [(slice(c * CHUNK, (c + 1) * CHUNK), slice(c * 8, (c + 1) * 8), h, sl[h])
                 for c in range(nc) for h in hs]
        cots = [(du_s[rows, cl], dw_s[rows, cl], dp_s[h, rows, :], dqd_s[rows, cl], dkd_s[rows, cl],
                 dgl_s[rows8, cl][0:1, 0:1]) for rows, rows8, h, cl in where]

        def recurrence():
            for c in reversed(range(nc)):
                rows = slice(c * CHUNK, (c + 1) * CHUNK)
                rows8 = slice(c * 8, (c + 1) * 8)
                srows = slice(c * DN_DIM, (c + 1) * DN_DIM)
                ds_ = [dstate[h] for h in hs]
                dsb = [_bf(x) for x in ds_]
                dob = [_bf(do_ref[rows, cl]) for cl in sl]
                pdo = [_tn(p_ref[h, rows, :], b) for h, b in zip(hs, dob)]
                qdo = [_tn(qd_ref[rows, cl], b) for cl, b in zip(sl, dob)]
                kds = [_nn(kd_ref[rows, cl], b) for cl, b in zip(sl, dsb)]
                yield
                dvn = [a + b for a, b in zip(kds, pdo)]
                dvb = [_bf(x) for x in dvn]
                wdv = [_tn(w_ref[rows, cl], b) for cl, b in zip(sl, dvb)]
                for h in hs:
                    dstate[h] = ds_[h] * gl_ref[rows8, sl[h]][0:1] + qdo[h] - wdv[h]
                sfs = [st_ref[srows, cl] for cl in sl]
                sbs = [_bf(x) for x in sfs]
                vnb = [_bf(vn_ref[rows, cl]) for cl in sl]
                for h in hs:
                    du_s[rows, sl[h]] = dvn[h]
                    dw_s[rows, sl[h]] = -_nt(dvb[h], sbs[h])
                    dqd_s[rows, sl[h]] = _nt(dob[h], sbs[h])
                    dkd_s[rows, sl[h]] = _nt(vnb[h], dsb[h])
                    dp_s[h, rows, :] = _nt(dob[h], vnb[h])
                    dgl = jnp.sum(jnp.sum(ds_[h] * sfs[h], axis=1, keepdims=True), axis=0, keepdims=True)
                    dgl_s[rows8, sl[h]] = jnp.broadcast_to(dgl, (8, DN_DIM))
                yield

        steps = recurrence()

        bgs = [bg_ref[rows, :] for rows, _, _, _ in where]
        outs = _chunk_bwd([q_ref[rows, cl] for rows, _, _, cl in where], [k_ref[rows, cl] for rows, _, _, cl in where],
                          [v_ref[rows, cl] for rows, _, _, cl in where],
                          [b[:, h:h + 1] for b, (_, _, h, _) in zip(bgs, where)],
                          [b[:, GC_LANE + h:GC_LANE + h + 1] for b, (_, _, h, _) in zip(bgs, where)],
                          [t_ref[h, rows, :] for rows, _, h, _ in where], cots, tick=lambda: next(steps, None))
        for _ in steps:
            pass
        lane = lax.broadcasted_iota(jnp.int32, (CHUNK, BA_PAD), 1)
        for c in range(nc):
            dbg = jnp.zeros((CHUNK, BA_PAD), F32)
            for (rows, _, h, cl), (dq, dk, dv, dbeta, dgc) in list(zip(where, outs))[c * DN_HEADS:(c + 1) * DN_HEADS]:
                dq_ref[rows, cl] = dq
                dk_ref[rows, cl] = dk
                dv_ref[rows, cl] = dv
                dbg = dbg + jnp.where(lane == h, dbeta, 0.0) + jnp.where(lane == GC_LANE + h, dgc, 0.0)
            dbg_ref[where[c * DN_HEADS][0], :] = dbg

    rec = lambda i: jnp.maximum(npass - 1 - i, 0)
    loc = lambda i: jnp.minimum(npass - i, npass - 1)
    tok_r = pl.BlockSpec((tp, DN_WIDTH), lambda i: (rec(i), 0))
    tok_l = pl.BlockSpec((tp, DN_WIDTH), lambda i: (loc(i), 0))
    sq_r = pl.BlockSpec((DN_HEADS, tp, CHUNK), lambda i: (0, rec(i), 0))
    sq_l = pl.BlockSpec((DN_HEADS, tp, CHUNK), lambda i: (0, loc(i), 0))
    ba_l = pl.BlockSpec((tp, BA_PAD), lambda i: (loc(i), 0))
    return pl.pallas_call(
        body, name="dn_backward", grid=(npass + 1,), compiler_params=_params("arbitrary"),
        in_specs=[tok_r, pl.BlockSpec((nc * DN_DIM, DN_WIDTH), lambda i: (rec(i), 0)), tok_r, tok_r, tok_r, tok_r,
                  sq_r, pl.BlockSpec((nc * 8, DN_WIDTH), lambda i: (rec(i), 0)),
                  tok_l, tok_l, tok_l, ba_l, sq_l],
        out_specs=(tok_l, tok_l, tok_l, ba_l),
        out_shape=(SDS((s, DN_WIDTH), F32),) * 3 + (SDS((s, BA_PAD), F32),),
        scratch_shapes=[pltpu.VMEM((DN_HEADS, DN_DIM, DN_DIM), F32)] + [pltpu.VMEM((tp, DN_WIDTH), F32)] * 4
        + [pltpu.VMEM((DN_HEADS, tp, CHUNK), F32), pltpu.VMEM((nc * 8, DN_WIDTH), F32)],
    )(do, st, vn, w, qd, kd, p, gl, q, k, v, bg, t)


def _dn_prep_bwd(qkv_pre, ba, dq, dk, dv, dbg, conv_w8, alog_row, dtb_row, hbf, dz_dn, ts):
    s = qkv_pre.shape[0]
    cw = 3 * DN_WIDTH
    nt = s // ts

    def body(pre_ref, ph_ref, nh_ref, ba_ref, dq_ref, dqh_ref, dk_ref, dkh_ref, dv_ref, dvh_ref, dbg_ref,
             cw_ref, al_ref, dtb_ref, h_ref, dz_ref, dpre_ref, dba_ref, dcw_ref, dal_ref, ddtb_ref,
             gqkv_out, gz_out, gba_out, gqkv_ref, gz_ref, gba_ref):
        n = pl.program_id(0)

        @pl.when(n == 0)
        def _():
            gqkv_ref[...] = jnp.zeros_like(gqkv_ref)
            gz_ref[...] = jnp.zeros_like(gz_ref)
            gba_ref[...] = jnp.zeros_like(gba_ref)
            dcw_ref[...] = jnp.zeros_like(dcw_ref)
            dal_ref[...] = jnp.zeros_like(dal_ref)
            ddtb_ref[...] = jnp.zeros_like(ddtb_ref)

        hb = h_ref[...]
        gz_ref[...] += lax.dot_general(hb, dz_ref[...], _TN, preferred_element_type=F32)
        last = n == nt - 1
        prev = jnp.where(n == 0, 0.0, ph_ref[...])
        ext = jnp.concatenate([prev, pre_ref[...], nh_ref[...]], axis=0)
        taps = _conv_taps(ext, ts + 8)
        conv = taps[0] * cw_ref[0:1, :]
        for j in range(1, CONV_K):
            conv = conv + taps[j] * cw_ref[j:j + 1, :]

        def cot(main, halo, cols):
            return jnp.concatenate([main[:, cols], jnp.where(last, 0.0, halo[:, cols])], axis=0)

        rows = ts + 8
        for grp, (fn, mref, href) in enumerate(((_post_q, dq_ref, dqh_ref), (_post_k, dk_ref, dkh_ref),
                                                (_post_v, dv_ref, dvh_ref))):
            gcols = slice(grp * DN_WIDTH, (grp + 1) * DN_WIDTH)
            pieces = []
            for h in range(DN_HEADS):
                cols = slice(h * DN_DIM, (h + 1) * DN_DIM)
                c0 = grp * DN_WIDTH + h * DN_DIM
                _, vjp = jax.vjp(fn, conv[:, c0:c0 + DN_DIM])
                pieces.append(vjp(cot(mref, href, cols))[0])
            dconv = jnp.concatenate(pieces, axis=1)
            dpre = dconv[:ts] * cw_ref[CONV_K - 1:CONV_K, gcols]
            for j in range(CONV_K - 1):
                sh = CONV_K - 1 - j
                dpre = dpre + pltpu.roll(dconv, rows - sh, 0)[:ts] * cw_ref[j:j + 1, gcols]
            dpre_b = _bf(dpre)
            dpre_ref[:, gcols] = dpre_b
            gqkv_ref[:, gcols] += lax.dot_general(hb, dpre_b, _TN, preferred_element_type=F32)
            for j in range(CONV_K):
                dcw_ref[j:j + 1, gcols] += jnp.sum(dconv[:ts] * taps[j][:ts, gcols], axis=0, keepdims=True)

        dbg = dbg_ref[...]
        lane = lax.broadcasted_iota(jnp.int32, dbg.shape, 1)
        dg = pltpu.roll(_chunk_cumsum(dbg, reverse=True), BA_PAD - DN_HEADS, 1)
        cot_bg = jnp.where(lane < DN_HEADS, dbg, jnp.where(lane < GC_LANE, dg, 0.0))
        _, vjp = jax.vjp(_beta_decay, ba_ref[...], al_ref[...], dtb_ref[...])
        dba, dal, ddtb = vjp(cot_bg)
        dba_b = _bf(dba)
        dba_ref[...] = dba_b
        gba_ref[...] += lax.dot_general(hb, dba_b, _TN, preferred_element_type=F32)
        dal_ref[...] += dal
        ddtb_ref[...] += ddtb

        @pl.when(last)
        def _():
            gqkv_out[...] = _bf(gqkv_ref[...])
            gz_out[...] = _bf(gz_ref[...])
            gba_out[...] = _bf(gba_ref[...])

    tok = lambda w: pl.BlockSpec((ts, w), lambda i: (i, 0))
    full = lambda a: pl.BlockSpec(a.shape, lambda i: (0, 0))
    prevh = lambda w: pl.BlockSpec((8, w), lambda i: (jnp.maximum(i * (ts // 8) - 1, 0), 0))
    nexth = lambda w: pl.BlockSpec((8, w), lambda i: (jnp.minimum((i + 1) * (ts // 8), s // 8 - 1), 0))
    row = pl.BlockSpec((1, LANES), lambda i: (0, 0))
    return pl.pallas_call(
        body, name="dn_prep_bwd", grid=(nt,), compiler_params=_params("arbitrary"),
        in_specs=[tok(cw), prevh(cw), nexth(cw), tok(BA_PAD),
                  tok(DN_WIDTH), nexth(DN_WIDTH), tok(DN_WIDTH), nexth(DN_WIDTH), tok(DN_WIDTH), nexth(DN_WIDTH),
                  tok(BA_PAD), full(conv_w8), full(alog_row), full(dtb_row), tok(D_MODEL), tok(DN_WIDTH)],
        out_specs=(tok(cw), tok(BA_PAD), pl.BlockSpec((8, cw), lambda i: (0, 0)), row, row)
        + tuple(pl.BlockSpec((D_MODEL, w), lambda i: (0, 0)) for w in (cw, DN_WIDTH, BA_PAD)),
        out_shape=(SDS((s, cw), BF16), SDS((s, BA_PAD), BF16), SDS((8, cw), F32), SDS((1, LANES), F32),
                   SDS((1, LANES), F32)) + tuple(SDS((D_MODEL, w), BF16) for w in (cw, DN_WIDTH, BA_PAD)),
        scratch_shapes=[pltpu.VMEM((D_MODEL, w), F32) for w in (cw, DN_WIDTH, BA_PAD)],
    )(qkv_pre, qkv_pre, qkv_pre, ba, dq, dq, dk, dk, dv, dv, dbg, conv_w8, alog_row, dtb_row, hbf, dz_dn)


def _dh_dx(dps, ws, x, mod, norm_w, dx2, ts):
    s = x.shape[0]
    widths = [w.shape[1] for w in ws]
    np_ = len(ws)

    def body(*refs):
        dp_refs, w_refs = refs[:np_], refs[np_:2 * np_]
        x_ref, mod_ref, nw_ref, dx2_ref, gx_ref, dshift, dscale, dnw = refs[2 * np_:]

        @pl.when(pl.program_id(0) == 0)
        def _():
            dshift[...] = jnp.zeros_like(dshift)
            dscale[...] = jnp.zeros_like(dscale)
            dnw[...] = jnp.zeros_like(dnw)

        dh = lax.dot_general(dp_refs[0][...], w_refs[0][...], _NT, preferred_element_type=F32)
        for a, b in zip(dp_refs[1:], w_refs[1:]):
            dh = dh + lax.dot_general(a[...], b[...], _NT, preferred_element_type=F32)
        xt = x_ref[...]
        r = lax.rsqrt(jnp.mean(xt * xt, axis=-1, keepdims=True) + EPS)
        xn = xt * r
        nw = nw_ref[...]
        sc1 = 1.0 + mod_ref[:, D_MODEL:2 * D_MODEL]
        dshift[...] += jnp.sum(dh, axis=0, keepdims=True)
        dscale[...] += jnp.sum(dh * (xn * nw), axis=0, keepdims=True)
        dnw[...] += jnp.sum(dh * sc1 * xn, axis=0, keepdims=True)
        dxn = dh * sc1 * nw
        gx_ref[...] = r * (dxn - xn * jnp.mean(dxn * xn, axis=-1, keepdims=True)) + dx2_ref[...]

    tok = lambda w: pl.BlockSpec((ts, w), lambda i: (i, 0))
    full = lambda a: pl.BlockSpec(a.shape, lambda i: (0, 0))
    row = pl.BlockSpec((1, D_MODEL), lambda i: (0, 0))
    return pl.pallas_call(
        body, name="dh_dx", grid=(s // ts,), compiler_params=_params("arbitrary"),
        in_specs=[tok(w) for w in widths] + [full(w) for w in ws] + [tok(D_MODEL), full(mod), full(norm_w),
                                                                    tok(D_MODEL)],
        out_specs=(tok(D_MODEL), row, row, row),
        out_shape=(SDS((s, D_MODEL), F32),) + (SDS((1, D_MODEL), F32),) * 3,
    )(*dps, *ws, x, mod, norm_w, dx2)


def _grad_w_in_at(h, dq, dk, dv, dz_at, cos_t, sin_t, ts):
    s = h.shape[0]
    assert ts % PERM_BLK == 0

    def body(h_ref, q_ref, k_ref, v_ref, dz_ref, cos_ref, sin_ref, oq, ok, ov, gq_out, gk_out, gv_out, gz_out,
             gq, gk, gv, gz):
        @pl.when(pl.program_id(0) == 0)
        def _():
            for o in (gq, gk, gv, gz):
                o[...] = jnp.zeros_like(o)

        hb = h_ref[...]
        gz[...] += lax.dot_general(hb, dz_ref[...], _TN, preferred_element_type=F32)
        back = _plane_perm(PERM_BLK, True)
        rows = PERM_BLK // PLANES
        halves = [(slice(i * PERM_BLK, (i + 1) * PERM_BLK), slice(i * rows, (i + 1) * rows))
                  for i in range(ts // PERM_BLK)]

        def planes(ref, j, prow):
            return jnp.concatenate([ref[j, r, prow, :] for r in range(PLANES)], axis=0)

        for trow, prow in halves:
            vp = jnp.concatenate([_bf(planes(v_ref, j, prow)) for j in range(AT_PAIRS)], axis=1)
            ov[trow, :] = _bf(jnp.dot(back, vp, preferred_element_type=F32))
        gv[...] += lax.dot_general(hb, ov[...], _TN, preferred_element_type=F32)
        for g_ref, o_ref, acc in ((q_ref, oq, gq), (k_ref, ok, gk)):
            for trow, prow in halves:
                cs, sn = cos_ref[trow, :], sin_ref[trow, :]
                gs = [planes(g_ref, j, prow) for j in range(AT_PAIRS)]
                gp = jnp.concatenate([_bf(g * cs + _swap_half64(g * sn)) for g in gs], axis=1)
                o_ref[trow, :] = _bf(jnp.dot(back, gp, preferred_element_type=F32))
            acc[...] += lax.dot_general(hb, o_ref[...], _TN, preferred_element_type=F32)

        @pl.when(pl.program_id(0) == s // ts - 1)
        def _():
            for o, a in ((gq_out, gq), (gk_out, gk), (gv_out, gv), (gz_out, gz)):
                o[...] = _bf(a[...])

    tok = lambda w: pl.BlockSpec((ts, w), lambda i: (i, 0))
    pairs = pl.BlockSpec((AT_PAIRS, PLANES, ts // PLANES, LANES), lambda i: (0, 0, i, 0))
    acc = pl.BlockSpec((D_MODEL, AT_WIDTH), lambda i: (0, 0))
    return pl.pallas_call(
        body, name="grad_w_in_at", grid=(s // ts,), compiler_params=_params("arbitrary"),
        in_specs=[tok(D_MODEL), pairs, pairs, pairs, tok(AT_WIDTH), tok(LANES), tok(LANES)],
        out_specs=(tok(AT_WIDTH),) * 3 + (acc,) * 4,
        out_shape=(SDS((s, AT_WIDTH), BF16),) * 3 + (SDS((D_MODEL, AT_WIDTH), BF16),) * 4,
        scratch_shapes=[pltpu.VMEM((D_MODEL, AT_WIDTH), F32)] * 4,
    )(h, dq, dk, dv, dz_at, cos_t, sin_t)


def _adamw_math(w, g, m, v):
    m = ADAM_B1 * m + (1.0 - ADAM_B1) * g
    v = ADAM_B2 * v + (1.0 - ADAM_B2) * (g * g)
    m_hat = m / (1.0 - ADAM_B1 ** ADAM_STEP)
    v_hat = v / (1.0 - ADAM_B2 ** ADAM_STEP)
    delta = -ADAM_LR * (m_hat / (jnp.sqrt(v_hat) + ADAM_EPS) + ADAM_WD * w)
    return delta, m, v


def _adamw(w, m, v, g, name, own=None):
    def body(w_ref, m_ref, v_ref, g_ref, *rest):
        g_out, d_out, m_out, v_out = rest[-4:]
        if own is None:
            g = g_ref[...]
        else:
            g = g_ref[0].astype(F32)
            for k in range(1, N_DEV):
                g = g + g_ref[k].astype(F32)
            g = g + rest[0][...].astype(F32)
        g_out[...] = g
        d_out[...], m_out[...], v_out[...] = _adamw_math(w_ref[...], g, m_ref[...], v_ref[...])

    args = (w, m, v, g) if own is None else (w, m, v, g, own)
    return pl.pallas_call(body, name=name, compiler_params=_params(),
                          out_shape=(SDS(w.shape, F32),) * 4)(*args)


def _adamw_w_mod(w, m, v, siluc_all, dmod_mine):
    def body(w_ref, m_ref, v_ref, sc_ref, dm_ref, g_out, d_out, m_out, v_out):
        g = _htn(sc_ref[...], dm_ref[...])
        g_out[...] = g
        d_out[...], m_out[...], v_out[...] = _adamw_math(w_ref[...], g, m_ref[...], v_ref[...])

    return pl.pallas_call(body, name="adamw_w_mod", compiler_params=_params(),
                          out_shape=(SDS(w.shape, F32),) * 4)(w, m, v, siluc_all, dmod_mine)


def _pack_sum(pack_all):
    def body(p_ref, o_ref):
        t = p_ref[0]
        for k in range(1, N_DEV):
            t = t + p_ref[k]
        o_ref[...] = t

    return pl.pallas_call(body, name="pack_sum", out_shape=SDS(pack_all.shape[1:], F32))(pack_all)


def _tile(s, want):
    t = min(want, s)
    assert s % t == 0
    return t


def _local_step(x, c, positions, w_mod_bf, b_mod, norm_w, w_in_bf, conv_w, a_log, dt_bias, dn_norm_w, at_norm_w,
                w_out_bf, final_norm_w, tgt):
    s = x.shape[0]
    o = [0]
    for wdt in IN_SPLITS:
        o.append(o[-1] + wdt)
    w_ba = jnp.pad(w_in_bf[:, o[2]:o[4]], ((0, 0), (0, BA_PAD - 2 * DN_HEADS)))
    ws = [w_in_bf[:, o[0]:o[1]], w_in_bf[:, o[1]:o[2]], w_ba, w_in_bf[:, o[4]:o[5]], w_in_bf[:, o[5]:o[6]],
          w_in_bf[:, o[6]:o[7]], w_in_bf[:, o[7]:o[8]]]
    conv_w8 = jnp.pad(conv_w, ((0, 8 - CONV_K), (0, 0)))
    alog_row = jnp.pad(a_log, ((0, 0), (DN_HEADS, BA_PAD - 2 * DN_HEADS)))
    dtb_row = jnp.pad(dt_bias, ((0, 0), (DN_HEADS, BA_PAD - 2 * DN_HEADS)))
    atw2 = jnp.concatenate([at_norm_w, at_norm_w], axis=1)

    half = AT_DIM // 2
    lane = jnp.arange(LANES)
    inv_freq = ROPE_THETA ** (-(lane % half).astype(F32) / half)
    pos = positions.reshape(s // PERM_BLK, PERM_BLK // PLANES, PLANES).transpose(0, 2, 1).reshape(s)
    ang = pos.astype(F32)[:, None] * inv_freq
    cos_t = jnp.cos(ang)
    sin_t = jnp.sin(ang) * jnp.where((lane // half) % 2 == 0, -1.0, 1.0)

    mod, siluc = _adaln_mod(c, w_mod_bf, b_mod)
    gate = mod[:, 2 * D_MODEL:]
    hbf, qkv_pre, z_dn, ba, qr, kr, vb, z_at, q, k, v, bg = _ln_proj(
        x, mod, norm_w, ws, cos_t, sin_t, conv_w8, alog_row, dtb_row, _tile(s, 256))
    w, qd, kd, p, gl, tinv, o_dn, vn, st = _dn_forward(q, k, v, bg)
    o_at, lse = _attn_fwd(qr, kr, vb)
    (dx2, gw_out, dfw, dgate, loss, do_dn, dz_dn, do_at, dz_at, delta, ddnw, datw) = _out_loss(
        o_dn, z_dn, o_at, z_at, dn_norm_w, atw2, x, tgt, w_out_bf, gate, final_norm_w, _tile(s, 512))

    daq, dak, dav, g_aq, g_ak, g_av, g_az = _grad_w_in_at(hbf, *_attn_bwd(qr, kr, vb, do_at, lse, delta), dz_at,
                                                           cos_t, sin_t, _tile(s, 512))
    dq, dk, dv, dbg = _dn_backward(do_dn, st, vn, w, qd, kd, p, gl, q, k, v, bg, tinv)
    dqkv, dba, dcw, dal, ddtb, g_qkv, g_z, g_ba = _dn_prep_bwd(qkv_pre, ba, dq, dk, dv, dbg, conv_w8, alog_row, dtb_row,
                                                               hbf, dz_dn, _tile(s, 512))
    dps = [dqkv, dz_dn, dba, daq, dak, dav, dz_at]
    gw_in = jnp.concatenate([g_qkv, g_z, g_ba[:, :2 * DN_HEADS], g_aq, g_ak, g_av, g_az], axis=1)
    small = dict(conv=dcw[:CONV_K], dgate=dgate, siluc=siluc, dfw=dfw, alog=dal, dtb=ddtb, dnn=ddnw, atn=datw)

    def input_grad(token):
        gx, dshift, dscale, dnw = _dh_dx(dps, ws, x, mod + token, norm_w, dx2, _tile(s, 512))
        return gx, jnp.concatenate([dshift, dscale, small["dgate"]], axis=1), dnw

    return loss, gw_in, gw_out, small, input_grad


def kernel(x, c, positions, w_mod, b_mod, norm_w, w_in, conv_w, a_log, dt_bias, dn_norm_w, at_norm_w, w_out, final_norm_w, loss_target, m_w_mod, m_b_mod, m_norm_w, m_w_in, m_conv_w, m_a_log, m_dt_bias, m_dn_norm_w, m_at_norm_w, m_w_out, m_final_norm_w, v_w_mod, v_b_mod, v_norm_w, v_w_in, v_conv_w, v_a_log, v_dt_bias, v_dn_norm_w, v_at_norm_w, v_w_out, v_final_norm_w):
    me = 4 * lax.axis_index("x") + 2 * lax.axis_index("y") + lax.axis_index("c")
    s = x.shape[1]

    g_mod, g_in, g_conv, g_out = _all_gather(
        [_bf(w_mod[0]), _bf(w_in[0]), conv_w[0], _bf(w_out[0])], "gather_weights")
    w_mod_bf = g_mod.transpose(1, 0, 2).reshape(D_MODEL, 3 * D_MODEL)
    w_in_bf = g_in.transpose(1, 0, 2).reshape(D_MODEL, IN_COLS)
    conv_full = g_conv.transpose(1, 0, 2).reshape(CONV_K, 3 * DN_WIDTH)
    w_out_bf = g_out.reshape(D_MODEL, D_MODEL)

    loss, gw_in, gw_out, small, input_grad = _local_step(
        x[0], c, positions[0], w_mod_bf, b_mod, norm_w, w_in_bf, conv_full, a_log, dt_bias, dn_norm_w, at_norm_w,
        w_out_bf, final_norm_w.reshape(1, D_MODEL), loss_target[0])

    gw_in_slabs = gw_in.reshape(D_MODEL, N_DEV, IN_SHARD).transpose(1, 0, 2)
    gw_out_slabs = _bf(gw_out).reshape(N_DEV, D_MODEL // N_DEV, D_MODEL)
    send_sems, recv_sems, srcs, lands, token = _scatter_start([gw_in_slabs, gw_out_slabs])
    gx, dmod, dnw = input_grad(token[0, 0])
    r_in, r_out = _scatter_wait(send_sems, recv_sems, srcs, lands, gx)
    own_in = lax.dynamic_index_in_dim(gw_in_slabs, me, 0, keepdims=False)
    own_out = lax.dynamic_index_in_dim(gw_out_slabs, me, 0, keepdims=False)

    pack = jnp.concatenate([small["conv"].reshape(1, -1), dmod, small["siluc"], dnw, small["dfw"],
                            small["alog"], small["dtb"], small["dnn"], small["atn"],
                            jnp.pad(loss, ((0, 0), (0, LANES - 1)))], axis=1).reshape(PK_ROWS, LANES)
    (pack_all,) = _exchange([pack], [False], "exchange_small")

    res = {}
    res["w_in"] = _adamw(w_in[0], m_w_in[0], v_w_in[0], r_in, "adamw_w_in", own=own_in)
    res["w_out"] = _adamw(w_out[0], m_w_out[0], v_w_out[0], r_out, "adamw_w_out", own=own_out)
    flat_all = pack_all.reshape(N_DEV, PK_END)
    dmod_mine = lax.dynamic_slice(flat_all, (0, PK_DMOD + me * (3 * D_MODEL // N_DEV)), (N_DEV, 3 * D_MODEL // N_DEV))
    res["w_mod"] = _adamw_w_mod(w_mod[0], m_w_mod[0], v_w_mod[0], flat_all[:, PK_SILUC:PK_DNW], dmod_mine)
    tot = _pack_sum(pack_all).reshape(1, PK_END)
    g_conv_full = tot[:, PK_CONV:PK_DMOD].reshape(CONV_K, 3 * DN_WIDTH)
    g_conv_mine = lax.dynamic_slice(g_conv_full, (0, me * (3 * DN_WIDTH // N_DEV)), (CONV_K, 3 * DN_WIDTH // N_DEV))
    res["conv_w"] = _adamw(conv_w[0], m_conv_w[0], v_conv_w[0], g_conv_mine, "adamw_conv_w")
    res["b_mod"] = _adamw(b_mod, m_b_mod, v_b_mod, tot[:, PK_DMOD:PK_SILUC], "adamw_b_mod")
    res["norm_w"] = _adamw(norm_w, m_norm_w, v_norm_w, tot[:, PK_DNW:PK_DFW], "adamw_norm_w")
    res["a_log"] = _adamw(a_log, m_a_log, v_a_log, tot[:, PK_ALOG + DN_HEADS:PK_ALOG + 2 * DN_HEADS], "adamw_a_log")
    res["dt_bias"] = _adamw(dt_bias, m_dt_bias, v_dt_bias, tot[:, PK_DTB + DN_HEADS:PK_DTB + 2 * DN_HEADS],
                            "adamw_dt_bias")
    res["dn_norm_w"] = _adamw(dn_norm_w, m_dn_norm_w, v_dn_norm_w, tot[:, PK_DNN:PK_ATN], "adamw_dn_norm_w")
    g_atn = tot[:, PK_ATN:PK_ATN + AT_DIM] + tot[:, PK_ATN + AT_DIM:PK_LOSS]
    res["at_norm_w"] = _adamw(at_norm_w, m_at_norm_w, v_at_norm_w, g_atn, "adamw_at_norm_w")
    fin = _adamw(final_norm_w.reshape(1, D_MODEL), m_final_norm_w.reshape(1, D_MODEL),
                 v_final_norm_w.reshape(1, D_MODEL), tot[:, PK_DFW:PK_ALOG], "adamw_final_norm_w")
    res["final_norm_w"] = tuple(a.reshape(D_MODEL) for a in fin)

    lead = ("w_mod", "w_in", "conv_w", "w_out")
    names = ("w_mod", "b_mod", "norm_w", "w_in", "conv_w", "a_log", "dt_bias", "dn_norm_w", "at_norm_w", "w_out",
             "final_norm_w")
    out = [tot[0, PK_LOSS], gx.reshape(1, s, D_MODEL)]
    for kind in range(4):
        for nm in names:
            a = res[nm][kind]
            out.append(a[None] if nm in lead else a)
    return tuple(out)
```

```python
import functools

import jax
import jax.numpy as jnp
from jax import lax
from jax.experimental import pallas as pl
from jax.experimental.pallas import tpu as pltpu

F32, BF16 = jnp.float32, jnp.bfloat16
HI = lax.Precision.HIGHEST
SDS = jax.ShapeDtypeStruct

D_MODEL = 1024
DN_HEADS, DN_DIM, DN_WIDTH = 4, 128, 512
AT_HEADS, AT_DIM, AT_WIDTH = 8, 64, 512
CONV_K = 4
CHUNK = 64
Q_BLOCK = 128
W_SUB = 128
DILATIONS = (1, 4, 16)
AT_PAIRS = 4
PLANES = 16
PERM_BLK = 256
ATT_BLK = Q_BLOCK * max(DILATIONS)
ATT_UNROLL_BWD = 4
CH_UNROLL, CH_UNROLL_BWD = 4, 8
ROPE_THETA = 10000.0
EPS = 1e-6
N_DEV = 8
LANES = 128
BA_PAD = 128
IN_SPLITS = (1536, 512, 4, 4, 512, 512, 512, 512)
IN_COLS = sum(IN_SPLITS)
IN_SHARD = IN_COLS // N_DEV
VMEM_LIMIT = 58 * 2 ** 20

ADAM_LR, ADAM_B1, ADAM_B2, ADAM_EPS, ADAM_WD, ADAM_STEP = 0.001, 0.9, 0.999, 1e-08, 0.01, 10

PK_CONV, PK_DMOD, PK_SILUC, PK_DNW, PK_DFW, PK_ALOG, PK_DTB, PK_DNN, PK_ATN, PK_LOSS, PK_END = (
    0, 6144, 9216, 10240, 11264, 12288, 12416, 12544, 12672, 12800, 12928)
PK_ROWS = PK_END // LANES

_NT = (((1,), (1,)), ((), ()))
_TN = (((0,), (0,)), ((), ()))


def _params(*sem):
    return pltpu.CompilerParams(dimension_semantics=sem or None, vmem_limit_bytes=VMEM_LIMIT)


def _bf(x):
    return x.astype(BF16)


def _nn(a, b):
    return jnp.dot(_bf(a), _bf(b), preferred_element_type=F32)


def _nt(a, b):
    return lax.dot_general(_bf(a), _bf(b), _NT, preferred_element_type=F32)


def _tn(a, b):
    return lax.dot_general(_bf(a), _bf(b), _TN, preferred_element_type=F32)


def _htn(a, b):
    return lax.dot_general(a, b, _TN, precision=HI, preferred_element_type=F32)


def _head_sum(x):
    r = lax.broadcasted_iota(jnp.int32, (LANES, LANES), 0)
    c = lax.broadcasted_iota(jnp.int32, (LANES, LANES), 1)
    same = jnp.where((r // AT_DIM) == (c // AT_DIM), 1.0, 0.0).astype(BF16)
    hi, lo = _hl(x)
    return jnp.dot(hi, same, preferred_element_type=F32) + jnp.dot(lo, same, preferred_element_type=F32)


@jax.custom_vjp
def _d_head_sum(x):
    return _head_sum(x)


_d_head_sum.defvjp(lambda x: (_head_sum(x), None), lambda _, g: (_head_sum(g),))


def _silu(x):
    return x * jax.nn.sigmoid(x)


def _softplus(x):
    return jnp.maximum(x, 0.0) + jnp.log(1.0 + jnp.exp(-jnp.abs(x)))


def _l2n(x):
    return x * lax.rsqrt(jnp.sum(x * x, axis=-1, keepdims=True) + EPS)


def _post_q(x):
    return _l2n(_silu(x)) * (DN_DIM ** -0.5)


def _post_k(x):
    return _l2n(_silu(x))


def _post_v(x):
    return _silu(x)


def _beta_decay(ba, alog_row, dtb_row):
    lane = lax.broadcasted_iota(jnp.int32, ba.shape, 1)
    return jnp.where(lane < DN_HEADS, jax.nn.sigmoid(ba), -jnp.exp(alog_row) * _softplus(ba + dtb_row))


def _gate_dn(o, z, w):
    return (o * lax.rsqrt(jnp.mean(o * o, axis=-1, keepdims=True) + EPS)) * w * _silu(z)


def _gate_at(o, z, w2, head_sum):
    ms = head_sum(o * o) * (1.0 / AT_DIM)
    return (o * lax.rsqrt(ms + EPS)) * w2 * _silu(z)


def _swap_half64(x):
    lane = lax.broadcasted_iota(jnp.int32, x.shape, 1)
    return jnp.where((lane & (AT_DIM - 1)) < AT_DIM // 2, pltpu.roll(x, LANES - AT_DIM // 2, 1),
                     pltpu.roll(x, AT_DIM // 2, 1))


_NN = (((1,), (0,)), ((), ()))


def _hl(a):
    hi = a.astype(BF16)
    return hi, (a - hi.astype(F32)).astype(BF16)


def _mm3(a, b, dims=_NN):
    (ah, al), (bh, bl) = a, b
    f = lambda x, y: lax.dot_general(x, y, dims, preferred_element_type=F32)
    return f(ah, bh) + (f(ah, bl) + f(al, bh))


def _chunk_masks():
    r = lax.broadcasted_iota(jnp.int32, (CHUNK, CHUNK), 0)
    c = lax.broadcasted_iota(jnp.int32, (CHUNK, CHUNK), 1)
    return r >= c, r > c, (r == c).astype(F32), (r // 16) == (c // 16)


def _tri_inv(mats, tick=lambda: None):
    _, _, eye, blk = _chunk_masks()
    dg = [jnp.where(blk, a, 0.0) for a in mats]
    lo = [jnp.where(blk, 0.0, a) for a in mats]
    sdg = [_hl(x) for x in dg]
    d2 = [_mm3(s, s) for s in sdg]
    tick()
    sd2 = [_hl(x) for x in d2]
    d4 = [_mm3(s, s) for s in sd2]
    tick()
    sd4 = [_hl(x) for x in d4]
    d8 = [_mm3(s, s) for s in sd4]
    tick()
    p1 = [_mm3(_hl(eye - a), _hl(eye + b)) for a, b in zip(dg, d2)]
    tick()
    p2 = [_mm3(_hl(a), _hl(eye + b)) for a, b in zip(p1, d4)]
    tick()
    dinv = [_mm3(_hl(a), _hl(eye + b)) for a, b in zip(p2, d8)]
    tick()
    sdinv = [_hl(x) for x in dinv]
    n1 = [_mm3(s, _hl(b)) for s, b in zip(sdinv, lo)]
    tick()
    sn1 = [_hl(x) for x in n1]
    n2 = [_mm3(s, s) for s in sn1]
    tick()
    q1 = [_mm3(_hl(eye - a), _hl(eye + b)) for a, b in zip(n1, n2)]
    return [_mm3(_hl(a), s) for a, s in zip(q1, sdinv)]


def _chunk_common(qs, ks, vs, betas, gcs):
    tril, _, _, _ = _chunk_masks()
    out = []
    for q, k, v, beta, gc in zip(qs, ks, vs, betas, gcs):
        gb = jnp.broadcast_to(gc, (CHUNK, DN_DIM))
        gt = gb.T[:CHUNK, :]
        gam = jnp.where(tril, jnp.exp(jnp.where(tril, gb[:, :CHUNK] - gt, 0.0)), 0.0)
        last = gb[CHUNK - 1:CHUNK, :]
        eg, e2 = jnp.exp(gb), jnp.exp(last - gb)
        kb, vb = k * beta, v * beta
        out.append(dict(gam=gam, eg=eg, e2=e2, gl=jnp.exp(last[:, 0:1]), kb=kb, vb=vb, kbg=kb * eg,
                        m=_nt(kb, k), qk=_nt(q, k)))
    return out


def _chunk_fwd(qs, ks, vs, betas, gcs, tick=lambda: None):
    tril, strict, _, _ = _chunk_masks()
    cm = _chunk_common(qs, ks, vs, betas, gcs)
    ts = _tri_inv([jnp.where(strict, c["m"] * c["gam"], 0.0) for c in cm], tick)
    outs = []
    for q, k, c, t in zip(qs, ks, cm, ts):
        uw = _nn(t, jnp.concatenate([c["vb"], c["kbg"]], axis=1))
        p = jnp.where(tril, c["qk"] * c["gam"], 0.0)
        outs.append((uw[:, :DN_DIM], uw[:, DN_DIM:], p, q * c["eg"], k * c["e2"], c["gl"], t.T))
    return outs


def _chunk_bwd(qs, ks, vs, betas, gcs, ts, cots, tick=lambda: None):
    tril, strict, _, _ = _chunk_masks()
    cm = _chunk_common(qs, ks, vs, betas, gcs)
    tick()
    row = lax.broadcasted_iota(jnp.int32, (CHUNK, 1), 0)
    ones = jnp.ones((CHUNK, DN_DIM), BF16)
    rs = lambda x: jnp.sum(x, axis=-1, keepdims=True)
    tts = [_bf(t) for t in ts]
    duw = [_bf(jnp.concatenate([ct[0], ct[1]], axis=1)) for ct in cots]
    dts = [_nt(a, jnp.concatenate([c["vb"], c["kbg"]], axis=1)) for a, c in zip(duw, cm)]
    tick()
    xs = [_nn(t, d) for t, d in zip(tts, dts)]
    tick()
    das = [jnp.where(strict, -_nn(x, t), 0.0) for x, t in zip(xs, tts)]
    dvks = [_nn(t, a) for t, a in zip(tts, duw)]
    tick()
    outs = []
    every = max(1, len(qs) // 5)
    for idx, (q, k, v, beta, c, ct, da, dvk) in enumerate(zip(qs, ks, vs, betas, cm, cots, das, dvks)):
        if idx and idx % every == 0:
            tick()
        _, _, dp, dqd, dkd, dgl = ct
        dvb, dkbg = dvk[:, :DN_DIM], dvk[:, DN_DIM:]
        dm = da * c["gam"]
        dqk = jnp.where(tril, dp, 0.0) * c["gam"]
        e = dm * c["m"] + dqk * c["qk"]
        dmq = jnp.concatenate([dm, dqk], axis=0)
        r1 = _nn(dmq, k)
        dkb = r1[:CHUNK] + dkbg * c["eg"]
        dq = r1[CHUNK:] + dqd * c["eg"]
        dk = _tn(dmq, jnp.concatenate([c["kb"], q], axis=0)) + dkd * c["e2"] + dkb * beta
        dbeta = rs(dkb * k + dvb * v)
        eh, el = _hl(e)
        colsum = (lax.dot_general(eh, ones, _TN, preferred_element_type=F32)
                  + lax.dot_general(el, ones, _TN, preferred_element_type=F32))[:, 0:1]
        pkd = dkd * (k * c["e2"])
        dgc = rs(e) - colsum + rs(dqd * q * c["eg"] + dkbg * c["kbg"] - pkd)
        tail = rs(jnp.sum(pkd, axis=0, keepdims=True)) + dgl * c["gl"]
        dgc = dgc + jnp.where(row == CHUNK - 1, tail, 0.0)
        outs.append((dq, dk, dvb * beta, dbeta, dgc))
    return outs


def _chunk_cumsum(x, reverse=False):
    n = x.shape[0]
    pos = lax.broadcasted_iota(jnp.int32, x.shape, 0) & (CHUNK - 1)
    sh = 1
    while sh < CHUNK:
        if reverse:
            x = x + jnp.where(pos < CHUNK - sh, pltpu.roll(x, n - sh, 0), 0.0)
        else:
            x = x + jnp.where(pos >= sh, pltpu.roll(x, sh, 0), 0.0)
        sh *= 2
    return x


GC_LANE = 2 * DN_HEADS


def _exchange(arrays, scatter, name):
    n = len(arrays)
    out_shapes = []
    for a, sc in zip(arrays, scatter):
        out_shapes.append(SDS(a.shape if sc else (N_DEV,) + a.shape, a.dtype))

    def body(*refs):
        ins, outs = refs[:n], refs[n:2 * n]
        send_sems, recv_sems, loc_sems = refs[2 * n:]
        x, y, c = lax.axis_index("x"), lax.axis_index("y"), lax.axis_index("c")
        me = 4 * x + 2 * y + c
        local, remote = [], []
        for i in range(n):
            src = ins[i].at[me] if scatter[i] else ins[i]
            cp = pltpu.make_async_copy(src, outs[i].at[me], loc_sems.at[i])
            cp.start()
            local.append(cp)
        for dlt in range(1, N_DEV):
            px = 1 - x if dlt & 4 else x
            py = 1 - y if dlt & 2 else y
            pc = 1 - c if dlt & 1 else c
            peer = 4 * px + 2 * py + pc
            for i in range(n):
                src = ins[i].at[peer] if scatter[i] else ins[i]
                cp = pltpu.make_async_remote_copy(
                    src_ref=src, dst_ref=outs[i].at[me],
                    send_sem=send_sems.at[i, dlt - 1], recv_sem=recv_sems.at[i, dlt - 1],
                    device_id=(px, py, pc), device_id_type=pl.DeviceIdType.MESH)
                cp.start()
                arrive = pltpu.make_async_remote_copy(
                    src_ref=src, dst_ref=outs[i].at[peer],
                    send_sem=send_sems.at[i, dlt - 1], recv_sem=recv_sems.at[i, dlt - 1],
                    device_id=(px, py, pc), device_id_type=pl.DeviceIdType.MESH)
                remote.append((cp, arrive))
        for cp, arrive in remote:
            cp.wait_send()
            arrive.wait_recv()
        for cp in local:
            cp.wait()

    any_spec = pl.BlockSpec(memory_space=pl.ANY)
    return pl.pallas_call(
        body, name=name, out_shape=tuple(out_shapes),
        in_specs=[any_spec] * n, out_specs=tuple([any_spec] * n),
        scratch_shapes=[pltpu.SemaphoreType.DMA((n, N_DEV - 1)), pltpu.SemaphoreType.DMA((n, N_DEV - 1)),
                        pltpu.SemaphoreType.DMA((n,))],
    )(*arrays)


def _all_gather(arrays, name):
    n = len(arrays)

    def body(*refs):
        ins, outs = refs[:n], refs[n:2 * n]
        send_sems, recv_sems, loc_sems = refs[2 * n:]
        x, y, c = lax.axis_index("x"), lax.axis_index("y"), lax.axis_index("c")
        me, sibling = (x, y, c), (x, y, 1 - c)
        chips = [(1 - x, y), (x, 1 - y), (1 - x, 1 - y)]

        def copy(i, k, block, to, src=None):
            slot = outs[i].at[4 * block[0] + 2 * block[1] + block[2]]
            return pltpu.make_async_remote_copy(
                src_ref=slot if src is None else src, dst_ref=slot,
                send_sem=send_sems.at[i, k], recv_sem=recv_sems.at[i, k],
                device_id=to, device_id_type=pl.DeviceIdType.MESH)

        mine = [pltpu.make_async_copy(ins[i], outs[i].at[4 * x + 2 * y + c], loc_sems.at[i]) for i in range(n)]
        for cp in mine:
            cp.start()
        first = []
        for i in range(n):
            first.append(copy(i, 0, me, sibling, src=ins[i]))
            first += [copy(i, 1 + j, me, (*chip, c), src=ins[i]) for j, chip in enumerate(chips)]
        for cp in first:
            cp.start()
        passed = []
        for j, chip in enumerate(chips):
            for i in range(n):
                copy(i, 1 + j, (*chip, c), me).wait_recv()
                fwd = copy(i, 4 + j, (*chip, c), sibling)
                fwd.start()
                passed.append(fwd)
        for i in range(n):
            copy(i, 0, sibling, me).wait_recv()
        for j, chip in enumerate(chips):
            for i in range(n):
                copy(i, 4 + j, (*chip, 1 - c), me).wait_recv()
        for cp in first + passed:
            cp.wait_send()
        for cp in mine:
            cp.wait()

    any_spec = pl.BlockSpec(memory_space=pl.ANY)
    return pl.pallas_call(
        body, name=name, out_shape=tuple(SDS((N_DEV,) + a.shape, a.dtype) for a in arrays),
        in_specs=[any_spec] * n, out_specs=tuple([any_spec] * n),
        scratch_shapes=[pltpu.SemaphoreType.DMA((n, N_DEV - 1)), pltpu.SemaphoreType.DMA((n, N_DEV - 1)),
                        pltpu.SemaphoreType.DMA((n,))],
    )(*arrays)


_HBM = pl.BlockSpec(memory_space=pltpu.HBM)
_SEM = pl.BlockSpec(memory_space=pltpu.SEMAPHORE)


def _peers(x, y, c):
    out = []
    for dlt in range(1, N_DEV):
        px = 1 - x if dlt & 4 else x
        py = 1 - y if dlt & 2 else y
        pc = 1 - c if dlt & 1 else c
        out.append((dlt, (px, py, pc), 4 * px + 2 * py + pc))
    return out


def _scatter_start(arrays):
    n = len(arrays)
    ns = n * (N_DEV - 1)

    def body(*refs):
        ins, lands = refs[:n], refs[n:2 * n]
        send_sems, recv_sems = refs[2 * n:2 * n + ns], refs[2 * n + ns:2 * n + 2 * ns]
        token = refs[-1]
        x, y, c = lax.axis_index("x"), lax.axis_index("y"), lax.axis_index("c")
        me = 4 * x + 2 * y + c
        for dlt, peer, pi in _peers(x, y, c):
            for i in range(n):
                k = i * (N_DEV - 1) + dlt - 1
                pltpu.make_async_remote_copy(
                    src_ref=ins[i].at[pi], dst_ref=lands[i].at[me], send_sem=send_sems[k], recv_sem=recv_sems[k],
                    device_id=peer, device_id_type=pl.DeviceIdType.MESH).start()
        token[...] = jnp.zeros_like(token)

    sem = pltpu.SemaphoreType.DMA(())
    thru = tuple(pltpu.HBM(a.shape, a.dtype) for a in arrays)
    hbm = lambda a: pltpu.with_memory_space_constraint(a, pltpu.HBM)
    outs = pl.pallas_call(
        body, name="scatter_start", out_shape=(sem,) * (2 * ns) + thru + thru + (SDS((8, LANES), F32),),
        in_specs=[_HBM] * (2 * n),
        out_specs=(_SEM,) * (2 * ns) + (_HBM,) * (2 * n) + (pl.BlockSpec(memory_space=pltpu.VMEM),),
        input_output_aliases={i: 2 * ns + i for i in range(2 * n)},
        compiler_params=pltpu.CompilerParams(has_side_effects=pltpu.SideEffectType.DATAFLOW_SIDE_EFFECTING),
    )(*[hbm(a) for a in arrays], *[hbm(jnp.zeros(a.shape, a.dtype)) for a in arrays])
    return outs[:ns], outs[ns:2 * ns], outs[2 * ns:2 * ns + n], outs[2 * ns + n:2 * ns + 2 * n], outs[-1]


def _scatter_wait(send_sems, recv_sems, srcs, lands, after):
    n = len(srcs)
    ns = n * (N_DEV - 1)

    def body(*refs):
        ins, lands_ = refs[:n], refs[n:2 * n]
        send, recv = refs[2 * n:2 * n + ns], refs[2 * n + ns:2 * n + 2 * ns]
        x, y, c = lax.axis_index("x"), lax.axis_index("y"), lax.axis_index("c")
        for dlt, peer, pi in _peers(x, y, c):
            for i in range(n):
                k = i * (N_DEV - 1) + dlt - 1
                cp = pltpu.make_async_remote_copy(
                    src_ref=ins[i].at[pi], dst_ref=lands_[i].at[pi], send_sem=send[k], recv_sem=recv[k],
                    device_id=peer, device_id_type=pl.DeviceIdType.MESH)
                cp.wait_send()
                cp.wait_recv()

    thru = tuple(pltpu.HBM(a.shape, a.dtype) for a in srcs)
    outs = pl.pallas_call(
        body, name="scatter_wait", out_shape=thru + thru,
        in_specs=[_HBM] * (2 * n) + [_SEM] * (2 * ns) + [pl.BlockSpec(memory_space=pl.ANY)],
        out_specs=(_HBM,) * (2 * n), input_output_aliases={i: i for i in range(2 * n)},
        compiler_params=pltpu.CompilerParams(has_side_effects=pltpu.SideEffectType.DATAFLOW_SIDE_EFFECTING),
    )(*srcs, *lands, *send_sems, *recv_sems, after)
    return outs[n:]


def _adaln_mod(c, w_mod, b_mod):
    def body(c_ref, w_ref, b_ref, mod_ref, sc_ref):
        sc = _silu(c_ref[...])
        sc8 = jnp.broadcast_to(sc, (8, D_MODEL))
        mod_ref[...] = _nn(sc8, w_ref[...])[0:1] + b_ref[...]
        sc_ref[...] = sc

    return pl.pallas_call(body, name="adaln_mod", compiler_params=_params(),
                          out_shape=(SDS((1, 3 * D_MODEL), F32), SDS((1, D_MODEL), F32)))(c, w_mod, b_mod)


def _ln_proj(x, mod, norm_w, ws, cos_t, sin_t, conv_w8, alog_row, dtb_row, ts):
    s = x.shape[0]
    widths = [w.shape[1] for w in ws]
    assert ts == PERM_BLK

    def body(x_ref, mod_ref, nw_ref, cos_ref, sin_ref, cw_ref, al_ref, dtb_ref, wqkv, wz, wba, waq, wak, wav, waz,
             h_ref, oqkv, oz, oba, oq, ok, ov, oaz, q_ref, k_ref, v_ref, bg_ref, halo):
        n = pl.program_id(0)
        xt = x_ref[...]
        r = lax.rsqrt(jnp.mean(xt * xt, axis=-1, keepdims=True) + EPS)
        shift, scale = mod_ref[:, 0:D_MODEL], mod_ref[:, D_MODEL:2 * D_MODEL]
        h = ((xt * r) * nw_ref[...]) * (1.0 + scale) + shift
        hb = _bf(h)
        h_ref[...] = hb
        hp = jnp.dot(_plane_perm(ts, False), hb, preferred_element_type=F32)
        pre = jnp.dot(hb, wqkv[...], preferred_element_type=F32)
        ba = jnp.dot(hb, wba[...], preferred_element_type=F32)
        hp = _bf(hp)
        tq = jnp.dot(hp, waq[...], preferred_element_type=F32)
        tk = jnp.dot(hp, wak[...], preferred_element_type=F32)
        tv = jnp.dot(hp, wav[...], preferred_element_type=F32)
        tz = jnp.dot(hb, wz[...], preferred_element_type=F32)
        taz = jnp.dot(hb, waz[...], preferred_element_type=F32)
        cs, sn = cos_ref[...], sin_ref[...]
        rows = ts // PLANES
        for t, o_ref in ((tq, oq), (tk, ok)):
            for j in range(AT_PAIRS):
                tj = t[:, j * LANES:(j + 1) * LANES]
                rot = tj * cs + _swap_half64(tj) * sn
                for r in range(PLANES):
                    o_ref[j, r] = rot[r * rows:(r + 1) * rows]
        oqkv[...] = pre
        ext = jnp.concatenate([jnp.where(n == 0, 0.0, halo[...]), pre], axis=0)
        halo[...] = pre[ts - 8:ts]
        taps = _conv_taps(ext, ts)
        conv = taps[0] * cw_ref[0:1, :]
        for j in range(1, CONV_K):
            conv = conv + taps[j] * cw_ref[j:j + 1, :]
        for hd in range(DN_HEADS):
            cols = slice(hd * DN_DIM, (hd + 1) * DN_DIM)
            q_ref[:, cols] = _post_q(conv[:, hd * DN_DIM:(hd + 1) * DN_DIM])
            k_ref[:, cols] = _post_k(conv[:, DN_WIDTH + hd * DN_DIM:DN_WIDTH + (hd + 1) * DN_DIM])
            v_ref[:, cols] = _post_v(conv[:, 2 * DN_WIDTH + hd * DN_DIM:2 * DN_WIDTH + (hd + 1) * DN_DIM])
        oba[...] = ba
        bg = _beta_decay(ba, al_ref[...], dtb_ref[...])
        lane = lax.broadcasted_iota(jnp.int32, bg.shape, 1)
        run = pltpu.roll(_chunk_cumsum(bg), DN_HEADS, 1)
        bg_ref[...] = jnp.where((lane >= GC_LANE) & (lane < GC_LANE + DN_HEADS), run, bg)
        for j in range(AT_PAIRS):
            for r in range(PLANES):
                ov[j, r] = tv[r * rows:(r + 1) * rows, j * LANES:(j + 1) * LANES]
        oz[...] = tz
        oaz[...] = taz

    tok = lambda w: pl.BlockSpec((ts, w), lambda i: (i, 0))
    full = lambda a: pl.BlockSpec(a.shape, lambda i: (0, 0))
    pairs = pl.BlockSpec((AT_PAIRS, PLANES, ts // PLANES, LANES), lambda i: (0, 0, i, 0))
    return pl.pallas_call(
        body, name="ln_proj", grid=(s // ts,), compiler_params=_params("arbitrary"),
        in_specs=[tok(D_MODEL), full(mod), full(norm_w), tok(LANES), tok(LANES), full(conv_w8), full(alog_row),
                  full(dtb_row)] + [full(w) for w in ws],
        out_specs=(tok(D_MODEL), tok(widths[0]), tok(widths[1]), tok(widths[2]), pairs, pairs, pairs,
                   tok(widths[6]), tok(DN_WIDTH), tok(DN_WIDTH), tok(DN_WIDTH), tok(BA_PAD)),
        out_shape=(SDS((s, D_MODEL), BF16), SDS((s, widths[0]), F32), SDS((s, widths[1]), F32),
                   SDS((s, widths[2]), F32)) + (SDS((AT_PAIRS, PLANES, s // PLANES, LANES), F32),) * 3 + (SDS((s, widths[6]), F32),)
        + (SDS((s, DN_WIDTH), F32),) * 3 + (SDS((s, BA_PAD), F32),),
        scratch_shapes=[pltpu.VMEM((8, widths[0]), F32)],
    )(x, mod, norm_w, cos_t, sin_t, conv_w8, alog_row, dtb_row, *ws)


def _conv_taps(ext, rows):
    taps = []
    for j in range(CONV_K):
        sh = CONV_K - 1 - j
        rolled = pltpu.roll(ext, sh, 0) if sh else ext
        taps.append(rolled[8:8 + rows])
    return taps


def _dn_forward(q, k, v, bg):
    s = q.shape[0]
    tp = CH_UNROLL * CHUNK
    npass = s // tp
    hs = range(DN_HEADS)
    sl = [slice(h * DN_DIM, (h + 1) * DN_DIM) for h in hs]

    def body(q_ref, k_ref, v_ref, bg_ref, w_ref, qd_ref, kd_ref, p_ref, gl_ref, t_ref, o_ref, vn_ref, st_ref,
             state, u_s, w_s, qd_s, kd_s, p_s, gl_s):
        @pl.when(pl.program_id(0) == 0)
        def _():
            for ref in (state, u_s, w_s, qd_s, kd_s, p_s, gl_s):
                ref[...] = jnp.zeros_like(ref)

        def recurrence():
            for c in range(CH_UNROLL):
                rows = slice(c * CHUNK, (c + 1) * CHUNK)
                rows8 = slice(c * 8, (c + 1) * 8)
                srows = slice(c * DN_DIM, (c + 1) * DN_DIM)
                sf = [state[h] for h in hs]
                sb = [_bf(x) for x in sf]
                ws = [_nn(w_s[rows, cl], b) for cl, b in zip(sl, sb)]
                qs = [_nn(qd_s[rows, cl], b) for cl, b in zip(sl, sb)]
                yield
                vn = [u_s[rows, cl] - x for cl, x in zip(sl, ws)]
                vb = [_bf(x) for x in vn]
                kv = [_tn(kd_s[rows, cl], b) for cl, b in zip(sl, vb)]
                pv = [_nn(p_s[h, rows, :], b) for h, b in zip(hs, vb)]
                for h in hs:
                    state[h] = sf[h] * gl_s[rows8, sl[h]][0:1] + kv[h]
                for h in hs:
                    st_ref[srows, sl[h]] = sf[h]
                    vn_ref[rows, sl[h]] = vn[h]
                    o_ref[rows, sl[h]] = qs[h] + pv[h]
                yield

        steps = recurrence()

        where = [(slice(c * CHUNK, (c + 1) * CHUNK), slice(c * 8, (c + 1) * 8), h, sl[h])
                 for c in range(CH_UNROLL) for h in hs]
        bgs = [bg_ref[rows, :] for rows, _, _, _ in where]
        outs = _chunk_fwd([q_ref[rows, cl] for rows, _, _, cl in where], [k_ref[rows, cl] for rows, _, _, cl in where],
                          [v_ref[rows, cl] for rows, _, _, cl in where],
                          [b[:, h:h + 1] for b, (_, _, h, _) in zip(bgs, where)],
                          [b[:, GC_LANE + h:GC_LANE + h + 1] for b, (_, _, h, _) in zip(bgs, where)],
                          tick=lambda: next(steps, None))
        for _ in steps:
            pass
        for (rows, rows8, h, cl), (u, w, p, qd, kd, gl, t) in zip(where, outs):
            g8 = jnp.broadcast_to(gl, (8, DN_DIM))
            u_s[rows, cl] = u
            w_ref[rows, cl] = w
            w_s[rows, cl] = w
            qd_ref[rows, cl] = qd
            qd_s[rows, cl] = qd
            kd_ref[rows, cl] = kd
            kd_s[rows, cl] = kd
            p_ref[h, rows, :] = p
            p_s[h, rows, :] = p
            gl_ref[rows8, cl] = g8
            gl_s[rows8, cl] = g8
            t_ref[h, rows, :] = t

    cur = lambda i: jnp.minimum(i, npass - 1)
    done = lambda i: jnp.maximum(i - 1, 0)
    tokc = pl.BlockSpec((tp, DN_WIDTH), lambda i: (cur(i), 0))
    tokd = pl.BlockSpec((tp, DN_WIDTH), lambda i: (done(i), 0))
    sq = pl.BlockSpec((DN_HEADS, tp, CHUNK), lambda i: (0, cur(i), 0))
    return pl.pallas_call(
        body, name="dn_forward", grid=(npass + 1,), compiler_params=_params("arbitrary"),
        in_specs=[tokc] * 3 + [pl.BlockSpec((tp, BA_PAD), lambda i: (cur(i), 0))],
        out_specs=(tokc, tokc, tokc, sq, pl.BlockSpec((CH_UNROLL * 8, DN_WIDTH), lambda i: (cur(i), 0)), sq,
                   tokd, tokd, pl.BlockSpec((CH_UNROLL * DN_DIM, DN_WIDTH), lambda i: (done(i), 0))),
        out_shape=(SDS((s, DN_WIDTH), F32),) * 3 + (SDS((DN_HEADS, s, CHUNK), F32),
                                                     SDS((s // CHUNK * 8, DN_WIDTH), F32),
                                                     SDS((DN_HEADS, s, CHUNK), F32),
                                                     SDS((s, DN_WIDTH), F32), SDS((s, DN_WIDTH), F32),
                                                     SDS((s // CHUNK * DN_DIM, DN_WIDTH), F32)),
        scratch_shapes=[pltpu.VMEM((DN_HEADS, DN_DIM, DN_DIM), F32)] + [pltpu.VMEM((tp, DN_WIDTH), F32)] * 4
        + [pltpu.VMEM((DN_HEADS, tp, CHUNK), F32), pltpu.VMEM((CH_UNROLL * 8, DN_WIDTH), F32)],
    )(q, k, v, bg)


LOG2E, LN2 = 1.4426950408889634, 0.6931471805599453
MASKED = -1e30


PLANE_ROWS = ATT_BLK // PLANES


def _to_planes(tile, scr):
    scr[...] = tile
    return [scr[pl.ds(r, tile.shape[0] // PLANES, stride=PLANES), :] for r in range(PLANES)]


def _from_planes(planes, scr):
    n = planes[0].shape[0]
    for r in range(PLANES):
        scr[pl.ds(r, n, stride=PLANES), :] = planes[r]
    return scr[...]


def _plane_perm(n, back):
    row = lax.broadcasted_iota(jnp.int32, (n, n), 0)
    col = lax.broadcasted_iota(jnp.int32, (n, n), 1)
    m, c = (col, row) if back else (row, col)
    return jnp.where(c == PLANES * (m % (n // PLANES)) + m // (n // PLANES), 1.0, 0.0).astype(BF16)


def _geom(d):
    nchunk = PLANES // d
    return nchunk, Q_BLOCK // nchunk


def _pattern_bias(d):
    nchunk, qlen = _geom(d)
    row = lax.broadcasted_iota(jnp.int32, (Q_BLOCK, 2 * Q_BLOCK), 0)
    col = lax.broadcasted_iota(jnp.int32, (Q_BLOCK, 2 * Q_BLOCK), 1)
    uq, aq = row // qlen, row % qlen
    uk, ak = col // (2 * qlen), col % (2 * qlen)
    rel = nchunk * (aq - ak + qlen) + (uq - uk)
    band = jnp.where((rel >= 0) & (rel <= W_SUB), 0.0, MASKED)
    col1 = lax.broadcasted_iota(jnp.int32, (1, 2 * Q_BLOCK), 1)
    return band, (col1 % (2 * qlen)) < qlen


def _aligned(start):
    return start if isinstance(start, int) else pl.multiple_of(start, 8)


def _keys(prev_ref, cur_ref, planes, mm, ql):
    parts = []
    for p in planes:
        if isinstance(mm, int) and mm == 0:
            parts += [prev_ref[0, p, PLANE_ROWS - ql:PLANE_ROWS, :], cur_ref[0, p, 0:ql, :]]
        else:
            parts.append(cur_ref[0, p, pl.ds(_aligned(ql * (mm - 1)), 2 * ql), :])
    return jnp.concatenate(parts, axis=0)


def _gather(ref, lead, planes, start, n):
    parts = [ref[lead + (p, pl.ds(start, n), slice(None))] for p in planes]
    return parts[0] if len(parts) == 1 else jnp.concatenate(parts, axis=0)


def _scatter(ref, lead, planes, start, n, val, add):
    for u, p in enumerate(planes):
        idx = lead + (p, pl.ds(start, n), slice(None))
        if add:
            ref[idx] += val[u * n:(u + 1) * n]
        else:
            ref[idx] = val[u * n:(u + 1) * n]


def _attn_fwd(qr, kr, vv):
    s16 = qr.shape[2]
    nblk = s16 // PLANE_ROWS
    scale = AT_DIM ** -0.5
    npat = len(DILATIONS)

    def body(q_ref, kp_ref, k_ref, vp_ref, v_ref, o_ref, lse_ref, o_p, l_p):
        n = pl.program_id(1)
        lo = lax.broadcasted_iota(jnp.int32, (Q_BLOCK, LANES), 1) < AT_DIM
        nq = ATT_BLK // Q_BLOCK
        heads = [(i, sel) for i in range(nq) for sel in (lo, ~lo)]
        for pi, d in enumerate(DILATIONS):
            band, prev_cols = _pattern_bias(d)
            nchunk, ql = _geom(d)
            cs = [([c % d + d * u for u in range(nchunk)], c // d) for c in range(nq)]
            band0 = band + jnp.where(prev_cols & (n == 0), MASKED, 0.0)
            bias = [band0 if mm == 0 else band for _, mm in cs]
            qb = [_bf(_gather(q_ref, (0,), pls, ql * mm, ql)) for pls, mm in cs]
            kk = [_bf(_keys(kp_ref, k_ref, pls, mm, ql)) for pls, mm in cs]
            vb = [_bf(_keys(vp_ref, v_ref, pls, mm, ql)) for pls, mm in cs]
            sc = [lax.dot_general(jnp.where(sel, qb[i], jnp.zeros_like(qb[i])), kk[i], _NT,
                                  preferred_element_type=F32) for i, sel in heads]
            sc = [x * (scale * LOG2E) + bias[i] for x, (i, _) in zip(sc, heads)]
            mx = [jnp.max(x, axis=-1, keepdims=True) for x in sc]
            pr = [jnp.exp2(x - m) for x, m in zip(sc, mx)]
            ls = [jnp.sum(x, axis=-1, keepdims=True) for x in pr]
            pv = [jnp.dot(_bf(x), vb[i], preferred_element_type=F32) for x, (i, _) in zip(pr, heads)]
            outs = [x / l for x, l in zip(pv, ls)]
            lses = [m * LN2 + jnp.log(l) for m, l in zip(mx, ls)]
            for i, (pls, mm) in enumerate(cs):
                _scatter(o_p, (pi,), pls, ql * mm, ql, jnp.where(lo, outs[2 * i], outs[2 * i + 1]), False)
                _scatter(l_p, (pi,), pls, ql * mm, ql, jnp.where(lo, lses[2 * i], lses[2 * i + 1]), False)

        def merge(r, carry):
            ls = [l_p[pi, r] for pi in range(npat)]
            mx = jnp.maximum(jnp.maximum(ls[0], ls[1]), ls[2])
            es = [jnp.exp(l - mx) for l in ls]
            den = es[0] + es[1] + es[2]
            o_ref[0, r] = (es[0] * o_p[0, r] + es[1] * o_p[1, r] + es[2] * o_p[2, r]) / den
            lse_ref[0, r] = mx + jnp.log(den)
            return carry

        lax.fori_loop(0, PLANES, merge, 0)

    blk = pl.BlockSpec((1, PLANES, PLANE_ROWS, LANES), lambda j, n: (j, 0, n, 0))
    prev = pl.BlockSpec((1, PLANES, PLANE_ROWS, LANES), lambda j, n: (j, 0, jnp.maximum(n - 1, 0), 0))
    return pl.pallas_call(
        body, name="attn_fwd", grid=(AT_PAIRS, nblk), compiler_params=_params("arbitrary", "arbitrary"),
        in_specs=[blk, prev, blk, prev, blk], out_specs=(blk, blk),
        out_shape=(SDS(qr.shape, F32),) * 2,
        scratch_shapes=[pltpu.VMEM((npat, PLANES, PLANE_ROWS, LANES), F32)] * 2,
    )(qr, kr, kr, vv, vv)


def _out_loss(o_dn, z_dn, o_at, z_at, dnw, atw2, x, tgt, w_out, gate, fw, ts):
    s = x.shape[0]

    def body(odn, zdn, oat, zat, dnw_ref, atw_ref, x_ref, t_ref, w_ref, g_ref, fw_ref,
             dx2_ref, gw_ref, dfw_ref, dgate_ref, loss_ref, dodn, dzdn, doat, dzat, delta, ddnw, datw, perm):
        @pl.when(pl.program_id(0) == 0)
        def _():
            for ref in (gw_ref, dfw_ref, dgate_ref, loss_ref, ddnw, datw):
                ref[...] = jnp.zeros_like(ref)

        parts, vjps = [], []
        for h in range(DN_HEADS):
            cols = slice(h * DN_DIM, (h + 1) * DN_DIM)
            y, vjp = jax.vjp(_gate_dn, odn[:, cols], zdn[:, cols], dnw_ref[...])
            parts.append(_bf(y))
            vjps.append(vjp)
        oats = [_from_planes([oat[j, r] for r in range(PLANES)], perm.at[j]) for j in range(AT_PAIRS)]
        for j in range(AT_PAIRS):
            y, vjp = jax.vjp(functools.partial(_gate_at, head_sum=_d_head_sum), oats[j],
                             zat[:, j * LANES:(j + 1) * LANES], atw_ref[...])
            parts.append(_bf(y))
            vjps.append(vjp)
        catb = jnp.concatenate(parts, axis=1)
        wb = w_ref[...]
        gate, fwv = g_ref[...], fw_ref[...]
        mix = jnp.dot(catb, wb, preferred_element_type=F32)
        x2 = x_ref[...] + gate * mix
        r2 = lax.rsqrt(jnp.mean(x2 * x2, axis=-1, keepdims=True) + EPS)
        xn2 = x2 * r2
        err = xn2 * fwv - t_ref[...]
        row = jnp.sum(err * err, axis=-1, keepdims=True) * (1.0 / D_MODEL)
        loss_ref[...] += 0.5 * jnp.sum(row, axis=0, keepdims=True)
        dy = err * (1.0 / D_MODEL)
        dfw_ref[...] += jnp.sum(dy * xn2, axis=0, keepdims=True)
        dxn = dy * fwv
        dx2 = r2 * (dxn - xn2 * jnp.mean(dxn * xn2, axis=-1, keepdims=True))
        dx2_ref[...] = dx2
        dgate_ref[...] += jnp.sum(dx2 * mix, axis=0, keepdims=True)
        dmix = _bf(gate * dx2)
        dcat = lax.dot_general(dmix, wb, _NT, preferred_element_type=F32)
        gw_ref[...] += lax.dot_general(catb, dmix, _TN, preferred_element_type=F32)
        for h in range(DN_HEADS):
            cols = slice(h * DN_DIM, (h + 1) * DN_DIM)
            do, dz, dw = vjps[h](dcat[:, cols])
            dodn[:, cols] = do
            dzdn[:, cols] = _bf(dz)
            ddnw[...] += dw
        for j in range(AT_PAIRS):
            cols = slice(j * LANES, (j + 1) * LANES)
            do, dz, dw = vjps[DN_HEADS + j](dcat[:, DN_WIDTH + j * LANES:DN_WIDTH + (j + 1) * LANES])
            for r, x in enumerate(_to_planes(do, perm.at[j])):
                doat[j, r] = x
            dzat[:, cols] = _bf(dz)
            datw[...] += dw
            for r, x in enumerate(_to_planes(_head_sum(do * oats[j]), perm.at[j])):
                delta[j, r] = x

    tok = lambda w: pl.BlockSpec((ts, w), lambda i: (i, 0))
    full = lambda a: pl.BlockSpec(a.shape, lambda i: (0, 0))
    row = pl.BlockSpec((1, D_MODEL), lambda i: (0, 0))
    lrow = pl.BlockSpec((1, LANES), lambda i: (0, 0))
    pairs = pl.BlockSpec((AT_PAIRS, PLANES, ts // PLANES, LANES), lambda i: (0, 0, i, 0))
    return pl.pallas_call(
        body, name="out_loss", grid=(s // ts,), compiler_params=_params("arbitrary"),
        in_specs=[tok(DN_WIDTH), tok(DN_WIDTH), pairs, tok(AT_WIDTH), full(dnw), full(atw2),
                  tok(D_MODEL), tok(D_MODEL), full(w_out), full(gate), full(fw)],
        out_specs=(tok(D_MODEL), pl.BlockSpec((D_MODEL, D_MODEL), lambda i: (0, 0)), row, row,
                   pl.BlockSpec((1, 1), lambda i: (0, 0)), tok(DN_WIDTH), tok(DN_WIDTH), pairs, tok(AT_WIDTH), pairs,
                   lrow, lrow),
        out_shape=(SDS((s, D_MODEL), F32), SDS((D_MODEL, D_MODEL), F32), SDS((1, D_MODEL), F32),
                   SDS((1, D_MODEL), F32), SDS((1, 1), F32), SDS((s, DN_WIDTH), F32), SDS((s, DN_WIDTH), BF16),
                   SDS((AT_PAIRS, PLANES, s // PLANES, LANES), F32), SDS((s, AT_WIDTH), BF16),
                   SDS((AT_PAIRS, PLANES, s // PLANES, LANES), F32), SDS((1, LANES), F32), SDS((1, LANES), F32)),
        scratch_shapes=[pltpu.VMEM((AT_PAIRS, ts, LANES), F32)],
    )(o_dn, z_dn, o_at, z_at, dnw, atw2, x, tgt, w_out, gate, fw)


def _shift_acc(ext, n):
    @pl.when(n == 0)
    def _():
        ext[:, 0:PLANE_ROWS, :] = jnp.zeros((PLANES, PLANE_ROWS, LANES), F32)

    @pl.when(n > 0)
    def _():
        ext[:, 0:PLANE_ROWS, :] = ext[:, PLANE_ROWS:2 * PLANE_ROWS, :]

    ext[:, PLANE_ROWS:2 * PLANE_ROWS, :] = jnp.zeros((PLANES, PLANE_ROWS, LANES), F32)


def _attn_bwd(qr, kr, vv, do, lse, delta):
    s16 = qr.shape[2]
    nblk = s16 // PLANE_ROWS
    scale = AT_DIM ** -0.5

    nu = ATT_UNROLL_BWD
    npass = ATT_BLK // Q_BLOCK // nu

    def blocks(g, d):
        nchunk, ql = _geom(d)
        out = []
        for u in range(nu):
            r0, mm = (u % d, g * (nu // d) + u // d) if nu % d == 0 else (g * nu + u, 0)
            out.append(([r0 + d * c for c in range(nchunk)], _aligned(ql * mm), _aligned(PLANE_ROWS + ql * mm - ql),
                        ql, mm))
        return out

    def body(q_ref, kp_ref, k_ref, vp_ref, v_ref, do_ref, lse_ref, dl_ref, dq_ref, dk_ref, dv_ref, dkext, dvext):
        n = pl.program_id(1)
        _shift_acc(dkext, n)
        _shift_acc(dvext, n)

        @pl.when(n < nblk)
        def _():
            dq_ref[0] = jnp.zeros((PLANES, PLANE_ROWS, LANES), F32)
            lo = lax.broadcasted_iota(jnp.int32, (Q_BLOCK, LANES), 1) < AT_DIM
            for d in DILATIONS:
                band, prev_cols = _pattern_bias(d)
                band0 = band + jnp.where(prev_cols & (n == 0), MASKED, 0.0)

                def group(g, carry, d=d, band=band, band0=band0):
                    cs = blocks(g, d)
                    heads = [(i, sel) for i in range(nu) for sel in (lo, ~lo)]
                    bias = [band0 if isinstance(mm, int) and mm == 0 else band for _, _, _, _, mm in cs]
                    qb = [_bf(_gather(q_ref, (0,), pls, qs, ql)) for pls, qs, _, ql, _ in cs]
                    dob = [_bf(_gather(do_ref, (0,), pls, qs, ql)) for pls, qs, _, ql, _ in cs]
                    kk = [_bf(_keys(kp_ref, k_ref, pls, mm, ql)) for pls, _, _, ql, mm in cs]
                    vb = [_bf(_keys(vp_ref, v_ref, pls, mm, ql)) for pls, _, _, ql, mm in cs]
                    lse2 = [_gather(lse_ref, (0,), pls, qs, ql) * LOG2E for pls, qs, _, ql, _ in cs]
                    dl2 = [_gather(dl_ref, (0,), pls, qs, ql) for pls, qs, _, ql, _ in cs]
                    qm = [jnp.where(sel, qb[i], jnp.zeros_like(qb[i])) for i, sel in heads]
                    dom = [jnp.where(sel, dob[i], jnp.zeros_like(dob[i])) for i, sel in heads]
                    lse_c = [jnp.max(jnp.where(sel, lse2[i], -jnp.inf), axis=-1, keepdims=True) for i, sel in heads]
                    dl_c = [jnp.max(jnp.where(sel, dl2[i], -jnp.inf), axis=-1, keepdims=True) for i, sel in heads]
                    sc = [lax.dot_general(a, kk[i], _NT, preferred_element_type=F32) for a, (i, _) in zip(qm, heads)]
                    dp = [lax.dot_general(a, vb[i], _NT, preferred_element_type=F32) for a, (i, _) in zip(dom, heads)]
                    pr = [jnp.exp2(x * (scale * LOG2E) + bias[i] - l) for x, l, (i, _) in zip(sc, lse_c, heads)]
                    ds = [_bf(p * (x - dl) * scale) for p, x, dl in zip(pr, dp, dl_c)]
                    prb = [_bf(p) for p in pr]
                    dq = [jnp.dot(x, kk[i], preferred_element_type=F32) for x, (i, _) in zip(ds, heads)]
                    dk = [lax.dot_general(x, a, _TN, preferred_element_type=F32) for x, a in zip(ds, qm)]
                    dv = [lax.dot_general(x, a, _TN, preferred_element_type=F32) for x, a in zip(prb, dom)]
                    for i, (pls, qs, ks, ql, _) in enumerate(cs):
                        _scatter(dq_ref, (0,), pls, qs, ql, jnp.where(lo, dq[2 * i], dq[2 * i + 1]), True)
                        _scatter(dkext, (), pls, ks, 2 * ql, dk[2 * i] + dk[2 * i + 1], True)
                        _scatter(dvext, (), pls, ks, 2 * ql, dv[2 * i] + dv[2 * i + 1], True)
                    return carry

                if nu % d == 0:
                    group(0, 0)
                    lax.fori_loop(1, npass, group, 0)
                else:
                    lax.fori_loop(0, npass, group, 0)

        dk_ref[0] = dkext[:, 0:PLANE_ROWS, :]
        dv_ref[0] = dvext[:, 0:PLANE_ROWS, :]

    at = lambda f: pl.BlockSpec((1, PLANES, PLANE_ROWS, LANES), lambda j, n: (j, 0, f(n), 0))
    cur = at(lambda n: jnp.minimum(n, nblk - 1))
    prev = at(lambda n: jnp.maximum(jnp.minimum(n, nblk - 1) - 1, 0))
    done = at(lambda n: jnp.maximum(n - 1, 0))
    return pl.pallas_call(
        body, name="attn_bwd", grid=(AT_PAIRS, nblk + 1), compiler_params=_params("arbitrary", "arbitrary"),
        in_specs=[cur, prev, cur, prev, cur, cur, cur, cur], out_specs=(cur, done, done),
        out_shape=(SDS(qr.shape, F32),) * 3,
        scratch_shapes=[pltpu.VMEM((PLANES, 2 * PLANE_ROWS, LANES), F32)] * 2,
    )(qr, kr, kr, vv, vv, do, lse, delta)


def _dn_backward(do, st, vn, w, qd, kd, p, gl, q, k, v, bg, t):
    s = do.shape[0]
    nc = CH_UNROLL_BWD
    tp = nc * CHUNK
    npass = s // tp
    hs = range(DN_HEADS)
    sl = [slice(h * DN_DIM, (h + 1) * DN_DIM) for h in hs]

    def body(do_ref, st_ref, vn_ref, w_ref, qd_ref, kd_ref, p_ref, gl_ref, q_ref, k_ref, v_ref, bg_ref, t_ref,
             dq_ref, dk_ref, dv_ref, dbg_ref, dstate, du_s, dw_s, dqd_s, dkd_s, dp_s, dgl_s):
        @pl.when(pl.program_id(0) == 0)
        def _():
            for ref in (dstate, du_s, dw_s, dqd_s, dkd_s, dp_s, dgl_s):
                ref[...] = jnp.zeros_like(ref)

        where = [(slice(c * CHUNK, (c + 1) * CHUNK), slice(c * 8, (c + 1) * 8), h, sl[h])
                 for c in range(nc) for h in hs]
        cots = [(du_s[rows, cl], dw_s[rows, cl], dp_s[h, rows, :], dqd_s[rows, cl], dkd_s[rows, cl],
                 dgl_s[rows8, cl][0:1, 0:1]) for rows, rows8, h, cl in where]

        def recurrence():
            for c in reversed(range(nc)):
                rows = slice(c * CHUNK, (c + 1) * CHUNK)
                rows8 = slice(c * 8, (c + 1) * 8)
                srows = slice(c * DN_DIM, (c + 1) * DN_DIM)
                ds_ = [dstate[h] for h in hs]
                dsb = [_bf(x) for x in ds_]
                dob = [_bf(do_ref[rows, cl]) for cl in sl]
                pdo = [_tn(p_ref[h, rows, :], b) for h, b in zip(hs, dob)]
                qdo = [_tn(qd_ref[rows, cl], b) for cl, b in zip(sl, dob)]
                kds = [_nn(kd_ref[rows, cl], b) for cl, b in zip(sl, dsb)]
                yield
                dvn = [a + b for a, b in zip(kds, pdo)]
                dvb = [_bf(x) for x in dvn]
                wdv = [_tn(w_ref[rows, cl], b) for cl, b in zip(sl, dvb)]
                for h in hs:
                    dstate[h] = ds_[h] * gl_ref[rows8, sl[h]][0:1] + qdo[h] - wdv[h]
                sfs = [st_ref[srows, cl] for cl in sl]
                sbs = [_bf(x) for x in sfs]
                vnb = [_bf(vn_ref[rows, cl]) for cl in sl]
                for h in hs:
                    du_s[rows, sl[h]] = dvn[h]
                    dw_s[rows, sl[h]] = -_nt(dvb[h], sbs[h])
                    dqd_s[rows, sl[h]] = _nt(dob[h], sbs[h])
                    dkd_s[rows, sl[h]] = _nt(vnb[h], dsb[h])
                    dp_s[h, rows, :] = _nt(dob[h], vnb[h])
                    dgl = jnp.sum(jnp.sum(ds_[h] * sfs[h], axis=1, keepdims=True), axis=0, keepdims=True)
                    dgl_s[rows8, sl[h]] = jnp.broadcast_to(dgl, (8, DN_DIM))
                yield

        steps = recurrence()

        bgs = [bg_ref[rows, :] for rows, _, _, _ in where]
        outs = _chunk_bwd([q_ref[rows, cl] for rows, _, _, cl in where], [k_ref[rows, cl] for rows, _, _, cl in where],
                          [v_ref[rows, cl] for rows, _, _, cl in where],
                          [b[:, h:h + 1] for b, (_, _, h, _) in zip(bgs, where)],
                          [b[:, GC_LANE + h:GC_LANE + h + 1] for b, (_, _, h, _) in zip(bgs, where)],
                          [t_ref[h, rows, :] for rows, _, h, _ in where], cots, tick=lambda: next(steps, None))
        for _ in steps:
            pass
        lane = lax.broadcasted_iota(jnp.int32, (CHUNK, BA_PAD), 1)
        for c in range(nc):
            dbg = jnp.zeros((CHUNK, BA_PAD), F32)
            for (rows, _, h, cl), (dq, dk, dv, dbeta, dgc) in list(zip(where, outs))[c * DN_HEADS:(c + 1) * DN_HEADS]:
                dq_ref[rows, cl] = dq
                dk_ref[rows, cl] = dk
                dv_ref[rows, cl] = dv
                dbg = dbg + jnp.where(lane == h, dbeta, 0.0) + jnp.where(lane == GC_LANE + h, dgc, 0.0)
            dbg_ref[where[c * DN_HEADS][0], :] = dbg

    rec = lambda i: jnp.maximum(npass - 1 - i, 0)
    loc = lambda i: jnp.minimum(npass - i, npass - 1)
    tok_r = pl.BlockSpec((tp, DN_WIDTH), lambda i: (rec(i), 0))
    tok_l = pl.BlockSpec((tp, DN_WIDTH), lambda i: (loc(i), 0))
    sq_r = pl.BlockSpec((DN_HEADS, tp, CHUNK), lambda i: (0, rec(i), 0))
    sq_l = pl.BlockSpec((DN_HEADS, tp, CHUNK), lambda i: (0, loc(i), 0))
    ba_l = pl.BlockSpec((tp, BA_PAD), lambda i: (loc(i), 0))
    return pl.pallas_call(
        body, name="dn_backward", grid=(npass + 1,), compiler_params=_params("arbitrary"),
        in_specs=[tok_r, pl.BlockSpec((nc * DN_DIM, DN_WIDTH), lambda i: (rec(i), 0)), tok_r, tok_r, tok_r, tok_r,
                  sq_r, pl.BlockSpec((nc * 8, DN_WIDTH), lambda i: (rec(i), 0)),
                  tok_l, tok_l, tok_l, ba_l, sq_l],
        out_specs=(tok_l, tok_l, tok_l, ba_l),
        out_shape=(SDS((s, DN_WIDTH), F32),) * 3 + (SDS((s, BA_PAD), F32),),
        scratch_shapes=[pltpu.VMEM((DN_HEADS, DN_DIM, DN_DIM), F32)] + [pltpu.VMEM((tp, DN_WIDTH), F32)] * 4
        + [pltpu.VMEM((DN_HEADS, tp, CHUNK), F32), pltpu.VMEM((nc * 8, DN_WIDTH), F32)],
    )(do, st, vn, w, qd, kd, p, gl, q, k, v, bg, t)


def _dn_prep_bwd(qkv_pre, ba, dq, dk, dv, dbg, conv_w8, alog_row, dtb_row, hbf, dz_dn, ts):
    s = qkv_pre.shape[0]
    cw = 3 * DN_WIDTH
    nt = s // ts

    def body(pre_ref, ph_ref, nh_ref, ba_ref, dq_ref, dqh_ref, dk_ref, dkh_ref, dv_ref, dvh_ref, dbg_ref,
             cw_ref, al_ref, dtb_ref, h_ref, dz_ref, dpre_ref, dba_ref, dcw_ref, dal_ref, ddtb_ref,
             gqkv_out, gz_out, gba_out, gqkv_ref, gz_ref, gba_ref):
        n = pl.program_id(0)

        @pl.when(n == 0)
        def _():
            gqkv_ref[...] = jnp.zeros_like(gqkv_ref)
            gz_ref[...] = jnp.zeros_like(gz_ref)
            gba_ref[...] = jnp.zeros_like(gba_ref)
            dcw_ref[...] = jnp.zeros_like(dcw_ref)
            dal_ref[...] = jnp.zeros_like(dal_ref)
            ddtb_ref[...] = jnp.zeros_like(ddtb_ref)

        hb = h_ref[...]
        gz_ref[...] += lax.dot_general(hb, dz_ref[...], _TN, preferred_element_type=F32)
        last = n == nt - 1
        prev = jnp.where(n == 0, 0.0, ph_ref[...])
        ext = jnp.concatenate([prev, pre_ref[...], nh_ref[...]], axis=0)
        taps = _conv_taps(ext, ts + 8)
        conv = taps[0] * cw_ref[0:1, :]
        for j in range(1, CONV_K):
            conv = conv + taps[j] * cw_ref[j:j + 1, :]

        def cot(main, halo, cols):
            return jnp.concatenate([main[:, cols], jnp.where(last, 0.0, halo[:, cols])], axis=0)

        rows = ts + 8
        for grp, (fn, mref, href) in enumerate(((_post_q, dq_ref, dqh_ref), (_post_k, dk_ref, dkh_ref),
                                                (_post_v, dv_ref, dvh_ref))):
            gcols = slice(grp * DN_WIDTH, (grp + 1) * DN_WIDTH)
            pieces = []
            for h in range(DN_HEADS):
                cols = slice(h * DN_DIM, (h + 1) * DN_DIM)
                c0 = grp * DN_WIDTH + h * DN_DIM
                _, vjp = jax.vjp(fn, conv[:, c0:c0 + DN_DIM])
                pieces.append(vjp(cot(mref, href, cols))[0])
            dconv = jnp.concatenate(pieces, axis=1)
            dpre = dconv[:ts] * cw_ref[CONV_K - 1:CONV_K, gcols]
            for j in range(CONV_K - 1):
                sh = CONV_K - 1 - j
                dpre = dpre + pltpu.roll(dconv, rows - sh, 0)[:ts] * cw_ref[j:j + 1, gcols]
            dpre_b = _bf(dpre)
            dpre_ref[:, gcols] = dpre_b
            gqkv_ref[:, gcols] += lax.dot_general(hb, dpre_b, _TN, preferred_element_type=F32)
            for j in range(CONV_K):
                dcw_ref[j:j + 1, gcols] += jnp.sum(dconv[:ts] * taps[j][:ts, gcols], axis=0, keepdims=True)

        dbg = dbg_ref[...]
        lane = lax.broadcasted_iota(jnp.int32, dbg.shape, 1)
        dg = pltpu.roll(_chunk_cumsum(dbg, reverse=True), BA_PAD - DN_HEADS, 1)
        cot_bg = jnp.where(lane < DN_HEADS, dbg, jnp.where(lane < GC_LANE, dg, 0.0))
        _, vjp = jax.vjp(_beta_decay, ba_ref[...], al_ref[...], dtb_ref[...])
        dba, dal, ddtb = vjp(cot_bg)
        dba_b = _bf(dba)
        dba_ref[...] = dba_b
        gba_ref[...] += lax.dot_general(hb, dba_b, _TN, preferred_element_type=F32)
        dal_ref[...] += dal
        ddtb_ref[...] += ddtb

        @pl.when(last)
        def _():
            gqkv_out[...] = _bf(gqkv_ref[...])
            gz_out[...] = _bf(gz_ref[...])
            gba_out[...] = _bf(gba_ref[...])

    tok = lambda w: pl.BlockSpec((ts, w), lambda i: (i, 0))
    full = lambda a: pl.BlockSpec(a.shape, lambda i: (0, 0))
    prevh = lambda w: pl.BlockSpec((8, w), lambda i: (jnp.maximum(i * (ts // 8) - 1, 0), 0))
    nexth = lambda w: pl.BlockSpec((8, w), lambda i: (jnp.minimum((i + 1) * (ts // 8), s // 8 - 1), 0))
    row = pl.BlockSpec((1, LANES), lambda i: (0, 0))
    return pl.pallas_call(
        body, name="dn_prep_bwd", grid=(nt,), compiler_params=_params("arbitrary"),
        in_specs=[tok(cw), prevh(cw), nexth(cw), tok(BA_PAD),
                  tok(DN_WIDTH), nexth(DN_WIDTH), tok(DN_WIDTH), nexth(DN_WIDTH), tok(DN_WIDTH), nexth(DN_WIDTH),
                  tok(BA_PAD), full(conv_w8), full(alog_row), full(dtb_row), tok(D_MODEL), tok(DN_WIDTH)],
        out_specs=(tok(cw), tok(BA_PAD), pl.BlockSpec((8, cw), lambda i: (0, 0)), row, row)
        + tuple(pl.BlockSpec((D_MODEL, w), lambda i: (0, 0)) for w in (cw, DN_WIDTH, BA_PAD)),
        out_shape=(SDS((s, cw), BF16), SDS((s, BA_PAD), BF16), SDS((8, cw), F32), SDS((1, LANES), F32),
                   SDS((1, LANES), F32)) + tuple(SDS((D_MODEL, w), BF16) for w in (cw, DN_WIDTH, BA_PAD)),
        scratch_shapes=[pltpu.VMEM((D_MODEL, w), F32) for w in (cw, DN_WIDTH, BA_PAD)],
    )(qkv_pre, qkv_pre, qkv_pre, ba, dq, dq, dk, dk, dv, dv, dbg, conv_w8, alog_row, dtb_row, hbf, dz_dn)


def _dh_dx(dps, ws, x, mod, norm_w, dx2, ts):
    s = x.shape[0]
    widths = [w.shape[1] for w in ws]
    np_ = len(ws)

    def body(*refs):
        dp_refs, w_refs = refs[:np_], refs[np_:2 * np_]
        x_ref, mod_ref, nw_ref, dx2_ref, gx_ref, dshift, dscale, dnw = refs[2 * np_:]

        @pl.when(pl.program_id(0) == 0)
        def _():
            dshift[...] = jnp.zeros_like(dshift)
            dscale[...] = jnp.zeros_like(dscale)
            dnw[...] = jnp.zeros_like(dnw)

        dh = lax.dot_general(dp_refs[0][...], w_refs[0][...], _NT, preferred_element_type=F32)
        for a, b in zip(dp_refs[1:], w_refs[1:]):
            dh = dh + lax.dot_general(a[...], b[...], _NT, preferred_element_type=F32)
        xt = x_ref[...]
        r = lax.rsqrt(jnp.mean(xt * xt, axis=-1, keepdims=True) + EPS)
        xn = xt * r
        nw = nw_ref[...]
        sc1 = 1.0 + mod_ref[:, D_MODEL:2 * D_MODEL]
        dshift[...] += jnp.sum(dh, axis=0, keepdims=True)
        dscale[...] += jnp.sum(dh * (xn * nw), axis=0, keepdims=True)
        dnw[...] += jnp.sum(dh * sc1 * xn, axis=0, keepdims=True)
        dxn = dh * sc1 * nw
        gx_ref[...] = r * (dxn - xn * jnp.mean(dxn * xn, axis=-1, keepdims=True)) + dx2_ref[...]

    tok = lambda w: pl.BlockSpec((ts, w), lambda i: (i, 0))
    full = lambda a: pl.BlockSpec(a.shape, lambda i: (0, 0))
    row = pl.BlockSpec((1, D_MODEL), lambda i: (0, 0))
    return pl.pallas_call(
        body, name="dh_dx", grid=(s // ts,), compiler_params=_params("arbitrary"),
        in_specs=[tok(w) for w in widths] + [full(w) for w in ws] + [tok(D_MODEL), full(mod), full(norm_w),
                                                                    tok(D_MODEL)],
        out_specs=(tok(D_MODEL), row, row, row),
        out_shape=(SDS((s, D_MODEL), F32),) + (SDS((1, D_MODEL), F32),) * 3,
    )(*dps, *ws, x, mod, norm_w, dx2)


def _grad_w_in_at(h, dq, dk, dv, dz_at, cos_t, sin_t, ts):
    s = h.shape[0]
    assert ts % PERM_BLK == 0

    def body(h_ref, q_ref, k_ref, v_ref, dz_ref, cos_ref, sin_ref, oq, ok, ov, gq_out, gk_out, gv_out, gz_out,
             gq, gk, gv, gz):
        @pl.when(pl.program_id(0) == 0)
        def _():
            for o in (gq, gk, gv, gz):
                o[...] = jnp.zeros_like(o)

        hb = h_ref[...]
        gz[...] += lax.dot_general(hb, dz_ref[...], _TN, preferred_element_type=F32)
        back = _plane_perm(PERM_BLK, True)
        rows = PERM_BLK // PLANES
        halves = [(slice(i * PERM_BLK, (i + 1) * PERM_BLK), slice(i * rows, (i + 1) * rows))
                  for i in range(ts // PERM_BLK)]

        def planes(ref, j, prow):
            return jnp.concatenate([ref[j, r, prow, :] for r in range(PLANES)], axis=0)

        for trow, prow in halves:
            vp = jnp.concatenate([_bf(planes(v_ref, j, prow)) for j in range(AT_PAIRS)], axis=1)
            ov[trow, :] = _bf(jnp.dot(back, vp, preferred_element_type=F32))
        gv[...] += lax.dot_general(hb, ov[...], _TN, preferred_element_type=F32)
        for g_ref, o_ref, acc in ((q_ref, oq, gq), (k_ref, ok, gk)):
            for trow, prow in halves:
                cs, sn = cos_ref[trow, :], sin_ref[trow, :]
                gs = [planes(g_ref, j, prow) for j in range(AT_PAIRS)]
                gp = jnp.concatenate([_bf(g * cs + _swap_half64(g * sn)) for g in gs], axis=1)
                o_ref[trow, :] = _bf(jnp.dot(back, gp, preferred_element_type=F32))
            acc[...] += lax.dot_general(hb, o_ref[...], _TN, preferred_element_type=F32)

        @pl.when(pl.program_id(0) == s // ts - 1)
        def _():
            for o, a in ((gq_out, gq), (gk_out, gk), (gv_out, gv), (gz_out, gz)):
                o[...] = _bf(a[...])

    tok = lambda w: pl.BlockSpec((ts, w), lambda i: (i, 0))
    pairs = pl.BlockSpec((AT_PAIRS, PLANES, ts // PLANES, LANES), lambda i: (0, 0, i, 0))
    acc = pl.BlockSpec((D_MODEL, AT_WIDTH), lambda i: (0, 0))
    return pl.pallas_call(
        body, name="grad_w_in_at", grid=(s // ts,), compiler_params=_params("arbitrary"),
        in_specs=[tok(D_MODEL), pairs, pairs, pairs, tok(AT_WIDTH), tok(LANES), tok(LANES)],
        out_specs=(tok(AT_WIDTH),) * 3 + (acc,) * 4,
        out_shape=(SDS((s, AT_WIDTH), BF16),) * 3 + (SDS((D_MODEL, AT_WIDTH), BF16),) * 4,
        scratch_shapes=[pltpu.VMEM((D_MODEL, AT_WIDTH), F32)] * 4,
    )(h, dq, dk, dv, dz_at, cos_t, sin_t)


def _adamw_math(w, g, m, v):
    m = ADAM_B1 * m + (1.0 - ADAM_B1) * g
    v = ADAM_B2 * v + (1.0 - ADAM_B2) * (g * g)
    m_hat = m / (1.0 - ADAM_B1 ** ADAM_STEP)
    v_hat = v / (1.0 - ADAM_B2 ** ADAM_STEP)
    delta = -ADAM_LR * (m_hat / (jnp.sqrt(v_hat) + ADAM_EPS) + ADAM_WD * w)
    return delta, m, v


def _adamw(w, m, v, g, name, own=None):
    def body(w_ref, m_ref, v_ref, g_ref, *rest):
        g_out, d_out, m_out, v_out = rest[-4:]
        if own is None:
            g = g_ref[...]
        else:
            g = g_ref[0].astype(F32)
            for k in range(1, N_DEV):
                g = g + g_ref[k].astype(F32)
            g = g + rest[0][...].astype(F32)
        g_out[...] = g
        d_out[...], m_out[...], v_out[...] = _adamw_math(w_ref[...], g, m_ref[...], v_ref[...])

    args = (w, m, v, g) if own is None else (w, m, v, g, own)
    return pl.pallas_call(body, name=name, compiler_params=_params(),
                          out_shape=(SDS(w.shape, F32),) * 4)(*args)


def _adamw_w_mod(w, m, v, siluc_all, dmod_mine):
    def body(w_ref, m_ref, v_ref, sc_ref, dm_ref, g_out, d_out, m_out, v_out):
        g = _htn(sc_ref[...], dm_ref[...])
        g_out[...] = g
        d_out[...], m_out[...], v_out[...] = _adamw_math(w_ref[...], g, m_ref[...], v_ref[...])

    return pl.pallas_call(body, name="adamw_w_mod", compiler_params=_params(),
                          out_shape=(SDS(w.shape, F32),) * 4)(w, m, v, siluc_all, dmod_mine)


def _pack_sum(pack_all):
    def body(p_ref, o_ref):
        t = p_ref[0]
        for k in range(1, N_DEV):
            t = t + p_ref[k]
        o_ref[...] = t

    return pl.pallas_call(body, name="pack_sum", out_shape=SDS(pack_all.shape[1:], F32))(pack_all)


def _tile(s, want):
    t = min(want, s)
    assert s % t == 0
    return t


def _local_step(x, c, positions, w_mod_bf, b_mod, norm_w, w_in_bf, conv_w, a_log, dt_bias, dn_norm_w, at_norm_w,
                w_out_bf, final_norm_w, tgt):
    s = x.shape[0]
    o = [0]
    for wdt in IN_SPLITS:
        o.append(o[-1] + wdt)
    w_ba = jnp.pad(w_in_bf[:, o[2]:o[4]], ((0, 0), (0, BA_PAD - 2 * DN_HEADS)))
    ws = [w_in_bf[:, o[0]:o[1]], w_in_bf[:, o[1]:o[2]], w_ba, w_in_bf[:, o[4]:o[5]], w_in_bf[:, o[5]:o[6]],
          w_in_bf[:, o[6]:o[7]], w_in_bf[:, o[7]:o[8]]]
    conv_w8 = jnp.pad(conv_w, ((0, 8 - CONV_K), (0, 0)))
    alog_row = jnp.pad(a_log, ((0, 0), (DN_HEADS, BA_PAD - 2 * DN_HEADS)))
    dtb_row = jnp.pad(dt_bias, ((0, 0), (DN_HEADS, BA_PAD - 2 * DN_HEADS)))
    atw2 = jnp.concatenate([at_norm_w, at_norm_w], axis=1)

    half = AT_DIM // 2
    lane = jnp.arange(LANES)
    inv_freq = ROPE_THETA ** (-(lane % half).astype(F32) / half)
    pos = positions.reshape(s // PERM_BLK, PERM_BLK // PLANES, PLANES).transpose(0, 2, 1).reshape(s)
    ang = pos.astype(F32)[:, None] * inv_freq
    cos_t = jnp.cos(ang)
    sin_t = jnp.sin(ang) * jnp.where((lane // half) % 2 == 0, -1.0, 1.0)

    mod, siluc = _adaln_mod(c, w_mod_bf, b_mod)
    gate = mod[:, 2 * D_MODEL:]
    hbf, qkv_pre, z_dn, ba, qr, kr, vb, z_at, q, k, v, bg = _ln_proj(
        x, mod, norm_w, ws, cos_t, sin_t, conv_w8, alog_row, dtb_row, _tile(s, 256))
    w, qd, kd, p, gl, tinv, o_dn, vn, st = _dn_forward(q, k, v, bg)
    o_at, lse = _attn_fwd(qr, kr, vb)
    (dx2, gw_out, dfw, dgate, loss, do_dn, dz_dn, do_at, dz_at, delta, ddnw, datw) = _out_loss(
        o_dn, z_dn, o_at, z_at, dn_norm_w, atw2, x, tgt, w_out_bf, gate, final_norm_w, _tile(s, 512))

    daq, dak, dav, g_aq, g_ak, g_av, g_az = _grad_w_in_at(hbf, *_attn_bwd(qr, kr, vb, do_at, lse, delta), dz_at,
                                                           cos_t, sin_t, _tile(s, 512))
    dq, dk, dv, dbg = _dn_backward(do_dn, st, vn, w, qd, kd, p, gl, q, k, v, bg, tinv)
    dqkv, dba, dcw, dal, ddtb, g_qkv, g_z, g_ba = _dn_prep_bwd(qkv_pre, ba, dq, dk, dv, dbg, conv_w8, alog_row, dtb_row,
                                                               hbf, dz_dn, _tile(s, 512))
    dps = [dqkv, dz_dn, dba, daq, dak, dav, dz_at]
    gw_in = jnp.concatenate([g_qkv, g_z, g_ba[:, :2 * DN_HEADS], g_aq, g_ak, g_av, g_az], axis=1)
    small = dict(conv=dcw[:CONV_K], dgate=dgate, siluc=siluc, dfw=dfw, alog=dal, dtb=ddtb, dnn=ddnw, atn=datw)

    def input_grad(token):
        gx, dshift, dscale, dnw = _dh_dx(dps, ws, x, mod + token, norm_w, dx2, _tile(s, 512))
        return gx, jnp.concatenate([dshift, dscale, small["dgate"]], axis=1), dnw

    return loss, gw_in, gw_out, small, input_grad


def kernel(x, c, positions, w_mod, b_mod, norm_w, w_in, conv_w, a_log, dt_bias, dn_norm_w, at_norm_w, w_out, final_norm_w, loss_target, m_w_mod, m_b_mod, m_norm_w, m_w_in, m_conv_w, m_a_log, m_dt_bias, m_dn_norm_w, m_at_norm_w, m_w_out, m_final_norm_w, v_w_mod, v_b_mod, v_norm_w, v_w_in, v_conv_w, v_a_log, v_dt_bias, v_dn_norm_w, v_at_norm_w, v_w_out, v_final_norm_w):
    me = 4 * lax.axis_index("x") + 2 * lax.axis_index("y") + lax.axis_index("c")
    s = x.shape[1]

    g_mod, g_in, g_conv, g_out = _all_gather(
        [_bf(w_mod[0]), _bf(w_in[0]), conv_w[0], _bf(w_out[0])], "gather_weights")
    w_mod_bf = g_mod.transpose(1, 0, 2).reshape(D_MODEL, 3 * D_MODEL)
    w_in_bf = g_in.transpose(1, 0, 2).reshape(D_MODEL, IN_COLS)
    conv_full = g_conv.transpose(1, 0, 2).reshape(CONV_K, 3 * DN_WIDTH)
    w_out_bf = g_out.reshape(D_MODEL, D_MODEL)

    loss, gw_in, gw_out, small, input_grad = _local_step(
        x[0], c, positions[0], w_mod_bf, b_mod, norm_w, w_in_bf, conv_full, a_log, dt_bias, dn_norm_w, at_norm_w,
        w_out_bf, final_norm_w.reshape(1, D_MODEL), loss_target[0])

    gw_in_slabs = gw_in.reshape(D_MODEL, N_DEV, IN_SHARD).transpose(1, 0, 2)
    gw_out_slabs = _bf(gw_out).reshape(N_DEV, D_MODEL // N_DEV, D_MODEL)
    send_sems, recv_sems, srcs, lands, token = _scatter_start([gw_in_slabs, gw_out_slabs])
    gx, dmod, dnw = input_grad(token[0, 0])
    r_in, r_out = _scatter_wait(send_sems, recv_sems, srcs, lands, gx)
    own_in = lax.dynamic_index_in_dim(gw_in_slabs, me, 0, keepdims=False)
    own_out = lax.dynamic_index_in_dim(gw_out_slabs, me, 0, keepdims=False)

    pack = jnp.concatenate([small["conv"].reshape(1, -1), dmod, small["siluc"], dnw, small["dfw"],
                            small["alog"], small["dtb"], small["dnn"], small["atn"],
                            jnp.pad(loss, ((0, 0), (0, LANES - 1)))], axis=1).reshape(PK_ROWS, LANES)
    (pack_all,) = _exchange([pack], [False], "exchange_small")

    res = {}
    res["w_in"] = _adamw(w_in[0], m_w_in[0], v_w_in[0], r_in, "adamw_w_in", own=own_in)
    res["w_out"] = _adamw(w_out[0], m_w_out[0], v_w_out[0], r_out, "adamw_w_out", own=own_out)
    flat_all = pack_all.reshape(N_DEV, PK_END)
    dmod_mine = lax.dynamic_slice(flat_all, (0, PK_DMOD + me * (3 * D_MODEL // N_DEV)), (N_DEV, 3 * D_MODEL // N_DEV))
    res["w_mod"] = _adamw_w_mod(w_mod[0], m_w_mod[0], v_w_mod[0], flat_all[:, PK_SILUC:PK_DNW], dmod_mine)
    tot = _pack_sum(pack_all).reshape(1, PK_END)
    g_conv_full = tot[:, PK_CONV:PK_DMOD].reshape(CONV_K, 3 * DN_WIDTH)
    g_conv_mine = lax.dynamic_slice(g_conv_full, (0, me * (3 * DN_WIDTH // N_DEV)), (CONV_K, 3 * DN_WIDTH // N_DEV))
    res["conv_w"] = _adamw(conv_w[0], m_conv_w[0], v_conv_w[0], g_conv_mine, "adamw_conv_w")
    res["b_mod"] = _adamw(b_mod, m_b_mod, v_b_mod, tot[:, PK_DMOD:PK_SILUC], "adamw_b_mod")
    res["norm_w"] = _adamw(norm_w, m_norm_w, v_norm_w, tot[:, PK_DNW:PK_DFW], "adamw_norm_w")
    res["a_log"] = _adamw(a_log, m_a_log, v_a_log, tot[:, PK_ALOG + DN_HEADS:PK_ALOG + 2 * DN_HEADS], "adamw_a_log")
    res["dt_bias"] = _adamw(dt_bias, m_dt_bias, v_dt_bias, tot[:, PK_DTB + DN_HEADS:PK_DTB + 2 * DN_HEADS],
                            "adamw_dt_bias")
    res["dn_norm_w"] = _adamw(dn_norm_w, m_dn_norm_w, v_dn_norm_w, tot[:, PK_DNN:PK_ATN], "adamw_dn_norm_w")
    g_atn = tot[:, PK_ATN:PK_ATN + AT_DIM] + tot[:, PK_ATN + AT_DIM:PK_LOSS]
    res["at_norm_w"] = _adamw(at_norm_w, m_at_norm_w, v_at_norm_w, g_atn, "adamw_at_norm_w")
    fin = _adamw(final_norm_w.reshape(1, D_MODEL), m_final_norm_w.reshape(1, D_MODEL),
                 v_final_norm_w.reshape(1, D_MODEL), tot[:, PK_DFW:PK_ALOG], "adamw_final_norm_w")
    res["final_norm_w"] = tuple(a.reshape(D_MODEL) for a in fin)

    lead = ("w_mod", "w_in", "conv_w", "w_out")
    names = ("w_mod", "b_mod", "norm_w", "w_in", "conv_w", "a_log", "dt_bias", "dn_norm_w", "at_norm_w", "w_out",
             "final_norm_w")
    out = [tot[0, PK_LOSS], gx.reshape(1, s, D_MODEL)]
    for kind in range(4):
        for nm in names:
            a = res[nm][kind]
            out.append(a[None] if nm in lead else a)
    return tuple(out)
```

```python
import functools

import jax
import jax.numpy as jnp
from jax import lax
from jax.experimental import pallas as pl
from jax.experimental.pallas import tpu as pltpu

F32, BF16 = jnp.float32, jnp.bfloat16
HI = lax.Precision.HIGHEST
SDS = jax.ShapeDtypeStruct

D_MODEL = 1024
DN_HEADS, DN_DIM, DN_WIDTH = 4, 128, 512
AT_HEADS, AT_DIM, AT_WIDTH = 8, 64, 512
CONV_K = 4
CHUNK = 64
Q_BLOCK = 128
W_SUB = 128
DILATIONS = (1, 4, 16)
AT_PAIRS = 4
PLANES = 16
PERM_BLK = 256
ATT_BLK = Q_BLOCK * max(DILATIONS)
ATT_UNROLL_BWD = 4
CH_UNROLL, CH_UNROLL_BWD = 4, 8
ROPE_THETA = 10000.0
EPS = 1e-6
N_DEV = 8
LANES = 128
BA_PAD = 128
IN_SPLITS = (1536, 512, 4, 4, 512, 512, 512, 512)
IN_COLS = sum(IN_SPLITS)
IN_SHARD = IN_COLS // N_DEV
VMEM_LIMIT = 58 * 2 ** 20

ADAM_LR, ADAM_B1, ADAM_B2, ADAM_EPS, ADAM_WD, ADAM_STEP = 0.001, 0.9, 0.999, 1e-08, 0.01, 10

PK_CONV, PK_DMOD, PK_SILUC, PK_DNW, PK_DFW, PK_ALOG, PK_DTB, PK_DNN, PK_ATN, PK_LOSS, PK_END = (
    0, 6144, 9216, 10240, 11264, 12288, 12416, 12544, 12672, 12800, 12928)
PK_ROWS = PK_END // LANES

_NT = (((1,), (1,)), ((), ()))
_TN = (((0,), (0,)), ((), ()))


def _params(*sem):
    return pltpu.CompilerParams(dimension_semantics=sem or None, vmem_limit_bytes=VMEM_LIMIT)


def _bf(x):
    return x.astype(BF16)


def _nn(a, b):
    return jnp.dot(_bf(a), _bf(b), preferred_element_type=F32)


def _nt(a, b):
    return lax.dot_general(_bf(a), _bf(b), _NT, preferred_element_type=F32)


def _tn(a, b):
    return lax.dot_general(_bf(a), _bf(b), _TN, preferred_element_type=F32)


def _htn(a, b):
    return lax.dot_general(a, b, _TN, precision=HI, preferred_element_type=F32)


def _head_sum(x):
    r = lax.broadcasted_iota(jnp.int32, (LANES, LANES), 0)
    c = lax.broadcasted_iota(jnp.int32, (LANES, LANES), 1)
    same = jnp.where((r // AT_DIM) == (c // AT_DIM), 1.0, 0.0).astype(BF16)
    hi, lo = _hl(x)
    return jnp.dot(hi, same, preferred_element_type=F32) + jnp.dot(lo, same, preferred_element_type=F32)


@jax.custom_vjp
def _d_head_sum(x):
    return _head_sum(x)


_d_head_sum.defvjp(lambda x: (_head_sum(x), None), lambda _, g: (_head_sum(g),))


def _silu(x):
    return x * jax.nn.sigmoid(x)


def _softplus(x):
    return jnp.maximum(x, 0.0) + jnp.log(1.0 + jnp.exp(-jnp.abs(x)))


def _l2n(x):
    return x * lax.rsqrt(jnp.sum(x * x, axis=-1, keepdims=True) + EPS)


def _post_q(x):
    return _l2n(_silu(x)) * (DN_DIM ** -0.5)


def _post_k(x):
    return _l2n(_silu(x))


def _post_v(x):
    return _silu(x)


def _beta_decay(ba, alog_row, dtb_row):
    lane = lax.broadcasted_iota(jnp.int32, ba.shape, 1)
    return jnp.where(lane < DN_HEADS, jax.nn.sigmoid(ba), -jnp.exp(alog_row) * _softplus(ba + dtb_row))


def _gate_dn(o, z, w):
    return (o * lax.rsqrt(jnp.mean(o * o, axis=-1, keepdims=True) + EPS)) * w * _silu(z)


def _gate_at(o, z, w2, head_sum):
    ms = head_sum(o * o) * (1.0 / AT_DIM)
    return (o * lax.rsqrt(ms + EPS)) * w2 * _silu(z)


def _swap_half64(x):
    lane = lax.broadcasted_iota(jnp.int32, x.shape, 1)
    return jnp.where((lane & (AT_DIM - 1)) < AT_DIM // 2, pltpu.roll(x, LANES - AT_DIM // 2, 1),
                     pltpu.roll(x, AT_DIM // 2, 1))


_NN = (((1,), (0,)), ((), ()))


def _hl(a):
    hi = a.astype(BF16)
    return hi, (a - hi.astype(F32)).astype(BF16)


def _mm3(a, b, dims=_NN):
    (ah, al), (bh, bl) = a, b
    f = lambda x, y: lax.dot_general(x, y, dims, preferred_element_type=F32)
    return f(ah, bh) + (f(ah, bl) + f(al, bh))


def _chunk_masks():
    r = lax.broadcasted_iota(jnp.int32, (CHUNK, CHUNK), 0)
    c = lax.broadcasted_iota(jnp.int32, (CHUNK, CHUNK), 1)
    return r >= c, r > c, (r == c).astype(F32), (r // 16) == (c // 16)


def _tri_inv(mats, tick=lambda: None):
    _, _, eye, blk = _chunk_masks()
    dg = [jnp.where(blk, a, 0.0) for a in mats]
    lo = [jnp.where(blk, 0.0, a) for a in mats]
    sdg = [_hl(x) for x in dg]
    d2 = [_mm3(s, s) for s in sdg]
    tick()
    sd2 = [_hl(x) for x in d2]
    d4 = [_mm3(s, s) for s in sd2]
    tick()
    sd4 = [_hl(x) for x in d4]
    d8 = [_mm3(s, s) for s in sd4]
    tick()
    p1 = [_mm3(_hl(eye - a), _hl(eye + b)) for a, b in zip(dg, d2)]
    tick()
    p2 = [_mm3(_hl(a), _hl(eye + b)) for a, b in zip(p1, d4)]
    tick()
    dinv = [_mm3(_hl(a), _hl(eye + b)) for a, b in zip(p2, d8)]
    tick()
    sdinv = [_hl(x) for x in dinv]
    n1 = [_mm3(s, _hl(b)) for s, b in zip(sdinv, lo)]
    tick()
    sn1 = [_hl(x) for x in n1]
    n2 = [_mm3(s, s) for s in sn1]
    tick()
    q1 = [_mm3(_hl(eye - a), _hl(eye + b)) for a, b in zip(n1, n2)]
    return [_mm3(_hl(a), s) for a, s in zip(q1, sdinv)]


def _chunk_common(qs, ks, vs, betas, gcs):
    tril, _, _, _ = _chunk_masks()
    out = []
    for q, k, v, beta, gc in zip(qs, ks, vs, betas, gcs):
        gb = jnp.broadcast_to(gc, (CHUNK, DN_DIM))
        gt = gb.T[:CHUNK, :]
        gam = jnp.where(tril, jnp.exp(jnp.where(tril, gb[:, :CHUNK] - gt, 0.0)), 0.0)
        last = gb[CHUNK - 1:CHUNK, :]
        eg, e2 = jnp.exp(gb), jnp.exp(last - gb)
        kb, vb = k * beta, v * beta
        out.append(dict(gam=gam, eg=eg, e2=e2, gl=jnp.exp(last[:, 0:1]), kb=kb, vb=vb, kbg=kb * eg,
                        m=_nt(kb, k), qk=_nt(q, k)))
    return out


def _chunk_fwd(qs, ks, vs, betas, gcs, tick=lambda: None):
    tril, strict, _, _ = _chunk_masks()
    cm = _chunk_common(qs, ks, vs, betas, gcs)
    ts = _tri_inv([jnp.where(strict, c["m"] * c["gam"], 0.0) for c in cm], tick)
    outs = []
    for q, k, c, t in zip(qs, ks, cm, ts):
        uw = _nn(t, jnp.concatenate([c["vb"], c["kbg"]], axis=1))
        p = jnp.where(tril, c["qk"] * c["gam"], 0.0)
        outs.append((uw[:, :DN_DIM], uw[:, DN_DIM:], p, q * c["eg"], k * c["e2"], c["gl"], t.T))
    return outs


def _chunk_bwd(qs, ks, vs, betas, gcs, ts, cots, tick=lambda: None):
    tril, strict, _, _ = _chunk_masks()
    cm = _chunk_common(qs, ks, vs, betas, gcs)
    tick()
    row = lax.broadcasted_iota(jnp.int32, (CHUNK, 1), 0)
    ones = jnp.ones((CHUNK, DN_DIM), BF16)
    rs = lambda x: jnp.sum(x, axis=-1, keepdims=True)
    tts = [_bf(t) for t in ts]
    duw = [_bf(jnp.concatenate([ct[0], ct[1]], axis=1)) for ct in cots]
    dts = [_nt(a, jnp.concatenate([c["vb"], c["kbg"]], axis=1)) for a, c in zip(duw, cm)]
    tick()
    xs = [_nn(t, d) for t, d in zip(tts, dts)]
    tick()
    das = [jnp.where(strict, -_nn(x, t), 0.0) for x, t in zip(xs, tts)]
    dvks = [_nn(t, a) for t, a in zip(tts, duw)]
    tick()
    outs = []
    every = max(1, len(qs) // 5)
    for idx, (q, k, v, beta, c, ct, da, dvk) in enumerate(zip(qs, ks, vs, betas, cm, cots, das, dvks)):
        if idx and idx % every == 0:
            tick()
        _, _, dp, dqd, dkd, dgl = ct
        dvb, dkbg = dvk[:, :DN_DIM], dvk[:, DN_DIM:]
        dm = da * c["gam"]
        dqk = jnp.where(tril, dp, 0.0) * c["gam"]
        e = dm * c["m"] + dqk * c["qk"]
        dmq = jnp.concatenate([dm, dqk], axis=0)
        r1 = _nn(dmq, k)
        dkb = r1[:CHUNK] + dkbg * c["eg"]
        dq = r1[CHUNK:] + dqd * c["eg"]
        dk = _tn(dmq, jnp.concatenate([c["kb"], q], axis=0)) + dkd * c["e2"] + dkb * beta
        dbeta = rs(dkb * k + dvb * v)
        eh, el = _hl(e)
        colsum = (lax.dot_general(eh, ones, _TN, preferred_element_type=F32)
                  + lax.dot_general(el, ones, _TN, preferred_element_type=F32))[:, 0:1]
        pkd = dkd * (k * c["e2"])
        dgc = rs(e) - colsum + rs(dqd * q * c["eg"] + dkbg * c["kbg"] - pkd)
        tail = rs(jnp.sum(pkd, axis=0, keepdims=True)) + dgl * c["gl"]
        dgc = dgc + jnp.where(row == CHUNK - 1, tail, 0.0)
        outs.append((dq, dk, dvb * beta, dbeta, dgc))
    return outs


def _chunk_cumsum(x, reverse=False):
    n = x.shape[0]
    pos = lax.broadcasted_iota(jnp.int32, x.shape, 0) & (CHUNK - 1)
    sh = 1
    while sh < CHUNK:
        if reverse:
            x = x + jnp.where(pos < CHUNK - sh, pltpu.roll(x, n - sh, 0), 0.0)
        else:
            x = x + jnp.where(pos >= sh, pltpu.roll(x, sh, 0), 0.0)
        sh *= 2
    return x


GC_LANE = 2 * DN_HEADS


def _exchange(arrays, scatter, name):
    n = len(arrays)
    out_shapes = []
    for a, sc in zip(arrays, scatter):
        out_shapes.append(SDS(a.shape if sc else (N_DEV,) + a.shape, a.dtype))

    def body(*refs):
        ins, outs = refs[:n], refs[n:2 * n]
        send_sems, recv_sems, loc_sems = refs[2 * n:]
        x, y, c = lax.axis_index("x"), lax.axis_index("y"), lax.axis_index("c")
        me = 4 * x + 2 * y + c
        local, remote = [], []
        for i in range(n):
            src = ins[i].at[me] if scatter[i] else ins[i]
            cp = pltpu.make_async_copy(src, outs[i].at[me], loc_sems.at[i])
            cp.start()
            local.append(cp)
        for dlt in range(1, N_DEV):
            px = 1 - x if dlt & 4 else x
            py = 1 - y if dlt & 2 else y
            pc = 1 - c if dlt & 1 else c
            peer = 4 * px + 2 * py + pc
            for i in range(n):
                src = ins[i].at[peer] if scatter[i] else ins[i]
                cp = pltpu.make_async_remote_copy(
                    src_ref=src, dst_ref=outs[i].at[me],
                    send_sem=send_sems.at[i, dlt - 1], recv_sem=recv_sems.at[i, dlt - 1],
                    device_id=(px, py, pc), device_id_type=pl.DeviceIdType.MESH)
                cp.start()
                arrive = pltpu.make_async_remote_copy(
                    src_ref=src, dst_ref=outs[i].at[peer],
                    send_sem=send_sems.at[i, dlt - 1], recv_sem=recv_sems.at[i, dlt - 1],
                    device_id=(px, py, pc), device_id_type=pl.DeviceIdType.MESH)
                remote.append((cp, arrive))
        for cp, arrive in remote:
            cp.wait_send()
            arrive.wait_recv()
        for cp in local:
            cp.wait()

    any_spec = pl.BlockSpec(memory_space=pl.ANY)
    return pl.pallas_call(
        body, name=name, out_shape=tuple(out_shapes),
        in_specs=[any_spec] * n, out_specs=tuple([any_spec] * n),
        scratch_shapes=[pltpu.SemaphoreType.DMA((n, N_DEV - 1)), pltpu.SemaphoreType.DMA((n, N_DEV - 1)),
                        pltpu.SemaphoreType.DMA((n,))],
    )(*arrays)


def _all_gather(arrays, name):
    n = len(arrays)

    def body(*refs):
        ins, outs = refs[:n], refs[n:2 * n]
        send_sems, recv_sems, loc_sems = refs[2 * n:]
        x, y, c = lax.axis_index("x"), lax.axis_index("y"), lax.axis_index("c")
        me, sibling = (x, y, c), (x, y, 1 - c)
        chips = [(1 - x, y), (x, 1 - y), (1 - x, 1 - y)]

        def copy(i, k, block, to, src=None):
            slot = outs[i].at[4 * block[0] + 2 * block[1] + block[2]]
            return pltpu.make_async_remote_copy(
                src_ref=slot if src is None else src, dst_ref=slot,
                send_sem=send_sems.at[i, k], recv_sem=recv_sems.at[i, k],
                device_id=to, device_id_type=pl.DeviceIdType.MESH)

        mine = [pltpu.make_async_copy(ins[i], outs[i].at[4 * x + 2 * y + c], loc_sems.at[i]) for i in range(n)]
        for cp in mine:
            cp.start()
        first = []
        for i in range(n):
            first.append(copy(i, 0, me, sibling, src=ins[i]))
            first += [copy(i, 1 + j, me, (*chip, c), src=ins[i]) for j, chip in enumerate(chips)]
        for cp in first:
            cp.start()
        passed = []
        for j, chip in enumerate(chips):
            for i in range(n):
                copy(i, 1 + j, (*chip, c), me).wait_recv()
                fwd = copy(i, 4 + j, (*chip, c), sibling)
                fwd.start()
                passed.append(fwd)
        for i in range(n):
            copy(i, 0, sibling, me).wait_recv()
        for j, chip in enumerate(chips):
            for i in range(n):
                copy(i, 4 + j, (*chip, 1 - c), me).wait_recv()
        for cp in first + passed:
            cp.wait_send()
        for cp in mine:
            cp.wait()

    any_spec = pl.BlockSpec(memory_space=pl.ANY)
    return pl.pallas_call(
        body, name=name, out_shape=tuple(SDS((N_DEV,) + a.shape, a.dtype) for a in arrays),
        in_specs=[any_spec] * n, out_specs=tuple([any_spec] * n),
        scratch_shapes=[pltpu.SemaphoreType.DMA((n, N_DEV - 1)), pltpu.SemaphoreType.DMA((n, N_DEV - 1)),
                        pltpu.SemaphoreType.DMA((n,))],
    )(*arrays)


_HBM = pl.BlockSpec(memory_space=pltpu.HBM)
_SEM = pl.BlockSpec(memory_space=pltpu.SEMAPHORE)


def _peers(x, y, c):
    out = []
    for dlt in range(1, N_DEV):
        px = 1 - x if dlt & 4 else x
        py = 1 - y if dlt & 2 else y
        pc = 1 - c if dlt & 1 else c
        out.append((dlt, (px, py, pc), 4 * px + 2 * py + pc))
    return out


def _scatter_start(arrays):
    n = len(arrays)
    ns = n * (N_DEV - 1)

    def body(*refs):
        ins, lands = refs[:n], refs[n:2 * n]
        send_sems, recv_sems = refs[2 * n:2 * n + ns], refs[2 * n + ns:2 * n + 2 * ns]
        token = refs[-1]
        x, y, c = lax.axis_index("x"), lax.axis_index("y"), lax.axis_index("c")
        me = 4 * x + 2 * y + c
        for dlt, peer, pi in _peers(x, y, c):
            for i in range(n):
                k = i * (N_DEV - 1) + dlt - 1
                pltpu.make_async_remote_copy(
                    src_ref=ins[i].at[pi], dst_ref=lands[i].at[me], send_sem=send_sems[k], recv_sem=recv_sems[k],
                    device_id=peer, device_id_type=pl.DeviceIdType.MESH).start()
        token[...] = jnp.zeros_like(token)

    sem = pltpu.SemaphoreType.DMA(())
    thru = tuple(pltpu.HBM(a.shape, a.dtype) for a in arrays)
    hbm = lambda a: pltpu.with_memory_space_constraint(a, pltpu.HBM)
    outs = pl.pallas_call(
        body, name="scatter_start", out_shape=(sem,) * (2 * ns) + thru + thru + (SDS((8, LANES), F32),),
        in_specs=[_HBM] * (2 * n),
        out_specs=(_SEM,) * (2 * ns) + (_HBM,) * (2 * n) + (pl.BlockSpec(memory_space=pltpu.VMEM),),
        input_output_aliases={i: 2 * ns + i for i in range(2 * n)},
        compiler_params=pltpu.CompilerParams(has_side_effects=pltpu.SideEffectType.DATAFLOW_SIDE_EFFECTING),
    )(*[hbm(a) for a in arrays], *[hbm(jnp.zeros(a.shape, a.dtype)) for a in arrays])
    return outs[:ns], outs[ns:2 * ns], outs[2 * ns:2 * ns + n], outs[2 * ns + n:2 * ns + 2 * n], outs[-1]


def _scatter_wait(send_sems, recv_sems, srcs, lands, after):
    n = len(srcs)
    ns = n * (N_DEV - 1)

    def body(*refs):
        ins, lands_ = refs[:n], refs[n:2 * n]
        send, recv = refs[2 * n:2 * n + ns], refs[2 * n + ns:2 * n + 2 * ns]
        x, y, c = lax.axis_index("x"), lax.axis_index("y"), lax.axis_index("c")
        for dlt, peer, pi in _peers(x, y, c):
            for i in range(n):
                k = i * (N_DEV - 1) + dlt - 1
                cp = pltpu.make_async_remote_copy(
                    src_ref=ins[i].at[pi], dst_ref=lands_[i].at[pi], send_sem=send[k], recv_sem=recv[k],
                    device_id=peer, device_id_type=pl.DeviceIdType.MESH)
                cp.wait_send()
                cp.wait_recv()

    thru = tuple(pltpu.HBM(a.shape, a.dtype) for a in srcs)
    outs = pl.pallas_call(
        body, name="scatter_wait", out_shape=thru + thru,
        in_specs=[_HBM] * (2 * n) + [_SEM] * (2 * ns) + [pl.BlockSpec(memory_space=pl.ANY)],
        out_specs=(_HBM,) * (2 * n), input_output_aliases={i: i for i in range(2 * n)},
        compiler_params=pltpu.CompilerParams(has_side_effects=pltpu.SideEffectType.DATAFLOW_SIDE_EFFECTING),
    )(*srcs, *lands, *send_sems, *recv_sems, after)
    return outs[n:]


def _adaln_mod(c, w_mod, b_mod):
    def body(c_ref, w_ref, b_ref, mod_ref, sc_ref):
        sc = _silu(c_ref[...])
        sc8 = jnp.broadcast_to(sc, (8, D_MODEL))
        mod_ref[...] = _nn(sc8, w_ref[...])[0:1] + b_ref[...]
        sc_ref[...] = sc

    return pl.pallas_call(body, name="adaln_mod", compiler_params=_params(),
                          out_shape=(SDS((1, 3 * D_MODEL), F32), SDS((1, D_MODEL), F32)))(c, w_mod, b_mod)


def _ln_proj(x, mod, norm_w, ws, cos_t, sin_t, conv_w8, alog_row, dtb_row, ts):
    s = x.shape[0]
    widths = [w.shape[1] for w in ws]
    assert ts == PERM_BLK

    def body(x_ref, mod_ref, nw_ref, cos_ref, sin_ref, cw_ref, al_ref, dtb_ref, wqkv, wz, wba, waq, wak, wav, waz,
             h_ref, oqkv, oz, oba, oq, ok, ov, oaz, q_ref, k_ref, v_ref, bg_ref, halo):
        n = pl.program_id(0)
        xt = x_ref[...]
        r = lax.rsqrt(jnp.mean(xt * xt, axis=-1, keepdims=True) + EPS)
        shift, scale = mod_ref[:, 0:D_MODEL], mod_ref[:, D_MODEL:2 * D_MODEL]
        h = ((xt * r) * nw_ref[...]) * (1.0 + scale) + shift
        hb = _bf(h)
        h_ref[...] = hb
        hp = jnp.dot(_plane_perm(ts, False), hb, preferred_element_type=F32)
        pre = jnp.dot(hb, wqkv[...], preferred_element_type=F32)
        ba = jnp.dot(hb, wba[...], preferred_element_type=F32)
        hp = _bf(hp)
        tq = jnp.dot(hp, waq[...], preferred_element_type=F32)
        tk = jnp.dot(hp, wak[...], preferred_element_type=F32)
        tv = jnp.dot(hp, wav[...], preferred_element_type=F32)
        tz = jnp.dot(hb, wz[...], preferred_element_type=F32)
        taz = jnp.dot(hb, waz[...], preferred_element_type=F32)
        cs, sn = cos_ref[...], sin_ref[...]
        rows = ts // PLANES
        for t, o_ref in ((tq, oq), (tk, ok)):
            for j in range(AT_PAIRS):
                tj = t[:, j * LANES:(j + 1) * LANES]
                rot = tj * cs + _swap_half64(tj) * sn
                for r in range(PLANES):
                    o_ref[j, r] = rot[r * rows:(r + 1) * rows]
        oqkv[...] = pre
        ext = jnp.concatenate([jnp.where(n == 0, 0.0, halo[...]), pre], axis=0)
        halo[...] = pre[ts - 8:ts]
        taps = _conv_taps(ext, ts)
        conv = taps[0] * cw_ref[0:1, :]
        for j in range(1, CONV_K):
            conv = conv + taps[j] * cw_ref[j:j + 1, :]
        for hd in range(DN_HEADS):
            cols = slice(hd * DN_DIM, (hd + 1) * DN_DIM)
            q_ref[:, cols] = _post_q(conv[:, hd * DN_DIM:(hd + 1) * DN_DIM])
            k_ref[:, cols] = _post_k(conv[:, DN_WIDTH + hd * DN_DIM:DN_WIDTH + (hd + 1) * DN_DIM])
            v_ref[:, cols] = _post_v(conv[:, 2 * DN_WIDTH + hd * DN_DIM:2 * DN_WIDTH + (hd + 1) * DN_DIM])
        oba[...] = ba
        bg = _beta_decay(ba, al_ref[...], dtb_ref[...])
        lane = lax.broadcasted_iota(jnp.int32, bg.shape, 1)
        run = pltpu.roll(_chunk_cumsum(bg), DN_HEADS, 1)
        bg_ref[...] = jnp.where((lane >= GC_LANE) & (lane < GC_LANE + DN_HEADS), run, bg)
        for j in range(AT_PAIRS):
            for r in range(PLANES):
                ov[j, r] = tv[r * rows:(r + 1) * rows, j * LANES:(j + 1) * LANES]
        oz[...] = tz
        oaz[...] = taz

    tok = lambda w: pl.BlockSpec((ts, w), lambda i: (i, 0))
    full = lambda a: pl.BlockSpec(a.shape, lambda i: (0, 0))
    pairs = pl.BlockSpec((AT_PAIRS, PLANES, ts // PLANES, LANES), lambda i: (0, 0, i, 0))
    return pl.pallas_call(
        body, name="ln_proj", grid=(s // ts,), compiler_params=_params("arbitrary"),
        in_specs=[tok(D_MODEL), full(mod), full(norm_w), tok(LANES), tok(LANES), full(conv_w8), full(alog_row),
                  full(dtb_row)] + [full(w) for w in ws],
        out_specs=(tok(D_MODEL), tok(widths[0]), tok(widths[1]), tok(widths[2]), pairs, pairs, pairs,
                   tok(widths[6]), tok(DN_WIDTH), tok(DN_WIDTH), tok(DN_WIDTH), tok(BA_PAD)),
        out_shape=(SDS((s, D_MODEL), BF16), SDS((s, widths[0]), F32), SDS((s, widths[1]), F32),
                   SDS((s, widths[2]), F32)) + (SDS((AT_PAIRS, PLANES, s // PLANES, LANES), F32),) * 3 + (SDS((s, widths[6]), F32),)
        + (SDS((s, DN_WIDTH), F32),) * 3 + (SDS((s, BA_PAD), F32),),
        scratch_shapes=[pltpu.VMEM((8, widths[0]), F32)],
    )(x, mod, norm_w, cos_t, sin_t, conv_w8, alog_row, dtb_row, *ws)


def _conv_taps(ext, rows):
    taps = []
    for j in range(CONV_K):
        sh = CONV_K - 1 - j
        rolled = pltpu.roll(ext, sh, 0) if sh else ext
        taps.append(rolled[8:8 + rows])
    return taps


def _dn_forward(q, k, v, bg):
    s = q.shape[0]
    tp = CH_UNROLL * CHUNK
    npass = s // tp
    hs = range(DN_HEADS)
    sl = [slice(h * DN_DIM, (h + 1) * DN_DIM) for h in hs]

    def body(q_ref, k_ref, v_ref, bg_ref, w_ref, qd_ref, kd_ref, p_ref, gl_ref, t_ref, o_ref, vn_ref, st_ref,
             state, u_s, w_s, qd_s, kd_s, p_s, gl_s):
        @pl.when(pl.program_id(0) == 0)
        def _():
            for ref in (state, u_s, w_s, qd_s, kd_s, p_s, gl_s):
                ref[...] = jnp.zeros_like(ref)

        def recurrence():
            for c in range(CH_UNROLL):
                rows = slice(c * CHUNK, (c + 1) * CHUNK)
                rows8 = slice(c * 8, (c + 1) * 8)
                srows = slice(c * DN_DIM, (c + 1) * DN_DIM)
                sf = [state[h] for h in hs]
                sb = [_bf(x) for x in sf]
                ws = [_nn(w_s[rows, cl], b) for cl, b in zip(sl, sb)]
                qs = [_nn(qd_s[rows, cl], b) for cl, b in zip(sl, sb)]
                yield
                vn = [u_s[rows, cl] - x for cl, x in zip(sl, ws)]
                vb = [_bf(x) for x in vn]
                kv = [_tn(kd_s[rows, cl], b) for cl, b in zip(sl, vb)]
                pv = [_nn(p_s[h, rows, :], b) for h, b in zip(hs, vb)]
                for h in hs:
                    state[h] = sf[h] * gl_s[rows8, sl[h]][0:1] + kv[h]
                for h in hs:
                    st_ref[srows, sl[h]] = sf[h]
                    vn_ref[rows, sl[h]] = vn[h]
                    o_ref[rows, sl[h]] = qs[h] + pv[h]
                yield

        steps = recurrence()

        where = [(slice(c * CHUNK, (c + 1) * CHUNK), slice(c * 8, (c + 1) * 8), h, sl[h])
                 for c in range(CH_UNROLL) for h in hs]
        bgs = [bg_ref[rows, :] for rows, _, _, _ in where]
        outs = _chunk_fwd([q_ref[rows, cl] for rows, _, _, cl in where], [k_ref[rows, cl] for rows, _, _, cl in where],
                          [v_ref[rows, cl] for rows, _, _, cl in where],
                          [b[:, h:h + 1] for b, (_, _, h, _) in zip(bgs, where)],
                          [b[:, GC_LANE + h:GC_LANE + h + 1] for b, (_, _, h, _) in zip(bgs, where)],
                          tick=lambda: next(steps, None))
        for _ in steps:
            pass
        for (rows, rows8, h, cl), (u, w, p, qd, kd, gl, t) in zip(where, outs):
            g8 = jnp.broadcast_to(gl, (8, DN_DIM))
            u_s[rows, cl] = u
            w_ref[rows, cl] = w
            w_s[rows, cl] = w
            qd_ref[rows, cl] = qd
            qd_s[rows, cl] = qd
            kd_ref[rows, cl] = kd
            kd_s[rows, cl] = kd
            p_ref[h, rows, :] = p
            p_s[h, rows, :] = p
            gl_ref[rows8, cl] = g8
            gl_s[rows8, cl] = g8
            t_ref[h, rows, :] = t

    cur = lambda i: jnp.minimum(i, npass - 1)
    done = lambda i: jnp.maximum(i - 1, 0)
    tokc = pl.BlockSpec((tp, DN_WIDTH), lambda i: (cur(i), 0))
    tokd = pl.BlockSpec((tp, DN_WIDTH), lambda i: (done(i), 0))
    sq = pl.BlockSpec((DN_HEADS, tp, CHUNK), lambda i: (0, cur(i), 0))
    return pl.pallas_call(
        body, name="dn_forward", grid=(npass + 1,), compiler_params=_params("arbitrary"),
        in_specs=[tokc] * 3 + [pl.BlockSpec((tp, BA_PAD), lambda i: (cur(i), 0))],
        out_specs=(tokc, tokc, tokc, sq, pl.BlockSpec((CH_UNROLL * 8, DN_WIDTH), lambda i: (cur(i), 0)), sq,
                   tokd, tokd, pl.BlockSpec((CH_UNROLL * DN_DIM, DN_WIDTH), lambda i: (done(i), 0))),
        out_shape=(SDS((s, DN_WIDTH), F32),) * 3 + (SDS((DN_HEADS, s, CHUNK), F32),
                                                     SDS((s // CHUNK * 8, DN_WIDTH), F32),
                                                     SDS((DN_HEADS, s, CHUNK), F32),
                                                     SDS((s, DN_WIDTH), F32), SDS((s, DN_WIDTH), F32),
                                                     SDS((s // CHUNK * DN_DIM, DN_WIDTH), F32)),
        scratch_shapes=[pltpu.VMEM((DN_HEADS, DN_DIM, DN_DIM), F32)] + [pltpu.VMEM((tp, DN_WIDTH), F32)] * 4
        + [pltpu.VMEM((DN_HEADS, tp, CHUNK), F32), pltpu.VMEM((CH_UNROLL * 8, DN_WIDTH), F32)],
    )(q, k, v, bg)


LOG2E, LN2 = 1.4426950408889634, 0.6931471805599453
MASKED = -1e30


PLANE_ROWS = ATT_BLK // PLANES


def _to_planes(tile, scr):
    scr[...] = tile
    return [scr[pl.ds(r, tile.shape[0] // PLANES, stride=PLANES), :] for r in range(PLANES)]


def _from_planes(planes, scr):
    n = planes[0].shape[0]
    for r in range(PLANES):
        scr[pl.ds(r, n, stride=PLANES), :] = planes[r]
    return scr[...]


def _plane_perm(n, back):
    row = lax.broadcasted_iota(jnp.int32, (n, n), 0)
    col = lax.broadcasted_iota(jnp.int32, (n, n), 1)
    m, c = (col, row) if back else (row, col)
    return jnp.where(c == PLANES * (m % (n // PLANES)) + m // (n // PLANES), 1.0, 0.0).astype(BF16)


def _geom(d):
    nchunk = PLANES // d
    return nchunk, Q_BLOCK // nchunk


def _pattern_bias(d):
    nchunk, qlen = _geom(d)
    row = lax.broadcasted_iota(jnp.int32, (Q_BLOCK, 2 * Q_BLOCK), 0)
    col = lax.broadcasted_iota(jnp.int32, (Q_BLOCK, 2 * Q_BLOCK), 1)
    uq, aq = row // qlen, row % qlen
    uk, ak = col // (2 * qlen), col % (2 * qlen)
    rel = nchunk * (aq - ak + qlen) + (uq - uk)
    band = jnp.where((rel >= 0) & (rel <= W_SUB), 0.0, MASKED)
    col1 = lax.broadcasted_iota(jnp.int32, (1, 2 * Q_BLOCK), 1)
    return band, (col1 % (2 * qlen)) < qlen


def _aligned(start):
    return start if isinstance(start, int) else pl.multiple_of(start, 8)


def _keys(prev_ref, cur_ref, planes, mm, ql):
    parts = []
    for p in planes:
        if isinstance(mm, int) and mm == 0:
            parts += [prev_ref[0, p, PLANE_ROWS - ql:PLANE_ROWS, :], cur_ref[0, p, 0:ql, :]]
        else:
            parts.append(cur_ref[0, p, pl.ds(_aligned(ql * (mm - 1)), 2 * ql), :])
    return jnp.concatenate(parts, axis=0)


def _gather(ref, lead, planes, start, n):
    parts = [ref[lead + (p, pl.ds(start, n), slice(None))] for p in planes]
    return parts[0] if len(parts) == 1 else jnp.concatenate(parts, axis=0)


def _scatter(ref, lead, planes, start, n, val, add):
    for u, p in enumerate(planes):
        idx = lead + (p, pl.ds(start, n), slice(None))
        if add:
            ref[idx] += val[u * n:(u + 1) * n]
        else:
            ref[idx] = val[u * n:(u + 1) * n]


def _attn_fwd(qr, kr, vv):
    s16 = qr.shape[2]
    nblk = s16 // PLANE_ROWS
    scale = AT_DIM ** -0.5
    npat = len(DILATIONS)

    def body(q_ref, kp_ref, k_ref, vp_ref, v_ref, o_ref, lse_ref, o_p, l_p):
        n = pl.program_id(1)
        lo = lax.broadcasted_iota(jnp.int32, (Q_BLOCK, LANES), 1) < AT_DIM
        nq = ATT_BLK // Q_BLOCK
        heads = [(i, sel) for i in range(nq) for sel in (lo, ~lo)]
        for pi, d in enumerate(DILATIONS):
            band, prev_cols = _pattern_bias(d)
            nchunk, ql = _geom(d)
            cs = [([c % d + d * u for u in range(nchunk)], c // d) for c in range(nq)]
            band0 = band + jnp.where(prev_cols & (n == 0), MASKED, 0.0)
            bias = [band0 if mm == 0 else band for _, mm in cs]
            qb = [_bf(_gather(q_ref, (0,), pls, ql * mm, ql)) for pls, mm in cs]
            kk = [_bf(_keys(kp_ref, k_ref, pls, mm, ql)) for pls, mm in cs]
            vb = [_bf(_keys(vp_ref, v_ref, pls, mm, ql)) for pls, mm in cs]
            sc = [lax.dot_general(jnp.where(sel, qb[i], jnp.zeros_like(qb[i])), kk[i], _NT,
                                  preferred_element_type=F32) for i, sel in heads]
            sc = [x * (scale * LOG2E) + bias[i] for x, (i, _) in zip(sc, heads)]
            mx = [jnp.max(x, axis=-1, keepdims=True) for x in sc]
            pr = [jnp.exp2(x - m) for x, m in zip(sc, mx)]
            ls = [jnp.sum(x, axis=-1, keepdims=True) for x in pr]
            pv = [jnp.dot(_bf(x), vb[i], preferred_element_type=F32) for x, (i, _) in zip(pr, heads)]
            outs = [x / l for x, l in zip(pv, ls)]
            lses = [m * LN2 + jnp.log(l) for m, l in zip(mx, ls)]
            for i, (pls, mm) in enumerate(cs):
                _scatter(o_p, (pi,), pls, ql * mm, ql, jnp.where(lo, outs[2 * i], outs[2 * i + 1]), False)
                _scatter(l_p, (pi,), pls, ql * mm, ql, jnp.where(lo, lses[2 * i], lses[2 * i + 1]), False)

        def merge(r, carry):
            ls = [l_p[pi, r] for pi in range(npat)]
            mx = jnp.maximum(jnp.maximum(ls[0], ls[1]), ls[2])
            es = [jnp.exp(l - mx) for l in ls]
            den = es[0] + es[1] + es[2]
            o_ref[0, r] = (es[0] * o_p[0, r] + es[1] * o_p[1, r] + es[2] * o_p[2, r]) / den
            lse_ref[0, r] = mx + jnp.log(den)
            return carry

        lax.fori_loop(0, PLANES, merge, 0)

    blk = pl.BlockSpec((1, PLANES, PLANE_ROWS, LANES), lambda j, n: (j, 0, n, 0))
    prev = pl.BlockSpec((1, PLANES, PLANE_ROWS, LANES), lambda j, n: (j, 0, jnp.maximum(n - 1, 0), 0))
    return pl.pallas_call(
        body, name="attn_fwd", grid=(AT_PAIRS, nblk), compiler_params=_params("arbitrary", "arbitrary"),
        in_specs=[blk, prev, blk, prev, blk], out_specs=(blk, blk),
        out_shape=(SDS(qr.shape, F32),) * 2,
        scratch_shapes=[pltpu.VMEM((npat, PLANES, PLANE_ROWS, LANES), F32)] * 2,
    )(qr, kr, kr, vv, vv)


def _out_loss(o_dn, z_dn, o_at, z_at, dnw, atw2, x, tgt, w_out, gate, fw, ts):
    s = x.shape[0]

    def body(odn, zdn, oat, zat, dnw_ref, atw_ref, x_ref, t_ref, w_ref, g_ref, fw_ref,
             dx2_ref, gw_ref, dfw_ref, dgate_ref, loss_ref, dodn, dzdn, doat, dzat, delta, ddnw, datw, perm):
        @pl.when(pl.program_id(0) == 0)
        def _():
            for ref in (gw_ref, dfw_ref, dgate_ref, loss_ref, ddnw, datw):
                ref[...] = jnp.zeros_like(ref)

        parts, vjps = [], []
        for h in range(DN_HEADS):
            cols = slice(h * DN_DIM, (h + 1) * DN_DIM)
            y, vjp = jax.vjp(_gate_dn, odn[:, cols], zdn[:, cols], dnw_ref[...])
            parts.append(_bf(y))
            vjps.append(vjp)
        oats = [_from_planes([oat[j, r] for r in range(PLANES)], perm.at[j]) for j in range(AT_PAIRS)]
        for j in range(AT_PAIRS):
            y, vjp = jax.vjp(functools.partial(_gate_at, head_sum=_d_head_sum), oats[j],
                             zat[:, j * LANES:(j + 1) * LANES], atw_ref[...])
            parts.append(_bf(y))
            vjps.append(vjp)
        catb = jnp.concatenate(parts, axis=1)
        wb = w_ref[...]
        gate, fwv = g_ref[...], fw_ref[...]
        mix = jnp.dot(catb, wb, preferred_element_type=F32)
        x2 = x_ref[...] + gate * mix
        r2 = lax.rsqrt(jnp.mean(x2 * x2, axis=-1, keepdims=True) + EPS)
        xn2 = x2 * r2
        err = xn2 * fwv - t_ref[...]
        row = jnp.sum(err * err, axis=-1, keepdims=True) * (1.0 / D_MODEL)
        loss_ref[...] += 0.5 * jnp.sum(row, axis=0, keepdims=True)
        dy = err * (1.0 / D_MODEL)
        dfw_ref[...] += jnp.sum(dy * xn2, axis=0, keepdims=True)
        dxn = dy * fwv
        dx2 = r2 * (dxn - xn2 * jnp.mean(dxn * xn2, axis=-1, keepdims=True))
        dx2_ref[...] = dx2
        dgate_ref[...] += jnp.sum(dx2 * mix, axis=0, keepdims=True)
        dmix = _bf(gate * dx2)
        dcat = lax.dot_general(dmix, wb, _NT, preferred_element_type=F32)
        gw_ref[...] += lax.dot_general(catb, dmix, _TN, preferred_element_type=F32)
        for h in range(DN_HEADS):
            cols = slice(h * DN_DIM, (h + 1) * DN_DIM)
            do, dz, dw = vjps[h](dcat[:, cols])
            dodn[:, cols] = do
            dzdn[:, cols] = _bf(dz)
            ddnw[...] += dw
        for j in range(AT_PAIRS):
            cols = slice(j * LANES, (j + 1) * LANES)
            do, dz, dw = vjps[DN_HEADS + j](dcat[:, DN_WIDTH + j * LANES:DN_WIDTH + (j + 1) * LANES])
            for r, x in enumerate(_to_planes(do, perm.at[j])):
                doat[j, r] = x
            dzat[:, cols] = _bf(dz)
            datw[...] += dw
            for r, x in enumerate(_to_planes(_head_sum(do * oats[j]), perm.at[j])):
                delta[j, r] = x

    tok = lambda w: pl.BlockSpec((ts, w), lambda i: (i, 0))
    full = lambda a: pl.BlockSpec(a.shape, lambda i: (0, 0))
    row = pl.BlockSpec((1, D_MODEL), lambda i: (0, 0))
    lrow = pl.BlockSpec((1, LANES), lambda i: (0, 0))
    pairs = pl.BlockSpec((AT_PAIRS, PLANES, ts // PLANES, LANES), lambda i: (0, 0, i, 0))
    return pl.pallas_call(
        body, name="out_loss", grid=(s // ts,), compiler_params=_params("arbitrary"),
        in_specs=[tok(DN_WIDTH), tok(DN_WIDTH), pairs, tok(AT_WIDTH), full(dnw), full(atw2),
                  tok(D_MODEL), tok(D_MODEL), full(w_out), full(gate), full(fw)],
        out_specs=(tok(D_MODEL), pl.BlockSpec((D_MODEL, D_MODEL), lambda i: (0, 0)), row, row,
                   pl.BlockSpec((1, 1), lambda i: (0, 0)), tok(DN_WIDTH), tok(DN_WIDTH), pairs, tok(AT_WIDTH), pairs,
                   lrow, lrow),
        out_shape=(SDS((s, D_MODEL), F32), SDS((D_MODEL, D_MODEL), F32), SDS((1, D_MODEL), F32),
                   SDS((1, D_MODEL), F32), SDS((1, 1), F32), SDS((s, DN_WIDTH), F32), SDS((s, DN_WIDTH), BF16),
                   SDS((AT_PAIRS, PLANES, s // PLANES, LANES), F32), SDS((s, AT_WIDTH), BF16),
                   SDS((AT_PAIRS, PLANES, s // PLANES, LANES), F32), SDS((1, LANES), F32), SDS((1, LANES), F32)),
        scratch_shapes=[pltpu.VMEM((AT_PAIRS, ts, LANES), F32)],
    )(o_dn, z_dn, o_at, z_at, dnw, atw2, x, tgt, w_out, gate, fw)


def _hand_over(out_ref, acc, n):
    zeros = jnp.zeros((PLANES, PLANE_ROWS, LANES), F32)

    @pl.when(n == 0)
    def _():
        out_ref[0] = zeros

    @pl.when(n > 0)
    def _():
        out_ref[0] = acc[...]

    acc[...] = zeros


def _add_keys(out_ref, acc, planes, mm, ql, val):
    for u, p in enumerate(planes):
        part = val[u * 2 * ql:(u + 1) * 2 * ql]
        if isinstance(mm, int) and mm == 0:
            out_ref[0, p, PLANE_ROWS - ql:PLANE_ROWS, :] += part[:ql]
            acc[p, 0:ql, :] += part[ql:]
        else:
            acc[p, pl.ds(_aligned(ql * (mm - 1)), 2 * ql), :] += part


def _attn_bwd(qr, kr, vv, do, lse, delta):
    s16 = qr.shape[2]
    nblk = s16 // PLANE_ROWS
    scale = AT_DIM ** -0.5

    nu = ATT_UNROLL_BWD
    npass = ATT_BLK // Q_BLOCK // nu

    def blocks(g, d):
        nchunk, ql = _geom(d)
        out = []
        for u in range(nu):
            r0, mm = (u % d, g * (nu // d) + u // d) if nu % d == 0 else (g * nu + u, 0)
            out.append(([r0 + d * c for c in range(nchunk)], _aligned(ql * mm), ql, mm))
        return out

    def body(q_ref, kp_ref, k_ref, vp_ref, v_ref, do_ref, lse_ref, dl_ref, dq_ref, dk_ref, dv_ref, dk_acc, dv_acc):
        n = pl.program_id(1)
        _hand_over(dk_ref, dk_acc, n)
        _hand_over(dv_ref, dv_acc, n)

        @pl.when(n < nblk)
        def _():
            lo = lax.broadcasted_iota(jnp.int32, (Q_BLOCK, LANES), 1) < AT_DIM
            for d in DILATIONS:
                band, prev_cols = _pattern_bias(d)
                band0 = band + jnp.where(prev_cols & (n == 0), MASKED, 0.0)

                def group(g, carry, d=d, band=band, band0=band0):
                    cs = blocks(g, d)
                    heads = [(i, sel) for i in range(nu) for sel in (lo, ~lo)]
                    bias = [band0 if isinstance(mm, int) and mm == 0 else band for _, _, _, mm in cs]
                    qb = [_bf(_gather(q_ref, (0,), pls, qs, ql)) for pls, qs, ql, _ in cs]
                    dob = [_bf(_gather(do_ref, (0,), pls, qs, ql)) for pls, qs, ql, _ in cs]
                    kk = [_bf(_keys(kp_ref, k_ref, pls, mm, ql)) for pls, _, ql, mm in cs]
                    vb = [_bf(_keys(vp_ref, v_ref, pls, mm, ql)) for pls, _, ql, mm in cs]
                    lse2 = [_gather(lse_ref, (0,), pls, qs, ql) * LOG2E for pls, qs, ql, _ in cs]
                    dl2 = [_gather(dl_ref, (0,), pls, qs, ql) for pls, qs, ql, _ in cs]
                    qm = [jnp.where(sel, qb[i], jnp.zeros_like(qb[i])) for i, sel in heads]
                    dom = [jnp.where(sel, dob[i], jnp.zeros_like(dob[i])) for i, sel in heads]
                    lse_c = [jnp.max(jnp.where(sel, lse2[i], -jnp.inf), axis=-1, keepdims=True) for i, sel in heads]
                    dl_c = [jnp.max(jnp.where(sel, dl2[i], -jnp.inf), axis=-1, keepdims=True) for i, sel in heads]
                    sc = [lax.dot_general(a, kk[i], _NT, preferred_element_type=F32) for a, (i, _) in zip(qm, heads)]
                    dp = [lax.dot_general(a, vb[i], _NT, preferred_element_type=F32) for a, (i, _) in zip(dom, heads)]
                    pr = [jnp.exp2(x * (scale * LOG2E) + bias[i] - l) for x, l, (i, _) in zip(sc, lse_c, heads)]
                    ds = [_bf(p * (x - dl) * scale) for p, x, dl in zip(pr, dp, dl_c)]
                    prb = [_bf(p) for p in pr]
                    dq = [jnp.dot(x, kk[i], preferred_element_type=F32) for x, (i, _) in zip(ds, heads)]
                    dk = [lax.dot_general(x, a, _TN, preferred_element_type=F32) for x, a in zip(ds, qm)]
                    dv = [lax.dot_general(x, a, _TN, preferred_element_type=F32) for x, a in zip(prb, dom)]
                    for i, (pls, qs, ql, mm) in enumerate(cs):
                        _scatter(dq_ref, (0,), pls, qs, ql, jnp.where(lo, dq[2 * i], dq[2 * i + 1]), d != DILATIONS[0])
                        _add_keys(dk_ref, dk_acc, pls, mm, ql, dk[2 * i] + dk[2 * i + 1])
                        _add_keys(dv_ref, dv_acc, pls, mm, ql, dv[2 * i] + dv[2 * i + 1])
                    return carry

                if nu % d == 0:
                    group(0, 0)
                    lax.fori_loop(1, npass, group, 0)
                else:
                    lax.fori_loop(0, npass, group, 0)

    at = lambda f: pl.BlockSpec((1, PLANES, PLANE_ROWS, LANES), lambda j, n: (j, 0, f(n), 0))
    cur = at(lambda n: jnp.minimum(n, nblk - 1))
    prev = at(lambda n: jnp.maximum(jnp.minimum(n, nblk - 1) - 1, 0))
    done = at(lambda n: jnp.maximum(n - 1, 0))
    return pl.pallas_call(
        body, name="attn_bwd", grid=(AT_PAIRS, nblk + 1), compiler_params=_params("arbitrary", "arbitrary"),
        in_specs=[cur, prev, cur, prev, cur, cur, cur, cur], out_specs=(cur, done, done),
        out_shape=(SDS(qr.shape, F32),) * 3,
        scratch_shapes=[pltpu.VMEM((PLANES, PLANE_ROWS, LANES), F32)] * 2,
    )(qr, kr, kr, vv, vv, do, lse, delta)


def _dn_backward(do, st, vn, w, qd, kd, p, gl, q, k, v, bg, t):
    s = do.shape[0]
    nc = CH_UNROLL_BWD
    tp = nc * CHUNK
    npass = s // tp
    hs = range(DN_HEADS)
    sl = [slice(h * DN_DIM, (h + 1) * DN_DIM) for h in hs]

    def body(do_ref, st_ref, vn_ref, w_ref, qd_ref, kd_ref, p_ref, gl_ref, q_ref, k_ref, v_ref, bg_ref, t_ref,
             dq_ref, dk_ref, dv_ref, dbg_ref, dstate, du_s, dw_s, dqd_s, dkd_s, dp_s, dgl_s):
        @pl.when(pl.program_id(0) == 0)
        def _():
            for ref in (dstate, du_s, dw_s, dqd_s, dkd_s, dp_s, dgl_s):
                ref[...] = jnp.zeros_like(ref)

        where = [(slice(c * CHUNK, (c + 1) * CHUNK), slice(c * 8, (c + 1) * 8), h, sl[h])
                 for c in range(nc) for h in hs]
        cots = [(du_s[rows, cl], dw_s[rows, cl], dp_s[h, rows, :], dqd_s[rows, cl], dkd_s[rows, cl],
                 dgl_s[rows8, cl][0:1, 0:1]) for rows, rows8, h, cl in where]

        def recurrence():
            for c in reversed(range(nc)):
                rows = slice(c * CHUNK, (c + 1) * CHUNK)
                rows8 = slice(c * 8, (c + 1) * 8)
                srows = slice(c * DN_DIM, (c + 1) * DN_DIM)
                ds_ = [dstate[h] for h in hs]
                dsb = [_bf(x) for x in ds_]
                dob = [_bf(do_ref[rows, cl]) for cl in sl]
                pdo = [_tn(p_ref[h, rows, :], b) for h, b in zip(hs, dob)]
                qdo = [_tn(qd_ref[rows, cl], b) for cl, b in zip(sl, dob)]
                kds = [_nn(kd_ref[rows, cl], b) for cl, b in zip(sl, dsb)]
                yield
                dvn = [a + b for a, b in zip(kds, pdo)]
                dvb = [_bf(x) for x in dvn]
                wdv = [_tn(w_ref[rows, cl], b) for cl, b in zip(sl, dvb)]
                for h in hs:
                    dstate[h] = ds_[h] * gl_ref[rows8, sl[h]][0:1] + qdo[h] - wdv[h]
                sfs = [st_ref[srows, cl] for cl in sl]
                sbs = [_bf(x) for x in sfs]
                vnb = [_bf(vn_ref[rows, cl]) for cl in sl]
                for h in hs:
                    du_s[rows, sl[h]] = dvn[h]
                    dw_s[rows, sl[h]] = -_nt(dvb[h], sbs[h])
                    dqd_s[rows, sl[h]] = _nt(dob[h], sbs[h])
                    dkd_s[rows, sl[h]] = _nt(vnb[h], dsb[h])
                    dp_s[h, rows, :] = _nt(dob[h], vnb[h])
                    dgl = jnp.sum(jnp.sum(ds_[h] * sfs[h], axis=1, keepdims=True), axis=0, keepdims=True)
                    dgl_s[rows8, sl[h]] = jnp.broadcast_to(dgl, (8, DN_DIM))
                yield

        steps = recurrence()

        bgs = [bg_ref[rows, :] for rows, _, _, _ in where]
        outs = _chunk_bwd([q_ref[rows, cl] for rows, _, _, cl in where], [k_ref[rows, cl] for rows, _, _, cl in where],
                          [v_ref[rows, cl] for rows, _, _, cl in where],
                          [b[:, h:h + 1] for b, (_, _, h, _) in zip(bgs, where)],
                          [b[:, GC_LANE + h:GC_LANE + h + 1] for b, (_, _, h, _) in zip(bgs, where)],
                          [t_ref[h, rows, :] for rows, _, h, _ in where], cots, tick=lambda: next(steps, None))
        for _ in steps:
            pass
        lane = lax.broadcasted_iota(jnp.int32, (CHUNK, BA_PAD), 1)
        for c in range(nc):
            dbg = jnp.zeros((CHUNK, BA_PAD), F32)
            for (rows, _, h, cl), (dq, dk, dv, dbeta, dgc) in list(zip(where, outs))[c * DN_HEADS:(c + 1) * DN_HEADS]:
                dq_ref[rows, cl] = dq
                dk_ref[rows, cl] = dk
                dv_ref[rows, cl] = dv
                dbg = dbg + jnp.where(lane == h, dbeta, 0.0) + jnp.where(lane == GC_LANE + h, dgc, 0.0)
            dbg_ref[where[c * DN_HEADS][0], :] = dbg

    rec = lambda i: jnp.maximum(npass - 1 - i, 0)
    loc = lambda i: jnp.minimum(npass - i, npass - 1)
    tok_r = pl.BlockSpec((tp, DN_WIDTH), lambda i: (rec(i), 0))
    tok_l = pl.BlockSpec((tp, DN_WIDTH), lambda i: (loc(i), 0))
    sq_r = pl.BlockSpec((DN_HEADS, tp, CHUNK), lambda i: (0, rec(i), 0))
    sq_l = pl.BlockSpec((DN_HEADS, tp, CHUNK), lambda i: (0, loc(i), 0))
    ba_l = pl.BlockSpec((tp, BA_PAD), lambda i: (loc(i), 0))
    return pl.pallas_call(
        body, name="dn_backward", grid=(npass + 1,), compiler_params=_params("arbitrary"),
        in_specs=[tok_r, pl.BlockSpec((nc * DN_DIM, DN_WIDTH), lambda i: (rec(i), 0)), tok_r, tok_r, tok_r, tok_r,
                  sq_r, pl.BlockSpec((nc * 8, DN_WIDTH), lambda i: (rec(i), 0)),
                  tok_l, tok_l, tok_l, ba_l, sq_l],
        out_specs=(tok_l, tok_l, tok_l, ba_l),
        out_shape=(SDS((s, DN_WIDTH), F32),) * 3 + (SDS((s, BA_PAD), F32),),
        scratch_shapes=[pltpu.VMEM((DN_HEADS, DN_DIM, DN_DIM), F32)] + [pltpu.VMEM((tp, DN_WIDTH), F32)] * 4
        + [pltpu.VMEM((DN_HEADS, tp, CHUNK), F32), pltpu.VMEM((nc * 8, DN_WIDTH), F32)],
    )(do, st, vn, w, qd, kd, p, gl, q, k, v, bg, t)


def _dn_prep_bwd(qkv_pre, ba, dq, dk, dv, dbg, conv_w8, alog_row, dtb_row, hbf, dz_dn, ts):
    s = qkv_pre.shape[0]
    cw = 3 * DN_WIDTH
    nt = s // ts

    def body(pre_ref, ph_ref, nh_ref, ba_ref, dq_ref, dqh_ref, dk_ref, dkh_ref, dv_ref, dvh_ref, dbg_ref,
             cw_ref, al_ref, dtb_ref, h_ref, dz_ref, dpre_ref, dba_ref, dcw_ref, dal_ref, ddtb_ref,
             gqkv_out, gz_out, gba_out, gqkv_ref, gz_ref, gba_ref):
        n = pl.program_id(0)

        @pl.when(n == 0)
        def _():
            gqkv_ref[...] = jnp.zeros_like(gqkv_ref)
            gz_ref[...] = jnp.zeros_like(gz_ref)
            gba_ref[...] = jnp.zeros_like(gba_ref)
            dcw_ref[...] = jnp.zeros_like(dcw_ref)
            dal_ref[...] = jnp.zeros_like(dal_ref)
            ddtb_ref[...] = jnp.zeros_like(ddtb_ref)

        hb = h_ref[...]
        gz_ref[...] += lax.dot_general(hb, dz_ref[...], _TN, preferred_element_type=F32)
        last = n == nt - 1
        prev = jnp.where(n == 0, 0.0, ph_ref[...])
        ext = jnp.concatenate([prev, pre_ref[...], nh_ref[...]], axis=0)
        taps = _conv_taps(ext, ts + 8)
        conv = taps[0] * cw_ref[0:1, :]
        for j in range(1, CONV_K):
            conv = conv + taps[j] * cw_ref[j:j + 1, :]

        def cot(main, halo, cols):
            return jnp.concatenate([main[:, cols], jnp.where(last, 0.0, halo[:, cols])], axis=0)

        rows = ts + 8
        for grp, (fn, mref, href) in enumerate(((_post_q, dq_ref, dqh_ref), (_post_k, dk_ref, dkh_ref),
                                                (_post_v, dv_ref, dvh_ref))):
            gcols = slice(grp * DN_WIDTH, (grp + 1) * DN_WIDTH)
            pieces = []
            for h in range(DN_HEADS):
                cols = slice(h * DN_DIM, (h + 1) * DN_DIM)
                c0 = grp * DN_WIDTH + h * DN_DIM
                _, vjp = jax.vjp(fn, conv[:, c0:c0 + DN_DIM])
                pieces.append(vjp(cot(mref, href, cols))[0])
            dconv = jnp.concatenate(pieces, axis=1)
            dpre = dconv[:ts] * cw_ref[CONV_K - 1:CONV_K, gcols]
            for j in range(CONV_K - 1):
                sh = CONV_K - 1 - j
                dpre = dpre + pltpu.roll(dconv, rows - sh, 0)[:ts] * cw_ref[j:j + 1, gcols]
            dpre_b = _bf(dpre)
            dpre_ref[:, gcols] = dpre_b
            gqkv_ref[:, gcols] += lax.dot_general(hb, dpre_b, _TN, preferred_element_type=F32)
            for j in range(CONV_K):
                dcw_ref[j:j + 1, gcols] += jnp.sum(dconv[:ts] * taps[j][:ts, gcols], axis=0, keepdims=True)

        dbg = dbg_ref[...]
        lane = lax.broadcasted_iota(jnp.int32, dbg.shape, 1)
        dg = pltpu.roll(_chunk_cumsum(dbg, reverse=True), BA_PAD - DN_HEADS, 1)
        cot_bg = jnp.where(lane < DN_HEADS, dbg, jnp.where(lane < GC_LANE, dg, 0.0))
        _, vjp = jax.vjp(_beta_decay, ba_ref[...], al_ref[...], dtb_ref[...])
        dba, dal, ddtb = vjp(cot_bg)
        dba_b = _bf(dba)
        dba_ref[...] = dba_b
        gba_ref[...] += lax.dot_general(hb, dba_b, _TN, preferred_element_type=F32)
        dal_ref[...] += dal
        ddtb_ref[...] += ddtb

        @pl.when(last)
        def _():
            gqkv_out[...] = _bf(gqkv_ref[...])
            gz_out[...] = _bf(gz_ref[...])
            gba_out[...] = _bf(gba_ref[...])

    tok = lambda w: pl.BlockSpec((ts, w), lambda i: (i, 0))
    full = lambda a: pl.BlockSpec(a.shape, lambda i: (0, 0))
    prevh = lambda w: pl.BlockSpec((8, w), lambda i: (jnp.maximum(i * (ts // 8) - 1, 0), 0))
    nexth = lambda w: pl.BlockSpec((8, w), lambda i: (jnp.minimum((i + 1) * (ts // 8), s // 8 - 1), 0))
    row = pl.BlockSpec((1, LANES), lambda i: (0, 0))
    return pl.pallas_call(
        body, name="dn_prep_bwd", grid=(nt,), compiler_params=_params("arbitrary"),
        in_specs=[tok(cw), prevh(cw), nexth(cw), tok(BA_PAD),
                  tok(DN_WIDTH), nexth(DN_WIDTH), tok(DN_WIDTH), nexth(DN_WIDTH), tok(DN_WIDTH), nexth(DN_WIDTH),
                  tok(BA_PAD), full(conv_w8), full(alog_row), full(dtb_row), tok(D_MODEL), tok(DN_WIDTH)],
        out_specs=(tok(cw), tok(BA_PAD), pl.BlockSpec((8, cw), lambda i: (0, 0)), row, row)
        + tuple(pl.BlockSpec((D_MODEL, w), lambda i: (0, 0)) for w in (cw, DN_WIDTH, BA_PAD)),
        out_shape=(SDS((s, cw), BF16), SDS((s, BA_PAD), BF16), SDS((8, cw), F32), SDS((1, LANES), F32),
                   SDS((1, LANES), F32)) + tuple(SDS((D_MODEL, w), BF16) for w in (cw, DN_WIDTH, BA_PAD)),
        scratch_shapes=[pltpu.VMEM((D_MODEL, w), F32) for w in (cw, DN_WIDTH, BA_PAD)],
    )(qkv_pre, qkv_pre, qkv_pre, ba, dq, dq, dk, dk, dv, dv, dbg, conv_w8, alog_row, dtb_row, hbf, dz_dn)


def _dh_dx(dps, ws, x, mod, norm_w, dx2, ts):
    s = x.shape[0]
    widths = [w.shape[1] for w in ws]
    np_ = len(ws)

    def body(*refs):
        dp_refs, w_refs = refs[:np_], refs[np_:2 * np_]
        x_ref, mod_ref, nw_ref, dx2_ref, gx_ref, dshift, dscale, dnw = refs[2 * np_:]

        @pl.when(pl.program_id(0) == 0)
        def _():
            dshift[...] = jnp.zeros_like(dshift)
            dscale[...] = jnp.zeros_like(dscale)
            dnw[...] = jnp.zeros_like(dnw)

        dh = lax.dot_general(dp_refs[0][...], w_refs[0][...], _NT, preferred_element_type=F32)
        for a, b in zip(dp_refs[1:], w_refs[1:]):
            dh = dh + lax.dot_general(a[...], b[...], _NT, preferred_element_type=F32)
        xt = x_ref[...]
        r = lax.rsqrt(jnp.mean(xt * xt, axis=-1, keepdims=True) + EPS)
        xn = xt * r
        nw = nw_ref[...]
        sc1 = 1.0 + mod_ref[:, D_MODEL:2 * D_MODEL]
        dshift[...] += jnp.sum(dh, axis=0, keepdims=True)
        dscale[...] += jnp.sum(dh * (xn * nw), axis=0, keepdims=True)
        dnw[...] += jnp.sum(dh * sc1 * xn, axis=0, keepdims=True)
        dxn = dh * sc1 * nw
        gx_ref[...] = r * (dxn - xn * jnp.mean(dxn * xn, axis=-1, keepdims=True)) + dx2_ref[...]

    tok = lambda w: pl.BlockSpec((ts, w), lambda i: (i, 0))
    full = lambda a: pl.BlockSpec(a.shape, lambda i: (0, 0))
    row = pl.BlockSpec((1, D_MODEL), lambda i: (0, 0))
    return pl.pallas_call(
        body, name="dh_dx", grid=(s // ts,), compiler_params=_params("arbitrary"),
        in_specs=[tok(w) for w in widths] + [full(w) for w in ws] + [tok(D_MODEL), full(mod), full(norm_w),
                                                                    tok(D_MODEL)],
        out_specs=(tok(D_MODEL), row, row, row),
        out_shape=(SDS((s, D_MODEL), F32),) + (SDS((1, D_MODEL), F32),) * 3,
    )(*dps, *ws, x, mod, norm_w, dx2)


def _grad_w_in_at(h, dq, dk, dv, dz_at, cos_t, sin_t, ts):
    s = h.shape[0]
    assert ts % PERM_BLK == 0

    def body(h_ref, q_ref, k_ref, v_ref, dz_ref, cos_ref, sin_ref, oq, ok, ov, gq_out, gk_out, gv_out, gz_out,
             gq, gk, gv, gz):
        @pl.when(pl.program_id(0) == 0)
        def _():
            for o in (gq, gk, gv, gz):
                o[...] = jnp.zeros_like(o)

        hb = h_ref[...]
        gz[...] += lax.dot_general(hb, dz_ref[...], _TN, preferred_element_type=F32)
        back = _plane_perm(PERM_BLK, True)
        rows = PERM_BLK // PLANES
        halves = [(slice(i * PERM_BLK, (i + 1) * PERM_BLK), slice(i * rows, (i + 1) * rows))
                  for i in range(ts // PERM_BLK)]

        def planes(ref, j, prow):
            return jnp.concatenate([ref[j, r, prow, :] for r in range(PLANES)], axis=0)

        for trow, prow in halves:
            vp = jnp.concatenate([_bf(planes(v_ref, j, prow)) for j in range(AT_PAIRS)], axis=1)
            ov[trow, :] = _bf(jnp.dot(back, vp, preferred_element_type=F32))
        gv[...] += lax.dot_general(hb, ov[...], _TN, preferred_element_type=F32)
        for g_ref, o_ref, acc in ((q_ref, oq, gq), (k_ref, ok, gk)):
            for trow, prow in halves:
                cs, sn = cos_ref[trow, :], sin_ref[trow, :]
                gs = [planes(g_ref, j, prow) for j in range(AT_PAIRS)]
                gp = jnp.concatenate([_bf(g * cs + _swap_half64(g * sn)) for g in gs], axis=1)
                o_ref[trow, :] = _bf(jnp.dot(back, gp, preferred_element_type=F32))
            acc[...] += lax.dot_general(hb, o_ref[...], _TN, preferred_element_type=F32)

        @pl.when(pl.program_id(0) == s // ts - 1)
        def _():
            for o, a in ((gq_out, gq), (gk_out, gk), (gv_out, gv), (gz_out, gz)):
                o[...] = _bf(a[...])

    tok = lambda w: pl.BlockSpec((ts, w), lambda i: (i, 0))
    pairs = pl.BlockSpec((AT_PAIRS, PLANES, ts // PLANES, LANES), lambda i: (0, 0, i, 0))
    acc = pl.BlockSpec((D_MODEL, AT_WIDTH), lambda i: (0, 0))
    return pl.pallas_call(
        body, name="grad_w_in_at", grid=(s // ts,), compiler_params=_params("arbitrary"),
        in_specs=[tok(D_MODEL), pairs, pairs, pairs, tok(AT_WIDTH), tok(LANES), tok(LANES)],
        out_specs=(tok(AT_WIDTH),) * 3 + (acc,) * 4,
        out_shape=(SDS((s, AT_WIDTH), BF16),) * 3 + (SDS((D_MODEL, AT_WIDTH), BF16),) * 4,
        scratch_shapes=[pltpu.VMEM((D_MODEL, AT_WIDTH), F32)] * 4,
    )(h, dq, dk, dv, dz_at, cos_t, sin_t)


def _adamw_math(w, g, m, v):
    m = ADAM_B1 * m + (1.0 - ADAM_B1) * g
    v = ADAM_B2 * v + (1.0 - ADAM_B2) * (g * g)
    m_hat = m / (1.0 - ADAM_B1 ** ADAM_STEP)
    v_hat = v / (1.0 - ADAM_B2 ** ADAM_STEP)
    delta = -ADAM_LR * (m_hat / (jnp.sqrt(v_hat) + ADAM_EPS) + ADAM_WD * w)
    return delta, m, v


def _adamw(w, m, v, g, name, own=None):
    def body(w_ref, m_ref, v_ref, g_ref, *rest):
        g_out, d_out, m_out, v_out = rest[-4:]
        if own is None:
            g = g_ref[...]
        else:
            g = g_ref[0].astype(F32)
            for k in range(1, N_DEV):
                g = g + g_ref[k].astype(F32)
            g = g + rest[0][...].astype(F32)
        g_out[...] = g
        d_out[...], m_out[...], v_out[...] = _adamw_math(w_ref[...], g, m_ref[...], v_ref[...])

    args = (w, m, v, g) if own is None else (w, m, v, g, own)
    return pl.pallas_call(body, name=name, compiler_params=_params(),
                          out_shape=(SDS(w.shape, F32),) * 4)(*args)


def _adamw_w_mod(w, m, v, siluc_all, dmod_mine):
    def body(w_ref, m_ref, v_ref, sc_ref, dm_ref, g_out, d_out, m_out, v_out):
        g = _htn(sc_ref[...], dm_ref[...])
        g_out[...] = g
        d_out[...], m_out[...], v_out[...] = _adamw_math(w_ref[...], g, m_ref[...], v_ref[...])

    return pl.pallas_call(body, name="adamw_w_mod", compiler_params=_params(),
                          out_shape=(SDS(w.shape, F32),) * 4)(w, m, v, siluc_all, dmod_mine)


def _pack_sum(pack_all):
    def body(p_ref, o_ref):
        t = p_ref[0]
        for k in range(1, N_DEV):
            t = t + p_ref[k]
        o_ref[...] = t

    return pl.pallas_call(body, name="pack_sum", out_shape=SDS(pack_all.shape[1:], F32))(pack_all)


def _tile(s, want):
    t = min(want, s)
    assert s % t == 0
    return t


def _local_step(x, c, positions, w_mod_bf, b_mod, norm_w, w_in_bf, conv_w, a_log, dt_bias, dn_norm_w, at_norm_w,
                w_out_bf, final_norm_w, tgt):
    s = x.shape[0]
    o = [0]
    for wdt in IN_SPLITS:
        o.append(o[-1] + wdt)
    w_ba = jnp.pad(w_in_bf[:, o[2]:o[4]], ((0, 0), (0, BA_PAD - 2 * DN_HEADS)))
    ws = [w_in_bf[:, o[0]:o[1]], w_in_bf[:, o[1]:o[2]], w_ba, w_in_bf[:, o[4]:o[5]], w_in_bf[:, o[5]:o[6]],
          w_in_bf[:, o[6]:o[7]], w_in_bf[:, o[7]:o[8]]]
    conv_w8 = jnp.pad(conv_w, ((0, 8 - CONV_K), (0, 0)))
    alog_row = jnp.pad(a_log, ((0, 0), (DN_HEADS, BA_PAD - 2 * DN_HEADS)))
    dtb_row = jnp.pad(dt_bias, ((0, 0), (DN_HEADS, BA_PAD - 2 * DN_HEADS)))
    atw2 = jnp.concatenate([at_norm_w, at_norm_w], axis=1)

    half = AT_DIM // 2
    lane = jnp.arange(LANES)
    inv_freq = ROPE_THETA ** (-(lane % half).astype(F32) / half)
    pos = positions.reshape(s // PERM_BLK, PERM_BLK // PLANES, PLANES).transpose(0, 2, 1).reshape(s)
    ang = pos.astype(F32)[:, None] * inv_freq
    cos_t = jnp.cos(ang)
    sin_t = jnp.sin(ang) * jnp.where((lane // half) % 2 == 0, -1.0, 1.0)

    mod, siluc = _adaln_mod(c, w_mod_bf, b_mod)
    gate = mod[:, 2 * D_MODEL:]
    hbf, qkv_pre, z_dn, ba, qr, kr, vb, z_at, q, k, v, bg = _ln_proj(
        x, mod, norm_w, ws, cos_t, sin_t, conv_w8, alog_row, dtb_row, _tile(s, 256))
    w, qd, kd, p, gl, tinv, o_dn, vn, st = _dn_forward(q, k, v, bg)
    o_at, lse = _attn_fwd(qr, kr, vb)
    (dx2, gw_out, dfw, dgate, loss, do_dn, dz_dn, do_at, dz_at, delta, ddnw, datw) = _out_loss(
        o_dn, z_dn, o_at, z_at, dn_norm_w, atw2, x, tgt, w_out_bf, gate, final_norm_w, _tile(s, 512))

    daq, dak, dav, g_aq, g_ak, g_av, g_az = _grad_w_in_at(hbf, *_attn_bwd(qr, kr, vb, do_at, lse, delta), dz_at,
                                                           cos_t, sin_t, _tile(s, 512))
    dq, dk, dv, dbg = _dn_backward(do_dn, st, vn, w, qd, kd, p, gl, q, k, v, bg, tinv)
    dqkv, dba, dcw, dal, ddtb, g_qkv, g_z, g_ba = _dn_prep_bwd(qkv_pre, ba, dq, dk, dv, dbg, conv_w8, alog_row, dtb_row,
                                                               hbf, dz_dn, _tile(s, 512))
    dps = [dqkv, dz_dn, dba, daq, dak, dav, dz_at]
    gw_in = jnp.concatenate([g_qkv, g_z, g_ba[:, :2 * DN_HEADS], g_aq, g_ak, g_av, g_az], axis=1)
    small = dict(conv=dcw[:CONV_K], dgate=dgate, siluc=siluc, dfw=dfw, alog=dal, dtb=ddtb, dnn=ddnw, atn=datw)

    def input_grad(token):
        gx, dshift, dscale, dnw = _dh_dx(dps, ws, x, mod + token, norm_w, dx2, _tile(s, 512))
        return gx, jnp.concatenate([dshift, dscale, small["dgate"]], axis=1), dnw

    return loss, gw_in, gw_out, small, input_grad


def kernel(x, c, positions, w_mod, b_mod, norm_w, w_in, conv_w, a_log, dt_bias, dn_norm_w, at_norm_w, w_out, final_norm_w, loss_target, m_w_mod, m_b_mod, m_norm_w, m_w_in, m_conv_w, m_a_log, m_dt_bias, m_dn_norm_w, m_at_norm_w, m_w_out, m_final_norm_w, v_w_mod, v_b_mod, v_norm_w, v_w_in, v_conv_w, v_a_log, v_dt_bias, v_dn_norm_w, v_at_norm_w, v_w_out, v_final_norm_w):
    me = 4 * lax.axis_index("x") + 2 * lax.axis_index("y") + lax.axis_index("c")
    s = x.shape[1]

    g_mod, g_in, g_conv, g_out = _all_gather(
        [_bf(w_mod[0]), _bf(w_in[0]), conv_w[0], _bf(w_out[0])], "gather_weights")
    w_mod_bf = g_mod.transpose(1, 0, 2).reshape(D_MODEL, 3 * D_MODEL)
    w_in_bf = g_in.transpose(1, 0, 2).reshape(D_MODEL, IN_COLS)
    conv_full = g_conv.transpose(1, 0, 2).reshape(CONV_K, 3 * DN_WIDTH)
    w_out_bf = g_out.reshape(D_MODEL, D_MODEL)

    loss, gw_in, gw_out, small, input_grad = _local_step(
        x[0], c, positions[0], w_mod_bf, b_mod, norm_w, w_in_bf, conv_full, a_log, dt_bias, dn_norm_w, at_norm_w,
        w_out_bf, final_norm_w.reshape(1, D_MODEL), loss_target[0])

    gw_in_slabs = gw_in.reshape(D_MODEL, N_DEV, IN_SHARD).transpose(1, 0, 2)
    gw_out_slabs = _bf(gw_out).reshape(N_DEV, D_MODEL // N_DEV, D_MODEL)
    send_sems, recv_sems, srcs, lands, token = _scatter_start([gw_in_slabs, gw_out_slabs])
    gx, dmod, dnw = input_grad(token[0, 0])
    r_in, r_out = _scatter_wait(send_sems, recv_sems, srcs, lands, gx)
    own_in = lax.dynamic_index_in_dim(gw_in_slabs, me, 0, keepdims=False)
    own_out = lax.dynamic_index_in_dim(gw_out_slabs, me, 0, keepdims=False)

    pack = jnp.concatenate([small["conv"].reshape(1, -1), dmod, small["siluc"], dnw, small["dfw"],
                            small["alog"], small["dtb"], small["dnn"], small["atn"],
                            jnp.pad(loss, ((0, 0), (0, LANES - 1)))], axis=1).reshape(PK_ROWS, LANES)
    (pack_all,) = _exchange([pack], [False], "exchange_small")

    res = {}
    res["w_in"] = _adamw(w_in[0], m_w_in[0], v_w_in[0], r_in, "adamw_w_in", own=own_in)
    res["w_out"] = _adamw(w_out[0], m_w_out[0], v_w_out[0], r_out, "adamw_w_out", own=own_out)
    flat_all = pack_all.reshape(N_DEV, PK_END)
    dmod_mine = lax.dynamic_slice(flat_all, (0, PK_DMOD + me * (3 * D_MODEL // N_DEV)), (N_DEV, 3 * D_MODEL // N_DEV))
    res["w_mod"] = _adamw_w_mod(w_mod[0], m_w_mod[0], v_w_mod[0], flat_all[:, PK_SILUC:PK_DNW], dmod_mine)
    tot = _pack_sum(pack_all).reshape(1, PK_END)
    g_conv_full = tot[:, PK_CONV:PK_DMOD].reshape(CONV_K, 3 * DN_WIDTH)
    g_conv_mine = lax.dynamic_slice(g_conv_full, (0, me * (3 * DN_WIDTH // N_DEV)), (CONV_K, 3 * DN_WIDTH // N_DEV))
    res["conv_w"] = _adamw(conv_w[0], m_conv_w[0], v_conv_w[0], g_conv_mine, "adamw_conv_w")
    res["b_mod"] = _adamw(b_mod, m_b_mod, v_b_mod, tot[:, PK_DMOD:PK_SILUC], "adamw_b_mod")
    res["norm_w"] = _adamw(norm_w, m_norm_w, v_norm_w, tot[:, PK_DNW:PK_DFW], "adamw_norm_w")
    res["a_log"] = _adamw(a_log, m_a_log, v_a_log, tot[:, PK_ALOG + DN_HEADS:PK_ALOG + 2 * DN_HEADS], "adamw_a_log")
    res["dt_bias"] = _adamw(dt_bias, m_dt_bias, v_dt_bias, tot[:, PK_DTB + DN_HEADS:PK_DTB + 2 * DN_HEADS],
                            "adamw_dt_bias")
    res["dn_norm_w"] = _adamw(dn_norm_w, m_dn_norm_w, v_dn_norm_w, tot[:, PK_DNN:PK_ATN], "adamw_dn_norm_w")
    g_atn = tot[:, PK_ATN:PK_ATN + AT_DIM] + tot[:, PK_ATN + AT_DIM:PK_LOSS]
    res["at_norm_w"] = _adamw(at_norm_w, m_at_norm_w, v_at_norm_w, g_atn, "adamw_at_norm_w")
    fin = _adamw(final_norm_w.reshape(1, D_MODEL), m_final_norm_w.reshape(1, D_MODEL),
                 v_final_norm_w.reshape(1, D_MODEL), tot[:, PK_DFW:PK_ALOG], "adamw_final_norm_w")
    res["final_norm_w"] = tuple(a.reshape(D_MODEL) for a in fin)

    lead = ("w_mod", "w_in", "conv_w", "w_out")
    names = ("w_mod", "b_mod", "norm_w", "w_in", "conv_w", "a_log", "dt_bias", "dn_norm_w", "at_norm_w", "w_out",
             "final_norm_w")
    out = [tot[0, PK_LOSS], gx.reshape(1, s, D_MODEL)]
    for kind in range(4):
        for nm in names:
            a = res[nm][kind]
            out.append(a[None] if nm in lead else a)
    return tuple(out)
```

```python
import functools

import jax
import jax.numpy as jnp
from jax import lax
from jax.experimental import pallas as pl
from jax.experimental.pallas import tpu as pltpu

F32, BF16 = jnp.float32, jnp.bfloat16
HI = lax.Precision.HIGHEST
SDS = jax.ShapeDtypeStruct

D_MODEL = 1024
DN_HEADS, DN_DIM, DN_WIDTH = 4, 128, 512
AT_HEADS, AT_DIM, AT_WIDTH = 8, 64, 512
CONV_K = 4
CHUNK = 64
Q_BLOCK = 128
W_SUB = 128
DILATIONS = (1, 4, 16)
AT_PAIRS = 4
PLANES = 16
PERM_BLK = 256
ATT_BLK = Q_BLOCK * max(DILATIONS)
ATT_UNROLL_BWD = 4
CH_UNROLL, CH_UNROLL_BWD = 4, 8
ROPE_THETA = 10000.0
EPS = 1e-6
N_DEV = 8
LANES = 128
BA_PAD = 128
IN_SPLITS = (1536, 512, 4, 4, 512, 512, 512, 512)
IN_COLS = sum(IN_SPLITS)
IN_SHARD = IN_COLS // N_DEV
VMEM_LIMIT = 58 * 2 ** 20

ADAM_LR, ADAM_B1, ADAM_B2, ADAM_EPS, ADAM_WD, ADAM_STEP = 0.001, 0.9, 0.999, 1e-08, 0.01, 10

PK_CONV, PK_DMOD, PK_SILUC, PK_DNW, PK_DFW, PK_ALOG, PK_DTB, PK_DNN, PK_ATN, PK_LOSS, PK_END = (
    0, 6144, 9216, 10240, 11264, 12288, 12416, 12544, 12672, 12800, 12928)
PK_ROWS = PK_END // LANES

_NT = (((1,), (1,)), ((), ()))
_TN = (((0,), (0,)), ((), ()))


def _params(*sem):
    return pltpu.CompilerParams(dimension_semantics=sem or None, vmem_limit_bytes=VMEM_LIMIT)


def _bf(x):
    return x.astype(BF16)


def _nn(a, b):
    return jnp.dot(_bf(a), _bf(b), preferred_element_type=F32)


def _nt(a, b):
    return lax.dot_general(_bf(a), _bf(b), _NT, preferred_element_type=F32)


def _tn(a, b):
    return lax.dot_general(_bf(a), _bf(b), _TN, preferred_element_type=F32)


def _htn(a, b):
    return lax.dot_general(a, b, _TN, precision=HI, preferred_element_type=F32)


def _head_sum(x):
    r = lax.broadcasted_iota(jnp.int32, (LANES, LANES), 0)
    c = lax.broadcasted_iota(jnp.int32, (LANES, LANES), 1)
    same = jnp.where((r // AT_DIM) == (c // AT_DIM), 1.0, 0.0).astype(BF16)
    hi, lo = _hl(x)
    return jnp.dot(hi, same, preferred_element_type=F32) + jnp.dot(lo, same, preferred_element_type=F32)


@jax.custom_vjp
def _d_head_sum(x):
    return _head_sum(x)


_d_head_sum.defvjp(lambda x: (_head_sum(x), None), lambda _, g: (_head_sum(g),))


def _silu(x):
    return x * jax.nn.sigmoid(x)


def _softplus(x):
    return jnp.maximum(x, 0.0) + jnp.log(1.0 + jnp.exp(-jnp.abs(x)))


def _l2n(x):
    return x * lax.rsqrt(jnp.sum(x * x, axis=-1, keepdims=True) + EPS)


def _post_q(x):
    return _l2n(_silu(x)) * (DN_DIM ** -0.5)


def _post_k(x):
    return _l2n(_silu(x))


def _post_v(x):
    return _silu(x)


def _beta_decay(ba, alog_row, dtb_row):
    lane = lax.broadcasted_iota(jnp.int32, ba.shape, 1)
    return jnp.where(lane < DN_HEADS, jax.nn.sigmoid(ba), -jnp.exp(alog_row) * _softplus(ba + dtb_row))


def _gate_dn(o, z, w):
    return (o * lax.rsqrt(jnp.mean(o * o, axis=-1, keepdims=True) + EPS)) * w * _silu(z)


def _gate_at(o, z, w2, head_sum):
    ms = head_sum(o * o) * (1.0 / AT_DIM)
    return (o * lax.rsqrt(ms + EPS)) * w2 * _silu(z)


def _swap_half64(x):
    lane = lax.broadcasted_iota(jnp.int32, x.shape, 1)
    return jnp.where((lane & (AT_DIM - 1)) < AT_DIM // 2, pltpu.roll(x, LANES - AT_DIM // 2, 1),
                     pltpu.roll(x, AT_DIM // 2, 1))


_NN = (((1,), (0,)), ((), ()))


def _hl(a):
    hi = a.astype(BF16)
    return hi, (a - hi.astype(F32)).astype(BF16)


def _mm3(a, b, dims=_NN):
    (ah, al), (bh, bl) = a, b
    f = lambda x, y: lax.dot_general(x, y, dims, preferred_element_type=F32)
    return f(ah, bh) + (f(ah, bl) + f(al, bh))


def _chunk_masks():
    r = lax.broadcasted_iota(jnp.int32, (CHUNK, CHUNK), 0)
    c = lax.broadcasted_iota(jnp.int32, (CHUNK, CHUNK), 1)
    return r >= c, r > c, (r == c).astype(F32), (r // 16) == (c // 16)


def _tri_inv(mats, tick=lambda: None):
    _, _, eye, blk = _chunk_masks()
    dg = [jnp.where(blk, a, 0.0) for a in mats]
    lo = [jnp.where(blk, 0.0, a) for a in mats]
    sdg = [_hl(x) for x in dg]
    d2 = [_mm3(s, s) for s in sdg]
    tick()
    sd2 = [_hl(x) for x in d2]
    d4 = [_mm3(s, s) for s in sd2]
    tick()
    sd4 = [_hl(x) for x in d4]
    d8 = [_mm3(s, s) for s in sd4]
    tick()
    p1 = [_mm3(_hl(eye - a), _hl(eye + b)) for a, b in zip(dg, d2)]
    tick()
    p2 = [_mm3(_hl(a), _hl(eye + b)) for a, b in zip(p1, d4)]
    tick()
    dinv = [_mm3(_hl(a), _hl(eye + b)) for a, b in zip(p2, d8)]
    tick()
    sdinv = [_hl(x) for x in dinv]
    n1 = [_mm3(s, _hl(b)) for s, b in zip(sdinv, lo)]
    tick()
    sn1 = [_hl(x) for x in n1]
    n2 = [_mm3(s, s) for s in sn1]
    tick()
    q1 = [_mm3(_hl(eye - a), _hl(eye + b)) for a, b in zip(n1, n2)]
    return [_mm3(_hl(a), s) for a, s in zip(q1, sdinv)]


def _chunk_common(qs, ks, vs, betas, gcs):
    tril, _, _, _ = _chunk_masks()
    out = []
    for q, k, v, beta, gc in zip(qs, ks, vs, betas, gcs):
        gb = jnp.broadcast_to(gc, (CHUNK, DN_DIM))
        gt = gb.T[:CHUNK, :]
        gam = jnp.where(tril, jnp.exp(jnp.where(tril, gb[:, :CHUNK] - gt, 0.0)), 0.0)
        last = gb[CHUNK - 1:CHUNK, :]
        eg, e2 = jnp.exp(gb), jnp.exp(last - gb)
        kb, vb = k * beta, v * beta
        out.append(dict(gam=gam, eg=eg, e2=e2, gl=jnp.exp(last[:, 0:1]), kb=kb, vb=vb, kbg=kb * eg,
                        m=_nt(kb, k), qk=_nt(q, k)))
    return out


def _chunk_fwd(qs, ks, vs, betas, gcs, tick=lambda: None):
    tril, strict, _, _ = _chunk_masks()
    cm = _chunk_common(qs, ks, vs, betas, gcs)
    ts = _tri_inv([jnp.where(strict, c["m"] * c["gam"], 0.0) for c in cm], tick)
    outs = []
    for q, k, c, t in zip(qs, ks, cm, ts):
        uw = _nn(t, jnp.concatenate([c["vb"], c["kbg"]], axis=1))
        p = jnp.where(tril, c["qk"] * c["gam"], 0.0)
        outs.append((uw[:, :DN_DIM], uw[:, DN_DIM:], p, q * c["eg"], k * c["e2"], c["gl"], t.T))
    return outs


def _chunk_bwd(qs, ks, vs, betas, gcs, ts, cots, tick=lambda: None):
    tril, strict, _, _ = _chunk_masks()
    cm = _chunk_common(qs, ks, vs, betas, gcs)
    tick()
    row = lax.broadcasted_iota(jnp.int32, (CHUNK, 1), 0)
    ones = jnp.ones((CHUNK, DN_DIM), BF16)
    rs = lambda x: jnp.sum(x, axis=-1, keepdims=True)
    tts = [_bf(t) for t in ts]
    duw = [_bf(jnp.concatenate([ct[0], ct[1]], axis=1)) for ct in cots]
    dts = [_nt(a, jnp.concatenate([c["vb"], c["kbg"]], axis=1)) for a, c in zip(duw, cm)]
    tick()
    xs = [_nn(t, d) for t, d in zip(tts, dts)]
    tick()
    das = [jnp.where(strict, -_nn(x, t), 0.0) for x, t in zip(xs, tts)]
    dvks = [_nn(t, a) for t, a in zip(tts, duw)]
    tick()
    outs = []
    every = max(1, len(qs) // 5)
    for idx, (q, k, v, beta, c, ct, da, dvk) in enumerate(zip(qs, ks, vs, betas, cm, cots, das, dvks)):
        if idx and idx % every == 0:
            tick()
        _, _, dp, dqd, dkd, dgl = ct
        dvb, dkbg = dvk[:, :DN_DIM], dvk[:, DN_DIM:]
        dm = da * c["gam"]
        dqk = jnp.where(tril, dp, 0.0) * c["gam"]
        e = dm * c["m"] + dqk * c["qk"]
        dmq = jnp.concatenate([dm, dqk], axis=0)
        r1 = _nn(dmq, k)
        dkb = r1[:CHUNK] + dkbg * c["eg"]
        dq = r1[CHUNK:] + dqd * c["eg"]
        dk = _tn(dmq, jnp.concatenate([c["kb"], q], axis=0)) + dkd * c["e2"] + dkb * beta
        dbeta = rs(dkb * k + dvb * v)
        eh, el = _hl(e)
        colsum = (lax.dot_general(eh, ones, _TN, preferred_element_type=F32)
                  + lax.dot_general(el, ones, _TN, preferred_element_type=F32))[:, 0:1]
        pkd = dkd * (k * c["e2"])
        dgc = rs(e) - colsum + rs(dqd * q * c["eg"] + dkbg * c["kbg"] - pkd)
        tail = rs(jnp.sum(pkd, axis=0, keepdims=True)) + dgl * c["gl"]
        dgc = dgc + jnp.where(row == CHUNK - 1, tail, 0.0)
        outs.append((dq, dk, dvb * beta, dbeta, dgc))
    return outs


def _chunk_cumsum(x, reverse=False):
    n = x.shape[0]
    pos = lax.broadcasted_iota(jnp.int32, x.shape, 0) & (CHUNK - 1)
    sh = 1
    while sh < CHUNK:
        if reverse:
            x = x + jnp.where(pos < CHUNK - sh, pltpu.roll(x, n - sh, 0), 0.0)
        else:
            x = x + jnp.where(pos >= sh, pltpu.roll(x, sh, 0), 0.0)
        sh *= 2
    return x


GC_LANE = 2 * DN_HEADS


def _exchange(arrays, scatter, name):
    n = len(arrays)
    out_shapes = []
    for a, sc in zip(arrays, scatter):
        out_shapes.append(SDS(a.shape if sc else (N_DEV,) + a.shape, a.dtype))

    def body(*refs):
        ins, outs = refs[:n], refs[n:2 * n]
        send_sems, recv_sems, loc_sems = refs[2 * n:]
        x, y, c = lax.axis_index("x"), lax.axis_index("y"), lax.axis_index("c")
        me = 4 * x + 2 * y + c
        local, remote = [], []
        for i in range(n):
            src = ins[i].at[me] if scatter[i] else ins[i]
            cp = pltpu.make_async_copy(src, outs[i].at[me], loc_sems.at[i])
            cp.start()
            local.append(cp)
        for dlt in range(1, N_DEV):
            px = 1 - x if dlt & 4 else x
            py = 1 - y if dlt & 2 else y
            pc = 1 - c if dlt & 1 else c
            peer = 4 * px + 2 * py + pc
            for i in range(n):
                src = ins[i].at[peer] if scatter[i] else ins[i]
                cp = pltpu.make_async_remote_copy(
                    src_ref=src, dst_ref=outs[i].at[me],
                    send_sem=send_sems.at[i, dlt - 1], recv_sem=recv_sems.at[i, dlt - 1],
                    device_id=(px, py, pc), device_id_type=pl.DeviceIdType.MESH)
                cp.start()
                arrive = pltpu.make_async_remote_copy(
                    src_ref=src, dst_ref=outs[i].at[peer],
                    send_sem=send_sems.at[i, dlt - 1], recv_sem=recv_sems.at[i, dlt - 1],
                    device_id=(px, py, pc), device_id_type=pl.DeviceIdType.MESH)
                remote.append((cp, arrive))
        for cp, arrive in remote:
            cp.wait_send()
            arrive.wait_recv()
        for cp in local:
            cp.wait()

    any_spec = pl.BlockSpec(memory_space=pl.ANY)
    return pl.pallas_call(
        body, name=name, out_shape=tuple(out_shapes),
        in_specs=[any_spec] * n, out_specs=tuple([any_spec] * n),
        scratch_shapes=[pltpu.SemaphoreType.DMA((n, N_DEV - 1)), pltpu.SemaphoreType.DMA((n, N_DEV - 1)),
                        pltpu.SemaphoreType.DMA((n,))],
    )(*arrays)


def _all_gather(arrays, name):
    n = len(arrays)

    def body(*refs):
        ins, outs = refs[:n], refs[n:2 * n]
        send_sems, recv_sems, loc_sems = refs[2 * n:]
        x, y, c = lax.axis_index("x"), lax.axis_index("y"), lax.axis_index("c")
        me, sibling = (x, y, c), (x, y, 1 - c)
        chips = [(1 - x, y), (x, 1 - y), (1 - x, 1 - y)]

        def copy(i, k, block, to, src=None):
            slot = outs[i].at[4 * block[0] + 2 * block[1] + block[2]]
            return pltpu.make_async_remote_copy(
                src_ref=slot if src is None else src, dst_ref=slot,
                send_sem=send_sems.at[i, k], recv_sem=recv_sems.at[i, k],
                device_id=to, device_id_type=pl.DeviceIdType.MESH)

        mine = [pltpu.make_async_copy(ins[i], outs[i].at[4 * x + 2 * y + c], loc_sems.at[i]) for i in range(n)]
        for cp in mine:
            cp.start()
        first = []
        for i in range(n):
            first.append(copy(i, 0, me, sibling, src=ins[i]))
            first += [copy(i, 1 + j, me, (*chip, c), src=ins[i]) for j, chip in enumerate(chips)]
        for cp in first:
            cp.start()
        passed = []
        for j, chip in enumerate(chips):
            for i in range(n):
                copy(i, 1 + j, (*chip, c), me).wait_recv()
                fwd = copy(i, 4 + j, (*chip, c), sibling)
                fwd.start()
                passed.append(fwd)
        for i in range(n):
            copy(i, 0, sibling, me).wait_recv()
        for j, chip in enumerate(chips):
            for i in range(n):
                copy(i, 4 + j, (*chip, 1 - c), me).wait_recv()
        for cp in first + passed:
            cp.wait_send()
        for cp in mine:
            cp.wait()

    any_spec = pl.BlockSpec(memory_space=pl.ANY)
    return pl.pallas_call(
        body, name=name, out_shape=tuple(SDS((N_DEV,) + a.shape, a.dtype) for a in arrays),
        in_specs=[any_spec] * n, out_specs=tuple([any_spec] * n),
        scratch_shapes=[pltpu.SemaphoreType.DMA((n, N_DEV - 1)), pltpu.SemaphoreType.DMA((n, N_DEV - 1)),
                        pltpu.SemaphoreType.DMA((n,))],
    )(*arrays)


_HBM = pl.BlockSpec(memory_space=pltpu.HBM)
_SEM = pl.BlockSpec(memory_space=pltpu.SEMAPHORE)


def _peers(x, y, c):
    out = []
    for dlt in range(1, N_DEV):
        px = 1 - x if dlt & 4 else x
        py = 1 - y if dlt & 2 else y
        pc = 1 - c if dlt & 1 else c
        out.append((dlt, (px, py, pc), 4 * px + 2 * py + pc))
    return out


def _scatter_start(arrays):
    n = len(arrays)
    ns = n * (N_DEV - 1)

    def body(*refs):
        ins, lands = refs[:n], refs[n:2 * n]
        send_sems, recv_sems = refs[2 * n:2 * n + ns], refs[2 * n + ns:2 * n + 2 * ns]
        token = refs[-1]
        x, y, c = lax.axis_index("x"), lax.axis_index("y"), lax.axis_index("c")
        me = 4 * x + 2 * y + c
        for dlt, peer, pi in _peers(x, y, c):
            for i in range(n):
                k = i * (N_DEV - 1) + dlt - 1
                pltpu.make_async_remote_copy(
                    src_ref=ins[i].at[pi], dst_ref=lands[i].at[me], send_sem=send_sems[k], recv_sem=recv_sems[k],
                    device_id=peer, device_id_type=pl.DeviceIdType.MESH).start()
        token[...] = jnp.zeros_like(token)

    sem = pltpu.SemaphoreType.DMA(())
    thru = tuple(pltpu.HBM(a.shape, a.dtype) for a in arrays)
    hbm = lambda a: pltpu.with_memory_space_constraint(a, pltpu.HBM)
    outs = pl.pallas_call(
        body, name="scatter_start", out_shape=(sem,) * (2 * ns) + thru + thru + (SDS((8, LANES), F32),),
        in_specs=[_HBM] * (2 * n),
        out_specs=(_SEM,) * (2 * ns) + (_HBM,) * (2 * n) + (pl.BlockSpec(memory_space=pltpu.VMEM),),
        input_output_aliases={i: 2 * ns + i for i in range(2 * n)},
        compiler_params=pltpu.CompilerParams(has_side_effects=pltpu.SideEffectType.DATAFLOW_SIDE_EFFECTING),
    )(*[hbm(a) for a in arrays], *[hbm(jnp.zeros(a.shape, a.dtype)) for a in arrays])
    return outs[:ns], outs[ns:2 * ns], outs[2 * ns:2 * ns + n], outs[2 * ns + n:2 * ns + 2 * n], outs[-1]


def _scatter_wait(send_sems, recv_sems, srcs, lands, after):
    n = len(srcs)
    ns = n * (N_DEV - 1)

    def body(*refs):
        ins, lands_ = refs[:n], refs[n:2 * n]
        send, recv = refs[2 * n:2 * n + ns], refs[2 * n + ns:2 * n + 2 * ns]
        x, y, c = lax.axis_index("x"), lax.axis_index("y"), lax.axis_index("c")
        for dlt, peer, pi in _peers(x, y, c):
            for i in range(n):
                k = i * (N_DEV - 1) + dlt - 1
                cp = pltpu.make_async_remote_copy(
                    src_ref=ins[i].at[pi], dst_ref=lands_[i].at[pi], send_sem=send[k], recv_sem=recv[k],
                    device_id=peer, device_id_type=pl.DeviceIdType.MESH)
                cp.wait_send()
                cp.wait_recv()

    thru = tuple(pltpu.HBM(a.shape, a.dtype) for a in srcs)
    outs = pl.pallas_call(
        body, name="scatter_wait", out_shape=thru + thru,
        in_specs=[_HBM] * (2 * n) + [_SEM] * (2 * ns) + [pl.BlockSpec(memory_space=pl.ANY)],
        out_specs=(_HBM,) * (2 * n), input_output_aliases={i: i for i in range(2 * n)},
        compiler_params=pltpu.CompilerParams(has_side_effects=pltpu.SideEffectType.DATAFLOW_SIDE_EFFECTING),
    )(*srcs, *lands, *send_sems, *recv_sems, after)
    return outs[n:]


def _adaln_mod(c, w_mod, b_mod):
    def body(c_ref, w_ref, b_ref, mod_ref, sc_ref):
        sc = _silu(c_ref[...])
        sc8 = jnp.broadcast_to(sc, (8, D_MODEL))
        mod_ref[...] = _nn(sc8, w_ref[...])[0:1] + b_ref[...]
        sc_ref[...] = sc

    return pl.pallas_call(body, name="adaln_mod", compiler_params=_params(),
                          out_shape=(SDS((1, 3 * D_MODEL), F32), SDS((1, D_MODEL), F32)))(c, w_mod, b_mod)


def _ln_proj(x, mod, norm_w, ws, cos_t, sin_t, conv_w8, alog_row, dtb_row, ts):
    s = x.shape[0]
    widths = [w.shape[1] for w in ws]
    assert ts == PERM_BLK

    def body(x_ref, mod_ref, nw_ref, cos_ref, sin_ref, cw_ref, al_ref, dtb_ref, wqkv, wz, wba, waq, wak, wav, waz,
             h_ref, oqkv, oz, oba, oq, ok, ov, oaz, q_ref, k_ref, v_ref, bg_ref, halo):
        n = pl.program_id(0)
        xt = x_ref[...]
        r = lax.rsqrt(jnp.mean(xt * xt, axis=-1, keepdims=True) + EPS)
        shift, scale = mod_ref[:, 0:D_MODEL], mod_ref[:, D_MODEL:2 * D_MODEL]
        h = ((xt * r) * nw_ref[...]) * (1.0 + scale) + shift
        hb = _bf(h)
        h_ref[...] = hb
        hp = jnp.dot(_plane_perm(ts, False), hb, preferred_element_type=F32)
        pre = jnp.dot(hb, wqkv[...], preferred_element_type=F32)
        ba = jnp.dot(hb, wba[...], preferred_element_type=F32)
        hp = _bf(hp)
        tq = jnp.dot(hp, waq[...], preferred_element_type=F32)
        tk = jnp.dot(hp, wak[...], preferred_element_type=F32)
        tv = jnp.dot(hp, wav[...], preferred_element_type=F32)
        tz = jnp.dot(hb, wz[...], preferred_element_type=F32)
        taz = jnp.dot(hb, waz[...], preferred_element_type=F32)
        cs, sn = cos_ref[...], sin_ref[...]
        rows = ts // PLANES
        for t, o_ref in ((tq, oq), (tk, ok)):
            for j in range(AT_PAIRS):
                tj = t[:, j * LANES:(j + 1) * LANES]
                rot = tj * cs + _swap_half64(tj) * sn
                for r in range(PLANES):
                    o_ref[j, r] = rot[r * rows:(r + 1) * rows]
        oqkv[...] = pre
        ext = jnp.concatenate([jnp.where(n == 0, 0.0, halo[...]), pre], axis=0)
        halo[...] = pre[ts - 8:ts]
        taps = _conv_taps(ext, ts)
        conv = taps[0] * cw_ref[0:1, :]
        for j in range(1, CONV_K):
            conv = conv + taps[j] * cw_ref[j:j + 1, :]
        for hd in range(DN_HEADS):
            cols = slice(hd * DN_DIM, (hd + 1) * DN_DIM)
            q_ref[:, cols] = _post_q(conv[:, hd * DN_DIM:(hd + 1) * DN_DIM])
            k_ref[:, cols] = _post_k(conv[:, DN_WIDTH + hd * DN_DIM:DN_WIDTH + (hd + 1) * DN_DIM])
            v_ref[:, cols] = _post_v(conv[:, 2 * DN_WIDTH + hd * DN_DIM:2 * DN_WIDTH + (hd + 1) * DN_DIM])
        oba[...] = ba
        bg = _beta_decay(ba, al_ref[...], dtb_ref[...])
        lane = lax.broadcasted_iota(jnp.int32, bg.shape, 1)
        run = pltpu.roll(_chunk_cumsum(bg), DN_HEADS, 1)
        bg_ref[...] = jnp.where((lane >= GC_LANE) & (lane < GC_LANE + DN_HEADS), run, bg)
        for j in range(AT_PAIRS):
            for r in range(PLANES):
                ov[j, r] = tv[r * rows:(r + 1) * rows, j * LANES:(j + 1) * LANES]
        oz[...] = tz
        oaz[...] = taz

    tok = lambda w: pl.BlockSpec((ts, w), lambda i: (i, 0))
    full = lambda a: pl.BlockSpec(a.shape, lambda i: (0, 0))
    pairs = pl.BlockSpec((AT_PAIRS, PLANES, ts // PLANES, LANES), lambda i: (0, 0, i, 0))
    return pl.pallas_call(
        body, name="ln_proj", grid=(s // ts,), compiler_params=_params("arbitrary"),
        in_specs=[tok(D_MODEL), full(mod), full(norm_w), tok(LANES), tok(LANES), full(conv_w8), full(alog_row),
                  full(dtb_row)] + [full(w) for w in ws],
        out_specs=(tok(D_MODEL), tok(widths[0]), tok(widths[1]), tok(widths[2]), pairs, pairs, pairs,
                   tok(widths[6]), tok(DN_WIDTH), tok(DN_WIDTH), tok(DN_WIDTH), tok(BA_PAD)),
        out_shape=(SDS((s, D_MODEL), BF16), SDS((s, widths[0]), F32), SDS((s, widths[1]), F32),
                   SDS((s, widths[2]), F32)) + (SDS((AT_PAIRS, PLANES, s // PLANES, LANES), F32),) * 3 + (SDS((s, widths[6]), F32),)
        + (SDS((s, DN_WIDTH), F32),) * 3 + (SDS((s, BA_PAD), F32),),
        scratch_shapes=[pltpu.VMEM((8, widths[0]), F32)],
    )(x, mod, norm_w, cos_t, sin_t, conv_w8, alog_row, dtb_row, *ws)


def _conv_taps(ext, rows):
    taps = []
    for j in range(CONV_K):
        sh = CONV_K - 1 - j
        rolled = pltpu.roll(ext, sh, 0) if sh else ext
        taps.append(rolled[8:8 + rows])
    return taps


def _dn_forward(q, k, v, bg):
    s = q.shape[0]
    tp = CH_UNROLL * CHUNK
    npass = s // tp
    hs = range(DN_HEADS)
    sl = [slice(h * DN_DIM, (h + 1) * DN_DIM) for h in hs]

    def body(q_ref, k_ref, v_ref, bg_ref, w_ref, qd_ref, kd_ref, p_ref, gl_ref, t_ref, o_ref, vn_ref, st_ref,
             state, u_s, w_s, qd_s, kd_s, p_s, gl_s):
        @pl.when(pl.program_id(0) == 0)
        def _():
            for ref in (state, u_s, w_s, qd_s, kd_s, p_s, gl_s):
                ref[...] = jnp.zeros_like(ref)

        def recurrence():
            for c in range(CH_UNROLL):
                rows = slice(c * CHUNK, (c + 1) * CHUNK)
                rows8 = slice(c * 8, (c + 1) * 8)
                srows = slice(c * DN_DIM, (c + 1) * DN_DIM)
                sf = [state[h] for h in hs]
                sb = [_bf(x) for x in sf]
                ws = [_nn(w_s[rows, cl], b) for cl, b in zip(sl, sb)]
                qs = [_nn(qd_s[rows, cl], b) for cl, b in zip(sl, sb)]
                yield
                vn = [u_s[rows, cl] - x for cl, x in zip(sl, ws)]
                vb = [_bf(x) for x in vn]
                kv = [_tn(kd_s[rows, cl], b) for cl, b in zip(sl, vb)]
                pv = [_nn(p_s[h, rows, :], b) for h, b in zip(hs, vb)]
                for h in hs:
                    state[h] = sf[h] * gl_s[rows8, sl[h]][0:1] + kv[h]
                for h in hs:
                    st_ref[srows, sl[h]] = sf[h]
                    vn_ref[rows, sl[h]] = vn[h]
                    o_ref[rows, sl[h]] = qs[h] + pv[h]
                yield

        steps = recurrence()

        where = [(slice(c * CHUNK, (c + 1) * CHUNK), slice(c * 8, (c + 1) * 8), h, sl[h])
                 for c in range(CH_UNROLL) for h in hs]
        bgs = [bg_ref[rows, :] for rows, _, _, _ in where]
        outs = _chunk_fwd([q_ref[rows, cl] for rows, _, _, cl in where], [k_ref[rows, cl] for rows, _, _, cl in where],
                          [v_ref[rows, cl] for rows, _, _, cl in where],
                          [b[:, h:h + 1] for b, (_, _, h, _) in zip(bgs, where)],
                          [b[:, GC_LANE + h:GC_LANE + h + 1] for b, (_, _, h, _) in zip(bgs, where)],
                          tick=lambda: next(steps, None))
        for _ in steps:
            pass
        for (rows, rows8, h, cl), (u, w, p, qd, kd, gl, t) in zip(where, outs):
            g8 = jnp.broadcast_to(gl, (8, DN_DIM))
            u_s[rows, cl] = u
            w_ref[rows, cl] = w
            w_s[rows, cl] = w
            qd_ref[rows, cl] = qd
            qd_s[rows, cl] = qd
            kd_ref[rows, cl] = kd
            kd_s[rows, cl] = kd
            p_ref[h, rows, :] = p
            p_s[h, rows, :] = p
            gl_ref[rows8, cl] = g8
            gl_s[rows8, cl] = g8
            t_ref[h, rows, :] = t

    cur = lambda i: jnp.minimum(i, npass - 1)
    done = lambda i: jnp.maximum(i - 1, 0)
    tokc = pl.BlockSpec((tp, DN_WIDTH), lambda i: (cur(i), 0))
    tokd = pl.BlockSpec((tp, DN_WIDTH), lambda i: (done(i), 0))
    sq = pl.BlockSpec((DN_HEADS, tp, CHUNK), lambda i: (0, cur(i), 0))
    return pl.pallas_call(
        body, name="dn_forward", grid=(npass + 1,), compiler_params=_params("arbitrary"),
        in_specs=[tokc] * 3 + [pl.BlockSpec((tp, BA_PAD), lambda i: (cur(i), 0))],
        out_specs=(tokc, tokc, tokc, sq, pl.BlockSpec((CH_UNROLL * 8, DN_WIDTH), lambda i: (cur(i), 0)), sq,
                   tokd, tokd, pl.BlockSpec((CH_UNROLL * DN_DIM, DN_WIDTH), lambda i: (done(i), 0))),
        out_shape=(SDS((s, DN_WIDTH), F32),) * 3 + (SDS((DN_HEADS, s, CHUNK), F32),
                                                     SDS((s // CHUNK * 8, DN_WIDTH), F32),
                                                     SDS((DN_HEADS, s, CHUNK), F32),
                                                     SDS((s, DN_WIDTH), F32), SDS((s, DN_WIDTH), F32),
                                                     SDS((s // CHUNK * DN_DIM, DN_WIDTH), F32)),
        scratch_shapes=[pltpu.VMEM((DN_HEADS, DN_DIM, DN_DIM), F32)] + [pltpu.VMEM((tp, DN_WIDTH), F32)] * 4
        + [pltpu.VMEM((DN_HEADS, tp, CHUNK), F32), pltpu.VMEM((CH_UNROLL * 8, DN_WIDTH), F32)],
    )(q, k, v, bg)


LOG2E, LN2 = 1.4426950408889634, 0.6931471805599453
MASKED = -1e30


PLANE_ROWS = ATT_BLK // PLANES


PLANE_PITCH = 24


def _to_planes(tile, scr):
    n = tile.shape[0] // PLANES
    for i in range(n):
        scr[i * PLANE_PITCH:i * PLANE_PITCH + PLANES, :] = tile[i * PLANES:(i + 1) * PLANES]
    return [scr[pl.ds(r, n, stride=PLANE_PITCH), :] for r in range(PLANES)]


def _from_planes(planes, scr):
    n = planes[0].shape[0]
    for r in range(PLANES):
        scr[pl.ds(r, n, stride=PLANE_PITCH), :] = planes[r]
    return jnp.concatenate([scr[i * PLANE_PITCH:i * PLANE_PITCH + PLANES, :] for i in range(n)], axis=0)


def _plane_perm(n, back):
    row = lax.broadcasted_iota(jnp.int32, (n, n), 0)
    col = lax.broadcasted_iota(jnp.int32, (n, n), 1)
    m, c = (col, row) if back else (row, col)
    return jnp.where(c == PLANES * (m % (n // PLANES)) + m // (n // PLANES), 1.0, 0.0).astype(BF16)


def _geom(d):
    nchunk = PLANES // d
    return nchunk, Q_BLOCK // nchunk


def _pattern_bias(d):
    nchunk, qlen = _geom(d)
    row = lax.broadcasted_iota(jnp.int32, (Q_BLOCK, 2 * Q_BLOCK), 0)
    col = lax.broadcasted_iota(jnp.int32, (Q_BLOCK, 2 * Q_BLOCK), 1)
    uq, aq = row // qlen, row % qlen
    uk, ak = col // (2 * qlen), col % (2 * qlen)
    rel = nchunk * (aq - ak + qlen) + (uq - uk)
    band = jnp.where((rel >= 0) & (rel <= W_SUB), 0.0, MASKED)
    col1 = lax.broadcasted_iota(jnp.int32, (1, 2 * Q_BLOCK), 1)
    return band, (col1 % (2 * qlen)) < qlen


def _aligned(start):
    return start if isinstance(start, int) else pl.multiple_of(start, 8)


def _keys(prev_ref, cur_ref, planes, mm, ql):
    parts = []
    for p in planes:
        if isinstance(mm, int) and mm == 0:
            parts += [prev_ref[0, p, PLANE_ROWS - ql:PLANE_ROWS, :], cur_ref[0, p, 0:ql, :]]
        else:
            parts.append(cur_ref[0, p, pl.ds(_aligned(ql * (mm - 1)), 2 * ql), :])
    return jnp.concatenate(parts, axis=0)


def _gather(ref, lead, planes, start, n):
    parts = [ref[lead + (p, pl.ds(start, n), slice(None))] for p in planes]
    return parts[0] if len(parts) == 1 else jnp.concatenate(parts, axis=0)


def _scatter(ref, lead, planes, start, n, val, add):
    for u, p in enumerate(planes):
        idx = lead + (p, pl.ds(start, n), slice(None))
        if add:
            ref[idx] += val[u * n:(u + 1) * n]
        else:
            ref[idx] = val[u * n:(u + 1) * n]


def _attn_fwd(qr, kr, vv):
    s16 = qr.shape[2]
    nblk = s16 // PLANE_ROWS
    scale = AT_DIM ** -0.5
    npat = len(DILATIONS)

    def body(q_ref, kp_ref, k_ref, vp_ref, v_ref, o_ref, lse_ref, o_p, l_p):
        n = pl.program_id(1)
        lo = lax.broadcasted_iota(jnp.int32, (Q_BLOCK, LANES), 1) < AT_DIM
        nq = ATT_BLK // Q_BLOCK
        heads = [(i, sel) for i in range(nq) for sel in (lo, ~lo)]
        for pi, d in enumerate(DILATIONS):
            band, prev_cols = _pattern_bias(d)
            nchunk, ql = _geom(d)
            cs = [([c % d + d * u for u in range(nchunk)], c // d) for c in range(nq)]
            band0 = band + jnp.where(prev_cols & (n == 0), MASKED, 0.0)
            bias = [band0 if mm == 0 else band for _, mm in cs]
            qb = [_bf(_gather(q_ref, (0,), pls, ql * mm, ql)) for pls, mm in cs]
            kk = [_bf(_keys(kp_ref, k_ref, pls, mm, ql)) for pls, mm in cs]
            vb = [_bf(_keys(vp_ref, v_ref, pls, mm, ql)) for pls, mm in cs]
            sc = [lax.dot_general(jnp.where(sel, qb[i], jnp.zeros_like(qb[i])), kk[i], _NT,
                                  preferred_element_type=F32) for i, sel in heads]
            sc = [x * (scale * LOG2E) + bias[i] for x, (i, _) in zip(sc, heads)]
            mx = [jnp.max(x, axis=-1, keepdims=True) for x in sc]
            pr = [jnp.exp2(x - m) for x, m in zip(sc, mx)]
            ls = [jnp.sum(x, axis=-1, keepdims=True) for x in pr]
            pv = [jnp.dot(_bf(x), vb[i], preferred_element_type=F32) for x, (i, _) in zip(pr, heads)]
            outs = [x / l for x, l in zip(pv, ls)]
            lses = [m * LN2 + jnp.log(l) for m, l in zip(mx, ls)]
            for i, (pls, mm) in enumerate(cs):
                _scatter(o_p, (pi,), pls, ql * mm, ql, jnp.where(lo, outs[2 * i], outs[2 * i + 1]), False)
                _scatter(l_p, (pi,), pls, ql * mm, ql, jnp.where(lo, lses[2 * i], lses[2 * i + 1]), False)

        def merge(r, carry):
            ls = [l_p[pi, r] for pi in range(npat)]
            mx = jnp.maximum(jnp.maximum(ls[0], ls[1]), ls[2])
            es = [jnp.exp(l - mx) for l in ls]
            den = es[0] + es[1] + es[2]
            o_ref[0, r] = (es[0] * o_p[0, r] + es[1] * o_p[1, r] + es[2] * o_p[2, r]) / den
            lse_ref[0, r] = mx + jnp.log(den)
            return carry

        lax.fori_loop(0, PLANES, merge, 0)

    blk = pl.BlockSpec((1, PLANES, PLANE_ROWS, LANES), lambda j, n: (j, 0, n, 0))
    prev = pl.BlockSpec((1, PLANES, PLANE_ROWS, LANES), lambda j, n: (j, 0, jnp.maximum(n - 1, 0), 0))
    return pl.pallas_call(
        body, name="attn_fwd", grid=(AT_PAIRS, nblk), compiler_params=_params("arbitrary", "arbitrary"),
        in_specs=[blk, prev, blk, prev, blk], out_specs=(blk, blk),
        out_shape=(SDS(qr.shape, F32),) * 2,
        scratch_shapes=[pltpu.VMEM((npat, PLANES, PLANE_ROWS, LANES), F32)] * 2,
    )(qr, kr, kr, vv, vv)


def _out_loss(o_dn, z_dn, o_at, z_at, dnw, atw2, x, tgt, w_out, gate, fw, ts):
    s = x.shape[0]

    def body(odn, zdn, oat, zat, dnw_ref, atw_ref, x_ref, t_ref, w_ref, g_ref, fw_ref,
             dx2_ref, gw_ref, dfw_ref, dgate_ref, loss_ref, dodn, dzdn, doat, dzat, delta, ddnw, datw, perm):
        @pl.when(pl.program_id(0) == 0)
        def _():
            for ref in (gw_ref, dfw_ref, dgate_ref, loss_ref, ddnw, datw):
                ref[...] = jnp.zeros_like(ref)

        parts, vjps = [], []
        for h in range(DN_HEADS):
            cols = slice(h * DN_DIM, (h + 1) * DN_DIM)
            y, vjp = jax.vjp(_gate_dn, odn[:, cols], zdn[:, cols], dnw_ref[...])
            parts.append(_bf(y))
            vjps.append(vjp)
        oats = [_from_planes([oat[j, r] for r in range(PLANES)], perm.at[j]) for j in range(AT_PAIRS)]
        for j in range(AT_PAIRS):
            y, vjp = jax.vjp(functools.partial(_gate_at, head_sum=_d_head_sum), oats[j],
                             zat[:, j * LANES:(j + 1) * LANES], atw_ref[...])
            parts.append(_bf(y))
            vjps.append(vjp)
        catb = jnp.concatenate(parts, axis=1)
        wb = w_ref[...]
        gate, fwv = g_ref[...], fw_ref[...]
        mix = jnp.dot(catb, wb, preferred_element_type=F32)
        x2 = x_ref[...] + gate * mix
        r2 = lax.rsqrt(jnp.mean(x2 * x2, axis=-1, keepdims=True) + EPS)
        xn2 = x2 * r2
        err = xn2 * fwv - t_ref[...]
        row = jnp.sum(err * err, axis=-1, keepdims=True) * (1.0 / D_MODEL)
        loss_ref[...] += 0.5 * jnp.sum(row, axis=0, keepdims=True)
        dy = err * (1.0 / D_MODEL)
        dfw_ref[...] += jnp.sum(dy * xn2, axis=0, keepdims=True)
        dxn = dy * fwv
        dx2 = r2 * (dxn - xn2 * jnp.mean(dxn * xn2, axis=-1, keepdims=True))
        dx2_ref[...] = dx2
        dgate_ref[...] += jnp.sum(dx2 * mix, axis=0, keepdims=True)
        dmix = _bf(gate * dx2)
        dcat = lax.dot_general(dmix, wb, _NT, preferred_element_type=F32)
        gw_ref[...] += lax.dot_general(catb, dmix, _TN, preferred_element_type=F32)
        for h in range(DN_HEADS):
            cols = slice(h * DN_DIM, (h + 1) * DN_DIM)
            do, dz, dw = vjps[h](dcat[:, cols])
            dodn[:, cols] = do
            dzdn[:, cols] = _bf(dz)
            ddnw[...] += dw
        for j in range(AT_PAIRS):
            cols = slice(j * LANES, (j + 1) * LANES)
            do, dz, dw = vjps[DN_HEADS + j](dcat[:, DN_WIDTH + j * LANES:DN_WIDTH + (j + 1) * LANES])
            for r, x in enumerate(_to_planes(do, perm.at[j])):
                doat[j, r] = x
            dzat[:, cols] = _bf(dz)
            datw[...] += dw
            for r, x in enumerate(_to_planes(_head_sum(do * oats[j]), perm.at[j])):
                delta[j, r] = x

    tok = lambda w: pl.BlockSpec((ts, w), lambda i: (i, 0))
    full = lambda a: pl.BlockSpec(a.shape, lambda i: (0, 0))
    row = pl.BlockSpec((1, D_MODEL), lambda i: (0, 0))
    lrow = pl.BlockSpec((1, LANES), lambda i: (0, 0))
    pairs = pl.BlockSpec((AT_PAIRS, PLANES, ts // PLANES, LANES), lambda i: (0, 0, i, 0))
    return pl.pallas_call(
        body, name="out_loss", grid=(s // ts,), compiler_params=_params("arbitrary"),
        in_specs=[tok(DN_WIDTH), tok(DN_WIDTH), pairs, tok(AT_WIDTH), full(dnw), full(atw2),
                  tok(D_MODEL), tok(D_MODEL), full(w_out), full(gate), full(fw)],
        out_specs=(tok(D_MODEL), pl.BlockSpec((D_MODEL, D_MODEL), lambda i: (0, 0)), row, row,
                   pl.BlockSpec((1, 1), lambda i: (0, 0)), tok(DN_WIDTH), tok(DN_WIDTH), pairs, tok(AT_WIDTH), pairs,
                   lrow, lrow),
        out_shape=(SDS((s, D_MODEL), F32), SDS((D_MODEL, D_MODEL), F32), SDS((1, D_MODEL), F32),
                   SDS((1, D_MODEL), F32), SDS((1, 1), F32), SDS((s, DN_WIDTH), F32), SDS((s, DN_WIDTH), BF16),
                   SDS((AT_PAIRS, PLANES, s // PLANES, LANES), F32), SDS((s, AT_WIDTH), BF16),
                   SDS((AT_PAIRS, PLANES, s // PLANES, LANES), F32), SDS((1, LANES), F32), SDS((1, LANES), F32)),
        scratch_shapes=[pltpu.VMEM((AT_PAIRS, ts // PLANES * PLANE_PITCH, LANES), F32)],
    )(o_dn, z_dn, o_at, z_at, dnw, atw2, x, tgt, w_out, gate, fw)


def _hand_over(out_ref, acc, n):
    zeros = jnp.zeros((PLANES, PLANE_ROWS, LANES), F32)

    @pl.when(n == 0)
    def _():
        out_ref[0] = zeros

    @pl.when(n > 0)
    def _():
        out_ref[0] = acc[...]

    acc[...] = zeros


def _add_keys(out_ref, acc, planes, mm, ql, val):
    for u, p in enumerate(planes):
        part = val[u * 2 * ql:(u + 1) * 2 * ql]
        if isinstance(mm, int) and mm == 0:
            out_ref[0, p, PLANE_ROWS - ql:PLANE_ROWS, :] += part[:ql]
            acc[p, 0:ql, :] += part[ql:]
        else:
            acc[p, pl.ds(_aligned(ql * (mm - 1)), 2 * ql), :] += part


def _attn_bwd(qr, kr, vv, do, lse, delta):
    s16 = qr.shape[2]
    nblk = s16 // PLANE_ROWS
    scale = AT_DIM ** -0.5

    nu = ATT_UNROLL_BWD
    npass = ATT_BLK // Q_BLOCK // nu

    def blocks(g, d):
        nchunk, ql = _geom(d)
        out = []
        for u in range(nu):
            r0, mm = (u % d, g * (nu // d) + u // d) if nu % d == 0 else (g * nu + u, 0)
            out.append(([r0 + d * c for c in range(nchunk)], _aligned(ql * mm), ql, mm))
        return out

    def body(q_ref, kp_ref, k_ref, vp_ref, v_ref, do_ref, lse_ref, dl_ref, dq_ref, dk_ref, dv_ref, dk_acc, dv_acc):
        n = pl.program_id(1)
        _hand_over(dk_ref, dk_acc, n)
        _hand_over(dv_ref, dv_acc, n)

        @pl.when(n < nblk)
        def _():
            lo = lax.broadcasted_iota(jnp.int32, (Q_BLOCK, LANES), 1) < AT_DIM
            for d in DILATIONS:
                band, prev_cols = _pattern_bias(d)
                band0 = band + jnp.where(prev_cols & (n == 0), MASKED, 0.0)

                def group(g, carry, d=d, band=band, band0=band0):
                    cs = blocks(g, d)
                    heads = [(i, sel) for i in range(nu) for sel in (lo, ~lo)]
                    bias = [band0 if isinstance(mm, int) and mm == 0 else band for _, _, _, mm in cs]
                    qb = [_bf(_gather(q_ref, (0,), pls, qs, ql)) for pls, qs, ql, _ in cs]
                    dob = [_bf(_gather(do_ref, (0,), pls, qs, ql)) for pls, qs, ql, _ in cs]
                    kk = [_bf(_keys(kp_ref, k_ref, pls, mm, ql)) for pls, _, ql, mm in cs]
                    vb = [_bf(_keys(vp_ref, v_ref, pls, mm, ql)) for pls, _, ql, mm in cs]
                    lse2 = [_gather(lse_ref, (0,), pls, qs, ql) * LOG2E for pls, qs, ql, _ in cs]
                    dl2 = [_gather(dl_ref, (0,), pls, qs, ql) for pls, qs, ql, _ in cs]
                    qm = [jnp.where(sel, qb[i], jnp.zeros_like(qb[i])) for i, sel in heads]
                    dom = [jnp.where(sel, dob[i], jnp.zeros_like(dob[i])) for i, sel in heads]
                    lse_c = [jnp.max(jnp.where(sel, lse2[i], -jnp.inf), axis=-1, keepdims=True) for i, sel in heads]
                    dl_c = [jnp.max(jnp.where(sel, dl2[i], -jnp.inf), axis=-1, keepdims=True) for i, sel in heads]
                    sc = [lax.dot_general(a, kk[i], _NT, preferred_element_type=F32) for a, (i, _) in zip(qm, heads)]
                    dp = [lax.dot_general(a, vb[i], _NT, preferred_element_type=F32) for a, (i, _) in zip(dom, heads)]
                    pr = [jnp.exp2(x * (scale * LOG2E) + bias[i] - l) for x, l, (i, _) in zip(sc, lse_c, heads)]
                    ds = [_bf(p * (x - dl) * scale) for p, x, dl in zip(pr, dp, dl_c)]
                    prb = [_bf(p) for p in pr]
                    dq = [jnp.dot(x, kk[i], preferred_element_type=F32) for x, (i, _) in zip(ds, heads)]
                    dk = [lax.dot_general(x, a, _TN, preferred_element_type=F32) for x, a in zip(ds, qm)]
                    dv = [lax.dot_general(x, a, _TN, preferred_element_type=F32) for x, a in zip(prb, dom)]
                    for i, (pls, qs, ql, mm) in enumerate(cs):
                        _scatter(dq_ref, (0,), pls, qs, ql, jnp.where(lo, dq[2 * i], dq[2 * i + 1]), d != DILATIONS[0])
                        _add_keys(dk_ref, dk_acc, pls, mm, ql, dk[2 * i] + dk[2 * i + 1])
                        _add_keys(dv_ref, dv_acc, pls, mm, ql, dv[2 * i] + dv[2 * i + 1])
                    return carry

                if nu % d == 0:
                    group(0, 0)
                    lax.fori_loop(1, npass, group, 0)
                else:
                    lax.fori_loop(0, npass, group, 0)

    at = lambda f: pl.BlockSpec((1, PLANES, PLANE_ROWS, LANES), lambda j, n: (j, 0, f(n), 0))
    cur = at(lambda n: jnp.minimum(n, nblk - 1))
    prev = at(lambda n: jnp.maximum(jnp.minimum(n, nblk - 1) - 1, 0))
    done = at(lambda n: jnp.maximum(n - 1, 0))
    return pl.pallas_call(
        body, name="attn_bwd", grid=(AT_PAIRS, nblk + 1), compiler_params=_params("arbitrary", "arbitrary"),
        in_specs=[cur, prev, cur, prev, cur, cur, cur, cur], out_specs=(cur, done, done),
        out_shape=(SDS(qr.shape, F32),) * 3,
        scratch_shapes=[pltpu.VMEM((PLANES, PLANE_ROWS, LANES), F32)] * 2,
    )(qr, kr, kr, vv, vv, do, lse, delta)


def _dn_backward(do, st, vn, w, qd, kd, p, gl, q, k, v, bg, t):
    s = do.shape[0]
    nc = CH_UNROLL_BWD
    tp = nc * CHUNK
    npass = s // tp
    hs = range(DN_HEADS)
    sl = [slice(h * DN_DIM, (h + 1) * DN_DIM) for h in hs]

    def body(do_ref, st_ref, vn_ref, w_ref, qd_ref, kd_ref, p_ref, gl_ref, q_ref, k_ref, v_ref, bg_ref, t_ref,
             dq_ref, dk_ref, dv_ref, dbg_ref, dstate, du_s, dw_s, dqd_s, dkd_s, dp_s, dgl_s):
        @pl.when(pl.program_id(0) == 0)
        def _():
            for ref in (dstate, du_s, dw_s, dqd_s, dkd_s, dp_s, dgl_s):
                ref[...] = jnp.zeros_like(ref)

        where = [(slice(c * CHUNK, (c + 1) * CHUNK), slice(c * 8, (c + 1) * 8), h, sl[h])
                 for c in range(nc) for h in hs]
        cots = [(du_s[rows, cl], dw_s[rows, cl], dp_s[h, rows, :], dqd_s[rows, cl], dkd_s[rows, cl],
                 dgl_s[rows8, cl][0:1, 0:1]) for rows, rows8, h, cl in where]

        def recurrence():
            for c in reversed(range(nc)):
                rows = slice(c * CHUNK, (c + 1) * CHUNK)
                rows8 = slice(c * 8, (c + 1) * 8)
                srows = slice(c * DN_DIM, (c + 1) * DN_DIM)
                ds_ = [dstate[h] for h in hs]
                dsb = [_bf(x) for x in ds_]
                dob = [_bf(do_ref[rows, cl]) for cl in sl]
                pdo = [_tn(p_ref[h, rows, :], b) for h, b in zip(hs, dob)]
                qdo = [_tn(qd_ref[rows, cl], b) for cl, b in zip(sl, dob)]
                kds = [_nn(kd_ref[rows, cl], b) for cl, b in zip(sl, dsb)]
                yield
                dvn = [a + b for a, b in zip(kds, pdo)]
                dvb = [_bf(x) for x in dvn]
                wdv = [_tn(w_ref[rows, cl], b) for cl, b in zip(sl, dvb)]
                for h in hs:
                    dstate[h] = ds_[h] * gl_ref[rows8, sl[h]][0:1] + qdo[h] - wdv[h]
                sfs = [st_ref[srows, cl] for cl in sl]
                sbs = [_bf(x) for x in sfs]
                vnb = [_bf(vn_ref[rows, cl]) for cl in sl]
                for h in hs:
                    du_s[rows, sl[h]] = dvn[h]
                    dw_s[rows, sl[h]] = -_nt(dvb[h], sbs[h])
                    dqd_s[rows, sl[h]] = _nt(dob[h], sbs[h])
                    dkd_s[rows, sl[h]] = _nt(vnb[h], dsb[h])
                    dp_s[h, rows, :] = _nt(dob[h], vnb[h])
                    dgl = jnp.sum(jnp.sum(ds_[h] * sfs[h], axis=1, keepdims=True), axis=0, keepdims=True)
                    dgl_s[rows8, sl[h]] = jnp.broadcast_to(dgl, (8, DN_DIM))
                yield

        steps = recurrence()

        bgs = [bg_ref[rows, :] for rows, _, _, _ in where]
        outs = _chunk_bwd([q_ref[rows, cl] for rows, _, _, cl in where], [k_ref[rows, cl] for rows, _, _, cl in where],
                          [v_ref[rows, cl] for rows, _, _, cl in where],
                          [b[:, h:h + 1] for b, (_, _, h, _) in zip(bgs, where)],
                          [b[:, GC_LANE + h:GC_LANE + h + 1] for b, (_, _, h, _) in zip(bgs, where)],
                          [t_ref[h, rows, :] for rows, _, h, _ in where], cots, tick=lambda: next(steps, None))
        for _ in steps:
            pass
        lane = lax.broadcasted_iota(jnp.int32, (CHUNK, BA_PAD), 1)
        for c in range(nc):
            dbg = jnp.zeros((CHUNK, BA_PAD), F32)
            for (rows, _, h, cl), (dq, dk, dv, dbeta, dgc) in list(zip(where, outs))[c * DN_HEADS:(c + 1) * DN_HEADS]:
                dq_ref[rows, cl] = dq
                dk_ref[rows, cl] = dk
                dv_ref[rows, cl] = dv
                dbg = dbg + jnp.where(lane == h, dbeta, 0.0) + jnp.where(lane == GC_LANE + h, dgc, 0.0)
            dbg_ref[where[c * DN_HEADS][0], :] = dbg

    rec = lambda i: jnp.maximum(npass - 1 - i, 0)
    loc = lambda i: jnp.minimum(npass - i, npass - 1)
    tok_r = pl.BlockSpec((tp, DN_WIDTH), lambda i: (rec(i), 0))
    tok_l = pl.BlockSpec((tp, DN_WIDTH), lambda i: (loc(i), 0))
    sq_r = pl.BlockSpec((DN_HEADS, tp, CHUNK), lambda i: (0, rec(i), 0))
    sq_l = pl.BlockSpec((DN_HEADS, tp, CHUNK), lambda i: (0, loc(i), 0))
    ba_l = pl.BlockSpec((tp, BA_PAD), lambda i: (loc(i), 0))
    return pl.pallas_call(
        body, name="dn_backward", grid=(npass + 1,), compiler_params=_params("arbitrary"),
        in_specs=[tok_r, pl.BlockSpec((nc * DN_DIM, DN_WIDTH), lambda i: (rec(i), 0)), tok_r, tok_r, tok_r, tok_r,
                  sq_r, pl.BlockSpec((nc * 8, DN_WIDTH), lambda i: (rec(i), 0)),
                  tok_l, tok_l, tok_l, ba_l, sq_l],
        out_specs=(tok_l, tok_l, tok_l, ba_l),
        out_shape=(SDS((s, DN_WIDTH), F32),) * 3 + (SDS((s, BA_PAD), F32),),
        scratch_shapes=[pltpu.VMEM((DN_HEADS, DN_DIM, DN_DIM), F32)] + [pltpu.VMEM((tp, DN_WIDTH), F32)] * 4
        + [pltpu.VMEM((DN_HEADS, tp, CHUNK), F32), pltpu.VMEM((nc * 8, DN_WIDTH), F32)],
    )(do, st, vn, w, qd, kd, p, gl, q, k, v, bg, t)


def _dn_prep_bwd(qkv_pre, ba, dq, dk, dv, dbg, conv_w8, alog_row, dtb_row, hbf, dz_dn, ts):
    s = qkv_pre.shape[0]
    cw = 3 * DN_WIDTH
    nt = s // ts

    def body(pre_ref, ph_ref, nh_ref, ba_ref, dq_ref, dqh_ref, dk_ref, dkh_ref, dv_ref, dvh_ref, dbg_ref,
             cw_ref, al_ref, dtb_ref, h_ref, dz_ref, dpre_ref, dba_ref, dcw_ref, dal_ref, ddtb_ref,
             gqkv_out, gz_out, gba_out, gqkv_ref, gz_ref, gba_ref):
        n = pl.program_id(0)

        @pl.when(n == 0)
        def _():
            gqkv_ref[...] = jnp.zeros_like(gqkv_ref)
            gz_ref[...] = jnp.zeros_like(gz_ref)
            gba_ref[...] = jnp.zeros_like(gba_ref)
            dcw_ref[...] = jnp.zeros_like(dcw_ref)
            dal_ref[...] = jnp.zeros_like(dal_ref)
            ddtb_ref[...] = jnp.zeros_like(ddtb_ref)

        hb = h_ref[...]
        gz_ref[...] += lax.dot_general(hb, dz_ref[...], _TN, preferred_element_type=F32)
        last = n == nt - 1
        prev = jnp.where(n == 0, 0.0, ph_ref[...])
        ext = jnp.concatenate([prev, pre_ref[...], nh_ref[...]], axis=0)
        taps = _conv_taps(ext, ts + 8)
        conv = taps[0] * cw_ref[0:1, :]
        for j in range(1, CONV_K):
            conv = conv + taps[j] * cw_ref[j:j + 1, :]

        def cot(main, halo, cols):
            return jnp.concatenate([main[:, cols], jnp.where(last, 0.0, halo[:, cols])], axis=0)

        rows = ts + 8
        for grp, (fn, mref, href) in enumerate(((_post_q, dq_ref, dqh_ref), (_post_k, dk_ref, dkh_ref),
                                                (_post_v, dv_ref, dvh_ref))):
            gcols = slice(grp * DN_WIDTH, (grp + 1) * DN_WIDTH)
            pieces = []
            for h in range(DN_HEADS):
                cols = slice(h * DN_DIM, (h + 1) * DN_DIM)
                c0 = grp * DN_WIDTH + h * DN_DIM
                _, vjp = jax.vjp(fn, conv[:, c0:c0 + DN_DIM])
                pieces.append(vjp(cot(mref, href, cols))[0])
            dconv = jnp.concatenate(pieces, axis=1)
            dpre = dconv[:ts] * cw_ref[CONV_K - 1:CONV_K, gcols]
            for j in range(CONV_K - 1):
                sh = CONV_K - 1 - j
                dpre = dpre + pltpu.roll(dconv, rows - sh, 0)[:ts] * cw_ref[j:j + 1, gcols]
            dpre_b = _bf(dpre)
            dpre_ref[:, gcols] = dpre_b
            gqkv_ref[:, gcols] += lax.dot_general(hb, dpre_b, _TN, preferred_element_type=F32)
            for j in range(CONV_K):
                dcw_ref[j:j + 1, gcols] += jnp.sum(dconv[:ts] * taps[j][:ts, gcols], axis=0, keepdims=True)

        dbg = dbg_ref[...]
        lane = lax.broadcasted_iota(jnp.int32, dbg.shape, 1)
        dg = pltpu.roll(_chunk_cumsum(dbg, reverse=True), BA_PAD - DN_HEADS, 1)
        cot_bg = jnp.where(lane < DN_HEADS, dbg, jnp.where(lane < GC_LANE, dg, 0.0))
        _, vjp = jax.vjp(_beta_decay, ba_ref[...], al_ref[...], dtb_ref[...])
        dba, dal, ddtb = vjp(cot_bg)
        dba_b = _bf(dba)
        dba_ref[...] = dba_b
        gba_ref[...] += lax.dot_general(hb, dba_b, _TN, preferred_element_type=F32)
        dal_ref[...] += dal
        ddtb_ref[...] += ddtb

        @pl.when(last)
        def _():
            gqkv_out[...] = _bf(gqkv_ref[...])
            gz_out[...] = _bf(gz_ref[...])
            gba_out[...] = _bf(gba_ref[...])

    tok = lambda w: pl.BlockSpec((ts, w), lambda i: (i, 0))
    full = lambda a: pl.BlockSpec(a.shape, lambda i: (0, 0))
    prevh = lambda w: pl.BlockSpec((8, w), lambda i: (jnp.maximum(i * (ts // 8) - 1, 0), 0))
    nexth = lambda w: pl.BlockSpec((8, w), lambda i: (jnp.minimum((i + 1) * (ts // 8), s // 8 - 1), 0))
    row = pl.BlockSpec((1, LANES), lambda i: (0, 0))
    return pl.pallas_call(
        body, name="dn_prep_bwd", grid=(nt,), compiler_params=_params("arbitrary"),
        in_specs=[tok(cw), prevh(cw), nexth(cw), tok(BA_PAD),
                  tok(DN_WIDTH), nexth(DN_WIDTH), tok(DN_WIDTH), nexth(DN_WIDTH), tok(DN_WIDTH), nexth(DN_WIDTH),
                  tok(BA_PAD), full(conv_w8), full(alog_row), full(dtb_row), tok(D_MODEL), tok(DN_WIDTH)],
        out_specs=(tok(cw), tok(BA_PAD), pl.BlockSpec((8, cw), lambda i: (0, 0)), row, row)
        + tuple(pl.BlockSpec((D_MODEL, w), lambda i: (0, 0)) for w in (cw, DN_WIDTH, BA_PAD)),
        out_shape=(SDS((s, cw), BF16), SDS((s, BA_PAD), BF16), SDS((8, cw), F32), SDS((1, LANES), F32),
                   SDS((1, LANES), F32)) + tuple(SDS((D_MODEL, w), BF16) for w in (cw, DN_WIDTH, BA_PAD)),
        scratch_shapes=[pltpu.VMEM((D_MODEL, w), F32) for w in (cw, DN_WIDTH, BA_PAD)],
    )(qkv_pre, qkv_pre, qkv_pre, ba, dq, dq, dk, dk, dv, dv, dbg, conv_w8, alog_row, dtb_row, hbf, dz_dn)


def _dh_dx(dps, ws, x, mod, norm_w, dx2, ts):
    s = x.shape[0]
    widths = [w.shape[1] for w in ws]
    np_ = len(ws)

    def body(*refs):
        dp_refs, w_refs = refs[:np_], refs[np_:2 * np_]
        x_ref, mod_ref, nw_ref, dx2_ref, gx_ref, dshift, dscale, dnw = refs[2 * np_:]

        @pl.when(pl.program_id(0) == 0)
        def _():
            dshift[...] = jnp.zeros_like(dshift)
            dscale[...] = jnp.zeros_like(dscale)
            dnw[...] = jnp.zeros_like(dnw)

        dh = lax.dot_general(dp_refs[0][...], w_refs[0][...], _NT, preferred_element_type=F32)
        for a, b in zip(dp_refs[1:], w_refs[1:]):
            dh = dh + lax.dot_general(a[...], b[...], _NT, preferred_element_type=F32)
        xt = x_ref[...]
        r = lax.rsqrt(jnp.mean(xt * xt, axis=-1, keepdims=True) + EPS)
        xn = xt * r
        nw = nw_ref[...]
        sc1 = 1.0 + mod_ref[:, D_MODEL:2 * D_MODEL]
        dshift[...] += jnp.sum(dh, axis=0, keepdims=True)
        dscale[...] += jnp.sum(dh * (xn * nw), axis=0, keepdims=True)
        dnw[...] += jnp.sum(dh * sc1 * xn, axis=0, keepdims=True)
        dxn = dh * sc1 * nw
        gx_ref[...] = r * (dxn - xn * jnp.mean(dxn * xn, axis=-1, keepdims=True)) + dx2_ref[...]

    tok = lambda w: pl.BlockSpec((ts, w), lambda i: (i, 0))
    full = lambda a: pl.BlockSpec(a.shape, lambda i: (0, 0))
    row = pl.BlockSpec((1, D_MODEL), lambda i: (0, 0))
    return pl.pallas_call(
        body, name="dh_dx", grid=(s // ts,), compiler_params=_params("arbitrary"),
        in_specs=[tok(w) for w in widths] + [full(w) for w in ws] + [tok(D_MODEL), full(mod), full(norm_w),
                                                                    tok(D_MODEL)],
        out_specs=(tok(D_MODEL), row, row, row),
        out_shape=(SDS((s, D_MODEL), F32),) + (SDS((1, D_MODEL), F32),) * 3,
    )(*dps, *ws, x, mod, norm_w, dx2)


def _grad_w_in_at(h, dq, dk, dv, dz_at, cos_t, sin_t, ts):
    s = h.shape[0]
    assert ts % PERM_BLK == 0

    def body(h_ref, q_ref, k_ref, v_ref, dz_ref, cos_ref, sin_ref, oq, ok, ov, gq_out, gk_out, gv_out, gz_out,
             gq, gk, gv, gz):
        @pl.when(pl.program_id(0) == 0)
        def _():
            for o in (gq, gk, gv, gz):
                o[...] = jnp.zeros_like(o)

        hb = h_ref[...]
        gz[...] += lax.dot_general(hb, dz_ref[...], _TN, preferred_element_type=F32)
        back = _plane_perm(PERM_BLK, True)
        rows = PERM_BLK // PLANES
        halves = [(slice(i * PERM_BLK, (i + 1) * PERM_BLK), slice(i * rows, (i + 1) * rows))
                  for i in range(ts // PERM_BLK)]

        def planes(ref, j, prow):
            return jnp.concatenate([ref[j, r, prow, :] for r in range(PLANES)], axis=0)

        for trow, prow in halves:
            vp = jnp.concatenate([_bf(planes(v_ref, j, prow)) for j in range(AT_PAIRS)], axis=1)
            ov[trow, :] = _bf(jnp.dot(back, vp, preferred_element_type=F32))
        gv[...] += lax.dot_general(hb, ov[...], _TN, preferred_element_type=F32)
        for g_ref, o_ref, acc in ((q_ref, oq, gq), (k_ref, ok, gk)):
            for trow, prow in halves:
                cs, sn = cos_ref[trow, :], sin_ref[trow, :]
                gs = [planes(g_ref, j, prow) for j in range(AT_PAIRS)]
                gp = jnp.concatenate([_bf(g * cs + _swap_half64(g * sn)) for g in gs], axis=1)
                o_ref[trow, :] = _bf(jnp.dot(back, gp, preferred_element_type=F32))
            acc[...] += lax.dot_general(hb, o_ref[...], _TN, preferred_element_type=F32)

        @pl.when(pl.program_id(0) == s // ts - 1)
        def _():
            for o, a in ((gq_out, gq), (gk_out, gk), (gv_out, gv), (gz_out, gz)):
                o[...] = _bf(a[...])

    tok = lambda w: pl.BlockSpec((ts, w), lambda i: (i, 0))
    pairs = pl.BlockSpec((AT_PAIRS, PLANES, ts // PLANES, LANES), lambda i: (0, 0, i, 0))
    acc = pl.BlockSpec((D_MODEL, AT_WIDTH), lambda i: (0, 0))
    return pl.pallas_call(
        body, name="grad_w_in_at", grid=(s // ts,), compiler_params=_params("arbitrary"),
        in_specs=[tok(D_MODEL), pairs, pairs, pairs, tok(AT_WIDTH), tok(LANES), tok(LANES)],
        out_specs=(tok(AT_WIDTH),) * 3 + (acc,) * 4,
        out_shape=(SDS((s, AT_WIDTH), BF16),) * 3 + (SDS((D_MODEL, AT_WIDTH), BF16),) * 4,
        scratch_shapes=[pltpu.VMEM((D_MODEL, AT_WIDTH), F32)] * 4,
    )(h, dq, dk, dv, dz_at, cos_t, sin_t)


def _adamw_math(w, g, m, v):
    m = ADAM_B1 * m + (1.0 - ADAM_B1) * g
    v = ADAM_B2 * v + (1.0 - ADAM_B2) * (g * g)
    m_hat = m / (1.0 - ADAM_B1 ** ADAM_STEP)
    v_hat = v / (1.0 - ADAM_B2 ** ADAM_STEP)
    delta = -ADAM_LR * (m_hat / (jnp.sqrt(v_hat) + ADAM_EPS) + ADAM_WD * w)
    return delta, m, v


def _adamw(w, m, v, g, name, own=None):
    def body(w_ref, m_ref, v_ref, g_ref, *rest):
        g_out, d_out, m_out, v_out = rest[-4:]
        if own is None:
            g = g_ref[...]
        else:
            g = g_ref[0].astype(F32)
            for k in range(1, N_DEV):
                g = g + g_ref[k].astype(F32)
            g = g + rest[0][...].astype(F32)
        g_out[...] = g
        d_out[...], m_out[...], v_out[...] = _adamw_math(w_ref[...], g, m_ref[...], v_ref[...])

    args = (w, m, v, g) if own is None else (w, m, v, g, own)
    return pl.pallas_call(body, name=name, compiler_params=_params(),
                          out_shape=(SDS(w.shape, F32),) * 4)(*args)


def _adamw_w_mod(w, m, v, siluc_all, dmod_mine):
    def body(w_ref, m_ref, v_ref, sc_ref, dm_ref, g_out, d_out, m_out, v_out):
        g = _htn(sc_ref[...], dm_ref[...])
        g_out[...] = g
        d_out[...], m_out[...], v_out[...] = _adamw_math(w_ref[...], g, m_ref[...], v_ref[...])

    return pl.pallas_call(body, name="adamw_w_mod", compiler_params=_params(),
                          out_shape=(SDS(w.shape, F32),) * 4)(w, m, v, siluc_all, dmod_mine)


def _pack_sum(pack_all):
    def body(p_ref, o_ref):
        t = p_ref[0]
        for k in range(1, N_DEV):
            t = t + p_ref[k]
        o_ref[...] = t

    return pl.pallas_call(body, name="pack_sum", out_shape=SDS(pack_all.shape[1:], F32))(pack_all)


def _tile(s, want):
    t = min(want, s)
    assert s % t == 0
    return t


def _local_step(x, c, positions, w_mod_bf, b_mod, norm_w, w_in_bf, conv_w, a_log, dt_bias, dn_norm_w, at_norm_w,
                w_out_bf, final_norm_w, tgt):
    s = x.shape[0]
    o = [0]
    for wdt in IN_SPLITS:
        o.append(o[-1] + wdt)
    w_ba = jnp.pad(w_in_bf[:, o[2]:o[4]], ((0, 0), (0, BA_PAD - 2 * DN_HEADS)))
    ws = [w_in_bf[:, o[0]:o[1]], w_in_bf[:, o[1]:o[2]], w_ba, w_in_bf[:, o[4]:o[5]], w_in_bf[:, o[5]:o[6]],
          w_in_bf[:, o[6]:o[7]], w_in_bf[:, o[7]:o[8]]]
    conv_w8 = jnp.pad(conv_w, ((0, 8 - CONV_K), (0, 0)))
    alog_row = jnp.pad(a_log, ((0, 0), (DN_HEADS, BA_PAD - 2 * DN_HEADS)))
    dtb_row = jnp.pad(dt_bias, ((0, 0), (DN_HEADS, BA_PAD - 2 * DN_HEADS)))
    atw2 = jnp.concatenate([at_norm_w, at_norm_w], axis=1)

    half = AT_DIM // 2
    lane = jnp.arange(LANES)
    inv_freq = ROPE_THETA ** (-(lane % half).astype(F32) / half)
    pos = positions.reshape(s // PERM_BLK, PERM_BLK // PLANES, PLANES).transpose(0, 2, 1).reshape(s)
    ang = pos.astype(F32)[:, None] * inv_freq
    cos_t = jnp.cos(ang)
    sin_t = jnp.sin(ang) * jnp.where((lane // half) % 2 == 0, -1.0, 1.0)

    mod, siluc = _adaln_mod(c, w_mod_bf, b_mod)
    gate = mod[:, 2 * D_MODEL:]
    hbf, qkv_pre, z_dn, ba, qr, kr, vb, z_at, q, k, v, bg = _ln_proj(
        x, mod, norm_w, ws, cos_t, sin_t, conv_w8, alog_row, dtb_row, _tile(s, 256))
    w, qd, kd, p, gl, tinv, o_dn, vn, st = _dn_forward(q, k, v, bg)
    o_at, lse = _attn_fwd(qr, kr, vb)
    (dx2, gw_out, dfw, dgate, loss, do_dn, dz_dn, do_at, dz_at, delta, ddnw, datw) = _out_loss(
        o_dn, z_dn, o_at, z_at, dn_norm_w, atw2, x, tgt, w_out_bf, gate, final_norm_w, _tile(s, 512))

    daq, dak, dav, g_aq, g_ak, g_av, g_az = _grad_w_in_at(hbf, *_attn_bwd(qr, kr, vb, do_at, lse, delta), dz_at,
                                                           cos_t, sin_t, _tile(s, 512))
    dq, dk, dv, dbg = _dn_backward(do_dn, st, vn, w, qd, kd, p, gl, q, k, v, bg, tinv)
    dqkv, dba, dcw, dal, ddtb, g_qkv, g_z, g_ba = _dn_prep_bwd(qkv_pre, ba, dq, dk, dv, dbg, conv_w8, alog_row, dtb_row,
                                                               hbf, dz_dn, _tile(s, 512))
    dps = [dqkv, dz_dn, dba, daq, dak, dav, dz_at]
    gw_in = jnp.concatenate([g_qkv, g_z, g_ba[:, :2 * DN_HEADS], g_aq, g_ak, g_av, g_az], axis=1)
    small = dict(conv=dcw[:CONV_K], dgate=dgate, siluc=siluc, dfw=dfw, alog=dal, dtb=ddtb, dnn=ddnw, atn=datw)

    def input_grad(token):
        gx, dshift, dscale, dnw = _dh_dx(dps, ws, x, mod + token, norm_w, dx2, _tile(s, 512))
        return gx, jnp.concatenate([dshift, dscale, small["dgate"]], axis=1), dnw

    return loss, gw_in, gw_out, small, input_grad


def kernel(x, c, positions, w_mod, b_mod, norm_w, w_in, conv_w, a_log, dt_bias, dn_norm_w, at_norm_w, w_out, final_norm_w, loss_target, m_w_mod, m_b_mod, m_norm_w, m_w_in, m_conv_w, m_a_log, m_dt_bias, m_dn_norm_w, m_at_norm_w, m_w_out, m_final_norm_w, v_w_mod, v_b_mod, v_norm_w, v_w_in, v_conv_w, v_a_log, v_dt_bias, v_dn_norm_w, v_at_norm_w, v_w_out, v_final_norm_w):
    me = 4 * lax.axis_index("x") + 2 * lax.axis_index("y") + lax.axis_index("c")
    s = x.shape[1]

    g_mod, g_in, g_conv, g_out = _all_gather(
        [_bf(w_mod[0]), _bf(w_in[0]), conv_w[0], _bf(w_out[0])], "gather_weights")
    w_mod_bf = g_mod.transpose(1, 0, 2).reshape(D_MODEL, 3 * D_MODEL)
    w_in_bf = g_in.transpose(1, 0, 2).reshape(D_MODEL, IN_COLS)
    conv_full = g_conv.transpose(1, 0, 2).reshape(CONV_K, 3 * DN_WIDTH)
    w_out_bf = g_out.reshape(D_MODEL, D_MODEL)

    loss, gw_in, gw_out, small, input_grad = _local_step(
        x[0], c, positions[0], w_mod_bf, b_mod, norm_w, w_in_bf, conv_full, a_log, dt_bias, dn_norm_w, at_norm_w,
        w_out_bf, final_norm_w.reshape(1, D_MODEL), loss_target[0])

    gw_in_slabs = gw_in.reshape(D_MODEL, N_DEV, IN_SHARD).transpose(1, 0, 2)
    gw_out_slabs = _bf(gw_out).reshape(N_DEV, D_MODEL // N_DEV, D_MODEL)
    send_sems, recv_sems, srcs, lands, token = _scatter_start([gw_in_slabs, gw_out_slabs])
    gx, dmod, dnw = input_grad(token[0, 0])
    r_in, r_out = _scatter_wait(send_sems, recv_sems, srcs, lands, gx)
    own_in = lax.dynamic_index_in_dim(gw_in_slabs, me, 0, keepdims=False)
    own_out = lax.dynamic_index_in_dim(gw_out_slabs, me, 0, keepdims=False)

    pack = jnp.concatenate([small["conv"].reshape(1, -1), dmod, small["siluc"], dnw, small["dfw"],
                            small["alog"], small["dtb"], small["dnn"], small["atn"],
                            jnp.pad(loss, ((0, 0), (0, LANES - 1)))], axis=1).reshape(PK_ROWS, LANES)
    (pack_all,) = _exchange([pack], [False], "exchange_small")

    res = {}
    res["w_in"] = _adamw(w_in[0], m_w_in[0], v_w_in[0], r_in, "adamw_w_in", own=own_in)
    res["w_out"] = _adamw(w_out[0], m_w_out[0], v_w_out[0], r_out, "adamw_w_out", own=own_out)
    flat_all = pack_all.reshape(N_DEV, PK_END)
    dmod_mine = lax.dynamic_slice(flat_all, (0, PK_DMOD + me * (3 * D_MODEL // N_DEV)), (N_DEV, 3 * D_MODEL // N_DEV))
    res["w_mod"] = _adamw_w_mod(w_mod[0], m_w_mod[0], v_w_mod[0], flat_all[:, PK_SILUC:PK_DNW], dmod_mine)
    tot = _pack_sum(pack_all).reshape(1, PK_END)
    g_conv_full = tot[:, PK_CONV:PK_DMOD].reshape(CONV_K, 3 * DN_WIDTH)
    g_conv_mine = lax.dynamic_slice(g_conv_full, (0, me * (3 * DN_WIDTH // N_DEV)), (CONV_K, 3 * DN_WIDTH // N_DEV))
    res["conv_w"] = _adamw(conv_w[0], m_conv_w[0], v_conv_w[0], g_conv_mine, "adamw_conv_w")
    res["b_mod"] = _adamw(b_mod, m_b_mod, v_b_mod, tot[:, PK_DMOD:PK_SILUC], "adamw_b_mod")
    res["norm_w"] = _adamw(norm_w, m_norm_w, v_norm_w, tot[:, PK_DNW:PK_DFW], "adamw_norm_w")
    res["a_log"] = _adamw(a_log, m_a_log, v_a_log, tot[:, PK_ALOG + DN_HEADS:PK_ALOG + 2 * DN_HEADS], "adamw_a_log")
    res["dt_bias"] = _adamw(dt_bias, m_dt_bias, v_dt_bias, tot[:, PK_DTB + DN_HEADS:PK_DTB + 2 * DN_HEADS],
                            "adamw_dt_bias")
    res["dn_norm_w"] = _adamw(dn_norm_w, m_dn_norm_w, v_dn_norm_w, tot[:, PK_DNN:PK_ATN], "adamw_dn_norm_w")
    g_atn = tot[:, PK_ATN:PK_ATN + AT_DIM] + tot[:, PK_ATN + AT_DIM:PK_LOSS]
    res["at_norm_w"] = _adamw(at_norm_w, m_at_norm_w, v_at_norm_w, g_atn, "adamw_at_norm_w")
    fin = _adamw(final_norm_w.reshape(1, D_MODEL), m_final_norm_w.reshape(1, D_MODEL),
                 v_final_norm_w.reshape(1, D_MODEL), tot[:, PK_DFW:PK_ALOG], "adamw_final_norm_w")
    res["final_norm_w"] = tuple(a.reshape(D_MODEL) for a in fin)

    lead = ("w_mod", "w_in", "conv_w", "w_out")
    names = ("w_mod", "b_mod", "norm_w", "w_in", "conv_w", "a_log", "dt_bias", "dn_norm_w", "at_norm_w", "w_out",
             "final_norm_w")
    out = [tot[0, PK_LOSS], gx.reshape(1, s, D_MODEL)]
    for kind in range(4):
        for nm in names:
            a = res[nm][kind]
            out.append(a[None] if nm in lead else a)
    return tuple(out)
```
